```python
import jax
import jax.numpy as jnp
from jax import lax
import numpy as np


D_MODEL = 1024
BATCH = 4
SEQ = 4096
DEPTH = 2

HEAD_DIM = 64
GROUP_HEADS = 4
GROUP_WIDTH = GROUP_HEADS * HEAD_DIM
N_GROUPS = 4
D_MIX = N_GROUPS * GROUP_WIDTH
ROPE_THETA = 500000.0
EPS = 1e-6
Q_BLOCK = 128
N_MEM = 256
XA_HEADS = 4
XA_WIDTH = XA_HEADS * HEAD_DIM
HG_CHUNK = 64
DSA_LATENT = 128
IDX_HEADS = 8
IDX_DIM = 32
DSA_TOPK_MAX = 256
CMP_BLOCK = 32
CMP_STRIDE = 16
CMP_HIDDEN = 256
SLC_BLOCK = 64
SLC_TOPN = 16
WINDOW = 512
ML_CHUNK = 64
CONV_WIDTH = 4
D_FF = -(-8 * D_MODEL // (3 * 256)) * 256

HG_SPLITS = (GROUP_WIDTH,) * 4
DSA_SPLITS = (GROUP_WIDTH, DSA_LATENT, IDX_HEADS * IDX_DIM, IDX_DIM, IDX_HEADS)
NSA_SPLITS = (GROUP_WIDTH,) + (HEAD_DIM,) * 6 + (3 * GROUP_HEADS,)
ML_SPLITS = (2 * GROUP_WIDTH, GROUP_WIDTH, GROUP_WIDTH, GROUP_HEADS, GROUP_HEADS)
GROUP_COLS = (sum(HG_SPLITS), sum(DSA_SPLITS), sum(NSA_SPLITS), sum(ML_SPLITS))
IN_COLS = sum(GROUP_COLS)

kernel_name = "hybrid_parallel_heads_decoder"

F32 = jnp.float32


def split_cols(t, sizes):
    return jnp.split(t, [int(s) for s in np.cumsum(sizes)[:-1]], axis=-1)


def split_heads(t, n):
    return t.reshape(t.shape[:-1] + (n, t.shape[-1] // n))


def rms_norm(t, g):
    t32 = t.astype(F32)
    y = t32 * lax.rsqrt(jnp.mean(t32 * t32, axis=-1, keepdims=True) + EPS)
    return (y * g.astype(F32)).astype(t.dtype)


def partial_rope(t, pos):
    d = t.shape[-1]
    rd = d // 4
    half = rd // 2
    inv = ROPE_THETA ** (-jnp.arange(half, dtype=F32) * 2.0 / rd)
    ang = pos.astype(F32)[:, None] * inv[None, :]
    cos = jnp.cos(ang)[:, None, :].astype(t.dtype)
    sin = jnp.sin(ang)[:, None, :].astype(t.dtype)
    x1, x2 = t[..., :half], t[..., half:rd]
    return jnp.concatenate([x1 * cos - x2 * sin, x2 * cos + x1 * sin, t[..., rd:]], axis=-1)


def masked_softmax(s, mask):
    s = jnp.where(mask, s.astype(F32), -jnp.inf)
    m = jnp.max(s, axis=-1, keepdims=True)
    m = jnp.where(jnp.isfinite(m), m, 0.0)
    e = jnp.exp(s - m)
    den = jnp.sum(e, axis=-1, keepdims=True)
    return e / jnp.where(den > 0, den, 1.0)


def gather_rows(table, idx):
    return jax.vmap(lambda tb, ix: tb[ix])(table, idx)


def causal_conv(t, w, b):
    y = lax.conv_general_dilated(t, w[:, None, :].astype(t.dtype), window_strides=(1,),
                                 padding=[(CONV_WIDTH - 1, 0)],
                                 dimension_numbers=('NWC', 'WIO', 'NWC'),
                                 feature_group_count=t.shape[-1])
    return y + b.astype(t.dtype)


def to_chunks(t, c):
    b_, l_, h_ = t.shape[:3]
    t = t.reshape((b_, l_ // c, c, h_) + t.shape[3:])
    return jnp.moveaxis(t, (1, 3), (0, 2))


def from_chunks(t):
    t = jnp.moveaxis(t, (0, 2), (1, 3))
    return t.reshape((t.shape[0], t.shape[1] * t.shape[2], t.shape[3]) + t.shape[4:])


def unblock(o):
    o = jnp.moveaxis(o, 0, 1)
    return o.reshape((o.shape[0], o.shape[1] * o.shape[2]) + o.shape[3:])


def hgrn2_mixer(cols, lb, o_gain):
    b_, l_ = cols.shape[:2]
    q, f, i, g = split_cols(cols, HG_SPLITS)
    q = split_heads(jax.nn.silu(q), GROUP_HEADS).astype(F32) * HEAD_DIM ** -0.5
    forget = lb + (1.0 - lb) * jax.nn.sigmoid(f.astype(F32))
    k = split_heads(1.0 - forget, GROUP_HEADS)
    logf = split_heads(jnp.log(forget), GROUP_HEADS)
    v = split_heads(i, GROUP_HEADS).astype(F32)
    tri = jnp.tril(jnp.ones((HG_CHUNK, HG_CHUNK), bool))[:, :, None]

    def step(state, inp):
        qc, kc, vc, gc = inp
        bcum = jnp.cumsum(gc, axis=2)
        o_inter = jnp.einsum('bhtk,bhkv->bhtv', qc * jnp.exp(bcum), state)
        diff = bcum[:, :, :, None, :] - bcum[:, :, None, :, :]
        decay = jnp.exp(jnp.where(tri, diff, -jnp.inf))
        attn = jnp.einsum('bhtk,bhsk,bhtsk->bhts', qc, kc, decay)
        out = o_inter + jnp.einsum('bhts,bhsv->bhtv', attn, vc)
        b_last = bcum[:, :, -1:, :]
        state = (jnp.exp(b_last[:, :, 0, :])[..., None] * state
                 + jnp.einsum('bhsk,bhsv->bhkv', kc * jnp.exp(b_last - bcum), vc))
        return state, out

    s0 = jnp.zeros((b_, GROUP_HEADS, HEAD_DIM, HEAD_DIM), F32)
    xs = (to_chunks(q, HG_CHUNK), to_chunks(k, HG_CHUNK), to_chunks(v, HG_CHUNK), to_chunks(logf, HG_CHUNK))
    _, o = lax.scan(step, s0, xs)
    o = rms_norm(from_chunks(o), o_gain).astype(cols.dtype)
    o = o * jax.nn.silu(split_heads(g, GROUP_HEADS))
    return o.reshape(b_, l_, GROUP_WIDTH)


def dsa_mixer(cols, pos, kv_gain, w_uk, w_uv, q_gain, k_gain, idxk_gain):
    b_, l_ = cols.shape[:2]
    q, ckv, iq, ik, iw = split_cols(cols, DSA_SPLITS)
    q = partial_rope(rms_norm(split_heads(q, GROUP_HEADS), q_gain), pos)
    ckv = rms_norm(ckv, kv_gain)
    k = partial_rope(rms_norm(ckv @ w_uk, k_gain)[:, :, None, :], pos)[:, :, 0]
    v = ckv @ w_uv
    iq = partial_rope(split_heads(iq, IDX_HEADS), pos)
    ik = partial_rope(rms_norm(ik, idxk_gain)[:, :, None, :], pos)[:, :, 0]
    iw = iw * (IDX_HEADS ** -0.5 * IDX_DIM ** -0.5)
    topk = min(DSA_TOPK_MAX, l_ // 4)
    key_pos = jnp.arange(l_)

    def block(bi):
        t0 = bi * Q_BLOCK
        qpos = t0 + jnp.arange(Q_BLOCK)
        qb = lax.dynamic_slice_in_dim(q, t0, Q_BLOCK, axis=1)
        iqb = lax.dynamic_slice_in_dim(iq, t0, Q_BLOCK, axis=1)
        iwb = lax.dynamic_slice_in_dim(iw, t0, Q_BLOCK, axis=1)
        rel = jax.nn.relu(jnp.einsum('bthd,bsd->bhts', iqb, ik))
        score = jnp.einsum('bhts,bth->bts', rel, iwb).astype(F32)
        causal = key_pos[None, :] <= qpos[:, None]
        score = jnp.where(causal[None], score, -jnp.inf)
        _, idx = lax.top_k(score, topk)
        k_sel = gather_rows(k, idx)
        v_sel = gather_rows(v, idx)
        s = jnp.einsum('bthd,btkd->bhtk', qb, k_sel) * HEAD_DIM ** -0.5
        p = masked_softmax(s, (idx <= qpos[None, :, None])[:, None]).astype(v.dtype)
        return jnp.einsum('bhtk,btkd->bthd', p, v_sel)

    o = unblock(lax.map(block, jnp.arange(l_ // Q_BLOCK)))
    return o.reshape(b_, l_, GROUP_WIDTH)


def nsa_mixer(cols, pos, pos_k, pos_v, k_w1, k_w2, v_w1, v_w2, q_gain, k_gains):
    b_, l_ = cols.shape[:2]
    scale = HEAD_DIM ** -0.5
    q, kc, vc, ks, vs, kw, vw, gates = split_cols(cols, NSA_SPLITS)
    q = rms_norm(split_heads(q, GROUP_HEADS), q_gain)
    q_rot = partial_rope(q, pos)
    gates = jax.nn.sigmoid(gates.reshape(b_, l_, 3, GROUP_HEADS, 1))

    n_cmp = (l_ - CMP_BLOCK) // CMP_STRIDE + 1
    cmp_idx = np.arange(n_cmp)[:, None] * CMP_STRIDE + np.arange(CMP_BLOCK)[None, :]

    def compress(t, pe, w1, w2):
        blk = (t[:, cmp_idx] + pe).reshape(b_, n_cmp, CMP_BLOCK * HEAD_DIM)
        return jax.nn.relu(blk @ w1) @ w2

    k_cmp = rms_norm(compress(kc, pos_k, k_w1, k_w2), k_gains[0])
    v_cmp = compress(vc, pos_v, v_w1, v_w2)
    cmp_vis = cmp_idx[:, -1][None, :] <= np.arange(l_)[:, None]
    p_cmp = masked_softmax(jnp.einsum('bthd,bjd->bhtj', q, k_cmp) * scale, cmp_vis)
    o_cmp = jnp.einsum('bhtj,bjd->bthd', p_cmp.astype(v_cmp.dtype), v_cmp)

    n_slc = l_ // SLC_BLOCK
    n_sel = min(SLC_TOPN, n_slc)
    st_c = np.arange(n_cmp) * CMP_STRIDE
    st_s = np.arange(n_slc) * SLC_BLOCK
    overlap = ((st_c[:, None] < st_s[None, :] + SLC_BLOCK)
               & (st_c[:, None] + CMP_BLOCK > st_s[None, :])).astype(np.float32)
    imp = jnp.einsum('bhtj,jn->btn', p_cmp, overlap)
    cur = np.arange(l_)[:, None] // SLC_BLOCK
    blk_id = np.arange(n_slc)[None, :]
    forced = (blk_id == 0) | (blk_id == cur) | (blk_id == cur - 1)
    imp = jnp.where(forced, jnp.inf, jnp.where(blk_id > cur, -jnp.inf, imp))
    _, sel = lax.top_k(imp, n_sel)

    k_s = partial_rope(rms_norm(ks, k_gains[1])[:, :, None], pos)[:, :, 0]
    k_blocks = k_s.reshape(b_, n_slc, SLC_BLOCK, HEAD_DIM)
    v_blocks = vs.reshape(b_, n_slc, SLC_BLOCK, HEAD_DIM)
    k_w = partial_rope(rms_norm(kw, k_gains[2])[:, :, None], pos)[:, :, 0]
    k_pad = jnp.pad(k_w, ((0, 0), (WINDOW, 0), (0, 0)))
    v_pad = jnp.pad(vw, ((0, 0), (WINDOW, 0), (0, 0)))

    def block(bi):
        t0 = bi * Q_BLOCK
        qpos = t0 + jnp.arange(Q_BLOCK)
        qb = lax.dynamic_slice_in_dim(q_rot, t0, Q_BLOCK, axis=1)
        sb = lax.dynamic_slice_in_dim(sel, t0, Q_BLOCK, axis=1)
        kb = gather_rows(k_blocks, sb)
        vb = gather_rows(v_blocks, sb)
        kpos = sb[..., None] * SLC_BLOCK + jnp.arange(SLC_BLOCK)
        s = jnp.einsum('bthd,btnsd->bhtns', qb, kb) * scale
        valid = (kpos <= qpos[None, :, None, None])[:, None]
        p = masked_softmax(s.reshape(b_, GROUP_HEADS, Q_BLOCK, n_sel * SLC_BLOCK),
                           valid.reshape(b_, 1, Q_BLOCK, n_sel * SLC_BLOCK))
        p = p.reshape(b_, GROUP_HEADS, Q_BLOCK, n_sel, SLC_BLOCK).astype(vb.dtype)
        o_slc = jnp.einsum('bhtns,btnsd->bthd', p, vb)
        kwb = lax.dynamic_slice_in_dim(k_pad, t0, WINDOW + Q_BLOCK, axis=1)
        vwb = lax.dynamic_slice_in_dim(v_pad, t0, WINDOW + Q_BLOCK, axis=1)
        wpos = t0 - WINDOW + jnp.arange(WINDOW + Q_BLOCK)
        dist = qpos[:, None] - wpos[None, :]
        wvalid = (dist >= 0) & (dist < WINDOW) & (wpos[None, :] >= 0)
        sw = jnp.einsum('bthd,bsd->bhts', qb, kwb) * scale
        pw = masked_softmax(sw, wvalid[None, None]).astype(vwb.dtype)
        o_swa = jnp.einsum('bhts,bsd->bthd', pw, vwb)
        return o_slc, o_swa

    o_slc, o_swa = lax.map(block, jnp.arange(l_ // Q_BLOCK))
    o = gates[:, :, 0] * o_cmp + gates[:, :, 1] * unblock(o_slc) + gates[:, :, 2] * unblock(o_swa)
    return o.reshape(b_, l_, GROUP_WIDTH)


def mlstm_mixer(cols, conv_w, conv_b, i_bias, f_bias, o_gain):
    b_, l_ = cols.shape[:2]
    qk, v, og, ig, fg = split_cols(cols, ML_SPLITS)
    qk = jax.nn.silu(causal_conv(qk, conv_w, conv_b))
    q, k = jnp.split(qk, 2, axis=-1)
    q = split_heads(q, GROUP_HEADS).astype(F32)
    k = split_heads(k, GROUP_HEADS).astype(F32) * HEAD_DIM ** -0.5
    v = split_heads(v, GROUP_HEADS).astype(F32)
    log_i = (ig + i_bias).astype(F32)
    log_f = jax.nn.log_sigmoid((fg + f_bias).astype(F32))
    tri = jnp.tril(jnp.ones((ML_CHUNK, ML_CHUNK), bool))

    def step(carry, inp):
        cmat, nvec, m = carry
        qc, kc, vc, li, lf = inp
        bcum = jnp.cumsum(lf, axis=-1)
        log_d = jnp.where(tri, bcum[..., :, None] - bcum[..., None, :] + li[..., None, :], -jnp.inf)
        inter = bcum + m[..., None]
        m_t = jnp.maximum(inter, jnp.max(log_d, axis=-1))
        d_mat = jnp.exp(log_d - m_t[..., None])
        w_inter = jnp.exp(inter - m_t)
        s = jnp.einsum('bhtd,bhsd->bhts', qc, kc) * d_mat
        num = (w_inter[..., None] * jnp.einsum('bhtd,bhdv->bhtv', qc, cmat)
               + jnp.einsum('bhts,bhsv->bhtv', s, vc))
        den = w_inter * jnp.einsum('bhtd,bhd->bht', qc, nvec) + jnp.sum(s, axis=-1)
        h = num / jnp.maximum(jnp.abs(den), jnp.exp(-m_t))[..., None]
        b_last = bcum[..., -1]
        log_w = b_last[..., None] - bcum + li
        m_new = jnp.maximum(b_last + m, jnp.max(log_w, axis=-1))
        w_s = jnp.exp(log_w - m_new[..., None])
        decay = jnp.exp(b_last + m - m_new)
        cmat = decay[..., None, None] * cmat + jnp.einsum('bhs,bhsd,bhsv->bhdv', w_s, kc, vc)
        nvec = decay[..., None] * nvec + jnp.einsum('bhs,bhsd->bhd', w_s, kc)
        return (cmat, nvec, m_new), h

    init = (jnp.zeros((b_, GROUP_HEADS, HEAD_DIM, HEAD_DIM), F32),
            jnp.zeros((b_, GROUP_HEADS, HEAD_DIM), F32),
            jnp.full((b_, GROUP_HEADS), -1e30, F32))
    xs = (to_chunks(q, ML_CHUNK), to_chunks(k, ML_CHUNK), to_chunks(v, ML_CHUNK),
          to_chunks(log_i, ML_CHUNK), to_chunks(log_f, ML_CHUNK))
    _, h = lax.scan(step, init, xs)
    h = rms_norm(from_chunks(h), o_gain).astype(cols.dtype)
    h = h * jax.nn.sigmoid(split_heads(og, GROUP_HEADS))
    return h.reshape(b_, l_, GROUP_WIDTH)


def cross_attention(h, m, wq, wkv, wo, q_gain, k_gain):
    b_, l_ = h.shape[:2]
    q = rms_norm(split_heads(h @ wq, XA_HEADS), q_gain)
    k, v = jnp.split(m @ wkv, 2, axis=-1)
    k = rms_norm(split_heads(k, XA_HEADS), k_gain)
    v = split_heads(v, XA_HEADS)
    s = jnp.einsum('bthd,bmhd->bhtm', q, k) * HEAD_DIM ** -0.5
    p = jax.nn.softmax(s.astype(F32), axis=-1).astype(v.dtype)
    o = jnp.einsum('bhtm,bmhd->bthd', p, v).reshape(b_, l_, XA_WIDTH)
    return o @ wo


def swiglu(h, w13, w2):
    a, b = jnp.split(h @ w13, 2, axis=-1)
    return (jax.nn.silu(a) * b) @ w2


def setup_inputs(seed: int = 0) -> dict:
    key = jax.random.key(seed)
    keys = iter(jax.random.split(key, 48))

    def nrm(shape, scale):
        return jax.random.normal(next(keys), shape, F32) * scale

    def gain(shape):
        return 1.0 + nrm(shape, 0.02)

    L = DEPTH
    return {
        'x': nrm((BATCH, SEQ, D_MODEL), 1.0),
        'mem': nrm((BATCH, N_MEM, D_MODEL), 1.0),
        'lb_param': nrm((L, GROUP_WIDTH), 0.5),
        'norm_mix': gain((L, D_MODEL)),
        'w_in': nrm((L, D_MODEL, IN_COLS), D_MODEL ** -0.5),
        'w_out': nrm((L, D_MIX, D_MODEL), D_MIX ** -0.5),
        'hg_o_gain': gain((L, HEAD_DIM)),
        'dsa_kv_gain': gain((L, DSA_LATENT)),
        'dsa_w_uk': nrm((L, DSA_LATENT, HEAD_DIM), DSA_LATENT ** -0.5),
        'dsa_w_uv': nrm((L, DSA_LATENT, HEAD_DIM), DSA_LATENT ** -0.5),
        'dsa_q_gain': gain((L, HEAD_DIM)),
        'dsa_k_gain': gain((L, HEAD_DIM)),
        'dsa_idxk_gain': gain((L, IDX_DIM)),
        'nsa_pos_k': nrm((L, CMP_BLOCK, HEAD_DIM), 0.1),
        'nsa_pos_v': nrm((L, CMP_BLOCK, HEAD_DIM), 0.1),
        'nsa_k_w1': nrm((L, CMP_BLOCK * HEAD_DIM, CMP_HIDDEN), (CMP_BLOCK * HEAD_DIM) ** -0.5),
        'nsa_k_w2': nrm((L, CMP_HIDDEN, HEAD_DIM), CMP_HIDDEN ** -0.5),
        'nsa_v_w1': nrm((L, CMP_BLOCK * HEAD_DIM, CMP_HIDDEN), (CMP_BLOCK * HEAD_DIM) ** -0.5),
        'nsa_v_w2': nrm((L, CMP_HIDDEN, HEAD_DIM), CMP_HIDDEN ** -0.5),
        'nsa_q_gain': gain((L, HEAD_DIM)),
        'nsa_k_gains': gain((L, 3, HEAD_DIM)),
        'ml_conv_w': nrm((L, CONV_WIDTH, 2 * GROUP_WIDTH), CONV_WIDTH ** -0.5),
        'ml_conv_b': nrm((L, 2 * GROUP_WIDTH), 0.02),
        'ml_i_bias': nrm((L, GROUP_HEADS), 0.1),
        'ml_f_bias': jnp.linspace(3.0, 6.0, GROUP_HEADS, dtype=F32)[None, :] + nrm((L, GROUP_HEADS), 0.1),
        'ml_o_gain': gain((L, HEAD_DIM)),
        'norm_xa': gain((L, D_MODEL)),
        'norm_mem': gain((L, D_MODEL)),
        'xa_wq': nrm((L, D_MODEL, XA_WIDTH), D_MODEL ** -0.5),
        'xa_wkv': nrm((L, D_MODEL, 2 * XA_WIDTH), D_MODEL ** -0.5),
        'xa_wo': nrm((L, XA_WIDTH, D_MODEL), XA_WIDTH ** -0.5),
        'xa_q_gain': gain((L, HEAD_DIM)),
        'xa_k_gain': gain((L, HEAD_DIM)),
        'norm_ffn': gain((L, D_MODEL)),
        'ffn_w13': nrm((L, D_MODEL, 2 * D_FF), D_MODEL ** -0.5),
        'ffn_w2': nrm((L, D_FF, D_MODEL), D_FF ** -0.5),
    }


def reference(x, mem, lb_param, norm_mix, w_in, w_out, hg_o_gain,
              dsa_kv_gain, dsa_w_uk, dsa_w_uv, dsa_q_gain, dsa_k_gain, dsa_idxk_gain,
              nsa_pos_k, nsa_pos_v, nsa_k_w1, nsa_k_w2, nsa_v_w1, nsa_v_w2, nsa_q_gain, nsa_k_gains,
              ml_conv_w, ml_conv_b, ml_i_bias, ml_f_bias, ml_o_gain,
              norm_xa, norm_mem, xa_wq, xa_wkv, xa_wo, xa_q_gain, xa_k_gain,
              norm_ffn, ffn_w13, ffn_w2):
    pos = jnp.arange(x.shape[1])
    lb_all = jnp.cumsum(jax.nn.softmax(lb_param.astype(F32), axis=0), axis=0)
    lb_all = lb_all - lb_all[:1]
    for l in range(DEPTH):
        h = rms_norm(x, norm_mix[l])
        c_hg, c_dsa, c_nsa, c_ml = split_cols(h @ w_in[l], GROUP_COLS)
        mixed = jnp.concatenate([
            hgrn2_mixer(c_hg, lb_all[l], hg_o_gain[l]),
            dsa_mixer(c_dsa, pos, dsa_kv_gain[l], dsa_w_uk[l], dsa_w_uv[l],
                      dsa_q_gain[l], dsa_k_gain[l], dsa_idxk_gain[l]),
            nsa_mixer(c_nsa, pos, nsa_pos_k[l], nsa_pos_v[l], nsa_k_w1[l], nsa_k_w2[l],
                      nsa_v_w1[l], nsa_v_w2[l], nsa_q_gain[l], nsa_k_gains[l]),
            mlstm_mixer(c_ml, ml_conv_w[l], ml_conv_b[l], ml_i_bias[l], ml_f_bias[l], ml_o_gain[l]),
        ], axis=-1)
        x = x + mixed @ w_out[l]
        x = x + cross_attention(rms_norm(x, norm_xa[l]), rms_norm(mem, norm_mem[l]),
                                xa_wq[l], xa_wkv[l], xa_wo[l], xa_q_gain[l], xa_k_gain[l])
        x = x + swiglu(rms_norm(x, norm_ffn[l]), ffn_w13[l], ffn_w2[l])
    return x
```

```python
import functools

import jax
import jax.numpy as jnp
from jax import lax
import numpy as np
from jax.experimental import pallas as pl
from jax.experimental.pallas import tpu as pltpu

F32 = jnp.float32
BF16 = jnp.bfloat16

D_MODEL = 1024
DEPTH = 2
HEAD_DIM = 64
GROUP_HEADS = 4
GROUP_WIDTH = GROUP_HEADS * HEAD_DIM
ROPE_THETA = 500000.0
EPS = 1e-6
Q_BLOCK = 128
N_MEM = 256
XA_HEADS = 4
XA_WIDTH = XA_HEADS * HEAD_DIM
HG_CHUNK = 64
DSA_LATENT = 128
IDX_HEADS = 8
IDX_DIM = 32
DSA_TOPK_MAX = 256
CMP_BLOCK = 32
CMP_STRIDE = 16
SLC_BLOCK = 64
SLC_TOPN = 16
WINDOW = 512
ML_CHUNK = 64
CONV_WIDTH = 4
D_FF = 2816

HG_SPLITS = (GROUP_WIDTH,) * 4
DSA_SPLITS = (GROUP_WIDTH, DSA_LATENT, IDX_HEADS * IDX_DIM, IDX_DIM, IDX_HEADS)
NSA_SPLITS = (GROUP_WIDTH,) + (HEAD_DIM,) * 6 + (3 * GROUP_HEADS,)
ML_SPLITS = (2 * GROUP_WIDTH, GROUP_WIDTH, GROUP_WIDTH, GROUP_HEADS, GROUP_HEADS)
GROUP_COLS = (sum(HG_SPLITS), sum(DSA_SPLITS), sum(NSA_SPLITS), sum(ML_SPLITS))
IN_COLS = sum(GROUP_COLS)

V7X_VMEM_LIMIT_BYTES = 56 * 1024 * 1024
LANES = 128
FF_CHUNK = 256


def _round_up(n, m):
    return -(-n // m) * m


def _rms_rows(t, g):
    return t * lax.rsqrt(jnp.mean(t * t, axis=-1, keepdims=True) + EPS) * g


def _const_spec(shape):
    return pl.BlockSpec(shape, lambda *_: (0,) * len(shape), pipeline_mode=pl.Buffered(1))


def _norm_matmul_kernel(x_ref, g_ref, w_ref, o_ref):
    h = _rms_rows(x_ref[...], g_ref[...]).astype(BF16)
    o_ref[...] = jnp.dot(h, w_ref[...], preferred_element_type=F32)


def norm_matmul(x2d, gain, w_bf16, tm):
    m, k = x2d.shape
    n = w_bf16.shape[1]
    return pl.pallas_call(
        _norm_matmul_kernel,
        out_shape=jax.ShapeDtypeStruct((m, n), F32),
        grid=(m // tm,),
        in_specs=[
            pl.BlockSpec((tm, k), lambda i: (i, 0)),
            _const_spec((1, k)),
            _const_spec((k, n)),
        ],
        out_specs=pl.BlockSpec((tm, n), lambda i: (i, 0)),
        compiler_params=pltpu.CompilerParams(
            dimension_semantics=("arbitrary",), vmem_limit_bytes=V7X_VMEM_LIMIT_BYTES),
        name="norm_matmul",
    )(x2d, gain.reshape(1, k), w_bf16)


def _mem_kv_kernel(m_ref, g_ref, w_ref, kg_ref, k_ref, v_ref):
    mn = _rms_rows(m_ref[...], g_ref[...]).astype(BF16)
    kv = jnp.dot(mn, w_ref[...], preferred_element_type=F32)
    kg = kg_ref[...]
    ks = []
    for h in range(XA_HEADS):
        kh = kv[:, h * HEAD_DIM:(h + 1) * HEAD_DIM]
        ks.append(_rms_rows(kh, kg))
    k_ref[...] = jnp.concatenate(ks, axis=-1).astype(BF16)
    v_ref[...] = kv[:, XA_WIDTH:].astype(BF16)


def mem_kv(mem2d, gain, wkv_bf16, k_gain):
    m, k = mem2d.shape
    return pl.pallas_call(
        _mem_kv_kernel,
        out_shape=(jax.ShapeDtypeStruct((m, XA_WIDTH), BF16),
                   jax.ShapeDtypeStruct((m, XA_WIDTH), BF16)),
        grid=(m // N_MEM,),
        in_specs=[
            pl.BlockSpec((N_MEM, k), lambda i: (i, 0)),
            _const_spec((1, k)),
            _const_spec((k, 2 * XA_WIDTH)),
            _const_spec((1, HEAD_DIM)),
        ],
        out_specs=(pl.BlockSpec((N_MEM, XA_WIDTH), lambda i: (i, 0)),
                   pl.BlockSpec((N_MEM, XA_WIDTH), lambda i: (i, 0))),
        compiler_params=pltpu.CompilerParams(
            dimension_semantics=("arbitrary",), vmem_limit_bytes=V7X_VMEM_LIMIT_BYTES),
        name="mem_kv",
    )(mem2d, gain.reshape(1, k), wkv_bf16, k_gain.reshape(1, HEAD_DIM))


def _post_kernel(x_ref, mix_ref, wout_ref, gxa_ref, wq_ref, qg_ref, k_ref, v_ref, wo_ref,
                 gffn_ref, w13_ref, w2_ref, o_ref):
    x = x_ref[...] + jnp.dot(mix_ref[...].astype(BF16), wout_ref[...],
                             preferred_element_type=F32)
    h = _rms_rows(x, gxa_ref[...]).astype(BF16)
    q = jnp.dot(h, wq_ref[...], preferred_element_type=F32)
    qg = qg_ref[...] * (HEAD_DIM ** -0.5)
    k = k_ref[...]
    v = v_ref[...]
    outs = []
    for hd in range(XA_HEADS):
        sl = slice(hd * HEAD_DIM, (hd + 1) * HEAD_DIM)
        qh = _rms_rows(q[:, sl], qg).astype(BF16)
        s = lax.dot_general(qh, k[:, sl], (((1,), (1,)), ((), ())),
                            preferred_element_type=F32)
        e = jnp.exp(s - jnp.max(s, axis=-1, keepdims=True))
        p = e / jnp.sum(e, axis=-1, keepdims=True)
        outs.append(jnp.dot(p.astype(BF16), v[:, sl], preferred_element_type=F32))
    o = jnp.concatenate(outs, axis=-1).astype(BF16)
    x = x + jnp.dot(o, wo_ref[...], preferred_element_type=F32)
    h = _rms_rows(x, gffn_ref[...]).astype(BF16)
    acc = x
    for c in range(D_FF // FF_CHUNK):
        a = jnp.dot(h, w13_ref[:, c * FF_CHUNK:(c + 1) * FF_CHUNK], preferred_element_type=F32)
        b = jnp.dot(h, w13_ref[:, D_FF + c * FF_CHUNK:D_FF + (c + 1) * FF_CHUNK],
                    preferred_element_type=F32)
        act = (a * jax.nn.sigmoid(a) * b).astype(BF16)
        acc = acc + jnp.dot(act, w2_ref[c * FF_CHUNK:(c + 1) * FF_CHUNK, :],
                            preferred_element_type=F32)
    o_ref[...] = acc


def post_block(x2d, mixed2d, seq, wout, gxa, wq, qg, k_mem, v_mem, wo, gffn, w13, w2, tm):
    m, d = x2d.shape
    tiles_per_batch = seq // tm
    row = lambda i: (i, 0)
    mem_row = lambda i: (i // tiles_per_batch, 0)
    return pl.pallas_call(
        _post_kernel,
        out_shape=jax.ShapeDtypeStruct((m, d), F32),
        grid=(m // tm,),
        in_specs=[
            pl.BlockSpec((tm, d), row),
            pl.BlockSpec((tm, mixed2d.shape[1]), row),
            _const_spec(wout.shape),
            _const_spec((1, d)),
            _const_spec(wq.shape),
            _const_spec((1, HEAD_DIM)),
            pl.BlockSpec((N_MEM, XA_WIDTH), mem_row),
            pl.BlockSpec((N_MEM, XA_WIDTH), mem_row),
            _const_spec(wo.shape),
            _const_spec((1, d)),
            _const_spec(w13.shape),
            _const_spec(w2.shape),
        ],
        out_specs=pl.BlockSpec((tm, d), row),
        compiler_params=pltpu.CompilerParams(
            dimension_semantics=("arbitrary",), vmem_limit_bytes=V7X_VMEM_LIMIT_BYTES),
        name="post_block",
    )(x2d, mixed2d, wout, gxa.reshape(1, d), wq, qg.reshape(1, HEAD_DIM), k_mem, v_mem, wo,
      gffn.reshape(1, d), w13, w2)


def split_cols(t, sizes):
    return jnp.split(t, [int(s) for s in np.cumsum(sizes)[:-1]], axis=-1)


def split_heads(t, n):
    return t.reshape(t.shape[:-1] + (n, t.shape[-1] // n))


def rms_norm(t, g):
    t32 = t.astype(F32)
    y = t32 * lax.rsqrt(jnp.mean(t32 * t32, axis=-1, keepdims=True) + EPS)
    return (y * g.astype(F32)).astype(t.dtype)


def partial_rope(t, pos):
    d = t.shape[-1]
    rd = d // 4
    half = rd // 2
    inv = ROPE_THETA ** (-jnp.arange(half, dtype=F32) * 2.0 / rd)
    ang = pos.astype(F32)[:, None] * inv[None, :]
    cos = jnp.cos(ang)[:, None, :].astype(t.dtype)
    sin = jnp.sin(ang)[:, None, :].astype(t.dtype)
    x1, x2 = t[..., :half], t[..., half:rd]
    return jnp.concatenate([x1 * cos - x2 * sin, x2 * cos + x1 * sin, t[..., rd:]], axis=-1)


def masked_softmax(s, mask):
    s = jnp.where(mask, s.astype(F32), -jnp.inf)
    m = jnp.max(s, axis=-1, keepdims=True)
    m = jnp.where(jnp.isfinite(m), m, 0.0)
    e = jnp.exp(s - m)
    den = jnp.sum(e, axis=-1, keepdims=True)
    return e / jnp.where(den > 0, den, 1.0)


def gather_rows(table, idx):
    return jax.vmap(lambda tb, ix: tb[ix])(table, idx)


def causal_conv(t, w, b):
    y = lax.conv_general_dilated(t, w[:, None, :].astype(t.dtype), window_strides=(1,),
                                 padding=[(CONV_WIDTH - 1, 0)],
                                 dimension_numbers=('NWC', 'WIO', 'NWC'),
                                 feature_group_count=t.shape[-1])
    return y + b.astype(t.dtype)


def to_chunks(t, c):
    b_, l_, h_ = t.shape[:3]
    t = t.reshape((b_, l_ // c, c, h_) + t.shape[3:])
    return jnp.moveaxis(t, (1, 3), (0, 2))


def from_chunks(t):
    t = jnp.moveaxis(t, (0, 2), (1, 3))
    return t.reshape((t.shape[0], t.shape[1] * t.shape[2], t.shape[3]) + t.shape[4:])


def unblock(o):
    o = jnp.moveaxis(o, 0, 1)
    return o.reshape((o.shape[0], o.shape[1] * o.shape[2]) + o.shape[3:])


def hgrn2_mixer(cols, lb, o_gain):
    b_, l_ = cols.shape[:2]
    q, f, i, g = split_cols(cols, HG_SPLITS)
    q = split_heads(jax.nn.silu(q), GROUP_HEADS).astype(F32) * HEAD_DIM ** -0.5
    forget = lb + (1.0 - lb) * jax.nn.sigmoid(f.astype(F32))
    k = split_heads(1.0 - forget, GROUP_HEADS)
    logf = split_heads(jnp.log(forget), GROUP_HEADS)
    v = split_heads(i, GROUP_HEADS).astype(F32)
    tri = jnp.tril(jnp.ones((HG_CHUNK, HG_CHUNK), bool))[:, :, None]

    def step(state, inp):
        qc, kc, vc, gc = inp
        bcum = jnp.cumsum(gc, axis=2)
        o_inter = jnp.einsum('bhtk,bhkv->bhtv', qc * jnp.exp(bcum), state)
        diff = bcum[:, :, :, None, :] - bcum[:, :, None, :, :]
        decay = jnp.exp(jnp.where(tri, diff, -jnp.inf))
        attn = jnp.einsum('bhtk,bhsk,bhtsk->bhts', qc, kc, decay)
        out = o_inter + jnp.einsum('bhts,bhsv->bhtv', attn, vc)
        b_last = bcum[:, :, -1:, :]
        state = (jnp.exp(b_last[:, :, 0, :])[..., None] * state
                 + jnp.einsum('bhsk,bhsv->bhkv', kc * jnp.exp(b_last - bcum), vc))
        return state, out

    s0 = jnp.zeros((b_, GROUP_HEADS, HEAD_DIM, HEAD_DIM), F32)
    xs = (to_chunks(q, HG_CHUNK), to_chunks(k, HG_CHUNK), to_chunks(v, HG_CHUNK),
          to_chunks(logf, HG_CHUNK))
    _, o = lax.scan(step, s0, xs)
    o = rms_norm(from_chunks(o), o_gain).astype(cols.dtype)
    o = o * jax.nn.silu(split_heads(g, GROUP_HEADS))
    return o.reshape(b_, l_, GROUP_WIDTH)


def dsa_mixer(cols, pos, kv_gain, w_uk, w_uv, q_gain, k_gain, idxk_gain):
    b_, l_ = cols.shape[:2]
    q, ckv, iq, ik, iw = split_cols(cols, DSA_SPLITS)
    q = partial_rope(rms_norm(split_heads(q, GROUP_HEADS), q_gain), pos)
    ckv = rms_norm(ckv, kv_gain)
    k = partial_rope(rms_norm(ckv @ w_uk, k_gain)[:, :, None, :], pos)[:, :, 0]
    v = ckv @ w_uv
    iq = partial_rope(split_heads(iq, IDX_HEADS), pos)
    ik = partial_rope(rms_norm(ik, idxk_gain)[:, :, None, :], pos)[:, :, 0]
    iw = iw * (IDX_HEADS ** -0.5 * IDX_DIM ** -0.5)
    topk = min(DSA_TOPK_MAX, l_ // 4)
    key_pos = jnp.arange(l_)

    def block(bi):
        t0 = bi * Q_BLOCK
        qpos = t0 + jnp.arange(Q_BLOCK)
        qb = lax.dynamic_slice_in_dim(q, t0, Q_BLOCK, axis=1)
        iqb = lax.dynamic_slice_in_dim(iq, t0, Q_BLOCK, axis=1)
        iwb = lax.dynamic_slice_in_dim(iw, t0, Q_BLOCK, axis=1)
        rel = jax.nn.relu(jnp.einsum('bthd,bsd->bhts', iqb, ik))
        score = jnp.einsum('bhts,bth->bts', rel, iwb).astype(F32)
        causal = key_pos[None, :] <= qpos[:, None]
        score = jnp.where(causal[None], score, -jnp.inf)
        _, idx = lax.top_k(score, topk)
        k_sel = gather_rows(k, idx)
        v_sel = gather_rows(v, idx)
        s = jnp.einsum('bthd,btkd->bhtk', qb, k_sel) * HEAD_DIM ** -0.5
        p = masked_softmax(s, (idx <= qpos[None, :, None])[:, None]).astype(v.dtype)
        return jnp.einsum('bhtk,btkd->bthd', p, v_sel)

    o = unblock(lax.map(block, jnp.arange(l_ // Q_BLOCK)))
    return o.reshape(b_, l_, GROUP_WIDTH)


def nsa_mixer(cols, pos, pos_k, pos_v, k_w1, k_w2, v_w1, v_w2, q_gain, k_gains):
    b_, l_ = cols.shape[:2]
    scale = HEAD_DIM ** -0.5
    q, kc, vc, ks, vs, kw, vw, gates = split_cols(cols, NSA_SPLITS)
    q = rms_norm(split_heads(q, GROUP_HEADS), q_gain)
    q_rot = partial_rope(q, pos)
    gates = jax.nn.sigmoid(gates.reshape(b_, l_, 3, GROUP_HEADS, 1))

    n_cmp = (l_ - CMP_BLOCK) // CMP_STRIDE + 1
    cmp_idx = np.arange(n_cmp)[:, None] * CMP_STRIDE + np.arange(CMP_BLOCK)[None, :]

    def compress(t, pe, w1, w2):
        blk = (t[:, cmp_idx] + pe).reshape(b_, n_cmp, CMP_BLOCK * HEAD_DIM)
        return jax.nn.relu(blk @ w1) @ w2

    k_cmp = rms_norm(compress(kc, pos_k, k_w1, k_w2), k_gains[0])
    v_cmp = compress(vc, pos_v, v_w1, v_w2)
    cmp_vis = cmp_idx[:, -1][None, :] <= np.arange(l_)[:, None]
    p_cmp = masked_softmax(jnp.einsum('bthd,bjd->bhtj', q, k_cmp) * scale, cmp_vis)
    o_cmp = jnp.einsum('bhtj,bjd->bthd', p_cmp.astype(v_cmp.dtype), v_cmp)

    n_slc = l_ // SLC_BLOCK
    n_sel = min(SLC_TOPN, n_slc)
    st_c = np.arange(n_cmp) * CMP_STRIDE
    st_s = np.arange(n_slc) * SLC_BLOCK
    overlap = ((st_c[:, None] < st_s[None, :] + SLC_BLOCK)
               & (st_c[:, None] + CMP_BLOCK > st_s[None, :])).astype(np.float32)
    imp = jnp.einsum('bhtj,jn->btn', p_cmp, overlap)
    cur = np.arange(l_)[:, None] // SLC_BLOCK
    blk_id = np.arange(n_slc)[None, :]
    forced = (blk_id == 0) | (blk_id == cur) | (blk_id == cur - 1)
    imp = jnp.where(forced, jnp.inf, jnp.where(blk_id > cur, -jnp.inf, imp))
    _, sel = lax.top_k(imp, n_sel)

    k_s = partial_rope(rms_norm(ks, k_gains[1])[:, :, None], pos)[:, :, 0]
    k_blocks = k_s.reshape(b_, n_slc, SLC_BLOCK, HEAD_DIM)
    v_blocks = vs.reshape(b_, n_slc, SLC_BLOCK, HEAD_DIM)
    k_w = partial_rope(rms_norm(kw, k_gains[2])[:, :, None], pos)[:, :, 0]
    k_pad = jnp.pad(k_w, ((0, 0), (WINDOW, 0), (0, 0)))
    v_pad = jnp.pad(vw, ((0, 0), (WINDOW, 0), (0, 0)))

    def block(bi):
        t0 = bi * Q_BLOCK
        qpos = t0 + jnp.arange(Q_BLOCK)
        qb = lax.dynamic_slice_in_dim(q_rot, t0, Q_BLOCK, axis=1)
        sb = lax.dynamic_slice_in_dim(sel, t0, Q_BLOCK, axis=1)
        kb = gather_rows(k_blocks, sb)
        vb = gather_rows(v_blocks, sb)
        kpos = sb[..., None] * SLC_BLOCK + jnp.arange(SLC_BLOCK)
        s = jnp.einsum('bthd,btnsd->bhtns', qb, kb) * scale
        valid = (kpos <= qpos[None, :, None, None])[:, None]
        p = masked_softmax(s.reshape(b_, GROUP_HEADS, Q_BLOCK, n_sel * SLC_BLOCK),
                           valid.reshape(b_, 1, Q_BLOCK, n_sel * SLC_BLOCK))
        p = p.reshape(b_, GROUP_HEADS, Q_BLOCK, n_sel, SLC_BLOCK).astype(vb.dtype)
        o_slc = jnp.einsum('bhtns,btnsd->bthd', p, vb)
        kwb = lax.dynamic_slice_in_dim(k_pad, t0, WINDOW + Q_BLOCK, axis=1)
        vwb = lax.dynamic_slice_in_dim(v_pad, t0, WINDOW + Q_BLOCK, axis=1)
        wpos = t0 - WINDOW + jnp.arange(WINDOW + Q_BLOCK)
        dist = qpos[:, None] - wpos[None, :]
        wvalid = (dist >= 0) & (dist < WINDOW) & (wpos[None, :] >= 0)
        sw = jnp.einsum('bthd,bsd->bhts', qb, kwb) * scale
        pw = masked_softmax(sw, wvalid[None, None]).astype(vwb.dtype)
        o_swa = jnp.einsum('bhts,bsd->bthd', pw, vwb)
        return o_slc, o_swa

    o_slc, o_swa = lax.map(block, jnp.arange(l_ // Q_BLOCK))
    o = gates[:, :, 0] * o_cmp + gates[:, :, 1] * unblock(o_slc) + gates[:, :, 2] * unblock(o_swa)
    return o.reshape(b_, l_, GROUP_WIDTH)


def mlstm_mixer(cols, conv_w, conv_b, i_bias, f_bias, o_gain):
    b_, l_ = cols.shape[:2]
    qk, v, og, ig, fg = split_cols(cols, ML_SPLITS)
    qk = jax.nn.silu(causal_conv(qk, conv_w, conv_b))
    q, k = jnp.split(qk, 2, axis=-1)
    q = split_heads(q, GROUP_HEADS).astype(F32)
    k = split_heads(k, GROUP_HEADS).astype(F32) * HEAD_DIM ** -0.5
    v = split_heads(v, GROUP_HEADS).astype(F32)
    log_i = (ig + i_bias).astype(F32)
    log_f = jax.nn.log_sigmoid((fg + f_bias).astype(F32))
    tri = jnp.tril(jnp.ones((ML_CHUNK, ML_CHUNK), bool))

    def step(carry, inp):
        cmat, nvec, m = carry
        qc, kc, vc, li, lf = inp
        bcum = jnp.cumsum(lf, axis=-1)
        log_d = jnp.where(tri, bcum[..., :, None] - bcum[..., None, :] + li[..., None, :], -jnp.inf)
        inter = bcum + m[..., None]
        m_t = jnp.maximum(inter, jnp.max(log_d, axis=-1))
        d_mat = jnp.exp(log_d - m_t[..., None])
        w_inter = jnp.exp(inter - m_t)
        s = jnp.einsum('bhtd,bhsd->bhts', qc, kc) * d_mat
        num = (w_inter[..., None] * jnp.einsum('bhtd,bhdv->bhtv', qc, cmat)
               + jnp.einsum('bhts,bhsv->bhtv', s, vc))
        den = w_inter * jnp.einsum('bhtd,bhd->bht', qc, nvec) + jnp.sum(s, axis=-1)
        h = num / jnp.maximum(jnp.abs(den), jnp.exp(-m_t))[..., None]
        b_last = bcum[..., -1]
        log_w = b_last[..., None] - bcum + li
        m_new = jnp.maximum(b_last + m, jnp.max(log_w, axis=-1))
        w_s = jnp.exp(log_w - m_new[..., None])
        decay = jnp.exp(b_last + m - m_new)
        cmat = decay[..., None, None] * cmat + jnp.einsum('bhs,bhsd,bhsv->bhdv', w_s, kc, vc)
        nvec = decay[..., None] * nvec + jnp.einsum('bhs,bhsd->bhd', w_s, kc)
        return (cmat, nvec, m_new), h

    init = (jnp.zeros((b_, GROUP_HEADS, HEAD_DIM, HEAD_DIM), F32),
            jnp.zeros((b_, GROUP_HEADS, HEAD_DIM), F32),
            jnp.full((b_, GROUP_HEADS), -1e30, F32))
    xs = (to_chunks(q, ML_CHUNK), to_chunks(k, ML_CHUNK), to_chunks(v, ML_CHUNK),
          to_chunks(log_i, ML_CHUNK), to_chunks(log_f, ML_CHUNK))
    _, h = lax.scan(step, init, xs)
    h = rms_norm(from_chunks(h), o_gain).astype(cols.dtype)
    h = h * jax.nn.sigmoid(split_heads(og, GROUP_HEADS))
    return h.reshape(b_, l_, GROUP_WIDTH)


def kernel(x, mem, lb_param, norm_mix, w_in, w_out, hg_o_gain, dsa_kv_gain, dsa_w_uk, dsa_w_uv,
           dsa_q_gain, dsa_k_gain, dsa_idxk_gain, nsa_pos_k, nsa_pos_v, nsa_k_w1, nsa_k_w2,
           nsa_v_w1, nsa_v_w2, nsa_q_gain, nsa_k_gains, ml_conv_w, ml_conv_b, ml_i_bias,
           ml_f_bias, ml_o_gain, norm_xa, norm_mem, xa_wq, xa_wkv, xa_wo, xa_q_gain, xa_k_gain,
           norm_ffn, ffn_w13, ffn_w2):
    b_, l_, d = x.shape
    pos = jnp.arange(l_)
    lb_all = jnp.cumsum(jax.nn.softmax(lb_param.astype(F32), axis=0), axis=0)
    lb_all = lb_all - lb_all[:1]
    n_pad = _round_up(IN_COLS, LANES)
    x2d = x.reshape(b_ * l_, d)
    mem2d = mem.reshape(b_ * N_MEM, d)
    for l in range(DEPTH):
        w_in_p = jnp.pad(w_in[l], ((0, 0), (0, n_pad - IN_COLS))).astype(BF16)
        cols = norm_matmul(x2d, norm_mix[l], w_in_p, tm=256)[:, :IN_COLS].reshape(b_, l_, IN_COLS)
        c_hg, c_dsa, c_nsa, c_ml = split_cols(cols, GROUP_COLS)
        mixed = jnp.concatenate([
            hgrn2_mixer(c_hg, lb_all[l], hg_o_gain[l]),
            dsa_mixer(c_dsa, pos, dsa_kv_gain[l], dsa_w_uk[l], dsa_w_uv[l],
                      dsa_q_gain[l], dsa_k_gain[l], dsa_idxk_gain[l]),
            nsa_mixer(c_nsa, pos, nsa_pos_k[l], nsa_pos_v[l], nsa_k_w1[l], nsa_k_w2[l],
                      nsa_v_w1[l], nsa_v_w2[l], nsa_q_gain[l], nsa_k_gains[l]),
            mlstm_mixer(c_ml, ml_conv_w[l], ml_conv_b[l], ml_i_bias[l], ml_f_bias[l], ml_o_gain[l]),
        ], axis=-1)
        k_mem, v_mem = mem_kv(mem2d, norm_mem[l], xa_wkv[l].astype(BF16), xa_k_gain[l])
        x2d = post_block(x2d, mixed.reshape(b_ * l_, -1), l_, w_out[l].astype(BF16), norm_xa[l],
                         xa_wq[l].astype(BF16), xa_q_gain[l], k_mem, v_mem,
                         xa_wo[l].astype(BF16), norm_ffn[l], ffn_w13[l].astype(BF16),
                         ffn_w2[l].astype(BF16), tm=256)
    return x2d.reshape(b_, l_, d)
```

```python
import functools

import jax
import jax.numpy as jnp
from jax import lax
import numpy as np
from jax.experimental import pallas as pl
from jax.experimental.pallas import tpu as pltpu

F32 = jnp.float32
BF16 = jnp.bfloat16

D_MODEL = 1024
DEPTH = 2
HEAD_DIM = 64
GROUP_HEADS = 4
GROUP_WIDTH = GROUP_HEADS * HEAD_DIM
ROPE_THETA = 500000.0
EPS = 1e-6
Q_BLOCK = 128
N_MEM = 256
XA_HEADS = 4
XA_WIDTH = XA_HEADS * HEAD_DIM
HG_CHUNK = 64
DSA_LATENT = 128
IDX_HEADS = 8
IDX_DIM = 32
DSA_TOPK_MAX = 256
CMP_BLOCK = 32
CMP_STRIDE = 16
SLC_BLOCK = 64
SLC_SHIFT = 6
SLC_TOPN = 16
WINDOW = 512
ML_CHUNK = 64
CONV_WIDTH = 4
D_FF = 2816

HG_SPLITS = (GROUP_WIDTH,) * 4
DSA_SPLITS = (GROUP_WIDTH, DSA_LATENT, IDX_HEADS * IDX_DIM, IDX_DIM, IDX_HEADS)
NSA_SPLITS = (GROUP_WIDTH,) + (HEAD_DIM,) * 6 + (3 * GROUP_HEADS,)
ML_SPLITS = (2 * GROUP_WIDTH, GROUP_WIDTH, GROUP_WIDTH, GROUP_HEADS, GROUP_HEADS)
GROUP_COLS = (sum(HG_SPLITS), sum(DSA_SPLITS), sum(NSA_SPLITS), sum(ML_SPLITS))
IN_COLS = sum(GROUP_COLS)

V7X_VMEM_LIMIT_BYTES = 56 * 1024 * 1024
LANES = 128
FF_CHUNK = 256


def _round_up(n, m):
    return -(-n // m) * m


def _rms_rows(t, g):
    return t * lax.rsqrt(jnp.mean(t * t, axis=-1, keepdims=True) + EPS) * g


def _const_spec(shape):
    return pl.BlockSpec(shape, lambda *_: (0,) * len(shape), pipeline_mode=pl.Buffered(1))


def _norm_matmul_kernel(x_ref, g_ref, w_ref, o_ref):
    h = _rms_rows(x_ref[...], g_ref[...]).astype(BF16)
    o_ref[...] = jnp.dot(h, w_ref[...], preferred_element_type=F32)


def norm_matmul(x2d, gain, w_bf16, tm):
    m, k = x2d.shape
    n = w_bf16.shape[1]
    return pl.pallas_call(
        _norm_matmul_kernel,
        out_shape=jax.ShapeDtypeStruct((m, n), F32),
        grid=(m // tm,),
        in_specs=[
            pl.BlockSpec((tm, k), lambda i: (i, 0)),
            _const_spec((1, k)),
            _const_spec((k, n)),
        ],
        out_specs=pl.BlockSpec((tm, n), lambda i: (i, 0)),
        compiler_params=pltpu.CompilerParams(
            dimension_semantics=("arbitrary",), vmem_limit_bytes=V7X_VMEM_LIMIT_BYTES),
        name="norm_matmul",
    )(x2d, gain.reshape(1, k), w_bf16)


def _mem_kv_kernel(m_ref, g_ref, w_ref, kg_ref, k_ref, v_ref):
    mn = _rms_rows(m_ref[...], g_ref[...]).astype(BF16)
    kv = jnp.dot(mn, w_ref[...], preferred_element_type=F32)
    kg = kg_ref[...]
    ks = []
    for h in range(XA_HEADS):
        kh = kv[:, h * HEAD_DIM:(h + 1) * HEAD_DIM]
        ks.append(_rms_rows(kh, kg))
    k_ref[...] = jnp.concatenate(ks, axis=-1).astype(BF16)
    v_ref[...] = kv[:, XA_WIDTH:].astype(BF16)


def mem_kv(mem2d, gain, wkv_bf16, k_gain):
    m, k = mem2d.shape
    return pl.pallas_call(
        _mem_kv_kernel,
        out_shape=(jax.ShapeDtypeStruct((m, XA_WIDTH), BF16),
                   jax.ShapeDtypeStruct((m, XA_WIDTH), BF16)),
        grid=(m // N_MEM,),
        in_specs=[
            pl.BlockSpec((N_MEM, k), lambda i: (i, 0)),
            _const_spec((1, k)),
            _const_spec((k, 2 * XA_WIDTH)),
            _const_spec((1, HEAD_DIM)),
        ],
        out_specs=(pl.BlockSpec((N_MEM, XA_WIDTH), lambda i: (i, 0)),
                   pl.BlockSpec((N_MEM, XA_WIDTH), lambda i: (i, 0))),
        compiler_params=pltpu.CompilerParams(
            dimension_semantics=("arbitrary",), vmem_limit_bytes=V7X_VMEM_LIMIT_BYTES),
        name="mem_kv",
    )(mem2d, gain.reshape(1, k), wkv_bf16, k_gain.reshape(1, HEAD_DIM))


def _post_kernel(x_ref, mix_ref, wout_ref, gxa_ref, wq_ref, qg_ref, k_ref, v_ref, wo_ref,
                 gffn_ref, w13_ref, w2_ref, o_ref):
    x = x_ref[...] + jnp.dot(mix_ref[...].astype(BF16), wout_ref[...],
                             preferred_element_type=F32)
    h = _rms_rows(x, gxa_ref[...]).astype(BF16)
    q = jnp.dot(h, wq_ref[...], preferred_element_type=F32)
    qg = qg_ref[...] * (HEAD_DIM ** -0.5)
    k = k_ref[...]
    v = v_ref[...]
    outs = []
    for hd in range(XA_HEADS):
        sl = slice(hd * HEAD_DIM, (hd + 1) * HEAD_DIM)
        qh = _rms_rows(q[:, sl], qg).astype(BF16)
        s = lax.dot_general(qh, k[:, sl], (((1,), (1,)), ((), ())),
                            preferred_element_type=F32)
        e = jnp.exp(s - jnp.max(s, axis=-1, keepdims=True))
        p = e / jnp.sum(e, axis=-1, keepdims=True)
        outs.append(jnp.dot(p.astype(BF16), v[:, sl], preferred_element_type=F32))
    o = jnp.concatenate(outs, axis=-1).astype(BF16)
    x = x + jnp.dot(o, wo_ref[...], preferred_element_type=F32)
    h = _rms_rows(x, gffn_ref[...]).astype(BF16)
    acc = x
    for c in range(D_FF // FF_CHUNK):
        a = jnp.dot(h, w13_ref[:, c * FF_CHUNK:(c + 1) * FF_CHUNK], preferred_element_type=F32)
        b = jnp.dot(h, w13_ref[:, D_FF + c * FF_CHUNK:D_FF + (c + 1) * FF_CHUNK],
                    preferred_element_type=F32)
        act = (a * jax.nn.sigmoid(a) * b).astype(BF16)
        acc = acc + jnp.dot(act, w2_ref[c * FF_CHUNK:(c + 1) * FF_CHUNK, :],
                            preferred_element_type=F32)
    o_ref[...] = acc


def post_block(x2d, mixed2d, seq, wout, gxa, wq, qg, k_mem, v_mem, wo, gffn, w13, w2, tm):
    m, d = x2d.shape
    tiles_per_batch = seq // tm
    row = lambda i: (i, 0)
    mem_row = lambda i: (i // tiles_per_batch, 0)
    return pl.pallas_call(
        _post_kernel,
        out_shape=jax.ShapeDtypeStruct((m, d), F32),
        grid=(m // tm,),
        in_specs=[
            pl.BlockSpec((tm, d), row),
            pl.BlockSpec((tm, mixed2d.shape[1]), row),
            _const_spec(wout.shape),
            _const_spec((1, d)),
            _const_spec(wq.shape),
            _const_spec((1, HEAD_DIM)),
            pl.BlockSpec((N_MEM, XA_WIDTH), mem_row),
            pl.BlockSpec((N_MEM, XA_WIDTH), mem_row),
            _const_spec(wo.shape),
            _const_spec((1, d)),
            _const_spec(w13.shape),
            _const_spec(w2.shape),
        ],
        out_specs=pl.BlockSpec((tm, d), row),
        compiler_params=pltpu.CompilerParams(
            dimension_semantics=("arbitrary",), vmem_limit_bytes=V7X_VMEM_LIMIT_BYTES),
        name="post_block",
    )(x2d, mixed2d, wout, gxa.reshape(1, d), wq, qg.reshape(1, HEAD_DIM), k_mem, v_mem, wo,
      gffn.reshape(1, d), w13, w2)


C_HG = 0
C_DSA_Q = 1024
C_DSA_IQ = 1280
C_DSA_CKV = 1536
C_DSA_IK = 1664
C_DSA_IW = 1792
C_NSA_GATE = 1920
C_NSA_Q = 2048
C_NSA_CMP = 2304
C_NSA_SLC = 2432
C_NSA_WIN = 2560
C_ML_GATE = 2688
C_ML_QK = 2816
C_ML_V = 3328
C_ML_OG = 3584
IN_COLS_PACKED = 3840


def pack_w_in(w):
    dsa0 = GROUP_COLS[0]
    nsa0 = dsa0 + GROUP_COLS[1]
    ml0 = nsa0 + GROUP_COLS[2]

    def seg(a, n, width=None):
        s = w[:, a:a + n]
        if width is not None and width > n:
            s = jnp.pad(s, ((0, 0), (0, width - n)))
        return s

    parts = [
        seg(0, 1024),
        seg(dsa0, 256), seg(dsa0 + 384, 256), seg(dsa0 + 256, 128),
        seg(dsa0 + 640, 32, LANES), seg(dsa0 + 672, 8, LANES),
        seg(nsa0 + 640, 12, LANES), seg(nsa0, 256), seg(nsa0 + 256, 128),
        seg(nsa0 + 384, 128), seg(nsa0 + 512, 128),
        seg(ml0 + 1024, 8, LANES), seg(ml0, 512), seg(ml0 + 512, 256), seg(ml0 + 768, 256),
    ]
    return jnp.concatenate(parts, axis=1)


def rope_tables(l_, d=HEAD_DIM):
    rd = d // 4
    half = rd // 2
    inv = ROPE_THETA ** (-jnp.arange(half, dtype=F32) * 2.0 / rd)
    ang = jnp.arange(l_).astype(F32)[:, None] * inv[None, :]
    cos, sin = jnp.cos(ang), jnp.sin(ang)
    zh = jnp.zeros((l_, half), F32)
    rest0 = jnp.zeros((l_, d - rd), F32)
    c = jnp.concatenate([cos, cos, rest0 + 1.0], axis=1)
    s1 = jnp.concatenate([-sin, zh, rest0], axis=1)
    s2 = jnp.concatenate([zh, sin, rest0], axis=1)
    return jnp.stack([c, s1, s2])


def rope_tables_pad(rt, width):
    n = width - rt.shape[-1]
    ident = jnp.stack([jnp.ones(rt.shape[1:2] + (n,), F32), jnp.zeros(rt.shape[1:2] + (n,), F32),
                       jnp.zeros(rt.shape[1:2] + (n,), F32)])
    return jnp.concatenate([rt, ident], axis=-1)


def rope_tables_kv_pair(rt):
    return rope_tables_pad(rt, 2 * rt.shape[-1])


def _apply_rope(t, rope_ref, half=HEAD_DIM // 8):
    w = t.shape[-1]
    return (t * rope_ref[0] + pltpu.roll(t, w - half, 1) * rope_ref[1]
            + pltpu.roll(t, half, 1) * rope_ref[2])


_NT = (((1,), (1,)), ((), ()))


def _dot_nt(a, b):
    return lax.dot_general(a, b, _NT, preferred_element_type=F32)


def _dot(a, b):
    return jnp.dot(a, b, preferred_element_type=F32)


NEG_BIG = -(2.0 ** 30)
NSA_KEY_TILE = 256
NSA_TQ = 128
NSA_SEL_TQ = 256
NSA_PREP_TK = 512


def _nsa_kv_prep_kernel(ps_ref, pw_ref, rope_ref, gs_ref, gw_ref, kaug_ref, vs_ref, kw_ref, vw_ref):
    tk = ps_ref.shape[0]
    lane = lax.broadcasted_iota(jnp.int32, (tk, LANES), 1)
    is_k = lane < HEAD_DIM

    def norm_rope(p, g):
        ms = jnp.sum(jnp.where(is_k, p * p, 0.0), axis=-1, keepdims=True) * (1.0 / HEAD_DIM)
        y = jnp.where(is_k, p * lax.rsqrt(ms + EPS) * g, p)
        return _apply_rope(y, rope_ref)

    ys = norm_rope(ps_ref[...], gs_ref[...])
    yw = norm_rope(pw_ref[...], gw_ref[...])
    row = pl.program_id(1) * tk + lax.broadcasted_iota(jnp.int32, (tk, LANES), 0)
    ind = jnp.where(jnp.right_shift(row, SLC_SHIFT) == (lane - HEAD_DIM), 1.0, 0.0)
    kaug_ref[0] = jnp.where(is_k, ys, ind).astype(BF16)
    vs_ref[0] = ys[:, HEAD_DIM:].astype(BF16)
    kw_ref[0] = yw[:, :HEAD_DIM].astype(BF16)
    vw_ref[0] = yw[:, HEAD_DIM:].astype(BF16)


def nsa_kv_prep(cols, b_, l_, rope_pair, g_slc, g_win):
    tk = NSA_PREP_TK
    nt = l_ // tk
    ones = jnp.ones((HEAD_DIM,), F32)
    gs = jnp.concatenate([g_slc, ones]).reshape(1, LANES)
    gw = jnp.concatenate([g_win, ones]).reshape(1, LANES)
    kv = lambda w: jax.ShapeDtypeStruct((b_, l_, w), BF16)
    out_blk = lambda w: pl.BlockSpec((1, tk, w), lambda b, i: (b, i, 0))
    return pl.pallas_call(
        _nsa_kv_prep_kernel,
        out_shape=(kv(LANES), kv(HEAD_DIM), kv(HEAD_DIM), kv(HEAD_DIM)),
        grid=(b_, nt),
        in_specs=[
            pl.BlockSpec((tk, LANES), lambda b, i: (b * nt + i, C_NSA_SLC // LANES)),
            pl.BlockSpec((tk, LANES), lambda b, i: (b * nt + i, C_NSA_WIN // LANES)),
            pl.BlockSpec((3, tk, LANES), lambda b, i: (0, i, 0)),
            pl.BlockSpec((1, LANES), lambda b, i: (0, 0)),
            pl.BlockSpec((1, LANES), lambda b, i: (0, 0)),
        ],
        out_specs=(out_blk(LANES), out_blk(HEAD_DIM), out_blk(HEAD_DIM), out_blk(HEAD_DIM)),
        compiler_params=pltpu.CompilerParams(
            dimension_semantics=("arbitrary", "arbitrary"),
            vmem_limit_bytes=V7X_VMEM_LIMIT_BYTES),
        name="nsa_kv_prep",
    )(cols, cols, rope_pair, gs, gw)


def _nsa_compress_kernel(r_ref, pea_ref, peb_ref, w1a_ref, w1b_ref, w2k_ref, w2v_ref, kg_ref,
                         kc_ref, vc_ref):
    r = r_ref[0]
    n = r.shape[0]
    a = _dot((r + pea_ref[...]).astype(BF16), w1a_ref[...])
    bm = _dot((r + peb_ref[...]).astype(BF16), w1b_ref[...])
    row = lax.broadcasted_iota(jnp.int32, bm.shape, 0)
    bm_up = jnp.where(row < n - 1, pltpu.roll(bm, n - 1, 0), 0.0)
    h = jnp.maximum(a + bm_up, 0.0).astype(BF16)
    hid = w2k_ref.shape[0]
    ck = _dot(h[:, :hid], w2k_ref[...])
    cv = _dot(h[:, hid:], w2v_ref[...])
    kc_ref[0] = _rms_rows(ck, kg_ref[...]).astype(BF16)
    vc_ref[0] = cv.astype(BF16)


def nsa_compress(cols, b_, l_, pos_k, pos_v, k_w1, k_w2, v_w1, v_w2, k_gain):
    rows = l_ // CMP_STRIDE
    hid = k_w1.shape[1]
    pair = cols[:, C_NSA_CMP:C_NSA_CMP + LANES].reshape(b_, rows, CMP_STRIDE * LANES)

    def interleave_pe(lo):
        pe = jnp.concatenate([pos_k[lo:lo + CMP_STRIDE], pos_v[lo:lo + CMP_STRIDE]], axis=1)
        return pe.reshape(1, CMP_STRIDE * LANES)

    def interleave_w(lo):
        wk = k_w1[lo * HEAD_DIM:(lo + CMP_STRIDE) * HEAD_DIM].reshape(CMP_STRIDE, HEAD_DIM, hid)
        wv = v_w1[lo * HEAD_DIM:(lo + CMP_STRIDE) * HEAD_DIM].reshape(CMP_STRIDE, HEAD_DIM, hid)
        z = jnp.zeros_like(wk)
        top = jnp.concatenate([wk, z], axis=2)
        bot = jnp.concatenate([z, wv], axis=2)
        return jnp.concatenate([top, bot], axis=1).reshape(CMP_STRIDE * LANES, 2 * hid).astype(BF16)

    out = jax.ShapeDtypeStruct((b_, rows, HEAD_DIM), BF16)
    return pl.pallas_call(
        _nsa_compress_kernel,
        out_shape=(out, out),
        grid=(b_,),
        in_specs=[
            pl.BlockSpec((1, rows, CMP_STRIDE * LANES), lambda b: (b, 0, 0)),
            _const_spec((1, CMP_STRIDE * LANES)),
            _const_spec((1, CMP_STRIDE * LANES)),
            _const_spec((CMP_STRIDE * LANES, 2 * hid)),
            _const_spec((CMP_STRIDE * LANES, 2 * hid)),
            _const_spec((hid, HEAD_DIM)),
            _const_spec((hid, HEAD_DIM)),
            _const_spec((1, HEAD_DIM)),
        ],
        out_specs=(pl.BlockSpec((1, rows, HEAD_DIM), lambda b: (b, 0, 0)),
                   pl.BlockSpec((1, rows, HEAD_DIM), lambda b: (b, 0, 0))),
        compiler_params=pltpu.CompilerParams(
            dimension_semantics=("arbitrary",), vmem_limit_bytes=V7X_VMEM_LIMIT_BYTES),
        name="nsa_compress",
    )(pair, interleave_pe(0), interleave_pe(CMP_STRIDE), interleave_w(0),
      interleave_w(CMP_STRIDE), k_w2.astype(BF16), v_w2.astype(BF16),
      k_gain.reshape(1, HEAD_DIM))


def _masked_softmax(s, valid, axis):
    m = jnp.max(jnp.where(valid, s, -jnp.inf), axis=axis, keepdims=True)
    m = jnp.where(m == -jnp.inf, 0.0, m)
    e = jnp.where(valid, jnp.exp(s - m), 0.0)
    den = jnp.sum(e, axis=axis, keepdims=True)
    return e / jnp.where(den > 0, den, 1.0)


def _nsa_select_kernel(q_ref, kc_ref, vc_ref, ovt_ref, rope_ref, qg_ref, qaug_ref, ocmp_ref,
                       imp_ref, *, n_sel):
    tq = q_ref.shape[0]
    ncr = kc_ref.shape[1]
    n_slc = ovt_ref.shape[0]
    t0 = pl.program_id(1) * tq
    scale = HEAD_DIM ** -0.5
    q = q_ref[...]
    g = qg_ref[...]
    kc = kc_ref[0]
    vc = vc_ref[0]
    last = CMP_BLOCK - 1
    vis = (lax.broadcasted_iota(jnp.int32, (tq, ncr), 1) * CMP_STRIDE + last
           <= t0 + lax.broadcasted_iota(jnp.int32, (tq, ncr), 0))
    vis_t = (lax.broadcasted_iota(jnp.int32, (ncr, tq), 0) * CMP_STRIDE + last
             <= t0 + lax.broadcasted_iota(jnp.int32, (ncr, tq), 1))
    qn_heads, o_heads = [], []
    psum_t = jnp.zeros((ncr, tq), F32)
    for h in range(GROUP_HEADS):
        qn = _rms_rows(q[:, h * HEAD_DIM:(h + 1) * HEAD_DIM], g)
        qn_heads.append(qn)
        qb = (qn * scale).astype(BF16)
        p = _masked_softmax(_dot_nt(qb, kc), vis, 1)
        o_heads.append(_dot(p.astype(BF16), vc))
        psum_t = psum_t + _masked_softmax(_dot_nt(kc, qb), vis_t, 0)
    ocmp_ref[0] = jnp.concatenate(o_heads, axis=-1)

    hi = psum_t.astype(BF16)
    lo = (psum_t - hi.astype(F32)).astype(BF16)
    imp = _dot(ovt_ref[...], hi) + _dot(ovt_ref[...], lo)
    blk = lax.broadcasted_iota(jnp.int32, (n_slc, tq), 0)
    cur = jnp.right_shift(t0 + lax.broadcasted_iota(jnp.int32, (n_slc, tq), 1), SLC_SHIFT)
    forced = (blk == 0) | (blk == cur) | (blk == cur - 1)
    imp = jnp.where(forced, jnp.inf, jnp.where(blk > cur, -jnp.inf, imp))
    imp_ref[...] = imp

    def count_beats(m, cnt):
        row = imp_ref[pl.ds(m, 1), :]
        beats = (row > imp) | ((row == imp) & (blk > m))
        return cnt + jnp.where(beats, 1.0, 0.0)

    cnt = lax.fori_loop(0, n_slc, count_beats, jnp.zeros((n_slc, tq), F32))
    mt = jnp.where(cnt < n_sel, 0.0, NEG_BIG)
    pad = jnp.zeros((HEAD_DIM - n_slc, tq), F32)
    mt = jnp.concatenate([mt, pad, mt, pad], axis=0) if n_slc < HEAD_DIM else jnp.concatenate(
        [mt, mt], axis=0)
    mt = mt.T

    qr = _apply_rope(jnp.concatenate(qn_heads, axis=-1), rope_ref) * scale
    lane = lax.broadcasted_iota(jnp.int32, (tq, LANES), 1)
    for j in range(GROUP_HEADS // 2):
        pair = qr[:, j * LANES:(j + 1) * LANES]
        swapped = pltpu.roll(pair, HEAD_DIM, 1)
        qaug_ref[0, :, (2 * j) * LANES:(2 * j + 1) * LANES] = jnp.where(
            lane < HEAD_DIM, pair, mt).astype(BF16)
        qaug_ref[0, :, (2 * j + 1) * LANES:(2 * j + 2) * LANES] = jnp.where(
            lane < HEAD_DIM, swapped, mt).astype(BF16)


def nsa_select(cols, b_, l_, k_cmp, v_cmp, rope_q, q_gain):
    tq = min(NSA_SEL_TQ, l_)
    nt = l_ // tq
    ncr = l_ // CMP_STRIDE
    n_slc = l_ // SLC_BLOCK
    n_sel = min(SLC_TOPN, n_slc)
    st_c = np.arange(ncr) * CMP_STRIDE
    st_s = np.arange(n_slc) * SLC_BLOCK
    ovt = ((st_c[None, :] < st_s[:, None] + SLC_BLOCK)
           & (st_c[None, :] + CMP_BLOCK > st_s[:, None])).astype(np.float32)
    return pl.pallas_call(
        functools.partial(_nsa_select_kernel, n_sel=n_sel),
        out_shape=(jax.ShapeDtypeStruct((b_, l_, GROUP_HEADS * LANES), BF16),
                   jax.ShapeDtypeStruct((b_, l_, GROUP_WIDTH), F32)),
        grid=(b_, nt),
        in_specs=[
            pl.BlockSpec((tq, GROUP_WIDTH), lambda b, i: (b * nt + i, C_NSA_Q // GROUP_WIDTH)),
            pl.BlockSpec((1, ncr, HEAD_DIM), lambda b, i: (b, 0, 0)),
            pl.BlockSpec((1, ncr, HEAD_DIM), lambda b, i: (b, 0, 0)),
            pl.BlockSpec((n_slc, ncr), lambda b, i: (0, 0)),
            pl.BlockSpec((3, tq, GROUP_WIDTH), lambda b, i: (0, i, 0)),
            pl.BlockSpec((1, HEAD_DIM), lambda b, i: (0, 0)),
        ],
        out_specs=(pl.BlockSpec((1, tq, GROUP_HEADS * LANES), lambda b, i: (b, i, 0)),
                   pl.BlockSpec((1, tq, GROUP_WIDTH), lambda b, i: (b, i, 0))),
        scratch_shapes=[pltpu.VMEM((n_slc, tq), F32)],
        compiler_params=pltpu.CompilerParams(
            dimension_semantics=("arbitrary", "arbitrary"),
            vmem_limit_bytes=V7X_VMEM_LIMIT_BYTES),
        name="nsa_select",
    )(cols, k_cmp, v_cmp, jnp.asarray(ovt, BF16), rope_q, q_gain.reshape(1, HEAD_DIM))


def _nsa_attend_kernel(qaug_ref, kaug_ref, vs_ref, kw_ref, vw_ref, ocmp_ref, gate_ref, o_ref):
    tq = qaug_ref.shape[1]
    nh = GROUP_HEADS
    ts = NSA_KEY_TILE
    t0 = pl.program_id(1) * tq
    qa = qaug_ref[0]
    qs = jnp.concatenate([qa[:, h * LANES:(h + 1) * LANES] for h in range(nh)], axis=0)
    qpos = t0 + lax.rem(lax.broadcasted_iota(jnp.int32, (nh * tq, 1), 0), tq)

    def key_tile(kt, carry):
        m, l, acc = carry
        k0 = pl.multiple_of(kt * ts, ts)
        s = _dot_nt(qs, kaug_ref[0, pl.ds(k0, ts), :])
        kpos = k0 + lax.broadcasted_iota(jnp.int32, (nh * tq, ts), 1)
        s = jnp.where(kpos <= qpos, s, NEG_BIG)
        m_new = jnp.maximum(m, jnp.max(s, axis=-1, keepdims=True))
        alpha = jnp.exp(m - m_new)
        p = jnp.exp(s - m_new)
        l = alpha * l + jnp.sum(p, axis=-1, keepdims=True)
        acc = alpha * acc + _dot(p.astype(BF16), vs_ref[0, pl.ds(k0, ts), :])
        return m_new, l, acc

    n_kt = (t0 + tq + ts - 1) // ts
    init = (jnp.full((nh * tq, 1), -jnp.inf, F32), jnp.zeros((nh * tq, 1), F32),
            jnp.zeros((nh * tq, HEAD_DIM), F32))
    _, l, acc = lax.fori_loop(0, n_kt, key_tile, init)
    o_slc = acc / l

    wlen = WINDOW + tq
    start = pl.multiple_of(jnp.maximum(t0 - WINDOW, 0), tq)
    sw = _dot_nt(qs[:, :HEAD_DIM], kw_ref[0, pl.ds(start, wlen), :])
    dist = qpos - (start + lax.broadcasted_iota(jnp.int32, (nh * tq, wlen), 1))
    valid = (dist >= 0) & (dist < WINDOW)
    sw = jnp.where(valid, sw, -jnp.inf)
    e = jnp.exp(sw - jnp.max(sw, axis=-1, keepdims=True))
    o_swa = _dot(e.astype(BF16), vw_ref[0, pl.ds(start, wlen), :]) / jnp.sum(
        e, axis=-1, keepdims=True)

    g = jax.nn.sigmoid(gate_ref[...])
    oc = ocmp_ref[0]
    outs = []
    for h in range(nh):
        rows = slice(h * tq, (h + 1) * tq)
        outs.append(g[:, h:h + 1] * oc[:, h * HEAD_DIM:(h + 1) * HEAD_DIM]
                    + g[:, nh + h:nh + h + 1] * o_slc[rows]
                    + g[:, 2 * nh + h:2 * nh + h + 1] * o_swa[rows])
    o_ref[...] = jnp.concatenate(outs, axis=-1)


def nsa_attend(cols, b_, l_, q_aug, k_aug, v_slc, k_win, v_win, o_cmp):
    tq = NSA_TQ
    nt = l_ // tq
    seq = lambda w: pl.BlockSpec((1, l_, w), lambda b, i: (b, 0, 0))
    return pl.pallas_call(
        _nsa_attend_kernel,
        out_shape=jax.ShapeDtypeStruct((b_ * l_, GROUP_WIDTH), F32),
        grid=(b_, nt),
        in_specs=[
            pl.BlockSpec((1, tq, GROUP_HEADS * LANES), lambda b, i: (b, i, 0)),
            seq(LANES), seq(HEAD_DIM), seq(HEAD_DIM), seq(HEAD_DIM),
            pl.BlockSpec((1, tq, GROUP_WIDTH), lambda b, i: (b, i, 0)),
            pl.BlockSpec((tq, LANES), lambda b, i: (b * nt + i, C_NSA_GATE // LANES)),
        ],
        out_specs=pl.BlockSpec((tq, GROUP_WIDTH), lambda b, i: (b * nt + i, 0)),
        compiler_params=pltpu.CompilerParams(
            dimension_semantics=("arbitrary", "arbitrary"),
            vmem_limit_bytes=V7X_VMEM_LIMIT_BYTES),
        name="nsa_attend",
    )(q_aug, k_aug, v_slc, k_win, v_win, o_cmp, cols)


def nsa_mixer_pallas(cols, b_, l_, rope, pos_k, pos_v, k_w1, k_w2, v_w1, v_w2, q_gain, k_gains):
    k_aug, v_slc, k_win, v_win = nsa_kv_prep(cols, b_, l_, rope_tables_kv_pair(rope),
                                             k_gains[1], k_gains[2])
    k_cmp, v_cmp = nsa_compress(cols, b_, l_, pos_k, pos_v, k_w1, k_w2, v_w1, v_w2, k_gains[0])
    q_aug, o_cmp = nsa_select(cols, b_, l_, k_cmp, v_cmp, jnp.tile(rope, (1, 1, GROUP_HEADS)),
                              q_gain)
    return nsa_attend(cols, b_, l_, q_aug, k_aug, v_slc, k_win, v_win, o_cmp)


DSA_TQ = 128
DSA_KEY_TILE = 256
DSA_PREP_TK = 512
IDX_PACK = LANES
INT_MIN = -2 ** 31
MASKED_SCORE = -1e30


def _split_hi_lo(t):
    hi = t.astype(BF16)
    lo = (t - hi.astype(F32)).astype(BF16)
    return hi, lo


def _placement(rows, cols, pairs):
    p = np.zeros((rows, cols), np.float32)
    for r, c in pairs:
        p[r, c] = 1.0
    return jnp.asarray(p, BF16)


def _dsa_kv_prep_kernel(ckv_ref, ik_ref, rope_ref, ropei_ref, kvg_ref, wkv_ref, kg_ref, ikg_ref,
                        pkh_ref, pkl_ref, k_ref, v_ref, ik3_ref):
    tk = ckv_ref.shape[0]
    lane = lax.broadcasted_iota(jnp.int32, (tk, LANES), 1)
    ckv = _rms_rows(ckv_ref[...], kvg_ref[...]).astype(BF16)
    kv = _dot(ckv, wkv_ref[...])
    is_k = lane < HEAD_DIM
    ms = jnp.sum(jnp.where(is_k, kv * kv, 0.0), axis=-1, keepdims=True) * (1.0 / HEAD_DIM)
    y = _apply_rope(jnp.where(is_k, kv * lax.rsqrt(ms + EPS) * kg_ref[...], kv), rope_ref)
    k_ref[0] = y[:, :HEAD_DIM].astype(BF16)
    v_ref[0] = y[:, HEAD_DIM:].astype(BF16)
    ik = ik_ref[...]
    ms = jnp.sum(ik * ik, axis=-1, keepdims=True) * (1.0 / IDX_DIM)
    ikn = _apply_rope(ik * lax.rsqrt(ms + EPS) * ikg_ref[...], ropei_ref, IDX_DIM // 8)
    hi, lo = _split_hi_lo(ikn)
    ik3_ref[0] = (_dot(hi, pkh_ref[...]) + _dot(lo, pkl_ref[...])).astype(BF16)


def dsa_kv_prep(cols, b_, l_, rope_pair, rope_idx, kv_gain, w_uk, w_uv, k_gain, idxk_gain):
    tk = DSA_PREP_TK
    nt = l_ // tk
    ones = jnp.ones((HEAD_DIM,), F32)
    kg = jnp.concatenate([k_gain, ones]).reshape(1, LANES)
    ikg = jnp.pad(idxk_gain, (0, LANES - IDX_DIM)).reshape(1, LANES)
    wkv = jnp.concatenate([w_uk, w_uv], axis=1).astype(BF16)
    d = range(IDX_DIM)
    pkh = _placement(LANES, IDX_PACK, [(i, i) for i in d] + [(i, 2 * IDX_DIM + i) for i in d])
    pkl = _placement(LANES, IDX_PACK, [(i, IDX_DIM + i) for i in d])
    const = lambda shape: pl.BlockSpec(shape, lambda b, i: (0,) * len(shape))
    out = lambda w: pl.BlockSpec((1, tk, w), lambda b, i: (b, i, 0))
    return pl.pallas_call(
        _dsa_kv_prep_kernel,
        out_shape=(jax.ShapeDtypeStruct((b_, l_, HEAD_DIM), BF16),
                   jax.ShapeDtypeStruct((b_, l_, HEAD_DIM), BF16),
                   jax.ShapeDtypeStruct((b_, l_, IDX_PACK), BF16)),
        grid=(b_, nt),
        in_specs=[
            pl.BlockSpec((tk, LANES), lambda b, i: (b * nt + i, C_DSA_CKV // LANES)),
            pl.BlockSpec((tk, LANES), lambda b, i: (b * nt + i, C_DSA_IK // LANES)),
            pl.BlockSpec((3, tk, LANES), lambda b, i: (0, i, 0)),
            pl.BlockSpec((3, tk, LANES), lambda b, i: (0, i, 0)),
            const((1, LANES)), const((DSA_LATENT, LANES)), const((1, LANES)), const((1, LANES)),
            const((LANES, IDX_PACK)), const((LANES, IDX_PACK)),
        ],
        out_specs=(out(HEAD_DIM), out(HEAD_DIM), out(IDX_PACK)),
        compiler_params=pltpu.CompilerParams(
            dimension_semantics=("arbitrary", "arbitrary"),
            vmem_limit_bytes=V7X_VMEM_LIMIT_BYTES),
        name="dsa_kv_prep",
    )(cols, cols, rope_pair, rope_idx, kv_gain.reshape(1, DSA_LATENT), wkv, kg, ikg, pkh, pkl)


def _dsa_q_prep_kernel(q_ref, iq_ref, rope_ref, ropei_ref, qg_ref, pqh_ref, pql_ref,
                       qh_ref, iq3_ref):
    g = qg_ref[...]
    q = q_ref[...]
    qn = jnp.concatenate([_rms_rows(q[:, h * HEAD_DIM:(h + 1) * HEAD_DIM], g)
                          for h in range(GROUP_HEADS)], axis=-1)
    qr = _apply_rope(qn, rope_ref) * (HEAD_DIM ** -0.5)
    for h in range(GROUP_HEADS):
        qh_ref[0, h] = qr[:, h * HEAD_DIM:(h + 1) * HEAD_DIM].astype(BF16)
    hi, lo = _split_hi_lo(_apply_rope(iq_ref[...], ropei_ref, IDX_DIM // 8))
    iq3_ref[0] = (_dot(hi, pqh_ref[...]) + _dot(lo, pql_ref[...])).astype(BF16)


def dsa_q_prep(cols, b_, l_, rope_q, rope_iq, q_gain):
    tq = 256
    nt = l_ // tq
    w = IDX_HEADS * IDX_DIM
    hd = [(h, i) for h in range(IDX_HEADS) for i in range(IDX_DIM)]
    pqh = _placement(w, IDX_HEADS * IDX_PACK,
                     [(IDX_DIM * h + i, IDX_PACK * h + i) for h, i in hd]
                     + [(IDX_DIM * h + i, IDX_PACK * h + IDX_DIM + i) for h, i in hd])
    pql = _placement(w, IDX_HEADS * IDX_PACK,
                     [(IDX_DIM * h + i, IDX_PACK * h + 2 * IDX_DIM + i) for h, i in hd])
    const = lambda shape: pl.BlockSpec(shape, lambda b, i: (0,) * len(shape))
    return pl.pallas_call(
        _dsa_q_prep_kernel,
        out_shape=(jax.ShapeDtypeStruct((b_, GROUP_HEADS, l_, HEAD_DIM), BF16),
                   jax.ShapeDtypeStruct((b_, l_, IDX_HEADS * IDX_PACK), BF16)),
        grid=(b_, nt),
        in_specs=[
            pl.BlockSpec((tq, GROUP_WIDTH), lambda b, i: (b * nt + i, C_DSA_Q // GROUP_WIDTH)),
            pl.BlockSpec((tq, w), lambda b, i: (b * nt + i, C_DSA_IQ // w)),
            pl.BlockSpec((3, tq, GROUP_WIDTH), lambda b, i: (0, i, 0)),
            pl.BlockSpec((3, tq, w), lambda b, i: (0, i, 0)),
            const((1, HEAD_DIM)), const(pqh.shape), const(pql.shape),
        ],
        out_specs=(pl.BlockSpec((1, GROUP_HEADS, tq, HEAD_DIM), lambda b, i: (b, 0, i, 0)),
                   pl.BlockSpec((1, tq, IDX_HEADS * IDX_PACK), lambda b, i: (b, i, 0))),
        compiler_params=pltpu.CompilerParams(
            dimension_semantics=("arbitrary", "arbitrary"),
            vmem_limit_bytes=V7X_VMEM_LIMIT_BYTES),
        name="dsa_q_prep",
    )(cols, cols, rope_q, rope_iq, q_gain.reshape(1, HEAD_DIM), pqh, pql)


def _dsa_attend_kernel(qh_ref, iq3_ref, iw_ref, k_ref, v_ref, ik3_ref, o_ref, key_ref, *,
                       topk, idx_bits):
    tq = iq3_ref.shape[1]
    ts = DSA_KEY_TILE
    nh = GROUP_HEADS
    t0 = pl.program_id(1) * tq
    n_kt = (t0 + tq + ts - 1) // ts
    qpos = t0 + lax.broadcasted_iota(jnp.int32, (tq, 1), 0)

    iq3 = iq3_ref[0]
    lhs = jnp.concatenate([iq3[:, h * IDX_PACK:(h + 1) * IDX_PACK] for h in range(IDX_HEADS)],
                          axis=0)
    iw = iw_ref[...] * (IDX_HEADS ** -0.5 * IDX_DIM ** -0.5)
    w_col = jnp.concatenate([iw[:, h:h + 1] for h in range(IDX_HEADS)], axis=0)

    def score_tile(kt, _):
        k0 = pl.multiple_of(kt * ts, ts)
        rel = jnp.maximum(_dot_nt(lhs, ik3_ref[0, pl.ds(k0, ts), :]), 0.0) * w_col
        sc = rel[0:tq]
        for h in range(1, IDX_HEADS):
            sc = sc + rel[h * tq:(h + 1) * tq]
        kpos = k0 + lax.broadcasted_iota(jnp.int32, (tq, ts), 1)
        sc = jnp.where(kpos <= qpos, sc + 0.0, -jnp.inf)
        bits = pltpu.bitcast(sc, jnp.int32)
        key_ref[:, pl.ds(k0, ts)] = bits ^ (jnp.right_shift(bits, 31) & jnp.int32(0x7FFFFFFF))
        return 0

    lax.fori_loop(0, n_kt, score_tile, 0)

    def count(pred):
        def tile(kt, c):
            k0 = pl.multiple_of(kt * ts, ts)
            kpos = k0 + lax.broadcasted_iota(jnp.int32, (tq, ts), 1)
            return c + jnp.where(pred(key_ref[:, pl.ds(k0, ts)], kpos), 1.0, 0.0)
        c = lax.fori_loop(0, n_kt, tile, jnp.zeros((tq, ts), F32))
        return jnp.sum(c, axis=-1, keepdims=True)

    def value_bit(i, thr):
        cand = thr | jnp.left_shift(jnp.int32(1), 31 - i)
        cand_s = cand ^ jnp.int32(INT_MIN)
        return jnp.where(count(lambda kk, kpos: kk >= cand_s) >= topk, cand, thr)

    thr = lax.fori_loop(0, 32, value_bit, jnp.zeros((tq, 1), jnp.int32)) ^ jnp.int32(INT_MIN)
    need = topk - count(lambda kk, kpos: kk > thr)

    def index_bit(i, last):
        cand = last | jnp.left_shift(jnp.int32(1), idx_bits - 1 - i)
        below = count(lambda kk, kpos: (kk == thr) & (kpos < cand))
        return jnp.where(below < need, cand, last)

    last = lax.fori_loop(0, idx_bits, index_bit, jnp.zeros((tq, 1), jnp.int32))

    qs = qh_ref[0].reshape(nh * tq, HEAD_DIM)

    def key_tile(kt, carry):
        m, l, acc = carry
        k0 = pl.multiple_of(kt * ts, ts)
        kk = key_ref[:, pl.ds(k0, ts)]
        kpos = k0 + lax.broadcasted_iota(jnp.int32, (tq, ts), 1)
        sel = ((kk > thr) | ((kk == thr) & (kpos <= last))) & (kpos <= qpos)
        bias = jnp.where(sel, 0.0, MASKED_SCORE)
        s = _dot_nt(qs, k_ref[0, pl.ds(k0, ts), :]) + jnp.concatenate([bias] * nh, axis=0)
        m_new = jnp.maximum(m, jnp.max(s, axis=-1, keepdims=True))
        alpha = jnp.exp(m - m_new)
        p = jnp.exp(s - m_new)
        l = alpha * l + jnp.sum(p, axis=-1, keepdims=True)
        acc = alpha * acc + _dot(p.astype(BF16), v_ref[0, pl.ds(k0, ts), :])
        return m_new, l, acc

    init = (jnp.full((nh * tq, 1), MASKED_SCORE, F32), jnp.zeros((nh * tq, 1), F32),
            jnp.zeros((nh * tq, HEAD_DIM), F32))
    _, l, acc = lax.fori_loop(0, n_kt, key_tile, init)
    o = acc / l
    o_ref[...] = jnp.concatenate([o[h * tq:(h + 1) * tq] for h in range(nh)], axis=-1)


def dsa_attend(cols, b_, l_, qh, iq3, k, v, ik3):
    tq = DSA_TQ
    nt = l_ // tq
    topk = min(DSA_TOPK_MAX, l_ // 4)
    idx_bits = int(np.log2(l_))
    assert 2 ** idx_bits == l_ and l_ % DSA_KEY_TILE == 0 and topk <= DSA_KEY_TILE
    seq = lambda w: pl.BlockSpec((1, l_, w), lambda b, i: (b, 0, 0))
    return pl.pallas_call(
        functools.partial(_dsa_attend_kernel, topk=topk, idx_bits=idx_bits),
        out_shape=jax.ShapeDtypeStruct((b_ * l_, GROUP_WIDTH), F32),
        grid=(b_, nt),
        in_specs=[
            pl.BlockSpec((1, GROUP_HEADS, tq, HEAD_DIM), lambda b, i: (b, 0, i, 0)),
            pl.BlockSpec((1, tq, IDX_HEADS * IDX_PACK), lambda b, i: (b, i, 0)),
            pl.BlockSpec((tq, LANES), lambda b, i: (b * nt + i, C_DSA_IW // LANES)),
            seq(HEAD_DIM), seq(HEAD_DIM), seq(IDX_PACK),
        ],
        out_specs=pl.BlockSpec((tq, GROUP_WIDTH), lambda b, i: (b * nt + i, 0)),
        scratch_shapes=[pltpu.VMEM((tq, l_), jnp.int32)],
        compiler_params=pltpu.CompilerParams(
            dimension_semantics=("arbitrary", "arbitrary"),
            vmem_limit_bytes=V7X_VMEM_LIMIT_BYTES),
        name="dsa_attend",
    )(qh, iq3, cols, k, v, ik3)


def dsa_mixer_pallas(cols, b_, l_, rope, kv_gain, w_uk, w_uv, q_gain, k_gain, idxk_gain):
    rope_i = rope_tables(l_, IDX_DIM)
    k, v, ik3 = dsa_kv_prep(cols, b_, l_, rope_tables_kv_pair(rope), rope_tables_pad(rope_i, LANES),
                            kv_gain, w_uk, w_uv, k_gain, idxk_gain)
    qh, iq3 = dsa_q_prep(cols, b_, l_, jnp.tile(rope, (1, 1, GROUP_HEADS)),
                         jnp.tile(rope_i, (1, 1, IDX_HEADS)), q_gain)
    return dsa_attend(cols, b_, l_, qh, iq3, k, v, ik3)


def split_cols(t, sizes):
    return jnp.split(t, [int(s) for s in np.cumsum(sizes)[:-1]], axis=-1)


def split_heads(t, n):
    return t.reshape(t.shape[:-1] + (n, t.shape[-1] // n))


def rms_norm(t, g):
    t32 = t.astype(F32)
    y = t32 * lax.rsqrt(jnp.mean(t32 * t32, axis=-1, keepdims=True) + EPS)
    return (y * g.astype(F32)).astype(t.dtype)


def partial_rope(t, pos):
    d = t.shape[-1]
    rd = d // 4
    half = rd // 2
    inv = ROPE_THETA ** (-jnp.arange(half, dtype=F32) * 2.0 / rd)
    ang = pos.astype(F32)[:, None] * inv[None, :]
    cos = jnp.cos(ang)[:, None, :].astype(t.dtype)
    sin = jnp.sin(ang)[:, None, :].astype(t.dtype)
    x1, x2 = t[..., :half], t[..., half:rd]
    return jnp.concatenate([x1 * cos - x2 * sin, x2 * cos + x1 * sin, t[..., rd:]], axis=-1)


def masked_softmax(s, mask):
    s = jnp.where(mask, s.astype(F32), -jnp.inf)
    m = jnp.max(s, axis=-1, keepdims=True)
    m = jnp.where(jnp.isfinite(m), m, 0.0)
    e = jnp.exp(s - m)
    den = jnp.sum(e, axis=-1, keepdims=True)
    return e / jnp.where(den > 0, den, 1.0)


def gather_rows(table, idx):
    return jax.vmap(lambda tb, ix: tb[ix])(table, idx)


def causal_conv(t, w, b):
    y = lax.conv_general_dilated(t, w[:, None, :].astype(t.dtype), window_strides=(1,),
                                 padding=[(CONV_WIDTH - 1, 0)],
                                 dimension_numbers=('NWC', 'WIO', 'NWC'),
                                 feature_group_count=t.shape[-1])
    return y + b.astype(t.dtype)


def to_chunks(t, c):
    b_, l_, h_ = t.shape[:3]
    t = t.reshape((b_, l_ // c, c, h_) + t.shape[3:])
    return jnp.moveaxis(t, (1, 3), (0, 2))


def from_chunks(t):
    t = jnp.moveaxis(t, (0, 2), (1, 3))
    return t.reshape((t.shape[0], t.shape[1] * t.shape[2], t.shape[3]) + t.shape[4:])


def unblock(o):
    o = jnp.moveaxis(o, 0, 1)
    return o.reshape((o.shape[0], o.shape[1] * o.shape[2]) + o.shape[3:])


def hgrn2_mixer(cols, lb, o_gain):
    b_, l_ = cols.shape[:2]
    q, f, i, g = split_cols(cols, HG_SPLITS)
    q = split_heads(jax.nn.silu(q), GROUP_HEADS).astype(F32) * HEAD_DIM ** -0.5
    forget = lb + (1.0 - lb) * jax.nn.sigmoid(f.astype(F32))
    k = split_heads(1.0 - forget, GROUP_HEADS)
    logf = split_heads(jnp.log(forget), GROUP_HEADS)
    v = split_heads(i, GROUP_HEADS).astype(F32)
    tri = jnp.tril(jnp.ones((HG_CHUNK, HG_CHUNK), bool))[:, :, None]

    def step(state, inp):
        qc, kc, vc, gc = inp
        bcum = jnp.cumsum(gc, axis=2)
        o_inter = jnp.einsum('bhtk,bhkv->bhtv', qc * jnp.exp(bcum), state)
        diff = bcum[:, :, :, None, :] - bcum[:, :, None, :, :]
        decay = jnp.exp(jnp.where(tri, diff, -jnp.inf))
        attn = jnp.einsum('bhtk,bhsk,bhtsk->bhts', qc, kc, decay)
        out = o_inter + jnp.einsum('bhts,bhsv->bhtv', attn, vc)
        b_last = bcum[:, :, -1:, :]
        state = (jnp.exp(b_last[:, :, 0, :])[..., None] * state
                 + jnp.einsum('bhsk,bhsv->bhkv', kc * jnp.exp(b_last - bcum), vc))
        return state, out

    s0 = jnp.zeros((b_, GROUP_HEADS, HEAD_DIM, HEAD_DIM), F32)
    xs = (to_chunks(q, HG_CHUNK), to_chunks(k, HG_CHUNK), to_chunks(v, HG_CHUNK),
          to_chunks(logf, HG_CHUNK))
    _, o = lax.scan(step, s0, xs)
    o = rms_norm(from_chunks(o), o_gain).astype(cols.dtype)
    o = o * jax.nn.silu(split_heads(g, GROUP_HEADS))
    return o.reshape(b_, l_, GROUP_WIDTH)


def dsa_mixer(cols, pos, kv_gain, w_uk, w_uv, q_gain, k_gain, idxk_gain):
    b_, l_ = cols.shape[:2]
    q, ckv, iq, ik, iw = split_cols(cols, DSA_SPLITS)
    q = partial_rope(rms_norm(split_heads(q, GROUP_HEADS), q_gain), pos)
    ckv = rms_norm(ckv, kv_gain)
    k = partial_rope(rms_norm(ckv @ w_uk, k_gain)[:, :, None, :], pos)[:, :, 0]
    v = ckv @ w_uv
    iq = partial_rope(split_heads(iq, IDX_HEADS), pos)
    ik = partial_rope(rms_norm(ik, idxk_gain)[:, :, None, :], pos)[:, :, 0]
    iw = iw * (IDX_HEADS ** -0.5 * IDX_DIM ** -0.5)
    topk = min(DSA_TOPK_MAX, l_ // 4)
    key_pos = jnp.arange(l_)

    def block(bi):
        t0 = bi * Q_BLOCK
        qpos = t0 + jnp.arange(Q_BLOCK)
        qb = lax.dynamic_slice_in_dim(q, t0, Q_BLOCK, axis=1)
        iqb = lax.dynamic_slice_in_dim(iq, t0, Q_BLOCK, axis=1)
        iwb = lax.dynamic_slice_in_dim(iw, t0, Q_BLOCK, axis=1)
        rel = jax.nn.relu(jnp.einsum('bthd,bsd->bhts', iqb, ik))
        score = jnp.einsum('bhts,bth->bts', rel, iwb).astype(F32)
        causal = key_pos[None, :] <= qpos[:, None]
        score = jnp.where(causal[None], score, -jnp.inf)
        _, idx = lax.top_k(score, topk)
        k_sel = gather_rows(k, idx)
        v_sel = gather_rows(v, idx)
        s = jnp.einsum('bthd,btkd->bhtk', qb, k_sel) * HEAD_DIM ** -0.5
        p = masked_softmax(s, (idx <= qpos[None, :, None])[:, None]).astype(v.dtype)
        return jnp.einsum('bhtk,btkd->bthd', p, v_sel)

    o = unblock(lax.map(block, jnp.arange(l_ // Q_BLOCK)))
    return o.reshape(b_, l_, GROUP_WIDTH)


def nsa_mixer(cols, pos, pos_k, pos_v, k_w1, k_w2, v_w1, v_w2, q_gain, k_gains):
    b_, l_ = cols.shape[:2]
    scale = HEAD_DIM ** -0.5
    q, kc, vc, ks, vs, kw, vw, gates = split_cols(cols, NSA_SPLITS)
    q = rms_norm(split_heads(q, GROUP_HEADS), q_gain)
    q_rot = partial_rope(q, pos)
    gates = jax.nn.sigmoid(gates.reshape(b_, l_, 3, GROUP_HEADS, 1))

    n_cmp = (l_ - CMP_BLOCK) // CMP_STRIDE + 1
    cmp_idx = np.arange(n_cmp)[:, None] * CMP_STRIDE + np.arange(CMP_BLOCK)[None, :]

    def compress(t, pe, w1, w2):
        blk = (t[:, cmp_idx] + pe).reshape(b_, n_cmp, CMP_BLOCK * HEAD_DIM)
        return jax.nn.relu(blk @ w1) @ w2

    k_cmp = rms_norm(compress(kc, pos_k, k_w1, k_w2), k_gains[0])
    v_cmp = compress(vc, pos_v, v_w1, v_w2)
    cmp_vis = cmp_idx[:, -1][None, :] <= np.arange(l_)[:, None]
    p_cmp = masked_softmax(jnp.einsum('bthd,bjd->bhtj', q, k_cmp) * scale, cmp_vis)
    o_cmp = jnp.einsum('bhtj,bjd->bthd', p_cmp.astype(v_cmp.dtype), v_cmp)

    n_slc = l_ // SLC_BLOCK
    n_sel = min(SLC_TOPN, n_slc)
    st_c = np.arange(n_cmp) * CMP_STRIDE
    st_s = np.arange(n_slc) * SLC_BLOCK
    overlap = ((st_c[:, None] < st_s[None, :] + SLC_BLOCK)
               & (st_c[:, None] + CMP_BLOCK > st_s[None, :])).astype(np.float32)
    imp = jnp.einsum('bhtj,jn->btn', p_cmp, overlap)
    cur = np.arange(l_)[:, None] // SLC_BLOCK
    blk_id = np.arange(n_slc)[None, :]
    forced = (blk_id == 0) | (blk_id == cur) | (blk_id == cur - 1)
    imp = jnp.where(forced, jnp.inf, jnp.where(blk_id > cur, -jnp.inf, imp))
    _, sel = lax.top_k(imp, n_sel)

    k_s = partial_rope(rms_norm(ks, k_gains[1])[:, :, None], pos)[:, :, 0]
    k_blocks = k_s.reshape(b_, n_slc, SLC_BLOCK, HEAD_DIM)
    v_blocks = vs.reshape(b_, n_slc, SLC_BLOCK, HEAD_DIM)
    k_w = partial_rope(rms_norm(kw, k_gains[2])[:, :, None], pos)[:, :, 0]
    k_pad = jnp.pad(k_w, ((0, 0), (WINDOW, 0), (0, 0)))
    v_pad = jnp.pad(vw, ((0, 0), (WINDOW, 0), (0, 0)))

    def block(bi):
        t0 = bi * Q_BLOCK
        qpos = t0 + jnp.arange(Q_BLOCK)
        qb = lax.dynamic_slice_in_dim(q_rot, t0, Q_BLOCK, axis=1)
        sb = lax.dynamic_slice_in_dim(sel, t0, Q_BLOCK, axis=1)
        kb = gather_rows(k_blocks, sb)
        vb = gather_rows(v_blocks, sb)
        kpos = sb[..., None] * SLC_BLOCK + jnp.arange(SLC_BLOCK)
        s = jnp.einsum('bthd,btnsd->bhtns', qb, kb) * scale
        valid = (kpos <= qpos[None, :, None, None])[:, None]
        p = masked_softmax(s.reshape(b_, GROUP_HEADS, Q_BLOCK, n_sel * SLC_BLOCK),
                           valid.reshape(b_, 1, Q_BLOCK, n_sel * SLC_BLOCK))
        p = p.reshape(b_, GROUP_HEADS, Q_BLOCK, n_sel, SLC_BLOCK).astype(vb.dtype)
        o_slc = jnp.einsum('bhtns,btnsd->bthd', p, vb)
        kwb = lax.dynamic_slice_in_dim(k_pad, t0, WINDOW + Q_BLOCK, axis=1)
        vwb = lax.dynamic_slice_in_dim(v_pad, t0, WINDOW + Q_BLOCK, axis=1)
        wpos = t0 - WINDOW + jnp.arange(WINDOW + Q_BLOCK)
        dist = qpos[:, None] - wpos[None, :]
        wvalid = (dist >= 0) & (dist < WINDOW) & (wpos[None, :] >= 0)
        sw = jnp.einsum('bthd,bsd->bhts', qb, kwb) * scale
        pw = masked_softmax(sw, wvalid[None, None]).astype(vwb.dtype)
        o_swa = jnp.einsum('bhts,bsd->bthd', pw, vwb)
        return o_slc, o_swa

    o_slc, o_swa = lax.map(block, jnp.arange(l_ // Q_BLOCK))
    o = gates[:, :, 0] * o_cmp + gates[:, :, 1] * unblock(o_slc) + gates[:, :, 2] * unblock(o_swa)
    return o.reshape(b_, l_, GROUP_WIDTH)


def mlstm_mixer(cols, conv_w, conv_b, i_bias, f_bias, o_gain):
    b_, l_ = cols.shape[:2]
    qk, v, og, ig, fg = split_cols(cols, ML_SPLITS)
    qk = jax.nn.silu(causal_conv(qk, conv_w, conv_b))
    q, k = jnp.split(qk, 2, axis=-1)
    q = split_heads(q, GROUP_HEADS).astype(F32)
    k = split_heads(k, GROUP_HEADS).astype(F32) * HEAD_DIM ** -0.5
    v = split_heads(v, GROUP_HEADS).astype(F32)
    log_i = (ig + i_bias).astype(F32)
    log_f = jax.nn.log_sigmoid((fg + f_bias).astype(F32))
    tri = jnp.tril(jnp.ones((ML_CHUNK, ML_CHUNK), bool))

    def step(carry, inp):
        cmat, nvec, m = carry
        qc, kc, vc, li, lf = inp
        bcum = jnp.cumsum(lf, axis=-1)
        log_d = jnp.where(tri, bcum[..., :, None] - bcum[..., None, :] + li[..., None, :], -jnp.inf)
        inter = bcum + m[..., None]
        m_t = jnp.maximum(inter, jnp.max(log_d, axis=-1))
        d_mat = jnp.exp(log_d - m_t[..., None])
        w_inter = jnp.exp(inter - m_t)
        s = jnp.einsum('bhtd,bhsd->bhts', qc, kc) * d_mat
        num = (w_inter[..., None] * jnp.einsum('bhtd,bhdv->bhtv', qc, cmat)
               + jnp.einsum('bhts,bhsv->bhtv', s, vc))
        den = w_inter * jnp.einsum('bhtd,bhd->bht', qc, nvec) + jnp.sum(s, axis=-1)
        h = num / jnp.maximum(jnp.abs(den), jnp.exp(-m_t))[..., None]
        b_last = bcum[..., -1]
        log_w = b_last[..., None] - bcum + li
        m_new = jnp.maximum(b_last + m, jnp.max(log_w, axis=-1))
        w_s = jnp.exp(log_w - m_new[..., None])
        decay = jnp.exp(b_last + m - m_new)
        cmat = decay[..., None, None] * cmat + jnp.einsum('bhs,bhsd,bhsv->bhdv', w_s, kc, vc)
        nvec = decay[..., None] * nvec + jnp.einsum('bhs,bhsd->bhd', w_s, kc)
        return (cmat, nvec, m_new), h

    init = (jnp.zeros((b_, GROUP_HEADS, HEAD_DIM, HEAD_DIM), F32),
            jnp.zeros((b_, GROUP_HEADS, HEAD_DIM), F32),
            jnp.full((b_, GROUP_HEADS), -1e30, F32))
    xs = (to_chunks(q, ML_CHUNK), to_chunks(k, ML_CHUNK), to_chunks(v, ML_CHUNK),
          to_chunks(log_i, ML_CHUNK), to_chunks(log_f, ML_CHUNK))
    _, h = lax.scan(step, init, xs)
    h = rms_norm(from_chunks(h), o_gain).astype(cols.dtype)
    h = h * jax.nn.sigmoid(split_heads(og, GROUP_HEADS))
    return h.reshape(b_, l_, GROUP_WIDTH)


def kernel(x, mem, lb_param, norm_mix, w_in, w_out, hg_o_gain, dsa_kv_gain, dsa_w_uk, dsa_w_uv,
           dsa_q_gain, dsa_k_gain, dsa_idxk_gain, nsa_pos_k, nsa_pos_v, nsa_k_w1, nsa_k_w2,
           nsa_v_w1, nsa_v_w2, nsa_q_gain, nsa_k_gains, ml_conv_w, ml_conv_b, ml_i_bias,
           ml_f_bias, ml_o_gain, norm_xa, norm_mem, xa_wq, xa_wkv, xa_wo, xa_q_gain, xa_k_gain,
           norm_ffn, ffn_w13, ffn_w2):
    b_, l_, d = x.shape
    pos = jnp.arange(l_)
    lb_all = jnp.cumsum(jax.nn.softmax(lb_param.astype(F32), axis=0), axis=0)
    lb_all = lb_all - lb_all[:1]
    x2d = x.reshape(b_ * l_, d)
    mem2d = mem.reshape(b_ * N_MEM, d)
    rope = rope_tables(l_)
    for l in range(DEPTH):
        cols = norm_matmul(x2d, norm_mix[l], pack_w_in(w_in[l]).astype(BF16), tm=256)
        c3 = cols.reshape(b_, l_, IN_COLS_PACKED)
        seg = lambda a, n: c3[:, :, a:a + n]
        c_hg = seg(C_HG, 1024)
        c_dsa = jnp.concatenate([seg(C_DSA_Q, 256), seg(C_DSA_CKV, 128), seg(C_DSA_IQ, 256),
                                 seg(C_DSA_IK, 32), seg(C_DSA_IW, 8)], axis=-1)
        c_ml = jnp.concatenate([seg(C_ML_QK, 512), seg(C_ML_V, 256), seg(C_ML_OG, 256),
                                seg(C_ML_GATE, 8)], axis=-1)
        mixed = jnp.concatenate([
            hgrn2_mixer(c_hg, lb_all[l], hg_o_gain[l]),
            dsa_mixer(c_dsa, pos, dsa_kv_gain[l], dsa_w_uk[l], dsa_w_uv[l],
                      dsa_q_gain[l], dsa_k_gain[l], dsa_idxk_gain[l]),
            nsa_mixer_pallas(cols, b_, l_, rope, nsa_pos_k[l], nsa_pos_v[l], nsa_k_w1[l],
                             nsa_k_w2[l], nsa_v_w1[l], nsa_v_w2[l], nsa_q_gain[l],
                             nsa_k_gains[l]).reshape(b_, l_, GROUP_WIDTH),
            mlstm_mixer(c_ml, ml_conv_w[l], ml_conv_b[l], ml_i_bias[l], ml_f_bias[l], ml_o_gain[l]),
        ], axis=-1)
        k_mem, v_mem = mem_kv(mem2d, norm_mem[l], xa_wkv[l].astype(BF16), xa_k_gain[l])
        x2d = post_block(x2d, mixed.reshape(b_ * l_, -1), l_, w_out[l].astype(BF16), norm_xa[l],
                         xa_wq[l].astype(BF16), xa_q_gain[l], k_mem, v_mem,
                         xa_wo[l].astype(BF16), norm_ffn[l], ffn_w13[l].astype(BF16),
                         ffn_w2[l].astype(BF16), tm=256)
    return x2d.reshape(b_, l_, d)
```

```python
import functools

import jax
import jax.numpy as jnp
from jax import lax
import numpy as np
from jax.experimental import pallas as pl
from jax.experimental.pallas import tpu as pltpu

F32 = jnp.float32
BF16 = jnp.bfloat16

D_MODEL = 1024
DEPTH = 2
HEAD_DIM = 64
GROUP_HEADS = 4
GROUP_WIDTH = GROUP_HEADS * HEAD_DIM
ROPE_THETA = 500000.0
EPS = 1e-6
Q_BLOCK = 128
N_MEM = 256
XA_HEADS = 4
XA_WIDTH = XA_HEADS * HEAD_DIM
HG_CHUNK = 64
DSA_LATENT = 128
IDX_HEADS = 8
IDX_DIM = 32
DSA_TOPK_MAX = 256
CMP_BLOCK = 32
CMP_STRIDE = 16
SLC_BLOCK = 64
SLC_SHIFT = 6
SLC_TOPN = 16
WINDOW = 512
ML_CHUNK = 64
CONV_WIDTH = 4
D_FF = 2816

HG_SPLITS = (GROUP_WIDTH,) * 4
DSA_SPLITS = (GROUP_WIDTH, DSA_LATENT, IDX_HEADS * IDX_DIM, IDX_DIM, IDX_HEADS)
NSA_SPLITS = (GROUP_WIDTH,) + (HEAD_DIM,) * 6 + (3 * GROUP_HEADS,)
ML_SPLITS = (2 * GROUP_WIDTH, GROUP_WIDTH, GROUP_WIDTH, GROUP_HEADS, GROUP_HEADS)
GROUP_COLS = (sum(HG_SPLITS), sum(DSA_SPLITS), sum(NSA_SPLITS), sum(ML_SPLITS))
IN_COLS = sum(GROUP_COLS)

V7X_VMEM_LIMIT_BYTES = 56 * 1024 * 1024
LANES = 128
FF_CHUNK = 256


def _round_up(n, m):
    return -(-n // m) * m


def _rms_rows(t, g):
    return t * lax.rsqrt(jnp.mean(t * t, axis=-1, keepdims=True) + EPS) * g


def _const_spec(shape):
    return pl.BlockSpec(shape, lambda *_: (0,) * len(shape), pipeline_mode=pl.Buffered(1))


def _norm_matmul_kernel(x_ref, g_ref, w_ref, o_ref):
    h = _rms_rows(x_ref[...], g_ref[...]).astype(BF16)
    o_ref[...] = jnp.dot(h, w_ref[...], preferred_element_type=F32)


def norm_matmul(x2d, gain, w_bf16, tm):
    m, k = x2d.shape
    n = w_bf16.shape[1]
    return pl.pallas_call(
        _norm_matmul_kernel,
        out_shape=jax.ShapeDtypeStruct((m, n), F32),
        grid=(m // tm,),
        in_specs=[
            pl.BlockSpec((tm, k), lambda i: (i, 0)),
            _const_spec((1, k)),
            _const_spec((k, n)),
        ],
        out_specs=pl.BlockSpec((tm, n), lambda i: (i, 0)),
        compiler_params=pltpu.CompilerParams(
            dimension_semantics=("arbitrary",), vmem_limit_bytes=V7X_VMEM_LIMIT_BYTES),
        name="norm_matmul",
    )(x2d, gain.reshape(1, k), w_bf16)


def _mem_kv_kernel(m_ref, g_ref, w_ref, kg_ref, k_ref, v_ref):
    mn = _rms_rows(m_ref[...], g_ref[...]).astype(BF16)
    kv = jnp.dot(mn, w_ref[...], preferred_element_type=F32)
    kg = kg_ref[...]
    ks = []
    for h in range(XA_HEADS):
        kh = kv[:, h * HEAD_DIM:(h + 1) * HEAD_DIM]
        ks.append(_rms_rows(kh, kg))
    k_ref[...] = jnp.concatenate(ks, axis=-1).astype(BF16)
    v_ref[...] = kv[:, XA_WIDTH:].astype(BF16)


def mem_kv(mem2d, gain, wkv_bf16, k_gain):
    m, k = mem2d.shape
    return pl.pallas_call(
        _mem_kv_kernel,
        out_shape=(jax.ShapeDtypeStruct((m, XA_WIDTH), BF16),
                   jax.ShapeDtypeStruct((m, XA_WIDTH), BF16)),
        grid=(m // N_MEM,),
        in_specs=[
            pl.BlockSpec((N_MEM, k), lambda i: (i, 0)),
            _const_spec((1, k)),
            _const_spec((k, 2 * XA_WIDTH)),
            _const_spec((1, HEAD_DIM)),
        ],
        out_specs=(pl.BlockSpec((N_MEM, XA_WIDTH), lambda i: (i, 0)),
                   pl.BlockSpec((N_MEM, XA_WIDTH), lambda i: (i, 0))),
        compiler_params=pltpu.CompilerParams(
            dimension_semantics=("arbitrary",), vmem_limit_bytes=V7X_VMEM_LIMIT_BYTES),
        name="mem_kv",
    )(mem2d, gain.reshape(1, k), wkv_bf16, k_gain.reshape(1, HEAD_DIM))


def _post_kernel(x_ref, mix_ref, wout_ref, gxa_ref, wq_ref, qg_ref, k_ref, v_ref, wo_ref,
                 gffn_ref, w13_ref, w2_ref, o_ref):
    x = x_ref[...] + jnp.dot(mix_ref[...].astype(BF16), wout_ref[...],
                             preferred_element_type=F32)
    h = _rms_rows(x, gxa_ref[...]).astype(BF16)
    q = jnp.dot(h, wq_ref[...], preferred_element_type=F32)
    qg = qg_ref[...] * (HEAD_DIM ** -0.5)
    k = k_ref[...]
    v = v_ref[...]
    outs = []
    for hd in range(XA_HEADS):
        sl = slice(hd * HEAD_DIM, (hd + 1) * HEAD_DIM)
        qh = _rms_rows(q[:, sl], qg).astype(BF16)
        s = lax.dot_general(qh, k[:, sl], (((1,), (1,)), ((), ())),
                            preferred_element_type=F32)
        e = jnp.exp(s - jnp.max(s, axis=-1, keepdims=True))
        p = e / jnp.sum(e, axis=-1, keepdims=True)
        outs.append(jnp.dot(p.astype(BF16), v[:, sl], preferred_element_type=F32))
    o = jnp.concatenate(outs, axis=-1).astype(BF16)
    x = x + jnp.dot(o, wo_ref[...], preferred_element_type=F32)
    h = _rms_rows(x, gffn_ref[...]).astype(BF16)
    acc = x
    for c in range(D_FF // FF_CHUNK):
        a = jnp.dot(h, w13_ref[:, c * FF_CHUNK:(c + 1) * FF_CHUNK], preferred_element_type=F32)
        b = jnp.dot(h, w13_ref[:, D_FF + c * FF_CHUNK:D_FF + (c + 1) * FF_CHUNK],
                    preferred_element_type=F32)
        act = (a * jax.nn.sigmoid(a) * b).astype(BF16)
        acc = acc + jnp.dot(act, w2_ref[c * FF_CHUNK:(c + 1) * FF_CHUNK, :],
                            preferred_element_type=F32)
    o_ref[...] = acc


def post_block(x2d, mixed2d, seq, wout, gxa, wq, qg, k_mem, v_mem, wo, gffn, w13, w2, tm):
    m, d = x2d.shape
    tiles_per_batch = seq // tm
    row = lambda i: (i, 0)
    mem_row = lambda i: (i // tiles_per_batch, 0)
    return pl.pallas_call(
        _post_kernel,
        out_shape=jax.ShapeDtypeStruct((m, d), F32),
        grid=(m // tm,),
        in_specs=[
            pl.BlockSpec((tm, d), row),
            pl.BlockSpec((tm, mixed2d.shape[1]), row),
            _const_spec(wout.shape),
            _const_spec((1, d)),
            _const_spec(wq.shape),
            _const_spec((1, HEAD_DIM)),
            pl.BlockSpec((N_MEM, XA_WIDTH), mem_row),
            pl.BlockSpec((N_MEM, XA_WIDTH), mem_row),
            _const_spec(wo.shape),
            _const_spec((1, d)),
            _const_spec(w13.shape),
            _const_spec(w2.shape),
        ],
        out_specs=pl.BlockSpec((tm, d), row),
        compiler_params=pltpu.CompilerParams(
            dimension_semantics=("arbitrary",), vmem_limit_bytes=V7X_VMEM_LIMIT_BYTES),
        name="post_block",
    )(x2d, mixed2d, wout, gxa.reshape(1, d), wq, qg.reshape(1, HEAD_DIM), k_mem, v_mem, wo,
      gffn.reshape(1, d), w13, w2)


C_HG = 0
C_DSA_Q = 1024
C_DSA_IQ = 1280
C_DSA_CKV = 1536
C_DSA_IK = 1664
C_DSA_IW = 1792
C_NSA_GATE = 1920
C_NSA_Q = 2048
C_NSA_CMP = 2304
C_NSA_SLC = 2432
C_NSA_WIN = 2560
C_ML_GATE = 2688
C_ML_QK = 2816
C_ML_V = 3328
C_ML_OG = 3584
IN_COLS_PACKED = 3840


def pack_w_in(w):
    dsa0 = GROUP_COLS[0]
    nsa0 = dsa0 + GROUP_COLS[1]
    ml0 = nsa0 + GROUP_COLS[2]

    def seg(a, n, width=None):
        s = w[:, a:a + n]
        if width is not None and width > n:
            s = jnp.pad(s, ((0, 0), (0, width - n)))
        return s

    parts = [
        seg(0, 1024),
        seg(dsa0, 256), seg(dsa0 + 384, 256), seg(dsa0 + 256, 128),
        seg(dsa0 + 640, 32, LANES), seg(dsa0 + 672, 8, LANES),
        seg(nsa0 + 640, 12, LANES), seg(nsa0, 256), seg(nsa0 + 256, 128),
        seg(nsa0 + 384, 128), seg(nsa0 + 512, 128),
        seg(ml0 + 1024, 8, LANES), seg(ml0, 512), seg(ml0 + 512, 256), seg(ml0 + 768, 256),
    ]
    return jnp.concatenate(parts, axis=1)


def rope_tables(l_, d=HEAD_DIM):
    rd = d // 4
    half = rd // 2
    inv = ROPE_THETA ** (-jnp.arange(half, dtype=F32) * 2.0 / rd)
    ang = jnp.arange(l_).astype(F32)[:, None] * inv[None, :]
    cos, sin = jnp.cos(ang), jnp.sin(ang)
    zh = jnp.zeros((l_, half), F32)
    rest0 = jnp.zeros((l_, d - rd), F32)
    c = jnp.concatenate([cos, cos, rest0 + 1.0], axis=1)
    s1 = jnp.concatenate([-sin, zh, rest0], axis=1)
    s2 = jnp.concatenate([zh, sin, rest0], axis=1)
    return jnp.stack([c, s1, s2])


def rope_tables_pad(rt, width):
    n = width - rt.shape[-1]
    ident = jnp.stack([jnp.ones(rt.shape[1:2] + (n,), F32), jnp.zeros(rt.shape[1:2] + (n,), F32),
                       jnp.zeros(rt.shape[1:2] + (n,), F32)])
    return jnp.concatenate([rt, ident], axis=-1)


def rope_tables_kv_pair(rt):
    return rope_tables_pad(rt, 2 * rt.shape[-1])


def _apply_rope(t, rope_ref, half=HEAD_DIM // 8):
    w = t.shape[-1]
    return (t * rope_ref[0] + pltpu.roll(t, w - half, 1) * rope_ref[1]
            + pltpu.roll(t, half, 1) * rope_ref[2])


_NT = (((1,), (1,)), ((), ()))


def _dot_nt(a, b):
    return lax.dot_general(a, b, _NT, preferred_element_type=F32)


def _dot(a, b):
    return jnp.dot(a, b, preferred_element_type=F32)


NEG_BIG = -(2.0 ** 30)
NSA_KEY_TILE = 256
NSA_TQ = 128
NSA_SEL_TQ = 256
NSA_PREP_TK = 512


def _nsa_kv_prep_kernel(ps_ref, pw_ref, rope_ref, gs_ref, gw_ref, kaug_ref, vs_ref, kw_ref, vw_ref):
    tk = ps_ref.shape[0]
    lane = lax.broadcasted_iota(jnp.int32, (tk, LANES), 1)
    is_k = lane < HEAD_DIM

    def norm_rope(p, g):
        ms = jnp.sum(jnp.where(is_k, p * p, 0.0), axis=-1, keepdims=True) * (1.0 / HEAD_DIM)
        y = jnp.where(is_k, p * lax.rsqrt(ms + EPS) * g, p)
        return _apply_rope(y, rope_ref)

    ys = norm_rope(ps_ref[...], gs_ref[...])
    yw = norm_rope(pw_ref[...], gw_ref[...])
    row = pl.program_id(1) * tk + lax.broadcasted_iota(jnp.int32, (tk, LANES), 0)
    ind = jnp.where(jnp.right_shift(row, SLC_SHIFT) == (lane - HEAD_DIM), 1.0, 0.0)
    kaug_ref[0] = jnp.where(is_k, ys, ind).astype(BF16)
    vs_ref[0] = ys[:, HEAD_DIM:].astype(BF16)
    kw_ref[0] = yw[:, :HEAD_DIM].astype(BF16)
    vw_ref[0] = yw[:, HEAD_DIM:].astype(BF16)


def nsa_kv_prep(cols, b_, l_, rope_pair, g_slc, g_win):
    tk = NSA_PREP_TK
    nt = l_ // tk
    ones = jnp.ones((HEAD_DIM,), F32)
    gs = jnp.concatenate([g_slc, ones]).reshape(1, LANES)
    gw = jnp.concatenate([g_win, ones]).reshape(1, LANES)
    kv = lambda w: jax.ShapeDtypeStruct((b_, l_, w), BF16)
    out_blk = lambda w: pl.BlockSpec((1, tk, w), lambda b, i: (b, i, 0))
    return pl.pallas_call(
        _nsa_kv_prep_kernel,
        out_shape=(kv(LANES), kv(HEAD_DIM), kv(HEAD_DIM), kv(HEAD_DIM)),
        grid=(b_, nt),
        in_specs=[
            pl.BlockSpec((tk, LANES), lambda b, i: (b * nt + i, C_NSA_SLC // LANES)),
            pl.BlockSpec((tk, LANES), lambda b, i: (b * nt + i, C_NSA_WIN // LANES)),
            pl.BlockSpec((3, tk, LANES), lambda b, i: (0, i, 0)),
            pl.BlockSpec((1, LANES), lambda b, i: (0, 0)),
            pl.BlockSpec((1, LANES), lambda b, i: (0, 0)),
        ],
        out_specs=(out_blk(LANES), out_blk(HEAD_DIM), out_blk(HEAD_DIM), out_blk(HEAD_DIM)),
        compiler_params=pltpu.CompilerParams(
            dimension_semantics=("arbitrary", "arbitrary"),
            vmem_limit_bytes=V7X_VMEM_LIMIT_BYTES),
        name="nsa_kv_prep",
    )(cols, cols, rope_pair, gs, gw)


def _nsa_compress_kernel(r_ref, pea_ref, peb_ref, w1a_ref, w1b_ref, w2k_ref, w2v_ref, kg_ref,
                         kc_ref, vc_ref):
    r = r_ref[0]
    n = r.shape[0]
    a = _dot((r + pea_ref[...]).astype(BF16), w1a_ref[...])
    bm = _dot((r + peb_ref[...]).astype(BF16), w1b_ref[...])
    row = lax.broadcasted_iota(jnp.int32, bm.shape, 0)
    bm_up = jnp.where(row < n - 1, pltpu.roll(bm, n - 1, 0), 0.0)
    h = jnp.maximum(a + bm_up, 0.0).astype(BF16)
    hid = w2k_ref.shape[0]
    ck = _dot(h[:, :hid], w2k_ref[...])
    cv = _dot(h[:, hid:], w2v_ref[...])
    kc_ref[0] = _rms_rows(ck, kg_ref[...]).astype(BF16)
    vc_ref[0] = cv.astype(BF16)


def nsa_compress(cols, b_, l_, pos_k, pos_v, k_w1, k_w2, v_w1, v_w2, k_gain):
    rows = l_ // CMP_STRIDE
    hid = k_w1.shape[1]
    pair = cols[:, C_NSA_CMP:C_NSA_CMP + LANES].reshape(b_, rows, CMP_STRIDE * LANES)

    def interleave_pe(lo):
        pe = jnp.concatenate([pos_k[lo:lo + CMP_STRIDE], pos_v[lo:lo + CMP_STRIDE]], axis=1)
        return pe.reshape(1, CMP_STRIDE * LANES)

    def interleave_w(lo):
        wk = k_w1[lo * HEAD_DIM:(lo + CMP_STRIDE) * HEAD_DIM].reshape(CMP_STRIDE, HEAD_DIM, hid)
        wv = v_w1[lo * HEAD_DIM:(lo + CMP_STRIDE) * HEAD_DIM].reshape(CMP_STRIDE, HEAD_DIM, hid)
        z = jnp.zeros_like(wk)
        top = jnp.concatenate([wk, z], axis=2)
        bot = jnp.concatenate([z, wv], axis=2)
        return jnp.concatenate([top, bot], axis=1).reshape(CMP_STRIDE * LANES, 2 * hid).astype(BF16)

    out = jax.ShapeDtypeStruct((b_, rows, HEAD_DIM), BF16)
    return pl.pallas_call(
        _nsa_compress_kernel,
        out_shape=(out, out),
        grid=(b_,),
        in_specs=[
            pl.BlockSpec((1, rows, CMP_STRIDE * LANES), lambda b: (b, 0, 0)),
            _const_spec((1, CMP_STRIDE * LANES)),
            _const_spec((1, CMP_STRIDE * LANES)),
            _const_spec((CMP_STRIDE * LANES, 2 * hid)),
            _const_spec((CMP_STRIDE * LANES, 2 * hid)),
            _const_spec((hid, HEAD_DIM)),
            _const_spec((hid, HEAD_DIM)),
            _const_spec((1, HEAD_DIM)),
        ],
        out_specs=(pl.BlockSpec((1, rows, HEAD_DIM), lambda b: (b, 0, 0)),
                   pl.BlockSpec((1, rows, HEAD_DIM), lambda b: (b, 0, 0))),
        compiler_params=pltpu.CompilerParams(
            dimension_semantics=("arbitrary",), vmem_limit_bytes=V7X_VMEM_LIMIT_BYTES),
        name="nsa_compress",
    )(pair, interleave_pe(0), interleave_pe(CMP_STRIDE), interleave_w(0),
      interleave_w(CMP_STRIDE), k_w2.astype(BF16), v_w2.astype(BF16),
      k_gain.reshape(1, HEAD_DIM))


def _masked_softmax(s, valid, axis):
    m = jnp.max(jnp.where(valid, s, -jnp.inf), axis=axis, keepdims=True)
    m = jnp.where(m == -jnp.inf, 0.0, m)
    e = jnp.where(valid, jnp.exp(s - m), 0.0)
    den = jnp.sum(e, axis=axis, keepdims=True)
    return e / jnp.where(den > 0, den, 1.0)


def _nsa_select_kernel(q_ref, kc_ref, vc_ref, ovt_ref, rope_ref, qg_ref, qaug_ref, ocmp_ref,
                       imp_ref, *, n_sel):
    tq = q_ref.shape[0]
    ncr = kc_ref.shape[1]
    n_slc = ovt_ref.shape[0]
    t0 = pl.program_id(1) * tq
    scale = HEAD_DIM ** -0.5
    q = q_ref[...]
    g = qg_ref[...]
    kc = kc_ref[0]
    vc = vc_ref[0]
    last = CMP_BLOCK - 1
    vis = (lax.broadcasted_iota(jnp.int32, (tq, ncr), 1) * CMP_STRIDE + last
           <= t0 + lax.broadcasted_iota(jnp.int32, (tq, ncr), 0))
    vis_t = (lax.broadcasted_iota(jnp.int32, (ncr, tq), 0) * CMP_STRIDE + last
             <= t0 + lax.broadcasted_iota(jnp.int32, (ncr, tq), 1))
    qn_heads, o_heads = [], []
    psum_t = jnp.zeros((ncr, tq), F32)
    for h in range(GROUP_HEADS):
        qn = _rms_rows(q[:, h * HEAD_DIM:(h + 1) * HEAD_DIM], g)
        qn_heads.append(qn)
        qb = (qn * scale).astype(BF16)
        p = _masked_softmax(_dot_nt(qb, kc), vis, 1)
        o_heads.append(_dot(p.astype(BF16), vc))
        psum_t = psum_t + _masked_softmax(_dot_nt(kc, qb), vis_t, 0)
    ocmp_ref[0] = jnp.concatenate(o_heads, axis=-1)

    hi = psum_t.astype(BF16)
    lo = (psum_t - hi.astype(F32)).astype(BF16)
    imp = _dot(ovt_ref[...], hi) + _dot(ovt_ref[...], lo)
    blk = lax.broadcasted_iota(jnp.int32, (n_slc, tq), 0)
    cur = jnp.right_shift(t0 + lax.broadcasted_iota(jnp.int32, (n_slc, tq), 1), SLC_SHIFT)
    forced = (blk == 0) | (blk == cur) | (blk == cur - 1)
    imp = jnp.where(forced, jnp.inf, jnp.where(blk > cur, -jnp.inf, imp))
    imp_ref[...] = imp

    def count_beats(m, cnt):
        row = imp_ref[pl.ds(m, 1), :]
        beats = (row > imp) | ((row == imp) & (blk > m))
        return cnt + jnp.where(beats, 1.0, 0.0)

    cnt = lax.fori_loop(0, n_slc, count_beats, jnp.zeros((n_slc, tq), F32))
    mt = jnp.where(cnt < n_sel, 0.0, NEG_BIG)
    pad = jnp.zeros((HEAD_DIM - n_slc, tq), F32)
    mt = jnp.concatenate([mt, pad, mt, pad], axis=0) if n_slc < HEAD_DIM else jnp.concatenate(
        [mt, mt], axis=0)
    mt = mt.T

    qr = _apply_rope(jnp.concatenate(qn_heads, axis=-1), rope_ref) * scale
    lane = lax.broadcasted_iota(jnp.int32, (tq, LANES), 1)
    for j in range(GROUP_HEADS // 2):
        pair = qr[:, j * LANES:(j + 1) * LANES]
        swapped = pltpu.roll(pair, HEAD_DIM, 1)
        qaug_ref[0, :, (2 * j) * LANES:(2 * j + 1) * LANES] = jnp.where(
            lane < HEAD_DIM, pair, mt).astype(BF16)
        qaug_ref[0, :, (2 * j + 1) * LANES:(2 * j + 2) * LANES] = jnp.where(
            lane < HEAD_DIM, swapped, mt).astype(BF16)


def nsa_select(cols, b_, l_, k_cmp, v_cmp, rope_q, q_gain):
    tq = min(NSA_SEL_TQ, l_)
    nt = l_ // tq
    ncr = l_ // CMP_STRIDE
    n_slc = l_ // SLC_BLOCK
    n_sel = min(SLC_TOPN, n_slc)
    st_c = np.arange(ncr) * CMP_STRIDE
    st_s = np.arange(n_slc) * SLC_BLOCK
    ovt = ((st_c[None, :] < st_s[:, None] + SLC_BLOCK)
           & (st_c[None, :] + CMP_BLOCK > st_s[:, None])).astype(np.float32)
    return pl.pallas_call(
        functools.partial(_nsa_select_kernel, n_sel=n_sel),
        out_shape=(jax.ShapeDtypeStruct((b_, l_, GROUP_HEADS * LANES), BF16),
                   jax.ShapeDtypeStruct((b_, l_, GROUP_WIDTH), F32)),
        grid=(b_, nt),
        in_specs=[
            pl.BlockSpec((tq, GROUP_WIDTH), lambda b, i: (b * nt + i, C_NSA_Q // GROUP_WIDTH)),
            pl.BlockSpec((1, ncr, HEAD_DIM), lambda b, i: (b, 0, 0)),
            pl.BlockSpec((1, ncr, HEAD_DIM), lambda b, i: (b, 0, 0)),
            pl.BlockSpec((n_slc, ncr), lambda b, i: (0, 0)),
            pl.BlockSpec((3, tq, GROUP_WIDTH), lambda b, i: (0, i, 0)),
            pl.BlockSpec((1, HEAD_DIM), lambda b, i: (0, 0)),
        ],
        out_specs=(pl.BlockSpec((1, tq, GROUP_HEADS * LANES), lambda b, i: (b, i, 0)),
                   pl.BlockSpec((1, tq, GROUP_WIDTH), lambda b, i: (b, i, 0))),
        scratch_shapes=[pltpu.VMEM((n_slc, tq), F32)],
        compiler_params=pltpu.CompilerParams(
            dimension_semantics=("arbitrary", "arbitrary"),
            vmem_limit_bytes=V7X_VMEM_LIMIT_BYTES),
        name="nsa_select",
    )(cols, k_cmp, v_cmp, jnp.asarray(ovt, BF16), rope_q, q_gain.reshape(1, HEAD_DIM))


def _nsa_attend_kernel(qaug_ref, kaug_ref, vs_ref, kw_ref, vw_ref, ocmp_ref, gate_ref, o_ref):
    tq = qaug_ref.shape[1]
    nh = GROUP_HEADS
    ts = NSA_KEY_TILE
    t0 = pl.program_id(1) * tq
    qa = qaug_ref[0]
    qs = jnp.concatenate([qa[:, h * LANES:(h + 1) * LANES] for h in range(nh)], axis=0)
    qpos = t0 + lax.rem(lax.broadcasted_iota(jnp.int32, (nh * tq, 1), 0), tq)

    def key_tile(kt, carry):
        m, l, acc = carry
        k0 = pl.multiple_of(kt * ts, ts)
        s = _dot_nt(qs, kaug_ref[0, pl.ds(k0, ts), :])
        kpos = k0 + lax.broadcasted_iota(jnp.int32, (nh * tq, ts), 1)
        s = jnp.where(kpos <= qpos, s, NEG_BIG)
        m_new = jnp.maximum(m, jnp.max(s, axis=-1, keepdims=True))
        alpha = jnp.exp(m - m_new)
        p = jnp.exp(s - m_new)
        l = alpha * l + jnp.sum(p, axis=-1, keepdims=True)
        acc = alpha * acc + _dot(p.astype(BF16), vs_ref[0, pl.ds(k0, ts), :])
        return m_new, l, acc

    n_kt = (t0 + tq + ts - 1) // ts
    init = (jnp.full((nh * tq, 1), -jnp.inf, F32), jnp.zeros((nh * tq, 1), F32),
            jnp.zeros((nh * tq, HEAD_DIM), F32))
    _, l, acc = lax.fori_loop(0, n_kt, key_tile, init)
    o_slc = acc / l

    wlen = WINDOW + tq
    start = pl.multiple_of(jnp.maximum(t0 - WINDOW, 0), tq)
    sw = _dot_nt(qs[:, :HEAD_DIM], kw_ref[0, pl.ds(start, wlen), :])
    dist = qpos - (start + lax.broadcasted_iota(jnp.int32, (nh * tq, wlen), 1))
    valid = (dist >= 0) & (dist < WINDOW)
    sw = jnp.where(valid, sw, -jnp.inf)
    e = jnp.exp(sw - jnp.max(sw, axis=-1, keepdims=True))
    o_swa = _dot(e.astype(BF16), vw_ref[0, pl.ds(start, wlen), :]) / jnp.sum(
        e, axis=-1, keepdims=True)

    g = jax.nn.sigmoid(gate_ref[...])
    oc = ocmp_ref[0]
    outs = []
    for h in range(nh):
        rows = slice(h * tq, (h + 1) * tq)
        outs.append(g[:, h:h + 1] * oc[:, h * HEAD_DIM:(h + 1) * HEAD_DIM]
                    + g[:, nh + h:nh + h + 1] * o_slc[rows]
                    + g[:, 2 * nh + h:2 * nh + h + 1] * o_swa[rows])
    o_ref[...] = jnp.concatenate(outs, axis=-1)


def nsa_attend(cols, b_, l_, q_aug, k_aug, v_slc, k_win, v_win, o_cmp):
    tq = NSA_TQ
    nt = l_ // tq
    seq = lambda w: pl.BlockSpec((1, l_, w), lambda b, i: (b, 0, 0))
    return pl.pallas_call(
        _nsa_attend_kernel,
        out_shape=jax.ShapeDtypeStruct((b_ * l_, GROUP_WIDTH), F32),
        grid=(b_, nt),
        in_specs=[
            pl.BlockSpec((1, tq, GROUP_HEADS * LANES), lambda b, i: (b, i, 0)),
            seq(LANES), seq(HEAD_DIM), seq(HEAD_DIM), seq(HEAD_DIM),
            pl.BlockSpec((1, tq, GROUP_WIDTH), lambda b, i: (b, i, 0)),
            pl.BlockSpec((tq, LANES), lambda b, i: (b * nt + i, C_NSA_GATE // LANES)),
        ],
        out_specs=pl.BlockSpec((tq, GROUP_WIDTH), lambda b, i: (b * nt + i, 0)),
        compiler_params=pltpu.CompilerParams(
            dimension_semantics=("arbitrary", "arbitrary"),
            vmem_limit_bytes=V7X_VMEM_LIMIT_BYTES),
        name="nsa_attend",
    )(q_aug, k_aug, v_slc, k_win, v_win, o_cmp, cols)


def nsa_mixer_pallas(cols, b_, l_, rope, pos_k, pos_v, k_w1, k_w2, v_w1, v_w2, q_gain, k_gains):
    k_aug, v_slc, k_win, v_win = nsa_kv_prep(cols, b_, l_, rope_tables_kv_pair(rope),
                                             k_gains[1], k_gains[2])
    k_cmp, v_cmp = nsa_compress(cols, b_, l_, pos_k, pos_v, k_w1, k_w2, v_w1, v_w2, k_gains[0])
    q_aug, o_cmp = nsa_select(cols, b_, l_, k_cmp, v_cmp, jnp.tile(rope, (1, 1, GROUP_HEADS)),
                              q_gain)
    return nsa_attend(cols, b_, l_, q_aug, k_aug, v_slc, k_win, v_win, o_cmp)


DSA_TQ = 128
DSA_KEY_TILE = 256
DSA_PREP_TK = 512
IDX_PACK = LANES
INT_MIN = -2 ** 31
MASKED_SCORE = -1e30


def _split_hi_lo(t):
    hi = t.astype(BF16)
    lo = (t - hi.astype(F32)).astype(BF16)
    return hi, lo


def _placement(rows, cols, pairs):
    p = np.zeros((rows, cols), np.float32)
    for r, c in pairs:
        p[r, c] = 1.0
    return jnp.asarray(p, BF16)


def _dsa_kv_prep_kernel(ckv_ref, ik_ref, rope_ref, ropei_ref, kvg_ref, wkv_ref, kg_ref, ikg_ref,
                        pkh_ref, pkl_ref, k_ref, v_ref, ik3_ref):
    tk = ckv_ref.shape[0]
    lane = lax.broadcasted_iota(jnp.int32, (tk, LANES), 1)
    ckv = _rms_rows(ckv_ref[...], kvg_ref[...]).astype(BF16)
    kv = _dot(ckv, wkv_ref[...])
    is_k = lane < HEAD_DIM
    ms = jnp.sum(jnp.where(is_k, kv * kv, 0.0), axis=-1, keepdims=True) * (1.0 / HEAD_DIM)
    y = _apply_rope(jnp.where(is_k, kv * lax.rsqrt(ms + EPS) * kg_ref[...], kv), rope_ref)
    k_ref[0] = y[:, :HEAD_DIM].astype(BF16)
    v_ref[0] = y[:, HEAD_DIM:].astype(BF16)
    ik = ik_ref[...]
    ms = jnp.sum(ik * ik, axis=-1, keepdims=True) * (1.0 / IDX_DIM)
    ikn = _apply_rope(ik * lax.rsqrt(ms + EPS) * ikg_ref[...], ropei_ref, IDX_DIM // 8)
    hi, lo = _split_hi_lo(ikn)
    ik3_ref[0] = (_dot(hi, pkh_ref[...]) + _dot(lo, pkl_ref[...])).astype(BF16)


def dsa_kv_prep(cols, b_, l_, rope_pair, rope_idx, kv_gain, w_uk, w_uv, k_gain, idxk_gain):
    tk = DSA_PREP_TK
    nt = l_ // tk
    ones = jnp.ones((HEAD_DIM,), F32)
    kg = jnp.concatenate([k_gain, ones]).reshape(1, LANES)
    ikg = jnp.pad(idxk_gain, (0, LANES - IDX_DIM)).reshape(1, LANES)
    wkv = jnp.concatenate([w_uk, w_uv], axis=1).astype(BF16)
    d = range(IDX_DIM)
    pkh = _placement(LANES, IDX_PACK, [(i, i) for i in d] + [(i, 2 * IDX_DIM + i) for i in d])
    pkl = _placement(LANES, IDX_PACK, [(i, IDX_DIM + i) for i in d])
    const = lambda shape: pl.BlockSpec(shape, lambda b, i: (0,) * len(shape))
    out = lambda w: pl.BlockSpec((1, tk, w), lambda b, i: (b, i, 0))
    return pl.pallas_call(
        _dsa_kv_prep_kernel,
        out_shape=(jax.ShapeDtypeStruct((b_, l_, HEAD_DIM), BF16),
                   jax.ShapeDtypeStruct((b_, l_, HEAD_DIM), BF16),
                   jax.ShapeDtypeStruct((b_, l_, IDX_PACK), BF16)),
        grid=(b_, nt),
        in_specs=[
            pl.BlockSpec((tk, LANES), lambda b, i: (b * nt + i, C_DSA_CKV // LANES)),
            pl.BlockSpec((tk, LANES), lambda b, i: (b * nt + i, C_DSA_IK // LANES)),
            pl.BlockSpec((3, tk, LANES), lambda b, i: (0, i, 0)),
            pl.BlockSpec((3, tk, LANES), lambda b, i: (0, i, 0)),
            const((1, LANES)), const((DSA_LATENT, LANES)), const((1, LANES)), const((1, LANES)),
            const((LANES, IDX_PACK)), const((LANES, IDX_PACK)),
        ],
        out_specs=(out(HEAD_DIM), out(HEAD_DIM), out(IDX_PACK)),
        compiler_params=pltpu.CompilerParams(
            dimension_semantics=("arbitrary", "arbitrary"),
            vmem_limit_bytes=V7X_VMEM_LIMIT_BYTES),
        name="dsa_kv_prep",
    )(cols, cols, rope_pair, rope_idx, kv_gain.reshape(1, DSA_LATENT), wkv, kg, ikg, pkh, pkl)


def _dsa_q_prep_kernel(q_ref, iq_ref, rope_ref, ropei_ref, qg_ref, pqh_ref, pql_ref,
                       qh_ref, iq3_ref):
    g = qg_ref[...]
    q = q_ref[...]
    qn = jnp.concatenate([_rms_rows(q[:, h * HEAD_DIM:(h + 1) * HEAD_DIM], g)
                          for h in range(GROUP_HEADS)], axis=-1)
    qr = _apply_rope(qn, rope_ref) * (HEAD_DIM ** -0.5)
    for h in range(GROUP_HEADS):
        qh_ref[0, h] = qr[:, h * HEAD_DIM:(h + 1) * HEAD_DIM].astype(BF16)
    hi, lo = _split_hi_lo(_apply_rope(iq_ref[...], ropei_ref, IDX_DIM // 8))
    iq3_ref[0] = (_dot(hi, pqh_ref[...]) + _dot(lo, pql_ref[...])).astype(BF16)


def dsa_q_prep(cols, b_, l_, rope_q, rope_iq, q_gain):
    tq = 256
    nt = l_ // tq
    w = IDX_HEADS * IDX_DIM
    hd = [(h, i) for h in range(IDX_HEADS) for i in range(IDX_DIM)]
    pqh = _placement(w, IDX_HEADS * IDX_PACK,
                     [(IDX_DIM * h + i, IDX_PACK * h + i) for h, i in hd]
                     + [(IDX_DIM * h + i, IDX_PACK * h + IDX_DIM + i) for h, i in hd])
    pql = _placement(w, IDX_HEADS * IDX_PACK,
                     [(IDX_DIM * h + i, IDX_PACK * h + 2 * IDX_DIM + i) for h, i in hd])
    const = lambda shape: pl.BlockSpec(shape, lambda b, i: (0,) * len(shape))
    return pl.pallas_call(
        _dsa_q_prep_kernel,
        out_shape=(jax.ShapeDtypeStruct((b_, GROUP_HEADS, l_, HEAD_DIM), BF16),
                   jax.ShapeDtypeStruct((b_, l_, IDX_HEADS * IDX_PACK), BF16)),
        grid=(b_, nt),
        in_specs=[
            pl.BlockSpec((tq, GROUP_WIDTH), lambda b, i: (b * nt + i, C_DSA_Q // GROUP_WIDTH)),
            pl.BlockSpec((tq, w), lambda b, i: (b * nt + i, C_DSA_IQ // w)),
            pl.BlockSpec((3, tq, GROUP_WIDTH), lambda b, i: (0, i, 0)),
            pl.BlockSpec((3, tq, w), lambda b, i: (0, i, 0)),
            const((1, HEAD_DIM)), const(pqh.shape), const(pql.shape),
        ],
        out_specs=(pl.BlockSpec((1, GROUP_HEADS, tq, HEAD_DIM), lambda b, i: (b, 0, i, 0)),
                   pl.BlockSpec((1, tq, IDX_HEADS * IDX_PACK), lambda b, i: (b, i, 0))),
        compiler_params=pltpu.CompilerParams(
            dimension_semantics=("arbitrary", "arbitrary"),
            vmem_limit_bytes=V7X_VMEM_LIMIT_BYTES),
        name="dsa_q_prep",
    )(cols, cols, rope_q, rope_iq, q_gain.reshape(1, HEAD_DIM), pqh, pql)


def _dsa_attend_kernel(qh_ref, iq3_ref, iw_ref, k_ref, v_ref, ik3_ref, o_ref, key_ref, *,
                       topk, idx_bits):
    tq = iq3_ref.shape[1]
    ts = DSA_KEY_TILE
    nh = GROUP_HEADS
    t0 = pl.program_id(1) * tq
    n_kt = (t0 + tq + ts - 1) // ts
    qpos = t0 + lax.broadcasted_iota(jnp.int32, (tq, 1), 0)

    iq3 = iq3_ref[0]
    lhs = jnp.concatenate([iq3[:, h * IDX_PACK:(h + 1) * IDX_PACK] for h in range(IDX_HEADS)],
                          axis=0)
    iw = iw_ref[...] * (IDX_HEADS ** -0.5 * IDX_DIM ** -0.5)
    w_col = jnp.concatenate([iw[:, h:h + 1] for h in range(IDX_HEADS)], axis=0)

    def score_tile(kt, _):
        k0 = pl.multiple_of(kt * ts, ts)
        rel = jnp.maximum(_dot_nt(lhs, ik3_ref[0, pl.ds(k0, ts), :]), 0.0) * w_col
        sc = rel[0:tq]
        for h in range(1, IDX_HEADS):
            sc = sc + rel[h * tq:(h + 1) * tq]
        kpos = k0 + lax.broadcasted_iota(jnp.int32, (tq, ts), 1)
        sc = jnp.where(kpos <= qpos, sc + 0.0, -jnp.inf)
        bits = pltpu.bitcast(sc, jnp.int32)
        key_ref[:, pl.ds(k0, ts)] = bits ^ (jnp.right_shift(bits, 31) & jnp.int32(0x7FFFFFFF))
        return 0

    lax.fori_loop(0, n_kt, score_tile, 0)

    def count(pred):
        def tile(kt, c):
            k0 = pl.multiple_of(kt * ts, ts)
            kpos = k0 + lax.broadcasted_iota(jnp.int32, (tq, ts), 1)
            return c + jnp.where(pred(key_ref[:, pl.ds(k0, ts)], kpos), 1.0, 0.0)
        c = lax.fori_loop(0, n_kt, tile, jnp.zeros((tq, ts), F32))
        return jnp.sum(c, axis=-1, keepdims=True)

    def value_bit(i, thr):
        cand = thr | jnp.left_shift(jnp.int32(1), 31 - i)
        cand_s = cand ^ jnp.int32(INT_MIN)
        return jnp.where(count(lambda kk, kpos: kk >= cand_s) >= topk, cand, thr)

    thr = lax.fori_loop(0, 32, value_bit, jnp.zeros((tq, 1), jnp.int32)) ^ jnp.int32(INT_MIN)
    need = topk - count(lambda kk, kpos: kk > thr)

    def index_bit(i, last):
        cand = last | jnp.left_shift(jnp.int32(1), idx_bits - 1 - i)
        below = count(lambda kk, kpos: (kk == thr) & (kpos < cand))
        return jnp.where(below < need, cand, last)

    last = lax.fori_loop(0, idx_bits, index_bit, jnp.zeros((tq, 1), jnp.int32))

    qs = qh_ref[0].reshape(nh * tq, HEAD_DIM)

    def key_tile(kt, carry):
        m, l, acc = carry
        k0 = pl.multiple_of(kt * ts, ts)
        kk = key_ref[:, pl.ds(k0, ts)]
        kpos = k0 + lax.broadcasted_iota(jnp.int32, (tq, ts), 1)
        sel = ((kk > thr) | ((kk == thr) & (kpos <= last))) & (kpos <= qpos)
        bias = jnp.where(sel, 0.0, MASKED_SCORE)
        s = _dot_nt(qs, k_ref[0, pl.ds(k0, ts), :]) + jnp.concatenate([bias] * nh, axis=0)
        m_new = jnp.maximum(m, jnp.max(s, axis=-1, keepdims=True))
        alpha = jnp.exp(m - m_new)
        p = jnp.exp(s - m_new)
        l = alpha * l + jnp.sum(p, axis=-1, keepdims=True)
        acc = alpha * acc + _dot(p.astype(BF16), v_ref[0, pl.ds(k0, ts), :])
        return m_new, l, acc

    init = (jnp.full((nh * tq, 1), MASKED_SCORE, F32), jnp.zeros((nh * tq, 1), F32),
            jnp.zeros((nh * tq, HEAD_DIM), F32))
    _, l, acc = lax.fori_loop(0, n_kt, key_tile, init)
    o = acc / l
    o_ref[...] = jnp.concatenate([o[h * tq:(h + 1) * tq] for h in range(nh)], axis=-1)


def dsa_attend(cols, b_, l_, qh, iq3, k, v, ik3):
    tq = DSA_TQ
    nt = l_ // tq
    topk = min(DSA_TOPK_MAX, l_ // 4)
    idx_bits = int(np.log2(l_))
    assert 2 ** idx_bits == l_ and l_ % DSA_KEY_TILE == 0 and topk <= DSA_KEY_TILE
    seq = lambda w: pl.BlockSpec((1, l_, w), lambda b, i: (b, 0, 0))
    return pl.pallas_call(
        functools.partial(_dsa_attend_kernel, topk=topk, idx_bits=idx_bits),
        out_shape=jax.ShapeDtypeStruct((b_ * l_, GROUP_WIDTH), F32),
        grid=(b_, nt),
        in_specs=[
            pl.BlockSpec((1, GROUP_HEADS, tq, HEAD_DIM), lambda b, i: (b, 0, i, 0)),
            pl.BlockSpec((1, tq, IDX_HEADS * IDX_PACK), lambda b, i: (b, i, 0)),
            pl.BlockSpec((tq, LANES), lambda b, i: (b * nt + i, C_DSA_IW // LANES)),
            seq(HEAD_DIM), seq(HEAD_DIM), seq(IDX_PACK),
        ],
        out_specs=pl.BlockSpec((tq, GROUP_WIDTH), lambda b, i: (b * nt + i, 0)),
        scratch_shapes=[pltpu.VMEM((tq, l_), jnp.int32)],
        compiler_params=pltpu.CompilerParams(
            dimension_semantics=("arbitrary", "arbitrary"),
            vmem_limit_bytes=V7X_VMEM_LIMIT_BYTES),
        name="dsa_attend",
    )(qh, iq3, cols, k, v, ik3)


def dsa_mixer_pallas(cols, b_, l_, rope, kv_gain, w_uk, w_uv, q_gain, k_gain, idxk_gain):
    rope_i = rope_tables(l_, IDX_DIM)
    k, v, ik3 = dsa_kv_prep(cols, b_, l_, rope_tables_kv_pair(rope), rope_tables_pad(rope_i, LANES),
                            kv_gain, w_uk, w_uv, k_gain, idxk_gain)
    qh, iq3 = dsa_q_prep(cols, b_, l_, jnp.tile(rope, (1, 1, GROUP_HEADS)),
                         jnp.tile(rope_i, (1, 1, IDX_HEADS)), q_gain)
    return dsa_attend(cols, b_, l_, qh, iq3, k, v, ik3)


_TN = (((0,), (0,)), ((), ()))


def _dot_tn(a, b):
    return lax.dot_general(a, b, _TN, preferred_element_type=F32)


def _split3(t):
    hi = t.astype(BF16)
    r = t - hi.astype(F32)
    mid = r.astype(BF16)
    lo = (r - mid.astype(F32)).astype(BF16)
    return hi, mid, lo


def _tri_cumsum(tri, t):
    hi, mid, lo = _split3(t)
    return _dot(tri, hi) + _dot(tri, mid) + _dot(tri, lo)


def _cumsum_tri_rows(t, tri_u):
    hi, mid, lo = _split3(t)
    return _dot(hi, tri_u) + _dot(mid, tri_u) + _dot(lo, tri_u)


def _head_rms(o, gain):
    return jnp.concatenate([_rms_rows(o[:, h * HEAD_DIM:(h + 1) * HEAD_DIM], gain)
                            for h in range(GROUP_HEADS)], axis=-1)


HG_SUB = 8


def _hgrn2_kernel(q_ref, f_ref, i_ref, g_ref, lb_ref, gain_ref, tri_ref, ones_ref, bd_ref,
                  o_ref, st_ref, b_ref, kk_ref, v_ref):
    c = HG_CHUNK
    w = GROUP_WIDTH

    @pl.when(pl.program_id(1) == 0)
    def _():
        st_ref[...] = jnp.zeros_like(st_ref)

    lb = lb_ref[...]
    q = q_ref[...]
    qs = q * jax.nn.sigmoid(q) * (HEAD_DIM ** -0.5)
    forget = lb + (1.0 - lb) * jax.nn.sigmoid(f_ref[...])
    kk = 1.0 - forget
    bcum = _tri_cumsum(tri_ref[...], jnp.log(forget))
    v = i_ref[...]
    b_ref[...] = bcum
    kk_ref[...] = kk
    v_ref[...] = v

    out = _dot_nt((qs * jnp.exp(bcum)).astype(BF16), st_ref[...].astype(BF16))

    ones_bd = ones_ref[...]
    pieces = []
    for g in range(c // HG_SUB):
        r0 = g * HG_SUB
        nr = c - r0
        qg = qs[r0:, :]
        bg = bcum[r0:, :]
        trow = r0 + lax.broadcasted_iota(jnp.int32, (nr, w), 0)
        terms = []
        for j in range(HG_SUB):
            s = r0 + j
            d = qg * kk_ref[s:s + 1, :] * jnp.exp(bg - b_ref[s:s + 1, :])
            terms.append(jnp.where(trow >= s, d, 0.0).astype(BF16))
        red = _dot(jnp.concatenate(terms, axis=0), ones_bd)
        acc = red[0:nr] * v_ref[r0:r0 + 1, :]
        for j in range(1, HG_SUB):
            acc = acc + red[j * nr:(j + 1) * nr] * v_ref[r0 + j:r0 + j + 1, :]
        pieces.append(acc)
    intra = pieces[0]
    for g in range(1, c // HG_SUB):
        pad = jnp.zeros((g * HG_SUB, w), F32)
        intra = intra + jnp.concatenate([pad, pieces[g]], axis=0)
    out = out + intra

    b_last = bcum[c - 1:c, :]
    kt = (kk * jnp.exp(b_last - bcum)).astype(BF16)
    st_ref[...] = jnp.exp(b_last) * st_ref[...] + _dot_tn(v.astype(BF16), kt) * bd_ref[...]

    g_in = g_ref[...]
    o_ref[...] = _head_rms(out, gain_ref[...]) * (g_in * jax.nn.sigmoid(g_in))


def hgrn2_mixer_pallas(cols, b_, l_, lb, o_gain):
    c = HG_CHUNK
    nt = l_ // c
    w = GROUP_WIDTH
    head = np.arange(w) // HEAD_DIM
    same = (head[:, None] == head[None, :]).astype(np.float32)
    tri = np.tril(np.ones((c, c), np.float32))
    col = lambda j: pl.BlockSpec((c, w), lambda b, i: (b * nt + i, C_HG // w + j))
    const = lambda shape: pl.BlockSpec(shape, lambda b, i: (0,) * len(shape))
    return pl.pallas_call(
        _hgrn2_kernel,
        out_shape=jax.ShapeDtypeStruct((b_ * l_, w), F32),
        grid=(b_, nt),
        in_specs=[col(0), col(1), col(2), col(3), const((1, w)), const((1, HEAD_DIM)),
                  const((c, c)), const((w, w)), const((w, w))],
        out_specs=pl.BlockSpec((c, w), lambda b, i: (b * nt + i, 0)),
        scratch_shapes=[pltpu.VMEM((w, w), F32), pltpu.VMEM((c, w), F32),
                        pltpu.VMEM((c, w), F32), pltpu.VMEM((c, w), F32)],
        compiler_params=pltpu.CompilerParams(
            dimension_semantics=("arbitrary", "arbitrary"),
            vmem_limit_bytes=V7X_VMEM_LIMIT_BYTES),
        name="hgrn2",
    )(cols, cols, cols, cols, lb.reshape(1, w), o_gain.reshape(1, HEAD_DIM),
      jnp.asarray(tri, BF16), jnp.asarray(same, BF16), jnp.asarray(same, F32))


ML_TC = 256
ML_M_INIT = -1e30


def _mlstm_kernel(gate_ref, qk_ref, v_ref, og_ref, cw_ref, cb_ref, gb_ref, gain_ref, tril_ref,
                  triu_ref, o_ref, xprev_ref, cmat_ref, nvec_ref, m_ref):
    c = ML_TC
    nh = GROUP_HEADS
    w = GROUP_WIDTH

    @pl.when(pl.program_id(1) == 0)
    def _():
        xprev_ref[...] = jnp.zeros_like(xprev_ref)
        cmat_ref[...] = jnp.zeros_like(cmat_ref)
        nvec_ref[...] = jnp.zeros_like(nvec_ref)
        m_ref[...] = jnp.full(m_ref.shape, ML_M_INIT, F32)

    x = qk_ref[...]
    prev = xprev_ref[...]
    row = lax.broadcasted_iota(jnp.int32, x.shape, 0)
    acc = x * cw_ref[CONV_WIDTH - 1:CONV_WIDTH, :] + cb_ref[...]
    for j in range(1, CONV_WIDTH):
        shifted = jnp.where(row < j, pltpu.roll(prev, j, 0), pltpu.roll(x, j, 0))
        acc = acc + shifted * cw_ref[CONV_WIDTH - 1 - j:CONV_WIDTH - j, :]
    xprev_ref[...] = x
    qk = acc * jax.nn.sigmoid(acc)
    q = qk[:, :w]
    k = qk[:, w:] * (HEAD_DIM ** -0.5)
    v = v_ref[...]

    pre = gate_ref[...] + gb_ref[...]
    lane = lax.broadcasted_iota(jnp.int32, pre.shape, 1)
    log_f = jnp.minimum(pre, 0.0) - jnp.log1p(jnp.exp(-jnp.abs(pre)))
    log_f = jnp.where((lane >= nh) & (lane < 2 * nh), log_f, 0.0)
    bcum_c = _tri_cumsum(tril_ref[...], log_f)
    bcum_r = _cumsum_tri_rows(log_f.T, triu_ref[...])
    pre_r = pre.T
    tri = (lax.broadcasted_iota(jnp.int32, (c, c), 0) >= lax.broadcasted_iota(jnp.int32, (c, c), 1))

    outs = []
    for h in range(nh):
        sl = slice(h * HEAD_DIM, (h + 1) * HEAD_DIM)
        qh, kh, vh = q[:, sl], k[:, sl], v[:, sl]
        bc = bcum_c[:, nh + h:nh + h + 1]
        li_c = pre[:, h:h + 1]
        a_r = pre_r[h:h + 1, :] - bcum_r[nh + h:nh + h + 1, :]
        m_prev = m_ref[h:h + 1, 0:1]
        log_d = jnp.where(tri, bc + a_r, -jnp.inf)
        inter = bc + m_prev
        m_t = jnp.maximum(inter, jnp.max(log_d, axis=-1, keepdims=True))
        d_mat = jnp.exp(log_d - m_t)
        w_inter = jnp.exp(inter - m_t)
        qb = qh.astype(BF16)
        s = _dot_nt(qb, kh.astype(BF16)) * d_mat
        num = w_inter * _dot(qb, cmat_ref[h].astype(BF16)) + _dot(s.astype(BF16), vh.astype(BF16))
        den = (w_inter * jnp.sum(qh * nvec_ref[h], axis=-1, keepdims=True)
               + jnp.sum(s, axis=-1, keepdims=True))
        outs.append(num / jnp.maximum(jnp.abs(den), jnp.exp(-m_t)))
        b_last = bc[c - 1:c, :]
        log_w = b_last + (li_c - bc)
        m_new = jnp.maximum(b_last + m_prev, jnp.max(log_w, axis=0, keepdims=True))
        kw = kh * jnp.exp(log_w - m_new)
        decay = jnp.exp(b_last + m_prev - m_new)
        cmat_ref[h] = decay * cmat_ref[h] + _dot_tn(kw.astype(BF16), vh.astype(BF16))
        nvec_ref[h] = decay * nvec_ref[h] + jnp.sum(kw, axis=0, keepdims=True)
        m_ref[h:h + 1, :] = jnp.broadcast_to(m_new, (1, LANES))

    hh = _head_rms(jnp.concatenate(outs, axis=-1), gain_ref[...])
    o_ref[...] = hh * jax.nn.sigmoid(og_ref[...])


def mlstm_mixer_pallas(cols, b_, l_, conv_w, conv_b, i_bias, f_bias, o_gain):
    c = ML_TC
    nt = l_ // c
    w = GROUP_WIDTH
    gb = jnp.pad(jnp.concatenate([i_bias, f_bias]), (0, LANES - 2 * GROUP_HEADS)).reshape(1, LANES)
    tril = np.tril(np.ones((c, c), np.float32))
    const = lambda shape: pl.BlockSpec(shape, lambda b, i: (0,) * len(shape))
    blk = lambda width, off: pl.BlockSpec((c, width), lambda b, i: (b * nt + i, off // width))
    return pl.pallas_call(
        _mlstm_kernel,
        out_shape=jax.ShapeDtypeStruct((b_ * l_, w), F32),
        grid=(b_, nt),
        in_specs=[blk(LANES, C_ML_GATE), blk(2 * w, C_ML_QK), blk(w, C_ML_V), blk(w, C_ML_OG),
                  const((CONV_WIDTH, 2 * w)), const((1, 2 * w)), const((1, LANES)),
                  const((1, HEAD_DIM)), const((c, c)), const((c, c))],
        out_specs=pl.BlockSpec((c, w), lambda b, i: (b * nt + i, 0)),
        scratch_shapes=[pltpu.VMEM((c, 2 * w), F32),
                        pltpu.VMEM((GROUP_HEADS, HEAD_DIM, HEAD_DIM), F32),
                        pltpu.VMEM((GROUP_HEADS, 1, HEAD_DIM), F32),
                        pltpu.VMEM((8, LANES), F32)],
        compiler_params=pltpu.CompilerParams(
            dimension_semantics=("arbitrary", "arbitrary"),
            vmem_limit_bytes=V7X_VMEM_LIMIT_BYTES),
        name="mlstm",
    )(cols, cols, cols, cols, conv_w, conv_b.reshape(1, 2 * w), gb, o_gain.reshape(1, HEAD_DIM),
      jnp.asarray(tril, BF16), jnp.asarray(tril.T, BF16))


def split_cols(t, sizes):
    return jnp.split(t, [int(s) for s in np.cumsum(sizes)[:-1]], axis=-1)


def split_heads(t, n):
    return t.reshape(t.shape[:-1] + (n, t.shape[-1] // n))


def rms_norm(t, g):
    t32 = t.astype(F32)
    y = t32 * lax.rsqrt(jnp.mean(t32 * t32, axis=-1, keepdims=True) + EPS)
    return (y * g.astype(F32)).astype(t.dtype)


def partial_rope(t, pos):
    d = t.shape[-1]
    rd = d // 4
    half = rd // 2
    inv = ROPE_THETA ** (-jnp.arange(half, dtype=F32) * 2.0 / rd)
    ang = pos.astype(F32)[:, None] * inv[None, :]
    cos = jnp.cos(ang)[:, None, :].astype(t.dtype)
    sin = jnp.sin(ang)[:, None, :].astype(t.dtype)
    x1, x2 = t[..., :half], t[..., half:rd]
    return jnp.concatenate([x1 * cos - x2 * sin, x2 * cos + x1 * sin, t[..., rd:]], axis=-1)


def masked_softmax(s, mask):
    s = jnp.where(mask, s.astype(F32), -jnp.inf)
    m = jnp.max(s, axis=-1, keepdims=True)
    m = jnp.where(jnp.isfinite(m), m, 0.0)
    e = jnp.exp(s - m)
    den = jnp.sum(e, axis=-1, keepdims=True)
    return e / jnp.where(den > 0, den, 1.0)


def gather_rows(table, idx):
    return jax.vmap(lambda tb, ix: tb[ix])(table, idx)


def causal_conv(t, w, b):
    y = lax.conv_general_dilated(t, w[:, None, :].astype(t.dtype), window_strides=(1,),
                                 padding=[(CONV_WIDTH - 1, 0)],
                                 dimension_numbers=('NWC', 'WIO', 'NWC'),
                                 feature_group_count=t.shape[-1])
    return y + b.astype(t.dtype)


def to_chunks(t, c):
    b_, l_, h_ = t.shape[:3]
    t = t.reshape((b_, l_ // c, c, h_) + t.shape[3:])
    return jnp.moveaxis(t, (1, 3), (0, 2))


def from_chunks(t):
    t = jnp.moveaxis(t, (0, 2), (1, 3))
    return t.reshape((t.shape[0], t.shape[1] * t.shape[2], t.shape[3]) + t.shape[4:])


def unblock(o):
    o = jnp.moveaxis(o, 0, 1)
    return o.reshape((o.shape[0], o.shape[1] * o.shape[2]) + o.shape[3:])


def hgrn2_mixer(cols, lb, o_gain):
    b_, l_ = cols.shape[:2]
    q, f, i, g = split_cols(cols, HG_SPLITS)
    q = split_heads(jax.nn.silu(q), GROUP_HEADS).astype(F32) * HEAD_DIM ** -0.5
    forget = lb + (1.0 - lb) * jax.nn.sigmoid(f.astype(F32))
    k = split_heads(1.0 - forget, GROUP_HEADS)
    logf = split_heads(jnp.log(forget), GROUP_HEADS)
    v = split_heads(i, GROUP_HEADS).astype(F32)
    tri = jnp.tril(jnp.ones((HG_CHUNK, HG_CHUNK), bool))[:, :, None]

    def step(state, inp):
        qc, kc, vc, gc = inp
        bcum = jnp.cumsum(gc, axis=2)
        o_inter = jnp.einsum('bhtk,bhkv->bhtv', qc * jnp.exp(bcum), state)
        diff = bcum[:, :, :, None, :] - bcum[:, :, None, :, :]
        decay = jnp.exp(jnp.where(tri, diff, -jnp.inf))
        attn = jnp.einsum('bhtk,bhsk,bhtsk->bhts', qc, kc, decay)
        out = o_inter + jnp.einsum('bhts,bhsv->bhtv', attn, vc)
        b_last = bcum[:, :, -1:, :]
        state = (jnp.exp(b_last[:, :, 0, :])[..., None] * state
                 + jnp.einsum('bhsk,bhsv->bhkv', kc * jnp.exp(b_last - bcum), vc))
        return state, out

    s0 = jnp.zeros((b_, GROUP_HEADS, HEAD_DIM, HEAD_DIM), F32)
    xs = (to_chunks(q, HG_CHUNK), to_chunks(k, HG_CHUNK), to_chunks(v, HG_CHUNK),
          to_chunks(logf, HG_CHUNK))
    _, o = lax.scan(step, s0, xs)
    o = rms_norm(from_chunks(o), o_gain).astype(cols.dtype)
    o = o * jax.nn.silu(split_heads(g, GROUP_HEADS))
    return o.reshape(b_, l_, GROUP_WIDTH)


def dsa_mixer(cols, pos, kv_gain, w_uk, w_uv, q_gain, k_gain, idxk_gain):
    b_, l_ = cols.shape[:2]
    q, ckv, iq, ik, iw = split_cols(cols, DSA_SPLITS)
    q = partial_rope(rms_norm(split_heads(q, GROUP_HEADS), q_gain), pos)
    ckv = rms_norm(ckv, kv_gain)
    k = partial_rope(rms_norm(ckv @ w_uk, k_gain)[:, :, None, :], pos)[:, :, 0]
    v = ckv @ w_uv
    iq = partial_rope(split_heads(iq, IDX_HEADS), pos)
    ik = partial_rope(rms_norm(ik, idxk_gain)[:, :, None, :], pos)[:, :, 0]
    iw = iw * (IDX_HEADS ** -0.5 * IDX_DIM ** -0.5)
    topk = min(DSA_TOPK_MAX, l_ // 4)
    key_pos = jnp.arange(l_)

    def block(bi):
        t0 = bi * Q_BLOCK
        qpos = t0 + jnp.arange(Q_BLOCK)
        qb = lax.dynamic_slice_in_dim(q, t0, Q_BLOCK, axis=1)
        iqb = lax.dynamic_slice_in_dim(iq, t0, Q_BLOCK, axis=1)
        iwb = lax.dynamic_slice_in_dim(iw, t0, Q_BLOCK, axis=1)
        rel = jax.nn.relu(jnp.einsum('bthd,bsd->bhts', iqb, ik))
        score = jnp.einsum('bhts,bth->bts', rel, iwb).astype(F32)
        causal = key_pos[None, :] <= qpos[:, None]
        score = jnp.where(causal[None], score, -jnp.inf)
        _, idx = lax.top_k(score, topk)
        k_sel = gather_rows(k, idx)
        v_sel = gather_rows(v, idx)
        s = jnp.einsum('bthd,btkd->bhtk', qb, k_sel) * HEAD_DIM ** -0.5
        p = masked_softmax(s, (idx <= qpos[None, :, None])[:, None]).astype(v.dtype)
        return jnp.einsum('bhtk,btkd->bthd', p, v_sel)

    o = unblock(lax.map(block, jnp.arange(l_ // Q_BLOCK)))
    return o.reshape(b_, l_, GROUP_WIDTH)


def nsa_mixer(cols, pos, pos_k, pos_v, k_w1, k_w2, v_w1, v_w2, q_gain, k_gains):
    b_, l_ = cols.shape[:2]
    scale = HEAD_DIM ** -0.5
    q, kc, vc, ks, vs, kw, vw, gates = split_cols(cols, NSA_SPLITS)
    q = rms_norm(split_heads(q, GROUP_HEADS), q_gain)
    q_rot = partial_rope(q, pos)
    gates = jax.nn.sigmoid(gates.reshape(b_, l_, 3, GROUP_HEADS, 1))

    n_cmp = (l_ - CMP_BLOCK) // CMP_STRIDE + 1
    cmp_idx = np.arange(n_cmp)[:, None] * CMP_STRIDE + np.arange(CMP_BLOCK)[None, :]

    def compress(t, pe, w1, w2):
        blk = (t[:, cmp_idx] + pe).reshape(b_, n_cmp, CMP_BLOCK * HEAD_DIM)
        return jax.nn.relu(blk @ w1) @ w2

    k_cmp = rms_norm(compress(kc, pos_k, k_w1, k_w2), k_gains[0])
    v_cmp = compress(vc, pos_v, v_w1, v_w2)
    cmp_vis = cmp_idx[:, -1][None, :] <= np.arange(l_)[:, None]
    p_cmp = masked_softmax(jnp.einsum('bthd,bjd->bhtj', q, k_cmp) * scale, cmp_vis)
    o_cmp = jnp.einsum('bhtj,bjd->bthd', p_cmp.astype(v_cmp.dtype), v_cmp)

    n_slc = l_ // SLC_BLOCK
    n_sel = min(SLC_TOPN, n_slc)
    st_c = np.arange(n_cmp) * CMP_STRIDE
    st_s = np.arange(n_slc) * SLC_BLOCK
    overlap = ((st_c[:, None] < st_s[None, :] + SLC_BLOCK)
               & (st_c[:, None] + CMP_BLOCK > st_s[None, :])).astype(np.float32)
    imp = jnp.einsum('bhtj,jn->btn', p_cmp, overlap)
    cur = np.arange(l_)[:, None] // SLC_BLOCK
    blk_id = np.arange(n_slc)[None, :]
    forced = (blk_id == 0) | (blk_id == cur) | (blk_id == cur - 1)
    imp = jnp.where(forced, jnp.inf, jnp.where(blk_id > cur, -jnp.inf, imp))
    _, sel = lax.top_k(imp, n_sel)

    k_s = partial_rope(rms_norm(ks, k_gains[1])[:, :, None], pos)[:, :, 0]
    k_blocks = k_s.reshape(b_, n_slc, SLC_BLOCK, HEAD_DIM)
    v_blocks = vs.reshape(b_, n_slc, SLC_BLOCK, HEAD_DIM)
    k_w = partial_rope(rms_norm(kw, k_gains[2])[:, :, None], pos)[:, :, 0]
    k_pad = jnp.pad(k_w, ((0, 0), (WINDOW, 0), (0, 0)))
    v_pad = jnp.pad(vw, ((0, 0), (WINDOW, 0), (0, 0)))

    def block(bi):
        t0 = bi * Q_BLOCK
        qpos = t0 + jnp.arange(Q_BLOCK)
        qb = lax.dynamic_slice_in_dim(q_rot, t0, Q_BLOCK, axis=1)
        sb = lax.dynamic_slice_in_dim(sel, t0, Q_BLOCK, axis=1)
        kb = gather_rows(k_blocks, sb)
        vb = gather_rows(v_blocks, sb)
        kpos = sb[..., None] * SLC_BLOCK + jnp.arange(SLC_BLOCK)
        s = jnp.einsum('bthd,btnsd->bhtns', qb, kb) * scale
        valid = (kpos <= qpos[None, :, None, None])[:, None]
        p = masked_softmax(s.reshape(b_, GROUP_HEADS, Q_BLOCK, n_sel * SLC_BLOCK),
                           valid.reshape(b_, 1, Q_BLOCK, n_sel * SLC_BLOCK))
        p = p.reshape(b_, GROUP_HEADS, Q_BLOCK, n_sel, SLC_BLOCK).astype(vb.dtype)
        o_slc = jnp.einsum('bhtns,btnsd->bthd', p, vb)
        kwb = lax.dynamic_slice_in_dim(k_pad, t0, WINDOW + Q_BLOCK, axis=1)
        vwb = lax.dynamic_slice_in_dim(v_pad, t0, WINDOW + Q_BLOCK, axis=1)
        wpos = t0 - WINDOW + jnp.arange(WINDOW + Q_BLOCK)
        dist = qpos[:, None] - wpos[None, :]
        wvalid = (dist >= 0) & (dist < WINDOW) & (wpos[None, :] >= 0)
        sw = jnp.einsum('bthd,bsd->bhts', qb, kwb) * scale
        pw = masked_softmax(sw, wvalid[None, None]).astype(vwb.dtype)
        o_swa = jnp.einsum('bhts,bsd->bthd', pw, vwb)
        return o_slc, o_swa

    o_slc, o_swa = lax.map(block, jnp.arange(l_ // Q_BLOCK))
    o = gates[:, :, 0] * o_cmp + gates[:, :, 1] * unblock(o_slc) + gates[:, :, 2] * unblock(o_swa)
    return o.reshape(b_, l_, GROUP_WIDTH)


def mlstm_mixer(cols, conv_w, conv_b, i_bias, f_bias, o_gain):
    b_, l_ = cols.shape[:2]
    qk, v, og, ig, fg = split_cols(cols, ML_SPLITS)
    qk = jax.nn.silu(causal_conv(qk, conv_w, conv_b))
    q, k = jnp.split(qk, 2, axis=-1)
    q = split_heads(q, GROUP_HEADS).astype(F32)
    k = split_heads(k, GROUP_HEADS).astype(F32) * HEAD_DIM ** -0.5
    v = split_heads(v, GROUP_HEADS).astype(F32)
    log_i = (ig + i_bias).astype(F32)
    log_f = jax.nn.log_sigmoid((fg + f_bias).astype(F32))
    tri = jnp.tril(jnp.ones((ML_CHUNK, ML_CHUNK), bool))

    def step(carry, inp):
        cmat, nvec, m = carry
        qc, kc, vc, li, lf = inp
        bcum = jnp.cumsum(lf, axis=-1)
        log_d = jnp.where(tri, bcum[..., :, None] - bcum[..., None, :] + li[..., None, :], -jnp.inf)
        inter = bcum + m[..., None]
        m_t = jnp.maximum(inter, jnp.max(log_d, axis=-1))
        d_mat = jnp.exp(log_d - m_t[..., None])
        w_inter = jnp.exp(inter - m_t)
        s = jnp.einsum('bhtd,bhsd->bhts', qc, kc) * d_mat
        num = (w_inter[..., None] * jnp.einsum('bhtd,bhdv->bhtv', qc, cmat)
               + jnp.einsum('bhts,bhsv->bhtv', s, vc))
        den = w_inter * jnp.einsum('bhtd,bhd->bht', qc, nvec) + jnp.sum(s, axis=-1)
        h = num / jnp.maximum(jnp.abs(den), jnp.exp(-m_t))[..., None]
        b_last = bcum[..., -1]
        log_w = b_last[..., None] - bcum + li
        m_new = jnp.maximum(b_last + m, jnp.max(log_w, axis=-1))
        w_s = jnp.exp(log_w - m_new[..., None])
        decay = jnp.exp(b_last + m - m_new)
        cmat = decay[..., None, None] * cmat + jnp.einsum('bhs,bhsd,bhsv->bhdv', w_s, kc, vc)
        nvec = decay[..., None] * nvec + jnp.einsum('bhs,bhsd->bhd', w_s, kc)
        return (cmat, nvec, m_new), h

    init = (jnp.zeros((b_, GROUP_HEADS, HEAD_DIM, HEAD_DIM), F32),
            jnp.zeros((b_, GROUP_HEADS, HEAD_DIM), F32),
            jnp.full((b_, GROUP_HEADS), -1e30, F32))
    xs = (to_chunks(q, ML_CHUNK), to_chunks(k, ML_CHUNK), to_chunks(v, ML_CHUNK),
          to_chunks(log_i, ML_CHUNK), to_chunks(log_f, ML_CHUNK))
    _, h = lax.scan(step, init, xs)
    h = rms_norm(from_chunks(h), o_gain).astype(cols.dtype)
    h = h * jax.nn.sigmoid(split_heads(og, GROUP_HEADS))
    return h.reshape(b_, l_, GROUP_WIDTH)


def kernel(x, mem, lb_param, norm_mix, w_in, w_out, hg_o_gain, dsa_kv_gain, dsa_w_uk, dsa_w_uv,
           dsa_q_gain, dsa_k_gain, dsa_idxk_gain, nsa_pos_k, nsa_pos_v, nsa_k_w1, nsa_k_w2,
           nsa_v_w1, nsa_v_w2, nsa_q_gain, nsa_k_gains, ml_conv_w, ml_conv_b, ml_i_bias,
           ml_f_bias, ml_o_gain, norm_xa, norm_mem, xa_wq, xa_wkv, xa_wo, xa_q_gain, xa_k_gain,
           norm_ffn, ffn_w13, ffn_w2):
    b_, l_, d = x.shape
    pos = jnp.arange(l_)
    lb_all = jnp.cumsum(jax.nn.softmax(lb_param.astype(F32), axis=0), axis=0)
    lb_all = lb_all - lb_all[:1]
    x2d = x.reshape(b_ * l_, d)
    mem2d = mem.reshape(b_ * N_MEM, d)
    rope = rope_tables(l_)
    for l in range(DEPTH):
        cols = norm_matmul(x2d, norm_mix[l], pack_w_in(w_in[l]).astype(BF16), tm=256)
        c3 = cols.reshape(b_, l_, IN_COLS_PACKED)
        seg = lambda a, n: c3[:, :, a:a + n]
        c_hg = seg(C_HG, 1024)
        c_ml = jnp.concatenate([seg(C_ML_QK, 512), seg(C_ML_V, 256), seg(C_ML_OG, 256),
                                seg(C_ML_GATE, 8)], axis=-1)
        mixed = jnp.concatenate([
            hgrn2_mixer(c_hg, lb_all[l], hg_o_gain[l]),
            dsa_mixer_pallas(cols, b_, l_, rope, dsa_kv_gain[l], dsa_w_uk[l], dsa_w_uv[l],
                             dsa_q_gain[l], dsa_k_gain[l],
                             dsa_idxk_gain[l]).reshape(b_, l_, GROUP_WIDTH),
            nsa_mixer_pallas(cols, b_, l_, rope, nsa_pos_k[l], nsa_pos_v[l], nsa_k_w1[l],
                             nsa_k_w2[l], nsa_v_w1[l], nsa_v_w2[l], nsa_q_gain[l],
                             nsa_k_gains[l]).reshape(b_, l_, GROUP_WIDTH),
            mlstm_mixer(c_ml, ml_conv_w[l], ml_conv_b[l], ml_i_bias[l], ml_f_bias[l], ml_o_gain[l]),
        ], axis=-1)
        k_mem, v_mem = mem_kv(mem2d, norm_mem[l], xa_wkv[l].astype(BF16), xa_k_gain[l])
        x2d = post_block(x2d, mixed.reshape(b_ * l_, -1), l_, w_out[l].astype(BF16), norm_xa[l],
                         xa_wq[l].astype(BF16), xa_q_gain[l], k_mem, v_mem,
                         xa_wo[l].astype(BF16), norm_ffn[l], ffn_w13[l].astype(BF16),
                         ffn_w2[l].astype(BF16), tm=256)
    return x2d.reshape(b_, l_, d)
```

```python
import functools

import jax
import jax.numpy as jnp
from jax import lax
import numpy as np
from jax.experimental import pallas as pl
from jax.experimental.pallas import tpu as pltpu

F32 = jnp.float32
BF16 = jnp.bfloat16

D_MODEL = 1024
DEPTH = 2
HEAD_DIM = 64
GROUP_HEADS = 4
GROUP_WIDTH = GROUP_HEADS * HEAD_DIM
ROPE_THETA = 500000.0
EPS = 1e-6
Q_BLOCK = 128
N_MEM = 256
XA_HEADS = 4
XA_WIDTH = XA_HEADS * HEAD_DIM
HG_CHUNK = 64
DSA_LATENT = 128
IDX_HEADS = 8
IDX_DIM = 32
DSA_TOPK_MAX = 256
CMP_BLOCK = 32
CMP_STRIDE = 16
SLC_BLOCK = 64
SLC_SHIFT = 6
SLC_TOPN = 16
WINDOW = 512
ML_CHUNK = 64
CONV_WIDTH = 4
D_FF = 2816

HG_SPLITS = (GROUP_WIDTH,) * 4
DSA_SPLITS = (GROUP_WIDTH, DSA_LATENT, IDX_HEADS * IDX_DIM, IDX_DIM, IDX_HEADS)
NSA_SPLITS = (GROUP_WIDTH,) + (HEAD_DIM,) * 6 + (3 * GROUP_HEADS,)
ML_SPLITS = (2 * GROUP_WIDTH, GROUP_WIDTH, GROUP_WIDTH, GROUP_HEADS, GROUP_HEADS)
GROUP_COLS = (sum(HG_SPLITS), sum(DSA_SPLITS), sum(NSA_SPLITS), sum(ML_SPLITS))
IN_COLS = sum(GROUP_COLS)

V7X_VMEM_LIMIT_BYTES = 56 * 1024 * 1024
LANES = 128
FF_CHUNK = 256


def _round_up(n, m):
    return -(-n // m) * m


def _rms_rows(t, g):
    return t * lax.rsqrt(jnp.mean(t * t, axis=-1, keepdims=True) + EPS) * g


def _const_spec(shape):
    return pl.BlockSpec(shape, lambda *_: (0,) * len(shape), pipeline_mode=pl.Buffered(1))


def _norm_matmul_kernel(x_ref, g_ref, w_ref, o_ref):
    h = _rms_rows(x_ref[...], g_ref[...]).astype(BF16)
    o_ref[...] = jnp.dot(h, w_ref[...], preferred_element_type=F32)


def norm_matmul(x2d, gain, w_bf16, tm):
    m, k = x2d.shape
    n = w_bf16.shape[1]
    return pl.pallas_call(
        _norm_matmul_kernel,
        out_shape=jax.ShapeDtypeStruct((m, n), F32),
        grid=(m // tm,),
        in_specs=[
            pl.BlockSpec((tm, k), lambda i: (i, 0)),
            _const_spec((1, k)),
            _const_spec((k, n)),
        ],
        out_specs=pl.BlockSpec((tm, n), lambda i: (i, 0)),
        compiler_params=pltpu.CompilerParams(
            dimension_semantics=("arbitrary",), vmem_limit_bytes=V7X_VMEM_LIMIT_BYTES),
        name="norm_matmul",
    )(x2d, gain.reshape(1, k), w_bf16)


def _mem_kv_kernel(m_ref, g_ref, w_ref, kg_ref, k_ref, v_ref):
    mn = _rms_rows(m_ref[...], g_ref[...]).astype(BF16)
    kv = jnp.dot(mn, w_ref[...], preferred_element_type=F32)
    kg = kg_ref[...]
    ks = []
    for h in range(XA_HEADS):
        kh = kv[:, h * HEAD_DIM:(h + 1) * HEAD_DIM]
        ks.append(_rms_rows(kh, kg))
    k_ref[...] = jnp.concatenate(ks, axis=-1).astype(BF16)
    v_ref[...] = kv[:, XA_WIDTH:].astype(BF16)


def mem_kv(mem2d, gain, wkv_bf16, k_gain):
    m, k = mem2d.shape
    return pl.pallas_call(
        _mem_kv_kernel,
        out_shape=(jax.ShapeDtypeStruct((m, XA_WIDTH), BF16),
                   jax.ShapeDtypeStruct((m, XA_WIDTH), BF16)),
        grid=(m // N_MEM,),
        in_specs=[
            pl.BlockSpec((N_MEM, k), lambda i: (i, 0)),
            _const_spec((1, k)),
            _const_spec((k, 2 * XA_WIDTH)),
            _const_spec((1, HEAD_DIM)),
        ],
        out_specs=(pl.BlockSpec((N_MEM, XA_WIDTH), lambda i: (i, 0)),
                   pl.BlockSpec((N_MEM, XA_WIDTH), lambda i: (i, 0))),
        compiler_params=pltpu.CompilerParams(
            dimension_semantics=("arbitrary",), vmem_limit_bytes=V7X_VMEM_LIMIT_BYTES),
        name="mem_kv",
    )(mem2d, gain.reshape(1, k), wkv_bf16, k_gain.reshape(1, HEAD_DIM))


def _post_kernel(x_ref, mix_ref, wout_ref, gxa_ref, wq_ref, qg_ref, k_ref, v_ref, wo_ref,
                 gffn_ref, w13_ref, w2_ref, o_ref):
    x = x_ref[...] + jnp.dot(mix_ref[...].astype(BF16), wout_ref[...],
                             preferred_element_type=F32)
    h = _rms_rows(x, gxa_ref[...]).astype(BF16)
    q = jnp.dot(h, wq_ref[...], preferred_element_type=F32)
    qg = qg_ref[...] * (HEAD_DIM ** -0.5)
    k = k_ref[...]
    v = v_ref[...]
    outs = []
    for hd in range(XA_HEADS):
        sl = slice(hd * HEAD_DIM, (hd + 1) * HEAD_DIM)
        qh = _rms_rows(q[:, sl], qg).astype(BF16)
        s = lax.dot_general(qh, k[:, sl], (((1,), (1,)), ((), ())),
                            preferred_element_type=F32)
        e = jnp.exp(s - jnp.max(s, axis=-1, keepdims=True))
        p = e / jnp.sum(e, axis=-1, keepdims=True)
        outs.append(jnp.dot(p.astype(BF16), v[:, sl], preferred_element_type=F32))
    o = jnp.concatenate(outs, axis=-1).astype(BF16)
    x = x + jnp.dot(o, wo_ref[...], preferred_element_type=F32)
    h = _rms_rows(x, gffn_ref[...]).astype(BF16)
    acc = x
    for c in range(D_FF // FF_CHUNK):
        a = jnp.dot(h, w13_ref[:, c * FF_CHUNK:(c + 1) * FF_CHUNK], preferred_element_type=F32)
        b = jnp.dot(h, w13_ref[:, D_FF + c * FF_CHUNK:D_FF + (c + 1) * FF_CHUNK],
                    preferred_element_type=F32)
        act = (a * jax.nn.sigmoid(a) * b).astype(BF16)
        acc = acc + jnp.dot(act, w2_ref[c * FF_CHUNK:(c + 1) * FF_CHUNK, :],
                            preferred_element_type=F32)
    o_ref[...] = acc


def post_block(x2d, mixed2d, seq, wout, gxa, wq, qg, k_mem, v_mem, wo, gffn, w13, w2, tm):
    m, d = x2d.shape
    tiles_per_batch = seq // tm
    row = lambda i: (i, 0)
    mem_row = lambda i: (i // tiles_per_batch, 0)
    return pl.pallas_call(
        _post_kernel,
        out_shape=jax.ShapeDtypeStruct((m, d), F32),
        grid=(m // tm,),
        in_specs=[
            pl.BlockSpec((tm, d), row),
            pl.BlockSpec((tm, mixed2d.shape[1]), row),
            _const_spec(wout.shape),
            _const_spec((1, d)),
            _const_spec(wq.shape),
            _const_spec((1, HEAD_DIM)),
            pl.BlockSpec((N_MEM, XA_WIDTH), mem_row),
            pl.BlockSpec((N_MEM, XA_WIDTH), mem_row),
            _const_spec(wo.shape),
            _const_spec((1, d)),
            _const_spec(w13.shape),
            _const_spec(w2.shape),
        ],
        out_specs=pl.BlockSpec((tm, d), row),
        compiler_params=pltpu.CompilerParams(
            dimension_semantics=("arbitrary",), vmem_limit_bytes=V7X_VMEM_LIMIT_BYTES),
        name="post_block",
    )(x2d, mixed2d, wout, gxa.reshape(1, d), wq, qg.reshape(1, HEAD_DIM), k_mem, v_mem, wo,
      gffn.reshape(1, d), w13, w2)


C_HG = 0
C_DSA_Q = 1024
C_DSA_IQ = 1280
C_NSA_Q = 1536
C_ML_V = 1792
C_ML_QK = 2048
C_ML_OG = 2560
C_DSA_CKV = 2816
C_DSA_IK = 2944
C_DSA_IW = 3072
C_NSA_GATE = 3200
C_NSA_CMP = 3328
C_NSA_SLC = 3456
C_NSA_WIN = 3584
C_ML_GATE = 3712
IN_COLS_PACKED = 3840


def pack_w_in(w):
    dsa0 = GROUP_COLS[0]
    nsa0 = dsa0 + GROUP_COLS[1]
    ml0 = nsa0 + GROUP_COLS[2]

    def seg(a, n, width=None):
        s = w[:, a:a + n]
        if width is not None and width > n:
            s = jnp.pad(s, ((0, 0), (0, width - n)))
        return s

    parts = [
        seg(0, 1024),
        seg(dsa0, 256), seg(dsa0 + 384, 256),
        seg(nsa0, 256),
        seg(ml0 + 512, 256), seg(ml0, 512), seg(ml0 + 768, 256),
        seg(dsa0 + 256, 128),
        seg(dsa0 + 640, 32, LANES), seg(dsa0 + 672, 8, LANES),
        seg(nsa0 + 640, 12, LANES),
        seg(nsa0 + 256, 128), seg(nsa0 + 384, 128), seg(nsa0 + 512, 128),
        seg(ml0 + 1024, 8, LANES),
    ]
    packed = jnp.concatenate(parts, axis=1)
    assert packed.shape[1] == IN_COLS_PACKED
    return packed


def rope_tables(l_, d=HEAD_DIM):
    rd = d // 4
    half = rd // 2
    inv = ROPE_THETA ** (-jnp.arange(half, dtype=F32) * 2.0 / rd)
    ang = jnp.arange(l_).astype(F32)[:, None] * inv[None, :]
    cos, sin = jnp.cos(ang), jnp.sin(ang)
    zh = jnp.zeros((l_, half), F32)
    rest0 = jnp.zeros((l_, d - rd), F32)
    c = jnp.concatenate([cos, cos, rest0 + 1.0], axis=1)
    s1 = jnp.concatenate([-sin, zh, rest0], axis=1)
    s2 = jnp.concatenate([zh, sin, rest0], axis=1)
    return jnp.stack([c, s1, s2])


def rope_tables_pad(rt, width):
    n = width - rt.shape[-1]
    ident = jnp.stack([jnp.ones(rt.shape[1:2] + (n,), F32), jnp.zeros(rt.shape[1:2] + (n,), F32),
                       jnp.zeros(rt.shape[1:2] + (n,), F32)])
    return jnp.concatenate([rt, ident], axis=-1)


def rope_tables_kv_pair(rt):
    return rope_tables_pad(rt, 2 * rt.shape[-1])


def _apply_rope(t, rope_ref, half=HEAD_DIM // 8):
    w = t.shape[-1]
    return (t * rope_ref[0] + pltpu.roll(t, w - half, 1) * rope_ref[1]
            + pltpu.roll(t, half, 1) * rope_ref[2])


_NT = (((1,), (1,)), ((), ()))


def _dot_nt(a, b):
    return lax.dot_general(a, b, _NT, preferred_element_type=F32)


def _dot(a, b):
    return jnp.dot(a, b, preferred_element_type=F32)


NEG_BIG = -(2.0 ** 30)
NSA_KEY_TILE = 256
NSA_TQ = 128
NSA_SEL_TQ = 256
NSA_PREP_TK = 512


def _nsa_kv_prep_kernel(ps_ref, pw_ref, rope_ref, gs_ref, gw_ref, kaug_ref, vs_ref, kw_ref, vw_ref):
    tk = ps_ref.shape[0]
    lane = lax.broadcasted_iota(jnp.int32, (tk, LANES), 1)
    is_k = lane < HEAD_DIM

    def norm_rope(p, g):
        ms = jnp.sum(jnp.where(is_k, p * p, 0.0), axis=-1, keepdims=True) * (1.0 / HEAD_DIM)
        y = jnp.where(is_k, p * lax.rsqrt(ms + EPS) * g, p)
        return _apply_rope(y, rope_ref)

    ys = norm_rope(ps_ref[...], gs_ref[...])
    yw = norm_rope(pw_ref[...], gw_ref[...])
    row = pl.program_id(1) * tk + lax.broadcasted_iota(jnp.int32, (tk, LANES), 0)
    ind = jnp.where(jnp.right_shift(row, SLC_SHIFT) == (lane - HEAD_DIM), 1.0, 0.0)
    kaug_ref[0] = jnp.where(is_k, ys, ind).astype(BF16)
    vs_ref[0] = ys[:, HEAD_DIM:].astype(BF16)
    kw_ref[0] = yw[:, :HEAD_DIM].astype(BF16)
    vw_ref[0] = yw[:, HEAD_DIM:].astype(BF16)


def nsa_kv_prep(cols, b_, l_, rope_pair, g_slc, g_win):
    tk = NSA_PREP_TK
    nt = l_ // tk
    ones = jnp.ones((HEAD_DIM,), F32)
    gs = jnp.concatenate([g_slc, ones]).reshape(1, LANES)
    gw = jnp.concatenate([g_win, ones]).reshape(1, LANES)
    kv = lambda w: jax.ShapeDtypeStruct((b_, l_, w), BF16)
    out_blk = lambda w: pl.BlockSpec((1, tk, w), lambda b, i: (b, i, 0))
    return pl.pallas_call(
        _nsa_kv_prep_kernel,
        out_shape=(kv(LANES), kv(HEAD_DIM), kv(HEAD_DIM), kv(HEAD_DIM)),
        grid=(b_, nt),
        in_specs=[
            pl.BlockSpec((tk, LANES), lambda b, i: (b * nt + i, C_NSA_SLC // LANES)),
            pl.BlockSpec((tk, LANES), lambda b, i: (b * nt + i, C_NSA_WIN // LANES)),
            pl.BlockSpec((3, tk, LANES), lambda b, i: (0, i, 0)),
            pl.BlockSpec((1, LANES), lambda b, i: (0, 0)),
            pl.BlockSpec((1, LANES), lambda b, i: (0, 0)),
        ],
        out_specs=(out_blk(LANES), out_blk(HEAD_DIM), out_blk(HEAD_DIM), out_blk(HEAD_DIM)),
        compiler_params=pltpu.CompilerParams(
            dimension_semantics=("arbitrary", "arbitrary"),
            vmem_limit_bytes=V7X_VMEM_LIMIT_BYTES),
        name="nsa_kv_prep",
    )(cols, cols, rope_pair, gs, gw)


def _nsa_compress_kernel(r_ref, pea_ref, peb_ref, w1a_ref, w1b_ref, w2k_ref, w2v_ref, kg_ref,
                         kc_ref, vc_ref):
    r = r_ref[0]
    n = r.shape[0]
    a = _dot((r + pea_ref[...]).astype(BF16), w1a_ref[...])
    bm = _dot((r + peb_ref[...]).astype(BF16), w1b_ref[...])
    row = lax.broadcasted_iota(jnp.int32, bm.shape, 0)
    bm_up = jnp.where(row < n - 1, pltpu.roll(bm, n - 1, 0), 0.0)
    h = jnp.maximum(a + bm_up, 0.0).astype(BF16)
    hid = w2k_ref.shape[0]
    ck = _dot(h[:, :hid], w2k_ref[...])
    cv = _dot(h[:, hid:], w2v_ref[...])
    kc_ref[0] = _rms_rows(ck, kg_ref[...]).astype(BF16)
    vc_ref[0] = cv.astype(BF16)


def nsa_compress(cols, b_, l_, pos_k, pos_v, k_w1, k_w2, v_w1, v_w2, k_gain):
    rows = l_ // CMP_STRIDE
    hid = k_w1.shape[1]
    pair = cols[:, C_NSA_CMP:C_NSA_CMP + LANES].reshape(b_, rows, CMP_STRIDE * LANES)

    def interleave_pe(lo):
        pe = jnp.concatenate([pos_k[lo:lo + CMP_STRIDE], pos_v[lo:lo + CMP_STRIDE]], axis=1)
        return pe.reshape(1, CMP_STRIDE * LANES)

    def interleave_w(lo):
        wk = k_w1[lo * HEAD_DIM:(lo + CMP_STRIDE) * HEAD_DIM].reshape(CMP_STRIDE, HEAD_DIM, hid)
        wv = v_w1[lo * HEAD_DIM:(lo + CMP_STRIDE) * HEAD_DIM].reshape(CMP_STRIDE, HEAD_DIM, hid)
        z = jnp.zeros_like(wk)
        top = jnp.concatenate([wk, z], axis=2)
        bot = jnp.concatenate([z, wv], axis=2)
        return jnp.concatenate([top, bot], axis=1).reshape(CMP_STRIDE * LANES, 2 * hid).astype(BF16)

    out = jax.ShapeDtypeStruct((b_, rows, HEAD_DIM), BF16)
    return pl.pallas_call(
        _nsa_compress_kernel,
        out_shape=(out, out),
        grid=(b_,),
        in_specs=[
            pl.BlockSpec((1, rows, CMP_STRIDE * LANES), lambda b: (b, 0, 0)),
            _const_spec((1, CMP_STRIDE * LANES)),
            _const_spec((1, CMP_STRIDE * LANES)),
            _const_spec((CMP_STRIDE * LANES, 2 * hid)),
            _const_spec((CMP_STRIDE * LANES, 2 * hid)),
            _const_spec((hid, HEAD_DIM)),
            _const_spec((hid, HEAD_DIM)),
            _const_spec((1, HEAD_DIM)),
        ],
        out_specs=(pl.BlockSpec((1, rows, HEAD_DIM), lambda b: (b, 0, 0)),
                   pl.BlockSpec((1, rows, HEAD_DIM), lambda b: (b, 0, 0))),
        compiler_params=pltpu.CompilerParams(
            dimension_semantics=("arbitrary",), vmem_limit_bytes=V7X_VMEM_LIMIT_BYTES),
        name="nsa_compress",
    )(pair, interleave_pe(0), interleave_pe(CMP_STRIDE), interleave_w(0),
      interleave_w(CMP_STRIDE), k_w2.astype(BF16), v_w2.astype(BF16),
      k_gain.reshape(1, HEAD_DIM))


def _masked_softmax(s, valid, axis):
    m = jnp.max(jnp.where(valid, s, -jnp.inf), axis=axis, keepdims=True)
    m = jnp.where(m == -jnp.inf, 0.0, m)
    e = jnp.where(valid, jnp.exp(s - m), 0.0)
    den = jnp.sum(e, axis=axis, keepdims=True)
    return e / jnp.where(den > 0, den, 1.0)


def _nsa_select_kernel(q_ref, kc_ref, vc_ref, ovt_ref, rope_ref, qg_ref, qaug_ref, ocmp_ref,
                       imp_ref, *, n_sel):
    tq = q_ref.shape[0]
    ncr = kc_ref.shape[1]
    n_slc = ovt_ref.shape[0]
    t0 = pl.program_id(1) * tq
    scale = HEAD_DIM ** -0.5
    q = q_ref[...]
    g = qg_ref[...]
    kc = kc_ref[0]
    vc = vc_ref[0]
    last = CMP_BLOCK - 1
    vis = (lax.broadcasted_iota(jnp.int32, (tq, ncr), 1) * CMP_STRIDE + last
           <= t0 + lax.broadcasted_iota(jnp.int32, (tq, ncr), 0))
    vis_t = (lax.broadcasted_iota(jnp.int32, (ncr, tq), 0) * CMP_STRIDE + last
             <= t0 + lax.broadcasted_iota(jnp.int32, (ncr, tq), 1))
    qn_heads, o_heads = [], []
    psum_t = jnp.zeros((ncr, tq), F32)
    for h in range(GROUP_HEADS):
        qn = _rms_rows(q[:, h * HEAD_DIM:(h + 1) * HEAD_DIM], g)
        qn_heads.append(qn)
        qb = (qn * scale).astype(BF16)
        p = _masked_softmax(_dot_nt(qb, kc), vis, 1)
        o_heads.append(_dot(p.astype(BF16), vc))
        psum_t = psum_t + _masked_softmax(_dot_nt(kc, qb), vis_t, 0)
    ocmp_ref[0] = jnp.concatenate(o_heads, axis=-1)

    hi = psum_t.astype(BF16)
    lo = (psum_t - hi.astype(F32)).astype(BF16)
    imp = _dot(ovt_ref[...], hi) + _dot(ovt_ref[...], lo)
    blk = lax.broadcasted_iota(jnp.int32, (n_slc, tq), 0)
    cur = jnp.right_shift(t0 + lax.broadcasted_iota(jnp.int32, (n_slc, tq), 1), SLC_SHIFT)
    forced = (blk == 0) | (blk == cur) | (blk == cur - 1)
    imp = jnp.where(forced, jnp.inf, jnp.where(blk > cur, -jnp.inf, imp))
    imp_ref[...] = imp

    def count_beats(m, cnt):
        row = imp_ref[pl.ds(m, 1), :]
        beats = (row > imp) | ((row == imp) & (blk > m))
        return cnt + jnp.where(beats, 1.0, 0.0)

    cnt = lax.fori_loop(0, n_slc, count_beats, jnp.zeros((n_slc, tq), F32))
    mt = jnp.where(cnt < n_sel, 0.0, NEG_BIG)
    pad = jnp.zeros((HEAD_DIM - n_slc, tq), F32)
    mt = jnp.concatenate([mt, pad, mt, pad], axis=0) if n_slc < HEAD_DIM else jnp.concatenate(
        [mt, mt], axis=0)
    mt = mt.T

    qr = _apply_rope(jnp.concatenate(qn_heads, axis=-1), rope_ref) * scale
    lane = lax.broadcasted_iota(jnp.int32, (tq, LANES), 1)
    for j in range(GROUP_HEADS // 2):
        pair = qr[:, j * LANES:(j + 1) * LANES]
        swapped = pltpu.roll(pair, HEAD_DIM, 1)
        qaug_ref[0, :, (2 * j) * LANES:(2 * j + 1) * LANES] = jnp.where(
            lane < HEAD_DIM, pair, mt).astype(BF16)
        qaug_ref[0, :, (2 * j + 1) * LANES:(2 * j + 2) * LANES] = jnp.where(
            lane < HEAD_DIM, swapped, mt).astype(BF16)


def nsa_select(cols, b_, l_, k_cmp, v_cmp, rope_q, q_gain):
    tq = min(NSA_SEL_TQ, l_)
    nt = l_ // tq
    ncr = l_ // CMP_STRIDE
    n_slc = l_ // SLC_BLOCK
    n_sel = min(SLC_TOPN, n_slc)
    st_c = np.arange(ncr) * CMP_STRIDE
    st_s = np.arange(n_slc) * SLC_BLOCK
    ovt = ((st_c[None, :] < st_s[:, None] + SLC_BLOCK)
           & (st_c[None, :] + CMP_BLOCK > st_s[:, None])).astype(np.float32)
    return pl.pallas_call(
        functools.partial(_nsa_select_kernel, n_sel=n_sel),
        out_shape=(jax.ShapeDtypeStruct((b_, l_, GROUP_HEADS * LANES), BF16),
                   jax.ShapeDtypeStruct((b_, l_, GROUP_WIDTH), F32)),
        grid=(b_, nt),
        in_specs=[
            pl.BlockSpec((tq, GROUP_WIDTH), lambda b, i: (b * nt + i, C_NSA_Q // GROUP_WIDTH)),
            pl.BlockSpec((1, ncr, HEAD_DIM), lambda b, i: (b, 0, 0)),
            pl.BlockSpec((1, ncr, HEAD_DIM), lambda b, i: (b, 0, 0)),
            pl.BlockSpec((n_slc, ncr), lambda b, i: (0, 0)),
            pl.BlockSpec((3, tq, GROUP_WIDTH), lambda b, i: (0, i, 0)),
            pl.BlockSpec((1, HEAD_DIM), lambda b, i: (0, 0)),
        ],
        out_specs=(pl.BlockSpec((1, tq, GROUP_HEADS * LANES), lambda b, i: (b, i, 0)),
                   pl.BlockSpec((1, tq, GROUP_WIDTH), lambda b, i: (b, i, 0))),
        scratch_shapes=[pltpu.VMEM((n_slc, tq), F32)],
        compiler_params=pltpu.CompilerParams(
            dimension_semantics=("arbitrary", "arbitrary"),
            vmem_limit_bytes=V7X_VMEM_LIMIT_BYTES),
        name="nsa_select",
    )(cols, k_cmp, v_cmp, jnp.asarray(ovt, BF16), rope_q, q_gain.reshape(1, HEAD_DIM))


def _nsa_attend_kernel(qaug_ref, kaug_ref, vs_ref, kw_ref, vw_ref, ocmp_ref, gate_ref, o_ref):
    tq = qaug_ref.shape[1]
    nh = GROUP_HEADS
    ts = NSA_KEY_TILE
    t0 = pl.program_id(1) * tq
    qa = qaug_ref[0]
    qs = jnp.concatenate([qa[:, h * LANES:(h + 1) * LANES] for h in range(nh)], axis=0)
    qpos = t0 + lax.rem(lax.broadcasted_iota(jnp.int32, (nh * tq, 1), 0), tq)

    def key_tile(kt, carry):
        m, l, acc = carry
        k0 = pl.multiple_of(kt * ts, ts)
        s = _dot_nt(qs, kaug_ref[0, pl.ds(k0, ts), :])
        kpos = k0 + lax.broadcasted_iota(jnp.int32, (nh * tq, ts), 1)
        s = jnp.where(kpos <= qpos, s, NEG_BIG)
        m_new = jnp.maximum(m, jnp.max(s, axis=-1, keepdims=True))
        alpha = jnp.exp(m - m_new)
        p = jnp.exp(s - m_new)
        l = alpha * l + jnp.sum(p, axis=-1, keepdims=True)
        acc = alpha * acc + _dot(p.astype(BF16), vs_ref[0, pl.ds(k0, ts), :])
        return m_new, l, acc

    n_kt = (t0 + tq + ts - 1) // ts
    init = (jnp.full((nh * tq, 1), -jnp.inf, F32), jnp.zeros((nh * tq, 1), F32),
            jnp.zeros((nh * tq, HEAD_DIM), F32))
    _, l, acc = lax.fori_loop(0, n_kt, key_tile, init)
    o_slc = acc / l

    wlen = WINDOW + tq
    start = pl.multiple_of(jnp.maximum(t0 - WINDOW, 0), tq)
    sw = _dot_nt(qs[:, :HEAD_DIM], kw_ref[0, pl.ds(start, wlen), :])
    dist = qpos - (start + lax.broadcasted_iota(jnp.int32, (nh * tq, wlen), 1))
    valid = (dist >= 0) & (dist < WINDOW)
    sw = jnp.where(valid, sw, -jnp.inf)
    e = jnp.exp(sw - jnp.max(sw, axis=-1, keepdims=True))
    o_swa = _dot(e.astype(BF16), vw_ref[0, pl.ds(start, wlen), :]) / jnp.sum(
        e, axis=-1, keepdims=True)

    g = jax.nn.sigmoid(gate_ref[...])
    oc = ocmp_ref[0]
    outs = []
    for h in range(nh):
        rows = slice(h * tq, (h + 1) * tq)
        outs.append(g[:, h:h + 1] * oc[:, h * HEAD_DIM:(h + 1) * HEAD_DIM]
                    + g[:, nh + h:nh + h + 1] * o_slc[rows]
                    + g[:, 2 * nh + h:2 * nh + h + 1] * o_swa[rows])
    o_ref[...] = jnp.concatenate(outs, axis=-1)


def nsa_attend(cols, b_, l_, q_aug, k_aug, v_slc, k_win, v_win, o_cmp):
    tq = NSA_TQ
    nt = l_ // tq
    seq = lambda w: pl.BlockSpec((1, l_, w), lambda b, i: (b, 0, 0))
    return pl.pallas_call(
        _nsa_attend_kernel,
        out_shape=jax.ShapeDtypeStruct((b_ * l_, GROUP_WIDTH), F32),
        grid=(b_, nt),
        in_specs=[
            pl.BlockSpec((1, tq, GROUP_HEADS * LANES), lambda b, i: (b, i, 0)),
            seq(LANES), seq(HEAD_DIM), seq(HEAD_DIM), seq(HEAD_DIM),
            pl.BlockSpec((1, tq, GROUP_WIDTH), lambda b, i: (b, i, 0)),
            pl.BlockSpec((tq, LANES), lambda b, i: (b * nt + i, C_NSA_GATE // LANES)),
        ],
        out_specs=pl.BlockSpec((tq, GROUP_WIDTH), lambda b, i: (b * nt + i, 0)),
        compiler_params=pltpu.CompilerParams(
            dimension_semantics=("arbitrary", "arbitrary"),
            vmem_limit_bytes=V7X_VMEM_LIMIT_BYTES),
        name="nsa_attend",
    )(q_aug, k_aug, v_slc, k_win, v_win, o_cmp, cols)


def nsa_mixer_pallas(cols, b_, l_, rope, pos_k, pos_v, k_w1, k_w2, v_w1, v_w2, q_gain, k_gains):
    k_aug, v_slc, k_win, v_win = nsa_kv_prep(cols, b_, l_, rope_tables_kv_pair(rope),
                                             k_gains[1], k_gains[2])
    k_cmp, v_cmp = nsa_compress(cols, b_, l_, pos_k, pos_v, k_w1, k_w2, v_w1, v_w2, k_gains[0])
    q_aug, o_cmp = nsa_select(cols, b_, l_, k_cmp, v_cmp, jnp.tile(rope, (1, 1, GROUP_HEADS)),
                              q_gain)
    return nsa_attend(cols, b_, l_, q_aug, k_aug, v_slc, k_win, v_win, o_cmp)


DSA_TQ = 128
DSA_KEY_TILE = 256
DSA_PREP_TK = 512
IDX_PACK = LANES
INT_MIN = -2 ** 31
MASKED_SCORE = -1e30


def _split_hi_lo(t):
    hi = t.astype(BF16)
    lo = (t - hi.astype(F32)).astype(BF16)
    return hi, lo


def _placement(rows, cols, pairs):
    p = np.zeros((rows, cols), np.float32)
    for r, c in pairs:
        p[r, c] = 1.0
    return jnp.asarray(p, BF16)


def _dsa_kv_prep_kernel(ckv_ref, ik_ref, rope_ref, ropei_ref, kvg_ref, wkv_ref, kg_ref, ikg_ref,
                        pkh_ref, pkl_ref, k_ref, vt_ref, ik3_ref):
    tk = ckv_ref.shape[0]
    lane = lax.broadcasted_iota(jnp.int32, (tk, LANES), 1)
    ckv = _rms_rows(ckv_ref[...], kvg_ref[...]).astype(BF16)
    kv = _dot(ckv, wkv_ref[...])
    is_k = lane < HEAD_DIM
    ms = jnp.sum(jnp.where(is_k, kv * kv, 0.0), axis=-1, keepdims=True) * (1.0 / HEAD_DIM)
    y = _apply_rope(jnp.where(is_k, kv * lax.rsqrt(ms + EPS) * kg_ref[...], kv), rope_ref)
    k_ref[0] = y[:, :HEAD_DIM].astype(BF16)
    vt_ref[0] = y.T[HEAD_DIM:, :].astype(BF16)
    ik = ik_ref[...]
    ms = jnp.sum(ik * ik, axis=-1, keepdims=True) * (1.0 / IDX_DIM)
    ikn = _apply_rope(ik * lax.rsqrt(ms + EPS) * ikg_ref[...], ropei_ref, IDX_DIM // 8)
    hi, lo = _split_hi_lo(ikn)
    ik3_ref[0] = (_dot(hi, pkh_ref[...]) + _dot(lo, pkl_ref[...])).astype(BF16)


def dsa_kv_prep(cols, b_, l_, rope_pair, rope_idx, kv_gain, w_uk, w_uv, k_gain, idxk_gain):
    tk = DSA_PREP_TK
    nt = l_ // tk
    ones = jnp.ones((HEAD_DIM,), F32)
    kg = jnp.concatenate([k_gain, ones]).reshape(1, LANES)
    ikg = jnp.pad(idxk_gain, (0, LANES - IDX_DIM)).reshape(1, LANES)
    wkv = jnp.concatenate([w_uk, w_uv], axis=1).astype(BF16)
    d = range(IDX_DIM)
    pkh = _placement(LANES, IDX_PACK, [(i, i) for i in d] + [(i, 2 * IDX_DIM + i) for i in d])
    pkl = _placement(LANES, IDX_PACK, [(i, IDX_DIM + i) for i in d])
    const = lambda shape: pl.BlockSpec(shape, lambda b, i: (0,) * len(shape))
    out = lambda w: pl.BlockSpec((1, tk, w), lambda b, i: (b, i, 0))
    return pl.pallas_call(
        _dsa_kv_prep_kernel,
        out_shape=(jax.ShapeDtypeStruct((b_, l_, HEAD_DIM), BF16),
                   jax.ShapeDtypeStruct((b_, HEAD_DIM, l_), BF16),
                   jax.ShapeDtypeStruct((b_, l_, IDX_PACK), BF16)),
        grid=(b_, nt),
        in_specs=[
            pl.BlockSpec((tk, LANES), lambda b, i: (b * nt + i, C_DSA_CKV // LANES)),
            pl.BlockSpec((tk, LANES), lambda b, i: (b * nt + i, C_DSA_IK // LANES)),
            pl.BlockSpec((3, tk, LANES), lambda b, i: (0, i, 0)),
            pl.BlockSpec((3, tk, LANES), lambda b, i: (0, i, 0)),
            const((1, LANES)), const((DSA_LATENT, LANES)), const((1, LANES)), const((1, LANES)),
            const((LANES, IDX_PACK)), const((LANES, IDX_PACK)),
        ],
        out_specs=(out(HEAD_DIM), pl.BlockSpec((1, HEAD_DIM, tk), lambda b, i: (b, 0, i)),
                   out(IDX_PACK)),
        compiler_params=pltpu.CompilerParams(
            dimension_semantics=("arbitrary", "arbitrary"),
            vmem_limit_bytes=V7X_VMEM_LIMIT_BYTES),
        name="dsa_kv_prep",
    )(cols, cols, rope_pair, rope_idx, kv_gain.reshape(1, DSA_LATENT), wkv, kg, ikg, pkh, pkl)


def _dsa_q_prep_kernel(q_ref, iq_ref, rope_ref, ropei_ref, qg_ref, pqh_ref, pql_ref,
                       qh_ref, iq3_ref):
    g = qg_ref[...]
    q = q_ref[...]
    qn = jnp.concatenate([_rms_rows(q[:, h * HEAD_DIM:(h + 1) * HEAD_DIM], g)
                          for h in range(GROUP_HEADS)], axis=-1)
    qr = _apply_rope(qn, rope_ref) * (HEAD_DIM ** -0.5)
    for h in range(GROUP_HEADS):
        qh_ref[0, h] = qr[:, h * HEAD_DIM:(h + 1) * HEAD_DIM].astype(BF16)
    hi, lo = _split_hi_lo(_apply_rope(iq_ref[...], ropei_ref, IDX_DIM // 8))
    iq3_ref[0] = (_dot(hi, pqh_ref[...]) + _dot(lo, pql_ref[...])).astype(BF16)


def dsa_q_prep(cols, b_, l_, rope_q, rope_iq, q_gain):
    tq = 256
    nt = l_ // tq
    w = IDX_HEADS * IDX_DIM
    hd = [(h, i) for h in range(IDX_HEADS) for i in range(IDX_DIM)]
    pqh = _placement(w, IDX_HEADS * IDX_PACK,
                     [(IDX_DIM * h + i, IDX_PACK * h + i) for h, i in hd]
                     + [(IDX_DIM * h + i, IDX_PACK * h + IDX_DIM + i) for h, i in hd])
    pql = _placement(w, IDX_HEADS * IDX_PACK,
                     [(IDX_DIM * h + i, IDX_PACK * h + 2 * IDX_DIM + i) for h, i in hd])
    const = lambda shape: pl.BlockSpec(shape, lambda b, i: (0,) * len(shape))
    return pl.pallas_call(
        _dsa_q_prep_kernel,
        out_shape=(jax.ShapeDtypeStruct((b_, GROUP_HEADS, l_, HEAD_DIM), BF16),
                   jax.ShapeDtypeStruct((b_, l_, IDX_HEADS * IDX_PACK), BF16)),
        grid=(b_, nt),
        in_specs=[
            pl.BlockSpec((tq, GROUP_WIDTH), lambda b, i: (b * nt + i, C_DSA_Q // GROUP_WIDTH)),
            pl.BlockSpec((tq, w), lambda b, i: (b * nt + i, C_DSA_IQ // w)),
            pl.BlockSpec((3, tq, GROUP_WIDTH), lambda b, i: (0, i, 0)),
            pl.BlockSpec((3, tq, w), lambda b, i: (0, i, 0)),
            const((1, HEAD_DIM)), const(pqh.shape), const(pql.shape),
        ],
        out_specs=(pl.BlockSpec((1, GROUP_HEADS, tq, HEAD_DIM), lambda b, i: (b, 0, i, 0)),
                   pl.BlockSpec((1, tq, IDX_HEADS * IDX_PACK), lambda b, i: (b, i, 0))),
        compiler_params=pltpu.CompilerParams(
            dimension_semantics=("arbitrary", "arbitrary"),
            vmem_limit_bytes=V7X_VMEM_LIMIT_BYTES),
        name="dsa_q_prep",
    )(cols, cols, rope_q, rope_iq, q_gain.reshape(1, HEAD_DIM), pqh, pql)


def _dsa_attend_kernel(qh_ref, iq3_ref, iw_ref, k_ref, v_ref, ik3_ref, o_ref, sc_ref, *,
                       topk, idx_bits):
    tq = iq3_ref.shape[1]
    ts = DSA_KEY_TILE
    nh = GROUP_HEADS
    t0 = pl.program_id(1) * tq
    n_kt = (t0 + tq + ts - 1) // ts
    qpos = t0 + lax.broadcasted_iota(jnp.int32, (tq, 1), 0)

    iq3 = iq3_ref[0]
    lhs = jnp.concatenate([iq3[:, h * IDX_PACK:(h + 1) * IDX_PACK] for h in range(IDX_HEADS)],
                          axis=0)
    iw = iw_ref[...] * (IDX_HEADS ** -0.5 * IDX_DIM ** -0.5)
    w_col = jnp.concatenate([iw[:, h:h + 1] for h in range(IDX_HEADS)], axis=0)

    def score_tile(kt, _):
        k0 = pl.multiple_of(kt * ts, ts)
        rel = jnp.maximum(_dot_nt(lhs, ik3_ref[0, pl.ds(k0, ts), :]), 0.0) * w_col
        sc = rel[0:tq]
        for h in range(1, IDX_HEADS):
            sc = sc + rel[h * tq:(h + 1) * tq]
        kpos = k0 + lax.broadcasted_iota(jnp.int32, (tq, ts), 1)
        sc_ref[:, pl.ds(k0, ts)] = jnp.where(kpos <= qpos, sc, -jnp.inf)
        return 0

    lax.fori_loop(0, n_kt, score_tile, 0)

    def count(pred):
        def tile(kt, c):
            for j in range(ts // LANES):
                k0 = pl.multiple_of(kt * ts + j * LANES, LANES)
                c = c + jnp.where(pred(sc_ref[:, pl.ds(k0, LANES)], k0), 1.0, 0.0)
            return c
        c = lax.fori_loop(0, n_kt, tile, jnp.zeros((tq, LANES), F32))
        return jnp.broadcast_to(jnp.sum(c, axis=-1, keepdims=True), (tq, LANES))

    def key_to_float(key):
        return pltpu.bitcast(jnp.where(key >= 0, key, key ^ jnp.int32(0x7FFFFFFF)), F32)

    def value_bit(i, thr_key):
        cand = thr_key | jnp.left_shift(jnp.int32(1), 31 - i)
        cand_f = key_to_float(cand ^ jnp.int32(INT_MIN))
        return jnp.where(count(lambda sc, k0: sc >= cand_f) >= topk, cand, thr_key)

    thr_key = lax.fori_loop(0, 32, value_bit, jnp.zeros((tq, LANES), jnp.int32))
    thr = key_to_float(thr_key ^ jnp.int32(INT_MIN))
    n_ge = count(lambda sc, k0: sc >= thr)
    thr = jnp.where(n_ge >= topk, thr, -jnp.inf)
    need = topk - count(lambda sc, k0: sc > thr)
    tie_break = jnp.max(jnp.where((n_ge > topk) & (thr > -jnp.inf), 1.0, 0.0)) > 0.0

    def last_tie_position():
        def index_bit(i, last):
            cand = last | jnp.left_shift(jnp.int32(1), idx_bits - 1 - i)

            def tied_below(sc, k0):
                kpos = k0 + lax.broadcasted_iota(jnp.int32, (tq, LANES), 1)
                return (sc == thr) & (kpos < cand)

            return jnp.where(count(tied_below) < need, cand, last)

        return lax.fori_loop(0, idx_bits, index_bit, jnp.zeros((tq, LANES), jnp.int32))

    last = lax.cond(tie_break, last_tie_position,
                    lambda: jnp.full((tq, LANES), 2 ** idx_bits, jnp.int32))
    widen = lambda t: jnp.concatenate([t] * (ts // LANES), axis=1)
    thr2 = widen(thr)
    last2 = widen(last)

    qrow = t0 + lax.broadcasted_iota(jnp.int32, (tq, ts), 0)

    def lane_fold(t, op):
        out = t[:, :LANES]
        for j in range(1, ts // LANES):
            out = op(out, t[:, j * LANES:(j + 1) * LANES])
        return out

    def max_tile(kt, ms):
        k0 = pl.multiple_of(kt * ts, ts)
        sc = sc_ref[:, pl.ds(k0, ts)]
        kpos = k0 + lax.broadcasted_iota(jnp.int32, (tq, ts), 1)
        sel = ((sc > thr2) | ((sc == thr2) & (kpos <= last2))) & (kpos <= qrow)
        bias = jnp.where(sel, 0.0, MASKED_SCORE)
        sc_ref[:, pl.ds(k0, ts)] = bias
        k_t = k_ref[0, pl.ds(k0, ts), :]
        return tuple(jnp.maximum(ms[h], lane_fold(_dot_nt(qh_ref[0, h], k_t) + bias, jnp.maximum))
                     for h in range(nh))

    ms = lax.fori_loop(0, n_kt, max_tile,
                       tuple(jnp.full((tq, LANES), MASKED_SCORE, F32) for _ in range(nh)))
    ms = tuple(jnp.broadcast_to(jnp.max(m, axis=-1, keepdims=True), (tq, LANES)) for m in ms)

    def sum_tile(kt, carry):
        k0 = pl.multiple_of(kt * ts, ts)
        bias = sc_ref[:, pl.ds(k0, ts)]
        k_t = k_ref[0, pl.ds(k0, ts), :]
        v_t = v_ref[0, pl.ds(k0, ts), :]
        new = []
        for h in range(nh):
            l, acc = carry[h]
            p = jnp.exp(_dot_nt(qh_ref[0, h], k_t) + bias - widen(ms[h]))
            new.append((l + lane_fold(p, jnp.add), acc + _dot(p.astype(BF16), v_t)))
        return tuple(new)

    init = tuple((jnp.zeros((tq, LANES), F32), jnp.zeros((tq, HEAD_DIM), F32)) for _ in range(nh))
    heads = lax.fori_loop(0, n_kt, sum_tile, init)
    o_ref[...] = jnp.concatenate([acc / jnp.sum(l, axis=-1, keepdims=True) for l, acc in heads],
                                 axis=-1)


SUBLANES = 8


def _fold_rows(t, op):
    parts = [t[i * SUBLANES:(i + 1) * SUBLANES] for i in range(t.shape[0] // SUBLANES)]
    while len(parts) > 1:
        parts = [op(parts[i], parts[i + 1]) if i + 1 < len(parts) else parts[i]
                 for i in range(0, len(parts), 2)]
    return parts[0]


def _dsa_attend_t_kernel(qh_ref, iq3_ref, iw_ref, k_ref, vt_ref, ik3_ref, o_ref, sc_ref, *,
                         topk, idx_bits):
    tq = iq3_ref.shape[1]
    ts = DSA_KEY_TILE
    nh = GROUP_HEADS
    t0 = pl.program_id(1) * tq
    n_kt = (t0 + tq + ts - 1) // ts
    qpos = t0 + lax.broadcasted_iota(jnp.int32, (ts, tq), 1)
    krow = lax.broadcasted_iota(jnp.int32, (ts, tq), 0)

    iq3 = iq3_ref[0]
    iq_rows = jnp.concatenate([iq3[:, h * IDX_PACK:(h + 1) * IDX_PACK]
                               for h in range(IDX_HEADS)], axis=0)
    iw_t = (iw_ref[...] * (IDX_HEADS ** -0.5 * IDX_DIM ** -0.5)).T

    def score_tile(kt, _):
        k0 = pl.multiple_of(kt * ts, ts)
        rel = jnp.maximum(_dot_nt(ik3_ref[0, pl.ds(k0, ts), :], iq_rows), 0.0)
        sc = rel[:, 0:tq] * iw_t[0:1, :]
        for h in range(1, IDX_HEADS):
            sc = sc + rel[:, h * tq:(h + 1) * tq] * iw_t[h:h + 1, :]
        sc_ref[pl.ds(k0, ts), :] = jnp.where(k0 + krow <= qpos, sc, -jnp.inf)
        return 0

    lax.fori_loop(0, n_kt, score_tile, 0)

    def count(pred):
        def tile(kt, c):
            k0 = pl.multiple_of(kt * ts, ts)
            hit = jnp.where(pred(sc_ref[pl.ds(k0, ts), :], k0), 1.0, 0.0)
            return c + _fold_rows(hit, jnp.add)
        c = lax.fori_loop(0, n_kt, tile, jnp.zeros((SUBLANES, tq), F32))
        return jnp.sum(c, axis=0, keepdims=True)

    def key_to_float(key):
        return pltpu.bitcast(jnp.where(key >= 0, key, key ^ jnp.int32(0x7FFFFFFF)), F32)

    def value_bit(i, thr_key):
        cand = thr_key | jnp.left_shift(jnp.int32(1), 31 - i)
        cand_f = key_to_float(cand ^ jnp.int32(INT_MIN))
        return jnp.where(count(lambda sc, k0: sc >= cand_f) >= topk, cand, thr_key)

    thr_key = lax.fori_loop(0, 32, value_bit, jnp.zeros((1, tq), jnp.int32))
    thr = key_to_float(thr_key ^ jnp.int32(INT_MIN))
    n_ge = count(lambda sc, k0: sc >= thr)
    thr = jnp.where(n_ge >= topk, thr, -jnp.inf)
    need = topk - count(lambda sc, k0: sc > thr)
    tie_break = jnp.max(jnp.where((n_ge > topk) & (thr > -jnp.inf), 1.0, 0.0)) > 0.0

    def last_tie_position():
        def index_bit(i, last):
            cand = last | jnp.left_shift(jnp.int32(1), idx_bits - 1 - i)
            tied_below = lambda sc, k0: (sc == thr) & (k0 + krow < cand)
            return jnp.where(count(tied_below) < need, cand, last)

        return lax.fori_loop(0, idx_bits, index_bit, jnp.zeros((1, tq), jnp.int32))

    last = lax.cond(tie_break, last_tie_position,
                    lambda: jnp.full((1, tq), 2 ** idx_bits, jnp.int32))

    qs = qh_ref[0].reshape(nh * tq, HEAD_DIM)

    def key_tile(kt, carry):
        m, l, acc = carry
        k0 = pl.multiple_of(kt * ts, ts)
        sc = sc_ref[pl.ds(k0, ts), :]
        kpos = k0 + krow
        sel = ((sc > thr) | ((sc == thr) & (kpos <= last))) & (kpos <= qpos)
        bias = jnp.where(sel, 0.0, MASKED_SCORE)
        s = _dot_nt(k_ref[0, pl.ds(k0, ts), :], qs) + jnp.concatenate([bias] * nh, axis=1)
        m_new = jnp.maximum(m, jnp.max(s, axis=0, keepdims=True))
        alpha = jnp.exp(m - m_new)
        p = jnp.exp(s - m_new)
        l = alpha * l + jnp.sum(p, axis=0, keepdims=True)
        acc = alpha * acc + _dot(vt_ref[0, :, pl.ds(k0, ts)], p.astype(BF16))
        return m_new, l, acc

    init = (jnp.full((1, nh * tq), MASKED_SCORE, F32), jnp.zeros((1, nh * tq), F32),
            jnp.zeros((HEAD_DIM, nh * tq), F32))
    _, l, acc = lax.fori_loop(0, n_kt, key_tile, init)
    o_t = acc / l
    o_ref[...] = jnp.concatenate([o_t[:, h * tq:(h + 1) * tq] for h in range(nh)], axis=0).T


def dsa_attend(cols, b_, l_, qh, iq3, k, v, ik3):
    tq = DSA_TQ
    nt = l_ // tq
    topk = min(DSA_TOPK_MAX, l_ // 4)
    idx_bits = int(np.log2(l_))
    assert 2 ** idx_bits == l_ and l_ % DSA_KEY_TILE == 0 and topk <= DSA_KEY_TILE
    seq = lambda w: pl.BlockSpec((1, l_, w), lambda b, i: (b, 0, 0))
    return pl.pallas_call(
        functools.partial(_dsa_attend_t_kernel, topk=topk, idx_bits=idx_bits),
        out_shape=jax.ShapeDtypeStruct((b_ * l_, GROUP_WIDTH), F32),
        grid=(b_, nt),
        in_specs=[
            pl.BlockSpec((1, GROUP_HEADS, tq, HEAD_DIM), lambda b, i: (b, 0, i, 0)),
            pl.BlockSpec((1, tq, IDX_HEADS * IDX_PACK), lambda b, i: (b, i, 0)),
            pl.BlockSpec((tq, LANES), lambda b, i: (b * nt + i, C_DSA_IW // LANES)),
            seq(HEAD_DIM), pl.BlockSpec((1, HEAD_DIM, l_), lambda b, i: (b, 0, 0)), seq(IDX_PACK),
        ],
        out_specs=pl.BlockSpec((tq, GROUP_WIDTH), lambda b, i: (b * nt + i, 0)),
        scratch_shapes=[pltpu.VMEM((l_, tq), F32)],
        compiler_params=pltpu.CompilerParams(
            dimension_semantics=("arbitrary", "arbitrary"),
            vmem_limit_bytes=V7X_VMEM_LIMIT_BYTES),
        name="dsa_attend",
    )(qh, iq3, cols, k, v, ik3)


def dsa_mixer_pallas(cols, b_, l_, rope, kv_gain, w_uk, w_uv, q_gain, k_gain, idxk_gain):
    rope_i = rope_tables(l_, IDX_DIM)
    k, v, ik3 = dsa_kv_prep(cols, b_, l_, rope_tables_kv_pair(rope), rope_tables_pad(rope_i, LANES),
                            kv_gain, w_uk, w_uv, k_gain, idxk_gain)
    qh, iq3 = dsa_q_prep(cols, b_, l_, jnp.tile(rope, (1, 1, GROUP_HEADS)),
                         jnp.tile(rope_i, (1, 1, IDX_HEADS)), q_gain)
    return dsa_attend(cols, b_, l_, qh, iq3, k, v, ik3)


_TN = (((0,), (0,)), ((), ()))


def _dot_tn(a, b):
    return lax.dot_general(a, b, _TN, preferred_element_type=F32)


def _split3(t):
    hi = t.astype(BF16)
    r = t - hi.astype(F32)
    mid = r.astype(BF16)
    lo = (r - mid.astype(F32)).astype(BF16)
    return hi, mid, lo


def _tri_cumsum(tri, t):
    hi, mid, lo = _split3(t)
    return _dot(tri, hi) + _dot(tri, mid) + _dot(tri, lo)


def _cumsum_tri_rows(t, tri_u):
    hi, mid, lo = _split3(t)
    return _dot(hi, tri_u) + _dot(mid, tri_u) + _dot(lo, tri_u)


def _head_rms(o, gain):
    return jnp.concatenate([_rms_rows(o[:, h * HEAD_DIM:(h + 1) * HEAD_DIM], gain)
                            for h in range(GROUP_HEADS)], axis=-1)


HG_SUB = 8


def _hgrn2_kernel(q_ref, f_ref, i_ref, g_ref, lb_ref, gain_ref, tri_ref, ones_ref, bd_ref,
                  o_ref, st_ref, b_ref, kk_ref, v_ref):
    c = HG_CHUNK
    w = GROUP_WIDTH

    @pl.when(pl.program_id(1) == 0)
    def _():
        st_ref[...] = jnp.zeros_like(st_ref)

    lb = lb_ref[...]
    q = q_ref[...]
    qs = q * jax.nn.sigmoid(q) * (HEAD_DIM ** -0.5)
    forget = lb + (1.0 - lb) * jax.nn.sigmoid(f_ref[...])
    kk = 1.0 - forget
    bcum = _tri_cumsum(tri_ref[...], jnp.log(forget))
    v = i_ref[...]
    b_ref[...] = bcum
    kk_ref[...] = kk
    v_ref[...] = v

    out = _dot_nt((qs * jnp.exp(bcum)).astype(BF16), st_ref[...].astype(BF16))

    ones_bd = ones_ref[...]
    pieces = []
    for g in range(c // HG_SUB):
        r0 = g * HG_SUB
        nr = c - r0
        qg = qs[r0:, :]
        bg = bcum[r0:, :]
        trow = r0 + lax.broadcasted_iota(jnp.int32, (nr, w), 0)
        terms = []
        for j in range(HG_SUB):
            s = r0 + j
            d = qg * kk_ref[s:s + 1, :] * jnp.exp(bg - b_ref[s:s + 1, :])
            terms.append(jnp.where(trow >= s, d, 0.0).astype(BF16))
        red = _dot(jnp.concatenate(terms, axis=0), ones_bd)
        acc = red[0:nr] * v_ref[r0:r0 + 1, :]
        for j in range(1, HG_SUB):
            acc = acc + red[j * nr:(j + 1) * nr] * v_ref[r0 + j:r0 + j + 1, :]
        pieces.append(acc)
    intra = pieces[0]
    for g in range(1, c // HG_SUB):
        pad = jnp.zeros((g * HG_SUB, w), F32)
        intra = intra + jnp.concatenate([pad, pieces[g]], axis=0)
    out = out + intra

    b_last = bcum[c - 1:c, :]
    kt = (kk * jnp.exp(b_last - bcum)).astype(BF16)
    st_ref[...] = jnp.exp(b_last) * st_ref[...] + _dot_tn(v.astype(BF16), kt) * bd_ref[...]

    g_in = g_ref[...]
    o_ref[...] = _head_rms(out, gain_ref[...]) * (g_in * jax.nn.sigmoid(g_in))


def hgrn2_mixer_pallas(cols, b_, l_, lb, o_gain):
    c = HG_CHUNK
    nt = l_ // c
    w = GROUP_WIDTH
    head = np.arange(w) // HEAD_DIM
    same = (head[:, None] == head[None, :]).astype(np.float32)
    tri = np.tril(np.ones((c, c), np.float32))
    col = lambda j: pl.BlockSpec((c, w), lambda b, i: (b * nt + i, C_HG // w + j))
    const = lambda shape: pl.BlockSpec(shape, lambda b, i: (0,) * len(shape))
    return pl.pallas_call(
        _hgrn2_kernel,
        out_shape=jax.ShapeDtypeStruct((b_ * l_, w), F32),
        grid=(b_, nt),
        in_specs=[col(0), col(1), col(2), col(3), const((1, w)), const((1, HEAD_DIM)),
                  const((c, c)), const((w, w)), const((w, w))],
        out_specs=pl.BlockSpec((c, w), lambda b, i: (b * nt + i, 0)),
        scratch_shapes=[pltpu.VMEM((w, w), F32), pltpu.VMEM((c, w), F32),
                        pltpu.VMEM((c, w), F32), pltpu.VMEM((c, w), F32)],
        compiler_params=pltpu.CompilerParams(
            dimension_semantics=("arbitrary", "arbitrary"),
            vmem_limit_bytes=V7X_VMEM_LIMIT_BYTES),
        name="hgrn2",
    )(cols, cols, cols, cols, lb.reshape(1, w), o_gain.reshape(1, HEAD_DIM),
      jnp.asarray(tri, BF16), jnp.asarray(same, BF16), jnp.asarray(same, F32))


ML_TC = 256
ML_M_INIT = -1e30


def _mlstm_kernel(gate_ref, qk_ref, v_ref, og_ref, cw_ref, cb_ref, gb_ref, gain_ref, tril_ref,
                  triu_ref, o_ref, xprev_ref, cmat_ref, nvec_ref, m_ref):
    c = ML_TC
    nh = GROUP_HEADS
    w = GROUP_WIDTH

    @pl.when(pl.program_id(1) == 0)
    def _():
        xprev_ref[...] = jnp.zeros_like(xprev_ref)
        cmat_ref[...] = jnp.zeros_like(cmat_ref)
        nvec_ref[...] = jnp.zeros_like(nvec_ref)
        m_ref[...] = jnp.full(m_ref.shape, ML_M_INIT, F32)

    x = qk_ref[...]
    prev = xprev_ref[...]
    row = lax.broadcasted_iota(jnp.int32, x.shape, 0)
    acc = x * cw_ref[CONV_WIDTH - 1:CONV_WIDTH, :] + cb_ref[...]
    for j in range(1, CONV_WIDTH):
        shifted = jnp.where(row < j, pltpu.roll(prev, j, 0), pltpu.roll(x, j, 0))
        acc = acc + shifted * cw_ref[CONV_WIDTH - 1 - j:CONV_WIDTH - j, :]
    xprev_ref[...] = x
    qk = acc * jax.nn.sigmoid(acc)
    q = qk[:, :w]
    k = qk[:, w:] * (HEAD_DIM ** -0.5)
    v = v_ref[...]

    pre = gate_ref[...] + gb_ref[...]
    lane = lax.broadcasted_iota(jnp.int32, pre.shape, 1)
    log_f = jnp.minimum(pre, 0.0) - jnp.log1p(jnp.exp(-jnp.abs(pre)))
    log_f = jnp.where((lane >= nh) & (lane < 2 * nh), log_f, 0.0)
    bcum_c = _tri_cumsum(tril_ref[...], log_f)
    bcum_r = _cumsum_tri_rows(log_f.T, triu_ref[...])
    pre_r = pre.T
    tri = (lax.broadcasted_iota(jnp.int32, (c, c), 0) >= lax.broadcasted_iota(jnp.int32, (c, c), 1))

    outs = []
    for h in range(nh):
        sl = slice(h * HEAD_DIM, (h + 1) * HEAD_DIM)
        qh, kh, vh = q[:, sl], k[:, sl], v[:, sl]
        bc = bcum_c[:, nh + h:nh + h + 1]
        li_c = pre[:, h:h + 1]
        a_r = pre_r[h:h + 1, :] - bcum_r[nh + h:nh + h + 1, :]
        m_prev = m_ref[h:h + 1, 0:1]
        log_d = jnp.where(tri, bc + a_r, -jnp.inf)
        inter = bc + m_prev
        m_t = jnp.maximum(inter, jnp.max(log_d, axis=-1, keepdims=True))
        d_mat = jnp.exp(log_d - m_t)
        w_inter = jnp.exp(inter - m_t)
        qb = qh.astype(BF16)
        s = _dot_nt(qb, kh.astype(BF16)) * d_mat
        num = w_inter * _dot(qb, cmat_ref[h].astype(BF16)) + _dot(s.astype(BF16), vh.astype(BF16))
        den = (w_inter * jnp.sum(qh * nvec_ref[h], axis=-1, keepdims=True)
               + jnp.sum(s, axis=-1, keepdims=True))
        outs.append(num / jnp.maximum(jnp.abs(den), jnp.exp(-m_t)))
        b_last = bc[c - 1:c, :]
        log_w = b_last + (li_c - bc)
        m_new = jnp.maximum(b_last + m_prev, jnp.max(log_w, axis=0, keepdims=True))
        kw = kh * jnp.exp(log_w - m_new)
        decay = jnp.exp(b_last + m_prev - m_new)
        cmat_ref[h] = decay * cmat_ref[h] + _dot_tn(kw.astype(BF16), vh.astype(BF16))
        nvec_ref[h] = decay * nvec_ref[h] + jnp.sum(kw, axis=0, keepdims=True)
        m_ref[h:h + 1, :] = jnp.broadcast_to(m_new, (1, LANES))

    hh = _head_rms(jnp.concatenate(outs, axis=-1), gain_ref[...])
    o_ref[...] = hh * jax.nn.sigmoid(og_ref[...])


def mlstm_mixer_pallas(cols, b_, l_, conv_w, conv_b, i_bias, f_bias, o_gain):
    c = ML_TC
    nt = l_ // c
    w = GROUP_WIDTH
    gb = jnp.pad(jnp.concatenate([i_bias, f_bias]), (0, LANES - 2 * GROUP_HEADS)).reshape(1, LANES)
    tril = np.tril(np.ones((c, c), np.float32))
    const = lambda shape: pl.BlockSpec(shape, lambda b, i: (0,) * len(shape))
    blk = lambda width, off: pl.BlockSpec((c, width), lambda b, i: (b * nt + i, off // width))
    return pl.pallas_call(
        _mlstm_kernel,
        out_shape=jax.ShapeDtypeStruct((b_ * l_, w), F32),
        grid=(b_, nt),
        in_specs=[blk(LANES, C_ML_GATE), blk(2 * w, C_ML_QK), blk(w, C_ML_V), blk(w, C_ML_OG),
                  const((CONV_WIDTH, 2 * w)), const((1, 2 * w)), const((1, LANES)),
                  const((1, HEAD_DIM)), const((c, c)), const((c, c))],
        out_specs=pl.BlockSpec((c, w), lambda b, i: (b * nt + i, 0)),
        scratch_shapes=[pltpu.VMEM((c, 2 * w), F32),
                        pltpu.VMEM((GROUP_HEADS, HEAD_DIM, HEAD_DIM), F32),
                        pltpu.VMEM((GROUP_HEADS, 1, HEAD_DIM), F32),
                        pltpu.VMEM((8, LANES), F32)],
        compiler_params=pltpu.CompilerParams(
            dimension_semantics=("arbitrary", "arbitrary"),
            vmem_limit_bytes=V7X_VMEM_LIMIT_BYTES),
        name="mlstm",
    )(cols, cols, cols, cols, conv_w, conv_b.reshape(1, 2 * w), gb, o_gain.reshape(1, HEAD_DIM),
      jnp.asarray(tril, BF16), jnp.asarray(tril.T, BF16))


def split_cols(t, sizes):
    return jnp.split(t, [int(s) for s in np.cumsum(sizes)[:-1]], axis=-1)


def split_heads(t, n):
    return t.reshape(t.shape[:-1] + (n, t.shape[-1] // n))


def rms_norm(t, g):
    t32 = t.astype(F32)
    y = t32 * lax.rsqrt(jnp.mean(t32 * t32, axis=-1, keepdims=True) + EPS)
    return (y * g.astype(F32)).astype(t.dtype)


def partial_rope(t, pos):
    d = t.shape[-1]
    rd = d // 4
    half = rd // 2
    inv = ROPE_THETA ** (-jnp.arange(half, dtype=F32) * 2.0 / rd)
    ang = pos.astype(F32)[:, None] * inv[None, :]
    cos = jnp.cos(ang)[:, None, :].astype(t.dtype)
    sin = jnp.sin(ang)[:, None, :].astype(t.dtype)
    x1, x2 = t[..., :half], t[..., half:rd]
    return jnp.concatenate([x1 * cos - x2 * sin, x2 * cos + x1 * sin, t[..., rd:]], axis=-1)


def masked_softmax(s, mask):
    s = jnp.where(mask, s.astype(F32), -jnp.inf)
    m = jnp.max(s, axis=-1, keepdims=True)
    m = jnp.where(jnp.isfinite(m), m, 0.0)
    e = jnp.exp(s - m)
    den = jnp.sum(e, axis=-1, keepdims=True)
    return e / jnp.where(den > 0, den, 1.0)


def gather_rows(table, idx):
    return jax.vmap(lambda tb, ix: tb[ix])(table, idx)


def causal_conv(t, w, b):
    y = lax.conv_general_dilated(t, w[:, None, :].astype(t.dtype), window_strides=(1,),
                                 padding=[(CONV_WIDTH - 1, 0)],
                                 dimension_numbers=('NWC', 'WIO', 'NWC'),
                                 feature_group_count=t.shape[-1])
    return y + b.astype(t.dtype)


def to_chunks(t, c):
    b_, l_, h_ = t.shape[:3]
    t = t.reshape((b_, l_ // c, c, h_) + t.shape[3:])
    return jnp.moveaxis(t, (1, 3), (0, 2))


def from_chunks(t):
    t = jnp.moveaxis(t, (0, 2), (1, 3))
    return t.reshape((t.shape[0], t.shape[1] * t.shape[2], t.shape[3]) + t.shape[4:])


def unblock(o):
    o = jnp.moveaxis(o, 0, 1)
    return o.reshape((o.shape[0], o.shape[1] * o.shape[2]) + o.shape[3:])


def hgrn2_mixer(cols, lb, o_gain):
    b_, l_ = cols.shape[:2]
    q, f, i, g = split_cols(cols, HG_SPLITS)
    q = split_heads(jax.nn.silu(q), GROUP_HEADS).astype(F32) * HEAD_DIM ** -0.5
    forget = lb + (1.0 - lb) * jax.nn.sigmoid(f.astype(F32))
    k = split_heads(1.0 - forget, GROUP_HEADS)
    logf = split_heads(jnp.log(forget), GROUP_HEADS)
    v = split_heads(i, GROUP_HEADS).astype(F32)
    tri = jnp.tril(jnp.ones((HG_CHUNK, HG_CHUNK), bool))[:, :, None]

    def step(state, inp):
        qc, kc, vc, gc = inp
        bcum = jnp.cumsum(gc, axis=2)
        o_inter = jnp.einsum('bhtk,bhkv->bhtv', qc * jnp.exp(bcum), state)
        diff = bcum[:, :, :, None, :] - bcum[:, :, None, :, :]
        decay = jnp.exp(jnp.where(tri, diff, -jnp.inf))
        attn = jnp.einsum('bhtk,bhsk,bhtsk->bhts', qc, kc, decay)
        out = o_inter + jnp.einsum('bhts,bhsv->bhtv', attn, vc)
        b_last = bcum[:, :, -1:, :]
        state = (jnp.exp(b_last[:, :, 0, :])[..., None] * state
                 + jnp.einsum('bhsk,bhsv->bhkv', kc * jnp.exp(b_last - bcum), vc))
        return state, out

    s0 = jnp.zeros((b_, GROUP_HEADS, HEAD_DIM, HEAD_DIM), F32)
    xs = (to_chunks(q, HG_CHUNK), to_chunks(k, HG_CHUNK), to_chunks(v, HG_CHUNK),
          to_chunks(logf, HG_CHUNK))
    _, o = lax.scan(step, s0, xs)
    o = rms_norm(from_chunks(o), o_gain).astype(cols.dtype)
    o = o * jax.nn.silu(split_heads(g, GROUP_HEADS))
    return o.reshape(b_, l_, GROUP_WIDTH)


def dsa_mixer(cols, pos, kv_gain, w_uk, w_uv, q_gain, k_gain, idxk_gain):
    b_, l_ = cols.shape[:2]
    q, ckv, iq, ik, iw = split_cols(cols, DSA_SPLITS)
    q = partial_rope(rms_norm(split_heads(q, GROUP_HEADS), q_gain), pos)
    ckv = rms_norm(ckv, kv_gain)
    k = partial_rope(rms_norm(ckv @ w_uk, k_gain)[:, :, None, :], pos)[:, :, 0]
    v = ckv @ w_uv
    iq = partial_rope(split_heads(iq, IDX_HEADS), pos)
    ik = partial_rope(rms_norm(ik, idxk_gain)[:, :, None, :], pos)[:, :, 0]
    iw = iw * (IDX_HEADS ** -0.5 * IDX_DIM ** -0.5)
    topk = min(DSA_TOPK_MAX, l_ // 4)
    key_pos = jnp.arange(l_)

    def block(bi):
        t0 = bi * Q_BLOCK
        qpos = t0 + jnp.arange(Q_BLOCK)
        qb = lax.dynamic_slice_in_dim(q, t0, Q_BLOCK, axis=1)
        iqb = lax.dynamic_slice_in_dim(iq, t0, Q_BLOCK, axis=1)
        iwb = lax.dynamic_slice_in_dim(iw, t0, Q_BLOCK, axis=1)
        rel = jax.nn.relu(jnp.einsum('bthd,bsd->bhts', iqb, ik))
        score = jnp.einsum('bhts,bth->bts', rel, iwb).astype(F32)
        causal = key_pos[None, :] <= qpos[:, None]
        score = jnp.where(causal[None], score, -jnp.inf)
        _, idx = lax.top_k(score, topk)
        k_sel = gather_rows(k, idx)
        v_sel = gather_rows(v, idx)
        s = jnp.einsum('bthd,btkd->bhtk', qb, k_sel) * HEAD_DIM ** -0.5
        p = masked_softmax(s, (idx <= qpos[None, :, None])[:, None]).astype(v.dtype)
        return jnp.einsum('bhtk,btkd->bthd', p, v_sel)

    o = unblock(lax.map(block, jnp.arange(l_ // Q_BLOCK)))
    return o.reshape(b_, l_, GROUP_WIDTH)


def nsa_mixer(cols, pos, pos_k, pos_v, k_w1, k_w2, v_w1, v_w2, q_gain, k_gains):
    b_, l_ = cols.shape[:2]
    scale = HEAD_DIM ** -0.5
    q, kc, vc, ks, vs, kw, vw, gates = split_cols(cols, NSA_SPLITS)
    q = rms_norm(split_heads(q, GROUP_HEADS), q_gain)
    q_rot = partial_rope(q, pos)
    gates = jax.nn.sigmoid(gates.reshape(b_, l_, 3, GROUP_HEADS, 1))

    n_cmp = (l_ - CMP_BLOCK) // CMP_STRIDE + 1
    cmp_idx = np.arange(n_cmp)[:, None] * CMP_STRIDE + np.arange(CMP_BLOCK)[None, :]

    def compress(t, pe, w1, w2):
        blk = (t[:, cmp_idx] + pe).reshape(b_, n_cmp, CMP_BLOCK * HEAD_DIM)
        return jax.nn.relu(blk @ w1) @ w2

    k_cmp = rms_norm(compress(kc, pos_k, k_w1, k_w2), k_gains[0])
    v_cmp = compress(vc, pos_v, v_w1, v_w2)
    cmp_vis = cmp_idx[:, -1][None, :] <= np.arange(l_)[:, None]
    p_cmp = masked_softmax(jnp.einsum('bthd,bjd->bhtj', q, k_cmp) * scale, cmp_vis)
    o_cmp = jnp.einsum('bhtj,bjd->bthd', p_cmp.astype(v_cmp.dtype), v_cmp)

    n_slc = l_ // SLC_BLOCK
    n_sel = min(SLC_TOPN, n_slc)
    st_c = np.arange(n_cmp) * CMP_STRIDE
    st_s = np.arange(n_slc) * SLC_BLOCK
    overlap = ((st_c[:, None] < st_s[None, :] + SLC_BLOCK)
               & (st_c[:, None] + CMP_BLOCK > st_s[None, :])).astype(np.float32)
    imp = jnp.einsum('bhtj,jn->btn', p_cmp, overlap)
    cur = np.arange(l_)[:, None] // SLC_BLOCK
    blk_id = np.arange(n_slc)[None, :]
    forced = (blk_id == 0) | (blk_id == cur) | (blk_id == cur - 1)
    imp = jnp.where(forced, jnp.inf, jnp.where(blk_id > cur, -jnp.inf, imp))
    _, sel = lax.top_k(imp, n_sel)

    k_s = partial_rope(rms_norm(ks, k_gains[1])[:, :, None], pos)[:, :, 0]
    k_blocks = k_s.reshape(b_, n_slc, SLC_BLOCK, HEAD_DIM)
    v_blocks = vs.reshape(b_, n_slc, SLC_BLOCK, HEAD_DIM)
    k_w = partial_rope(rms_norm(kw, k_gains[2])[:, :, None], pos)[:, :, 0]
    k_pad = jnp.pad(k_w, ((0, 0), (WINDOW, 0), (0, 0)))
    v_pad = jnp.pad(vw, ((0, 0), (WINDOW, 0), (0, 0)))

    def block(bi):
        t0 = bi * Q_BLOCK
        qpos = t0 + jnp.arange(Q_BLOCK)
        qb = lax.dynamic_slice_in_dim(q_rot, t0, Q_BLOCK, axis=1)
        sb = lax.dynamic_slice_in_dim(sel, t0, Q_BLOCK, axis=1)
        kb = gather_rows(k_blocks, sb)
        vb = gather_rows(v_blocks, sb)
        kpos = sb[..., None] * SLC_BLOCK + jnp.arange(SLC_BLOCK)
        s = jnp.einsum('bthd,btnsd->bhtns', qb, kb) * scale
        valid = (kpos <= qpos[None, :, None, None])[:, None]
        p = masked_softmax(s.reshape(b_, GROUP_HEADS, Q_BLOCK, n_sel * SLC_BLOCK),
                           valid.reshape(b_, 1, Q_BLOCK, n_sel * SLC_BLOCK))
        p = p.reshape(b_, GROUP_HEADS, Q_BLOCK, n_sel, SLC_BLOCK).astype(vb.dtype)
        o_slc = jnp.einsum('bhtns,btnsd->bthd', p, vb)
        kwb = lax.dynamic_slice_in_dim(k_pad, t0, WINDOW + Q_BLOCK, axis=1)
        vwb = lax.dynamic_slice_in_dim(v_pad, t0, WINDOW + Q_BLOCK, axis=1)
        wpos = t0 - WINDOW + jnp.arange(WINDOW + Q_BLOCK)
        dist = qpos[:, None] - wpos[None, :]
        wvalid = (dist >= 0) & (dist < WINDOW) & (wpos[None, :] >= 0)
        sw = jnp.einsum('bthd,bsd->bhts', qb, kwb) * scale
        pw = masked_softmax(sw, wvalid[None, None]).astype(vwb.dtype)
        o_swa = jnp.einsum('bhts,bsd->bthd', pw, vwb)
        return o_slc, o_swa

    o_slc, o_swa = lax.map(block, jnp.arange(l_ // Q_BLOCK))
    o = gates[:, :, 0] * o_cmp + gates[:, :, 1] * unblock(o_slc) + gates[:, :, 2] * unblock(o_swa)
    return o.reshape(b_, l_, GROUP_WIDTH)


def mlstm_mixer(cols, conv_w, conv_b, i_bias, f_bias, o_gain):
    b_, l_ = cols.shape[:2]
    qk, v, og, ig, fg = split_cols(cols, ML_SPLITS)
    qk = jax.nn.silu(causal_conv(qk, conv_w, conv_b))
    q, k = jnp.split(qk, 2, axis=-1)
    q = split_heads(q, GROUP_HEADS).astype(F32)
    k = split_heads(k, GROUP_HEADS).astype(F32) * HEAD_DIM ** -0.5
    v = split_heads(v, GROUP_HEADS).astype(F32)
    log_i = (ig + i_bias).astype(F32)
    log_f = jax.nn.log_sigmoid((fg + f_bias).astype(F32))
    tri = jnp.tril(jnp.ones((ML_CHUNK, ML_CHUNK), bool))

    def step(carry, inp):
        cmat, nvec, m = carry
        qc, kc, vc, li, lf = inp
        bcum = jnp.cumsum(lf, axis=-1)
        log_d = jnp.where(tri, bcum[..., :, None] - bcum[..., None, :] + li[..., None, :], -jnp.inf)
        inter = bcum + m[..., None]
        m_t = jnp.maximum(inter, jnp.max(log_d, axis=-1))
        d_mat = jnp.exp(log_d - m_t[..., None])
        w_inter = jnp.exp(inter - m_t)
        s = jnp.einsum('bhtd,bhsd->bhts', qc, kc) * d_mat
        num = (w_inter[..., None] * jnp.einsum('bhtd,bhdv->bhtv', qc, cmat)
               + jnp.einsum('bhts,bhsv->bhtv', s, vc))
        den = w_inter * jnp.einsum('bhtd,bhd->bht', qc, nvec) + jnp.sum(s, axis=-1)
        h = num / jnp.maximum(jnp.abs(den), jnp.exp(-m_t))[..., None]
        b_last = bcum[..., -1]
        log_w = b_last[..., None] - bcum + li
        m_new = jnp.maximum(b_last + m, jnp.max(log_w, axis=-1))
        w_s = jnp.exp(log_w - m_new[..., None])
        decay = jnp.exp(b_last + m - m_new)
        cmat = decay[..., None, None] * cmat + jnp.einsum('bhs,bhsd,bhsv->bhdv', w_s, kc, vc)
        nvec = decay[..., None] * nvec + jnp.einsum('bhs,bhsd->bhd', w_s, kc)
        return (cmat, nvec, m_new), h

    init = (jnp.zeros((b_, GROUP_HEADS, HEAD_DIM, HEAD_DIM), F32),
            jnp.zeros((b_, GROUP_HEADS, HEAD_DIM), F32),
            jnp.full((b_, GROUP_HEADS), -1e30, F32))
    xs = (to_chunks(q, ML_CHUNK), to_chunks(k, ML_CHUNK), to_chunks(v, ML_CHUNK),
          to_chunks(log_i, ML_CHUNK), to_chunks(log_f, ML_CHUNK))
    _, h = lax.scan(step, init, xs)
    h = rms_norm(from_chunks(h), o_gain).astype(cols.dtype)
    h = h * jax.nn.sigmoid(split_heads(og, GROUP_HEADS))
    return h.reshape(b_, l_, GROUP_WIDTH)


def kernel(x, mem, lb_param, norm_mix, w_in, w_out, hg_o_gain, dsa_kv_gain, dsa_w_uk, dsa_w_uv,
           dsa_q_gain, dsa_k_gain, dsa_idxk_gain, nsa_pos_k, nsa_pos_v, nsa_k_w1, nsa_k_w2,
           nsa_v_w1, nsa_v_w2, nsa_q_gain, nsa_k_gains, ml_conv_w, ml_conv_b, ml_i_bias,
           ml_f_bias, ml_o_gain, norm_xa, norm_mem, xa_wq, xa_wkv, xa_wo, xa_q_gain, xa_k_gain,
           norm_ffn, ffn_w13, ffn_w2):
    b_, l_, d = x.shape
    pos = jnp.arange(l_)
    lb_all = jnp.cumsum(jax.nn.softmax(lb_param.astype(F32), axis=0), axis=0)
    lb_all = lb_all - lb_all[:1]
    x2d = x.reshape(b_ * l_, d)
    mem2d = mem.reshape(b_ * N_MEM, d)
    rope = rope_tables(l_)
    for l in range(DEPTH):
        cols = norm_matmul(x2d, norm_mix[l], pack_w_in(w_in[l]).astype(BF16), tm=256)
        mixed = jnp.concatenate([
            hgrn2_mixer_pallas(cols, b_, l_, lb_all[l], hg_o_gain[l]),
            dsa_mixer_pallas(cols, b_, l_, rope, dsa_kv_gain[l], dsa_w_uk[l], dsa_w_uv[l],
                             dsa_q_gain[l], dsa_k_gain[l], dsa_idxk_gain[l]),
            nsa_mixer_pallas(cols, b_, l_, rope, nsa_pos_k[l], nsa_pos_v[l], nsa_k_w1[l],
                             nsa_k_w2[l], nsa_v_w1[l], nsa_v_w2[l], nsa_q_gain[l],
                             nsa_k_gains[l]),
            mlstm_mixer_pallas(cols, b_, l_, ml_conv_w[l], ml_conv_b[l], ml_i_bias[l],
                               ml_f_bias[l], ml_o_gain[l]),
        ], axis=-1)
        k_mem, v_mem = mem_kv(mem2d, norm_mem[l], xa_wkv[l].astype(BF16), xa_k_gain[l])
        x2d = post_block(x2d, mixed.reshape(b_ * l_, -1), l_, w_out[l].astype(BF16), norm_xa[l],
                         xa_wq[l].astype(BF16), xa_q_gain[l], k_mem, v_mem,
                         xa_wo[l].astype(BF16), norm_ffn[l], ffn_w13[l].astype(BF16),
                         ffn_w2[l].astype(BF16), tm=256)
    return x2d.reshape(b_, l_, d)
```

```python
import functools

import jax
import jax.numpy as jnp
from jax import lax
import numpy as np
from jax.experimental import pallas as pl
from jax.experimental.pallas import tpu as pltpu

F32 = jnp.float32
BF16 = jnp.bfloat16

D_MODEL = 1024
DEPTH = 2
HEAD_DIM = 64
GROUP_HEADS = 4
GROUP_WIDTH = GROUP_HEADS * HEAD_DIM
ROPE_THETA = 500000.0
EPS = 1e-6
Q_BLOCK = 128
N_MEM = 256
XA_HEADS = 4
XA_WIDTH = XA_HEADS * HEAD_DIM
HG_CHUNK = 64
DSA_LATENT = 128
IDX_HEADS = 8
IDX_DIM = 32
DSA_TOPK_MAX = 256
CMP_BLOCK = 32
CMP_STRIDE = 16
SLC_BLOCK = 64
SLC_SHIFT = 6
SLC_TOPN = 16
WINDOW = 512
ML_CHUNK = 64
CONV_WIDTH = 4
D_FF = 2816

HG_SPLITS = (GROUP_WIDTH,) * 4
DSA_SPLITS = (GROUP_WIDTH, DSA_LATENT, IDX_HEADS * IDX_DIM, IDX_DIM, IDX_HEADS)
NSA_SPLITS = (GROUP_WIDTH,) + (HEAD_DIM,) * 6 + (3 * GROUP_HEADS,)
ML_SPLITS = (2 * GROUP_WIDTH, GROUP_WIDTH, GROUP_WIDTH, GROUP_HEADS, GROUP_HEADS)
GROUP_COLS = (sum(HG_SPLITS), sum(DSA_SPLITS), sum(NSA_SPLITS), sum(ML_SPLITS))
IN_COLS = sum(GROUP_COLS)

V7X_VMEM_LIMIT_BYTES = 56 * 1024 * 1024
LANES = 128
FF_CHUNK = 256


def _round_up(n, m):
    return -(-n // m) * m


def _rms_rows(t, g):
    return t * lax.rsqrt(jnp.mean(t * t, axis=-1, keepdims=True) + EPS) * g


def _const_spec(shape):
    return pl.BlockSpec(shape, lambda *_: (0,) * len(shape), pipeline_mode=pl.Buffered(1))


def _norm_matmul_kernel(x_ref, g_ref, w_ref, o_ref):
    h = _rms_rows(x_ref[...], g_ref[...]).astype(BF16)
    o_ref[...] = jnp.dot(h, w_ref[...], preferred_element_type=F32)


def norm_matmul(x2d, gain, w_bf16, tm):
    m, k = x2d.shape
    n = w_bf16.shape[1]
    return pl.pallas_call(
        _norm_matmul_kernel,
        out_shape=jax.ShapeDtypeStruct((m, n), F32),
        grid=(m // tm,),
        in_specs=[
            pl.BlockSpec((tm, k), lambda i: (i, 0)),
            _const_spec((1, k)),
            _const_spec((k, n)),
        ],
        out_specs=pl.BlockSpec((tm, n), lambda i: (i, 0)),
        compiler_params=pltpu.CompilerParams(
            dimension_semantics=("arbitrary",), vmem_limit_bytes=V7X_VMEM_LIMIT_BYTES),
        name="norm_matmul",
    )(x2d, gain.reshape(1, k), w_bf16)


def _mem_kv_kernel(m_ref, g_ref, w_ref, kg_ref, k_ref, v_ref):
    mn = _rms_rows(m_ref[...], g_ref[...]).astype(BF16)
    kv = jnp.dot(mn, w_ref[...], preferred_element_type=F32)
    kg = kg_ref[...]
    ks = []
    for h in range(XA_HEADS):
        kh = kv[:, h * HEAD_DIM:(h + 1) * HEAD_DIM]
        ks.append(_rms_rows(kh, kg))
    k_ref[...] = jnp.concatenate(ks, axis=-1).astype(BF16)
    v_ref[...] = kv[:, XA_WIDTH:].astype(BF16)


def mem_kv(mem2d, gain, wkv_bf16, k_gain):
    m, k = mem2d.shape
    return pl.pallas_call(
        _mem_kv_kernel,
        out_shape=(jax.ShapeDtypeStruct((m, XA_WIDTH), BF16),
                   jax.ShapeDtypeStruct((m, XA_WIDTH), BF16)),
        grid=(m // N_MEM,),
        in_specs=[
            pl.BlockSpec((N_MEM, k), lambda i: (i, 0)),
            _const_spec((1, k)),
            _const_spec((k, 2 * XA_WIDTH)),
            _const_spec((1, HEAD_DIM)),
        ],
        out_specs=(pl.BlockSpec((N_MEM, XA_WIDTH), lambda i: (i, 0)),
                   pl.BlockSpec((N_MEM, XA_WIDTH), lambda i: (i, 0))),
        compiler_params=pltpu.CompilerParams(
            dimension_semantics=("arbitrary",), vmem_limit_bytes=V7X_VMEM_LIMIT_BYTES),
        name="mem_kv",
    )(mem2d, gain.reshape(1, k), wkv_bf16, k_gain.reshape(1, HEAD_DIM))


def _post_kernel(x_ref, mix_ref, wout_ref, gxa_ref, wq_ref, qg_ref, k_ref, v_ref, wo_ref,
                 gffn_ref, w13_ref, w2_ref, o_ref):
    x = x_ref[...] + jnp.dot(mix_ref[...].astype(BF16), wout_ref[...],
                             preferred_element_type=F32)
    h = _rms_rows(x, gxa_ref[...]).astype(BF16)
    q = jnp.dot(h, wq_ref[...], preferred_element_type=F32)
    qg = qg_ref[...] * (HEAD_DIM ** -0.5)
    k = k_ref[...]
    v = v_ref[...]
    outs = []
    for hd in range(XA_HEADS):
        sl = slice(hd * HEAD_DIM, (hd + 1) * HEAD_DIM)
        qh = _rms_rows(q[:, sl], qg).astype(BF16)
        s = lax.dot_general(qh, k[:, sl], (((1,), (1,)), ((), ())),
                            preferred_element_type=F32)
        e = jnp.exp(s - jnp.max(s, axis=-1, keepdims=True))
        p = e / jnp.sum(e, axis=-1, keepdims=True)
        outs.append(jnp.dot(p.astype(BF16), v[:, sl], preferred_element_type=F32))
    o = jnp.concatenate(outs, axis=-1).astype(BF16)
    x = x + jnp.dot(o, wo_ref[...], preferred_element_type=F32)
    h = _rms_rows(x, gffn_ref[...]).astype(BF16)
    acc = x
    for c in range(D_FF // FF_CHUNK):
        a = jnp.dot(h, w13_ref[:, c * FF_CHUNK:(c + 1) * FF_CHUNK], preferred_element_type=F32)
        b = jnp.dot(h, w13_ref[:, D_FF + c * FF_CHUNK:D_FF + (c + 1) * FF_CHUNK],
                    preferred_element_type=F32)
        act = (a * jax.nn.sigmoid(a) * b).astype(BF16)
        acc = acc + jnp.dot(act, w2_ref[c * FF_CHUNK:(c + 1) * FF_CHUNK, :],
                            preferred_element_type=F32)
    o_ref[...] = acc


def post_block(x2d, mixed2d, seq, wout, gxa, wq, qg, k_mem, v_mem, wo, gffn, w13, w2, tm):
    m, d = x2d.shape
    tiles_per_batch = seq // tm
    row = lambda i: (i, 0)
    mem_row = lambda i: (i // tiles_per_batch, 0)
    return pl.pallas_call(
        _post_kernel,
        out_shape=jax.ShapeDtypeStruct((m, d), F32),
        grid=(m // tm,),
        in_specs=[
            pl.BlockSpec((tm, d), row),
            pl.BlockSpec((tm, mixed2d.shape[1]), row),
            _const_spec(wout.shape),
            _const_spec((1, d)),
            _const_spec(wq.shape),
            _const_spec((1, HEAD_DIM)),
            pl.BlockSpec((N_MEM, XA_WIDTH), mem_row),
            pl.BlockSpec((N_MEM, XA_WIDTH), mem_row),
            _const_spec(wo.shape),
            _const_spec((1, d)),
            _const_spec(w13.shape),
            _const_spec(w2.shape),
        ],
        out_specs=pl.BlockSpec((tm, d), row),
        compiler_params=pltpu.CompilerParams(
            dimension_semantics=("arbitrary",), vmem_limit_bytes=V7X_VMEM_LIMIT_BYTES),
        name="post_block",
    )(x2d, mixed2d, wout, gxa.reshape(1, d), wq, qg.reshape(1, HEAD_DIM), k_mem, v_mem, wo,
      gffn.reshape(1, d), w13, w2)


C_HG = 0
C_DSA_Q = 1024
C_DSA_IQ = 1280
C_NSA_Q = 1536
C_ML_V = 1792
C_ML_QK = 2048
C_ML_OG = 2560
C_DSA_CKV = 2816
C_DSA_IK = 2944
C_DSA_IW = 3072
C_NSA_GATE = 3200
C_NSA_CMP = 3328
C_NSA_SLC = 3456
C_NSA_WIN = 3584
C_ML_GATE = 3712
IN_COLS_PACKED = 3840


def pack_w_in(w):
    dsa0 = GROUP_COLS[0]
    nsa0 = dsa0 + GROUP_COLS[1]
    ml0 = nsa0 + GROUP_COLS[2]

    def seg(a, n, width=None):
        s = w[:, a:a + n]
        if width is not None and width > n:
            s = jnp.pad(s, ((0, 0), (0, width - n)))
        return s

    parts = [
        seg(0, 1024),
        seg(dsa0, 256), seg(dsa0 + 384, 256),
        seg(nsa0, 256),
        seg(ml0 + 512, 256), seg(ml0, 512), seg(ml0 + 768, 256),
        seg(dsa0 + 256, 128),
        seg(dsa0 + 640, 32, LANES), seg(dsa0 + 672, 8, LANES),
        seg(nsa0 + 640, 12, LANES),
        seg(nsa0 + 256, 128), seg(nsa0 + 384, 128), seg(nsa0 + 512, 128),
        seg(ml0 + 1024, 8, LANES),
    ]
    packed = jnp.concatenate(parts, axis=1)
    assert packed.shape[1] == IN_COLS_PACKED
    return packed


def rope_tables(l_, d=HEAD_DIM):
    rd = d // 4
    half = rd // 2
    inv = ROPE_THETA ** (-jnp.arange(half, dtype=F32) * 2.0 / rd)
    ang = jnp.arange(l_).astype(F32)[:, None] * inv[None, :]
    cos, sin = jnp.cos(ang), jnp.sin(ang)
    zh = jnp.zeros((l_, half), F32)
    rest0 = jnp.zeros((l_, d - rd), F32)
    c = jnp.concatenate([cos, cos, rest0 + 1.0], axis=1)
    s1 = jnp.concatenate([-sin, zh, rest0], axis=1)
    s2 = jnp.concatenate([zh, sin, rest0], axis=1)
    return jnp.stack([c, s1, s2])


def rope_tables_pad(rt, width):
    n = width - rt.shape[-1]
    ident = jnp.stack([jnp.ones(rt.shape[1:2] + (n,), F32), jnp.zeros(rt.shape[1:2] + (n,), F32),
                       jnp.zeros(rt.shape[1:2] + (n,), F32)])
    return jnp.concatenate([rt, ident], axis=-1)


def rope_tables_kv_pair(rt):
    return rope_tables_pad(rt, 2 * rt.shape[-1])


def _apply_rope(t, rope_ref, half=HEAD_DIM // 8):
    w = t.shape[-1]
    return (t * rope_ref[0] + pltpu.roll(t, w - half, 1) * rope_ref[1]
            + pltpu.roll(t, half, 1) * rope_ref[2])


_NT = (((1,), (1,)), ((), ()))


def _dot_nt(a, b):
    return lax.dot_general(a, b, _NT, preferred_element_type=F32)


def _dot(a, b):
    return jnp.dot(a, b, preferred_element_type=F32)


NEG_BIG = -(2.0 ** 30)
NSA_KEY_TILE = 512
NSA_TQ = 128
NSA_SEL_TQ = 256
NSA_PREP_TK = 512


def _nsa_kv_prep_kernel(ps_ref, pw_ref, rope_ref, gs_ref, gw_ref, kaug_ref, vs_ref, kw_ref, vw_ref):
    tk = ps_ref.shape[0]
    lane = lax.broadcasted_iota(jnp.int32, (tk, LANES), 1)
    is_k = lane < HEAD_DIM

    def norm_rope(p, g):
        ms = jnp.sum(jnp.where(is_k, p * p, 0.0), axis=-1, keepdims=True) * (1.0 / HEAD_DIM)
        y = jnp.where(is_k, p * lax.rsqrt(ms + EPS) * g, p)
        return _apply_rope(y, rope_ref)

    ys = norm_rope(ps_ref[...], gs_ref[...])
    yw = norm_rope(pw_ref[...], gw_ref[...])
    row = pl.program_id(1) * tk + lax.broadcasted_iota(jnp.int32, (tk, LANES), 0)
    ind = jnp.where(jnp.right_shift(row, SLC_SHIFT) == (lane - HEAD_DIM), 1.0, 0.0)
    kaug_ref[0] = jnp.where(is_k, ys, ind).astype(BF16)
    vs_ref[0] = ys.T[HEAD_DIM:, :].astype(BF16)
    kw_ref[0] = yw[:, :HEAD_DIM].astype(BF16)
    vw_ref[0] = yw.T[HEAD_DIM:, :].astype(BF16)


def nsa_kv_prep(cols, b_, l_, rope_pair, g_slc, g_win):
    tk = NSA_PREP_TK
    nt = l_ // tk
    ones = jnp.ones((HEAD_DIM,), F32)
    gs = jnp.concatenate([g_slc, ones]).reshape(1, LANES)
    gw = jnp.concatenate([g_win, ones]).reshape(1, LANES)
    kv = lambda w: jax.ShapeDtypeStruct((b_, l_, w), BF16)
    kv_t = jax.ShapeDtypeStruct((b_, HEAD_DIM, l_), BF16)
    out_blk = lambda w: pl.BlockSpec((1, tk, w), lambda b, i: (b, i, 0))
    out_t = pl.BlockSpec((1, HEAD_DIM, tk), lambda b, i: (b, 0, i))
    return pl.pallas_call(
        _nsa_kv_prep_kernel,
        out_shape=(kv(LANES), kv_t, kv(HEAD_DIM), kv_t),
        grid=(b_, nt),
        in_specs=[
            pl.BlockSpec((tk, LANES), lambda b, i: (b * nt + i, C_NSA_SLC // LANES)),
            pl.BlockSpec((tk, LANES), lambda b, i: (b * nt + i, C_NSA_WIN // LANES)),
            pl.BlockSpec((3, tk, LANES), lambda b, i: (0, i, 0)),
            pl.BlockSpec((1, LANES), lambda b, i: (0, 0)),
            pl.BlockSpec((1, LANES), lambda b, i: (0, 0)),
        ],
        out_specs=(out_blk(LANES), out_t, out_blk(HEAD_DIM), out_t),
        compiler_params=pltpu.CompilerParams(
            dimension_semantics=("arbitrary", "arbitrary"),
            vmem_limit_bytes=V7X_VMEM_LIMIT_BYTES),
        name="nsa_kv_prep",
    )(cols, cols, rope_pair, gs, gw)


def _nsa_compress_kernel(r_ref, pea_ref, peb_ref, w1a_ref, w1b_ref, w2k_ref, w2v_ref, kg_ref,
                         kc_ref, vc_ref):
    r = r_ref[0]
    n = r.shape[0]
    a = _dot((r + pea_ref[...]).astype(BF16), w1a_ref[...])
    bm = _dot((r + peb_ref[...]).astype(BF16), w1b_ref[...])
    row = lax.broadcasted_iota(jnp.int32, bm.shape, 0)
    bm_up = jnp.where(row < n - 1, pltpu.roll(bm, n - 1, 0), 0.0)
    h = jnp.maximum(a + bm_up, 0.0).astype(BF16)
    hid = w2k_ref.shape[0]
    ck = _dot(h[:, :hid], w2k_ref[...])
    cv = _dot(h[:, hid:], w2v_ref[...])
    kc_ref[0] = _rms_rows(ck, kg_ref[...]).astype(BF16)
    vc_ref[0] = cv.astype(BF16)


def nsa_compress(cols, b_, l_, pos_k, pos_v, k_w1, k_w2, v_w1, v_w2, k_gain):
    rows = l_ // CMP_STRIDE
    hid = k_w1.shape[1]
    pair = cols[:, C_NSA_CMP:C_NSA_CMP + LANES].reshape(b_, rows, CMP_STRIDE * LANES)

    def interleave_pe(lo):
        pe = jnp.concatenate([pos_k[lo:lo + CMP_STRIDE], pos_v[lo:lo + CMP_STRIDE]], axis=1)
        return pe.reshape(1, CMP_STRIDE * LANES)

    def interleave_w(lo):
        wk = k_w1[lo * HEAD_DIM:(lo + CMP_STRIDE) * HEAD_DIM].reshape(CMP_STRIDE, HEAD_DIM, hid)
        wv = v_w1[lo * HEAD_DIM:(lo + CMP_STRIDE) * HEAD_DIM].reshape(CMP_STRIDE, HEAD_DIM, hid)
        z = jnp.zeros_like(wk)
        top = jnp.concatenate([wk, z], axis=2)
        bot = jnp.concatenate([z, wv], axis=2)
        return jnp.concatenate([top, bot], axis=1).reshape(CMP_STRIDE * LANES, 2 * hid).astype(BF16)

    out = jax.ShapeDtypeStruct((b_, rows, HEAD_DIM), BF16)
    return pl.pallas_call(
        _nsa_compress_kernel,
        out_shape=(out, out),
        grid=(b_,),
        in_specs=[
            pl.BlockSpec((1, rows, CMP_STRIDE * LANES), lambda b: (b, 0, 0)),
            _const_spec((1, CMP_STRIDE * LANES)),
            _const_spec((1, CMP_STRIDE * LANES)),
            _const_spec((CMP_STRIDE * LANES, 2 * hid)),
            _const_spec((CMP_STRIDE * LANES, 2 * hid)),
            _const_spec((hid, HEAD_DIM)),
            _const_spec((hid, HEAD_DIM)),
            _const_spec((1, HEAD_DIM)),
        ],
        out_specs=(pl.BlockSpec((1, rows, HEAD_DIM), lambda b: (b, 0, 0)),
                   pl.BlockSpec((1, rows, HEAD_DIM), lambda b: (b, 0, 0))),
        compiler_params=pltpu.CompilerParams(
            dimension_semantics=("arbitrary",), vmem_limit_bytes=V7X_VMEM_LIMIT_BYTES),
        name="nsa_compress",
    )(pair, interleave_pe(0), interleave_pe(CMP_STRIDE), interleave_w(0),
      interleave_w(CMP_STRIDE), k_w2.astype(BF16), v_w2.astype(BF16),
      k_gain.reshape(1, HEAD_DIM))


def _masked_softmax(s, valid, axis):
    m = jnp.max(jnp.where(valid, s, -jnp.inf), axis=axis, keepdims=True)
    m = jnp.where(m == -jnp.inf, 0.0, m)
    e = jnp.where(valid, jnp.exp(s - m), 0.0)
    den = jnp.sum(e, axis=axis, keepdims=True)
    return e / jnp.where(den > 0, den, 1.0)


def _nsa_select_kernel(q_ref, kc_ref, vc_ref, ovt_ref, rope_ref, qg_ref, qaug_ref, ocmp_ref,
                       imp_ref, *, n_sel):
    tq = q_ref.shape[0]
    ncr = kc_ref.shape[1]
    n_slc = ovt_ref.shape[0]
    t0 = pl.program_id(1) * tq
    scale = HEAD_DIM ** -0.5
    q = q_ref[...]
    g = qg_ref[...]
    kc = kc_ref[0]
    vc = vc_ref[0]
    last = CMP_BLOCK - 1
    vis = (lax.broadcasted_iota(jnp.int32, (tq, ncr), 1) * CMP_STRIDE + last
           <= t0 + lax.broadcasted_iota(jnp.int32, (tq, ncr), 0))
    vis_t = (lax.broadcasted_iota(jnp.int32, (ncr, tq), 0) * CMP_STRIDE + last
             <= t0 + lax.broadcasted_iota(jnp.int32, (ncr, tq), 1))
    qn_heads, o_heads = [], []
    psum_t = jnp.zeros((ncr, tq), F32)
    for h in range(GROUP_HEADS):
        qn = _rms_rows(q[:, h * HEAD_DIM:(h + 1) * HEAD_DIM], g)
        qn_heads.append(qn)
        qb = (qn * scale).astype(BF16)
        p = _masked_softmax(_dot_nt(qb, kc), vis, 1)
        o_heads.append(_dot(p.astype(BF16), vc))
        psum_t = psum_t + _masked_softmax(_dot_nt(kc, qb), vis_t, 0)
    ocmp_ref[0] = jnp.concatenate(o_heads, axis=-1)

    hi = psum_t.astype(BF16)
    lo = (psum_t - hi.astype(F32)).astype(BF16)
    imp = _dot(ovt_ref[...], hi) + _dot(ovt_ref[...], lo)
    blk = lax.broadcasted_iota(jnp.int32, (n_slc, tq), 0)
    cur = jnp.right_shift(t0 + lax.broadcasted_iota(jnp.int32, (n_slc, tq), 1), SLC_SHIFT)
    forced = (blk == 0) | (blk == cur) | (blk == cur - 1)
    imp = jnp.where(forced, jnp.inf, jnp.where(blk > cur, -jnp.inf, imp))
    imp_ref[...] = imp

    def count_beats(m, cnt):
        row = imp_ref[pl.ds(m, 1), :]
        beats = (row > imp) | ((row == imp) & (blk > m))
        return cnt + jnp.where(beats, 1.0, 0.0)

    cnt = lax.fori_loop(0, n_slc, count_beats, jnp.zeros((n_slc, tq), F32))
    mt = jnp.where(cnt < n_sel, 0.0, NEG_BIG)
    pad = jnp.zeros((HEAD_DIM - n_slc, tq), F32)
    mt = jnp.concatenate([mt, pad, mt, pad], axis=0) if n_slc < HEAD_DIM else jnp.concatenate(
        [mt, mt], axis=0)
    mt = mt.T

    qr = _apply_rope(jnp.concatenate(qn_heads, axis=-1), rope_ref) * scale
    lane = lax.broadcasted_iota(jnp.int32, (tq, LANES), 1)
    for j in range(GROUP_HEADS // 2):
        pair = qr[:, j * LANES:(j + 1) * LANES]
        swapped = pltpu.roll(pair, HEAD_DIM, 1)
        qaug_ref[0, :, (2 * j) * LANES:(2 * j + 1) * LANES] = jnp.where(
            lane < HEAD_DIM, pair, mt).astype(BF16)
        qaug_ref[0, :, (2 * j + 1) * LANES:(2 * j + 2) * LANES] = jnp.where(
            lane < HEAD_DIM, swapped, mt).astype(BF16)


def nsa_select(cols, b_, l_, k_cmp, v_cmp, rope_q, q_gain):
    tq = min(NSA_SEL_TQ, l_)
    nt = l_ // tq
    ncr = l_ // CMP_STRIDE
    n_slc = l_ // SLC_BLOCK
    n_sel = min(SLC_TOPN, n_slc)
    st_c = np.arange(ncr) * CMP_STRIDE
    st_s = np.arange(n_slc) * SLC_BLOCK
    ovt = ((st_c[None, :] < st_s[:, None] + SLC_BLOCK)
           & (st_c[None, :] + CMP_BLOCK > st_s[:, None])).astype(np.float32)
    return pl.pallas_call(
        functools.partial(_nsa_select_kernel, n_sel=n_sel),
        out_shape=(jax.ShapeDtypeStruct((b_, l_, GROUP_HEADS * LANES), BF16),
                   jax.ShapeDtypeStruct((b_, l_, GROUP_WIDTH), F32)),
        grid=(b_, nt),
        in_specs=[
            pl.BlockSpec((tq, GROUP_WIDTH), lambda b, i: (b * nt + i, C_NSA_Q // GROUP_WIDTH)),
            pl.BlockSpec((1, ncr, HEAD_DIM), lambda b, i: (b, 0, 0)),
            pl.BlockSpec((1, ncr, HEAD_DIM), lambda b, i: (b, 0, 0)),
            pl.BlockSpec((n_slc, ncr), lambda b, i: (0, 0)),
            pl.BlockSpec((3, tq, GROUP_WIDTH), lambda b, i: (0, i, 0)),
            pl.BlockSpec((1, HEAD_DIM), lambda b, i: (0, 0)),
        ],
        out_specs=(pl.BlockSpec((1, tq, GROUP_HEADS * LANES), lambda b, i: (b, i, 0)),
                   pl.BlockSpec((1, tq, GROUP_WIDTH), lambda b, i: (b, i, 0))),
        scratch_shapes=[pltpu.VMEM((n_slc, tq), F32)],
        compiler_params=pltpu.CompilerParams(
            dimension_semantics=("arbitrary", "arbitrary"),
            vmem_limit_bytes=V7X_VMEM_LIMIT_BYTES),
        name="nsa_select",
    )(cols, k_cmp, v_cmp, jnp.asarray(ovt, BF16), rope_q, q_gain.reshape(1, HEAD_DIM))


def _nsa_attend_kernel(qaug_ref, kaug_ref, vs_ref, kw_ref, vw_ref, ocmp_ref, gate_ref, o_ref):
    tq = qaug_ref.shape[1]
    nh = GROUP_HEADS
    ts = NSA_KEY_TILE
    t0 = pl.program_id(1) * tq
    qa = qaug_ref[0]
    qs = jnp.concatenate([qa[:, h * LANES:(h + 1) * LANES] for h in range(nh)], axis=0)
    qpos = t0 + lax.rem(lax.broadcasted_iota(jnp.int32, (nh * tq, 1), 0), tq)

    def key_tile(kt, carry):
        m, l, acc = carry
        k0 = pl.multiple_of(kt * ts, ts)
        s = _dot_nt(qs, kaug_ref[0, pl.ds(k0, ts), :])
        kpos = k0 + lax.broadcasted_iota(jnp.int32, (nh * tq, ts), 1)
        s = jnp.where(kpos <= qpos, s, NEG_BIG)
        m_new = jnp.maximum(m, jnp.max(s, axis=-1, keepdims=True))
        alpha = jnp.exp(m - m_new)
        p = jnp.exp(s - m_new)
        l = alpha * l + jnp.sum(p, axis=-1, keepdims=True)
        acc = alpha * acc + _dot(p.astype(BF16), vs_ref[0, pl.ds(k0, ts), :])
        return m_new, l, acc

    n_kt = (t0 + tq + ts - 1) // ts
    init = (jnp.full((nh * tq, 1), -jnp.inf, F32), jnp.zeros((nh * tq, 1), F32),
            jnp.zeros((nh * tq, HEAD_DIM), F32))
    _, l, acc = lax.fori_loop(0, n_kt, key_tile, init)
    o_slc = acc / l

    wlen = WINDOW + tq
    start = pl.multiple_of(jnp.maximum(t0 - WINDOW, 0), tq)
    sw = _dot_nt(qs[:, :HEAD_DIM], kw_ref[0, pl.ds(start, wlen), :])
    dist = qpos - (start + lax.broadcasted_iota(jnp.int32, (nh * tq, wlen), 1))
    valid = (dist >= 0) & (dist < WINDOW)
    sw = jnp.where(valid, sw, -jnp.inf)
    e = jnp.exp(sw - jnp.max(sw, axis=-1, keepdims=True))
    o_swa = _dot(e.astype(BF16), vw_ref[0, pl.ds(start, wlen), :]) / jnp.sum(
        e, axis=-1, keepdims=True)

    g = jax.nn.sigmoid(gate_ref[...])
    oc = ocmp_ref[0]
    outs = []
    for h in range(nh):
        rows = slice(h * tq, (h + 1) * tq)
        outs.append(g[:, h:h + 1] * oc[:, h * HEAD_DIM:(h + 1) * HEAD_DIM]
                    + g[:, nh + h:nh + h + 1] * o_slc[rows]
                    + g[:, 2 * nh + h:2 * nh + h + 1] * o_swa[rows])
    o_ref[...] = jnp.concatenate(outs, axis=-1)


def _softmax_stats_update(s, m, l):
    m_new = jnp.maximum(m, jnp.max(s, axis=0, keepdims=True))
    alpha = jnp.exp(m - m_new)
    p = jnp.exp(s - m_new)
    return m_new, alpha, p, alpha * l + jnp.sum(p, axis=0, keepdims=True)


def _nsa_attend_t_kernel(qaug_ref, kaug_ref, vst_ref, kw_ref, vwt_ref, ocmp_ref, gate_ref, o_ref):
    tq = qaug_ref.shape[1]
    nh = GROUP_HEADS
    ts = NSA_KEY_TILE
    t0 = pl.program_id(1) * tq
    qa = qaug_ref[0]
    qs = jnp.concatenate([qa[:, h * LANES:(h + 1) * LANES] for h in range(nh)], axis=0)
    n = nh * tq
    qpos_tile = t0 + lax.broadcasted_iota(jnp.int32, (1, tq), 1)
    qpos = jnp.concatenate([qpos_tile] * nh, axis=1)

    def key_tile(kt, carry):
        m, l, acc = carry
        k0 = pl.multiple_of(kt * ts, ts)
        s = _dot_nt(kaug_ref[0, pl.ds(k0, ts), :], qs)
        kpos = k0 + lax.broadcasted_iota(jnp.int32, (ts, n), 0)
        s = jnp.where(kpos <= qpos, s, NEG_BIG)
        m_new, alpha, p, l = _softmax_stats_update(s, m, l)
        acc = alpha * acc + _dot(vst_ref[0, :, pl.ds(k0, ts)], p.astype(BF16))
        return m_new, l, acc

    n_kt = (t0 + tq + ts - 1) // ts
    init = (jnp.full((1, n), -jnp.inf, F32), jnp.zeros((1, n), F32),
            jnp.zeros((HEAD_DIM, n), F32))
    _, l, acc = lax.fori_loop(0, n_kt, key_tile, init)
    o_slc = acc / l

    wlen = WINDOW + tq
    start = pl.multiple_of(jnp.maximum(t0 - WINDOW, 0), tq)
    sw = _dot_nt(kw_ref[0, pl.ds(start, wlen), :], qs[:, :HEAD_DIM])
    dist = qpos - (start + lax.broadcasted_iota(jnp.int32, (wlen, n), 0))
    sw = jnp.where((dist >= 0) & (dist < WINDOW), sw, -jnp.inf)
    e = jnp.exp(sw - jnp.max(sw, axis=0, keepdims=True))
    o_swa = _dot(vwt_ref[0, :, pl.ds(start, wlen)], e.astype(BF16)) / jnp.sum(
        e, axis=0, keepdims=True)

    g = jax.nn.sigmoid(gate_ref[...]).T
    oc = ocmp_ref[0].T
    outs = []
    for h in range(nh):
        cols_h = slice(h * tq, (h + 1) * tq)
        outs.append(g[h:h + 1, :] * oc[h * HEAD_DIM:(h + 1) * HEAD_DIM, :]
                    + g[nh + h:nh + h + 1, :] * o_slc[:, cols_h]
                    + g[2 * nh + h:2 * nh + h + 1, :] * o_swa[:, cols_h])
    o_ref[...] = jnp.concatenate(outs, axis=0).T


def nsa_attend(cols, b_, l_, q_aug, k_aug, v_slc, k_win, v_win, o_cmp):
    tq = NSA_TQ
    nt = l_ // tq
    seq = lambda w: pl.BlockSpec((1, l_, w), lambda b, i: (b, 0, 0))
    seq_t = pl.BlockSpec((1, HEAD_DIM, l_), lambda b, i: (b, 0, 0))
    return pl.pallas_call(
        _nsa_attend_t_kernel,
        out_shape=jax.ShapeDtypeStruct((b_ * l_, GROUP_WIDTH), F32),
        grid=(b_, nt),
        in_specs=[
            pl.BlockSpec((1, tq, GROUP_HEADS * LANES), lambda b, i: (b, i, 0)),
            seq(LANES), seq_t, seq(HEAD_DIM), seq_t,
            pl.BlockSpec((1, tq, GROUP_WIDTH), lambda b, i: (b, i, 0)),
            pl.BlockSpec((tq, LANES), lambda b, i: (b * nt + i, C_NSA_GATE // LANES)),
        ],
        out_specs=pl.BlockSpec((tq, GROUP_WIDTH), lambda b, i: (b * nt + i, 0)),
        compiler_params=pltpu.CompilerParams(
            dimension_semantics=("arbitrary", "arbitrary"),
            vmem_limit_bytes=V7X_VMEM_LIMIT_BYTES),
        name="nsa_attend",
    )(q_aug, k_aug, v_slc, k_win, v_win, o_cmp, cols)


def nsa_mixer_pallas(cols, b_, l_, rope, pos_k, pos_v, k_w1, k_w2, v_w1, v_w2, q_gain, k_gains):
    k_aug, v_slc, k_win, v_win = nsa_kv_prep(cols, b_, l_, rope_tables_kv_pair(rope),
                                             k_gains[1], k_gains[2])
    k_cmp, v_cmp = nsa_compress(cols, b_, l_, pos_k, pos_v, k_w1, k_w2, v_w1, v_w2, k_gains[0])
    q_aug, o_cmp = nsa_select(cols, b_, l_, k_cmp, v_cmp, jnp.tile(rope, (1, 1, GROUP_HEADS)),
                              q_gain)
    return nsa_attend(cols, b_, l_, q_aug, k_aug, v_slc, k_win, v_win, o_cmp)


DSA_TQ = 128
DSA_KEY_TILE = 512
DSA_PREP_TK = 512
IDX_PACK = LANES
INT_MIN = -2 ** 31
MASKED_SCORE = -1e30


def _split_hi_lo(t):
    hi = t.astype(BF16)
    lo = (t - hi.astype(F32)).astype(BF16)
    return hi, lo


def _placement(rows, cols, pairs):
    p = np.zeros((rows, cols), np.float32)
    for r, c in pairs:
        p[r, c] = 1.0
    return jnp.asarray(p, BF16)


def _dsa_kv_prep_kernel(ckv_ref, ik_ref, rope_ref, ropei_ref, kvg_ref, wkv_ref, kg_ref, ikg_ref,
                        pkh_ref, pkl_ref, k_ref, vt_ref, ik3_ref):
    tk = ckv_ref.shape[0]
    lane = lax.broadcasted_iota(jnp.int32, (tk, LANES), 1)
    ckv = _rms_rows(ckv_ref[...], kvg_ref[...]).astype(BF16)
    kv = _dot(ckv, wkv_ref[...])
    is_k = lane < HEAD_DIM
    ms = jnp.sum(jnp.where(is_k, kv * kv, 0.0), axis=-1, keepdims=True) * (1.0 / HEAD_DIM)
    y = _apply_rope(jnp.where(is_k, kv * lax.rsqrt(ms + EPS) * kg_ref[...], kv), rope_ref)
    k_ref[0] = y[:, :HEAD_DIM].astype(BF16)
    vt_ref[0] = y.T[HEAD_DIM:, :].astype(BF16)
    ik = ik_ref[...]
    ms = jnp.sum(ik * ik, axis=-1, keepdims=True) * (1.0 / IDX_DIM)
    ikn = _apply_rope(ik * lax.rsqrt(ms + EPS) * ikg_ref[...], ropei_ref, IDX_DIM // 8)
    hi, lo = _split_hi_lo(ikn)
    ik3_ref[0] = (_dot(hi, pkh_ref[...]) + _dot(lo, pkl_ref[...])).astype(BF16)


def dsa_kv_prep(cols, b_, l_, rope_pair, rope_idx, kv_gain, w_uk, w_uv, k_gain, idxk_gain):
    tk = DSA_PREP_TK
    nt = l_ // tk
    ones = jnp.ones((HEAD_DIM,), F32)
    kg = jnp.concatenate([k_gain, ones]).reshape(1, LANES)
    ikg = jnp.pad(idxk_gain, (0, LANES - IDX_DIM)).reshape(1, LANES)
    wkv = jnp.concatenate([w_uk, w_uv], axis=1).astype(BF16)
    d = range(IDX_DIM)
    pkh = _placement(LANES, IDX_PACK, [(i, i) for i in d] + [(i, 2 * IDX_DIM + i) for i in d])
    pkl = _placement(LANES, IDX_PACK, [(i, IDX_DIM + i) for i in d])
    const = lambda shape: pl.BlockSpec(shape, lambda b, i: (0,) * len(shape))
    out = lambda w: pl.BlockSpec((1, tk, w), lambda b, i: (b, i, 0))
    return pl.pallas_call(
        _dsa_kv_prep_kernel,
        out_shape=(jax.ShapeDtypeStruct((b_, l_, HEAD_DIM), BF16),
                   jax.ShapeDtypeStruct((b_, HEAD_DIM, l_), BF16),
                   jax.ShapeDtypeStruct((b_, l_, IDX_PACK), BF16)),
        grid=(b_, nt),
        in_specs=[
            pl.BlockSpec((tk, LANES), lambda b, i: (b * nt + i, C_DSA_CKV // LANES)),
            pl.BlockSpec((tk, LANES), lambda b, i: (b * nt + i, C_DSA_IK // LANES)),
            pl.BlockSpec((3, tk, LANES), lambda b, i: (0, i, 0)),
            pl.BlockSpec((3, tk, LANES), lambda b, i: (0, i, 0)),
            const((1, LANES)), const((DSA_LATENT, LANES)), const((1, LANES)), const((1, LANES)),
            const((LANES, IDX_PACK)), const((LANES, IDX_PACK)),
        ],
        out_specs=(out(HEAD_DIM), pl.BlockSpec((1, HEAD_DIM, tk), lambda b, i: (b, 0, i)),
                   out(IDX_PACK)),
        compiler_params=pltpu.CompilerParams(
            dimension_semantics=("arbitrary", "arbitrary"),
            vmem_limit_bytes=V7X_VMEM_LIMIT_BYTES),
        name="dsa_kv_prep",
    )(cols, cols, rope_pair, rope_idx, kv_gain.reshape(1, DSA_LATENT), wkv, kg, ikg, pkh, pkl)


def _dsa_q_prep_kernel(q_ref, iq_ref, rope_ref, ropei_ref, qg_ref, pqh_ref, pql_ref,
                       qh_ref, iq3_ref):
    g = qg_ref[...]
    q = q_ref[...]
    qn = jnp.concatenate([_rms_rows(q[:, h * HEAD_DIM:(h + 1) * HEAD_DIM], g)
                          for h in range(GROUP_HEADS)], axis=-1)
    qr = _apply_rope(qn, rope_ref) * (HEAD_DIM ** -0.5)
    for h in range(GROUP_HEADS):
        qh_ref[0, h] = qr[:, h * HEAD_DIM:(h + 1) * HEAD_DIM].astype(BF16)
    hi, lo = _split_hi_lo(_apply_rope(iq_ref[...], ropei_ref, IDX_DIM // 8))
    iq3_ref[0] = (_dot(hi, pqh_ref[...]) + _dot(lo, pql_ref[...])).astype(BF16)


def dsa_q_prep(cols, b_, l_, rope_q, rope_iq, q_gain):
    tq = 256
    nt = l_ // tq
    w = IDX_HEADS * IDX_DIM
    hd = [(h, i) for h in range(IDX_HEADS) for i in range(IDX_DIM)]
    pqh = _placement(w, IDX_HEADS * IDX_PACK,
                     [(IDX_DIM * h + i, IDX_PACK * h + i) for h, i in hd]
                     + [(IDX_DIM * h + i, IDX_PACK * h + IDX_DIM + i) for h, i in hd])
    pql = _placement(w, IDX_HEADS * IDX_PACK,
                     [(IDX_DIM * h + i, IDX_PACK * h + 2 * IDX_DIM + i) for h, i in hd])
    const = lambda shape: pl.BlockSpec(shape, lambda b, i: (0,) * len(shape))
    return pl.pallas_call(
        _dsa_q_prep_kernel,
        out_shape=(jax.ShapeDtypeStruct((b_, GROUP_HEADS, l_, HEAD_DIM), BF16),
                   jax.ShapeDtypeStruct((b_, l_, IDX_HEADS * IDX_PACK), BF16)),
        grid=(b_, nt),
        in_specs=[
            pl.BlockSpec((tq, GROUP_WIDTH), lambda b, i: (b * nt + i, C_DSA_Q // GROUP_WIDTH)),
            pl.BlockSpec((tq, w), lambda b, i: (b * nt + i, C_DSA_IQ // w)),
            pl.BlockSpec((3, tq, GROUP_WIDTH), lambda b, i: (0, i, 0)),
            pl.BlockSpec((3, tq, w), lambda b, i: (0, i, 0)),
            const((1, HEAD_DIM)), const(pqh.shape), const(pql.shape),
        ],
        out_specs=(pl.BlockSpec((1, GROUP_HEADS, tq, HEAD_DIM), lambda b, i: (b, 0, i, 0)),
                   pl.BlockSpec((1, tq, IDX_HEADS * IDX_PACK), lambda b, i: (b, i, 0))),
        compiler_params=pltpu.CompilerParams(
            dimension_semantics=("arbitrary", "arbitrary"),
            vmem_limit_bytes=V7X_VMEM_LIMIT_BYTES),
        name="dsa_q_prep",
    )(cols, cols, rope_q, rope_iq, q_gain.reshape(1, HEAD_DIM), pqh, pql)


def _dsa_attend_kernel(qh_ref, iq3_ref, iw_ref, k_ref, v_ref, ik3_ref, o_ref, sc_ref, *,
                       topk, idx_bits):
    tq = iq3_ref.shape[1]
    ts = DSA_KEY_TILE
    nh = GROUP_HEADS
    t0 = pl.program_id(1) * tq
    n_kt = (t0 + tq + ts - 1) // ts
    qpos = t0 + lax.broadcasted_iota(jnp.int32, (tq, 1), 0)

    iq3 = iq3_ref[0]
    lhs = jnp.concatenate([iq3[:, h * IDX_PACK:(h + 1) * IDX_PACK] for h in range(IDX_HEADS)],
                          axis=0)
    iw = iw_ref[...] * (IDX_HEADS ** -0.5 * IDX_DIM ** -0.5)
    w_col = jnp.concatenate([iw[:, h:h + 1] for h in range(IDX_HEADS)], axis=0)

    def score_tile(kt, _):
        k0 = pl.multiple_of(kt * ts, ts)
        rel = jnp.maximum(_dot_nt(lhs, ik3_ref[0, pl.ds(k0, ts), :]), 0.0) * w_col
        sc = rel[0:tq]
        for h in range(1, IDX_HEADS):
            sc = sc + rel[h * tq:(h + 1) * tq]
        kpos = k0 + lax.broadcasted_iota(jnp.int32, (tq, ts), 1)
        sc_ref[:, pl.ds(k0, ts)] = jnp.where(kpos <= qpos, sc, -jnp.inf)
        return 0

    lax.fori_loop(0, n_kt, score_tile, 0)

    def count(pred):
        def tile(kt, c):
            for j in range(ts // LANES):
                k0 = pl.multiple_of(kt * ts + j * LANES, LANES)
                c = c + jnp.where(pred(sc_ref[:, pl.ds(k0, LANES)], k0), 1.0, 0.0)
            return c
        c = lax.fori_loop(0, n_kt, tile, jnp.zeros((tq, LANES), F32))
        return jnp.broadcast_to(jnp.sum(c, axis=-1, keepdims=True), (tq, LANES))

    def key_to_float(key):
        return pltpu.bitcast(jnp.where(key >= 0, key, key ^ jnp.int32(0x7FFFFFFF)), F32)

    def value_bit(i, thr_key):
        cand = thr_key | jnp.left_shift(jnp.int32(1), 31 - i)
        cand_f = key_to_float(cand ^ jnp.int32(INT_MIN))
        return jnp.where(count(lambda sc, k0: sc >= cand_f) >= topk, cand, thr_key)

    thr_key = lax.fori_loop(0, 32, value_bit, jnp.zeros((tq, LANES), jnp.int32))
    thr = key_to_float(thr_key ^ jnp.int32(INT_MIN))
    n_ge = count(lambda sc, k0: sc >= thr)
    thr = jnp.where(n_ge >= topk, thr, -jnp.inf)
    need = topk - count(lambda sc, k0: sc > thr)
    tie_break = jnp.max(jnp.where((n_ge > topk) & (thr > -jnp.inf), 1.0, 0.0)) > 0.0

    def last_tie_position():
        def index_bit(i, last):
            cand = last | jnp.left_shift(jnp.int32(1), idx_bits - 1 - i)

            def tied_below(sc, k0):
                kpos = k0 + lax.broadcasted_iota(jnp.int32, (tq, LANES), 1)
                return (sc == thr) & (kpos < cand)

            return jnp.where(count(tied_below) < need, cand, last)

        return lax.fori_loop(0, idx_bits, index_bit, jnp.zeros((tq, LANES), jnp.int32))

    last = lax.cond(tie_break, last_tie_position,
                    lambda: jnp.full((tq, LANES), 2 ** idx_bits, jnp.int32))
    widen = lambda t: jnp.concatenate([t] * (ts // LANES), axis=1)
    thr2 = widen(thr)
    last2 = widen(last)

    qrow = t0 + lax.broadcasted_iota(jnp.int32, (tq, ts), 0)

    def lane_fold(t, op):
        out = t[:, :LANES]
        for j in range(1, ts // LANES):
            out = op(out, t[:, j * LANES:(j + 1) * LANES])
        return out

    def max_tile(kt, ms):
        k0 = pl.multiple_of(kt * ts, ts)
        sc = sc_ref[:, pl.ds(k0, ts)]
        kpos = k0 + lax.broadcasted_iota(jnp.int32, (tq, ts), 1)
        sel = ((sc > thr2) | ((sc == thr2) & (kpos <= last2))) & (kpos <= qrow)
        bias = jnp.where(sel, 0.0, MASKED_SCORE)
        sc_ref[:, pl.ds(k0, ts)] = bias
        k_t = k_ref[0, pl.ds(k0, ts), :]
        return tuple(jnp.maximum(ms[h], lane_fold(_dot_nt(qh_ref[0, h], k_t) + bias, jnp.maximum))
                     for h in range(nh))

    ms = lax.fori_loop(0, n_kt, max_tile,
                       tuple(jnp.full((tq, LANES), MASKED_SCORE, F32) for _ in range(nh)))
    ms = tuple(jnp.broadcast_to(jnp.max(m, axis=-1, keepdims=True), (tq, LANES)) for m in ms)

    def sum_tile(kt, carry):
        k0 = pl.multiple_of(kt * ts, ts)
        bias = sc_ref[:, pl.ds(k0, ts)]
        k_t = k_ref[0, pl.ds(k0, ts), :]
        v_t = v_ref[0, pl.ds(k0, ts), :]
        new = []
        for h in range(nh):
            l, acc = carry[h]
            p = jnp.exp(_dot_nt(qh_ref[0, h], k_t) + bias - widen(ms[h]))
            new.append((l + lane_fold(p, jnp.add), acc + _dot(p.astype(BF16), v_t)))
        return tuple(new)

    init = tuple((jnp.zeros((tq, LANES), F32), jnp.zeros((tq, HEAD_DIM), F32)) for _ in range(nh))
    heads = lax.fori_loop(0, n_kt, sum_tile, init)
    o_ref[...] = jnp.concatenate([acc / jnp.sum(l, axis=-1, keepdims=True) for l, acc in heads],
                                 axis=-1)


SUBLANES = 8


def _fold_rows(t, op):
    parts = [t[i * SUBLANES:(i + 1) * SUBLANES] for i in range(t.shape[0] // SUBLANES)]
    while len(parts) > 1:
        parts = [op(parts[i], parts[i + 1]) if i + 1 < len(parts) else parts[i]
                 for i in range(0, len(parts), 2)]
    return parts[0]


def _dsa_attend_t_kernel(qh_ref, iq3_ref, iw_ref, k_ref, vt_ref, ik3_ref, o_ref, sc_ref, *,
                         topk, idx_bits):
    tq = iq3_ref.shape[1]
    ts = DSA_KEY_TILE
    nh = GROUP_HEADS
    t0 = pl.program_id(1) * tq
    n_kt = (t0 + tq + ts - 1) // ts
    qpos = t0 + lax.broadcasted_iota(jnp.int32, (ts, tq), 1)
    krow = lax.broadcasted_iota(jnp.int32, (ts, tq), 0)

    iq3 = iq3_ref[0]
    iq_rows = jnp.concatenate([iq3[:, h * IDX_PACK:(h + 1) * IDX_PACK]
                               for h in range(IDX_HEADS)], axis=0)
    iw_t = (iw_ref[...] * (IDX_HEADS ** -0.5 * IDX_DIM ** -0.5)).T

    def score_tile(kt, _):
        k0 = pl.multiple_of(kt * ts, ts)
        rel = jnp.maximum(_dot_nt(ik3_ref[0, pl.ds(k0, ts), :], iq_rows), 0.0)
        sc = rel[:, 0:tq] * iw_t[0:1, :]
        for h in range(1, IDX_HEADS):
            sc = sc + rel[:, h * tq:(h + 1) * tq] * iw_t[h:h + 1, :]
        sc_ref[pl.ds(k0, ts), :] = jnp.where(k0 + krow <= qpos, sc, -jnp.inf)
        return 0

    lax.fori_loop(0, n_kt, score_tile, 0)

    def count(pred):
        def tile(kt, c):
            k0 = pl.multiple_of(kt * ts, ts)
            hit = jnp.where(pred(sc_ref[pl.ds(k0, ts), :], k0), 1.0, 0.0)
            return c + _fold_rows(hit, jnp.add)
        c = lax.fori_loop(0, n_kt, tile, jnp.zeros((SUBLANES, tq), F32))
        return jnp.sum(c, axis=0, keepdims=True)

    def key_to_float(key):
        return pltpu.bitcast(jnp.where(key >= 0, key, key ^ jnp.int32(0x7FFFFFFF)), F32)

    def value_bit(i, thr_key):
        cand = thr_key | jnp.left_shift(jnp.int32(1), 31 - i)
        cand_f = key_to_float(cand ^ jnp.int32(INT_MIN))
        return jnp.where(count(lambda sc, k0: sc >= cand_f) >= topk, cand, thr_key)

    thr_key = lax.fori_loop(0, 32, value_bit, jnp.zeros((1, tq), jnp.int32))
    thr = key_to_float(thr_key ^ jnp.int32(INT_MIN))
    n_ge = count(lambda sc, k0: sc >= thr)
    thr = jnp.where(n_ge >= topk, thr, -jnp.inf)
    need = topk - count(lambda sc, k0: sc > thr)
    tie_break = jnp.max(jnp.where((n_ge > topk) & (thr > -jnp.inf), 1.0, 0.0)) > 0.0

    def last_tie_position():
        def index_bit(i, last):
            cand = last | jnp.left_shift(jnp.int32(1), idx_bits - 1 - i)
            tied_below = lambda sc, k0: (sc == thr) & (k0 + krow < cand)
            return jnp.where(count(tied_below) < need, cand, last)

        return lax.fori_loop(0, idx_bits, index_bit, jnp.zeros((1, tq), jnp.int32))

    last = lax.cond(tie_break, last_tie_position,
                    lambda: jnp.full((1, tq), 2 ** idx_bits, jnp.int32))

    qs = qh_ref[0].reshape(nh * tq, HEAD_DIM)

    def key_tile(kt, carry):
        m, l, acc = carry
        k0 = pl.multiple_of(kt * ts, ts)
        sc = sc_ref[pl.ds(k0, ts), :]
        kpos = k0 + krow
        sel = ((sc > thr) | ((sc == thr) & (kpos <= last))) & (kpos <= qpos)
        bias = jnp.where(sel, 0.0, MASKED_SCORE)
        s = _dot_nt(k_ref[0, pl.ds(k0, ts), :], qs) + jnp.concatenate([bias] * nh, axis=1)
        m_new = jnp.maximum(m, jnp.max(s, axis=0, keepdims=True))
        alpha = jnp.exp(m - m_new)
        p = jnp.exp(s - m_new)
        l = alpha * l + jnp.sum(p, axis=0, keepdims=True)
        acc = alpha * acc + _dot(vt_ref[0, :, pl.ds(k0, ts)], p.astype(BF16))
        return m_new, l, acc

    init = (jnp.full((1, nh * tq), MASKED_SCORE, F32), jnp.zeros((1, nh * tq), F32),
            jnp.zeros((HEAD_DIM, nh * tq), F32))
    _, l, acc = lax.fori_loop(0, n_kt, key_tile, init)
    o_t = acc / l
    o_ref[...] = jnp.concatenate([o_t[:, h * tq:(h + 1) * tq] for h in range(nh)], axis=0).T


def dsa_attend(cols, b_, l_, qh, iq3, k, v, ik3):
    tq = DSA_TQ
    nt = l_ // tq
    topk = min(DSA_TOPK_MAX, l_ // 4)
    idx_bits = int(np.log2(l_))
    assert 2 ** idx_bits == l_ and l_ % DSA_KEY_TILE == 0 and topk <= DSA_KEY_TILE
    seq = lambda w: pl.BlockSpec((1, l_, w), lambda b, i: (b, 0, 0))
    return pl.pallas_call(
        functools.partial(_dsa_attend_t_kernel, topk=topk, idx_bits=idx_bits),
        out_shape=jax.ShapeDtypeStruct((b_ * l_, GROUP_WIDTH), F32),
        grid=(b_, nt),
        in_specs=[
            pl.BlockSpec((1, GROUP_HEADS, tq, HEAD_DIM), lambda b, i: (b, 0, i, 0)),
            pl.BlockSpec((1, tq, IDX_HEADS * IDX_PACK), lambda b, i: (b, i, 0)),
            pl.BlockSpec((tq, LANES), lambda b, i: (b * nt + i, C_DSA_IW // LANES)),
            seq(HEAD_DIM), pl.BlockSpec((1, HEAD_DIM, l_), lambda b, i: (b, 0, 0)), seq(IDX_PACK),
        ],
        out_specs=pl.BlockSpec((tq, GROUP_WIDTH), lambda b, i: (b * nt + i, 0)),
        scratch_shapes=[pltpu.VMEM((l_, tq), F32)],
        compiler_params=pltpu.CompilerParams(
            dimension_semantics=("arbitrary", "arbitrary"),
            vmem_limit_bytes=V7X_VMEM_LIMIT_BYTES),
        name="dsa_attend",
    )(qh, iq3, cols, k, v, ik3)


def dsa_mixer_pallas(cols, b_, l_, rope, kv_gain, w_uk, w_uv, q_gain, k_gain, idxk_gain):
    rope_i = rope_tables(l_, IDX_DIM)
    k, v, ik3 = dsa_kv_prep(cols, b_, l_, rope_tables_kv_pair(rope), rope_tables_pad(rope_i, LANES),
                            kv_gain, w_uk, w_uv, k_gain, idxk_gain)
    qh, iq3 = dsa_q_prep(cols, b_, l_, jnp.tile(rope, (1, 1, GROUP_HEADS)),
                         jnp.tile(rope_i, (1, 1, IDX_HEADS)), q_gain)
    return dsa_attend(cols, b_, l_, qh, iq3, k, v, ik3)


_TN = (((0,), (0,)), ((), ()))


def _dot_tn(a, b):
    return lax.dot_general(a, b, _TN, preferred_element_type=F32)


def _split3(t):
    hi = t.astype(BF16)
    r = t - hi.astype(F32)
    mid = r.astype(BF16)
    lo = (r - mid.astype(F32)).astype(BF16)
    return hi, mid, lo


def _tri_cumsum(tri, t):
    hi, mid, lo = _split3(t)
    return _dot(tri, hi) + _dot(tri, mid) + _dot(tri, lo)


def _cumsum_tri_rows(t, tri_u):
    hi, mid, lo = _split3(t)
    return _dot(hi, tri_u) + _dot(mid, tri_u) + _dot(lo, tri_u)


def _head_rms(o, gain):
    return jnp.concatenate([_rms_rows(o[:, h * HEAD_DIM:(h + 1) * HEAD_DIM], gain)
                            for h in range(GROUP_HEADS)], axis=-1)


HG_SUB = 8


def _hgrn2_kernel(q_ref, f_ref, i_ref, g_ref, lb_ref, gain_ref, tri_ref, ones_ref, bd_ref,
                  o_ref, st_ref, b_ref, kk_ref, v_ref):
    c = HG_CHUNK
    w = GROUP_WIDTH

    @pl.when(pl.program_id(1) == 0)
    def _():
        st_ref[...] = jnp.zeros_like(st_ref)

    lb = lb_ref[...]
    q = q_ref[...]
    qs = q * jax.nn.sigmoid(q) * (HEAD_DIM ** -0.5)
    forget = lb + (1.0 - lb) * jax.nn.sigmoid(f_ref[...])
    kk = 1.0 - forget
    bcum = _tri_cumsum(tri_ref[...], jnp.log(forget))
    v = i_ref[...]
    b_ref[...] = bcum
    kk_ref[...] = kk
    v_ref[...] = v

    out = _dot_nt((qs * jnp.exp(bcum)).astype(BF16), st_ref[...].astype(BF16))

    ones_bd = ones_ref[...]
    pieces = []
    for g in range(c // HG_SUB):
        r0 = g * HG_SUB
        nr = c - r0
        qg = qs[r0:, :]
        bg = bcum[r0:, :]
        trow = r0 + lax.broadcasted_iota(jnp.int32, (nr, w), 0)
        terms = []
        for j in range(HG_SUB):
            s = r0 + j
            d = qg * kk_ref[s:s + 1, :] * jnp.exp(bg - b_ref[s:s + 1, :])
            terms.append(jnp.where(trow >= s, d, 0.0).astype(BF16))
        red = _dot(jnp.concatenate(terms, axis=0), ones_bd)
        acc = red[0:nr] * v_ref[r0:r0 + 1, :]
        for j in range(1, HG_SUB):
            acc = acc + red[j * nr:(j + 1) * nr] * v_ref[r0 + j:r0 + j + 1, :]
        pieces.append(acc)
    intra = pieces[0]
    for g in range(1, c // HG_SUB):
        pad = jnp.zeros((g * HG_SUB, w), F32)
        intra = intra + jnp.concatenate([pad, pieces[g]], axis=0)
    out = out + intra

    b_last = bcum[c - 1:c, :]
    kt = (kk * jnp.exp(b_last - bcum)).astype(BF16)
    st_ref[...] = jnp.exp(b_last) * st_ref[...] + _dot_tn(v.astype(BF16), kt) * bd_ref[...]

    g_in = g_ref[...]
    o_ref[...] = _head_rms(out, gain_ref[...]) * (g_in * jax.nn.sigmoid(g_in))


def hgrn2_mixer_pallas(cols, b_, l_, lb, o_gain):
    c = HG_CHUNK
    nt = l_ // c
    w = GROUP_WIDTH
    head = np.arange(w) // HEAD_DIM
    same = (head[:, None] == head[None, :]).astype(np.float32)
    tri = np.tril(np.ones((c, c), np.float32))
    col = lambda j: pl.BlockSpec((c, w), lambda b, i: (b * nt + i, C_HG // w + j))
    const = lambda shape: pl.BlockSpec(shape, lambda b, i: (0,) * len(shape))
    return pl.pallas_call(
        _hgrn2_kernel,
        out_shape=jax.ShapeDtypeStruct((b_ * l_, w), F32),
        grid=(b_, nt),
        in_specs=[col(0), col(1), col(2), col(3), const((1, w)), const((1, HEAD_DIM)),
                  const((c, c)), const((w, w)), const((w, w))],
        out_specs=pl.BlockSpec((c, w), lambda b, i: (b * nt + i, 0)),
        scratch_shapes=[pltpu.VMEM((w, w), F32), pltpu.VMEM((c, w), F32),
                        pltpu.VMEM((c, w), F32), pltpu.VMEM((c, w), F32)],
        compiler_params=pltpu.CompilerParams(
            dimension_semantics=("arbitrary", "arbitrary"),
            vmem_limit_bytes=V7X_VMEM_LIMIT_BYTES),
        name="hgrn2",
    )(cols, cols, cols, cols, lb.reshape(1, w), o_gain.reshape(1, HEAD_DIM),
      jnp.asarray(tri, BF16), jnp.asarray(same, BF16), jnp.asarray(same, F32))


ML_TC = 256
ML_M_INIT = -1e30


def _mlstm_kernel(gate_ref, qk_ref, v_ref, og_ref, cw_ref, cb_ref, gb_ref, gain_ref, tril_ref,
                  triu_ref, o_ref, xprev_ref, cmat_ref, nvec_ref, m_ref):
    c = ML_TC
    nh = GROUP_HEADS
    w = GROUP_WIDTH

    @pl.when(pl.program_id(1) == 0)
    def _():
        xprev_ref[...] = jnp.zeros_like(xprev_ref)
        cmat_ref[...] = jnp.zeros_like(cmat_ref)
        nvec_ref[...] = jnp.zeros_like(nvec_ref)
        m_ref[...] = jnp.full(m_ref.shape, ML_M_INIT, F32)

    x = qk_ref[...]
    prev = xprev_ref[...]
    row = lax.broadcasted_iota(jnp.int32, x.shape, 0)
    acc = x * cw_ref[CONV_WIDTH - 1:CONV_WIDTH, :] + cb_ref[...]
    for j in range(1, CONV_WIDTH):
        shifted = jnp.where(row < j, pltpu.roll(prev, j, 0), pltpu.roll(x, j, 0))
        acc = acc + shifted * cw_ref[CONV_WIDTH - 1 - j:CONV_WIDTH - j, :]
    xprev_ref[...] = x
    qk = acc * jax.nn.sigmoid(acc)
    q = qk[:, :w]
    k = qk[:, w:] * (HEAD_DIM ** -0.5)
    v = v_ref[...]

    pre = gate_ref[...] + gb_ref[...]
    lane = lax.broadcasted_iota(jnp.int32, pre.shape, 1)
    log_f = jnp.minimum(pre, 0.0) - jnp.log1p(jnp.exp(-jnp.abs(pre)))
    log_f = jnp.where((lane >= nh) & (lane < 2 * nh), log_f, 0.0)
    bcum_c = _tri_cumsum(tril_ref[...], log_f)
    bcum_r = _cumsum_tri_rows(log_f.T, triu_ref[...])
    pre_r = pre.T
    tri = (lax.broadcasted_iota(jnp.int32, (c, c), 0) >= lax.broadcasted_iota(jnp.int32, (c, c), 1))

    outs = []
    for h in range(nh):
        sl = slice(h * HEAD_DIM, (h + 1) * HEAD_DIM)
        qh, kh, vh = q[:, sl], k[:, sl], v[:, sl]
        bc = bcum_c[:, nh + h:nh + h + 1]
        li_c = pre[:, h:h + 1]
        a_r = pre_r[h:h + 1, :] - bcum_r[nh + h:nh + h + 1, :]
        m_prev = m_ref[h:h + 1, 0:1]
        log_d = jnp.where(tri, bc + a_r, -jnp.inf)
        inter = bc + m_prev
        m_t = jnp.maximum(inter, jnp.max(log_d, axis=-1, keepdims=True))
        d_mat = jnp.exp(log_d - m_t)
        w_inter = jnp.exp(inter - m_t)
        qb = qh.astype(BF16)
        s = _dot_nt(qb, kh.astype(BF16)) * d_mat
        num = w_inter * _dot(qb, cmat_ref[h].astype(BF16)) + _dot(s.astype(BF16), vh.astype(BF16))
        den = (w_inter * jnp.sum(qh * nvec_ref[h], axis=-1, keepdims=True)
               + jnp.sum(s, axis=-1, keepdims=True))
        outs.append(num / jnp.maximum(jnp.abs(den), jnp.exp(-m_t)))
        b_last = bc[c - 1:c, :]
        log_w = b_last + (li_c - bc)
        m_new = jnp.maximum(b_last + m_prev, jnp.max(log_w, axis=0, keepdims=True))
        kw = kh * jnp.exp(log_w - m_new)
        decay = jnp.exp(b_last + m_prev - m_new)
        cmat_ref[h] = decay * cmat_ref[h] + _dot_tn(kw.astype(BF16), vh.astype(BF16))
        nvec_ref[h] = decay * nvec_ref[h] + jnp.sum(kw, axis=0, keepdims=True)
        m_ref[h:h + 1, :] = jnp.broadcast_to(m_new, (1, LANES))

    hh = _head_rms(jnp.concatenate(outs, axis=-1), gain_ref[...])
    o_ref[...] = hh * jax.nn.sigmoid(og_ref[...])


def mlstm_mixer_pallas(cols, b_, l_, conv_w, conv_b, i_bias, f_bias, o_gain):
    c = ML_TC
    nt = l_ // c
    w = GROUP_WIDTH
    gb = jnp.pad(jnp.concatenate([i_bias, f_bias]), (0, LANES - 2 * GROUP_HEADS)).reshape(1, LANES)
    tril = np.tril(np.ones((c, c), np.float32))
    const = lambda shape: pl.BlockSpec(shape, lambda b, i: (0,) * len(shape))
    blk = lambda width, off: pl.BlockSpec((c, width), lambda b, i: (b * nt + i, off // width))
    return pl.pallas_call(
        _mlstm_kernel,
        out_shape=jax.ShapeDtypeStruct((b_ * l_, w), F32),
        grid=(b_, nt),
        in_specs=[blk(LANES, C_ML_GATE), blk(2 * w, C_ML_QK), blk(w, C_ML_V), blk(w, C_ML_OG),
                  const((CONV_WIDTH, 2 * w)), const((1, 2 * w)), const((1, LANES)),
                  const((1, HEAD_DIM)), const((c, c)), const((c, c))],
        out_specs=pl.BlockSpec((c, w), lambda b, i: (b * nt + i, 0)),
        scratch_shapes=[pltpu.VMEM((c, 2 * w), F32),
                        pltpu.VMEM((GROUP_HEADS, HEAD_DIM, HEAD_DIM), F32),
                        pltpu.VMEM((GROUP_HEADS, 1, HEAD_DIM), F32),
                        pltpu.VMEM((8, LANES), F32)],
        compiler_params=pltpu.CompilerParams(
            dimension_semantics=("arbitrary", "arbitrary"),
            vmem_limit_bytes=V7X_VMEM_LIMIT_BYTES),
        name="mlstm",
    )(cols, cols, cols, cols, conv_w, conv_b.reshape(1, 2 * w), gb, o_gain.reshape(1, HEAD_DIM),
      jnp.asarray(tril, BF16), jnp.asarray(tril.T, BF16))


def split_cols(t, sizes):
    return jnp.split(t, [int(s) for s in np.cumsum(sizes)[:-1]], axis=-1)


def split_heads(t, n):
    return t.reshape(t.shape[:-1] + (n, t.shape[-1] // n))


def rms_norm(t, g):
    t32 = t.astype(F32)
    y = t32 * lax.rsqrt(jnp.mean(t32 * t32, axis=-1, keepdims=True) + EPS)
    return (y * g.astype(F32)).astype(t.dtype)


def partial_rope(t, pos):
    d = t.shape[-1]
    rd = d // 4
    half = rd // 2
    inv = ROPE_THETA ** (-jnp.arange(half, dtype=F32) * 2.0 / rd)
    ang = pos.astype(F32)[:, None] * inv[None, :]
    cos = jnp.cos(ang)[:, None, :].astype(t.dtype)
    sin = jnp.sin(ang)[:, None, :].astype(t.dtype)
    x1, x2 = t[..., :half], t[..., half:rd]
    return jnp.concatenate([x1 * cos - x2 * sin, x2 * cos + x1 * sin, t[..., rd:]], axis=-1)


def masked_softmax(s, mask):
    s = jnp.where(mask, s.astype(F32), -jnp.inf)
    m = jnp.max(s, axis=-1, keepdims=True)
    m = jnp.where(jnp.isfinite(m), m, 0.0)
    e = jnp.exp(s - m)
    den = jnp.sum(e, axis=-1, keepdims=True)
    return e / jnp.where(den > 0, den, 1.0)


def gather_rows(table, idx):
    return jax.vmap(lambda tb, ix: tb[ix])(table, idx)


def causal_conv(t, w, b):
    y = lax.conv_general_dilated(t, w[:, None, :].astype(t.dtype), window_strides=(1,),
                                 padding=[(CONV_WIDTH - 1, 0)],
                                 dimension_numbers=('NWC', 'WIO', 'NWC'),
                                 feature_group_count=t.shape[-1])
    return y + b.astype(t.dtype)


def to_chunks(t, c):
    b_, l_, h_ = t.shape[:3]
    t = t.reshape((b_, l_ // c, c, h_) + t.shape[3:])
    return jnp.moveaxis(t, (1, 3), (0, 2))


def from_chunks(t):
    t = jnp.moveaxis(t, (0, 2), (1, 3))
    return t.reshape((t.shape[0], t.shape[1] * t.shape[2], t.shape[3]) + t.shape[4:])


def unblock(o):
    o = jnp.moveaxis(o, 0, 1)
    return o.reshape((o.shape[0], o.shape[1] * o.shape[2]) + o.shape[3:])


def hgrn2_mixer(cols, lb, o_gain):
    b_, l_ = cols.shape[:2]
    q, f, i, g = split_cols(cols, HG_SPLITS)
    q = split_heads(jax.nn.silu(q), GROUP_HEADS).astype(F32) * HEAD_DIM ** -0.5
    forget = lb + (1.0 - lb) * jax.nn.sigmoid(f.astype(F32))
    k = split_heads(1.0 - forget, GROUP_HEADS)
    logf = split_heads(jnp.log(forget), GROUP_HEADS)
    v = split_heads(i, GROUP_HEADS).astype(F32)
    tri = jnp.tril(jnp.ones((HG_CHUNK, HG_CHUNK), bool))[:, :, None]

    def step(state, inp):
        qc, kc, vc, gc = inp
        bcum = jnp.cumsum(gc, axis=2)
        o_inter = jnp.einsum('bhtk,bhkv->bhtv', qc * jnp.exp(bcum), state)
        diff = bcum[:, :, :, None, :] - bcum[:, :, None, :, :]
        decay = jnp.exp(jnp.where(tri, diff, -jnp.inf))
        attn = jnp.einsum('bhtk,bhsk,bhtsk->bhts', qc, kc, decay)
        out = o_inter + jnp.einsum('bhts,bhsv->bhtv', attn, vc)
        b_last = bcum[:, :, -1:, :]
        state = (jnp.exp(b_last[:, :, 0, :])[..., None] * state
                 + jnp.einsum('bhsk,bhsv->bhkv', kc * jnp.exp(b_last - bcum), vc))
        return state, out

    s0 = jnp.zeros((b_, GROUP_HEADS, HEAD_DIM, HEAD_DIM), F32)
    xs = (to_chunks(q, HG_CHUNK), to_chunks(k, HG_CHUNK), to_chunks(v, HG_CHUNK),
          to_chunks(logf, HG_CHUNK))
    _, o = lax.scan(step, s0, xs)
    o = rms_norm(from_chunks(o), o_gain).astype(cols.dtype)
    o = o * jax.nn.silu(split_heads(g, GROUP_HEADS))
    return o.reshape(b_, l_, GROUP_WIDTH)


def dsa_mixer(cols, pos, kv_gain, w_uk, w_uv, q_gain, k_gain, idxk_gain):
    b_, l_ = cols.shape[:2]
    q, ckv, iq, ik, iw = split_cols(cols, DSA_SPLITS)
    q = partial_rope(rms_norm(split_heads(q, GROUP_HEADS), q_gain), pos)
    ckv = rms_norm(ckv, kv_gain)
    k = partial_rope(rms_norm(ckv @ w_uk, k_gain)[:, :, None, :], pos)[:, :, 0]
    v = ckv @ w_uv
    iq = partial_rope(split_heads(iq, IDX_HEADS), pos)
    ik = partial_rope(rms_norm(ik, idxk_gain)[:, :, None, :], pos)[:, :, 0]
    iw = iw * (IDX_HEADS ** -0.5 * IDX_DIM ** -0.5)
    topk = min(DSA_TOPK_MAX, l_ // 4)
    key_pos = jnp.arange(l_)

    def block(bi):
        t0 = bi * Q_BLOCK
        qpos = t0 + jnp.arange(Q_BLOCK)
        qb = lax.dynamic_slice_in_dim(q, t0, Q_BLOCK, axis=1)
        iqb = lax.dynamic_slice_in_dim(iq, t0, Q_BLOCK, axis=1)
        iwb = lax.dynamic_slice_in_dim(iw, t0, Q_BLOCK, axis=1)
        rel = jax.nn.relu(jnp.einsum('bthd,bsd->bhts', iqb, ik))
        score = jnp.einsum('bhts,bth->bts', rel, iwb).astype(F32)
        causal = key_pos[None, :] <= qpos[:, None]
        score = jnp.where(causal[None], score, -jnp.inf)
        _, idx = lax.top_k(score, topk)
        k_sel = gather_rows(k, idx)
        v_sel = gather_rows(v, idx)
        s = jnp.einsum('bthd,btkd->bhtk', qb, k_sel) * HEAD_DIM ** -0.5
        p = masked_softmax(s, (idx <= qpos[None, :, None])[:, None]).astype(v.dtype)
        return jnp.einsum('bhtk,btkd->bthd', p, v_sel)

    o = unblock(lax.map(block, jnp.arange(l_ // Q_BLOCK)))
    return o.reshape(b_, l_, GROUP_WIDTH)


def nsa_mixer(cols, pos, pos_k, pos_v, k_w1, k_w2, v_w1, v_w2, q_gain, k_gains):
    b_, l_ = cols.shape[:2]
    scale = HEAD_DIM ** -0.5
    q, kc, vc, ks, vs, kw, vw, gates = split_cols(cols, NSA_SPLITS)
    q = rms_norm(split_heads(q, GROUP_HEADS), q_gain)
    q_rot = partial_rope(q, pos)
    gates = jax.nn.sigmoid(gates.reshape(b_, l_, 3, GROUP_HEADS, 1))

    n_cmp = (l_ - CMP_BLOCK) // CMP_STRIDE + 1
    cmp_idx = np.arange(n_cmp)[:, None] * CMP_STRIDE + np.arange(CMP_BLOCK)[None, :]

    def compress(t, pe, w1, w2):
        blk = (t[:, cmp_idx] + pe).reshape(b_, n_cmp, CMP_BLOCK * HEAD_DIM)
        return jax.nn.relu(blk @ w1) @ w2

    k_cmp = rms_norm(compress(kc, pos_k, k_w1, k_w2), k_gains[0])
    v_cmp = compress(vc, pos_v, v_w1, v_w2)
    cmp_vis = cmp_idx[:, -1][None, :] <= np.arange(l_)[:, None]
    p_cmp = masked_softmax(jnp.einsum('bthd,bjd->bhtj', q, k_cmp) * scale, cmp_vis)
    o_cmp = jnp.einsum('bhtj,bjd->bthd', p_cmp.astype(v_cmp.dtype), v_cmp)

    n_slc = l_ // SLC_BLOCK
    n_sel = min(SLC_TOPN, n_slc)
    st_c = np.arange(n_cmp) * CMP_STRIDE
    st_s = np.arange(n_slc) * SLC_BLOCK
    overlap = ((st_c[:, None] < st_s[None, :] + SLC_BLOCK)
               & (st_c[:, None] + CMP_BLOCK > st_s[None, :])).astype(np.float32)
    imp = jnp.einsum('bhtj,jn->btn', p_cmp, overlap)
    cur = np.arange(l_)[:, None] // SLC_BLOCK
    blk_id = np.arange(n_slc)[None, :]
    forced = (blk_id == 0) | (blk_id == cur) | (blk_id == cur - 1)
    imp = jnp.where(forced, jnp.inf, jnp.where(blk_id > cur, -jnp.inf, imp))
    _, sel = lax.top_k(imp, n_sel)

    k_s = partial_rope(rms_norm(ks, k_gains[1])[:, :, None], pos)[:, :, 0]
    k_blocks = k_s.reshape(b_, n_slc, SLC_BLOCK, HEAD_DIM)
    v_blocks = vs.reshape(b_, n_slc, SLC_BLOCK, HEAD_DIM)
    k_w = partial_rope(rms_norm(kw, k_gains[2])[:, :, None], pos)[:, :, 0]
    k_pad = jnp.pad(k_w, ((0, 0), (WINDOW, 0), (0, 0)))
    v_pad = jnp.pad(vw, ((0, 0), (WINDOW, 0), (0, 0)))

    def block(bi):
        t0 = bi * Q_BLOCK
        qpos = t0 + jnp.arange(Q_BLOCK)
        qb = lax.dynamic_slice_in_dim(q_rot, t0, Q_BLOCK, axis=1)
        sb = lax.dynamic_slice_in_dim(sel, t0, Q_BLOCK, axis=1)
        kb = gather_rows(k_blocks, sb)
        vb = gather_rows(v_blocks, sb)
        kpos = sb[..., None] * SLC_BLOCK + jnp.arange(SLC_BLOCK)
        s = jnp.einsum('bthd,btnsd->bhtns', qb, kb) * scale
        valid = (kpos <= qpos[None, :, None, None])[:, None]
        p = masked_softmax(s.reshape(b_, GROUP_HEADS, Q_BLOCK, n_sel * SLC_BLOCK),
                           valid.reshape(b_, 1, Q_BLOCK, n_sel * SLC_BLOCK))
        p = p.reshape(b_, GROUP_HEADS, Q_BLOCK, n_sel, SLC_BLOCK).astype(vb.dtype)
        o_slc = jnp.einsum('bhtns,btnsd->bthd', p, vb)
        kwb = lax.dynamic_slice_in_dim(k_pad, t0, WINDOW + Q_BLOCK, axis=1)
        vwb = lax.dynamic_slice_in_dim(v_pad, t0, WINDOW + Q_BLOCK, axis=1)
        wpos = t0 - WINDOW + jnp.arange(WINDOW + Q_BLOCK)
        dist = qpos[:, None] - wpos[None, :]
        wvalid = (dist >= 0) & (dist < WINDOW) & (wpos[None, :] >= 0)
        sw = jnp.einsum('bthd,bsd->bhts', qb, kwb) * scale
        pw = masked_softmax(sw, wvalid[None, None]).astype(vwb.dtype)
        o_swa = jnp.einsum('bhts,bsd->bthd', pw, vwb)
        return o_slc, o_swa

    o_slc, o_swa = lax.map(block, jnp.arange(l_ // Q_BLOCK))
    o = gates[:, :, 0] * o_cmp + gates[:, :, 1] * unblock(o_slc) + gates[:, :, 2] * unblock(o_swa)
    return o.reshape(b_, l_, GROUP_WIDTH)


def mlstm_mixer(cols, conv_w, conv_b, i_bias, f_bias, o_gain):
    b_, l_ = cols.shape[:2]
    qk, v, og, ig, fg = split_cols(cols, ML_SPLITS)
    qk = jax.nn.silu(causal_conv(qk, conv_w, conv_b))
    q, k = jnp.split(qk, 2, axis=-1)
    q = split_heads(q, GROUP_HEADS).astype(F32)
    k = split_heads(k, GROUP_HEADS).astype(F32) * HEAD_DIM ** -0.5
    v = split_heads(v, GROUP_HEADS).astype(F32)
    log_i = (ig + i_bias).astype(F32)
    log_f = jax.nn.log_sigmoid((fg + f_bias).astype(F32))
    tri = jnp.tril(jnp.ones((ML_CHUNK, ML_CHUNK), bool))

    def step(carry, inp):
        cmat, nvec, m = carry
        qc, kc, vc, li, lf = inp
        bcum = jnp.cumsum(lf, axis=-1)
        log_d = jnp.where(tri, bcum[..., :, None] - bcum[..., None, :] + li[..., None, :], -jnp.inf)
        inter = bcum + m[..., None]
        m_t = jnp.maximum(inter, jnp.max(log_d, axis=-1))
        d_mat = jnp.exp(log_d - m_t[..., None])
        w_inter = jnp.exp(inter - m_t)
        s = jnp.einsum('bhtd,bhsd->bhts', qc, kc) * d_mat
        num = (w_inter[..., None] * jnp.einsum('bhtd,bhdv->bhtv', qc, cmat)
               + jnp.einsum('bhts,bhsv->bhtv', s, vc))
        den = w_inter * jnp.einsum('bhtd,bhd->bht', qc, nvec) + jnp.sum(s, axis=-1)
        h = num / jnp.maximum(jnp.abs(den), jnp.exp(-m_t))[..., None]
        b_last = bcum[..., -1]
        log_w = b_last[..., None] - bcum + li
        m_new = jnp.maximum(b_last + m, jnp.max(log_w, axis=-1))
        w_s = jnp.exp(log_w - m_new[..., None])
        decay = jnp.exp(b_last + m - m_new)
        cmat = decay[..., None, None] * cmat + jnp.einsum('bhs,bhsd,bhsv->bhdv', w_s, kc, vc)
        nvec = decay[..., None] * nvec + jnp.einsum('bhs,bhsd->bhd', w_s, kc)
        return (cmat, nvec, m_new), h

    init = (jnp.zeros((b_, GROUP_HEADS, HEAD_DIM, HEAD_DIM), F32),
            jnp.zeros((b_, GROUP_HEADS, HEAD_DIM), F32),
            jnp.full((b_, GROUP_HEADS), -1e30, F32))
    xs = (to_chunks(q, ML_CHUNK), to_chunks(k, ML_CHUNK), to_chunks(v, ML_CHUNK),
          to_chunks(log_i, ML_CHUNK), to_chunks(log_f, ML_CHUNK))
    _, h = lax.scan(step, init, xs)
    h = rms_norm(from_chunks(h), o_gain).astype(cols.dtype)
    h = h * jax.nn.sigmoid(split_heads(og, GROUP_HEADS))
    return h.reshape(b_, l_, GROUP_WIDTH)


def kernel(x, mem, lb_param, norm_mix, w_in, w_out, hg_o_gain, dsa_kv_gain, dsa_w_uk, dsa_w_uv,
           dsa_q_gain, dsa_k_gain, dsa_idxk_gain, nsa_pos_k, nsa_pos_v, nsa_k_w1, nsa_k_w2,
           nsa_v_w1, nsa_v_w2, nsa_q_gain, nsa_k_gains, ml_conv_w, ml_conv_b, ml_i_bias,
           ml_f_bias, ml_o_gain, norm_xa, norm_mem, xa_wq, xa_wkv, xa_wo, xa_q_gain, xa_k_gain,
           norm_ffn, ffn_w13, ffn_w2):
    b_, l_, d = x.shape
    pos = jnp.arange(l_)
    lb_all = jnp.cumsum(jax.nn.softmax(lb_param.astype(F32), axis=0), axis=0)
    lb_all = lb_all - lb_all[:1]
    x2d = x.reshape(b_ * l_, d)
    mem2d = mem.reshape(b_ * N_MEM, d)
    rope = rope_tables(l_)
    for l in range(DEPTH):
        cols = norm_matmul(x2d, norm_mix[l], pack_w_in(w_in[l]).astype(BF16), tm=256)
        mixed = jnp.concatenate([
            hgrn2_mixer_pallas(cols, b_, l_, lb_all[l], hg_o_gain[l]),
            dsa_mixer_pallas(cols, b_, l_, rope, dsa_kv_gain[l], dsa_w_uk[l], dsa_w_uv[l],
                             dsa_q_gain[l], dsa_k_gain[l], dsa_idxk_gain[l]),
            nsa_mixer_pallas(cols, b_, l_, rope, nsa_pos_k[l], nsa_pos_v[l], nsa_k_w1[l],
                             nsa_k_w2[l], nsa_v_w1[l], nsa_v_w2[l], nsa_q_gain[l],
                             nsa_k_gains[l]),
            mlstm_mixer_pallas(cols, b_, l_, ml_conv_w[l], ml_conv_b[l], ml_i_bias[l],
                               ml_f_bias[l], ml_o_gain[l]),
        ], axis=-1)
        k_mem, v_mem = mem_kv(mem2d, norm_mem[l], xa_wkv[l].astype(BF16), xa_k_gain[l])
        x2d = post_block(x2d, mixed.reshape(b_ * l_, -1), l_, w_out[l].astype(BF16), norm_xa[l],
                         xa_wq[l].astype(BF16), xa_q_gain[l], k_mem, v_mem,
                         xa_wo[l].astype(BF16), norm_ffn[l], ffn_w13[l].astype(BF16),
                         ffn_w2[l].astype(BF16), tm=256)
    return x2d.reshape(b_, l_, d)
```

```python
import functools

import jax
import jax.numpy as jnp
from jax import lax
import numpy as np
from jax.experimental import pallas as pl
from jax.experimental.pallas import tpu as pltpu

F32 = jnp.float32
BF16 = jnp.bfloat16

D_MODEL = 1024
DEPTH = 2
HEAD_DIM = 64
GROUP_HEADS = 4
GROUP_WIDTH = GROUP_HEADS * HEAD_DIM
ROPE_THETA = 500000.0
EPS = 1e-6
Q_BLOCK = 128
N_MEM = 256
XA_HEADS = 4
XA_WIDTH = XA_HEADS * HEAD_DIM
HG_CHUNK = 64
DSA_LATENT = 128
IDX_HEADS = 8
IDX_DIM = 32
DSA_TOPK_MAX = 256
CMP_BLOCK = 32
CMP_STRIDE = 16
SLC_BLOCK = 64
SLC_SHIFT = 6
SLC_TOPN = 16
WINDOW = 512
ML_CHUNK = 64
CONV_WIDTH = 4
D_FF = 2816

HG_SPLITS = (GROUP_WIDTH,) * 4
DSA_SPLITS = (GROUP_WIDTH, DSA_LATENT, IDX_HEADS * IDX_DIM, IDX_DIM, IDX_HEADS)
NSA_SPLITS = (GROUP_WIDTH,) + (HEAD_DIM,) * 6 + (3 * GROUP_HEADS,)
ML_SPLITS = (2 * GROUP_WIDTH, GROUP_WIDTH, GROUP_WIDTH, GROUP_HEADS, GROUP_HEADS)
GROUP_COLS = (sum(HG_SPLITS), sum(DSA_SPLITS), sum(NSA_SPLITS), sum(ML_SPLITS))
IN_COLS = sum(GROUP_COLS)

V7X_VMEM_LIMIT_BYTES = 56 * 1024 * 1024
LANES = 128
FF_CHUNK = 256


def _round_up(n, m):
    return -(-n // m) * m


def _rms_rows(t, g):
    return t * lax.rsqrt(jnp.mean(t * t, axis=-1, keepdims=True) + EPS) * g


def _const_spec(shape):
    return pl.BlockSpec(shape, lambda *_: (0,) * len(shape), pipeline_mode=pl.Buffered(1))


def _norm_matmul_kernel(x_ref, g_ref, w_ref, o_ref):
    h = _rms_rows(x_ref[...], g_ref[...]).astype(BF16)
    o_ref[...] = jnp.dot(h, w_ref[...], preferred_element_type=F32)


def norm_matmul(x2d, gain, w_bf16, tm):
    m, k = x2d.shape
    n = w_bf16.shape[1]
    return pl.pallas_call(
        _norm_matmul_kernel,
        out_shape=jax.ShapeDtypeStruct((m, n), F32),
        grid=(m // tm,),
        in_specs=[
            pl.BlockSpec((tm, k), lambda i: (i, 0)),
            _const_spec((1, k)),
            _const_spec((k, n)),
        ],
        out_specs=pl.BlockSpec((tm, n), lambda i: (i, 0)),
        compiler_params=pltpu.CompilerParams(
            dimension_semantics=("arbitrary",), vmem_limit_bytes=V7X_VMEM_LIMIT_BYTES),
        name="norm_matmul",
    )(x2d, gain.reshape(1, k), w_bf16)


def _mem_kv_kernel(m_ref, g_ref, w_ref, kg_ref, k_ref, v_ref):
    mn = _rms_rows(m_ref[...], g_ref[...]).astype(BF16)
    kv = jnp.dot(mn, w_ref[...], preferred_element_type=F32)
    kg = kg_ref[...]
    ks = []
    for h in range(XA_HEADS):
        kh = kv[:, h * HEAD_DIM:(h + 1) * HEAD_DIM]
        ks.append(_rms_rows(kh, kg))
    k_ref[...] = jnp.concatenate(ks, axis=-1).astype(BF16)
    v_ref[...] = kv[:, XA_WIDTH:].astype(BF16)


def mem_kv(mem2d, gain, wkv_bf16, k_gain):
    m, k = mem2d.shape
    return pl.pallas_call(
        _mem_kv_kernel,
        out_shape=(jax.ShapeDtypeStruct((m, XA_WIDTH), BF16),
                   jax.ShapeDtypeStruct((m, XA_WIDTH), BF16)),
        grid=(m // N_MEM,),
        in_specs=[
            pl.BlockSpec((N_MEM, k), lambda i: (i, 0)),
            _const_spec((1, k)),
            _const_spec((k, 2 * XA_WIDTH)),
            _const_spec((1, HEAD_DIM)),
        ],
        out_specs=(pl.BlockSpec((N_MEM, XA_WIDTH), lambda i: (i, 0)),
                   pl.BlockSpec((N_MEM, XA_WIDTH), lambda i: (i, 0))),
        compiler_params=pltpu.CompilerParams(
            dimension_semantics=("arbitrary",), vmem_limit_bytes=V7X_VMEM_LIMIT_BYTES),
        name="mem_kv",
    )(mem2d, gain.reshape(1, k), wkv_bf16, k_gain.reshape(1, HEAD_DIM))


def _post_kernel(x_ref, mix_ref, wout_ref, gxa_ref, wq_ref, qg_ref, k_ref, v_ref, wo_ref,
                 gffn_ref, w13_ref, w2_ref, o_ref):
    x = x_ref[...] + _dot(mix_ref[...].astype(BF16), wout_ref[...])
    h = _rms_rows(x, gxa_ref[...]).astype(BF16)
    q = _dot(h, wq_ref[...])
    qg = qg_ref[...] * (HEAD_DIM ** -0.5)
    k = k_ref[...]
    v = v_ref[...]
    heads = [slice(hd * HEAD_DIM, (hd + 1) * HEAD_DIM) for hd in range(XA_HEADS)]
    scores = [_dot_nt(_rms_rows(q[:, sl], qg).astype(BF16), k[:, sl]) for sl in heads]
    probs = []
    for s in scores:
        e = jnp.exp(s - jnp.max(s, axis=-1, keepdims=True))
        probs.append((e / jnp.sum(e, axis=-1, keepdims=True)).astype(BF16))
    o = jnp.concatenate([_dot(p, v[:, sl]) for p, sl in zip(probs, heads)], axis=-1)
    x = x + _dot(o.astype(BF16), wo_ref[...])
    h = _rms_rows(x, gffn_ref[...]).astype(BF16)

    def up(c):
        return (_dot(h, w13_ref[:, c * FF_CHUNK:(c + 1) * FF_CHUNK]),
                _dot(h, w13_ref[:, D_FF + c * FF_CHUNK:D_FF + (c + 1) * FF_CHUNK]))

    n_chunks = D_FF // FF_CHUNK
    acc = x
    a, b = up(0)
    for c in range(n_chunks):
        nxt = up(c + 1) if c + 1 < n_chunks else None
        act = (a * jax.nn.sigmoid(a) * b).astype(BF16)
        acc = acc + _dot(act, w2_ref[c * FF_CHUNK:(c + 1) * FF_CHUNK, :])
        if nxt is not None:
            a, b = nxt
    o_ref[...] = acc


def post_block(x2d, mixed2d, seq, wout, gxa, wq, qg, k_mem, v_mem, wo, gffn, w13, w2, tm):
    m, d = x2d.shape
    tiles_per_batch = seq // tm
    row = lambda i: (i, 0)
    mem_row = lambda i: (i // tiles_per_batch, 0)
    return pl.pallas_call(
        _post_kernel,
        out_shape=jax.ShapeDtypeStruct((m, d), F32),
        grid=(m // tm,),
        in_specs=[
            pl.BlockSpec((tm, d), row),
            pl.BlockSpec((tm, mixed2d.shape[1]), row),
            _const_spec(wout.shape),
            _const_spec((1, d)),
            _const_spec(wq.shape),
            _const_spec((1, HEAD_DIM)),
            pl.BlockSpec((N_MEM, XA_WIDTH), mem_row),
            pl.BlockSpec((N_MEM, XA_WIDTH), mem_row),
            _const_spec(wo.shape),
            _const_spec((1, d)),
            _const_spec(w13.shape),
            _const_spec(w2.shape),
        ],
        out_specs=pl.BlockSpec((tm, d), row),
        compiler_params=pltpu.CompilerParams(
            dimension_semantics=("arbitrary",), vmem_limit_bytes=V7X_VMEM_LIMIT_BYTES),
        name="post_block",
    )(x2d, mixed2d, wout, gxa.reshape(1, d), wq, qg.reshape(1, HEAD_DIM), k_mem, v_mem, wo,
      gffn.reshape(1, d), w13, w2)


C_HG = 0
C_DSA_Q = 1024
C_DSA_IQ = 1280
C_NSA_Q = 1536
C_ML_V = 1792
C_ML_QK = 2048
C_ML_OG = 2560
C_DSA_CKV = 2816
C_DSA_IK = 2944
C_DSA_IW = 3072
C_NSA_GATE = 3200
C_NSA_CMP = 3328
C_NSA_SLC = 3456
C_NSA_WIN = 3584
C_ML_GATE = 3712
IN_COLS_PACKED = 3840


def _packed_source_columns():
    dsa0 = GROUP_COLS[0]
    nsa0 = dsa0 + GROUP_COLS[1]
    ml0 = nsa0 + GROUP_COLS[2]
    segs = [(0, 1024, 1024), (dsa0, 256, 256), (dsa0 + 384, 256, 256), (nsa0, 256, 256),
            (ml0 + 512, 256, 256), (ml0, 512, 512), (ml0 + 768, 256, 256), (dsa0 + 256, 128, 128),
            (dsa0 + 640, 32, LANES), (dsa0 + 672, 8, LANES), (nsa0 + 640, 12, LANES),
            (nsa0 + 256, 128, 128), (nsa0 + 384, 128, 128), (nsa0 + 512, 128, 128),
            (ml0 + 1024, 8, LANES)]
    src = np.concatenate([np.concatenate([np.arange(a, a + n), np.full(width - n, -1)])
                          for a, n, width in segs]).astype(np.int32)
    assert src.shape == (IN_COLS_PACKED,)
    return src


PACK_TN = 256


def _pack_w_kernel(w_ref, src_ref, o_ref, wb_ref):
    @pl.when(pl.program_id(0) == 0)
    def _():
        wb_ref[...] = w_ref[...].astype(BF16)

    row = lax.broadcasted_iota(jnp.int32, (wb_ref.shape[1], PACK_TN), 0)
    sel = jnp.where(row == src_ref[...], 1.0, 0.0).astype(BF16)
    o_ref[...] = _dot(wb_ref[...], sel).astype(BF16)


def pack_w_in_bf16(w):
    d, n = w.shape
    n_pad = _round_up(n, LANES)
    w_pad = jnp.pad(w, ((0, 0), (0, n_pad - n)))
    src = jnp.asarray(_packed_source_columns()).reshape(1, IN_COLS_PACKED)
    return pl.pallas_call(
        _pack_w_kernel,
        out_shape=jax.ShapeDtypeStruct((d, IN_COLS_PACKED), BF16),
        grid=(IN_COLS_PACKED // PACK_TN,),
        in_specs=[_const_spec((d, n_pad)), pl.BlockSpec((1, PACK_TN), lambda j: (0, j))],
        out_specs=pl.BlockSpec((d, PACK_TN), lambda j: (0, j)),
        scratch_shapes=[pltpu.VMEM((d, n_pad), BF16)],
        compiler_params=pltpu.CompilerParams(
            dimension_semantics=("arbitrary",), vmem_limit_bytes=V7X_VMEM_LIMIT_BYTES),
        name="pack_w_in",
    )(w_pad, src)


def pack_w_in(w):
    dsa0 = GROUP_COLS[0]
    nsa0 = dsa0 + GROUP_COLS[1]
    ml0 = nsa0 + GROUP_COLS[2]

    def seg(a, n, width=None):
        s = w[:, a:a + n]
        if width is not None and width > n:
            s = jnp.pad(s, ((0, 0), (0, width - n)))
        return s

    parts = [
        seg(0, 1024),
        seg(dsa0, 256), seg(dsa0 + 384, 256),
        seg(nsa0, 256),
        seg(ml0 + 512, 256), seg(ml0, 512), seg(ml0 + 768, 256),
        seg(dsa0 + 256, 128),
        seg(dsa0 + 640, 32, LANES), seg(dsa0 + 672, 8, LANES),
        seg(nsa0 + 640, 12, LANES),
        seg(nsa0 + 256, 128), seg(nsa0 + 384, 128), seg(nsa0 + 512, 128),
        seg(ml0 + 1024, 8, LANES),
    ]
    packed = jnp.concatenate(parts, axis=1)
    assert packed.shape[1] == IN_COLS_PACKED
    return packed


def rope_tables(l_, d=HEAD_DIM):
    rd = d // 4
    half = rd // 2
    inv = ROPE_THETA ** (-jnp.arange(half, dtype=F32) * 2.0 / rd)
    ang = jnp.arange(l_).astype(F32)[:, None] * inv[None, :]
    cos, sin = jnp.cos(ang), jnp.sin(ang)
    zh = jnp.zeros((l_, half), F32)
    rest0 = jnp.zeros((l_, d - rd), F32)
    c = jnp.concatenate([cos, cos, rest0 + 1.0], axis=1)
    s1 = jnp.concatenate([-sin, zh, rest0], axis=1)
    s2 = jnp.concatenate([zh, sin, rest0], axis=1)
    return jnp.stack([c, s1, s2])


def rope_tables_pad(rt, width):
    n = width - rt.shape[-1]
    ident = jnp.stack([jnp.ones(rt.shape[1:2] + (n,), F32), jnp.zeros(rt.shape[1:2] + (n,), F32),
                       jnp.zeros(rt.shape[1:2] + (n,), F32)])
    return jnp.concatenate([rt, ident], axis=-1)


def rope_tables_kv_pair(rt):
    return rope_tables_pad(rt, 2 * rt.shape[-1])


def _apply_rope(t, rope_ref, half=HEAD_DIM // 8):
    w = t.shape[-1]
    return (t * rope_ref[0] + pltpu.roll(t, w - half, 1) * rope_ref[1]
            + pltpu.roll(t, half, 1) * rope_ref[2])


_NT = (((1,), (1,)), ((), ()))


def _dot_nt(a, b):
    return lax.dot_general(a, b, _NT, preferred_element_type=F32)


def _dot(a, b):
    return jnp.dot(a, b, preferred_element_type=F32)


NEG_BIG = -(2.0 ** 30)
NSA_KEY_TILE = 512
NSA_TQ = 128
NSA_SEL_TQ = 256
NSA_PREP_TK = 512


def _nsa_kv_prep_kernel(ps_ref, pw_ref, rope_ref, gs_ref, gw_ref, kaug_ref, vs_ref, kw_ref, vw_ref):
    tk = ps_ref.shape[0]
    lane = lax.broadcasted_iota(jnp.int32, (tk, LANES), 1)
    is_k = lane < HEAD_DIM

    def norm_rope(p, g):
        ms = jnp.sum(jnp.where(is_k, p * p, 0.0), axis=-1, keepdims=True) * (1.0 / HEAD_DIM)
        y = jnp.where(is_k, p * lax.rsqrt(ms + EPS) * g, p)
        return _apply_rope(y, rope_ref)

    ys = norm_rope(ps_ref[...], gs_ref[...])
    yw = norm_rope(pw_ref[...], gw_ref[...])
    row = pl.program_id(1) * tk + lax.broadcasted_iota(jnp.int32, (tk, LANES), 0)
    ind = jnp.where(jnp.right_shift(row, SLC_SHIFT) == (lane - HEAD_DIM), 1.0, 0.0)
    kaug_ref[0] = jnp.where(is_k, ys, ind).astype(BF16)
    vs_ref[0] = ys.T[HEAD_DIM:, :].astype(BF16)
    kw_ref[0] = yw[:, :HEAD_DIM].astype(BF16)
    vw_ref[0] = yw.T[HEAD_DIM:, :].astype(BF16)


def nsa_kv_prep(cols, b_, l_, rope_pair, g_slc, g_win):
    tk = NSA_PREP_TK
    nt = l_ // tk
    ones = jnp.ones((HEAD_DIM,), F32)
    gs = jnp.concatenate([g_slc, ones]).reshape(1, LANES)
    gw = jnp.concatenate([g_win, ones]).reshape(1, LANES)
    kv = lambda w: jax.ShapeDtypeStruct((b_, l_, w), BF16)
    kv_t = jax.ShapeDtypeStruct((b_, HEAD_DIM, l_), BF16)
    out_blk = lambda w: pl.BlockSpec((1, tk, w), lambda b, i: (b, i, 0))
    out_t = pl.BlockSpec((1, HEAD_DIM, tk), lambda b, i: (b, 0, i))
    return pl.pallas_call(
        _nsa_kv_prep_kernel,
        out_shape=(kv(LANES), kv_t, kv(HEAD_DIM), kv_t),
        grid=(b_, nt),
        in_specs=[
            pl.BlockSpec((tk, LANES), lambda b, i: (b * nt + i, C_NSA_SLC // LANES)),
            pl.BlockSpec((tk, LANES), lambda b, i: (b * nt + i, C_NSA_WIN // LANES)),
            pl.BlockSpec((3, tk, LANES), lambda b, i: (0, i, 0)),
            pl.BlockSpec((1, LANES), lambda b, i: (0, 0)),
            pl.BlockSpec((1, LANES), lambda b, i: (0, 0)),
        ],
        out_specs=(out_blk(LANES), out_t, out_blk(HEAD_DIM), out_t),
        compiler_params=pltpu.CompilerParams(
            dimension_semantics=("arbitrary", "arbitrary"),
            vmem_limit_bytes=V7X_VMEM_LIMIT_BYTES),
        name="nsa_kv_prep",
    )(cols, cols, rope_pair, gs, gw)


def _nsa_compress_kernel(r_ref, pea_ref, peb_ref, w1a_ref, w1b_ref, w2k_ref, w2v_ref, kg_ref,
                         kc_ref, vc_ref):
    r = r_ref[0]
    n = r.shape[0]
    a = _dot((r + pea_ref[...]).astype(BF16), w1a_ref[...])
    bm = _dot((r + peb_ref[...]).astype(BF16), w1b_ref[...])
    row = lax.broadcasted_iota(jnp.int32, bm.shape, 0)
    bm_up = jnp.where(row < n - 1, pltpu.roll(bm, n - 1, 0), 0.0)
    h = jnp.maximum(a + bm_up, 0.0).astype(BF16)
    hid = w2k_ref.shape[0]
    ck = _dot(h[:, :hid], w2k_ref[...])
    cv = _dot(h[:, hid:], w2v_ref[...])
    kc_ref[0] = _rms_rows(ck, kg_ref[...]).astype(BF16)
    vc_ref[0] = cv.astype(BF16)


def nsa_compress(cols, b_, l_, pos_k, pos_v, k_w1, k_w2, v_w1, v_w2, k_gain):
    rows = l_ // CMP_STRIDE
    hid = k_w1.shape[1]
    pair = cols[:, C_NSA_CMP:C_NSA_CMP + LANES].reshape(b_, rows, CMP_STRIDE * LANES)

    def interleave_pe(lo):
        pe = jnp.concatenate([pos_k[lo:lo + CMP_STRIDE], pos_v[lo:lo + CMP_STRIDE]], axis=1)
        return pe.reshape(1, CMP_STRIDE * LANES)

    def interleave_w(lo):
        wk = k_w1[lo * HEAD_DIM:(lo + CMP_STRIDE) * HEAD_DIM].reshape(CMP_STRIDE, HEAD_DIM, hid)
        wv = v_w1[lo * HEAD_DIM:(lo + CMP_STRIDE) * HEAD_DIM].reshape(CMP_STRIDE, HEAD_DIM, hid)
        z = jnp.zeros_like(wk)
        top = jnp.concatenate([wk, z], axis=2)
        bot = jnp.concatenate([z, wv], axis=2)
        return jnp.concatenate([top, bot], axis=1).reshape(CMP_STRIDE * LANES, 2 * hid).astype(BF16)

    out = jax.ShapeDtypeStruct((b_, rows, HEAD_DIM), BF16)
    return pl.pallas_call(
        _nsa_compress_kernel,
        out_shape=(out, out),
        grid=(b_,),
        in_specs=[
            pl.BlockSpec((1, rows, CMP_STRIDE * LANES), lambda b: (b, 0, 0)),
            _const_spec((1, CMP_STRIDE * LANES)),
            _const_spec((1, CMP_STRIDE * LANES)),
            _const_spec((CMP_STRIDE * LANES, 2 * hid)),
            _const_spec((CMP_STRIDE * LANES, 2 * hid)),
            _const_spec((hid, HEAD_DIM)),
            _const_spec((hid, HEAD_DIM)),
            _const_spec((1, HEAD_DIM)),
        ],
        out_specs=(pl.BlockSpec((1, rows, HEAD_DIM), lambda b: (b, 0, 0)),
                   pl.BlockSpec((1, rows, HEAD_DIM), lambda b: (b, 0, 0))),
        compiler_params=pltpu.CompilerParams(
            dimension_semantics=("arbitrary",), vmem_limit_bytes=V7X_VMEM_LIMIT_BYTES),
        name="nsa_compress",
    )(pair, interleave_pe(0), interleave_pe(CMP_STRIDE), interleave_w(0),
      interleave_w(CMP_STRIDE), k_w2.astype(BF16), v_w2.astype(BF16),
      k_gain.reshape(1, HEAD_DIM))


def _masked_softmax(s, valid, axis):
    m = jnp.max(jnp.where(valid, s, -jnp.inf), axis=axis, keepdims=True)
    m = jnp.where(m == -jnp.inf, 0.0, m)
    e = jnp.where(valid, jnp.exp(s - m), 0.0)
    den = jnp.sum(e, axis=axis, keepdims=True)
    return e / jnp.where(den > 0, den, 1.0)


def _nsa_select_kernel(q_ref, kc_ref, vc_ref, ovt_ref, rope_ref, qg_ref, qaug_ref, ocmp_ref,
                       imp_ref, *, n_sel):
    tq = q_ref.shape[0]
    ncr = kc_ref.shape[1]
    n_slc = ovt_ref.shape[0]
    t0 = pl.program_id(1) * tq
    scale = HEAD_DIM ** -0.5
    q = q_ref[...]
    g = qg_ref[...]
    kc = kc_ref[0]
    vc = vc_ref[0]
    last = CMP_BLOCK - 1
    vis = (lax.broadcasted_iota(jnp.int32, (tq, ncr), 1) * CMP_STRIDE + last
           <= t0 + lax.broadcasted_iota(jnp.int32, (tq, ncr), 0))
    vis_t = (lax.broadcasted_iota(jnp.int32, (ncr, tq), 0) * CMP_STRIDE + last
             <= t0 + lax.broadcasted_iota(jnp.int32, (ncr, tq), 1))
    qn_heads, o_heads = [], []
    psum_t = jnp.zeros((ncr, tq), F32)
    for h in range(GROUP_HEADS):
        qn = _rms_rows(q[:, h * HEAD_DIM:(h + 1) * HEAD_DIM], g)
        qn_heads.append(qn)
        qb = (qn * scale).astype(BF16)
        p = _masked_softmax(_dot_nt(qb, kc), vis, 1)
        o_heads.append(_dot(p.astype(BF16), vc))
        psum_t = psum_t + _masked_softmax(_dot_nt(kc, qb), vis_t, 0)
    ocmp_ref[0] = jnp.concatenate(o_heads, axis=-1)

    hi = psum_t.astype(BF16)
    lo = (psum_t - hi.astype(F32)).astype(BF16)
    imp = _dot(ovt_ref[...], hi) + _dot(ovt_ref[...], lo)
    blk = lax.broadcasted_iota(jnp.int32, (n_slc, tq), 0)
    cur = jnp.right_shift(t0 + lax.broadcasted_iota(jnp.int32, (n_slc, tq), 1), SLC_SHIFT)
    forced = (blk == 0) | (blk == cur) | (blk == cur - 1)
    imp = jnp.where(forced, jnp.inf, jnp.where(blk > cur, -jnp.inf, imp))
    imp_ref[...] = imp

    def count_beats(m, cnt):
        row = imp_ref[pl.ds(m, 1), :]
        beats = (row > imp) | ((row == imp) & (blk > m))
        return cnt + jnp.where(beats, 1.0, 0.0)

    cnt = lax.fori_loop(0, n_slc, count_beats, jnp.zeros((n_slc, tq), F32))
    mt = jnp.where(cnt < n_sel, 0.0, NEG_BIG)
    pad = jnp.zeros((HEAD_DIM - n_slc, tq), F32)
    mt = jnp.concatenate([mt, pad, mt, pad], axis=0) if n_slc < HEAD_DIM else jnp.concatenate(
        [mt, mt], axis=0)
    mt = mt.T

    qr = _apply_rope(jnp.concatenate(qn_heads, axis=-1), rope_ref) * scale
    lane = lax.broadcasted_iota(jnp.int32, (tq, LANES), 1)
    for j in range(GROUP_HEADS // 2):
        pair = qr[:, j * LANES:(j + 1) * LANES]
        swapped = pltpu.roll(pair, HEAD_DIM, 1)
        qaug_ref[0, :, (2 * j) * LANES:(2 * j + 1) * LANES] = jnp.where(
            lane < HEAD_DIM, pair, mt).astype(BF16)
        qaug_ref[0, :, (2 * j + 1) * LANES:(2 * j + 2) * LANES] = jnp.where(
            lane < HEAD_DIM, swapped, mt).astype(BF16)


def nsa_select(cols, b_, l_, k_cmp, v_cmp, rope_q, q_gain):
    tq = min(NSA_SEL_TQ, l_)
    nt = l_ // tq
    ncr = l_ // CMP_STRIDE
    n_slc = l_ // SLC_BLOCK
    n_sel = min(SLC_TOPN, n_slc)
    st_c = np.arange(ncr) * CMP_STRIDE
    st_s = np.arange(n_slc) * SLC_BLOCK
    ovt = ((st_c[None, :] < st_s[:, None] + SLC_BLOCK)
           & (st_c[None, :] + CMP_BLOCK > st_s[:, None])).astype(np.float32)
    return pl.pallas_call(
        functools.partial(_nsa_select_kernel, n_sel=n_sel),
        out_shape=(jax.ShapeDtypeStruct((b_, l_, GROUP_HEADS * LANES), BF16),
                   jax.ShapeDtypeStruct((b_, l_, GROUP_WIDTH), F32)),
        grid=(b_, nt),
        in_specs=[
            pl.BlockSpec((tq, GROUP_WIDTH), lambda b, i: (b * nt + i, C_NSA_Q // GROUP_WIDTH)),
            pl.BlockSpec((1, ncr, HEAD_DIM), lambda b, i: (b, 0, 0)),
            pl.BlockSpec((1, ncr, HEAD_DIM), lambda b, i: (b, 0, 0)),
            pl.BlockSpec((n_slc, ncr), lambda b, i: (0, 0)),
            pl.BlockSpec((3, tq, GROUP_WIDTH), lambda b, i: (0, i, 0)),
            pl.BlockSpec((1, HEAD_DIM), lambda b, i: (0, 0)),
        ],
        out_specs=(pl.BlockSpec((1, tq, GROUP_HEADS * LANES), lambda b, i: (b, i, 0)),
                   pl.BlockSpec((1, tq, GROUP_WIDTH), lambda b, i: (b, i, 0))),
        scratch_shapes=[pltpu.VMEM((n_slc, tq), F32)],
        compiler_params=pltpu.CompilerParams(
            dimension_semantics=("arbitrary", "arbitrary"),
            vmem_limit_bytes=V7X_VMEM_LIMIT_BYTES),
        name="nsa_select",
    )(cols, k_cmp, v_cmp, jnp.asarray(ovt, BF16), rope_q, q_gain.reshape(1, HEAD_DIM))


def _nsa_attend_kernel(qaug_ref, kaug_ref, vs_ref, kw_ref, vw_ref, ocmp_ref, gate_ref, o_ref):
    tq = qaug_ref.shape[1]
    nh = GROUP_HEADS
    ts = NSA_KEY_TILE
    t0 = pl.program_id(1) * tq
    qa = qaug_ref[0]
    qs = jnp.concatenate([qa[:, h * LANES:(h + 1) * LANES] for h in range(nh)], axis=0)
    qpos = t0 + lax.rem(lax.broadcasted_iota(jnp.int32, (nh * tq, 1), 0), tq)

    def key_tile(kt, carry):
        m, l, acc = carry
        k0 = pl.multiple_of(kt * ts, ts)
        s = _dot_nt(qs, kaug_ref[0, pl.ds(k0, ts), :])
        kpos = k0 + lax.broadcasted_iota(jnp.int32, (nh * tq, ts), 1)
        s = jnp.where(kpos <= qpos, s, NEG_BIG)
        m_new = jnp.maximum(m, jnp.max(s, axis=-1, keepdims=True))
        alpha = jnp.exp(m - m_new)
        p = jnp.exp(s - m_new)
        l = alpha * l + jnp.sum(p, axis=-1, keepdims=True)
        acc = alpha * acc + _dot(p.astype(BF16), vs_ref[0, pl.ds(k0, ts), :])
        return m_new, l, acc

    n_kt = (t0 + tq + ts - 1) // ts
    init = (jnp.full((nh * tq, 1), -jnp.inf, F32), jnp.zeros((nh * tq, 1), F32),
            jnp.zeros((nh * tq, HEAD_DIM), F32))
    _, l, acc = lax.fori_loop(0, n_kt, key_tile, init)
    o_slc = acc / l

    wlen = WINDOW + tq
    start = pl.multiple_of(jnp.maximum(t0 - WINDOW, 0), tq)
    sw = _dot_nt(qs[:, :HEAD_DIM], kw_ref[0, pl.ds(start, wlen), :])
    dist = qpos - (start + lax.broadcasted_iota(jnp.int32, (nh * tq, wlen), 1))
    valid = (dist >= 0) & (dist < WINDOW)
    sw = jnp.where(valid, sw, -jnp.inf)
    e = jnp.exp(sw - jnp.max(sw, axis=-1, keepdims=True))
    o_swa = _dot(e.astype(BF16), vw_ref[0, pl.ds(start, wlen), :]) / jnp.sum(
        e, axis=-1, keepdims=True)

    g = jax.nn.sigmoid(gate_ref[...])
    oc = ocmp_ref[0]
    outs = []
    for h in range(nh):
        rows = slice(h * tq, (h + 1) * tq)
        outs.append(g[:, h:h + 1] * oc[:, h * HEAD_DIM:(h + 1) * HEAD_DIM]
                    + g[:, nh + h:nh + h + 1] * o_slc[rows]
                    + g[:, 2 * nh + h:2 * nh + h + 1] * o_swa[rows])
    o_ref[...] = jnp.concatenate(outs, axis=-1)


def _softmax_stats_update(s, m, l):
    m_new = jnp.maximum(m, jnp.max(s, axis=0, keepdims=True))
    alpha = jnp.exp(m - m_new)
    p = jnp.exp(s - m_new)
    return m_new, alpha, p, alpha * l + jnp.sum(p, axis=0, keepdims=True)


def _nsa_attend_t_kernel(qaug_ref, kaug_ref, vst_ref, kw_ref, vwt_ref, ocmp_ref, gate_ref, o_ref):
    tq = qaug_ref.shape[1]
    nh = GROUP_HEADS
    ts = NSA_KEY_TILE
    t0 = pl.program_id(1) * tq
    qa = qaug_ref[0]
    qs = jnp.concatenate([qa[:, h * LANES:(h + 1) * LANES] for h in range(nh)], axis=0)
    n = nh * tq
    qpos_tile = t0 + lax.broadcasted_iota(jnp.int32, (1, tq), 1)
    qpos = jnp.concatenate([qpos_tile] * nh, axis=1)

    def key_tile(kt, carry):
        m, l, acc = carry
        k0 = pl.multiple_of(kt * ts, ts)
        s = _dot_nt(kaug_ref[0, pl.ds(k0, ts), :], qs)
        kpos = k0 + lax.broadcasted_iota(jnp.int32, (ts, n), 0)
        s = jnp.where(kpos <= qpos, s, NEG_BIG)
        m_new, alpha, p, l = _softmax_stats_update(s, m, l)
        acc = alpha * acc + _dot(vst_ref[0, :, pl.ds(k0, ts)], p.astype(BF16))
        return m_new, l, acc

    n_kt = (t0 + tq + ts - 1) // ts
    init = (jnp.full((1, n), -jnp.inf, F32), jnp.zeros((1, n), F32),
            jnp.zeros((HEAD_DIM, n), F32))
    _, l, acc = lax.fori_loop(0, n_kt, key_tile, init)
    o_slc = acc / l

    wlen = WINDOW + tq
    start = pl.multiple_of(jnp.maximum(t0 - WINDOW, 0), tq)
    sw = _dot_nt(kw_ref[0, pl.ds(start, wlen), :], qs[:, :HEAD_DIM])
    dist = qpos - (start + lax.broadcasted_iota(jnp.int32, (wlen, n), 0))
    sw = jnp.where((dist >= 0) & (dist < WINDOW), sw, -jnp.inf)
    e = jnp.exp(sw - jnp.max(sw, axis=0, keepdims=True))
    o_swa = _dot(vwt_ref[0, :, pl.ds(start, wlen)], e.astype(BF16)) / jnp.sum(
        e, axis=0, keepdims=True)

    g = jax.nn.sigmoid(gate_ref[...]).T
    oc = ocmp_ref[0].T
    outs = []
    for h in range(nh):
        cols_h = slice(h * tq, (h + 1) * tq)
        outs.append(g[h:h + 1, :] * oc[h * HEAD_DIM:(h + 1) * HEAD_DIM, :]
                    + g[nh + h:nh + h + 1, :] * o_slc[:, cols_h]
                    + g[2 * nh + h:2 * nh + h + 1, :] * o_swa[:, cols_h])
    o_ref[...] = jnp.concatenate(outs, axis=0).T


def nsa_attend(cols, b_, l_, q_aug, k_aug, v_slc, k_win, v_win, o_cmp):
    tq = NSA_TQ
    nt = l_ // tq
    seq = lambda w: pl.BlockSpec((1, l_, w), lambda b, i: (b, 0, 0))
    seq_t = pl.BlockSpec((1, HEAD_DIM, l_), lambda b, i: (b, 0, 0))
    return pl.pallas_call(
        _nsa_attend_t_kernel,
        out_shape=jax.ShapeDtypeStruct((b_ * l_, GROUP_WIDTH), F32),
        grid=(b_, nt),
        in_specs=[
            pl.BlockSpec((1, tq, GROUP_HEADS * LANES), lambda b, i: (b, i, 0)),
            seq(LANES), seq_t, seq(HEAD_DIM), seq_t,
            pl.BlockSpec((1, tq, GROUP_WIDTH), lambda b, i: (b, i, 0)),
            pl.BlockSpec((tq, LANES), lambda b, i: (b * nt + i, C_NSA_GATE // LANES)),
        ],
        out_specs=pl.BlockSpec((tq, GROUP_WIDTH), lambda b, i: (b * nt + i, 0)),
        compiler_params=pltpu.CompilerParams(
            dimension_semantics=("arbitrary", "arbitrary"),
            vmem_limit_bytes=V7X_VMEM_LIMIT_BYTES),
        name="nsa_attend",
    )(q_aug, k_aug, v_slc, k_win, v_win, o_cmp, cols)


def nsa_mixer_pallas(cols, b_, l_, rope, pos_k, pos_v, k_w1, k_w2, v_w1, v_w2, q_gain, k_gains):
    k_aug, v_slc, k_win, v_win = nsa_kv_prep(cols, b_, l_, rope_tables_kv_pair(rope),
                                             k_gains[1], k_gains[2])
    k_cmp, v_cmp = nsa_compress(cols, b_, l_, pos_k, pos_v, k_w1, k_w2, v_w1, v_w2, k_gains[0])
    q_aug, o_cmp = nsa_select(cols, b_, l_, k_cmp, v_cmp, jnp.tile(rope, (1, 1, GROUP_HEADS)),
                              q_gain)
    return nsa_attend(cols, b_, l_, q_aug, k_aug, v_slc, k_win, v_win, o_cmp)


DSA_TQ = 128
DSA_KEY_TILE = 512
DSA_PREP_TK = 512
IDX_PACK = LANES
INT_MIN = -2 ** 31
MASKED_SCORE = -1e30


def _split_hi_lo(t):
    hi = t.astype(BF16)
    lo = (t - hi.astype(F32)).astype(BF16)
    return hi, lo


def _placement(rows, cols, pairs):
    p = np.zeros((rows, cols), np.float32)
    for r, c in pairs:
        p[r, c] = 1.0
    return jnp.asarray(p, BF16)


def _dsa_kv_prep_kernel(ckv_ref, ik_ref, rope_ref, ropei_ref, kvg_ref, wkv_ref, kg_ref, ikg_ref,
                        pkh_ref, pkl_ref, k_ref, vt_ref, ik3_ref):
    tk = ckv_ref.shape[0]
    lane = lax.broadcasted_iota(jnp.int32, (tk, LANES), 1)
    ckv = _rms_rows(ckv_ref[...], kvg_ref[...]).astype(BF16)
    kv = _dot(ckv, wkv_ref[...])
    is_k = lane < HEAD_DIM
    ms = jnp.sum(jnp.where(is_k, kv * kv, 0.0), axis=-1, keepdims=True) * (1.0 / HEAD_DIM)
    y = _apply_rope(jnp.where(is_k, kv * lax.rsqrt(ms + EPS) * kg_ref[...], kv), rope_ref)
    k_ref[0] = y[:, :HEAD_DIM].astype(BF16)
    vt_ref[0] = y.T[HEAD_DIM:, :].astype(BF16)
    ik = ik_ref[...]
    ms = jnp.sum(ik * ik, axis=-1, keepdims=True) * (1.0 / IDX_DIM)
    ikn = _apply_rope(ik * lax.rsqrt(ms + EPS) * ikg_ref[...], ropei_ref, IDX_DIM // 8)
    hi, lo = _split_hi_lo(ikn)
    ik3_ref[0] = (_dot(hi, pkh_ref[...]) + _dot(lo, pkl_ref[...])).astype(BF16)


def dsa_kv_prep(cols, b_, l_, rope_pair, rope_idx, kv_gain, w_uk, w_uv, k_gain, idxk_gain):
    tk = DSA_PREP_TK
    nt = l_ // tk
    ones = jnp.ones((HEAD_DIM,), F32)
    kg = jnp.concatenate([k_gain, ones]).reshape(1, LANES)
    ikg = jnp.pad(idxk_gain, (0, LANES - IDX_DIM)).reshape(1, LANES)
    wkv = jnp.concatenate([w_uk, w_uv], axis=1).astype(BF16)
    d = range(IDX_DIM)
    pkh = _placement(LANES, IDX_PACK, [(i, i) for i in d] + [(i, 2 * IDX_DIM + i) for i in d])
    pkl = _placement(LANES, IDX_PACK, [(i, IDX_DIM + i) for i in d])
    const = lambda shape: pl.BlockSpec(shape, lambda b, i: (0,) * len(shape))
    out = lambda w: pl.BlockSpec((1, tk, w), lambda b, i: (b, i, 0))
    return pl.pallas_call(
        _dsa_kv_prep_kernel,
        out_shape=(jax.ShapeDtypeStruct((b_, l_, HEAD_DIM), BF16),
                   jax.ShapeDtypeStruct((b_, HEAD_DIM, l_), BF16),
                   jax.ShapeDtypeStruct((b_, l_, IDX_PACK), BF16)),
        grid=(b_, nt),
        in_specs=[
            pl.BlockSpec((tk, LANES), lambda b, i: (b * nt + i, C_DSA_CKV // LANES)),
            pl.BlockSpec((tk, LANES), lambda b, i: (b * nt + i, C_DSA_IK // LANES)),
            pl.BlockSpec((3, tk, LANES), lambda b, i: (0, i, 0)),
            pl.BlockSpec((3, tk, LANES), lambda b, i: (0, i, 0)),
            const((1, LANES)), const((DSA_LATENT, LANES)), const((1, LANES)), const((1, LANES)),
            const((LANES, IDX_PACK)), const((LANES, IDX_PACK)),
        ],
        out_specs=(out(HEAD_DIM), pl.BlockSpec((1, HEAD_DIM, tk), lambda b, i: (b, 0, i)),
                   out(IDX_PACK)),
        compiler_params=pltpu.CompilerParams(
            dimension_semantics=("arbitrary", "arbitrary"),
            vmem_limit_bytes=V7X_VMEM_LIMIT_BYTES),
        name="dsa_kv_prep",
    )(cols, cols, rope_pair, rope_idx, kv_gain.reshape(1, DSA_LATENT), wkv, kg, ikg, pkh, pkl)


def _dsa_q_prep_kernel(q_ref, iq_ref, rope_ref, ropei_ref, qg_ref, pqh_ref, pql_ref,
                       qh_ref, iq3_ref):
    g = qg_ref[...]
    q = q_ref[...]
    qn = jnp.concatenate([_rms_rows(q[:, h * HEAD_DIM:(h + 1) * HEAD_DIM], g)
                          for h in range(GROUP_HEADS)], axis=-1)
    qr = _apply_rope(qn, rope_ref) * (HEAD_DIM ** -0.5)
    for h in range(GROUP_HEADS):
        qh_ref[0, h] = qr[:, h * HEAD_DIM:(h + 1) * HEAD_DIM].astype(BF16)
    hi, lo = _split_hi_lo(_apply_rope(iq_ref[...], ropei_ref, IDX_DIM // 8))
    iq3_ref[0] = (_dot(hi, pqh_ref[...]) + _dot(lo, pql_ref[...])).astype(BF16)


def dsa_q_prep(cols, b_, l_, rope_q, rope_iq, q_gain):
    tq = 256
    nt = l_ // tq
    w = IDX_HEADS * IDX_DIM
    hd = [(h, i) for h in range(IDX_HEADS) for i in range(IDX_DIM)]
    pqh = _placement(w, IDX_HEADS * IDX_PACK,
                     [(IDX_DIM * h + i, IDX_PACK * h + i) for h, i in hd]
                     + [(IDX_DIM * h + i, IDX_PACK * h + IDX_DIM + i) for h, i in hd])
    pql = _placement(w, IDX_HEADS * IDX_PACK,
                     [(IDX_DIM * h + i, IDX_PACK * h + 2 * IDX_DIM + i) for h, i in hd])
    const = lambda shape: pl.BlockSpec(shape, lambda b, i: (0,) * len(shape))
    return pl.pallas_call(
        _dsa_q_prep_kernel,
        out_shape=(jax.ShapeDtypeStruct((b_, GROUP_HEADS, l_, HEAD_DIM), BF16),
                   jax.ShapeDtypeStruct((b_, l_, IDX_HEADS * IDX_PACK), BF16)),
        grid=(b_, nt),
        in_specs=[
            pl.BlockSpec((tq, GROUP_WIDTH), lambda b, i: (b * nt + i, C_DSA_Q // GROUP_WIDTH)),
            pl.BlockSpec((tq, w), lambda b, i: (b * nt + i, C_DSA_IQ // w)),
            pl.BlockSpec((3, tq, GROUP_WIDTH), lambda b, i: (0, i, 0)),
            pl.BlockSpec((3, tq, w), lambda b, i: (0, i, 0)),
            const((1, HEAD_DIM)), const(pqh.shape), const(pql.shape),
        ],
        out_specs=(pl.BlockSpec((1, GROUP_HEADS, tq, HEAD_DIM), lambda b, i: (b, 0, i, 0)),
                   pl.BlockSpec((1, tq, IDX_HEADS * IDX_PACK), lambda b, i: (b, i, 0))),
        compiler_params=pltpu.CompilerParams(
            dimension_semantics=("arbitrary", "arbitrary"),
            vmem_limit_bytes=V7X_VMEM_LIMIT_BYTES),
        name="dsa_q_prep",
    )(cols, cols, rope_q, rope_iq, q_gain.reshape(1, HEAD_DIM), pqh, pql)


def _dsa_attend_kernel(qh_ref, iq3_ref, iw_ref, k_ref, v_ref, ik3_ref, o_ref, sc_ref, *,
                       topk, idx_bits):
    tq = iq3_ref.shape[1]
    ts = DSA_KEY_TILE
    nh = GROUP_HEADS
    t0 = pl.program_id(1) * tq
    n_kt = (t0 + tq + ts - 1) // ts
    qpos = t0 + lax.broadcasted_iota(jnp.int32, (tq, 1), 0)

    iq3 = iq3_ref[0]
    lhs = jnp.concatenate([iq3[:, h * IDX_PACK:(h + 1) * IDX_PACK] for h in range(IDX_HEADS)],
                          axis=0)
    iw = iw_ref[...] * (IDX_HEADS ** -0.5 * IDX_DIM ** -0.5)
    w_col = jnp.concatenate([iw[:, h:h + 1] for h in range(IDX_HEADS)], axis=0)

    def score_tile(kt, _):
        k0 = pl.multiple_of(kt * ts, ts)
        rel = jnp.maximum(_dot_nt(lhs, ik3_ref[0, pl.ds(k0, ts), :]), 0.0) * w_col
        sc = rel[0:tq]
        for h in range(1, IDX_HEADS):
            sc = sc + rel[h * tq:(h + 1) * tq]
        kpos = k0 + lax.broadcasted_iota(jnp.int32, (tq, ts), 1)
        sc_ref[:, pl.ds(k0, ts)] = jnp.where(kpos <= qpos, sc, -jnp.inf)
        return 0

    lax.fori_loop(0, n_kt, score_tile, 0)

    def count(pred):
        def tile(kt, c):
            for j in range(ts // LANES):
                k0 = pl.multiple_of(kt * ts + j * LANES, LANES)
                c = c + jnp.where(pred(sc_ref[:, pl.ds(k0, LANES)], k0), 1.0, 0.0)
            return c
        c = lax.fori_loop(0, n_kt, tile, jnp.zeros((tq, LANES), F32))
        return jnp.broadcast_to(jnp.sum(c, axis=-1, keepdims=True), (tq, LANES))

    def key_to_float(key):
        return pltpu.bitcast(jnp.where(key >= 0, key, key ^ jnp.int32(0x7FFFFFFF)), F32)

    def value_bit(i, thr_key):
        cand = thr_key | jnp.left_shift(jnp.int32(1), 31 - i)
        cand_f = key_to_float(cand ^ jnp.int32(INT_MIN))
        return jnp.where(count(lambda sc, k0: sc >= cand_f) >= topk, cand, thr_key)

    thr_key = lax.fori_loop(0, 32, value_bit, jnp.zeros((tq, LANES), jnp.int32))
    thr = key_to_float(thr_key ^ jnp.int32(INT_MIN))
    n_ge = count(lambda sc, k0: sc >= thr)
    thr = jnp.where(n_ge >= topk, thr, -jnp.inf)
    need = topk - count(lambda sc, k0: sc > thr)
    tie_break = jnp.max(jnp.where((n_ge > topk) & (thr > -jnp.inf), 1.0, 0.0)) > 0.0

    def last_tie_position():
        def index_bit(i, last):
            cand = last | jnp.left_shift(jnp.int32(1), idx_bits - 1 - i)

            def tied_below(sc, k0):
                kpos = k0 + lax.broadcasted_iota(jnp.int32, (tq, LANES), 1)
                return (sc == thr) & (kpos < cand)

            return jnp.where(count(tied_below) < need, cand, last)

        return lax.fori_loop(0, idx_bits, index_bit, jnp.zeros((tq, LANES), jnp.int32))

    last = lax.cond(tie_break, last_tie_position,
                    lambda: jnp.full((tq, LANES), 2 ** idx_bits, jnp.int32))
    widen = lambda t: jnp.concatenate([t] * (ts // LANES), axis=1)
    thr2 = widen(thr)
    last2 = widen(last)

    qrow = t0 + lax.broadcasted_iota(jnp.int32, (tq, ts), 0)

    def lane_fold(t, op):
        out = t[:, :LANES]
        for j in range(1, ts // LANES):
            out = op(out, t[:, j * LANES:(j + 1) * LANES])
        return out

    def max_tile(kt, ms):
        k0 = pl.multiple_of(kt * ts, ts)
        sc = sc_ref[:, pl.ds(k0, ts)]
        kpos = k0 + lax.broadcasted_iota(jnp.int32, (tq, ts), 1)
        sel = ((sc > thr2) | ((sc == thr2) & (kpos <= last2))) & (kpos <= qrow)
        bias = jnp.where(sel, 0.0, MASKED_SCORE)
        sc_ref[:, pl.ds(k0, ts)] = bias
        k_t = k_ref[0, pl.ds(k0, ts), :]
        return tuple(jnp.maximum(ms[h], lane_fold(_dot_nt(qh_ref[0, h], k_t) + bias, jnp.maximum))
                     for h in range(nh))

    ms = lax.fori_loop(0, n_kt, max_tile,
                       tuple(jnp.full((tq, LANES), MASKED_SCORE, F32) for _ in range(nh)))
    ms = tuple(jnp.broadcast_to(jnp.max(m, axis=-1, keepdims=True), (tq, LANES)) for m in ms)

    def sum_tile(kt, carry):
        k0 = pl.multiple_of(kt * ts, ts)
        bias = sc_ref[:, pl.ds(k0, ts)]
        k_t = k_ref[0, pl.ds(k0, ts), :]
        v_t = v_ref[0, pl.ds(k0, ts), :]
        new = []
        for h in range(nh):
            l, acc = carry[h]
            p = jnp.exp(_dot_nt(qh_ref[0, h], k_t) + bias - widen(ms[h]))
            new.append((l + lane_fold(p, jnp.add), acc + _dot(p.astype(BF16), v_t)))
        return tuple(new)

    init = tuple((jnp.zeros((tq, LANES), F32), jnp.zeros((tq, HEAD_DIM), F32)) for _ in range(nh))
    heads = lax.fori_loop(0, n_kt, sum_tile, init)
    o_ref[...] = jnp.concatenate([acc / jnp.sum(l, axis=-1, keepdims=True) for l, acc in heads],
                                 axis=-1)


SUBLANES = 8


def _fold_rows(t, op, group=SUBLANES):
    parts = [t[i * group:(i + 1) * group] for i in range(t.shape[0] // group)]
    while len(parts) > 1:
        parts = [op(parts[i], parts[i + 1]) if i + 1 < len(parts) else parts[i]
                 for i in range(0, len(parts), 2)]
    return parts[0]


def _dsa_attend_t_kernel(qh_ref, iq3_ref, iw_ref, k_ref, vt_ref, ik3_ref, o_ref, sc_ref, *,
                         topk, idx_bits):
    tq = iq3_ref.shape[1]
    ts = DSA_KEY_TILE
    nh = GROUP_HEADS
    t0 = pl.program_id(1) * tq
    n_kt = (t0 + tq + ts - 1) // ts
    qpos = t0 + lax.broadcasted_iota(jnp.int32, (ts, tq), 1)
    krow = lax.broadcasted_iota(jnp.int32, (ts, tq), 0)

    iq3 = iq3_ref[0]
    iq_rows = jnp.concatenate([iq3[:, h * IDX_PACK:(h + 1) * IDX_PACK]
                               for h in range(IDX_HEADS)], axis=0)
    iw_t = (iw_ref[...] * (IDX_HEADS ** -0.5 * IDX_DIM ** -0.5)).T

    def score_tile(kt, _):
        k0 = pl.multiple_of(kt * ts, ts)
        rel = jnp.maximum(_dot_nt(ik3_ref[0, pl.ds(k0, ts), :], iq_rows), 0.0)
        sc = rel[:, 0:tq] * iw_t[0:1, :]
        for h in range(1, IDX_HEADS):
            sc = sc + rel[:, h * tq:(h + 1) * tq] * iw_t[h:h + 1, :]
        sc_ref[pl.ds(k0, ts), :] = jnp.where(k0 + krow <= qpos, sc, -jnp.inf)
        return 0

    lax.fori_loop(0, n_kt, score_tile, 0)

    def count(pred):
        def tile(kt, c):
            k0 = pl.multiple_of(kt * ts, ts)
            hit = jnp.where(pred(sc_ref[pl.ds(k0, ts), :], k0), 1.0, 0.0)
            return c + _fold_rows(hit, jnp.add)
        c = lax.fori_loop(0, n_kt, tile, jnp.zeros((SUBLANES, tq), F32))
        return jnp.sum(c, axis=0, keepdims=True)

    def key_to_float(key):
        return pltpu.bitcast(jnp.where(key >= 0, key, key ^ jnp.int32(0x7FFFFFFF)), F32)

    def value_bit(i, thr_key):
        cand = thr_key | jnp.left_shift(jnp.int32(1), 31 - i)
        cand_f = key_to_float(cand ^ jnp.int32(INT_MIN))
        return jnp.where(count(lambda sc, k0: sc >= cand_f) >= topk, cand, thr_key)

    thr_key = lax.fori_loop(0, 32, value_bit, jnp.zeros((1, tq), jnp.int32))
    few = t0 + lax.broadcasted_iota(jnp.int32, (1, tq), 1) + 1 < topk
    thr = jnp.where(few, -jnp.inf, key_to_float(thr_key ^ jnp.int32(INT_MIN)))
    n_ge = count(lambda sc, k0: sc >= thr)
    tie_break = jnp.max(jnp.where((n_ge > topk) & jnp.logical_not(few), 1.0, 0.0)) > 0.0

    def last_tie_position():
        need = topk - count(lambda sc, k0: sc > thr)

        def index_bit(i, last):
            cand = last | jnp.left_shift(jnp.int32(1), idx_bits - 1 - i)
            tied_below = lambda sc, k0: (sc == thr) & (k0 + krow < cand)
            return jnp.where(count(tied_below) < need, cand, last)

        return lax.fori_loop(0, idx_bits, index_bit, jnp.zeros((1, tq), jnp.int32))

    last = lax.cond(tie_break, last_tie_position,
                    lambda: jnp.full((1, tq), 2 ** idx_bits, jnp.int32))

    qs = qh_ref[0].reshape(nh * tq, HEAD_DIM)

    def key_tile(kt, carry):
        m, l, acc = carry
        k0 = pl.multiple_of(kt * ts, ts)
        sc = sc_ref[pl.ds(k0, ts), :]
        kpos = k0 + krow
        sel = ((sc > thr) | ((sc == thr) & (kpos <= last))) & (kpos <= qpos)
        bias = jnp.where(sel, 0.0, MASKED_SCORE)
        s = _dot_nt(k_ref[0, pl.ds(k0, ts), :], qs) + jnp.concatenate([bias] * nh, axis=1)
        m_new = jnp.maximum(m, jnp.max(s, axis=0, keepdims=True))
        alpha = jnp.exp(m - m_new)
        p = jnp.exp(s - m_new)
        l = alpha * l + jnp.sum(p, axis=0, keepdims=True)
        acc = alpha * acc + _dot(vt_ref[0, :, pl.ds(k0, ts)], p.astype(BF16))
        return m_new, l, acc

    init = (jnp.full((1, nh * tq), MASKED_SCORE, F32), jnp.zeros((1, nh * tq), F32),
            jnp.zeros((HEAD_DIM, nh * tq), F32))
    _, l, acc = lax.fori_loop(0, n_kt, key_tile, init)
    o_t = acc / l
    o_ref[...] = jnp.concatenate([o_t[:, h * tq:(h + 1) * tq] for h in range(nh)], axis=0).T


def dsa_attend(cols, b_, l_, qh, iq3, k, v, ik3):
    tq = DSA_TQ
    nt = l_ // tq
    topk = min(DSA_TOPK_MAX, l_ // 4)
    idx_bits = int(np.log2(l_))
    assert 2 ** idx_bits == l_ and l_ % DSA_KEY_TILE == 0 and topk <= DSA_KEY_TILE
    seq = lambda w: pl.BlockSpec((1, l_, w), lambda b, i: (b, 0, 0))
    return pl.pallas_call(
        functools.partial(_dsa_attend_t_kernel, topk=topk, idx_bits=idx_bits),
        out_shape=jax.ShapeDtypeStruct((b_ * l_, GROUP_WIDTH), F32),
        grid=(b_, nt),
        in_specs=[
            pl.BlockSpec((1, GROUP_HEADS, tq, HEAD_DIM), lambda b, i: (b, 0, i, 0)),
            pl.BlockSpec((1, tq, IDX_HEADS * IDX_PACK), lambda b, i: (b, i, 0)),
            pl.BlockSpec((tq, LANES), lambda b, i: (b * nt + i, C_DSA_IW // LANES)),
            seq(HEAD_DIM), pl.BlockSpec((1, HEAD_DIM, l_), lambda b, i: (b, 0, 0)), seq(IDX_PACK),
        ],
        out_specs=pl.BlockSpec((tq, GROUP_WIDTH), lambda b, i: (b * nt + i, 0)),
        scratch_shapes=[pltpu.VMEM((l_, tq), F32)],
        compiler_params=pltpu.CompilerParams(
            dimension_semantics=("arbitrary", "arbitrary"),
            vmem_limit_bytes=V7X_VMEM_LIMIT_BYTES),
        name="dsa_attend",
    )(qh, iq3, cols, k, v, ik3)


def dsa_mixer_pallas(cols, b_, l_, rope, kv_gain, w_uk, w_uv, q_gain, k_gain, idxk_gain):
    rope_i = rope_tables(l_, IDX_DIM)
    k, v, ik3 = dsa_kv_prep(cols, b_, l_, rope_tables_kv_pair(rope), rope_tables_pad(rope_i, LANES),
                            kv_gain, w_uk, w_uv, k_gain, idxk_gain)
    qh, iq3 = dsa_q_prep(cols, b_, l_, jnp.tile(rope, (1, 1, GROUP_HEADS)),
                         jnp.tile(rope_i, (1, 1, IDX_HEADS)), q_gain)
    return dsa_attend(cols, b_, l_, qh, iq3, k, v, ik3)


_TN = (((0,), (0,)), ((), ()))


def _dot_tn(a, b):
    return lax.dot_general(a, b, _TN, preferred_element_type=F32)


def _split3(t):
    hi = t.astype(BF16)
    r = t - hi.astype(F32)
    mid = r.astype(BF16)
    lo = (r - mid.astype(F32)).astype(BF16)
    return hi, mid, lo


def _tri_cumsum(tri, t):
    hi, mid, lo = _split3(t)
    return _dot(tri, hi) + _dot(tri, mid) + _dot(tri, lo)


def _cumsum_tri_rows(t, tri_u):
    hi, mid, lo = _split3(t)
    return _dot(hi, tri_u) + _dot(mid, tri_u) + _dot(lo, tri_u)


def _head_rms(o, gain):
    return jnp.concatenate([_rms_rows(o[:, h * HEAD_DIM:(h + 1) * HEAD_DIM], gain)
                            for h in range(GROUP_HEADS)], axis=-1)


HG_SUB = 8


def _hgrn2_kernel(q_ref, f_ref, i_ref, g_ref, lb_ref, gain_ref, tri_ref, ones_ref, bd_ref,
                  o_ref, st_ref, b_ref, kk_ref, v_ref):
    c = HG_CHUNK
    w = GROUP_WIDTH

    @pl.when(pl.program_id(1) == 0)
    def _():
        st_ref[...] = jnp.zeros_like(st_ref)

    lb = lb_ref[...]
    q = q_ref[...]
    qs = q * jax.nn.sigmoid(q) * (HEAD_DIM ** -0.5)
    forget = lb + (1.0 - lb) * jax.nn.sigmoid(f_ref[...])
    kk = 1.0 - forget
    bcum = _tri_cumsum(tri_ref[...], jnp.log(forget))
    v = i_ref[...]
    b_ref[...] = bcum
    kk_ref[...] = kk
    v_ref[...] = v

    out = _dot_nt((qs * jnp.exp(bcum)).astype(BF16), st_ref[...].astype(BF16))

    ones_bd = ones_ref[...]
    pieces = []
    for g in range(c // HG_SUB):
        r0 = g * HG_SUB
        nr = c - r0
        qg = qs[r0:, :]
        bg = bcum[r0:, :]
        trow = r0 + lax.broadcasted_iota(jnp.int32, (nr, w), 0)
        terms = []
        for j in range(HG_SUB):
            s = r0 + j
            d = qg * kk_ref[s:s + 1, :] * jnp.exp(bg - b_ref[s:s + 1, :])
            terms.append(jnp.where(trow >= s, d, 0.0).astype(BF16))
        red = _dot(jnp.concatenate(terms, axis=0), ones_bd)
        acc = red[0:nr] * v_ref[r0:r0 + 1, :]
        for j in range(1, HG_SUB):
            acc = acc + red[j * nr:(j + 1) * nr] * v_ref[r0 + j:r0 + j + 1, :]
        pieces.append(acc)
    intra = pieces[0]
    for g in range(1, c // HG_SUB):
        pad = jnp.zeros((g * HG_SUB, w), F32)
        intra = intra + jnp.concatenate([pad, pieces[g]], axis=0)
    out = out + intra

    b_last = bcum[c - 1:c, :]
    kt = (kk * jnp.exp(b_last - bcum)).astype(BF16)
    st_ref[...] = jnp.exp(b_last) * st_ref[...] + _dot_tn(v.astype(BF16), kt) * bd_ref[...]

    g_in = g_ref[...]
    o_ref[...] = _head_rms(out, gain_ref[...]) * (g_in * jax.nn.sigmoid(g_in))


def hgrn2_mixer_pallas(cols, b_, l_, lb, o_gain):
    c = HG_CHUNK
    nt = l_ // c
    w = GROUP_WIDTH
    head = np.arange(w) // HEAD_DIM
    same = (head[:, None] == head[None, :]).astype(np.float32)
    tri = np.tril(np.ones((c, c), np.float32))
    col = lambda j: pl.BlockSpec((c, w), lambda b, i: (b * nt + i, C_HG // w + j))
    const = lambda shape: pl.BlockSpec(shape, lambda b, i: (0,) * len(shape))
    return pl.pallas_call(
        _hgrn2_kernel,
        out_shape=jax.ShapeDtypeStruct((b_ * l_, w), F32),
        grid=(b_, nt),
        in_specs=[col(0), col(1), col(2), col(3), const((1, w)), const((1, HEAD_DIM)),
                  const((c, c)), const((w, w)), const((w, w))],
        out_specs=pl.BlockSpec((c, w), lambda b, i: (b * nt + i, 0)),
        scratch_shapes=[pltpu.VMEM((w, w), F32), pltpu.VMEM((c, w), F32),
                        pltpu.VMEM((c, w), F32), pltpu.VMEM((c, w), F32)],
        compiler_params=pltpu.CompilerParams(
            dimension_semantics=("arbitrary", "arbitrary"),
            vmem_limit_bytes=V7X_VMEM_LIMIT_BYTES),
        name="hgrn2",
    )(cols, cols, cols, cols, lb.reshape(1, w), o_gain.reshape(1, HEAD_DIM),
      jnp.asarray(tri, BF16), jnp.asarray(same, BF16), jnp.asarray(same, F32))


ML_TC = 256
ML_M_INIT = -1e30


def _mlstm_kernel(gate_ref, qk_ref, v_ref, og_ref, cw_ref, cb_ref, gb_ref, gain_ref, tril_ref,
                  triu_ref, o_ref, xprev_ref, cmat_ref, nvec_ref, m_ref):
    c = ML_TC
    nh = GROUP_HEADS
    w = GROUP_WIDTH

    @pl.when(pl.program_id(1) == 0)
    def _():
        xprev_ref[...] = jnp.zeros_like(xprev_ref)
        cmat_ref[...] = jnp.zeros_like(cmat_ref)
        nvec_ref[...] = jnp.zeros_like(nvec_ref)
        m_ref[...] = jnp.full(m_ref.shape, ML_M_INIT, F32)

    x = qk_ref[...]
    prev = xprev_ref[...]
    row = lax.broadcasted_iota(jnp.int32, x.shape, 0)
    acc = x * cw_ref[CONV_WIDTH - 1:CONV_WIDTH, :] + cb_ref[...]
    for j in range(1, CONV_WIDTH):
        shifted = jnp.where(row < j, pltpu.roll(prev, j, 0), pltpu.roll(x, j, 0))
        acc = acc + shifted * cw_ref[CONV_WIDTH - 1 - j:CONV_WIDTH - j, :]
    xprev_ref[...] = x
    qk = acc * jax.nn.sigmoid(acc)
    q = qk[:, :w]
    k = qk[:, w:] * (HEAD_DIM ** -0.5)
    v = v_ref[...]

    pre = gate_ref[...] + gb_ref[...]
    lane = lax.broadcasted_iota(jnp.int32, pre.shape, 1)
    log_f = jnp.minimum(pre, 0.0) - jnp.log1p(jnp.exp(-jnp.abs(pre)))
    log_f = jnp.where((lane >= nh) & (lane < 2 * nh), log_f, 0.0)
    bcum_c = _tri_cumsum(tril_ref[...], log_f)
    bcum_r = _cumsum_tri_rows(log_f.T, triu_ref[...])
    pre_r = pre.T
    tri = (lax.broadcasted_iota(jnp.int32, (c, c), 0) >= lax.broadcasted_iota(jnp.int32, (c, c), 1))

    outs = []
    for h in range(nh):
        sl = slice(h * HEAD_DIM, (h + 1) * HEAD_DIM)
        qh, kh, vh = q[:, sl], k[:, sl], v[:, sl]
        bc = bcum_c[:, nh + h:nh + h + 1]
        li_c = pre[:, h:h + 1]
        a_r = pre_r[h:h + 1, :] - bcum_r[nh + h:nh + h + 1, :]
        m_prev = m_ref[h:h + 1, 0:1]
        log_d = jnp.where(tri, bc + a_r, -jnp.inf)
        inter = bc + m_prev
        m_t = jnp.maximum(inter, jnp.max(log_d, axis=-1, keepdims=True))
        d_mat = jnp.exp(log_d - m_t)
        w_inter = jnp.exp(inter - m_t)
        qb = qh.astype(BF16)
        s = _dot_nt(qb, kh.astype(BF16)) * d_mat
        num = w_inter * _dot(qb, cmat_ref[h].astype(BF16)) + _dot(s.astype(BF16), vh.astype(BF16))
        den = (w_inter * jnp.sum(qh * nvec_ref[h], axis=-1, keepdims=True)
               + jnp.sum(s, axis=-1, keepdims=True))
        outs.append(num / jnp.maximum(jnp.abs(den), jnp.exp(-m_t)))
        b_last = bc[c - 1:c, :]
        log_w = b_last + (li_c - bc)
        m_new = jnp.maximum(b_last + m_prev, jnp.max(log_w, axis=0, keepdims=True))
        kw = kh * jnp.exp(log_w - m_new)
        decay = jnp.exp(b_last + m_prev - m_new)
        cmat_ref[h] = decay * cmat_ref[h] + _dot_tn(kw.astype(BF16), vh.astype(BF16))
        nvec_ref[h] = decay * nvec_ref[h] + jnp.sum(kw, axis=0, keepdims=True)
        m_ref[h:h + 1, :] = jnp.broadcast_to(m_new, (1, LANES))

    hh = _head_rms(jnp.concatenate(outs, axis=-1), gain_ref[...])
    o_ref[...] = hh * jax.nn.sigmoid(og_ref[...])


def mlstm_mixer_pallas(cols, b_, l_, conv_w, conv_b, i_bias, f_bias, o_gain):
    c = ML_TC
    nt = l_ // c
    w = GROUP_WIDTH
    gb = jnp.pad(jnp.concatenate([i_bias, f_bias]), (0, LANES - 2 * GROUP_HEADS)).reshape(1, LANES)
    tril = np.tril(np.ones((c, c), np.float32))
    const = lambda shape: pl.BlockSpec(shape, lambda b, i: (0,) * len(shape))
    blk = lambda width, off: pl.BlockSpec((c, width), lambda b, i: (b * nt + i, off // width))
    return pl.pallas_call(
        _mlstm_kernel,
        out_shape=jax.ShapeDtypeStruct((b_ * l_, w), F32),
        grid=(b_, nt),
        in_specs=[blk(LANES, C_ML_GATE), blk(2 * w, C_ML_QK), blk(w, C_ML_V), blk(w, C_ML_OG),
                  const((CONV_WIDTH, 2 * w)), const((1, 2 * w)), const((1, LANES)),
                  const((1, HEAD_DIM)), const((c, c)), const((c, c))],
        out_specs=pl.BlockSpec((c, w), lambda b, i: (b * nt + i, 0)),
        scratch_shapes=[pltpu.VMEM((c, 2 * w), F32),
                        pltpu.VMEM((GROUP_HEADS, HEAD_DIM, HEAD_DIM), F32),
                        pltpu.VMEM((GROUP_HEADS, 1, HEAD_DIM), F32),
                        pltpu.VMEM((8, LANES), F32)],
        compiler_params=pltpu.CompilerParams(
            dimension_semantics=("arbitrary", "arbitrary"),
            vmem_limit_bytes=V7X_VMEM_LIMIT_BYTES),
        name="mlstm",
    )(cols, cols, cols, cols, conv_w, conv_b.reshape(1, 2 * w), gb, o_gain.reshape(1, HEAD_DIM),
      jnp.asarray(tril, BF16), jnp.asarray(tril.T, BF16))


def split_cols(t, sizes):
    return jnp.split(t, [int(s) for s in np.cumsum(sizes)[:-1]], axis=-1)


def split_heads(t, n):
    return t.reshape(t.shape[:-1] + (n, t.shape[-1] // n))


def rms_norm(t, g):
    t32 = t.astype(F32)
    y = t32 * lax.rsqrt(jnp.mean(t32 * t32, axis=-1, keepdims=True) + EPS)
    return (y * g.astype(F32)).astype(t.dtype)


def partial_rope(t, pos):
    d = t.shape[-1]
    rd = d // 4
    half = rd // 2
    inv = ROPE_THETA ** (-jnp.arange(half, dtype=F32) * 2.0 / rd)
    ang = pos.astype(F32)[:, None] * inv[None, :]
    cos = jnp.cos(ang)[:, None, :].astype(t.dtype)
    sin = jnp.sin(ang)[:, None, :].astype(t.dtype)
    x1, x2 = t[..., :half], t[..., half:rd]
    return jnp.concatenate([x1 * cos - x2 * sin, x2 * cos + x1 * sin, t[..., rd:]], axis=-1)


def masked_softmax(s, mask):
    s = jnp.where(mask, s.astype(F32), -jnp.inf)
    m = jnp.max(s, axis=-1, keepdims=True)
    m = jnp.where(jnp.isfinite(m), m, 0.0)
    e = jnp.exp(s - m)
    den = jnp.sum(e, axis=-1, keepdims=True)
    return e / jnp.where(den > 0, den, 1.0)


def gather_rows(table, idx):
    return jax.vmap(lambda tb, ix: tb[ix])(table, idx)


def causal_conv(t, w, b):
    y = lax.conv_general_dilated(t, w[:, None, :].astype(t.dtype), window_strides=(1,),
                                 padding=[(CONV_WIDTH - 1, 0)],
                                 dimension_numbers=('NWC', 'WIO', 'NWC'),
                                 feature_group_count=t.shape[-1])
    return y + b.astype(t.dtype)


def to_chunks(t, c):
    b_, l_, h_ = t.shape[:3]
    t = t.reshape((b_, l_ // c, c, h_) + t.shape[3:])
    return jnp.moveaxis(t, (1, 3), (0, 2))


def from_chunks(t):
    t = jnp.moveaxis(t, (0, 2), (1, 3))
    return t.reshape((t.shape[0], t.shape[1] * t.shape[2], t.shape[3]) + t.shape[4:])


def unblock(o):
    o = jnp.moveaxis(o, 0, 1)
    return o.reshape((o.shape[0], o.shape[1] * o.shape[2]) + o.shape[3:])


def hgrn2_mixer(cols, lb, o_gain):
    b_, l_ = cols.shape[:2]
    q, f, i, g = split_cols(cols, HG_SPLITS)
    q = split_heads(jax.nn.silu(q), GROUP_HEADS).astype(F32) * HEAD_DIM ** -0.5
    forget = lb + (1.0 - lb) * jax.nn.sigmoid(f.astype(F32))
    k = split_heads(1.0 - forget, GROUP_HEADS)
    logf = split_heads(jnp.log(forget), GROUP_HEADS)
    v = split_heads(i, GROUP_HEADS).astype(F32)
    tri = jnp.tril(jnp.ones((HG_CHUNK, HG_CHUNK), bool))[:, :, None]

    def step(state, inp):
        qc, kc, vc, gc = inp
        bcum = jnp.cumsum(gc, axis=2)
        o_inter = jnp.einsum('bhtk,bhkv->bhtv', qc * jnp.exp(bcum), state)
        diff = bcum[:, :, :, None, :] - bcum[:, :, None, :, :]
        decay = jnp.exp(jnp.where(tri, diff, -jnp.inf))
        attn = jnp.einsum('bhtk,bhsk,bhtsk->bhts', qc, kc, decay)
        out = o_inter + jnp.einsum('bhts,bhsv->bhtv', attn, vc)
        b_last = bcum[:, :, -1:, :]
        state = (jnp.exp(b_last[:, :, 0, :])[..., None] * state
                 + jnp.einsum('bhsk,bhsv->bhkv', kc * jnp.exp(b_last - bcum), vc))
        return state, out

    s0 = jnp.zeros((b_, GROUP_HEADS, HEAD_DIM, HEAD_DIM), F32)
    xs = (to_chunks(q, HG_CHUNK), to_chunks(k, HG_CHUNK), to_chunks(v, HG_CHUNK),
          to_chunks(logf, HG_CHUNK))
    _, o = lax.scan(step, s0, xs)
    o = rms_norm(from_chunks(o), o_gain).astype(cols.dtype)
    o = o * jax.nn.silu(split_heads(g, GROUP_HEADS))
    return o.reshape(b_, l_, GROUP_WIDTH)


def dsa_mixer(cols, pos, kv_gain, w_uk, w_uv, q_gain, k_gain, idxk_gain):
    b_, l_ = cols.shape[:2]
    q, ckv, iq, ik, iw = split_cols(cols, DSA_SPLITS)
    q = partial_rope(rms_norm(split_heads(q, GROUP_HEADS), q_gain), pos)
    ckv = rms_norm(ckv, kv_gain)
    k = partial_rope(rms_norm(ckv @ w_uk, k_gain)[:, :, None, :], pos)[:, :, 0]
    v = ckv @ w_uv
    iq = partial_rope(split_heads(iq, IDX_HEADS), pos)
    ik = partial_rope(rms_norm(ik, idxk_gain)[:, :, None, :], pos)[:, :, 0]
    iw = iw * (IDX_HEADS ** -0.5 * IDX_DIM ** -0.5)
    topk = min(DSA_TOPK_MAX, l_ // 4)
    key_pos = jnp.arange(l_)

    def block(bi):
        t0 = bi * Q_BLOCK
        qpos = t0 + jnp.arange(Q_BLOCK)
        qb = lax.dynamic_slice_in_dim(q, t0, Q_BLOCK, axis=1)
        iqb = lax.dynamic_slice_in_dim(iq, t0, Q_BLOCK, axis=1)
        iwb = lax.dynamic_slice_in_dim(iw, t0, Q_BLOCK, axis=1)
        rel = jax.nn.relu(jnp.einsum('bthd,bsd->bhts', iqb, ik))
        score = jnp.einsum('bhts,bth->bts', rel, iwb).astype(F32)
        causal = key_pos[None, :] <= qpos[:, None]
        score = jnp.where(causal[None], score, -jnp.inf)
        _, idx = lax.top_k(score, topk)
        k_sel = gather_rows(k, idx)
        v_sel = gather_rows(v, idx)
        s = jnp.einsum('bthd,btkd->bhtk', qb, k_sel) * HEAD_DIM ** -0.5
        p = masked_softmax(s, (idx <= qpos[None, :, None])[:, None]).astype(v.dtype)
        return jnp.einsum('bhtk,btkd->bthd', p, v_sel)

    o = unblock(lax.map(block, jnp.arange(l_ // Q_BLOCK)))
    return o.reshape(b_, l_, GROUP_WIDTH)


def nsa_mixer(cols, pos, pos_k, pos_v, k_w1, k_w2, v_w1, v_w2, q_gain, k_gains):
    b_, l_ = cols.shape[:2]
    scale = HEAD_DIM ** -0.5
    q, kc, vc, ks, vs, kw, vw, gates = split_cols(cols, NSA_SPLITS)
    q = rms_norm(split_heads(q, GROUP_HEADS), q_gain)
    q_rot = partial_rope(q, pos)
    gates = jax.nn.sigmoid(gates.reshape(b_, l_, 3, GROUP_HEADS, 1))

    n_cmp = (l_ - CMP_BLOCK) // CMP_STRIDE + 1
    cmp_idx = np.arange(n_cmp)[:, None] * CMP_STRIDE + np.arange(CMP_BLOCK)[None, :]

    def compress(t, pe, w1, w2):
        blk = (t[:, cmp_idx] + pe).reshape(b_, n_cmp, CMP_BLOCK * HEAD_DIM)
        return jax.nn.relu(blk @ w1) @ w2

    k_cmp = rms_norm(compress(kc, pos_k, k_w1, k_w2), k_gains[0])
    v_cmp = compress(vc, pos_v, v_w1, v_w2)
    cmp_vis = cmp_idx[:, -1][None, :] <= np.arange(l_)[:, None]
    p_cmp = masked_softmax(jnp.einsum('bthd,bjd->bhtj', q, k_cmp) * scale, cmp_vis)
    o_cmp = jnp.einsum('bhtj,bjd->bthd', p_cmp.astype(v_cmp.dtype), v_cmp)

    n_slc = l_ // SLC_BLOCK
    n_sel = min(SLC_TOPN, n_slc)
    st_c = np.arange(n_cmp) * CMP_STRIDE
    st_s = np.arange(n_slc) * SLC_BLOCK
    overlap = ((st_c[:, None] < st_s[None, :] + SLC_BLOCK)
               & (st_c[:, None] + CMP_BLOCK > st_s[None, :])).astype(np.float32)
    imp = jnp.einsum('bhtj,jn->btn', p_cmp, overlap)
    cur = np.arange(l_)[:, None] // SLC_BLOCK
    blk_id = np.arange(n_slc)[None, :]
    forced = (blk_id == 0) | (blk_id == cur) | (blk_id == cur - 1)
    imp = jnp.where(forced, jnp.inf, jnp.where(blk_id > cur, -jnp.inf, imp))
    _, sel = lax.top_k(imp, n_sel)

    k_s = partial_rope(rms_norm(ks, k_gains[1])[:, :, None], pos)[:, :, 0]
    k_blocks = k_s.reshape(b_, n_slc, SLC_BLOCK, HEAD_DIM)
    v_blocks = vs.reshape(b_, n_slc, SLC_BLOCK, HEAD_DIM)
    k_w = partial_rope(rms_norm(kw, k_gains[2])[:, :, None], pos)[:, :, 0]
    k_pad = jnp.pad(k_w, ((0, 0), (WINDOW, 0), (0, 0)))
    v_pad = jnp.pad(vw, ((0, 0), (WINDOW, 0), (0, 0)))

    def block(bi):
        t0 = bi * Q_BLOCK
        qpos = t0 + jnp.arange(Q_BLOCK)
        qb = lax.dynamic_slice_in_dim(q_rot, t0, Q_BLOCK, axis=1)
        sb = lax.dynamic_slice_in_dim(sel, t0, Q_BLOCK, axis=1)
        kb = gather_rows(k_blocks, sb)
        vb = gather_rows(v_blocks, sb)
        kpos = sb[..., None] * SLC_BLOCK + jnp.arange(SLC_BLOCK)
        s = jnp.einsum('bthd,btnsd->bhtns', qb, kb) * scale
        valid = (kpos <= qpos[None, :, None, None])[:, None]
        p = masked_softmax(s.reshape(b_, GROUP_HEADS, Q_BLOCK, n_sel * SLC_BLOCK),
                           valid.reshape(b_, 1, Q_BLOCK, n_sel * SLC_BLOCK))
        p = p.reshape(b_, GROUP_HEADS, Q_BLOCK, n_sel, SLC_BLOCK).astype(vb.dtype)
        o_slc = jnp.einsum('bhtns,btnsd->bthd', p, vb)
        kwb = lax.dynamic_slice_in_dim(k_pad, t0, WINDOW + Q_BLOCK, axis=1)
        vwb = lax.dynamic_slice_in_dim(v_pad, t0, WINDOW + Q_BLOCK, axis=1)
        wpos = t0 - WINDOW + jnp.arange(WINDOW + Q_BLOCK)
        dist = qpos[:, None] - wpos[None, :]
        wvalid = (dist >= 0) & (dist < WINDOW) & (wpos[None, :] >= 0)
        sw = jnp.einsum('bthd,bsd->bhts', qb, kwb) * scale
        pw = masked_softmax(sw, wvalid[None, None]).astype(vwb.dtype)
        o_swa = jnp.einsum('bhts,bsd->bthd', pw, vwb)
        return o_slc, o_swa

    o_slc, o_swa = lax.map(block, jnp.arange(l_ // Q_BLOCK))
    o = gates[:, :, 0] * o_cmp + gates[:, :, 1] * unblock(o_slc) + gates[:, :, 2] * unblock(o_swa)
    return o.reshape(b_, l_, GROUP_WIDTH)


def mlstm_mixer(cols, conv_w, conv_b, i_bias, f_bias, o_gain):
    b_, l_ = cols.shape[:2]
    qk, v, og, ig, fg = split_cols(cols, ML_SPLITS)
    qk = jax.nn.silu(causal_conv(qk, conv_w, conv_b))
    q, k = jnp.split(qk, 2, axis=-1)
    q = split_heads(q, GROUP_HEADS).astype(F32)
    k = split_heads(k, GROUP_HEADS).astype(F32) * HEAD_DIM ** -0.5
    v = split_heads(v, GROUP_HEADS).astype(F32)
    log_i = (ig + i_bias).astype(F32)
    log_f = jax.nn.log_sigmoid((fg + f_bias).astype(F32))
    tri = jnp.tril(jnp.ones((ML_CHUNK, ML_CHUNK), bool))

    def step(carry, inp):
        cmat, nvec, m = carry
        qc, kc, vc, li, lf = inp
        bcum = jnp.cumsum(lf, axis=-1)
        log_d = jnp.where(tri, bcum[..., :, None] - bcum[..., None, :] + li[..., None, :], -jnp.inf)
        inter = bcum + m[..., None]
        m_t = jnp.maximum(inter, jnp.max(log_d, axis=-1))
        d_mat = jnp.exp(log_d - m_t[..., None])
        w_inter = jnp.exp(inter - m_t)
        s = jnp.einsum('bhtd,bhsd->bhts', qc, kc) * d_mat
        num = (w_inter[..., None] * jnp.einsum('bhtd,bhdv->bhtv', qc, cmat)
               + jnp.einsum('bhts,bhsv->bhtv', s, vc))
        den = w_inter * jnp.einsum('bhtd,bhd->bht', qc, nvec) + jnp.sum(s, axis=-1)
        h = num / jnp.maximum(jnp.abs(den), jnp.exp(-m_t))[..., None]
        b_last = bcum[..., -1]
        log_w = b_last[..., None] - bcum + li
        m_new = jnp.maximum(b_last + m, jnp.max(log_w, axis=-1))
        w_s = jnp.exp(log_w - m_new[..., None])
        decay = jnp.exp(b_last + m - m_new)
        cmat = decay[..., None, None] * cmat + jnp.einsum('bhs,bhsd,bhsv->bhdv', w_s, kc, vc)
        nvec = decay[..., None] * nvec + jnp.einsum('bhs,bhsd->bhd', w_s, kc)
        return (cmat, nvec, m_new), h

    init = (jnp.zeros((b_, GROUP_HEADS, HEAD_DIM, HEAD_DIM), F32),
            jnp.zeros((b_, GROUP_HEADS, HEAD_DIM), F32),
            jnp.full((b_, GROUP_HEADS), -1e30, F32))
    xs = (to_chunks(q, ML_CHUNK), to_chunks(k, ML_CHUNK), to_chunks(v, ML_CHUNK),
          to_chunks(log_i, ML_CHUNK), to_chunks(log_f, ML_CHUNK))
    _, h = lax.scan(step, init, xs)
    h = rms_norm(from_chunks(h), o_gain).astype(cols.dtype)
    h = h * jax.nn.sigmoid(split_heads(og, GROUP_HEADS))
    return h.reshape(b_, l_, GROUP_WIDTH)


def kernel(x, mem, lb_param, norm_mix, w_in, w_out, hg_o_gain, dsa_kv_gain, dsa_w_uk, dsa_w_uv,
           dsa_q_gain, dsa_k_gain, dsa_idxk_gain, nsa_pos_k, nsa_pos_v, nsa_k_w1, nsa_k_w2,
           nsa_v_w1, nsa_v_w2, nsa_q_gain, nsa_k_gains, ml_conv_w, ml_conv_b, ml_i_bias,
           ml_f_bias, ml_o_gain, norm_xa, norm_mem, xa_wq, xa_wkv, xa_wo, xa_q_gain, xa_k_gain,
           norm_ffn, ffn_w13, ffn_w2):
    b_, l_, d = x.shape
    pos = jnp.arange(l_)
    lb_all = jnp.cumsum(jax.nn.softmax(lb_param.astype(F32), axis=0), axis=0)
    lb_all = lb_all - lb_all[:1]
    x2d = x.reshape(b_ * l_, d)
    mem2d = mem.reshape(b_ * N_MEM, d)
    rope = rope_tables(l_)
    for l in range(DEPTH):
        cols = norm_matmul(x2d, norm_mix[l], pack_w_in_bf16(w_in[l]), tm=256)
        mixed = jnp.concatenate([
            hgrn2_mixer_pallas(cols, b_, l_, lb_all[l], hg_o_gain[l]),
            dsa_mixer_pallas(cols, b_, l_, rope, dsa_kv_gain[l], dsa_w_uk[l], dsa_w_uv[l],
                             dsa_q_gain[l], dsa_k_gain[l], dsa_idxk_gain[l]),
            nsa_mixer_pallas(cols, b_, l_, rope, nsa_pos_k[l], nsa_pos_v[l], nsa_k_w1[l],
                             nsa_k_w2[l], nsa_v_w1[l], nsa_v_w2[l], nsa_q_gain[l],
                             nsa_k_gains[l]),
            mlstm_mixer_pallas(cols, b_, l_, ml_conv_w[l], ml_conv_b[l], ml_i_bias[l],
                               ml_f_bias[l], ml_o_gain[l]),
        ], axis=-1)
        k_mem, v_mem = mem_kv(mem2d, norm_mem[l], xa_wkv[l].astype(BF16), xa_k_gain[l])
        x2d = post_block(x2d, mixed.reshape(b_ * l_, -1), l_, w_out[l].astype(BF16), norm_xa[l],
                         xa_wq[l].astype(BF16), xa_q_gain[l], k_mem, v_mem,
                         xa_wo[l].astype(BF16), norm_ffn[l], ffn_w13[l].astype(BF16),
                         ffn_w2[l].astype(BF16), tm=512)
    return x2d.reshape(b_, l_, d)
```

```python
import functools

import jax
import jax.numpy as jnp
from jax import lax
import numpy as np
from jax.experimental import pallas as pl
from jax.experimental.pallas import tpu as pltpu

F32 = jnp.float32
BF16 = jnp.bfloat16

D_MODEL = 1024
DEPTH = 2
HEAD_DIM = 64
GROUP_HEADS = 4
GROUP_WIDTH = GROUP_HEADS * HEAD_DIM
ROPE_THETA = 500000.0
EPS = 1e-6
N_MEM = 256
XA_HEADS = 4
XA_WIDTH = XA_HEADS * HEAD_DIM
HG_CHUNK = 64
DSA_LATENT = 128
IDX_HEADS = 8
IDX_DIM = 32
DSA_TOPK_MAX = 256
CMP_BLOCK = 32
CMP_STRIDE = 16
SLC_BLOCK = 64
SLC_SHIFT = 6
SLC_TOPN = 16
WINDOW = 512
CONV_WIDTH = 4
D_FF = 2816

HG_SPLITS = (GROUP_WIDTH,) * 4
DSA_SPLITS = (GROUP_WIDTH, DSA_LATENT, IDX_HEADS * IDX_DIM, IDX_DIM, IDX_HEADS)
NSA_SPLITS = (GROUP_WIDTH,) + (HEAD_DIM,) * 6 + (3 * GROUP_HEADS,)
ML_SPLITS = (2 * GROUP_WIDTH, GROUP_WIDTH, GROUP_WIDTH, GROUP_HEADS, GROUP_HEADS)
GROUP_COLS = (sum(HG_SPLITS), sum(DSA_SPLITS), sum(NSA_SPLITS), sum(ML_SPLITS))
IN_COLS = sum(GROUP_COLS)

V7X_VMEM_LIMIT_BYTES = 56 * 1024 * 1024
LANES = 128
FF_CHUNK = 256


def _round_up(n, m):
    return -(-n // m) * m


def _rms_rows(t, g):
    return t * lax.rsqrt(jnp.mean(t * t, axis=-1, keepdims=True) + EPS) * g


def _const_spec(shape):
    return pl.BlockSpec(shape, lambda *_: (0,) * len(shape), pipeline_mode=pl.Buffered(1))


def _norm_matmul_kernel(x_ref, g_ref, w_ref, o_ref):
    h = _rms_rows(x_ref[...], g_ref[...]).astype(BF16)
    o_ref[...] = jnp.dot(h, w_ref[...], preferred_element_type=F32)


def norm_matmul(x2d, gain, w_bf16, tm):
    m, k = x2d.shape
    n = w_bf16.shape[1]
    return pl.pallas_call(
        _norm_matmul_kernel,
        out_shape=jax.ShapeDtypeStruct((m, n), F32),
        grid=(m // tm,),
        in_specs=[
            pl.BlockSpec((tm, k), lambda i: (i, 0)),
            _const_spec((1, k)),
            _const_spec((k, n)),
        ],
        out_specs=pl.BlockSpec((tm, n), lambda i: (i, 0)),
        compiler_params=pltpu.CompilerParams(
            dimension_semantics=("arbitrary",), vmem_limit_bytes=V7X_VMEM_LIMIT_BYTES),
        name="norm_matmul",
    )(x2d, gain.reshape(1, k), w_bf16)


def _mem_kv_kernel(m_ref, g_ref, w_ref, kg_ref, k_ref, v_ref):
    mn = _rms_rows(m_ref[...], g_ref[...]).astype(BF16)
    kv = jnp.dot(mn, w_ref[...], preferred_element_type=F32)
    kg = kg_ref[...]
    ks = []
    for h in range(XA_HEADS):
        kh = kv[:, h * HEAD_DIM:(h + 1) * HEAD_DIM]
        ks.append(_rms_rows(kh, kg))
    k_ref[...] = jnp.concatenate(ks, axis=-1).astype(BF16)
    v_ref[...] = kv[:, XA_WIDTH:].astype(BF16)


def mem_kv(mem2d, gain, wkv_bf16, k_gain):
    m, k = mem2d.shape
    return pl.pallas_call(
        _mem_kv_kernel,
        out_shape=(jax.ShapeDtypeStruct((m, XA_WIDTH), BF16),
                   jax.ShapeDtypeStruct((m, XA_WIDTH), BF16)),
        grid=(m // N_MEM,),
        in_specs=[
            pl.BlockSpec((N_MEM, k), lambda i: (i, 0)),
            _const_spec((1, k)),
            _const_spec((k, 2 * XA_WIDTH)),
            _const_spec((1, HEAD_DIM)),
        ],
        out_specs=(pl.BlockSpec((N_MEM, XA_WIDTH), lambda i: (i, 0)),
                   pl.BlockSpec((N_MEM, XA_WIDTH), lambda i: (i, 0))),
        compiler_params=pltpu.CompilerParams(
            dimension_semantics=("arbitrary",), vmem_limit_bytes=V7X_VMEM_LIMIT_BYTES),
        name="mem_kv",
    )(mem2d, gain.reshape(1, k), wkv_bf16, k_gain.reshape(1, HEAD_DIM))


def _post_kernel(x_ref, mix_ref, wout_ref, gxa_ref, wq_ref, qg_ref, k_ref, v_ref, wo_ref,
                 gffn_ref, w13_ref, w2_ref, o_ref):
    x = x_ref[...] + _dot(mix_ref[...].astype(BF16), wout_ref[...])
    h = _rms_rows(x, gxa_ref[...]).astype(BF16)
    q = _dot(h, wq_ref[...])
    qg = qg_ref[...] * (HEAD_DIM ** -0.5)
    k = k_ref[...]
    v = v_ref[...]
    heads = [slice(hd * HEAD_DIM, (hd + 1) * HEAD_DIM) for hd in range(XA_HEADS)]
    scores = [_dot_nt(_rms_rows(q[:, sl], qg).astype(BF16), k[:, sl]) for sl in heads]
    probs = []
    for s in scores:
        e = jnp.exp(s - jnp.max(s, axis=-1, keepdims=True))
        probs.append((e / jnp.sum(e, axis=-1, keepdims=True)).astype(BF16))
    o = jnp.concatenate([_dot(p, v[:, sl]) for p, sl in zip(probs, heads)], axis=-1)
    x = x + _dot(o.astype(BF16), wo_ref[...])
    h = _rms_rows(x, gffn_ref[...]).astype(BF16)

    def up(c):
        return (_dot(h, w13_ref[:, c * FF_CHUNK:(c + 1) * FF_CHUNK]),
                _dot(h, w13_ref[:, D_FF + c * FF_CHUNK:D_FF + (c + 1) * FF_CHUNK]))

    n_chunks = D_FF // FF_CHUNK
    acc = x
    a, b = up(0)
    for c in range(n_chunks):
        nxt = up(c + 1) if c + 1 < n_chunks else None
        act = (a * jax.nn.sigmoid(a) * b).astype(BF16)
        acc = acc + _dot(act, w2_ref[c * FF_CHUNK:(c + 1) * FF_CHUNK, :])
        if nxt is not None:
            a, b = nxt
    o_ref[...] = acc


def post_block(x2d, mixed2d, seq, wout, gxa, wq, qg, k_mem, v_mem, wo, gffn, w13, w2, tm):
    m, d = x2d.shape
    tiles_per_batch = seq // tm
    row = lambda i: (i, 0)
    mem_row = lambda i: (i // tiles_per_batch, 0)
    return pl.pallas_call(
        _post_kernel,
        out_shape=jax.ShapeDtypeStruct((m, d), F32),
        grid=(m // tm,),
        in_specs=[
            pl.BlockSpec((tm, d), row),
            pl.BlockSpec((tm, mixed2d.shape[1]), row),
            _const_spec(wout.shape),
            _const_spec((1, d)),
            _const_spec(wq.shape),
            _const_spec((1, HEAD_DIM)),
            pl.BlockSpec((N_MEM, XA_WIDTH), mem_row),
            pl.BlockSpec((N_MEM, XA_WIDTH), mem_row),
            _const_spec(wo.shape),
            _const_spec((1, d)),
            _const_spec(w13.shape),
            _const_spec(w2.shape),
        ],
        out_specs=pl.BlockSpec((tm, d), row),
        compiler_params=pltpu.CompilerParams(
            dimension_semantics=("arbitrary",), vmem_limit_bytes=V7X_VMEM_LIMIT_BYTES),
        name="post_block",
    )(x2d, mixed2d, wout, gxa.reshape(1, d), wq, qg.reshape(1, HEAD_DIM), k_mem, v_mem, wo,
      gffn.reshape(1, d), w13, w2)


C_HG = 0
C_DSA_Q = 1024
C_DSA_IQ = 1280
C_NSA_Q = 1536
C_ML_V = 1792
C_ML_QK = 2048
C_ML_OG = 2560
C_DSA_CKV = 2816
C_DSA_IK = 2944
C_DSA_IW = 3072
C_NSA_GATE = 3200
C_NSA_CMP = 3328
C_NSA_SLC = 3456
C_NSA_WIN = 3584
C_ML_GATE = 3712
IN_COLS_PACKED = 3840


def _packed_source_columns():
    dsa0 = GROUP_COLS[0]
    nsa0 = dsa0 + GROUP_COLS[1]
    ml0 = nsa0 + GROUP_COLS[2]
    segs = [(0, 1024, 1024), (dsa0, 256, 256), (dsa0 + 384, 256, 256), (nsa0, 256, 256),
            (ml0 + 512, 256, 256), (ml0, 512, 512), (ml0 + 768, 256, 256), (dsa0 + 256, 128, 128),
            (dsa0 + 640, 32, LANES), (dsa0 + 672, 8, LANES), (nsa0 + 640, 12, LANES),
            (nsa0 + 256, 128, 128), (nsa0 + 384, 128, 128), (nsa0 + 512, 128, 128),
            (ml0 + 1024, 8, LANES)]
    src = np.concatenate([np.concatenate([np.arange(a, a + n), np.full(width - n, -1)])
                          for a, n, width in segs]).astype(np.int32)
    assert src.shape == (IN_COLS_PACKED,)
    return src


PACK_TN = 256


def _pack_w_kernel(w_ref, src_ref, o_ref, wb_ref):
    @pl.when(pl.program_id(0) == 0)
    def _():
        wb_ref[...] = w_ref[...].astype(BF16)

    row = lax.broadcasted_iota(jnp.int32, (wb_ref.shape[1], PACK_TN), 0)
    sel = jnp.where(row == src_ref[...], 1.0, 0.0).astype(BF16)
    o_ref[...] = _dot(wb_ref[...], sel).astype(BF16)


def pack_w_in_bf16(w):
    d, n = w.shape
    n_pad = _round_up(n, LANES)
    w_pad = jnp.pad(w, ((0, 0), (0, n_pad - n)))
    src = jnp.asarray(_packed_source_columns()).reshape(1, IN_COLS_PACKED)
    return pl.pallas_call(
        _pack_w_kernel,
        out_shape=jax.ShapeDtypeStruct((d, IN_COLS_PACKED), BF16),
        grid=(IN_COLS_PACKED // PACK_TN,),
        in_specs=[_const_spec((d, n_pad)), pl.BlockSpec((1, PACK_TN), lambda j: (0, j))],
        out_specs=pl.BlockSpec((d, PACK_TN), lambda j: (0, j)),
        scratch_shapes=[pltpu.VMEM((d, n_pad), BF16)],
        compiler_params=pltpu.CompilerParams(
            dimension_semantics=("arbitrary",), vmem_limit_bytes=V7X_VMEM_LIMIT_BYTES),
        name="pack_w_in",
    )(w_pad, src)


def rope_tables(l_, d=HEAD_DIM):
    rd = d // 4
    half = rd // 2
    inv = ROPE_THETA ** (-jnp.arange(half, dtype=F32) * 2.0 / rd)
    ang = jnp.arange(l_).astype(F32)[:, None] * inv[None, :]
    cos, sin = jnp.cos(ang), jnp.sin(ang)
    zh = jnp.zeros((l_, half), F32)
    rest0 = jnp.zeros((l_, d - rd), F32)
    c = jnp.concatenate([cos, cos, rest0 + 1.0], axis=1)
    s1 = jnp.concatenate([-sin, zh, rest0], axis=1)
    s2 = jnp.concatenate([zh, sin, rest0], axis=1)
    return jnp.stack([c, s1, s2])


def rope_tables_pad(rt, width):
    n = width - rt.shape[-1]
    ident = jnp.stack([jnp.ones(rt.shape[1:2] + (n,), F32), jnp.zeros(rt.shape[1:2] + (n,), F32),
                       jnp.zeros(rt.shape[1:2] + (n,), F32)])
    return jnp.concatenate([rt, ident], axis=-1)


def rope_tables_kv_pair(rt):
    return rope_tables_pad(rt, 2 * rt.shape[-1])


def _apply_rope(t, rope_ref, half=HEAD_DIM // 8):
    w = t.shape[-1]
    return (t * rope_ref[0] + pltpu.roll(t, w - half, 1) * rope_ref[1]
            + pltpu.roll(t, half, 1) * rope_ref[2])


_NT = (((1,), (1,)), ((), ()))


def _dot_nt(a, b):
    return lax.dot_general(a, b, _NT, preferred_element_type=F32)


def _dot(a, b):
    return jnp.dot(a, b, preferred_element_type=F32)


NEG_BIG = -(2.0 ** 30)
NSA_KEY_TILE = 512
NSA_TQ = 128
NSA_SEL_TQ = 256
NSA_PREP_TK = 512


def _nsa_kv_prep_kernel(ps_ref, pw_ref, rope_ref, gs_ref, gw_ref, kaug_ref, vs_ref, kw_ref, vw_ref):
    tk = ps_ref.shape[0]
    lane = lax.broadcasted_iota(jnp.int32, (tk, LANES), 1)
    is_k = lane < HEAD_DIM

    def norm_rope(p, g):
        ms = jnp.sum(jnp.where(is_k, p * p, 0.0), axis=-1, keepdims=True) * (1.0 / HEAD_DIM)
        y = jnp.where(is_k, p * lax.rsqrt(ms + EPS) * g, p)
        return _apply_rope(y, rope_ref)

    ys = norm_rope(ps_ref[...], gs_ref[...])
    yw = norm_rope(pw_ref[...], gw_ref[...])
    row = pl.program_id(1) * tk + lax.broadcasted_iota(jnp.int32, (tk, LANES), 0)
    ind = jnp.where(jnp.right_shift(row, SLC_SHIFT) == (lane - HEAD_DIM), 1.0, 0.0)
    kaug_ref[0] = jnp.where(is_k, ys, ind).astype(BF16)
    vs_ref[0] = ys.T[HEAD_DIM:, :].astype(BF16)
    kw_ref[0] = yw[:, :HEAD_DIM].astype(BF16)
    vw_ref[0] = yw.T[HEAD_DIM:, :].astype(BF16)


def nsa_kv_prep(cols, b_, l_, rope_pair, g_slc, g_win):
    tk = NSA_PREP_TK
    nt = l_ // tk
    ones = jnp.ones((HEAD_DIM,), F32)
    gs = jnp.concatenate([g_slc, ones]).reshape(1, LANES)
    gw = jnp.concatenate([g_win, ones]).reshape(1, LANES)
    kv = lambda w: jax.ShapeDtypeStruct((b_, l_, w), BF16)
    kv_t = jax.ShapeDtypeStruct((b_, HEAD_DIM, l_), BF16)
    out_blk = lambda w: pl.BlockSpec((1, tk, w), lambda b, i: (b, i, 0))
    out_t = pl.BlockSpec((1, HEAD_DIM, tk), lambda b, i: (b, 0, i))
    return pl.pallas_call(
        _nsa_kv_prep_kernel,
        out_shape=(kv(LANES), kv_t, kv(HEAD_DIM), kv_t),
        grid=(b_, nt),
        in_specs=[
            pl.BlockSpec((tk, LANES), lambda b, i: (b * nt + i, C_NSA_SLC // LANES)),
            pl.BlockSpec((tk, LANES), lambda b, i: (b * nt + i, C_NSA_WIN // LANES)),
            pl.BlockSpec((3, tk, LANES), lambda b, i: (0, i, 0)),
            pl.BlockSpec((1, LANES), lambda b, i: (0, 0)),
            pl.BlockSpec((1, LANES), lambda b, i: (0, 0)),
        ],
        out_specs=(out_blk(LANES), out_t, out_blk(HEAD_DIM), out_t),
        compiler_params=pltpu.CompilerParams(
            dimension_semantics=("arbitrary", "arbitrary"),
            vmem_limit_bytes=V7X_VMEM_LIMIT_BYTES),
        name="nsa_kv_prep",
    )(cols, cols, rope_pair, gs, gw)


def _nsa_compress_kernel(r_ref, pea_ref, peb_ref, w1a_ref, w1b_ref, w2k_ref, w2v_ref, kg_ref,
                         kc_ref, vc_ref):
    r = r_ref[0]
    n = r.shape[0]
    a = _dot((r + pea_ref[...]).astype(BF16), w1a_ref[...])
    bm = _dot((r + peb_ref[...]).astype(BF16), w1b_ref[...])
    row = lax.broadcasted_iota(jnp.int32, bm.shape, 0)
    bm_up = jnp.where(row < n - 1, pltpu.roll(bm, n - 1, 0), 0.0)
    h = jnp.maximum(a + bm_up, 0.0).astype(BF16)
    hid = w2k_ref.shape[0]
    ck = _dot(h[:, :hid], w2k_ref[...])
    cv = _dot(h[:, hid:], w2v_ref[...])
    kc_ref[0] = _rms_rows(ck, kg_ref[...]).astype(BF16)
    vc_ref[0] = cv.astype(BF16)


def nsa_compress(cols, b_, l_, pos_k, pos_v, k_w1, k_w2, v_w1, v_w2, k_gain):
    rows = l_ // CMP_STRIDE
    hid = k_w1.shape[1]
    pair = cols[:, C_NSA_CMP:C_NSA_CMP + LANES].reshape(b_, rows, CMP_STRIDE * LANES)

    def interleave_pe(lo):
        pe = jnp.concatenate([pos_k[lo:lo + CMP_STRIDE], pos_v[lo:lo + CMP_STRIDE]], axis=1)
        return pe.reshape(1, CMP_STRIDE * LANES)

    def interleave_w(lo):
        wk = k_w1[lo * HEAD_DIM:(lo + CMP_STRIDE) * HEAD_DIM].reshape(CMP_STRIDE, HEAD_DIM, hid)
        wv = v_w1[lo * HEAD_DIM:(lo + CMP_STRIDE) * HEAD_DIM].reshape(CMP_STRIDE, HEAD_DIM, hid)
        z = jnp.zeros_like(wk)
        top = jnp.concatenate([wk, z], axis=2)
        bot = jnp.concatenate([z, wv], axis=2)
        return jnp.concatenate([top, bot], axis=1).reshape(CMP_STRIDE * LANES, 2 * hid).astype(BF16)

    out = jax.ShapeDtypeStruct((b_, rows, HEAD_DIM), BF16)
    return pl.pallas_call(
        _nsa_compress_kernel,
        out_shape=(out, out),
        grid=(b_,),
        in_specs=[
            pl.BlockSpec((1, rows, CMP_STRIDE * LANES), lambda b: (b, 0, 0)),
            _const_spec((1, CMP_STRIDE * LANES)),
            _const_spec((1, CMP_STRIDE * LANES)),
            _const_spec((CMP_STRIDE * LANES, 2 * hid)),
            _const_spec((CMP_STRIDE * LANES, 2 * hid)),
            _const_spec((hid, HEAD_DIM)),
            _const_spec((hid, HEAD_DIM)),
            _const_spec((1, HEAD_DIM)),
        ],
        out_specs=(pl.BlockSpec((1, rows, HEAD_DIM), lambda b: (b, 0, 0)),
                   pl.BlockSpec((1, rows, HEAD_DIM), lambda b: (b, 0, 0))),
        compiler_params=pltpu.CompilerParams(
            dimension_semantics=("arbitrary",), vmem_limit_bytes=V7X_VMEM_LIMIT_BYTES),
        name="nsa_compress",
    )(pair, interleave_pe(0), interleave_pe(CMP_STRIDE), interleave_w(0),
      interleave_w(CMP_STRIDE), k_w2.astype(BF16), v_w2.astype(BF16),
      k_gain.reshape(1, HEAD_DIM))


def _masked_softmax(s, valid, axis):
    m = jnp.max(jnp.where(valid, s, -jnp.inf), axis=axis, keepdims=True)
    m = jnp.where(m == -jnp.inf, 0.0, m)
    e = jnp.where(valid, jnp.exp(s - m), 0.0)
    den = jnp.sum(e, axis=axis, keepdims=True)
    return e / jnp.where(den > 0, den, 1.0)


def _nsa_select_kernel(q_ref, kc_ref, vc_ref, ovt_ref, rope_ref, qg_ref, qaug_ref, ocmp_ref,
                       *, n_sel):
    tq = q_ref.shape[0]
    ncr = kc_ref.shape[1]
    n_slc = ovt_ref.shape[0]
    t0 = pl.program_id(1) * tq
    scale = HEAD_DIM ** -0.5
    q = q_ref[...]
    g = qg_ref[...]
    kc = kc_ref[0]
    vc = vc_ref[0]
    last = CMP_BLOCK - 1
    vis = (lax.broadcasted_iota(jnp.int32, (tq, ncr), 1) * CMP_STRIDE + last
           <= t0 + lax.broadcasted_iota(jnp.int32, (tq, ncr), 0))
    vis_t = (lax.broadcasted_iota(jnp.int32, (ncr, tq), 0) * CMP_STRIDE + last
             <= t0 + lax.broadcasted_iota(jnp.int32, (ncr, tq), 1))
    qn_heads, o_heads = [], []
    psum_t = jnp.zeros((ncr, tq), F32)
    for h in range(GROUP_HEADS):
        qn = _rms_rows(q[:, h * HEAD_DIM:(h + 1) * HEAD_DIM], g)
        qn_heads.append(qn)
        qb = (qn * scale).astype(BF16)
        p = _masked_softmax(_dot_nt(qb, kc), vis, 1)
        o_heads.append(_dot(p.astype(BF16), vc))
        psum_t = psum_t + _masked_softmax(_dot_nt(kc, qb), vis_t, 0)
    ocmp_ref[0] = jnp.concatenate(o_heads, axis=-1)

    hi = psum_t.astype(BF16)
    lo = (psum_t - hi.astype(F32)).astype(BF16)
    imp = _dot(ovt_ref[...], hi) + _dot(ovt_ref[...], lo)
    blk = lax.broadcasted_iota(jnp.int32, (n_slc, tq), 0)
    cur = jnp.right_shift(t0 + lax.broadcasted_iota(jnp.int32, (n_slc, tq), 1), SLC_SHIFT)
    forced = (blk == 0) | (blk == cur) | (blk == cur - 1)
    imp = jnp.where(forced, jnp.inf, jnp.where(blk > cur, -jnp.inf, imp))
    n_grp = n_slc // SUBLANES
    groups = [imp[g * SUBLANES:(g + 1) * SUBLANES] for g in range(n_grp)]
    cnts = [jnp.zeros((SUBLANES, tq), F32) for _ in range(n_grp)]
    sub = lax.broadcasted_iota(jnp.int32, (SUBLANES, tq), 0)
    for m in range(n_slc):
        row = imp[m:m + 1, :]
        gm, rm = divmod(m, SUBLANES)
        for g in range(n_grp):
            if g < gm:
                beats = row > groups[g]
            elif g > gm:
                beats = row >= groups[g]
            else:
                beats = (row > groups[g]) | ((row == groups[g]) & (sub > rm))
            cnts[g] = cnts[g] + jnp.where(beats, 1.0, 0.0)
    cnt = jnp.concatenate(cnts, axis=0)
    mt = jnp.where(cnt < n_sel, 0.0, NEG_BIG)
    pad = jnp.zeros((HEAD_DIM - n_slc, tq), F32)
    mt = jnp.concatenate([mt, pad, mt, pad], axis=0) if n_slc < HEAD_DIM else jnp.concatenate(
        [mt, mt], axis=0)
    mt = mt.T

    qr = _apply_rope(jnp.concatenate(qn_heads, axis=-1), rope_ref) * scale
    lane = lax.broadcasted_iota(jnp.int32, (tq, LANES), 1)
    for j in range(GROUP_HEADS // 2):
        pair = qr[:, j * LANES:(j + 1) * LANES]
        swapped = pltpu.roll(pair, HEAD_DIM, 1)
        qaug_ref[0, :, (2 * j) * LANES:(2 * j + 1) * LANES] = jnp.where(
            lane < HEAD_DIM, pair, mt).astype(BF16)
        qaug_ref[0, :, (2 * j + 1) * LANES:(2 * j + 2) * LANES] = jnp.where(
            lane < HEAD_DIM, swapped, mt).astype(BF16)


def nsa_select(cols, b_, l_, k_cmp, v_cmp, rope_q, q_gain):
    tq = min(NSA_SEL_TQ, l_)
    nt = l_ // tq
    ncr = l_ // CMP_STRIDE
    n_slc = l_ // SLC_BLOCK
    n_sel = min(SLC_TOPN, n_slc)
    st_c = np.arange(ncr) * CMP_STRIDE
    st_s = np.arange(n_slc) * SLC_BLOCK
    ovt = ((st_c[None, :] < st_s[:, None] + SLC_BLOCK)
           & (st_c[None, :] + CMP_BLOCK > st_s[:, None])).astype(np.float32)
    return pl.pallas_call(
        functools.partial(_nsa_select_kernel, n_sel=n_sel),
        out_shape=(jax.ShapeDtypeStruct((b_, l_, GROUP_HEADS * LANES), BF16),
                   jax.ShapeDtypeStruct((b_, l_, GROUP_WIDTH), F32)),
        grid=(b_, nt),
        in_specs=[
            pl.BlockSpec((tq, GROUP_WIDTH), lambda b, i: (b * nt + i, C_NSA_Q // GROUP_WIDTH)),
            pl.BlockSpec((1, ncr, HEAD_DIM), lambda b, i: (b, 0, 0)),
            pl.BlockSpec((1, ncr, HEAD_DIM), lambda b, i: (b, 0, 0)),
            pl.BlockSpec((n_slc, ncr), lambda b, i: (0, 0)),
            pl.BlockSpec((3, tq, GROUP_WIDTH), lambda b, i: (0, i, 0)),
            pl.BlockSpec((1, HEAD_DIM), lambda b, i: (0, 0)),
        ],
        out_specs=(pl.BlockSpec((1, tq, GROUP_HEADS * LANES), lambda b, i: (b, i, 0)),
                   pl.BlockSpec((1, tq, GROUP_WIDTH), lambda b, i: (b, i, 0))),
        compiler_params=pltpu.CompilerParams(
            dimension_semantics=("arbitrary", "arbitrary"),
            vmem_limit_bytes=V7X_VMEM_LIMIT_BYTES),
        name="nsa_select",
    )(cols, k_cmp, v_cmp, jnp.asarray(ovt, BF16), rope_q, q_gain.reshape(1, HEAD_DIM))


def _softmax_stats_update(s, m, l):
    m_new = jnp.maximum(m, jnp.max(s, axis=0, keepdims=True))
    alpha = jnp.exp(m - m_new)
    p = jnp.exp(s - m_new)
    return m_new, alpha, p, alpha * l + jnp.sum(p, axis=0, keepdims=True)


def _nsa_attend_t_kernel(qaug_ref, kaug_ref, vst_ref, kw_ref, vwt_ref, ocmp_ref, gate_ref, o_ref):
    tq = qaug_ref.shape[1]
    nh = GROUP_HEADS
    ts = NSA_KEY_TILE
    t0 = pl.program_id(1) * tq
    qa = qaug_ref[0]
    qs = jnp.concatenate([qa[:, h * LANES:(h + 1) * LANES] for h in range(nh)], axis=0)
    n = nh * tq
    qpos_tile = t0 + lax.broadcasted_iota(jnp.int32, (1, tq), 1)
    qpos = jnp.concatenate([qpos_tile] * nh, axis=1)

    def key_tile(kt, carry):
        m, l, acc = carry
        k0 = pl.multiple_of(kt * ts, ts)
        s = _dot_nt(kaug_ref[0, pl.ds(k0, ts), :], qs)
        kpos = k0 + lax.broadcasted_iota(jnp.int32, (ts, n), 0)
        s = jnp.where(kpos <= qpos, s, NEG_BIG)
        m_new, alpha, p, l = _softmax_stats_update(s, m, l)
        acc = alpha * acc + _dot(vst_ref[0, :, pl.ds(k0, ts)], p.astype(BF16))
        return m_new, l, acc

    n_kt = (t0 + tq + ts - 1) // ts
    init = (jnp.full((1, n), -jnp.inf, F32), jnp.zeros((1, n), F32),
            jnp.zeros((HEAD_DIM, n), F32))
    _, l, acc = lax.fori_loop(0, n_kt, key_tile, init)
    o_slc = acc / l

    wlen = WINDOW + tq
    start = pl.multiple_of(jnp.maximum(t0 - WINDOW, 0), tq)
    sw = _dot_nt(kw_ref[0, pl.ds(start, wlen), :], qs[:, :HEAD_DIM])
    dist = qpos - (start + lax.broadcasted_iota(jnp.int32, (wlen, n), 0))
    sw = jnp.where((dist >= 0) & (dist < WINDOW), sw, -jnp.inf)
    e = jnp.exp(sw - jnp.max(sw, axis=0, keepdims=True))
    o_swa = _dot(vwt_ref[0, :, pl.ds(start, wlen)], e.astype(BF16)) / jnp.sum(
        e, axis=0, keepdims=True)

    g = jax.nn.sigmoid(gate_ref[...]).T
    oc = ocmp_ref[0].T
    outs = []
    for h in range(nh):
        cols_h = slice(h * tq, (h + 1) * tq)
        outs.append(g[h:h + 1, :] * oc[h * HEAD_DIM:(h + 1) * HEAD_DIM, :]
                    + g[nh + h:nh + h + 1, :] * o_slc[:, cols_h]
                    + g[2 * nh + h:2 * nh + h + 1, :] * o_swa[:, cols_h])
    o_ref[...] = jnp.concatenate(outs, axis=0).T


def nsa_attend(cols, b_, l_, q_aug, k_aug, v_slc, k_win, v_win, o_cmp):
    tq = NSA_TQ
    nt = l_ // tq
    seq = lambda w: pl.BlockSpec((1, l_, w), lambda b, i: (b, 0, 0))
    seq_t = pl.BlockSpec((1, HEAD_DIM, l_), lambda b, i: (b, 0, 0))
    return pl.pallas_call(
        _nsa_attend_t_kernel,
        out_shape=jax.ShapeDtypeStruct((b_ * l_, GROUP_WIDTH), F32),
        grid=(b_, nt),
        in_specs=[
            pl.BlockSpec((1, tq, GROUP_HEADS * LANES), lambda b, i: (b, i, 0)),
            seq(LANES), seq_t, seq(HEAD_DIM), seq_t,
            pl.BlockSpec((1, tq, GROUP_WIDTH), lambda b, i: (b, i, 0)),
            pl.BlockSpec((tq, LANES), lambda b, i: (b * nt + i, C_NSA_GATE // LANES)),
        ],
        out_specs=pl.BlockSpec((tq, GROUP_WIDTH), lambda b, i: (b * nt + i, 0)),
        compiler_params=pltpu.CompilerParams(
            dimension_semantics=("arbitrary", "arbitrary"),
            vmem_limit_bytes=V7X_VMEM_LIMIT_BYTES),
        name="nsa_attend",
    )(q_aug, k_aug, v_slc, k_win, v_win, o_cmp, cols)


def nsa_mixer_pallas(cols, b_, l_, rope, pos_k, pos_v, k_w1, k_w2, v_w1, v_w2, q_gain, k_gains):
    k_aug, v_slc, k_win, v_win = nsa_kv_prep(cols, b_, l_, rope_tables_kv_pair(rope),
                                             k_gains[1], k_gains[2])
    k_cmp, v_cmp = nsa_compress(cols, b_, l_, pos_k, pos_v, k_w1, k_w2, v_w1, v_w2, k_gains[0])
    q_aug, o_cmp = nsa_select(cols, b_, l_, k_cmp, v_cmp, jnp.tile(rope, (1, 1, GROUP_HEADS)),
                              q_gain)
    return nsa_attend(cols, b_, l_, q_aug, k_aug, v_slc, k_win, v_win, o_cmp)


DSA_TQ = 128
DSA_KEY_TILE = 512
DSA_PREP_TK = 512
IDX_PACK = LANES
INT_MIN = -2 ** 31
MASKED_SCORE = -1e30


def _split_hi_lo(t):
    hi = t.astype(BF16)
    lo = (t - hi.astype(F32)).astype(BF16)
    return hi, lo


def _placement(rows, cols, pairs):
    p = np.zeros((rows, cols), np.float32)
    for r, c in pairs:
        p[r, c] = 1.0
    return jnp.asarray(p, BF16)


def _dsa_kv_prep_kernel(ckv_ref, ik_ref, rope_ref, ropei_ref, kvg_ref, wkv_ref, kg_ref, ikg_ref,
                        pkh_ref, pkl_ref, k_ref, vt_ref, ik3_ref):
    tk = ckv_ref.shape[0]
    lane = lax.broadcasted_iota(jnp.int32, (tk, LANES), 1)
    ckv = _rms_rows(ckv_ref[...], kvg_ref[...]).astype(BF16)
    kv = _dot(ckv, wkv_ref[...])
    is_k = lane < HEAD_DIM
    ms = jnp.sum(jnp.where(is_k, kv * kv, 0.0), axis=-1, keepdims=True) * (1.0 / HEAD_DIM)
    y = _apply_rope(jnp.where(is_k, kv * lax.rsqrt(ms + EPS) * kg_ref[...], kv), rope_ref)
    k_ref[0] = y[:, :HEAD_DIM].astype(BF16)
    vt_ref[0] = y.T[HEAD_DIM:, :].astype(BF16)
    ik = ik_ref[...]
    ms = jnp.sum(ik * ik, axis=-1, keepdims=True) * (1.0 / IDX_DIM)
    ikn = _apply_rope(ik * lax.rsqrt(ms + EPS) * ikg_ref[...], ropei_ref, IDX_DIM // 8)
    hi, lo = _split_hi_lo(ikn)
    ik3_ref[0] = (_dot(hi, pkh_ref[...]) + _dot(lo, pkl_ref[...])).astype(BF16)


def dsa_kv_prep(cols, b_, l_, rope_pair, rope_idx, kv_gain, w_uk, w_uv, k_gain, idxk_gain):
    tk = DSA_PREP_TK
    nt = l_ // tk
    ones = jnp.ones((HEAD_DIM,), F32)
    kg = jnp.concatenate([k_gain, ones]).reshape(1, LANES)
    ikg = jnp.pad(idxk_gain, (0, LANES - IDX_DIM)).reshape(1, LANES)
    wkv = jnp.concatenate([w_uk, w_uv], axis=1).astype(BF16)
    d = range(IDX_DIM)
    pkh = _placement(LANES, IDX_PACK, [(i, i) for i in d] + [(i, 2 * IDX_DIM + i) for i in d])
    pkl = _placement(LANES, IDX_PACK, [(i, IDX_DIM + i) for i in d])
    const = lambda shape: pl.BlockSpec(shape, lambda b, i: (0,) * len(shape))
    out = lambda w: pl.BlockSpec((1, tk, w), lambda b, i: (b, i, 0))
    return pl.pallas_call(
        _dsa_kv_prep_kernel,
        out_shape=(jax.ShapeDtypeStruct((b_, l_, HEAD_DIM), BF16),
                   jax.ShapeDtypeStruct((b_, HEAD_DIM, l_), BF16),
                   jax.ShapeDtypeStruct((b_, l_, IDX_PACK), BF16)),
        grid=(b_, nt),
        in_specs=[
            pl.BlockSpec((tk, LANES), lambda b, i: (b * nt + i, C_DSA_CKV // LANES)),
            pl.BlockSpec((tk, LANES), lambda b, i: (b * nt + i, C_DSA_IK // LANES)),
            pl.BlockSpec((3, tk, LANES), lambda b, i: (0, i, 0)),
            pl.BlockSpec((3, tk, LANES), lambda b, i: (0, i, 0)),
            const((1, LANES)), const((DSA_LATENT, LANES)), const((1, LANES)), const((1, LANES)),
            const((LANES, IDX_PACK)), const((LANES, IDX_PACK)),
        ],
        out_specs=(out(HEAD_DIM), pl.BlockSpec((1, HEAD_DIM, tk), lambda b, i: (b, 0, i)),
                   out(IDX_PACK)),
        compiler_params=pltpu.CompilerParams(
            dimension_semantics=("arbitrary", "arbitrary"),
            vmem_limit_bytes=V7X_VMEM_LIMIT_BYTES),
        name="dsa_kv_prep",
    )(cols, cols, rope_pair, rope_idx, kv_gain.reshape(1, DSA_LATENT), wkv, kg, ikg, pkh, pkl)


def _dsa_q_prep_kernel(q_ref, iq_ref, rope_ref, ropei_ref, qg_ref, pqh_ref, pql_ref,
                       qh_ref, iq3_ref):
    g = qg_ref[...]
    q = q_ref[...]
    qn = jnp.concatenate([_rms_rows(q[:, h * HEAD_DIM:(h + 1) * HEAD_DIM], g)
                          for h in range(GROUP_HEADS)], axis=-1)
    qr = _apply_rope(qn, rope_ref) * (HEAD_DIM ** -0.5)
    for h in range(GROUP_HEADS):
        qh_ref[0, h] = qr[:, h * HEAD_DIM:(h + 1) * HEAD_DIM].astype(BF16)
    hi, lo = _split_hi_lo(_apply_rope(iq_ref[...], ropei_ref, IDX_DIM // 8))
    iq3_ref[0] = (_dot(hi, pqh_ref[...]) + _dot(lo, pql_ref[...])).astype(BF16)


def dsa_q_prep(cols, b_, l_, rope_q, rope_iq, q_gain):
    tq = 256
    nt = l_ // tq
    w = IDX_HEADS * IDX_DIM
    hd = [(h, i) for h in range(IDX_HEADS) for i in range(IDX_DIM)]
    pqh = _placement(w, IDX_HEADS * IDX_PACK,
                     [(IDX_DIM * h + i, IDX_PACK * h + i) for h, i in hd]
                     + [(IDX_DIM * h + i, IDX_PACK * h + IDX_DIM + i) for h, i in hd])
    pql = _placement(w, IDX_HEADS * IDX_PACK,
                     [(IDX_DIM * h + i, IDX_PACK * h + 2 * IDX_DIM + i) for h, i in hd])
    const = lambda shape: pl.BlockSpec(shape, lambda b, i: (0,) * len(shape))
    return pl.pallas_call(
        _dsa_q_prep_kernel,
        out_shape=(jax.ShapeDtypeStruct((b_, GROUP_HEADS, l_, HEAD_DIM), BF16),
                   jax.ShapeDtypeStruct((b_, l_, IDX_HEADS * IDX_PACK), BF16)),
        grid=(b_, nt),
        in_specs=[
            pl.BlockSpec((tq, GROUP_WIDTH), lambda b, i: (b * nt + i, C_DSA_Q // GROUP_WIDTH)),
            pl.BlockSpec((tq, w), lambda b, i: (b * nt + i, C_DSA_IQ // w)),
            pl.BlockSpec((3, tq, GROUP_WIDTH), lambda b, i: (0, i, 0)),
            pl.BlockSpec((3, tq, w), lambda b, i: (0, i, 0)),
            const((1, HEAD_DIM)), const(pqh.shape), const(pql.shape),
        ],
        out_specs=(pl.BlockSpec((1, GROUP_HEADS, tq, HEAD_DIM), lambda b, i: (b, 0, i, 0)),
                   pl.BlockSpec((1, tq, IDX_HEADS * IDX_PACK), lambda b, i: (b, i, 0))),
        compiler_params=pltpu.CompilerParams(
            dimension_semantics=("arbitrary", "arbitrary"),
            vmem_limit_bytes=V7X_VMEM_LIMIT_BYTES),
        name="dsa_q_prep",
    )(cols, cols, rope_q, rope_iq, q_gain.reshape(1, HEAD_DIM), pqh, pql)


SUBLANES = 8


def _fold_rows(t, op, group=SUBLANES):
    parts = [t[i * group:(i + 1) * group] for i in range(t.shape[0] // group)]
    while len(parts) > 1:
        parts = [op(parts[i], parts[i + 1]) if i + 1 < len(parts) else parts[i]
                 for i in range(0, len(parts), 2)]
    return parts[0]


def _dsa_attend_t_kernel(qh_ref, iq3_ref, iw_ref, k_ref, vt_ref, ik3_ref, o_ref, sc_ref, *,
                         topk, idx_bits):
    tq = iq3_ref.shape[1]
    ts = DSA_KEY_TILE
    nh = GROUP_HEADS
    t0 = pl.program_id(1) * tq
    n_kt = (t0 + tq + ts - 1) // ts
    qpos = t0 + lax.broadcasted_iota(jnp.int32, (ts, tq), 1)
    krow = lax.broadcasted_iota(jnp.int32, (ts, tq), 0)

    iq3 = iq3_ref[0]
    iq_rows = jnp.concatenate([iq3[:, h * IDX_PACK:(h + 1) * IDX_PACK]
                               for h in range(IDX_HEADS)], axis=0)
    iw_t = (iw_ref[...] * (IDX_HEADS ** -0.5 * IDX_DIM ** -0.5)).T

    def score_tile(kt, _):
        k0 = pl.multiple_of(kt * ts, ts)
        rel = jnp.maximum(_dot_nt(ik3_ref[0, pl.ds(k0, ts), :], iq_rows), 0.0)
        sc = rel[:, 0:tq] * iw_t[0:1, :]
        for h in range(1, IDX_HEADS):
            sc = sc + rel[:, h * tq:(h + 1) * tq] * iw_t[h:h + 1, :]
        sc_ref[pl.ds(k0, ts), :] = jnp.where(k0 + krow <= qpos, sc, -jnp.inf)
        return 0

    lax.fori_loop(0, n_kt, score_tile, 0)

    def count(pred):
        def tile(kt, c):
            k0 = pl.multiple_of(kt * ts, ts)
            hit = jnp.where(pred(sc_ref[pl.ds(k0, ts), :], k0), 1.0, 0.0)
            return c + _fold_rows(hit, jnp.add)
        c = lax.fori_loop(0, n_kt, tile, jnp.zeros((SUBLANES, tq), F32))
        return jnp.sum(c, axis=0, keepdims=True)

    def key_to_float(key):
        return pltpu.bitcast(jnp.where(key >= 0, key, key ^ jnp.int32(0x7FFFFFFF)), F32)

    def value_bit(i, thr_key):
        cand = thr_key | jnp.left_shift(jnp.int32(1), 31 - i)
        cand_f = key_to_float(cand ^ jnp.int32(INT_MIN))
        return jnp.where(count(lambda sc, k0: sc >= cand_f) >= topk, cand, thr_key)

    thr_key = lax.fori_loop(0, 32, value_bit, jnp.zeros((1, tq), jnp.int32))
    few = t0 + lax.broadcasted_iota(jnp.int32, (1, tq), 1) + 1 < topk
    thr = jnp.where(few, -jnp.inf, key_to_float(thr_key ^ jnp.int32(INT_MIN)))
    n_ge = count(lambda sc, k0: sc >= thr)
    tie_break = jnp.max(jnp.where((n_ge > topk) & jnp.logical_not(few), 1.0, 0.0)) > 0.0

    def last_tie_position():
        need = topk - count(lambda sc, k0: sc > thr)

        def index_bit(i, last):
            cand = last | jnp.left_shift(jnp.int32(1), idx_bits - 1 - i)
            tied_below = lambda sc, k0: (sc == thr) & (k0 + krow < cand)
            return jnp.where(count(tied_below) < need, cand, last)

        return lax.fori_loop(0, idx_bits, index_bit, jnp.zeros((1, tq), jnp.int32))

    last = lax.cond(tie_break, last_tie_position,
                    lambda: jnp.full((1, tq), 2 ** idx_bits, jnp.int32))

    qs = qh_ref[0].reshape(nh * tq, HEAD_DIM)

    def key_tile(kt, carry):
        m, l, acc = carry
        k0 = pl.multiple_of(kt * ts, ts)
        sc = sc_ref[pl.ds(k0, ts), :]
        kpos = k0 + krow
        sel = ((sc > thr) | ((sc == thr) & (kpos <= last))) & (kpos <= qpos)
        bias = jnp.where(sel, 0.0, MASKED_SCORE)
        s = _dot_nt(k_ref[0, pl.ds(k0, ts), :], qs) + jnp.concatenate([bias] * nh, axis=1)
        m_new = jnp.maximum(m, jnp.max(s, axis=0, keepdims=True))
        alpha = jnp.exp(m - m_new)
        p = jnp.exp(s - m_new)
        l = alpha * l + jnp.sum(p, axis=0, keepdims=True)
        acc = alpha * acc + _dot(vt_ref[0, :, pl.ds(k0, ts)], p.astype(BF16))
        return m_new, l, acc

    init = (jnp.full((1, nh * tq), MASKED_SCORE, F32), jnp.zeros((1, nh * tq), F32),
            jnp.zeros((HEAD_DIM, nh * tq), F32))
    _, l, acc = lax.fori_loop(0, n_kt, key_tile, init)
    o_t = acc / l
    o_ref[...] = jnp.concatenate([o_t[:, h * tq:(h + 1) * tq] for h in range(nh)], axis=0).T


def dsa_attend(cols, b_, l_, qh, iq3, k, v, ik3):
    tq = DSA_TQ
    nt = l_ // tq
    topk = min(DSA_TOPK_MAX, l_ // 4)
    idx_bits = int(np.log2(l_))
    assert 2 ** idx_bits == l_ and l_ % DSA_KEY_TILE == 0 and topk <= DSA_KEY_TILE
    seq = lambda w: pl.BlockSpec((1, l_, w), lambda b, i: (b, 0, 0))
    return pl.pallas_call(
        functools.partial(_dsa_attend_t_kernel, topk=topk, idx_bits=idx_bits),
        out_shape=jax.ShapeDtypeStruct((b_ * l_, GROUP_WIDTH), F32),
        grid=(b_, nt),
        in_specs=[
            pl.BlockSpec((1, GROUP_HEADS, tq, HEAD_DIM), lambda b, i: (b, 0, i, 0)),
            pl.BlockSpec((1, tq, IDX_HEADS * IDX_PACK), lambda b, i: (b, i, 0)),
            pl.BlockSpec((tq, LANES), lambda b, i: (b * nt + i, C_DSA_IW // LANES)),
            seq(HEAD_DIM), pl.BlockSpec((1, HEAD_DIM, l_), lambda b, i: (b, 0, 0)), seq(IDX_PACK),
        ],
        out_specs=pl.BlockSpec((tq, GROUP_WIDTH), lambda b, i: (b * nt + i, 0)),
        scratch_shapes=[pltpu.VMEM((l_, tq), F32)],
        compiler_params=pltpu.CompilerParams(
            dimension_semantics=("arbitrary", "arbitrary"),
            vmem_limit_bytes=V7X_VMEM_LIMIT_BYTES),
        name="dsa_attend",
    )(qh, iq3, cols, k, v, ik3)


def dsa_mixer_pallas(cols, b_, l_, rope, kv_gain, w_uk, w_uv, q_gain, k_gain, idxk_gain):
    rope_i = rope_tables(l_, IDX_DIM)
    k, v, ik3 = dsa_kv_prep(cols, b_, l_, rope_tables_kv_pair(rope), rope_tables_pad(rope_i, LANES),
                            kv_gain, w_uk, w_uv, k_gain, idxk_gain)
    qh, iq3 = dsa_q_prep(cols, b_, l_, jnp.tile(rope, (1, 1, GROUP_HEADS)),
                         jnp.tile(rope_i, (1, 1, IDX_HEADS)), q_gain)
    return dsa_attend(cols, b_, l_, qh, iq3, k, v, ik3)


_TN = (((0,), (0,)), ((), ()))


def _dot_tn(a, b):
    return lax.dot_general(a, b, _TN, preferred_element_type=F32)


def _split3(t):
    hi = t.astype(BF16)
    r = t - hi.astype(F32)
    mid = r.astype(BF16)
    lo = (r - mid.astype(F32)).astype(BF16)
    return hi, mid, lo


def _tri_cumsum(tri, t):
    hi, mid, lo = _split3(t)
    return _dot(tri, hi) + _dot(tri, mid) + _dot(tri, lo)


def _cumsum_tri_rows(t, tri_u):
    hi, mid, lo = _split3(t)
    return _dot(hi, tri_u) + _dot(mid, tri_u) + _dot(lo, tri_u)


def _head_rms(o, gain):
    return jnp.concatenate([_rms_rows(o[:, h * HEAD_DIM:(h + 1) * HEAD_DIM], gain)
                            for h in range(GROUP_HEADS)], axis=-1)


HG_SUB = 8
LOG2_E = 1.4426950408889634


def _hgrn2_kernel(q_ref, f_ref, i_ref, g_ref, lb_ref, gain_ref, tri_ref, ones_ref, bd_ref,
                  o_ref, st_ref, b_ref, kk_ref, v_ref):
    c = HG_CHUNK
    w = GROUP_WIDTH

    @pl.when(pl.program_id(1) == 0)
    def _():
        st_ref[...] = jnp.zeros_like(st_ref)

    lb = lb_ref[...]
    q = q_ref[...]
    qs = q * jax.nn.sigmoid(q) * (HEAD_DIM ** -0.5)
    forget = lb + (1.0 - lb) * jax.nn.sigmoid(f_ref[...])
    kk = 1.0 - forget
    bcum = _tri_cumsum(tri_ref[...], jnp.log(forget))
    v = i_ref[...]
    b2 = bcum * LOG2_E
    b_ref[...] = b2
    kk_ref[...] = kk
    v_ref[...] = v

    out = _dot_nt((qs * jnp.exp(bcum)).astype(BF16), st_ref[...].astype(BF16))

    ones_bd = ones_ref[...]
    pieces = []
    for g in range(c // HG_SUB):
        r0 = g * HG_SUB
        nr = c - r0
        qg = qs[r0:, :]
        bg = b2[r0:, :]
        trow = r0 + lax.broadcasted_iota(jnp.int32, (HG_SUB, w), 0)
        terms = []
        for j in range(HG_SUB):
            s = r0 + j
            d = qg * kk_ref[s:s + 1, :] * jnp.exp2(bg - b_ref[s:s + 1, :])
            if j > 0:
                head = jnp.where(trow >= s, d[:HG_SUB], 0.0)
                d = jnp.concatenate([head, d[HG_SUB:]], axis=0) if nr > HG_SUB else head
            terms.append(d.astype(BF16))
        red = _dot(jnp.concatenate(terms, axis=0), ones_bd)
        acc = red[0:nr] * v_ref[r0:r0 + 1, :]
        for j in range(1, HG_SUB):
            acc = acc + red[j * nr:(j + 1) * nr] * v_ref[r0 + j:r0 + j + 1, :]
        pieces.append(acc)
    intra = pieces[0]
    for g in range(1, c // HG_SUB):
        pad = jnp.zeros((g * HG_SUB, w), F32)
        intra = intra + jnp.concatenate([pad, pieces[g]], axis=0)
    out = out + intra

    b_last = bcum[c - 1:c, :]
    kt = (kk * jnp.exp(b_last - bcum)).astype(BF16)
    st_ref[...] = jnp.exp(b_last) * st_ref[...] + _dot_tn(v.astype(BF16), kt) * bd_ref[...]

    g_in = g_ref[...]
    o_ref[...] = _head_rms(out, gain_ref[...]) * (g_in * jax.nn.sigmoid(g_in))


def hgrn2_mixer_pallas(cols, b_, l_, lb, o_gain):
    c = HG_CHUNK
    nt = l_ // c
    w = GROUP_WIDTH
    head = np.arange(w) // HEAD_DIM
    same = (head[:, None] == head[None, :]).astype(np.float32)
    tri = np.tril(np.ones((c, c), np.float32))
    col = lambda j: pl.BlockSpec((c, w), lambda b, i: (b * nt + i, C_HG // w + j))
    const = lambda shape: pl.BlockSpec(shape, lambda b, i: (0,) * len(shape))
    return pl.pallas_call(
        _hgrn2_kernel,
        out_shape=jax.ShapeDtypeStruct((b_ * l_, w), F32),
        grid=(b_, nt),
        in_specs=[col(0), col(1), col(2), col(3), const((1, w)), const((1, HEAD_DIM)),
                  const((c, c)), const((w, w)), const((w, w))],
        out_specs=pl.BlockSpec((c, w), lambda b, i: (b * nt + i, 0)),
        scratch_shapes=[pltpu.VMEM((w, w), F32), pltpu.VMEM((c, w), F32),
                        pltpu.VMEM((c, w), F32), pltpu.VMEM((c, w), F32)],
        compiler_params=pltpu.CompilerParams(
            dimension_semantics=("arbitrary", "arbitrary"),
            vmem_limit_bytes=V7X_VMEM_LIMIT_BYTES),
        name="hgrn2",
    )(cols, cols, cols, cols, lb.reshape(1, w), o_gain.reshape(1, HEAD_DIM),
      jnp.asarray(tri, BF16), jnp.asarray(same, BF16), jnp.asarray(same, F32))


ML_TC = 256
ML_M_INIT = -1e30


def _mlstm_kernel(gate_ref, qk_ref, v_ref, og_ref, cw_ref, cb_ref, gb_ref, gain_ref, tril_ref,
                  triu_ref, o_ref, xprev_ref, cmat_ref, nvec_ref, m_ref):
    c = ML_TC
    nh = GROUP_HEADS
    w = GROUP_WIDTH

    @pl.when(pl.program_id(1) == 0)
    def _():
        xprev_ref[...] = jnp.zeros_like(xprev_ref)
        cmat_ref[...] = jnp.zeros_like(cmat_ref)
        nvec_ref[...] = jnp.zeros_like(nvec_ref)
        m_ref[...] = jnp.full(m_ref.shape, ML_M_INIT, F32)

    x = qk_ref[...]
    prev = xprev_ref[...]
    row = lax.broadcasted_iota(jnp.int32, x.shape, 0)
    acc = x * cw_ref[CONV_WIDTH - 1:CONV_WIDTH, :] + cb_ref[...]
    for j in range(1, CONV_WIDTH):
        shifted = jnp.where(row < j, pltpu.roll(prev, j, 0), pltpu.roll(x, j, 0))
        acc = acc + shifted * cw_ref[CONV_WIDTH - 1 - j:CONV_WIDTH - j, :]
    xprev_ref[...] = x
    qk = acc * jax.nn.sigmoid(acc)
    q = qk[:, :w]
    k = qk[:, w:] * (HEAD_DIM ** -0.5)
    v = v_ref[...]

    pre = gate_ref[...] + gb_ref[...]
    lane = lax.broadcasted_iota(jnp.int32, pre.shape, 1)
    log_f = jnp.minimum(pre, 0.0) - jnp.log1p(jnp.exp(-jnp.abs(pre)))
    log_f = jnp.where((lane >= nh) & (lane < 2 * nh), log_f, 0.0)
    bcum_c = _tri_cumsum(tril_ref[...], log_f)
    bcum_r = _cumsum_tri_rows(log_f.T, triu_ref[...])
    pre_r = pre.T
    tri = (lax.broadcasted_iota(jnp.int32, (c, c), 0) >= lax.broadcasted_iota(jnp.int32, (c, c), 1))

    outs = []
    for h in range(nh):
        sl = slice(h * HEAD_DIM, (h + 1) * HEAD_DIM)
        qh, kh, vh = q[:, sl], k[:, sl], v[:, sl]
        bc = bcum_c[:, nh + h:nh + h + 1]
        li_c = pre[:, h:h + 1]
        a_r = pre_r[h:h + 1, :] - bcum_r[nh + h:nh + h + 1, :]
        m_prev = m_ref[h:h + 1, 0:1]
        log_d = jnp.where(tri, bc + a_r, -jnp.inf)
        inter = bc + m_prev
        m_t = jnp.maximum(inter, jnp.max(log_d, axis=-1, keepdims=True))
        d_mat = jnp.exp(log_d - m_t)
        w_inter = jnp.exp(inter - m_t)
        qb = qh.astype(BF16)
        s = _dot_nt(qb, kh.astype(BF16)) * d_mat
        num = w_inter * _dot(qb, cmat_ref[h].astype(BF16)) + _dot(s.astype(BF16), vh.astype(BF16))
        den = (w_inter * jnp.sum(qh * nvec_ref[h], axis=-1, keepdims=True)
               + jnp.sum(s, axis=-1, keepdims=True))
        outs.append(num / jnp.maximum(jnp.abs(den), jnp.exp(-m_t)))
        b_last = bc[c - 1:c, :]
        log_w = b_last + (li_c - bc)
        m_new = jnp.maximum(b_last + m_prev, jnp.max(log_w, axis=0, keepdims=True))
        kw = kh * jnp.exp(log_w - m_new)
        decay = jnp.exp(b_last + m_prev - m_new)
        cmat_ref[h] = decay * cmat_ref[h] + _dot_tn(kw.astype(BF16), vh.astype(BF16))
        nvec_ref[h] = decay * nvec_ref[h] + jnp.sum(kw, axis=0, keepdims=True)
        m_ref[h:h + 1, :] = jnp.broadcast_to(m_new, (1, LANES))

    hh = _head_rms(jnp.concatenate(outs, axis=-1), gain_ref[...])
    o_ref[...] = hh * jax.nn.sigmoid(og_ref[...])


def mlstm_mixer_pallas(cols, b_, l_, conv_w, conv_b, i_bias, f_bias, o_gain):
    c = ML_TC
    nt = l_ // c
    w = GROUP_WIDTH
    gb = jnp.pad(jnp.concatenate([i_bias, f_bias]), (0, LANES - 2 * GROUP_HEADS)).reshape(1, LANES)
    tril = np.tril(np.ones((c, c), np.float32))
    const = lambda shape: pl.BlockSpec(shape, lambda b, i: (0,) * len(shape))
    blk = lambda width, off: pl.BlockSpec((c, width), lambda b, i: (b * nt + i, off // width))
    return pl.pallas_call(
        _mlstm_kernel,
        out_shape=jax.ShapeDtypeStruct((b_ * l_, w), F32),
        grid=(b_, nt),
        in_specs=[blk(LANES, C_ML_GATE), blk(2 * w, C_ML_QK), blk(w, C_ML_V), blk(w, C_ML_OG),
                  const((CONV_WIDTH, 2 * w)), const((1, 2 * w)), const((1, LANES)),
                  const((1, HEAD_DIM)), const((c, c)), const((c, c))],
        out_specs=pl.BlockSpec((c, w), lambda b, i: (b * nt + i, 0)),
        scratch_shapes=[pltpu.VMEM((c, 2 * w), F32),
                        pltpu.VMEM((GROUP_HEADS, HEAD_DIM, HEAD_DIM), F32),
                        pltpu.VMEM((GROUP_HEADS, 1, HEAD_DIM), F32),
                        pltpu.VMEM((8, LANES), F32)],
        compiler_params=pltpu.CompilerParams(
            dimension_semantics=("arbitrary", "arbitrary"),
            vmem_limit_bytes=V7X_VMEM_LIMIT_BYTES),
        name="mlstm",
    )(cols, cols, cols, cols, conv_w, conv_b.reshape(1, 2 * w), gb, o_gain.reshape(1, HEAD_DIM),
      jnp.asarray(tril, BF16), jnp.asarray(tril.T, BF16))


def kernel(x, mem, lb_param, norm_mix, w_in, w_out, hg_o_gain, dsa_kv_gain, dsa_w_uk, dsa_w_uv,
           dsa_q_gain, dsa_k_gain, dsa_idxk_gain, nsa_pos_k, nsa_pos_v, nsa_k_w1, nsa_k_w2,
           nsa_v_w1, nsa_v_w2, nsa_q_gain, nsa_k_gains, ml_conv_w, ml_conv_b, ml_i_bias,
           ml_f_bias, ml_o_gain, norm_xa, norm_mem, xa_wq, xa_wkv, xa_wo, xa_q_gain, xa_k_gain,
           norm_ffn, ffn_w13, ffn_w2):
    b_, l_, d = x.shape
    pos = jnp.arange(l_)
    lb_all = jnp.cumsum(jax.nn.softmax(lb_param.astype(F32), axis=0), axis=0)
    lb_all = lb_all - lb_all[:1]
    x2d = x.reshape(b_ * l_, d)
    mem2d = mem.reshape(b_ * N_MEM, d)
    rope = rope_tables(l_)
    for l in range(DEPTH):
        cols = norm_matmul(x2d, norm_mix[l], pack_w_in_bf16(w_in[l]), tm=256)
        mixed = jnp.concatenate([
            hgrn2_mixer_pallas(cols, b_, l_, lb_all[l], hg_o_gain[l]),
            dsa_mixer_pallas(cols, b_, l_, rope, dsa_kv_gain[l], dsa_w_uk[l], dsa_w_uv[l],
                             dsa_q_gain[l], dsa_k_gain[l], dsa_idxk_gain[l]),
            nsa_mixer_pallas(cols, b_, l_, rope, nsa_pos_k[l], nsa_pos_v[l], nsa_k_w1[l],
                             nsa_k_w2[l], nsa_v_w1[l], nsa_v_w2[l], nsa_q_gain[l],
                             nsa_k_gains[l]),
            mlstm_mixer_pallas(cols, b_, l_, ml_conv_w[l], ml_conv_b[l], ml_i_bias[l],
                               ml_f_bias[l], ml_o_gain[l]),
        ], axis=-1)
        k_mem, v_mem = mem_kv(mem2d, norm_mem[l], xa_wkv[l].astype(BF16), xa_k_gain[l])
        x2d = post_block(x2d, mixed.reshape(b_ * l_, -1), l_, w_out[l].astype(BF16), norm_xa[l],
                         xa_wq[l].astype(BF16), xa_q_gain[l], k_mem, v_mem,
                         xa_wo[l].astype(BF16), norm_ffn[l], ffn_w13[l].astype(BF16),
                         ffn_w2[l].astype(BF16), tm=512)
    return x2d.reshape(b_, l_, d)
```

```python
import functools

import jax
import jax.numpy as jnp
from jax import lax
import numpy as np
from jax.experimental import pallas as pl
from jax.experimental.pallas import tpu as pltpu

F32 = jnp.float32
BF16 = jnp.bfloat16

D_MODEL = 1024
DEPTH = 2
HEAD_DIM = 64
GROUP_HEADS = 4
GROUP_WIDTH = GROUP_HEADS * HEAD_DIM
ROPE_THETA = 500000.0
EPS = 1e-6
N_MEM = 256
XA_HEADS = 4
XA_WIDTH = XA_HEADS * HEAD_DIM
HG_CHUNK = 64
DSA_LATENT = 128
IDX_HEADS = 8
IDX_DIM = 32
DSA_TOPK_MAX = 256
CMP_BLOCK = 32
CMP_STRIDE = 16
SLC_BLOCK = 64
SLC_SHIFT = 6
SLC_TOPN = 16
WINDOW = 512
CONV_WIDTH = 4
D_FF = 2816

HG_SPLITS = (GROUP_WIDTH,) * 4
DSA_SPLITS = (GROUP_WIDTH, DSA_LATENT, IDX_HEADS * IDX_DIM, IDX_DIM, IDX_HEADS)
NSA_SPLITS = (GROUP_WIDTH,) + (HEAD_DIM,) * 6 + (3 * GROUP_HEADS,)
ML_SPLITS = (2 * GROUP_WIDTH, GROUP_WIDTH, GROUP_WIDTH, GROUP_HEADS, GROUP_HEADS)
GROUP_COLS = (sum(HG_SPLITS), sum(DSA_SPLITS), sum(NSA_SPLITS), sum(ML_SPLITS))
IN_COLS = sum(GROUP_COLS)

V7X_VMEM_LIMIT_BYTES = 56 * 1024 * 1024
LANES = 128
FF_CHUNK = 256


def _round_up(n, m):
    return -(-n // m) * m


def _rms_rows(t, g):
    return t * lax.rsqrt(jnp.mean(t * t, axis=-1, keepdims=True) + EPS) * g


def _const_spec(shape):
    return pl.BlockSpec(shape, lambda *_: (0,) * len(shape), pipeline_mode=pl.Buffered(1))


def _norm_matmul_kernel(x_ref, g_ref, w_ref, o_ref):
    h = _rms_rows(x_ref[...], g_ref[...]).astype(BF16)
    o_ref[...] = jnp.dot(h, w_ref[...], preferred_element_type=F32)


def norm_matmul(x2d, gain, w_bf16, tm):
    m, k = x2d.shape
    n = w_bf16.shape[1]
    return pl.pallas_call(
        _norm_matmul_kernel,
        out_shape=jax.ShapeDtypeStruct((m, n), F32),
        grid=(m // tm,),
        in_specs=[
            pl.BlockSpec((tm, k), lambda i: (i, 0)),
            _const_spec((1, k)),
            _const_spec((k, n)),
        ],
        out_specs=pl.BlockSpec((tm, n), lambda i: (i, 0)),
        compiler_params=pltpu.CompilerParams(
            dimension_semantics=("arbitrary",), vmem_limit_bytes=V7X_VMEM_LIMIT_BYTES),
        name="norm_matmul",
    )(x2d, gain.reshape(1, k), w_bf16)


def _mem_kv_kernel(m_ref, g_ref, w_ref, kg_ref, k_ref, v_ref):
    mn = _rms_rows(m_ref[...], g_ref[...]).astype(BF16)
    kv = jnp.dot(mn, w_ref[...], preferred_element_type=F32)
    kg = kg_ref[...]
    ks = []
    for h in range(XA_HEADS):
        kh = kv[:, h * HEAD_DIM:(h + 1) * HEAD_DIM]
        ks.append(_rms_rows(kh, kg))
    k_ref[...] = jnp.concatenate(ks, axis=-1).astype(BF16)
    v_ref[...] = kv[:, XA_WIDTH:].astype(BF16)


def mem_kv(mem2d, gain, wkv_bf16, k_gain):
    m, k = mem2d.shape
    return pl.pallas_call(
        _mem_kv_kernel,
        out_shape=(jax.ShapeDtypeStruct((m, XA_WIDTH), BF16),
                   jax.ShapeDtypeStruct((m, XA_WIDTH), BF16)),
        grid=(m // N_MEM,),
        in_specs=[
            pl.BlockSpec((N_MEM, k), lambda i: (i, 0)),
            _const_spec((1, k)),
            _const_spec((k, 2 * XA_WIDTH)),
            _const_spec((1, HEAD_DIM)),
        ],
        out_specs=(pl.BlockSpec((N_MEM, XA_WIDTH), lambda i: (i, 0)),
                   pl.BlockSpec((N_MEM, XA_WIDTH), lambda i: (i, 0))),
        compiler_params=pltpu.CompilerParams(
            dimension_semantics=("arbitrary",), vmem_limit_bytes=V7X_VMEM_LIMIT_BYTES),
        name="mem_kv",
    )(mem2d, gain.reshape(1, k), wkv_bf16, k_gain.reshape(1, HEAD_DIM))


def _post_kernel(x_ref, mhg_ref, mdsa_ref, mnsa_ref, mml_ref, wout_ref, gxa_ref, wq_ref, qg_ref,
                 k_ref, v_ref, wo_ref, gffn_ref, w13_ref, w2_ref, o_ref):
    x = x_ref[...]
    for g, mix_ref in enumerate((mhg_ref, mdsa_ref, mnsa_ref, mml_ref)):
        x = x + _dot(mix_ref[...].astype(BF16),
                     wout_ref[g * GROUP_WIDTH:(g + 1) * GROUP_WIDTH, :])
    h = _rms_rows(x, gxa_ref[...]).astype(BF16)
    q = _dot(h, wq_ref[...])
    qg = qg_ref[...] * (HEAD_DIM ** -0.5)
    k = k_ref[...]
    v = v_ref[...]
    heads = [slice(hd * HEAD_DIM, (hd + 1) * HEAD_DIM) for hd in range(XA_HEADS)]
    scores = [_dot_nt(_rms_rows(q[:, sl], qg).astype(BF16), k[:, sl]) for sl in heads]
    probs = []
    for s in scores:
        e = jnp.exp(s - jnp.max(s, axis=-1, keepdims=True))
        probs.append((e / jnp.sum(e, axis=-1, keepdims=True)).astype(BF16))
    o = jnp.concatenate([_dot(p, v[:, sl]) for p, sl in zip(probs, heads)], axis=-1)
    x = x + _dot(o.astype(BF16), wo_ref[...])
    h = _rms_rows(x, gffn_ref[...]).astype(BF16)

    def up(c):
        return (_dot(h, w13_ref[:, c * FF_CHUNK:(c + 1) * FF_CHUNK]),
                _dot(h, w13_ref[:, D_FF + c * FF_CHUNK:D_FF + (c + 1) * FF_CHUNK]))

    n_chunks = D_FF // FF_CHUNK
    acc = x
    a, b = up(0)
    for c in range(n_chunks):
        nxt = up(c + 1) if c + 1 < n_chunks else None
        act = (a * jax.nn.sigmoid(a) * b).astype(BF16)
        acc = acc + _dot(act, w2_ref[c * FF_CHUNK:(c + 1) * FF_CHUNK, :])
        if nxt is not None:
            a, b = nxt
    o_ref[...] = acc


def post_block(x2d, mixers, seq, wout, gxa, wq, qg, k_mem, v_mem, wo, gffn, w13, w2, tm):
    m, d = x2d.shape
    tiles_per_batch = seq // tm
    row = lambda i: (i, 0)
    mem_row = lambda i: (i // tiles_per_batch, 0)
    return pl.pallas_call(
        _post_kernel,
        out_shape=jax.ShapeDtypeStruct((m, d), F32),
        grid=(m // tm,),
        in_specs=[
            pl.BlockSpec((tm, d), row),
            *[pl.BlockSpec((tm, GROUP_WIDTH), row) for _ in mixers],
            _const_spec(wout.shape),
            _const_spec((1, d)),
            _const_spec(wq.shape),
            _const_spec((1, HEAD_DIM)),
            pl.BlockSpec((N_MEM, XA_WIDTH), mem_row),
            pl.BlockSpec((N_MEM, XA_WIDTH), mem_row),
            _const_spec(wo.shape),
            _const_spec((1, d)),
            _const_spec(w13.shape),
            _const_spec(w2.shape),
        ],
        out_specs=pl.BlockSpec((tm, d), row),
        compiler_params=pltpu.CompilerParams(
            dimension_semantics=("arbitrary",), vmem_limit_bytes=V7X_VMEM_LIMIT_BYTES),
        name="post_block",
    )(x2d, *mixers, wout, gxa.reshape(1, d), wq, qg.reshape(1, HEAD_DIM), k_mem, v_mem, wo,
      gffn.reshape(1, d), w13, w2)


C_HG = 0
C_DSA_Q = 1024
C_DSA_IQ = 1280
C_NSA_Q = 1536
C_ML_V = 1792
C_ML_QK = 2048
C_ML_OG = 2560
C_DSA_CKV = 2816
C_DSA_IK = 2944
C_DSA_IW = 3072
C_NSA_GATE = 3200
C_NSA_CMP = 3328
C_NSA_SLC = 3456
C_NSA_WIN = 3584
C_ML_GATE = 3712
IN_COLS_PACKED = 3840


def _packed_source_columns():
    dsa0 = GROUP_COLS[0]
    nsa0 = dsa0 + GROUP_COLS[1]
    ml0 = nsa0 + GROUP_COLS[2]
    segs = [(0, 1024, 1024), (dsa0, 256, 256), (dsa0 + 384, 256, 256), (nsa0, 256, 256),
            (ml0 + 512, 256, 256), (ml0, 512, 512), (ml0 + 768, 256, 256), (dsa0 + 256, 128, 128),
            (dsa0 + 640, 32, LANES), (dsa0 + 672, 8, LANES), (nsa0 + 640, 12, LANES),
            (nsa0 + 256, 128, 128), (nsa0 + 384, 128, 128), (nsa0 + 512, 128, 128),
            (ml0 + 1024, 8, LANES)]
    src = np.concatenate([np.concatenate([np.arange(a, a + n), np.full(width - n, -1)])
                          for a, n, width in segs]).astype(np.int32)
    assert src.shape == (IN_COLS_PACKED,)
    return src


PACK_TN = 256


def _pack_w_kernel(w_ref, src_ref, o_ref, wb_ref):
    @pl.when(pl.program_id(0) == 0)
    def _():
        wb_ref[...] = w_ref[...].astype(BF16)

    row = lax.broadcasted_iota(jnp.int32, (wb_ref.shape[1], PACK_TN), 0)
    sel = jnp.where(row == src_ref[...], 1.0, 0.0).astype(BF16)
    o_ref[...] = _dot(wb_ref[...], sel).astype(BF16)


def pack_w_in_bf16(w):
    d, n = w.shape
    n_pad = _round_up(n, LANES)
    w_pad = jnp.pad(w, ((0, 0), (0, n_pad - n)))
    src = jnp.asarray(_packed_source_columns()).reshape(1, IN_COLS_PACKED)
    return pl.pallas_call(
        _pack_w_kernel,
        out_shape=jax.ShapeDtypeStruct((d, IN_COLS_PACKED), BF16),
        grid=(IN_COLS_PACKED // PACK_TN,),
        in_specs=[_const_spec((d, n_pad)), pl.BlockSpec((1, PACK_TN), lambda j: (0, j))],
        out_specs=pl.BlockSpec((d, PACK_TN), lambda j: (0, j)),
        scratch_shapes=[pltpu.VMEM((d, n_pad), BF16)],
        compiler_params=pltpu.CompilerParams(
            dimension_semantics=("arbitrary",), vmem_limit_bytes=V7X_VMEM_LIMIT_BYTES),
        name="pack_w_in",
    )(w_pad, src)


def rope_tables(l_, d=HEAD_DIM):
    rd = d // 4
    half = rd // 2
    inv = ROPE_THETA ** (-jnp.arange(half, dtype=F32) * 2.0 / rd)
    ang = jnp.arange(l_).astype(F32)[:, None] * inv[None, :]
    cos, sin = lax.optimization_barrier((jnp.cos(ang), jnp.sin(ang)))
    zh = jnp.zeros((l_, half), F32)
    rest0 = jnp.zeros((l_, d - rd), F32)
    c = jnp.concatenate([cos, cos, rest0 + 1.0], axis=1)
    s1 = jnp.concatenate([-sin, zh, rest0], axis=1)
    s2 = jnp.concatenate([zh, sin, rest0], axis=1)
    return jnp.stack([c, s1, s2])


def rope_tables_pad(rt, width):
    n = width - rt.shape[-1]
    ident = jnp.stack([jnp.ones(rt.shape[1:2] + (n,), F32), jnp.zeros(rt.shape[1:2] + (n,), F32),
                       jnp.zeros(rt.shape[1:2] + (n,), F32)])
    return jnp.concatenate([rt, ident], axis=-1)


def rope_tables_kv_pair(rt):
    return rope_tables_pad(rt, 2 * rt.shape[-1])


def _apply_rope(t, rope_ref, half=HEAD_DIM // 8):
    w = t.shape[-1]
    return (t * rope_ref[0] + pltpu.roll(t, w - half, 1) * rope_ref[1]
            + pltpu.roll(t, half, 1) * rope_ref[2])


_NT = (((1,), (1,)), ((), ()))


def _dot_nt(a, b):
    return lax.dot_general(a, b, _NT, preferred_element_type=F32)


def _dot(a, b):
    return jnp.dot(a, b, preferred_element_type=F32)


NEG_BIG = -(2.0 ** 30)
NSA_KEY_TILE = 512
NSA_TQ = 128
NSA_SEL_TQ = 256
NSA_PREP_TK = 512


def _nsa_kv_prep_kernel(ps_ref, pw_ref, rope_ref, gs_ref, gw_ref, kaug_ref, vs_ref, kw_ref, vw_ref):
    tk = ps_ref.shape[0]
    lane = lax.broadcasted_iota(jnp.int32, (tk, LANES), 1)
    is_k = lane < HEAD_DIM

    def norm_rope(p, g):
        ms = jnp.sum(jnp.where(is_k, p * p, 0.0), axis=-1, keepdims=True) * (1.0 / HEAD_DIM)
        y = jnp.where(is_k, p * lax.rsqrt(ms + EPS) * g, p)
        return _apply_rope(y, rope_ref)

    ys = norm_rope(ps_ref[...], gs_ref[...])
    yw = norm_rope(pw_ref[...], gw_ref[...])
    row = pl.program_id(1) * tk + lax.broadcasted_iota(jnp.int32, (tk, LANES), 0)
    ind = jnp.where(jnp.right_shift(row, SLC_SHIFT) == (lane - HEAD_DIM), 1.0, 0.0)
    kaug_ref[0] = jnp.where(is_k, ys, ind).astype(BF16)
    vs_ref[0] = ys.T[HEAD_DIM:, :].astype(BF16)
    kw_ref[0] = yw[:, :HEAD_DIM].astype(BF16)
    vw_ref[0] = yw.T[HEAD_DIM:, :].astype(BF16)


def nsa_kv_prep(cols, b_, l_, rope_pair, g_slc, g_win):
    tk = NSA_PREP_TK
    nt = l_ // tk
    ones = jnp.ones((HEAD_DIM,), F32)
    gs = jnp.concatenate([g_slc, ones]).reshape(1, LANES)
    gw = jnp.concatenate([g_win, ones]).reshape(1, LANES)
    kv = lambda w: jax.ShapeDtypeStruct((b_, l_, w), BF16)
    kv_t = jax.ShapeDtypeStruct((b_, HEAD_DIM, l_), BF16)
    out_blk = lambda w: pl.BlockSpec((1, tk, w), lambda b, i: (b, i, 0))
    out_t = pl.BlockSpec((1, HEAD_DIM, tk), lambda b, i: (b, 0, i))
    return pl.pallas_call(
        _nsa_kv_prep_kernel,
        out_shape=(kv(LANES), kv_t, kv(HEAD_DIM), kv_t),
        grid=(b_, nt),
        in_specs=[
            pl.BlockSpec((tk, LANES), lambda b, i: (b * nt + i, C_NSA_SLC // LANES)),
            pl.BlockSpec((tk, LANES), lambda b, i: (b * nt + i, C_NSA_WIN // LANES)),
            pl.BlockSpec((3, tk, LANES), lambda b, i: (0, i, 0)),
            pl.BlockSpec((1, LANES), lambda b, i: (0, 0)),
            pl.BlockSpec((1, LANES), lambda b, i: (0, 0)),
        ],
        out_specs=(out_blk(LANES), out_t, out_blk(HEAD_DIM), out_t),
        compiler_params=pltpu.CompilerParams(
            dimension_semantics=("arbitrary", "arbitrary"),
            vmem_limit_bytes=V7X_VMEM_LIMIT_BYTES),
        name="nsa_kv_prep",
    )(cols, cols, rope_pair, gs, gw)


def _nsa_compress_kernel(r_ref, pea_ref, peb_ref, w1a_ref, w1b_ref, w2k_ref, w2v_ref, kg_ref,
                         kc_ref, vc_ref):
    r = r_ref[0]
    n = r.shape[0]
    a = _dot((r + pea_ref[...]).astype(BF16), w1a_ref[...])
    bm = _dot((r + peb_ref[...]).astype(BF16), w1b_ref[...])
    row = lax.broadcasted_iota(jnp.int32, bm.shape, 0)
    bm_up = jnp.where(row < n - 1, pltpu.roll(bm, n - 1, 0), 0.0)
    h = jnp.maximum(a + bm_up, 0.0).astype(BF16)
    hid = w2k_ref.shape[0]
    ck = _dot(h[:, :hid], w2k_ref[...])
    cv = _dot(h[:, hid:], w2v_ref[...])
    kc_ref[0] = _rms_rows(ck, kg_ref[...]).astype(BF16)
    vc_ref[0] = cv.astype(BF16)


def nsa_compress(cols, b_, l_, pos_k, pos_v, k_w1, k_w2, v_w1, v_w2, k_gain):
    rows = l_ // CMP_STRIDE
    hid = k_w1.shape[1]
    pair = cols[:, C_NSA_CMP:C_NSA_CMP + LANES].reshape(b_, rows, CMP_STRIDE * LANES)

    def interleave_pe(lo):
        pe = jnp.concatenate([pos_k[lo:lo + CMP_STRIDE], pos_v[lo:lo + CMP_STRIDE]], axis=1)
        return pe.reshape(1, CMP_STRIDE * LANES)

    def interleave_w(lo):
        wk = k_w1[lo * HEAD_DIM:(lo + CMP_STRIDE) * HEAD_DIM].reshape(CMP_STRIDE, HEAD_DIM, hid)
        wv = v_w1[lo * HEAD_DIM:(lo + CMP_STRIDE) * HEAD_DIM].reshape(CMP_STRIDE, HEAD_DIM, hid)
        z = jnp.zeros_like(wk)
        top = jnp.concatenate([wk, z], axis=2)
        bot = jnp.concatenate([z, wv], axis=2)
        return jnp.concatenate([top, bot], axis=1).reshape(CMP_STRIDE * LANES, 2 * hid).astype(BF16)

    out = jax.ShapeDtypeStruct((b_, rows, HEAD_DIM), BF16)
    return pl.pallas_call(
        _nsa_compress_kernel,
        out_shape=(out, out),
        grid=(b_,),
        in_specs=[
            pl.BlockSpec((1, rows, CMP_STRIDE * LANES), lambda b: (b, 0, 0)),
            _const_spec((1, CMP_STRIDE * LANES)),
            _const_spec((1, CMP_STRIDE * LANES)),
            _const_spec((CMP_STRIDE * LANES, 2 * hid)),
            _const_spec((CMP_STRIDE * LANES, 2 * hid)),
            _const_spec((hid, HEAD_DIM)),
            _const_spec((hid, HEAD_DIM)),
            _const_spec((1, HEAD_DIM)),
        ],
        out_specs=(pl.BlockSpec((1, rows, HEAD_DIM), lambda b: (b, 0, 0)),
                   pl.BlockSpec((1, rows, HEAD_DIM), lambda b: (b, 0, 0))),
        compiler_params=pltpu.CompilerParams(
            dimension_semantics=("arbitrary",), vmem_limit_bytes=V7X_VMEM_LIMIT_BYTES),
        name="nsa_compress",
    )(pair, interleave_pe(0), interleave_pe(CMP_STRIDE), interleave_w(0),
      interleave_w(CMP_STRIDE), k_w2.astype(BF16), v_w2.astype(BF16),
      k_gain.reshape(1, HEAD_DIM))


def _masked_softmax(s, valid, axis):
    m = jnp.max(jnp.where(valid, s, -jnp.inf), axis=axis, keepdims=True)
    m = jnp.where(m == -jnp.inf, 0.0, m)
    e = jnp.where(valid, jnp.exp(s - m), 0.0)
    den = jnp.sum(e, axis=axis, keepdims=True)
    return e / jnp.where(den > 0, den, 1.0)


def _nsa_select_kernel(q_ref, kc_ref, vc_ref, ovt_ref, rope_ref, qg_ref, qaug_ref, ocmp_ref,
                       *, n_sel):
    tq = q_ref.shape[0]
    ncr = kc_ref.shape[1]
    n_slc = ovt_ref.shape[0]
    t0 = pl.program_id(1) * tq
    scale = HEAD_DIM ** -0.5
    q = q_ref[...]
    g = qg_ref[...]
    kc = kc_ref[0]
    vc = vc_ref[0]
    last = CMP_BLOCK - 1
    vis = (lax.broadcasted_iota(jnp.int32, (tq, ncr), 1) * CMP_STRIDE + last
           <= t0 + lax.broadcasted_iota(jnp.int32, (tq, ncr), 0))
    vis_t = (lax.broadcasted_iota(jnp.int32, (ncr, tq), 0) * CMP_STRIDE + last
             <= t0 + lax.broadcasted_iota(jnp.int32, (ncr, tq), 1))
    qn_heads, o_heads = [], []
    psum_t = jnp.zeros((ncr, tq), F32)
    for h in range(GROUP_HEADS):
        qn = _rms_rows(q[:, h * HEAD_DIM:(h + 1) * HEAD_DIM], g)
        qn_heads.append(qn)
        qb = (qn * scale).astype(BF16)
        p = _masked_softmax(_dot_nt(qb, kc), vis, 1)
        o_heads.append(_dot(p.astype(BF16), vc))
        psum_t = psum_t + _masked_softmax(_dot_nt(kc, qb), vis_t, 0)
    ocmp_ref[0] = jnp.concatenate(o_heads, axis=-1)

    hi = psum_t.astype(BF16)
    lo = (psum_t - hi.astype(F32)).astype(BF16)
    imp = _dot(ovt_ref[...], hi) + _dot(ovt_ref[...], lo)
    blk = lax.broadcasted_iota(jnp.int32, (n_slc, tq), 0)
    cur = jnp.right_shift(t0 + lax.broadcasted_iota(jnp.int32, (n_slc, tq), 1), SLC_SHIFT)
    forced = (blk == 0) | (blk == cur) | (blk == cur - 1)
    imp = jnp.where(forced, jnp.inf, jnp.where(blk > cur, -jnp.inf, imp))
    n_grp = n_slc // SUBLANES
    groups = [imp[g * SUBLANES:(g + 1) * SUBLANES] for g in range(n_grp)]
    cnts = [jnp.zeros((SUBLANES, tq), F32) for _ in range(n_grp)]
    sub = lax.broadcasted_iota(jnp.int32, (SUBLANES, tq), 0)
    for m in range(n_slc):
        row = imp[m:m + 1, :]
        gm, rm = divmod(m, SUBLANES)
        for g in range(n_grp):
            if g < gm:
                beats = row > groups[g]
            elif g > gm:
                beats = row >= groups[g]
            else:
                beats = (row > groups[g]) | ((row == groups[g]) & (sub > rm))
            cnts[g] = cnts[g] + jnp.where(beats, 1.0, 0.0)
    cnt = jnp.concatenate(cnts, axis=0)
    mt = jnp.where(cnt < n_sel, 0.0, NEG_BIG)
    pad = jnp.zeros((HEAD_DIM - n_slc, tq), F32)
    mt = jnp.concatenate([mt, pad, mt, pad], axis=0) if n_slc < HEAD_DIM else jnp.concatenate(
        [mt, mt], axis=0)
    mt = mt.T

    qr = _apply_rope(jnp.concatenate(qn_heads, axis=-1), rope_ref) * scale
    lane = lax.broadcasted_iota(jnp.int32, (tq, LANES), 1)
    for j in range(GROUP_HEADS // 2):
        pair = qr[:, j * LANES:(j + 1) * LANES]
        swapped = pltpu.roll(pair, HEAD_DIM, 1)
        qaug_ref[0, :, (2 * j) * LANES:(2 * j + 1) * LANES] = jnp.where(
            lane < HEAD_DIM, pair, mt).astype(BF16)
        qaug_ref[0, :, (2 * j + 1) * LANES:(2 * j + 2) * LANES] = jnp.where(
            lane < HEAD_DIM, swapped, mt).astype(BF16)


def nsa_select(cols, b_, l_, k_cmp, v_cmp, rope_q, q_gain):
    tq = min(NSA_SEL_TQ, l_)
    nt = l_ // tq
    ncr = l_ // CMP_STRIDE
    n_slc = l_ // SLC_BLOCK
    n_sel = min(SLC_TOPN, n_slc)
    st_c = np.arange(ncr) * CMP_STRIDE
    st_s = np.arange(n_slc) * SLC_BLOCK
    ovt = ((st_c[None, :] < st_s[:, None] + SLC_BLOCK)
           & (st_c[None, :] + CMP_BLOCK > st_s[:, None])).astype(np.float32)
    return pl.pallas_call(
        functools.partial(_nsa_select_kernel, n_sel=n_sel),
        out_shape=(jax.ShapeDtypeStruct((b_, l_, GROUP_HEADS * LANES), BF16),
                   jax.ShapeDtypeStruct((b_, l_, GROUP_WIDTH), F32)),
        grid=(b_, nt),
        in_specs=[
            pl.BlockSpec((tq, GROUP_WIDTH), lambda b, i: (b * nt + i, C_NSA_Q // GROUP_WIDTH)),
            pl.BlockSpec((1, ncr, HEAD_DIM), lambda b, i: (b, 0, 0)),
            pl.BlockSpec((1, ncr, HEAD_DIM), lambda b, i: (b, 0, 0)),
            pl.BlockSpec((n_slc, ncr), lambda b, i: (0, 0)),
            pl.BlockSpec((3, tq, GROUP_WIDTH), lambda b, i: (0, i, 0)),
            pl.BlockSpec((1, HEAD_DIM), lambda b, i: (0, 0)),
        ],
        out_specs=(pl.BlockSpec((1, tq, GROUP_HEADS * LANES), lambda b, i: (b, i, 0)),
                   pl.BlockSpec((1, tq, GROUP_WIDTH), lambda b, i: (b, i, 0))),
        compiler_params=pltpu.CompilerParams(
            dimension_semantics=("arbitrary", "arbitrary"),
            vmem_limit_bytes=V7X_VMEM_LIMIT_BYTES),
        name="nsa_select",
    )(cols, k_cmp, v_cmp, jnp.asarray(ovt, BF16), rope_q, q_gain.reshape(1, HEAD_DIM))


def _softmax_stats_update(s, m, l):
    m_new = jnp.maximum(m, jnp.max(s, axis=0, keepdims=True))
    alpha = jnp.exp(m - m_new)
    p = jnp.exp(s - m_new)
    return m_new, alpha, p, alpha * l + jnp.sum(p, axis=0, keepdims=True)


def _nsa_attend_t_kernel(qaug_ref, kaug_ref, vst_ref, kw_ref, vwt_ref, ocmp_ref, gate_ref, o_ref):
    tq = qaug_ref.shape[1]
    nh = GROUP_HEADS
    ts = NSA_KEY_TILE
    t0 = pl.program_id(1) * tq
    qa = qaug_ref[0]
    qs = jnp.concatenate([qa[:, h * LANES:(h + 1) * LANES] for h in range(nh)], axis=0)
    n = nh * tq
    qpos_tile = t0 + lax.broadcasted_iota(jnp.int32, (1, tq), 1)
    qpos = jnp.concatenate([qpos_tile] * nh, axis=1)

    def key_tile(kt, carry):
        m, l, acc = carry
        k0 = pl.multiple_of(kt * ts, ts)
        s = _dot_nt(kaug_ref[0, pl.ds(k0, ts), :], qs)
        kpos = k0 + lax.broadcasted_iota(jnp.int32, (ts, n), 0)
        s = jnp.where(kpos <= qpos, s, NEG_BIG)
        m, alpha, p, l = _softmax_stats_update(s, m, l)
        acc = alpha * acc + _dot(vst_ref[0, :, pl.ds(k0, ts)], p.astype(BF16))
        return m, l, acc

    n_kt = (t0 + tq + ts - 1) // ts
    init = (jnp.full((1, n), -jnp.inf, F32), jnp.zeros((1, n), F32),
            jnp.zeros((HEAD_DIM, n), F32))
    _, l, acc = lax.fori_loop(0, n_kt, key_tile, init)
    o_slc = acc / l

    wlen = WINDOW + tq
    start = pl.multiple_of(jnp.maximum(t0 - WINDOW, 0), tq)
    sw = _dot_nt(kw_ref[0, pl.ds(start, wlen), :], qs[:, :HEAD_DIM])
    dist = qpos - (start + lax.broadcasted_iota(jnp.int32, (wlen, n), 0))
    sw = jnp.where((dist >= 0) & (dist < WINDOW), sw, -jnp.inf)
    e = jnp.exp(sw - jnp.max(sw, axis=0, keepdims=True))
    o_swa = _dot(vwt_ref[0, :, pl.ds(start, wlen)], e.astype(BF16)) / jnp.sum(
        e, axis=0, keepdims=True)

    g = jax.nn.sigmoid(gate_ref[...]).T
    oc = ocmp_ref[0].T
    outs = []
    for h in range(nh):
        cols_h = slice(h * tq, (h + 1) * tq)
        outs.append(g[h:h + 1, :] * oc[h * HEAD_DIM:(h + 1) * HEAD_DIM, :]
                    + g[nh + h:nh + h + 1, :] * o_slc[:, cols_h]
                    + g[2 * nh + h:2 * nh + h + 1, :] * o_swa[:, cols_h])
    o_ref[...] = jnp.concatenate(outs, axis=0).T


def nsa_attend(cols, b_, l_, q_aug, k_aug, v_slc, k_win, v_win, o_cmp):
    tq = NSA_TQ
    nt = l_ // tq
    seq = lambda w: pl.BlockSpec((1, l_, w), lambda b, i: (b, 0, 0))
    seq_t = pl.BlockSpec((1, HEAD_DIM, l_), lambda b, i: (b, 0, 0))
    return pl.pallas_call(
        _nsa_attend_t_kernel,
        out_shape=jax.ShapeDtypeStruct((b_ * l_, GROUP_WIDTH), F32),
        grid=(b_, nt),
        in_specs=[
            pl.BlockSpec((1, tq, GROUP_HEADS * LANES), lambda b, i: (b, i, 0)),
            seq(LANES), seq_t, seq(HEAD_DIM), seq_t,
            pl.BlockSpec((1, tq, GROUP_WIDTH), lambda b, i: (b, i, 0)),
            pl.BlockSpec((tq, LANES), lambda b, i: (b * nt + i, C_NSA_GATE // LANES)),
        ],
        out_specs=pl.BlockSpec((tq, GROUP_WIDTH), lambda b, i: (b * nt + i, 0)),
        compiler_params=pltpu.CompilerParams(
            dimension_semantics=("arbitrary", "arbitrary"),
            vmem_limit_bytes=V7X_VMEM_LIMIT_BYTES),
        name="nsa_attend",
    )(q_aug, k_aug, v_slc, k_win, v_win, o_cmp, cols)


def nsa_mixer_pallas(cols, b_, l_, rope, pos_k, pos_v, k_w1, k_w2, v_w1, v_w2, q_gain, k_gains):
    k_aug, v_slc, k_win, v_win = nsa_kv_prep(cols, b_, l_, rope_tables_kv_pair(rope),
                                             k_gains[1], k_gains[2])
    k_cmp, v_cmp = nsa_compress(cols, b_, l_, pos_k, pos_v, k_w1, k_w2, v_w1, v_w2, k_gains[0])
    q_aug, o_cmp = nsa_select(cols, b_, l_, k_cmp, v_cmp, jnp.tile(rope, (1, 1, GROUP_HEADS)),
                              q_gain)
    return nsa_attend(cols, b_, l_, q_aug, k_aug, v_slc, k_win, v_win, o_cmp)


DSA_TQ = 128
DSA_KEY_TILE = 512
DSA_PREP_TK = 512
IDX_PACK = LANES
INT_MIN = -2 ** 31
MASKED_SCORE = -1e30


def _split_hi_lo(t):
    hi = t.astype(BF16)
    lo = (t - hi.astype(F32)).astype(BF16)
    return hi, lo


def _placement(rows, cols, pairs):
    p = np.zeros((rows, cols), np.float32)
    for r, c in pairs:
        p[r, c] = 1.0
    return jnp.asarray(p, BF16)


def _dsa_kv_prep_kernel(ckv_ref, ik_ref, rope_ref, ropei_ref, kvg_ref, wkv_ref, kg_ref, ikg_ref,
                        pkh_ref, pkl_ref, k_ref, vt_ref, ik3_ref):
    tk = ckv_ref.shape[0]
    lane = lax.broadcasted_iota(jnp.int32, (tk, LANES), 1)
    ckv = _rms_rows(ckv_ref[...], kvg_ref[...]).astype(BF16)
    kv = _dot(ckv, wkv_ref[...])
    is_k = lane < HEAD_DIM
    ms = jnp.sum(jnp.where(is_k, kv * kv, 0.0), axis=-1, keepdims=True) * (1.0 / HEAD_DIM)
    y = _apply_rope(jnp.where(is_k, kv * lax.rsqrt(ms + EPS) * kg_ref[...], kv), rope_ref)
    k_ref[0] = y[:, :HEAD_DIM].astype(BF16)
    vt_ref[0] = y.T[HEAD_DIM:, :].astype(BF16)
    ik = ik_ref[...]
    ms = jnp.sum(ik * ik, axis=-1, keepdims=True) * (1.0 / IDX_DIM)
    ikn = _apply_rope(ik * lax.rsqrt(ms + EPS) * ikg_ref[...], ropei_ref, IDX_DIM // 8)
    hi, lo = _split_hi_lo(ikn)
    ik3_ref[0] = (_dot(hi, pkh_ref[...]) + _dot(lo, pkl_ref[...])).astype(BF16)


def dsa_kv_prep(cols, b_, l_, rope_pair, rope_idx, kv_gain, w_uk, w_uv, k_gain, idxk_gain):
    tk = DSA_PREP_TK
    nt = l_ // tk
    ones = jnp.ones((HEAD_DIM,), F32)
    kg = jnp.concatenate([k_gain, ones]).reshape(1, LANES)
    ikg = jnp.pad(idxk_gain, (0, LANES - IDX_DIM)).reshape(1, LANES)
    wkv = jnp.concatenate([w_uk, w_uv], axis=1).astype(BF16)
    d = range(IDX_DIM)
    pkh = _placement(LANES, IDX_PACK, [(i, i) for i in d] + [(i, 2 * IDX_DIM + i) for i in d])
    pkl = _placement(LANES, IDX_PACK, [(i, IDX_DIM + i) for i in d])
    const = lambda shape: pl.BlockSpec(shape, lambda b, i: (0,) * len(shape))
    out = lambda w: pl.BlockSpec((1, tk, w), lambda b, i: (b, i, 0))
    return pl.pallas_call(
        _dsa_kv_prep_kernel,
        out_shape=(jax.ShapeDtypeStruct((b_, l_, HEAD_DIM), BF16),
                   jax.ShapeDtypeStruct((b_, HEAD_DIM, l_), BF16),
                   jax.ShapeDtypeStruct((b_, l_, IDX_PACK), BF16)),
        grid=(b_, nt),
        in_specs=[
            pl.BlockSpec((tk, LANES), lambda b, i: (b * nt + i, C_DSA_CKV // LANES)),
            pl.BlockSpec((tk, LANES), lambda b, i: (b * nt + i, C_DSA_IK // LANES)),
            pl.BlockSpec((3, tk, LANES), lambda b, i: (0, i, 0)),
            pl.BlockSpec((3, tk, LANES), lambda b, i: (0, i, 0)),
            const((1, LANES)), const((DSA_LATENT, LANES)), const((1, LANES)), const((1, LANES)),
            const((LANES, IDX_PACK)), const((LANES, IDX_PACK)),
        ],
        out_specs=(out(HEAD_DIM), pl.BlockSpec((1, HEAD_DIM, tk), lambda b, i: (b, 0, i)),
                   out(IDX_PACK)),
        compiler_params=pltpu.CompilerParams(
            dimension_semantics=("arbitrary", "arbitrary"),
            vmem_limit_bytes=V7X_VMEM_LIMIT_BYTES),
        name="dsa_kv_prep",
    )(cols, cols, rope_pair, rope_idx, kv_gain.reshape(1, DSA_LATENT), wkv, kg, ikg, pkh, pkl)


def _dsa_q_prep_kernel(q_ref, iq_ref, rope_ref, ropei_ref, qg_ref, pqh_ref, pql_ref,
                       qh_ref, iq3_ref):
    g = qg_ref[...]
    q = q_ref[...]
    qn = jnp.concatenate([_rms_rows(q[:, h * HEAD_DIM:(h + 1) * HEAD_DIM], g)
                          for h in range(GROUP_HEADS)], axis=-1)
    qr = _apply_rope(qn, rope_ref) * (HEAD_DIM ** -0.5)
    for h in range(GROUP_HEADS):
        qh_ref[0, h] = qr[:, h * HEAD_DIM:(h + 1) * HEAD_DIM].astype(BF16)
    hi, lo = _split_hi_lo(_apply_rope(iq_ref[...], ropei_ref, IDX_DIM // 8))
    iq3_ref[0] = (_dot(hi, pqh_ref[...]) + _dot(lo, pql_ref[...])).astype(BF16)


def dsa_q_prep(cols, b_, l_, rope_q, rope_iq, q_gain):
    tq = 256
    nt = l_ // tq
    w = IDX_HEADS * IDX_DIM
    hd = [(h, i) for h in range(IDX_HEADS) for i in range(IDX_DIM)]
    pqh = _placement(w, IDX_HEADS * IDX_PACK,
                     [(IDX_DIM * h + i, IDX_PACK * h + i) for h, i in hd]
                     + [(IDX_DIM * h + i, IDX_PACK * h + IDX_DIM + i) for h, i in hd])
    pql = _placement(w, IDX_HEADS * IDX_PACK,
                     [(IDX_DIM * h + i, IDX_PACK * h + 2 * IDX_DIM + i) for h, i in hd])
    const = lambda shape: pl.BlockSpec(shape, lambda b, i: (0,) * len(shape))
    return pl.pallas_call(
        _dsa_q_prep_kernel,
        out_shape=(jax.ShapeDtypeStruct((b_, GROUP_HEADS, l_, HEAD_DIM), BF16),
                   jax.ShapeDtypeStruct((b_, l_, IDX_HEADS * IDX_PACK), BF16)),
        grid=(b_, nt),
        in_specs=[
            pl.BlockSpec((tq, GROUP_WIDTH), lambda b, i: (b * nt + i, C_DSA_Q // GROUP_WIDTH)),
            pl.BlockSpec((tq, w), lambda b, i: (b * nt + i, C_DSA_IQ // w)),
            pl.BlockSpec((3, tq, GROUP_WIDTH), lambda b, i: (0, i, 0)),
            pl.BlockSpec((3, tq, w), lambda b, i: (0, i, 0)),
            const((1, HEAD_DIM)), const(pqh.shape), const(pql.shape),
        ],
        out_specs=(pl.BlockSpec((1, GROUP_HEADS, tq, HEAD_DIM), lambda b, i: (b, 0, i, 0)),
                   pl.BlockSpec((1, tq, IDX_HEADS * IDX_PACK), lambda b, i: (b, i, 0))),
        compiler_params=pltpu.CompilerParams(
            dimension_semantics=("arbitrary", "arbitrary"),
            vmem_limit_bytes=V7X_VMEM_LIMIT_BYTES),
        name="dsa_q_prep",
    )(cols, cols, rope_q, rope_iq, q_gain.reshape(1, HEAD_DIM), pqh, pql)


SUBLANES = 8


def _fold_rows(t, op, group=SUBLANES):
    parts = [t[i * group:(i + 1) * group] for i in range(t.shape[0] // group)]
    while len(parts) > 1:
        parts = [op(parts[i], parts[i + 1]) if i + 1 < len(parts) else parts[i]
                 for i in range(0, len(parts), 2)]
    return parts[0]


def _dsa_attend_t_kernel(qh_ref, iq3_ref, iw_ref, k_ref, vt_ref, ik3_ref, o_ref, sc_ref, *,
                         topk, idx_bits):
    tq = iq3_ref.shape[1]
    ts = DSA_KEY_TILE
    nh = GROUP_HEADS
    t0 = pl.program_id(1) * tq
    n_kt = (t0 + tq + ts - 1) // ts
    qpos = t0 + lax.broadcasted_iota(jnp.int32, (ts, tq), 1)
    krow = lax.broadcasted_iota(jnp.int32, (ts, tq), 0)

    iq3 = iq3_ref[0]
    iq_rows = jnp.concatenate([iq3[:, h * IDX_PACK:(h + 1) * IDX_PACK]
                               for h in range(IDX_HEADS)], axis=0)
    iw_t = (iw_ref[...] * (IDX_HEADS ** -0.5 * IDX_DIM ** -0.5)).T

    def score_tile(kt, _):
        k0 = pl.multiple_of(kt * ts, ts)
        rel = jnp.maximum(_dot_nt(ik3_ref[0, pl.ds(k0, ts), :], iq_rows), 0.0)
        sc = rel[:, 0:tq] * iw_t[0:1, :]
        for h in range(1, IDX_HEADS):
            sc = sc + rel[:, h * tq:(h + 1) * tq] * iw_t[h:h + 1, :]
        sc_ref[pl.ds(k0, ts), :] = jnp.where(k0 + krow <= qpos, sc, -jnp.inf)
        return 0

    lax.fori_loop(0, n_kt, score_tile, 0)

    def count(pred):
        def tile(kt, c):
            k0 = pl.multiple_of(kt * ts, ts)
            hit = jnp.where(pred(sc_ref[pl.ds(k0, ts), :], k0), 1.0, 0.0)
            return c + _fold_rows(hit, jnp.add)
        c = lax.fori_loop(0, n_kt, tile, jnp.zeros((SUBLANES, tq), F32))
        return jnp.sum(c, axis=0, keepdims=True)

    def key_to_float(key):
        return pltpu.bitcast(jnp.where(key >= 0, key, key ^ jnp.int32(0x7FFFFFFF)), F32)

    def value_bit(i, thr_key):
        cand = thr_key | jnp.left_shift(jnp.int32(1), 31 - i)
        cand_f = key_to_float(cand ^ jnp.int32(INT_MIN))
        return jnp.where(count(lambda sc, k0: sc >= cand_f) >= topk, cand, thr_key)

    thr_key = lax.fori_loop(0, 32, value_bit, jnp.zeros((1, tq), jnp.int32))
    few = t0 + lax.broadcasted_iota(jnp.int32, (1, tq), 1) + 1 < topk
    thr = jnp.where(few, -jnp.inf, key_to_float(thr_key ^ jnp.int32(INT_MIN)))
    n_ge = count(lambda sc, k0: sc >= thr)
    tie_break = jnp.max(jnp.where((n_ge > topk) & jnp.logical_not(few), 1.0, 0.0)) > 0.0

    def last_tie_position():
        need = topk - count(lambda sc, k0: sc > thr)

        def index_bit(i, last):
            cand = last | jnp.left_shift(jnp.int32(1), idx_bits - 1 - i)
            tied_below = lambda sc, k0: (sc == thr) & (k0 + krow < cand)
            return jnp.where(count(tied_below) < need, cand, last)

        return lax.fori_loop(0, idx_bits, index_bit, jnp.zeros((1, tq), jnp.int32))

    last = lax.cond(tie_break, last_tie_position,
                    lambda: jnp.full((1, tq), 2 ** idx_bits, jnp.int32))

    qs = qh_ref[0].reshape(nh * tq, HEAD_DIM)

    def key_tile(kt, carry):
        m, l, acc = carry
        k0 = pl.multiple_of(kt * ts, ts)
        sc = sc_ref[pl.ds(k0, ts), :]
        kpos = k0 + krow
        sel = ((sc > thr) | ((sc == thr) & (kpos <= last))) & (kpos <= qpos)
        bias = jnp.where(sel, 0.0, MASKED_SCORE)
        s = _dot_nt(k_ref[0, pl.ds(k0, ts), :], qs) + jnp.concatenate([bias] * nh, axis=1)
        m, alpha, p, l = _softmax_stats_update(s, m, l)
        acc = alpha * acc + _dot(vt_ref[0, :, pl.ds(k0, ts)], p.astype(BF16))
        return m, l, acc

    init = (jnp.full((1, nh * tq), MASKED_SCORE, F32), jnp.zeros((1, nh * tq), F32),
            jnp.zeros((HEAD_DIM, nh * tq), F32))
    _, l, acc = lax.fori_loop(0, n_kt, key_tile, init)
    o_t = acc / l
    o_ref[...] = jnp.concatenate([o_t[:, h * tq:(h + 1) * tq] for h in range(nh)], axis=0).T


def dsa_attend(cols, b_, l_, qh, iq3, k, v, ik3):
    tq = DSA_TQ
    nt = l_ // tq
    topk = min(DSA_TOPK_MAX, l_ // 4)
    idx_bits = int(np.log2(l_))
    assert 2 ** idx_bits == l_ and l_ % DSA_KEY_TILE == 0 and topk <= DSA_KEY_TILE
    seq = lambda w: pl.BlockSpec((1, l_, w), lambda b, i: (b, 0, 0))
    return pl.pallas_call(
        functools.partial(_dsa_attend_t_kernel, topk=topk, idx_bits=idx_bits),
        out_shape=jax.ShapeDtypeStruct((b_ * l_, GROUP_WIDTH), F32),
        grid=(b_, nt),
        in_specs=[
            pl.BlockSpec((1, GROUP_HEADS, tq, HEAD_DIM), lambda b, i: (b, 0, i, 0)),
            pl.BlockSpec((1, tq, IDX_HEADS * IDX_PACK), lambda b, i: (b, i, 0)),
            pl.BlockSpec((tq, LANES), lambda b, i: (b * nt + i, C_DSA_IW // LANES)),
            seq(HEAD_DIM), pl.BlockSpec((1, HEAD_DIM, l_), lambda b, i: (b, 0, 0)), seq(IDX_PACK),
        ],
        out_specs=pl.BlockSpec((tq, GROUP_WIDTH), lambda b, i: (b * nt + i, 0)),
        scratch_shapes=[pltpu.VMEM((l_, tq), F32)],
        compiler_params=pltpu.CompilerParams(
            dimension_semantics=("arbitrary", "arbitrary"),
            vmem_limit_bytes=V7X_VMEM_LIMIT_BYTES),
        name="dsa_attend",
    )(qh, iq3, cols, k, v, ik3)


def dsa_mixer_pallas(cols, b_, l_, rope, kv_gain, w_uk, w_uv, q_gain, k_gain, idxk_gain):
    rope_i = rope_tables(l_, IDX_DIM)
    k, v, ik3 = dsa_kv_prep(cols, b_, l_, rope_tables_kv_pair(rope), rope_tables_pad(rope_i, LANES),
                            kv_gain, w_uk, w_uv, k_gain, idxk_gain)
    qh, iq3 = dsa_q_prep(cols, b_, l_, jnp.tile(rope, (1, 1, GROUP_HEADS)),
                         jnp.tile(rope_i, (1, 1, IDX_HEADS)), q_gain)
    return dsa_attend(cols, b_, l_, qh, iq3, k, v, ik3)


_TN = (((0,), (0,)), ((), ()))


def _dot_tn(a, b):
    return lax.dot_general(a, b, _TN, preferred_element_type=F32)


def _split3(t):
    hi = t.astype(BF16)
    r = t - hi.astype(F32)
    mid = r.astype(BF16)
    lo = (r - mid.astype(F32)).astype(BF16)
    return hi, mid, lo


def _tri_cumsum(tri, t):
    hi, mid, lo = _split3(t)
    return _dot(tri, hi) + _dot(tri, mid) + _dot(tri, lo)


def _cumsum_tri_rows(t, tri_u):
    hi, mid, lo = _split3(t)
    return _dot(hi, tri_u) + _dot(mid, tri_u) + _dot(lo, tri_u)


def _head_rms(o, gain):
    return jnp.concatenate([_rms_rows(o[:, h * HEAD_DIM:(h + 1) * HEAD_DIM], gain)
                            for h in range(GROUP_HEADS)], axis=-1)


HG_SUB = 8
LOG2_E = 1.4426950408889634


def _hgrn2_kernel(q_ref, f_ref, i_ref, g_ref, lb_ref, gain_ref, tri_ref, ones_ref, bd_ref,
                  o_ref, st_ref, b_ref, kk_ref, v_ref):
    c = HG_CHUNK
    w = GROUP_WIDTH

    @pl.when(pl.program_id(1) == 0)
    def _():
        st_ref[...] = jnp.zeros_like(st_ref)

    lb = lb_ref[...]
    q = q_ref[...]
    qs = q * jax.nn.sigmoid(q) * (HEAD_DIM ** -0.5)
    forget = lb + (1.0 - lb) * jax.nn.sigmoid(f_ref[...])
    kk = 1.0 - forget
    bcum = _tri_cumsum(tri_ref[...], jnp.log(forget))
    v = i_ref[...]
    b2 = bcum * LOG2_E
    b_ref[...] = b2
    kk_ref[...] = kk
    v_ref[...] = v

    out = _dot_nt((qs * jnp.exp(bcum)).astype(BF16), st_ref[...].astype(BF16))

    ones_bd = ones_ref[...]
    pieces = []
    for g in range(c // HG_SUB):
        r0 = g * HG_SUB
        nr = c - r0
        qg = qs[r0:, :]
        bg = b2[r0:, :]
        trow = r0 + lax.broadcasted_iota(jnp.int32, (HG_SUB, w), 0)
        terms = []
        for j in range(HG_SUB):
            s = r0 + j
            d = qg * kk_ref[s:s + 1, :] * jnp.exp2(bg - b_ref[s:s + 1, :])
            if j > 0:
                head = jnp.where(trow >= s, d[:HG_SUB], 0.0)
                d = jnp.concatenate([head, d[HG_SUB:]], axis=0) if nr > HG_SUB else head
            terms.append(d.astype(BF16))
        red = _dot(jnp.concatenate(terms, axis=0), ones_bd)
        acc = red[0:nr] * v_ref[r0:r0 + 1, :]
        for j in range(1, HG_SUB):
            acc = acc + red[j * nr:(j + 1) * nr] * v_ref[r0 + j:r0 + j + 1, :]
        pieces.append(acc)
    intra = pieces[0]
    for g in range(1, c // HG_SUB):
        pad = jnp.zeros((g * HG_SUB, w), F32)
        intra = intra + jnp.concatenate([pad, pieces[g]], axis=0)
    out = out + intra

    b_last = bcum[c - 1:c, :]
    kt = (kk * jnp.exp(b_last - bcum)).astype(BF16)
    st_ref[...] = jnp.exp(b_last) * st_ref[...] + _dot_tn(v.astype(BF16), kt) * bd_ref[...]

    g_in = g_ref[...]
    o_ref[...] = _head_rms(out, gain_ref[...]) * (g_in * jax.nn.sigmoid(g_in))


def hgrn2_mixer_pallas(cols, b_, l_, lb, o_gain):
    c = HG_CHUNK
    nt = l_ // c
    w = GROUP_WIDTH
    head = np.arange(w) // HEAD_DIM
    same = (head[:, None] == head[None, :]).astype(np.float32)
    tri = np.tril(np.ones((c, c), np.float32))
    col = lambda j: pl.BlockSpec((c, w), lambda b, i: (b * nt + i, C_HG // w + j))
    const = lambda shape: pl.BlockSpec(shape, lambda b, i: (0,) * len(shape))
    return pl.pallas_call(
        _hgrn2_kernel,
        out_shape=jax.ShapeDtypeStruct((b_ * l_, w), F32),
        grid=(b_, nt),
        in_specs=[col(0), col(1), col(2), col(3), const((1, w)), const((1, HEAD_DIM)),
                  const((c, c)), const((w, w)), const((w, w))],
        out_specs=pl.BlockSpec((c, w), lambda b, i: (b * nt + i, 0)),
        scratch_shapes=[pltpu.VMEM((w, w), F32), pltpu.VMEM((c, w), F32),
                        pltpu.VMEM((c, w), F32), pltpu.VMEM((c, w), F32)],
        compiler_params=pltpu.CompilerParams(
            dimension_semantics=("arbitrary", "arbitrary"),
            vmem_limit_bytes=V7X_VMEM_LIMIT_BYTES),
        name="hgrn2",
    )(cols, cols, cols, cols, lb.reshape(1, w), o_gain.reshape(1, HEAD_DIM),
      jnp.asarray(tri, BF16), jnp.asarray(same, BF16), jnp.asarray(same, F32))


ML_TC = 256
ML_M_INIT = -1e30


def _mlstm_kernel(gate_ref, qk_ref, v_ref, og_ref, cw_ref, cb_ref, gb_ref, gain_ref, tril_ref,
                  triu_ref, o_ref, xprev_ref, cmat_ref, nvec_ref, m_ref):
    c = ML_TC
    nh = GROUP_HEADS
    w = GROUP_WIDTH

    @pl.when(pl.program_id(1) == 0)
    def _():
        xprev_ref[...] = jnp.zeros_like(xprev_ref)
        cmat_ref[...] = jnp.zeros_like(cmat_ref)
        nvec_ref[...] = jnp.zeros_like(nvec_ref)
        m_ref[...] = jnp.full(m_ref.shape, ML_M_INIT, F32)

    x = qk_ref[...]
    prev = xprev_ref[...]
    row = lax.broadcasted_iota(jnp.int32, x.shape, 0)
    acc = x * cw_ref[CONV_WIDTH - 1:CONV_WIDTH, :] + cb_ref[...]
    for j in range(1, CONV_WIDTH):
        shifted = jnp.where(row < j, pltpu.roll(prev, j, 0), pltpu.roll(x, j, 0))
        acc = acc + shifted * cw_ref[CONV_WIDTH - 1 - j:CONV_WIDTH - j, :]
    xprev_ref[...] = x
    qk = acc * jax.nn.sigmoid(acc)
    q = qk[:, :w]
    k = qk[:, w:] * (HEAD_DIM ** -0.5)
    v = v_ref[...]

    pre = gate_ref[...] + gb_ref[...]
    lane = lax.broadcasted_iota(jnp.int32, pre.shape, 1)
    log_f = jnp.minimum(pre, 0.0) - jnp.log1p(jnp.exp(-jnp.abs(pre)))
    log_f = jnp.where((lane >= nh) & (lane < 2 * nh), log_f, 0.0)
    bcum_c = _tri_cumsum(tril_ref[...], log_f)
    bcum_r = _cumsum_tri_rows(log_f.T, triu_ref[...])
    pre_r = pre.T
    tri = (lax.broadcasted_iota(jnp.int32, (c, c), 0) >= lax.broadcasted_iota(jnp.int32, (c, c), 1))

    outs = []
    for h in range(nh):
        sl = slice(h * HEAD_DIM, (h + 1) * HEAD_DIM)
        qh, kh, vh = q[:, sl], k[:, sl], v[:, sl]
        bc = bcum_c[:, nh + h:nh + h + 1]
        li_c = pre[:, h:h + 1]
        a_r = pre_r[h:h + 1, :] - bcum_r[nh + h:nh + h + 1, :]
        m_prev = m_ref[h:h + 1, 0:1]
        log_d = jnp.where(tri, bc + a_r, -jnp.inf)
        inter = bc + m_prev
        m_t = jnp.maximum(inter, jnp.max(log_d, axis=-1, keepdims=True))
        d_mat = jnp.exp(log_d - m_t)
        w_inter = jnp.exp(inter - m_t)
        qb = qh.astype(BF16)
        s = _dot_nt(qb, kh.astype(BF16)) * d_mat
        num = w_inter * _dot(qb, cmat_ref[h].astype(BF16)) + _dot(s.astype(BF16), vh.astype(BF16))
        den = (w_inter * jnp.sum(qh * nvec_ref[h], axis=-1, keepdims=True)
               + jnp.sum(s, axis=-1, keepdims=True))
        outs.append(num / jnp.maximum(jnp.abs(den), jnp.exp(-m_t)))
        b_last = bc[c - 1:c, :]
        log_w = b_last + (li_c - bc)
        m_new = jnp.maximum(b_last + m_prev, jnp.max(log_w, axis=0, keepdims=True))
        kw = kh * jnp.exp(log_w - m_new)
        decay = jnp.exp(b_last + m_prev - m_new)
        cmat_ref[h] = decay * cmat_ref[h] + _dot_tn(kw.astype(BF16), vh.astype(BF16))
        nvec_ref[h] = decay * nvec_ref[h] + jnp.sum(kw, axis=0, keepdims=True)
        m_ref[h:h + 1, :] = jnp.broadcast_to(m_new, (1, LANES))

    hh = _head_rms(jnp.concatenate(outs, axis=-1), gain_ref[...])
    o_ref[...] = hh * jax.nn.sigmoid(og_ref[...])


def mlstm_mixer_pallas(cols, b_, l_, conv_w, conv_b, i_bias, f_bias, o_gain):
    c = ML_TC
    nt = l_ // c
    w = GROUP_WIDTH
    gb = jnp.pad(jnp.concatenate([i_bias, f_bias]), (0, LANES - 2 * GROUP_HEADS)).reshape(1, LANES)
    tril = np.tril(np.ones((c, c), np.float32))
    const = lambda shape: pl.BlockSpec(shape, lambda b, i: (0,) * len(shape))
    blk = lambda width, off: pl.BlockSpec((c, width), lambda b, i: (b * nt + i, off // width))
    return pl.pallas_call(
        _mlstm_kernel,
        out_shape=jax.ShapeDtypeStruct((b_ * l_, w), F32),
        grid=(b_, nt),
        in_specs=[blk(LANES, C_ML_GATE), blk(2 * w, C_ML_QK), blk(w, C_ML_V), blk(w, C_ML_OG),
                  const((CONV_WIDTH, 2 * w)), const((1, 2 * w)), const((1, LANES)),
                  const((1, HEAD_DIM)), const((c, c)), const((c, c))],
        out_specs=pl.BlockSpec((c, w), lambda b, i: (b * nt + i, 0)),
        scratch_shapes=[pltpu.VMEM((c, 2 * w), F32),
                        pltpu.VMEM((GROUP_HEADS, HEAD_DIM, HEAD_DIM), F32),
                        pltpu.VMEM((GROUP_HEADS, 1, HEAD_DIM), F32),
                        pltpu.VMEM((8, LANES), F32)],
        compiler_params=pltpu.CompilerParams(
            dimension_semantics=("arbitrary", "arbitrary"),
            vmem_limit_bytes=V7X_VMEM_LIMIT_BYTES),
        name="mlstm",
    )(cols, cols, cols, cols, conv_w, conv_b.reshape(1, 2 * w), gb, o_gain.reshape(1, HEAD_DIM),
      jnp.asarray(tril, BF16), jnp.asarray(tril.T, BF16))


def kernel(x, mem, lb_param, norm_mix, w_in, w_out, hg_o_gain, dsa_kv_gain, dsa_w_uk, dsa_w_uv,
           dsa_q_gain, dsa_k_gain, dsa_idxk_gain, nsa_pos_k, nsa_pos_v, nsa_k_w1, nsa_k_w2,
           nsa_v_w1, nsa_v_w2, nsa_q_gain, nsa_k_gains, ml_conv_w, ml_conv_b, ml_i_bias,
           ml_f_bias, ml_o_gain, norm_xa, norm_mem, xa_wq, xa_wkv, xa_wo, xa_q_gain, xa_k_gain,
           norm_ffn, ffn_w13, ffn_w2):
    b_, l_, d = x.shape
    lb_all = jnp.cumsum(jax.nn.softmax(lb_param.astype(F32), axis=0), axis=0)
    lb_all = lb_all - lb_all[:1]
    x2d = x.reshape(b_ * l_, d)
    mem2d = mem.reshape(b_ * N_MEM, d)
    rope = rope_tables(l_)
    for l in range(DEPTH):
        cols = norm_matmul(x2d, norm_mix[l], pack_w_in_bf16(w_in[l]), tm=256)
        mixers = (
            hgrn2_mixer_pallas(cols, b_, l_, lb_all[l], hg_o_gain[l]),
            dsa_mixer_pallas(cols, b_, l_, rope, dsa_kv_gain[l], dsa_w_uk[l], dsa_w_uv[l],
                             dsa_q_gain[l], dsa_k_gain[l], dsa_idxk_gain[l]),
            nsa_mixer_pallas(cols, b_, l_, rope, nsa_pos_k[l], nsa_pos_v[l], nsa_k_w1[l],
                             nsa_k_w2[l], nsa_v_w1[l], nsa_v_w2[l], nsa_q_gain[l],
                             nsa_k_gains[l]),
            mlstm_mixer_pallas(cols, b_, l_, ml_conv_w[l], ml_conv_b[l], ml_i_bias[l],
                               ml_f_bias[l], ml_o_gain[l]),
        )
        k_mem, v_mem = mem_kv(mem2d, norm_mem[l], xa_wkv[l].astype(BF16), xa_k_gain[l])
        x2d = post_block(x2d, mixers, l_, w_out[l].astype(BF16), norm_xa[l],
                         xa_wq[l].astype(BF16), xa_q_gain[l], k_mem, v_mem,
                         xa_wo[l].astype(BF16), norm_ffn[l], ffn_w13[l].astype(BF16),
                         ffn_w2[l].astype(BF16), tm=512)
    return x2d.reshape(b_, l_, d)
```

```python
import functools

import jax
import jax.numpy as jnp
from jax import lax
import numpy as np
from jax.experimental import pallas as pl
from jax.experimental.pallas import tpu as pltpu

F32 = jnp.float32
BF16 = jnp.bfloat16

D_MODEL = 1024
DEPTH = 2
HEAD_DIM = 64
GROUP_HEADS = 4
GROUP_WIDTH = GROUP_HEADS * HEAD_DIM
ROPE_THETA = 500000.0
EPS = 1e-6
N_MEM = 256
XA_HEADS = 4
XA_WIDTH = XA_HEADS * HEAD_DIM
HG_CHUNK = 64
DSA_LATENT = 128
IDX_HEADS = 8
IDX_DIM = 32
DSA_TOPK_MAX = 256
CMP_BLOCK = 32
CMP_STRIDE = 16
SLC_BLOCK = 64
SLC_SHIFT = 6
SLC_TOPN = 16
WINDOW = 512
CONV_WIDTH = 4
D_FF = 2816

HG_SPLITS = (GROUP_WIDTH,) * 4
DSA_SPLITS = (GROUP_WIDTH, DSA_LATENT, IDX_HEADS * IDX_DIM, IDX_DIM, IDX_HEADS)
NSA_SPLITS = (GROUP_WIDTH,) + (HEAD_DIM,) * 6 + (3 * GROUP_HEADS,)
ML_SPLITS = (2 * GROUP_WIDTH, GROUP_WIDTH, GROUP_WIDTH, GROUP_HEADS, GROUP_HEADS)
GROUP_COLS = (sum(HG_SPLITS), sum(DSA_SPLITS), sum(NSA_SPLITS), sum(ML_SPLITS))
IN_COLS = sum(GROUP_COLS)

V7X_VMEM_LIMIT_BYTES = 56 * 1024 * 1024
LANES = 128
FF_CHUNK = 256


def _round_up(n, m):
    return -(-n // m) * m


def _rms_rows(t, g):
    return t * lax.rsqrt(jnp.mean(t * t, axis=-1, keepdims=True) + EPS) * g


def _const_spec(shape):
    return pl.BlockSpec(shape, lambda *_: (0,) * len(shape), pipeline_mode=pl.Buffered(1))


def _norm_matmul_kernel(x_ref, g_ref, w_ref, o_ref):
    h = _rms_rows(x_ref[...], g_ref[...]).astype(BF16)
    o_ref[...] = jnp.dot(h, w_ref[...], preferred_element_type=F32)


def norm_matmul(x2d, gain, w_bf16, tm):
    m, k = x2d.shape
    n = w_bf16.shape[1]
    return pl.pallas_call(
        _norm_matmul_kernel,
        out_shape=jax.ShapeDtypeStruct((m, n), F32),
        grid=(m // tm,),
        in_specs=[
            pl.BlockSpec((tm, k), lambda i: (i, 0)),
            _const_spec((1, k)),
            _const_spec((k, n)),
        ],
        out_specs=pl.BlockSpec((tm, n), lambda i: (i, 0)),
        compiler_params=pltpu.CompilerParams(
            dimension_semantics=("arbitrary",), vmem_limit_bytes=V7X_VMEM_LIMIT_BYTES),
        name="norm_matmul",
    )(x2d, gain.reshape(1, k), w_bf16)


def _mem_kv_kernel(m_ref, g_ref, w_ref, kg_ref, k_ref, v_ref):
    mn = _rms_rows(m_ref[...], g_ref[...]).astype(BF16)
    kv = jnp.dot(mn, w_ref[...], preferred_element_type=F32)
    kg = kg_ref[...]
    ks = []
    for h in range(XA_HEADS):
        kh = kv[:, h * HEAD_DIM:(h + 1) * HEAD_DIM]
        ks.append(_rms_rows(kh, kg))
    k_ref[...] = jnp.concatenate(ks, axis=-1).astype(BF16)
    v_ref[...] = kv[:, XA_WIDTH:].astype(BF16)


def mem_kv(mem2d, gain, wkv_bf16, k_gain):
    m, k = mem2d.shape
    return pl.pallas_call(
        _mem_kv_kernel,
        out_shape=(jax.ShapeDtypeStruct((m, XA_WIDTH), BF16),
                   jax.ShapeDtypeStruct((m, XA_WIDTH), BF16)),
        grid=(m // N_MEM,),
        in_specs=[
            pl.BlockSpec((N_MEM, k), lambda i: (i, 0)),
            _const_spec((1, k)),
            _const_spec((k, 2 * XA_WIDTH)),
            _const_spec((1, HEAD_DIM)),
        ],
        out_specs=(pl.BlockSpec((N_MEM, XA_WIDTH), lambda i: (i, 0)),
                   pl.BlockSpec((N_MEM, XA_WIDTH), lambda i: (i, 0))),
        compiler_params=pltpu.CompilerParams(
            dimension_semantics=("arbitrary",), vmem_limit_bytes=V7X_VMEM_LIMIT_BYTES),
        name="mem_kv",
    )(mem2d, gain.reshape(1, k), wkv_bf16, k_gain.reshape(1, HEAD_DIM))


def _post_kernel(x_ref, mhg_ref, mdsa_ref, mnsa_ref, mml_ref, wout_ref, gxa_ref, wq_ref, qg_ref,
                 k_ref, v_ref, wo_ref, gffn_ref, w13_ref, w2_ref, o_ref):
    x = x_ref[...]
    for g, mix_ref in enumerate((mhg_ref, mdsa_ref, mnsa_ref, mml_ref)):
        x = x + _dot(mix_ref[...].astype(BF16),
                     wout_ref[g * GROUP_WIDTH:(g + 1) * GROUP_WIDTH, :])
    h = _rms_rows(x, gxa_ref[...]).astype(BF16)
    q = _dot(h, wq_ref[...])
    qg = qg_ref[...] * (HEAD_DIM ** -0.5)
    k = k_ref[...]
    v = v_ref[...]
    heads = [slice(hd * HEAD_DIM, (hd + 1) * HEAD_DIM) for hd in range(XA_HEADS)]
    scores = [_dot_nt(_rms_rows(q[:, sl], qg).astype(BF16), k[:, sl]) for sl in heads]
    probs = []
    for s in scores:
        e = jnp.exp(s - jnp.max(s, axis=-1, keepdims=True))
        probs.append((e / jnp.sum(e, axis=-1, keepdims=True)).astype(BF16))
    o = jnp.concatenate([_dot(p, v[:, sl]) for p, sl in zip(probs, heads)], axis=-1)
    x = x + _dot(o.astype(BF16), wo_ref[...])
    h = _rms_rows(x, gffn_ref[...]).astype(BF16)

    def up(c):
        return (_dot(h, w13_ref[:, c * FF_CHUNK:(c + 1) * FF_CHUNK]),
                _dot(h, w13_ref[:, D_FF + c * FF_CHUNK:D_FF + (c + 1) * FF_CHUNK]))

    n_chunks = D_FF // FF_CHUNK
    acc = x
    a, b = up(0)
    for c in range(n_chunks):
        nxt = up(c + 1) if c + 1 < n_chunks else None
        act = (a * jax.nn.sigmoid(a) * b).astype(BF16)
        acc = acc + _dot(act, w2_ref[c * FF_CHUNK:(c + 1) * FF_CHUNK, :])
        if nxt is not None:
            a, b = nxt
    o_ref[...] = acc


def post_block(x2d, mixers, seq, wout, gxa, wq, qg, k_mem, v_mem, wo, gffn, w13, w2, tm):
    m, d = x2d.shape
    tiles_per_batch = seq // tm
    row = lambda i: (i, 0)
    mem_row = lambda i: (i // tiles_per_batch, 0)
    return pl.pallas_call(
        _post_kernel,
        out_shape=jax.ShapeDtypeStruct((m, d), F32),
        grid=(m // tm,),
        in_specs=[
            pl.BlockSpec((tm, d), row),
            *[pl.BlockSpec((tm, GROUP_WIDTH), row) for _ in mixers],
            _const_spec(wout.shape),
            _const_spec((1, d)),
            _const_spec(wq.shape),
            _const_spec((1, HEAD_DIM)),
            pl.BlockSpec((N_MEM, XA_WIDTH), mem_row),
            pl.BlockSpec((N_MEM, XA_WIDTH), mem_row),
            _const_spec(wo.shape),
            _const_spec((1, d)),
            _const_spec(w13.shape),
            _const_spec(w2.shape),
        ],
        out_specs=pl.BlockSpec((tm, d), row),
        compiler_params=pltpu.CompilerParams(
            dimension_semantics=("arbitrary",), vmem_limit_bytes=V7X_VMEM_LIMIT_BYTES),
        name="post_block",
    )(x2d, *mixers, wout, gxa.reshape(1, d), wq, qg.reshape(1, HEAD_DIM), k_mem, v_mem, wo,
      gffn.reshape(1, d), w13, w2)


C_HG = 0
C_DSA_Q = 1024
C_DSA_IQ = 1280
C_NSA_Q = 1536
C_ML_V = 1792
C_ML_QK = 2048
C_ML_OG = 2560
C_DSA_CKV = 2816
C_DSA_IK = 2944
C_DSA_IW = 3072
C_NSA_GATE = 3200
C_NSA_CMP = 3328
C_NSA_SLC = 3456
C_NSA_WIN = 3584
C_ML_GATE = 3712
IN_COLS_PACKED = 3840


def _packed_source_columns():
    dsa0 = GROUP_COLS[0]
    nsa0 = dsa0 + GROUP_COLS[1]
    ml0 = nsa0 + GROUP_COLS[2]
    segs = [(0, 1024, 1024), (dsa0, 256, 256), (dsa0 + 384, 256, 256), (nsa0, 256, 256),
            (ml0 + 512, 256, 256), (ml0, 512, 512), (ml0 + 768, 256, 256), (dsa0 + 256, 128, 128),
            (dsa0 + 640, 32, LANES), (dsa0 + 672, 8, LANES), (nsa0 + 640, 12, LANES),
            (nsa0 + 256, 128, 128), (nsa0 + 384, 128, 128), (nsa0 + 512, 128, 128),
            (ml0 + 1024, 8, LANES)]
    src = np.concatenate([np.concatenate([np.arange(a, a + n), np.full(width - n, -1)])
                          for a, n, width in segs]).astype(np.int32)
    assert src.shape == (IN_COLS_PACKED,)
    return src


PACK_TN = 256


def _pack_w_kernel(w_ref, src_ref, o_ref, wb_ref):
    @pl.when(pl.program_id(0) == 0)
    def _():
        wb_ref[...] = w_ref[...].astype(BF16)

    row = lax.broadcasted_iota(jnp.int32, (wb_ref.shape[1], PACK_TN), 0)
    sel = jnp.where(row == src_ref[...], 1.0, 0.0).astype(BF16)
    o_ref[...] = _dot(wb_ref[...], sel).astype(BF16)


def pack_w_in_bf16(w):
    d, n = w.shape
    n_pad = _round_up(n, LANES)
    w_pad = jnp.pad(w, ((0, 0), (0, n_pad - n)))
    src = jnp.asarray(_packed_source_columns()).reshape(1, IN_COLS_PACKED)
    return pl.pallas_call(
        _pack_w_kernel,
        out_shape=jax.ShapeDtypeStruct((d, IN_COLS_PACKED), BF16),
        grid=(IN_COLS_PACKED // PACK_TN,),
        in_specs=[_const_spec((d, n_pad)), pl.BlockSpec((1, PACK_TN), lambda j: (0, j))],
        out_specs=pl.BlockSpec((d, PACK_TN), lambda j: (0, j)),
        scratch_shapes=[pltpu.VMEM((d, n_pad), BF16)],
        compiler_params=pltpu.CompilerParams(
            dimension_semantics=("arbitrary",), vmem_limit_bytes=V7X_VMEM_LIMIT_BYTES),
        name="pack_w_in",
    )(w_pad, src)


def rope_tables(l_, d=HEAD_DIM):
    rd = d // 4
    half = rd // 2
    inv = ROPE_THETA ** (-jnp.arange(half, dtype=F32) * 2.0 / rd)
    ang = jnp.arange(l_).astype(F32)[:, None] * inv[None, :]
    cos, sin = lax.optimization_barrier((jnp.cos(ang), jnp.sin(ang)))
    zh = jnp.zeros((l_, half), F32)
    rest0 = jnp.zeros((l_, d - rd), F32)
    c = jnp.concatenate([cos, cos, rest0 + 1.0], axis=1)
    s1 = jnp.concatenate([-sin, zh, rest0], axis=1)
    s2 = jnp.concatenate([zh, sin, rest0], axis=1)
    return jnp.stack([c, s1, s2])


def rope_tables_pad(rt, width):
    n = width - rt.shape[-1]
    ident = jnp.stack([jnp.ones(rt.shape[1:2] + (n,), F32), jnp.zeros(rt.shape[1:2] + (n,), F32),
                       jnp.zeros(rt.shape[1:2] + (n,), F32)])
    return jnp.concatenate([rt, ident], axis=-1)


def rope_tables_kv_pair(rt):
    return rope_tables_pad(rt, 2 * rt.shape[-1])


def _apply_rope(t, rope_ref, half=HEAD_DIM // 8):
    w = t.shape[-1]
    return (t * rope_ref[0] + pltpu.roll(t, w - half, 1) * rope_ref[1]
            + pltpu.roll(t, half, 1) * rope_ref[2])


_NT = (((1,), (1,)), ((), ()))


def _dot_nt(a, b):
    return lax.dot_general(a, b, _NT, preferred_element_type=F32)


def _dot(a, b):
    return jnp.dot(a, b, preferred_element_type=F32)


NEG_BIG = -(2.0 ** 30)
NSA_KEY_TILE = 512
NSA_TQ = 128
NSA_SEL_TQ = 256
NSA_PREP_TK = 512


def _nsa_kv_prep_kernel(ps_ref, pw_ref, rope_ref, gs_ref, gw_ref, kaug_ref, vs_ref, kw_ref, vw_ref):
    tk = ps_ref.shape[0]
    lane = lax.broadcasted_iota(jnp.int32, (tk, LANES), 1)
    is_k = lane < HEAD_DIM

    def norm_rope(p, g):
        ms = jnp.sum(jnp.where(is_k, p * p, 0.0), axis=-1, keepdims=True) * (1.0 / HEAD_DIM)
        y = jnp.where(is_k, p * lax.rsqrt(ms + EPS) * g, p)
        return _apply_rope(y, rope_ref)

    ys = norm_rope(ps_ref[...], gs_ref[...])
    yw = norm_rope(pw_ref[...], gw_ref[...])
    row = pl.program_id(1) * tk + lax.broadcasted_iota(jnp.int32, (tk, LANES), 0)
    ind = jnp.where(jnp.right_shift(row, SLC_SHIFT) == (lane - HEAD_DIM), 1.0, 0.0)
    kaug_ref[0] = jnp.where(is_k, ys, ind).astype(BF16)
    vs_ref[0] = ys.T[HEAD_DIM:, :].astype(BF16)
    kw_ref[0] = yw[:, :HEAD_DIM].astype(BF16)
    vw_ref[0] = yw.T[HEAD_DIM:, :].astype(BF16)


def nsa_kv_prep(cols, b_, l_, rope_pair, g_slc, g_win):
    tk = NSA_PREP_TK
    nt = l_ // tk
    ones = jnp.ones((HEAD_DIM,), F32)
    gs = jnp.concatenate([g_slc, ones]).reshape(1, LANES)
    gw = jnp.concatenate([g_win, ones]).reshape(1, LANES)
    kv = lambda w: jax.ShapeDtypeStruct((b_, l_, w), BF16)
    kv_t = jax.ShapeDtypeStruct((b_, HEAD_DIM, l_), BF16)
    out_blk = lambda w: pl.BlockSpec((1, tk, w), lambda b, i: (b, i, 0))
    out_t = pl.BlockSpec((1, HEAD_DIM, tk), lambda b, i: (b, 0, i))
    return pl.pallas_call(
        _nsa_kv_prep_kernel,
        out_shape=(kv(LANES), kv_t, kv(HEAD_DIM), kv_t),
        grid=(b_, nt),
        in_specs=[
            pl.BlockSpec((tk, LANES), lambda b, i: (b * nt + i, C_NSA_SLC // LANES)),
            pl.BlockSpec((tk, LANES), lambda b, i: (b * nt + i, C_NSA_WIN // LANES)),
            pl.BlockSpec((3, tk, LANES), lambda b, i: (0, i, 0)),
            pl.BlockSpec((1, LANES), lambda b, i: (0, 0)),
            pl.BlockSpec((1, LANES), lambda b, i: (0, 0)),
        ],
        out_specs=(out_blk(LANES), out_t, out_blk(HEAD_DIM), out_t),
        compiler_params=pltpu.CompilerParams(
            dimension_semantics=("arbitrary", "arbitrary"),
            vmem_limit_bytes=V7X_VMEM_LIMIT_BYTES),
        name="nsa_kv_prep",
    )(cols, cols, rope_pair, gs, gw)


def _nsa_compress_kernel(r_ref, pea_ref, peb_ref, w1a_ref, w1b_ref, w2k_ref, w2v_ref, kg_ref,
                         kc_ref, vc_ref):
    r = r_ref[0]
    n = r.shape[0]
    a = _dot((r + pea_ref[...]).astype(BF16), w1a_ref[...])
    bm = _dot((r + peb_ref[...]).astype(BF16), w1b_ref[...])
    row = lax.broadcasted_iota(jnp.int32, bm.shape, 0)
    bm_up = jnp.where(row < n - 1, pltpu.roll(bm, n - 1, 0), 0.0)
    h = jnp.maximum(a + bm_up, 0.0).astype(BF16)
    hid = w2k_ref.shape[0]
    ck = _dot(h[:, :hid], w2k_ref[...])
    cv = _dot(h[:, hid:], w2v_ref[...])
    kc_ref[0] = _rms_rows(ck, kg_ref[...]).astype(BF16)
    vc_ref[0] = cv.astype(BF16)


def nsa_compress(cols, b_, l_, pos_k, pos_v, k_w1, k_w2, v_w1, v_w2, k_gain):
    rows = l_ // CMP_STRIDE
    hid = k_w1.shape[1]
    pair = cols[:, C_NSA_CMP:C_NSA_CMP + LANES].reshape(b_, rows, CMP_STRIDE * LANES)

    def interleave_pe(lo):
        pe = jnp.concatenate([pos_k[lo:lo + CMP_STRIDE], pos_v[lo:lo + CMP_STRIDE]], axis=1)
        return pe.reshape(1, CMP_STRIDE * LANES)

    def interleave_w(lo):
        wk = k_w1[lo * HEAD_DIM:(lo + CMP_STRIDE) * HEAD_DIM].reshape(CMP_STRIDE, HEAD_DIM, hid)
        wv = v_w1[lo * HEAD_DIM:(lo + CMP_STRIDE) * HEAD_DIM].reshape(CMP_STRIDE, HEAD_DIM, hid)
        z = jnp.zeros_like(wk)
        top = jnp.concatenate([wk, z], axis=2)
        bot = jnp.concatenate([z, wv], axis=2)
        return jnp.concatenate([top, bot], axis=1).reshape(CMP_STRIDE * LANES, 2 * hid).astype(BF16)

    out = jax.ShapeDtypeStruct((b_, rows, HEAD_DIM), BF16)
    return pl.pallas_call(
        _nsa_compress_kernel,
        out_shape=(out, out),
        grid=(b_,),
        in_specs=[
            pl.BlockSpec((1, rows, CMP_STRIDE * LANES), lambda b: (b, 0, 0)),
            _const_spec((1, CMP_STRIDE * LANES)),
            _const_spec((1, CMP_STRIDE * LANES)),
            _const_spec((CMP_STRIDE * LANES, 2 * hid)),
            _const_spec((CMP_STRIDE * LANES, 2 * hid)),
            _const_spec((hid, HEAD_DIM)),
            _const_spec((hid, HEAD_DIM)),
            _const_spec((1, HEAD_DIM)),
        ],
        out_specs=(pl.BlockSpec((1, rows, HEAD_DIM), lambda b: (b, 0, 0)),
                   pl.BlockSpec((1, rows, HEAD_DIM), lambda b: (b, 0, 0))),
        compiler_params=pltpu.CompilerParams(
            dimension_semantics=("arbitrary",), vmem_limit_bytes=V7X_VMEM_LIMIT_BYTES),
        name="nsa_compress",
    )(pair, interleave_pe(0), interleave_pe(CMP_STRIDE), interleave_w(0),
      interleave_w(CMP_STRIDE), k_w2.astype(BF16), v_w2.astype(BF16),
      k_gain.reshape(1, HEAD_DIM))


def _masked_softmax(s, valid, axis):
    m = jnp.max(jnp.where(valid, s, -jnp.inf), axis=axis, keepdims=True)
    m = jnp.where(m == -jnp.inf, 0.0, m)
    e = jnp.where(valid, jnp.exp(s - m), 0.0)
    den = jnp.sum(e, axis=axis, keepdims=True)
    return e / jnp.where(den > 0, den, 1.0)


def _nsa_select_kernel(q_ref, kc_ref, vc_ref, ovt_ref, rope_ref, qg_ref, qaug_ref, ocmp_ref,
                       *, n_sel):
    tq = q_ref.shape[0]
    ncr = kc_ref.shape[1]
    n_slc = ovt_ref.shape[0]
    t0 = pl.program_id(1) * tq
    scale = HEAD_DIM ** -0.5
    q = q_ref[...]
    g = qg_ref[...]
    kc = kc_ref[0]
    vc = vc_ref[0]
    last = CMP_BLOCK - 1
    vis = (lax.broadcasted_iota(jnp.int32, (tq, ncr), 1) * CMP_STRIDE + last
           <= t0 + lax.broadcasted_iota(jnp.int32, (tq, ncr), 0))
    vis_t = (lax.broadcasted_iota(jnp.int32, (ncr, tq), 0) * CMP_STRIDE + last
             <= t0 + lax.broadcasted_iota(jnp.int32, (ncr, tq), 1))
    qn_heads, o_heads = [], []
    psum_t = jnp.zeros((ncr, tq), F32)
    for h in range(GROUP_HEADS):
        qn = _rms_rows(q[:, h * HEAD_DIM:(h + 1) * HEAD_DIM], g)
        qn_heads.append(qn)
        qb = (qn * scale).astype(BF16)
        p = _masked_softmax(_dot_nt(qb, kc), vis, 1)
        o_heads.append(_dot(p.astype(BF16), vc))
        psum_t = psum_t + _masked_softmax(_dot_nt(kc, qb), vis_t, 0)
    ocmp_ref[0] = jnp.concatenate(o_heads, axis=-1)

    hi = psum_t.astype(BF16)
    lo = (psum_t - hi.astype(F32)).astype(BF16)
    imp = _dot(ovt_ref[...], hi) + _dot(ovt_ref[...], lo)
    blk = lax.broadcasted_iota(jnp.int32, (n_slc, tq), 0)
    cur = jnp.right_shift(t0 + lax.broadcasted_iota(jnp.int32, (n_slc, tq), 1), SLC_SHIFT)
    forced = (blk == 0) | (blk == cur) | (blk == cur - 1)
    imp = jnp.where(forced, jnp.inf, jnp.where(blk > cur, -jnp.inf, imp))
    n_grp = n_slc // SUBLANES
    groups = [imp[g * SUBLANES:(g + 1) * SUBLANES] for g in range(n_grp)]
    cnts = [jnp.zeros((SUBLANES, tq), F32) for _ in range(n_grp)]
    sub = lax.broadcasted_iota(jnp.int32, (SUBLANES, tq), 0)
    for m in range(n_slc):
        row = imp[m:m + 1, :]
        gm, rm = divmod(m, SUBLANES)
        for g in range(n_grp):
            if g < gm:
                beats = row > groups[g]
            elif g > gm:
                beats = row >= groups[g]
            else:
                beats = (row > groups[g]) | ((row == groups[g]) & (sub > rm))
            cnts[g] = cnts[g] + jnp.where(beats, 1.0, 0.0)
    cnt = jnp.concatenate(cnts, axis=0)
    mt = jnp.where(cnt < n_sel, 0.0, NEG_BIG)
    pad = jnp.zeros((HEAD_DIM - n_slc, tq), F32)
    mt = jnp.concatenate([mt, pad, mt, pad], axis=0) if n_slc < HEAD_DIM else jnp.concatenate(
        [mt, mt], axis=0)
    mt = mt.T

    qr = _apply_rope(jnp.concatenate(qn_heads, axis=-1), rope_ref) * scale
    lane = lax.broadcasted_iota(jnp.int32, (tq, LANES), 1)
    for j in range(GROUP_HEADS // 2):
        pair = qr[:, j * LANES:(j + 1) * LANES]
        swapped = pltpu.roll(pair, HEAD_DIM, 1)
        qaug_ref[0, :, (2 * j) * LANES:(2 * j + 1) * LANES] = jnp.where(
            lane < HEAD_DIM, pair, mt).astype(BF16)
        qaug_ref[0, :, (2 * j + 1) * LANES:(2 * j + 2) * LANES] = jnp.where(
            lane < HEAD_DIM, swapped, mt).astype(BF16)


def nsa_select(cols, b_, l_, k_cmp, v_cmp, rope_q, q_gain):
    tq = min(NSA_SEL_TQ, l_)
    nt = l_ // tq
    ncr = l_ // CMP_STRIDE
    n_slc = l_ // SLC_BLOCK
    n_sel = min(SLC_TOPN, n_slc)
    st_c = np.arange(ncr) * CMP_STRIDE
    st_s = np.arange(n_slc) * SLC_BLOCK
    ovt = ((st_c[None, :] < st_s[:, None] + SLC_BLOCK)
           & (st_c[None, :] + CMP_BLOCK > st_s[:, None])).astype(np.float32)
    return pl.pallas_call(
        functools.partial(_nsa_select_kernel, n_sel=n_sel),
        out_shape=(jax.ShapeDtypeStruct((b_, l_, GROUP_HEADS * LANES), BF16),
                   jax.ShapeDtypeStruct((b_, l_, GROUP_WIDTH), F32)),
        grid=(b_, nt),
        in_specs=[
            pl.BlockSpec((tq, GROUP_WIDTH), lambda b, i: (b * nt + i, C_NSA_Q // GROUP_WIDTH)),
            pl.BlockSpec((1, ncr, HEAD_DIM), lambda b, i: (b, 0, 0)),
            pl.BlockSpec((1, ncr, HEAD_DIM), lambda b, i: (b, 0, 0)),
            pl.BlockSpec((n_slc, ncr), lambda b, i: (0, 0)),
            pl.BlockSpec((3, tq, GROUP_WIDTH), lambda b, i: (0, i, 0)),
            pl.BlockSpec((1, HEAD_DIM), lambda b, i: (0, 0)),
        ],
        out_specs=(pl.BlockSpec((1, tq, GROUP_HEADS * LANES), lambda b, i: (b, i, 0)),
                   pl.BlockSpec((1, tq, GROUP_WIDTH), lambda b, i: (b, i, 0))),
        compiler_params=pltpu.CompilerParams(
            dimension_semantics=("arbitrary", "arbitrary"),
            vmem_limit_bytes=V7X_VMEM_LIMIT_BYTES),
        name="nsa_select",
    )(cols, k_cmp, v_cmp, jnp.asarray(ovt, BF16), rope_q, q_gain.reshape(1, HEAD_DIM))


def _softmax_stats_update(s, m, l):
    m_new = jnp.maximum(m, jnp.max(s, axis=0, keepdims=True))
    alpha = jnp.exp(m - m_new)
    p = jnp.exp(s - m_new)
    return m_new, alpha, p, alpha * l + jnp.sum(p, axis=0, keepdims=True)


def _nsa_attend_t_kernel(qaug_ref, kaug_ref, vst_ref, kw_ref, vwt_ref, ocmp_ref, gate_ref, o_ref):
    tq = qaug_ref.shape[1]
    nh = GROUP_HEADS
    ts = NSA_KEY_TILE
    t0 = pl.program_id(1) * tq
    qa = qaug_ref[0]
    qs = jnp.concatenate([qa[:, h * LANES:(h + 1) * LANES] for h in range(nh)], axis=0)
    n = nh * tq
    qpos_tile = t0 + lax.broadcasted_iota(jnp.int32, (1, tq), 1)
    qpos = jnp.concatenate([qpos_tile] * nh, axis=1)

    def key_tile(kt, carry):
        m, l, acc = carry
        k0 = pl.multiple_of(kt * ts, ts)
        s = _dot_nt(kaug_ref[0, pl.ds(k0, ts), :], qs)
        kpos = k0 + lax.broadcasted_iota(jnp.int32, (ts, n), 0)
        s = jnp.where(kpos <= qpos, s, NEG_BIG)
        m, alpha, p, l = _softmax_stats_update(s, m, l)
        acc = alpha * acc + _dot(vst_ref[0, :, pl.ds(k0, ts)], p.astype(BF16))
        return m, l, acc

    n_kt = (t0 + tq + ts - 1) // ts
    init = (jnp.full((1, n), -jnp.inf, F32), jnp.zeros((1, n), F32),
            jnp.zeros((HEAD_DIM, n), F32))
    _, l, acc = lax.fori_loop(0, n_kt, key_tile, init)
    o_slc = acc / l

    wlen = WINDOW + tq
    start = pl.multiple_of(jnp.maximum(t0 - WINDOW, 0), tq)
    sw = _dot_nt(kw_ref[0, pl.ds(start, wlen), :], qs[:, :HEAD_DIM])
    dist = qpos - (start + lax.broadcasted_iota(jnp.int32, (wlen, n), 0))
    sw = jnp.where((dist >= 0) & (dist < WINDOW), sw, -jnp.inf)
    e = jnp.exp(sw - jnp.max(sw, axis=0, keepdims=True))
    o_swa = _dot(vwt_ref[0, :, pl.ds(start, wlen)], e.astype(BF16)) / jnp.sum(
        e, axis=0, keepdims=True)

    g = jax.nn.sigmoid(gate_ref[...]).T
    oc = ocmp_ref[0].T
    outs = []
    for h in range(nh):
        cols_h = slice(h * tq, (h + 1) * tq)
        outs.append(g[h:h + 1, :] * oc[h * HEAD_DIM:(h + 1) * HEAD_DIM, :]
                    + g[nh + h:nh + h + 1, :] * o_slc[:, cols_h]
                    + g[2 * nh + h:2 * nh + h + 1, :] * o_swa[:, cols_h])
    o_ref[...] = jnp.concatenate(outs, axis=0).T


def nsa_attend(cols, b_, l_, q_aug, k_aug, v_slc, k_win, v_win, o_cmp):
    tq = NSA_TQ
    nt = l_ // tq
    seq = lambda w: pl.BlockSpec((1, l_, w), lambda b, i: (b, 0, 0))
    seq_t = pl.BlockSpec((1, HEAD_DIM, l_), lambda b, i: (b, 0, 0))
    return pl.pallas_call(
        _nsa_attend_t_kernel,
        out_shape=jax.ShapeDtypeStruct((b_ * l_, GROUP_WIDTH), F32),
        grid=(b_, nt),
        in_specs=[
            pl.BlockSpec((1, tq, GROUP_HEADS * LANES), lambda b, i: (b, i, 0)),
            seq(LANES), seq_t, seq(HEAD_DIM), seq_t,
            pl.BlockSpec((1, tq, GROUP_WIDTH), lambda b, i: (b, i, 0)),
            pl.BlockSpec((tq, LANES), lambda b, i: (b * nt + i, C_NSA_GATE // LANES)),
        ],
        out_specs=pl.BlockSpec((tq, GROUP_WIDTH), lambda b, i: (b * nt + i, 0)),
        compiler_params=pltpu.CompilerParams(
            dimension_semantics=("arbitrary", "arbitrary"),
            vmem_limit_bytes=V7X_VMEM_LIMIT_BYTES),
        name="nsa_attend",
    )(q_aug, k_aug, v_slc, k_win, v_win, o_cmp, cols)


def nsa_mixer_pallas(cols, b_, l_, rope, pos_k, pos_v, k_w1, k_w2, v_w1, v_w2, q_gain, k_gains):
    k_aug, v_slc, k_win, v_win = nsa_kv_prep(cols, b_, l_, rope_tables_kv_pair(rope),
                                             k_gains[1], k_gains[2])
    k_cmp, v_cmp = nsa_compress(cols, b_, l_, pos_k, pos_v, k_w1, k_w2, v_w1, v_w2, k_gains[0])
    q_aug, o_cmp = nsa_select(cols, b_, l_, k_cmp, v_cmp, jnp.tile(rope, (1, 1, GROUP_HEADS)),
                              q_gain)
    return nsa_attend(cols, b_, l_, q_aug, k_aug, v_slc, k_win, v_win, o_cmp)


DSA_TQ = 128
DSA_KEY_TILE = 512
DSA_PREP_TK = 512
IDX_PACK = LANES
INT_MIN = -2 ** 31
MASKED_SCORE = -1e30


def _split_hi_lo(t):
    hi = t.astype(BF16)
    lo = (t - hi.astype(F32)).astype(BF16)
    return hi, lo


def _placement(rows, cols, pairs):
    p = np.zeros((rows, cols), np.float32)
    for r, c in pairs:
        p[r, c] = 1.0
    return jnp.asarray(p, BF16)


def _dsa_kv_prep_kernel(ckv_ref, ik_ref, rope_ref, ropei_ref, kvg_ref, wkv_ref, kg_ref, ikg_ref,
                        pkh_ref, pkl_ref, k_ref, vt_ref, ik3_ref):
    tk = ckv_ref.shape[0]
    lane = lax.broadcasted_iota(jnp.int32, (tk, LANES), 1)
    ckv = _rms_rows(ckv_ref[...], kvg_ref[...]).astype(BF16)
    kv = _dot(ckv, wkv_ref[...])
    is_k = lane < HEAD_DIM
    ms = jnp.sum(jnp.where(is_k, kv * kv, 0.0), axis=-1, keepdims=True) * (1.0 / HEAD_DIM)
    y = _apply_rope(jnp.where(is_k, kv * lax.rsqrt(ms + EPS) * kg_ref[...], kv), rope_ref)
    k_ref[0] = y[:, :HEAD_DIM].astype(BF16)
    vt_ref[0] = y.T[HEAD_DIM:, :].astype(BF16)
    ik = ik_ref[...]
    ms = jnp.sum(ik * ik, axis=-1, keepdims=True) * (1.0 / IDX_DIM)
    ikn = _apply_rope(ik * lax.rsqrt(ms + EPS) * ikg_ref[...], ropei_ref, IDX_DIM // 8)
    hi, lo = _split_hi_lo(ikn)
    ik3_ref[0] = (_dot(hi, pkh_ref[...]) + _dot(lo, pkl_ref[...])).astype(BF16)


def dsa_kv_prep(cols, b_, l_, rope_pair, rope_idx, kv_gain, w_uk, w_uv, k_gain, idxk_gain):
    tk = DSA_PREP_TK
    nt = l_ // tk
    ones = jnp.ones((HEAD_DIM,), F32)
    kg = jnp.concatenate([k_gain, ones]).reshape(1, LANES)
    ikg = jnp.pad(idxk_gain, (0, LANES - IDX_DIM)).reshape(1, LANES)
    wkv = jnp.concatenate([w_uk, w_uv], axis=1).astype(BF16)
    d = range(IDX_DIM)
    pkh = _placement(LANES, IDX_PACK, [(i, i) for i in d] + [(i, 2 * IDX_DIM + i) for i in d])
    pkl = _placement(LANES, IDX_PACK, [(i, IDX_DIM + i) for i in d])
    const = lambda shape: pl.BlockSpec(shape, lambda b, i: (0,) * len(shape))
    out = lambda w: pl.BlockSpec((1, tk, w), lambda b, i: (b, i, 0))
    return pl.pallas_call(
        _dsa_kv_prep_kernel,
        out_shape=(jax.ShapeDtypeStruct((b_, l_, HEAD_DIM), BF16),
                   jax.ShapeDtypeStruct((b_, HEAD_DIM, l_), BF16),
                   jax.ShapeDtypeStruct((b_, l_, IDX_PACK), BF16)),
        grid=(b_, nt),
        in_specs=[
            pl.BlockSpec((tk, LANES), lambda b, i: (b * nt + i, C_DSA_CKV // LANES)),
            pl.BlockSpec((tk, LANES), lambda b, i: (b * nt + i, C_DSA_IK // LANES)),
            pl.BlockSpec((3, tk, LANES), lambda b, i: (0, i, 0)),
            pl.BlockSpec((3, tk, LANES), lambda b, i: (0, i, 0)),
            const((1, LANES)), const((DSA_LATENT, LANES)), const((1, LANES)), const((1, LANES)),
            const((LANES, IDX_PACK)), const((LANES, IDX_PACK)),
        ],
        out_specs=(out(HEAD_DIM), pl.BlockSpec((1, HEAD_DIM, tk), lambda b, i: (b, 0, i)),
                   out(IDX_PACK)),
        compiler_params=pltpu.CompilerParams(
            dimension_semantics=("arbitrary", "arbitrary"),
            vmem_limit_bytes=V7X_VMEM_LIMIT_BYTES),
        name="dsa_kv_prep",
    )(cols, cols, rope_pair, rope_idx, kv_gain.reshape(1, DSA_LATENT), wkv, kg, ikg, pkh, pkl)


def _dsa_q_prep_kernel(q_ref, iq_ref, rope_ref, ropei_ref, qg_ref, pqh_ref, pql_ref,
                       qh_ref, iq3_ref):
    g = qg_ref[...]
    q = q_ref[...]
    qn = jnp.concatenate([_rms_rows(q[:, h * HEAD_DIM:(h + 1) * HEAD_DIM], g)
                          for h in range(GROUP_HEADS)], axis=-1)
    qr = _apply_rope(qn, rope_ref) * (HEAD_DIM ** -0.5)
    for h in range(GROUP_HEADS):
        qh_ref[0, h] = qr[:, h * HEAD_DIM:(h + 1) * HEAD_DIM].astype(BF16)
    hi, lo = _split_hi_lo(_apply_rope(iq_ref[...], ropei_ref, IDX_DIM // 8))
    iq3_ref[0] = (_dot(hi, pqh_ref[...]) + _dot(lo, pql_ref[...])).astype(BF16)


def dsa_q_prep(cols, b_, l_, rope_q, rope_iq, q_gain):
    tq = 256
    nt = l_ // tq
    w = IDX_HEADS * IDX_DIM
    hd = [(h, i) for h in range(IDX_HEADS) for i in range(IDX_DIM)]
    pqh = _placement(w, IDX_HEADS * IDX_PACK,
                     [(IDX_DIM * h + i, IDX_PACK * h + i) for h, i in hd]
                     + [(IDX_DIM * h + i, IDX_PACK * h + IDX_DIM + i) for h, i in hd])
    pql = _placement(w, IDX_HEADS * IDX_PACK,
                     [(IDX_DIM * h + i, IDX_PACK * h + 2 * IDX_DIM + i) for h, i in hd])
    const = lambda shape: pl.BlockSpec(shape, lambda b, i: (0,) * len(shape))
    return pl.pallas_call(
        _dsa_q_prep_kernel,
        out_shape=(jax.ShapeDtypeStruct((b_, GROUP_HEADS, l_, HEAD_DIM), BF16),
                   jax.ShapeDtypeStruct((b_, l_, IDX_HEADS * IDX_PACK), BF16)),
        grid=(b_, nt),
        in_specs=[
            pl.BlockSpec((tq, GROUP_WIDTH), lambda b, i: (b * nt + i, C_DSA_Q // GROUP_WIDTH)),
            pl.BlockSpec((tq, w), lambda b, i: (b * nt + i, C_DSA_IQ // w)),
            pl.BlockSpec((3, tq, GROUP_WIDTH), lambda b, i: (0, i, 0)),
            pl.BlockSpec((3, tq, w), lambda b, i: (0, i, 0)),
            const((1, HEAD_DIM)), const(pqh.shape), const(pql.shape),
        ],
        out_specs=(pl.BlockSpec((1, GROUP_HEADS, tq, HEAD_DIM), lambda b, i: (b, 0, i, 0)),
                   pl.BlockSpec((1, tq, IDX_HEADS * IDX_PACK), lambda b, i: (b, i, 0))),
        compiler_params=pltpu.CompilerParams(
            dimension_semantics=("arbitrary", "arbitrary"),
            vmem_limit_bytes=V7X_VMEM_LIMIT_BYTES),
        name="dsa_q_prep",
    )(cols, cols, rope_q, rope_iq, q_gain.reshape(1, HEAD_DIM), pqh, pql)


SUBLANES = 8


def _fold_rows(t, op, group=SUBLANES):
    parts = [t[i * group:(i + 1) * group] for i in range(t.shape[0] // group)]
    while len(parts) > 1:
        parts = [op(parts[i], parts[i + 1]) if i + 1 < len(parts) else parts[i]
                 for i in range(0, len(parts), 2)]
    return parts[0]


def _dsa_attend_t_kernel(qh_ref, iq3_ref, iw_ref, k_ref, vt_ref, ik3_ref, o_ref, sc_ref, *,
                         topk, idx_bits):
    tq = iq3_ref.shape[1]
    ts = DSA_KEY_TILE
    nh = GROUP_HEADS
    t0 = pl.program_id(1) * tq
    n_kt = (t0 + tq + ts - 1) // ts
    qpos = t0 + lax.broadcasted_iota(jnp.int32, (ts, tq), 1)
    krow = lax.broadcasted_iota(jnp.int32, (ts, tq), 0)

    iq3 = iq3_ref[0]
    iq_rows = jnp.concatenate([iq3[:, h * IDX_PACK:(h + 1) * IDX_PACK]
                               for h in range(IDX_HEADS)], axis=0)
    iw_t = (iw_ref[...] * (IDX_HEADS ** -0.5 * IDX_DIM ** -0.5)).T

    def score_tile(kt, _):
        k0 = pl.multiple_of(kt * ts, ts)
        rel = jnp.maximum(_dot_nt(ik3_ref[0, pl.ds(k0, ts), :], iq_rows), 0.0)
        sc = rel[:, 0:tq] * iw_t[0:1, :]
        for h in range(1, IDX_HEADS):
            sc = sc + rel[:, h * tq:(h + 1) * tq] * iw_t[h:h + 1, :]
        sc_ref[pl.ds(k0, ts), :] = jnp.where(k0 + krow <= qpos, sc, -jnp.inf)
        return 0

    lax.fori_loop(0, n_kt, score_tile, 0)

    def count(pred):
        def tile(kt, c):
            k0 = pl.multiple_of(kt * ts, ts)
            hit = jnp.where(pred(sc_ref[pl.ds(k0, ts), :], k0), 1.0, 0.0)
            return c + _fold_rows(hit, jnp.add)
        c = lax.fori_loop(0, n_kt, tile, jnp.zeros((SUBLANES, tq), F32))
        return jnp.sum(c, axis=0, keepdims=True)

    def key_to_float(key):
        return pltpu.bitcast(jnp.where(key >= 0, key, key ^ jnp.int32(0x7FFFFFFF)), F32)

    def value_bit(i, thr_key):
        cand = thr_key | jnp.left_shift(jnp.int32(1), 31 - i)
        cand_f = key_to_float(cand ^ jnp.int32(INT_MIN))
        return jnp.where(count(lambda sc, k0: sc >= cand_f) >= topk, cand, thr_key)

    thr_key = lax.fori_loop(0, 32, value_bit, jnp.zeros((1, tq), jnp.int32))
    few = t0 + lax.broadcasted_iota(jnp.int32, (1, tq), 1) + 1 < topk
    thr = jnp.where(few, -jnp.inf, key_to_float(thr_key ^ jnp.int32(INT_MIN)))
    n_ge = count(lambda sc, k0: sc >= thr)
    tie_break = jnp.max(jnp.where((n_ge > topk) & jnp.logical_not(few), 1.0, 0.0)) > 0.0

    def last_tie_position():
        need = topk - count(lambda sc, k0: sc > thr)

        def index_bit(i, last):
            cand = last | jnp.left_shift(jnp.int32(1), idx_bits - 1 - i)
            tied_below = lambda sc, k0: (sc == thr) & (k0 + krow < cand)
            return jnp.where(count(tied_below) < need, cand, last)

        return lax.fori_loop(0, idx_bits, index_bit, jnp.zeros((1, tq), jnp.int32))

    last = lax.cond(tie_break, last_tie_position,
                    lambda: jnp.full((1, tq), 2 ** idx_bits, jnp.int32))

    qs = qh_ref[0].reshape(nh * tq, HEAD_DIM)

    def key_tile(kt, carry):
        m, l, acc = carry
        k0 = pl.multiple_of(kt * ts, ts)
        sc = sc_ref[pl.ds(k0, ts), :]
        kpos = k0 + krow
        sel = ((sc > thr) | ((sc == thr) & (kpos <= last))) & (kpos <= qpos)
        bias = jnp.where(sel, 0.0, MASKED_SCORE)
        s = _dot_nt(k_ref[0, pl.ds(k0, ts), :], qs) + jnp.concatenate([bias] * nh, axis=1)
        m, alpha, p, l = _softmax_stats_update(s, m, l)
        acc = alpha * acc + _dot(vt_ref[0, :, pl.ds(k0, ts)], p.astype(BF16))
        return m, l, acc

    init = (jnp.full((1, nh * tq), MASKED_SCORE, F32), jnp.zeros((1, nh * tq), F32),
            jnp.zeros((HEAD_DIM, nh * tq), F32))
    _, l, acc = lax.fori_loop(0, n_kt, key_tile, init)
    o_t = acc / l
    o_ref[...] = jnp.concatenate([o_t[:, h * tq:(h + 1) * tq] for h in range(nh)], axis=0).T


def dsa_attend(cols, b_, l_, qh, iq3, k, v, ik3):
    tq = DSA_TQ
    nt = l_ // tq
    topk = min(DSA_TOPK_MAX, l_ // 4)
    idx_bits = int(np.log2(l_))
    assert 2 ** idx_bits == l_ and l_ % DSA_KEY_TILE == 0 and topk <= DSA_KEY_TILE
    seq = lambda w: pl.BlockSpec((1, l_, w), lambda b, i: (b, 0, 0))
    return pl.pallas_call(
        functools.partial(_dsa_attend_t_kernel, topk=topk, idx_bits=idx_bits),
        out_shape=jax.ShapeDtypeStruct((b_ * l_, GROUP_WIDTH), F32),
        grid=(b_, nt),
        in_specs=[
            pl.BlockSpec((1, GROUP_HEADS, tq, HEAD_DIM), lambda b, i: (b, 0, i, 0)),
            pl.BlockSpec((1, tq, IDX_HEADS * IDX_PACK), lambda b, i: (b, i, 0)),
            pl.BlockSpec((tq, LANES), lambda b, i: (b * nt + i, C_DSA_IW // LANES)),
            seq(HEAD_DIM), pl.BlockSpec((1, HEAD_DIM, l_), lambda b, i: (b, 0, 0)), seq(IDX_PACK),
        ],
        out_specs=pl.BlockSpec((tq, GROUP_WIDTH), lambda b, i: (b * nt + i, 0)),
        scratch_shapes=[pltpu.VMEM((l_, tq), F32)],
        compiler_params=pltpu.CompilerParams(
            dimension_semantics=("arbitrary", "arbitrary"),
            vmem_limit_bytes=V7X_VMEM_LIMIT_BYTES),
        name="dsa_attend",
    )(qh, iq3, cols, k, v, ik3)


def dsa_mixer_pallas(cols, b_, l_, rope, kv_gain, w_uk, w_uv, q_gain, k_gain, idxk_gain):
    rope_i = rope_tables(l_, IDX_DIM)
    k, v, ik3 = dsa_kv_prep(cols, b_, l_, rope_tables_kv_pair(rope), rope_tables_pad(rope_i, LANES),
                            kv_gain, w_uk, w_uv, k_gain, idxk_gain)
    qh, iq3 = dsa_q_prep(cols, b_, l_, jnp.tile(rope, (1, 1, GROUP_HEADS)),
                         jnp.tile(rope_i, (1, 1, IDX_HEADS)), q_gain)
    return dsa_attend(cols, b_, l_, qh, iq3, k, v, ik3)


_TN = (((0,), (0,)), ((), ()))


def _dot_tn(a, b):
    return lax.dot_general(a, b, _TN, preferred_element_type=F32)


def _split3(t):
    hi = t.astype(BF16)
    r = t - hi.astype(F32)
    mid = r.astype(BF16)
    lo = (r - mid.astype(F32)).astype(BF16)
    return hi, mid, lo


def _tri_cumsum(tri, t):
    hi, mid, lo = _split3(t)
    return _dot(tri, hi) + _dot(tri, mid) + _dot(tri, lo)


def _cumsum_tri_rows(t, tri_u):
    hi, mid, lo = _split3(t)
    return _dot(hi, tri_u) + _dot(mid, tri_u) + _dot(lo, tri_u)


def _head_rms(o, gain):
    return jnp.concatenate([_rms_rows(o[:, h * HEAD_DIM:(h + 1) * HEAD_DIM], gain)
                            for h in range(GROUP_HEADS)], axis=-1)


HG_SUB = 8
HG_BLOCK = 16
LOG2_E = 1.4426950408889634


def _hgrn2_kernel(q_ref, f_ref, i_ref, g_ref, lb_ref, gain_ref, tri_ref, ones_ref, bd_ref,
                  hm_ref, jm_ref, o_ref, st_ref, b_ref, kk_ref, v_ref):
    c = HG_CHUNK
    w = GROUP_WIDTH

    @pl.when(pl.program_id(1) == 0)
    def _():
        st_ref[...] = jnp.zeros_like(st_ref)

    lb = lb_ref[...]
    q = q_ref[...]
    qs = q * jax.nn.sigmoid(q) * (HEAD_DIM ** -0.5)
    forget = lb + (1.0 - lb) * jax.nn.sigmoid(f_ref[...])
    kk = 1.0 - forget
    bcum = _tri_cumsum(tri_ref[...], jnp.log(forget))
    v = i_ref[...]
    b2 = bcum * LOG2_E
    b_ref[...] = b2
    kk_ref[...] = kk
    v_ref[...] = v

    out = _dot_nt((qs * jnp.exp(bcum)).astype(BF16), st_ref[...].astype(BF16))

    ones_bd = ones_ref[...]
    pieces = []
    for g in range(c // HG_SUB):
        r0 = g * HG_SUB
        nr = (r0 // HG_BLOCK + 1) * HG_BLOCK - r0
        qg = qs[r0:r0 + nr, :]
        bg = b2[r0:r0 + nr, :]
        trow = r0 + lax.broadcasted_iota(jnp.int32, (HG_SUB, w), 0)
        terms = []
        for j in range(HG_SUB):
            s = r0 + j
            d = qg * kk_ref[s:s + 1, :] * jnp.exp2(bg - b_ref[s:s + 1, :])
            if j > 0:
                head = jnp.where(trow >= s, d[:HG_SUB], 0.0)
                d = jnp.concatenate([head, d[HG_SUB:]], axis=0) if nr > HG_SUB else head
            terms.append(d.astype(BF16))
        red = _dot(jnp.concatenate(terms, axis=0), ones_bd)
        acc = red[0:nr] * v_ref[r0:r0 + 1, :]
        for j in range(1, HG_SUB):
            acc = acc + red[j * nr:(j + 1) * nr] * v_ref[r0 + j:r0 + j + 1, :]
        pieces.append((r0, acc))

    nb = c // HG_BLOCK
    hm = hm_ref[...]
    q_rows, k_rows, v_rows = [], [], []
    for j in range(nb - 1):
        blk = slice(j * HG_BLOCK, (j + 1) * HG_BLOCK)
        r_j = b_ref[(j + 1) * HG_BLOCK - 1:(j + 1) * HG_BLOCK, :]
        later = slice((j + 1) * HG_BLOCK, c)
        q_rows.append(qs[later] * jnp.exp2(b2[later] - r_j))
        k_blk = kk[blk] * jnp.exp2(r_j - b2[blk])
        k_rows.append(jnp.concatenate([k_blk] * GROUP_HEADS, axis=0) * hm)
        v_rows.append(jnp.concatenate([v[blk]] * GROUP_HEADS, axis=0) * hm)
    scores = _dot_nt(jnp.concatenate(q_rows, axis=0).astype(BF16),
                     jnp.concatenate(k_rows, axis=0).astype(BF16))
    scores = jnp.where(jm_ref[...] > 0.0, scores, 0.0)
    off = _dot(scores.astype(BF16), jnp.concatenate(v_rows, axis=0).astype(BF16))
    row0 = 0
    for j in range(nb - 1):
        n_later = c - (j + 1) * HG_BLOCK
        pieces.append(((j + 1) * HG_BLOCK, off[row0:row0 + n_later]))
        row0 += n_later

    for r0, piece in pieces:
        parts = [piece]
        if r0 > 0:
            parts.insert(0, jnp.zeros((r0, w), F32))
        if r0 + piece.shape[0] < c:
            parts.append(jnp.zeros((c - r0 - piece.shape[0], w), F32))
        out = out + (jnp.concatenate(parts, axis=0) if len(parts) > 1 else piece)

    b_last = bcum[c - 1:c, :]
    kt = (kk * jnp.exp(b_last - bcum)).astype(BF16)
    st_ref[...] = jnp.exp(b_last) * st_ref[...] + _dot_tn(v.astype(BF16), kt) * bd_ref[...]

    g_in = g_ref[...]
    o_ref[...] = _head_rms(out, gain_ref[...]) * (g_in * jax.nn.sigmoid(g_in))


def hgrn2_mixer_pallas(cols, b_, l_, lb, o_gain):
    c = HG_CHUNK
    nt = l_ // c
    w = GROUP_WIDTH
    head = np.arange(w) // HEAD_DIM
    same = (head[:, None] == head[None, :]).astype(np.float32)
    tri = np.tril(np.ones((c, c), np.float32))
    nb = c // HG_BLOCK
    row_head = np.arange(GROUP_HEADS * HG_BLOCK) // HG_BLOCK
    hm = (row_head[:, None] == head[None, :]).astype(np.float32)
    q_block = np.concatenate([np.full(c - (j + 1) * HG_BLOCK, j) for j in range(nb - 1)])
    k_block = np.arange((nb - 1) * GROUP_HEADS * HG_BLOCK) // (GROUP_HEADS * HG_BLOCK)
    jm = (q_block[:, None] == k_block[None, :]).astype(np.float32)
    col = lambda j: pl.BlockSpec((c, w), lambda b, i: (b * nt + i, C_HG // w + j))
    const = lambda shape: pl.BlockSpec(shape, lambda b, i: (0,) * len(shape))
    return pl.pallas_call(
        _hgrn2_kernel,
        out_shape=jax.ShapeDtypeStruct((b_ * l_, w), F32),
        grid=(b_, nt),
        in_specs=[col(0), col(1), col(2), col(3), const((1, w)), const((1, HEAD_DIM)),
                  const((c, c)), const((w, w)), const((w, w)), const(hm.shape), const(jm.shape)],
        out_specs=pl.BlockSpec((c, w), lambda b, i: (b * nt + i, 0)),
        scratch_shapes=[pltpu.VMEM((w, w), F32), pltpu.VMEM((c, w), F32),
                        pltpu.VMEM((c, w), F32), pltpu.VMEM((c, w), F32)],
        compiler_params=pltpu.CompilerParams(
            dimension_semantics=("arbitrary", "arbitrary"),
            vmem_limit_bytes=V7X_VMEM_LIMIT_BYTES),
        name="hgrn2",
    )(cols, cols, cols, cols, lb.reshape(1, w), o_gain.reshape(1, HEAD_DIM),
      jnp.asarray(tri, BF16), jnp.asarray(same, BF16), jnp.asarray(same, F32),
      jnp.asarray(hm), jnp.asarray(jm))


ML_TC = 256
ML_M_INIT = -1e30


def _mlstm_kernel(gate_ref, qk_ref, v_ref, og_ref, cw_ref, cb_ref, gb_ref, gain_ref, tril_ref,
                  triu_ref, o_ref, xprev_ref, cmat_ref, nvec_ref, m_ref):
    c = ML_TC
    nh = GROUP_HEADS
    w = GROUP_WIDTH

    @pl.when(pl.program_id(1) == 0)
    def _():
        xprev_ref[...] = jnp.zeros_like(xprev_ref)
        cmat_ref[...] = jnp.zeros_like(cmat_ref)
        nvec_ref[...] = jnp.zeros_like(nvec_ref)
        m_ref[...] = jnp.full(m_ref.shape, ML_M_INIT, F32)

    x = qk_ref[...]
    prev = xprev_ref[...]
    row = lax.broadcasted_iota(jnp.int32, x.shape, 0)
    acc = x * cw_ref[CONV_WIDTH - 1:CONV_WIDTH, :] + cb_ref[...]
    for j in range(1, CONV_WIDTH):
        shifted = jnp.where(row < j, pltpu.roll(prev, j, 0), pltpu.roll(x, j, 0))
        acc = acc + shifted * cw_ref[CONV_WIDTH - 1 - j:CONV_WIDTH - j, :]
    xprev_ref[...] = x
    qk = acc * jax.nn.sigmoid(acc)
    q = qk[:, :w]
    k = qk[:, w:] * (HEAD_DIM ** -0.5)
    v = v_ref[...]

    pre = gate_ref[...] + gb_ref[...]
    lane = lax.broadcasted_iota(jnp.int32, pre.shape, 1)
    log_f = jnp.minimum(pre, 0.0) - jnp.log1p(jnp.exp(-jnp.abs(pre)))
    log_f = jnp.where((lane >= nh) & (lane < 2 * nh), log_f, 0.0)
    bcum_c = _tri_cumsum(tril_ref[...], log_f)
    bcum_r = _cumsum_tri_rows(log_f.T, triu_ref[...])
    pre_r = pre.T
    tri = (lax.broadcasted_iota(jnp.int32, (c, c), 0) >= lax.broadcasted_iota(jnp.int32, (c, c), 1))

    outs = []
    for h in range(nh):
        sl = slice(h * HEAD_DIM, (h + 1) * HEAD_DIM)
        qh, kh, vh = q[:, sl], k[:, sl], v[:, sl]
        bc = bcum_c[:, nh + h:nh + h + 1]
        li_c = pre[:, h:h + 1]
        a_r = pre_r[h:h + 1, :] - bcum_r[nh + h:nh + h + 1, :]
        m_prev = m_ref[h:h + 1, 0:1]
        log_d = jnp.where(tri, bc + a_r, -jnp.inf)
        inter = bc + m_prev
        m_t = jnp.maximum(inter, jnp.max(log_d, axis=-1, keepdims=True))
        d_mat = jnp.exp(log_d - m_t)
        w_inter = jnp.exp(inter - m_t)
        qb = qh.astype(BF16)
        s = _dot_nt(qb, kh.astype(BF16)) * d_mat
        num = w_inter * _dot(qb, cmat_ref[h].astype(BF16)) + _dot(s.astype(BF16), vh.astype(BF16))
        den = (w_inter * jnp.sum(qh * nvec_ref[h], axis=-1, keepdims=True)
               + jnp.sum(s, axis=-1, keepdims=True))
        outs.append(num / jnp.maximum(jnp.abs(den), jnp.exp(-m_t)))
        b_last = bc[c - 1:c, :]
        log_w = b_last + (li_c - bc)
        m_new = jnp.maximum(b_last + m_prev, jnp.max(log_w, axis=0, keepdims=True))
        kw = kh * jnp.exp(log_w - m_new)
        decay = jnp.exp(b_last + m_prev - m_new)
        cmat_ref[h] = decay * cmat_ref[h] + _dot_tn(kw.astype(BF16), vh.astype(BF16))
        nvec_ref[h] = decay * nvec_ref[h] + jnp.sum(kw, axis=0, keepdims=True)
        m_ref[h:h + 1, :] = jnp.broadcast_to(m_new, (1, LANES))

    hh = _head_rms(jnp.concatenate(outs, axis=-1), gain_ref[...])
    o_ref[...] = hh * jax.nn.sigmoid(og_ref[...])


def mlstm_mixer_pallas(cols, b_, l_, conv_w, conv_b, i_bias, f_bias, o_gain):
    c = ML_TC
    nt = l_ // c
    w = GROUP_WIDTH
    gb = jnp.pad(jnp.concatenate([i_bias, f_bias]), (0, LANES - 2 * GROUP_HEADS)).reshape(1, LANES)
    tril = np.tril(np.ones((c, c), np.float32))
    const = lambda shape: pl.BlockSpec(shape, lambda b, i: (0,) * len(shape))
    blk = lambda width, off: pl.BlockSpec((c, width), lambda b, i: (b * nt + i, off // width))
    return pl.pallas_call(
        _mlstm_kernel,
        out_shape=jax.ShapeDtypeStruct((b_ * l_, w), F32),
        grid=(b_, nt),
        in_specs=[blk(LANES, C_ML_GATE), blk(2 * w, C_ML_QK), blk(w, C_ML_V), blk(w, C_ML_OG),
                  const((CONV_WIDTH, 2 * w)), const((1, 2 * w)), const((1, LANES)),
                  const((1, HEAD_DIM)), const((c, c)), const((c, c))],
        out_specs=pl.BlockSpec((c, w), lambda b, i: (b * nt + i, 0)),
        scratch_shapes=[pltpu.VMEM((c, 2 * w), F32),
                        pltpu.VMEM((GROUP_HEADS, HEAD_DIM, HEAD_DIM), F32),
                        pltpu.VMEM((GROUP_HEADS, 1, HEAD_DIM), F32),
                        pltpu.VMEM((8, LANES), F32)],
        compiler_params=pltpu.CompilerParams(
            dimension_semantics=("arbitrary", "arbitrary"),
            vmem_limit_bytes=V7X_VMEM_LIMIT_BYTES),
        name="mlstm",
    )(cols, cols, cols, cols, conv_w, conv_b.reshape(1, 2 * w), gb, o_gain.reshape(1, HEAD_DIM),
      jnp.asarray(tril, BF16), jnp.asarray(tril.T, BF16))


def kernel(x, mem, lb_param, norm_mix, w_in, w_out, hg_o_gain, dsa_kv_gain, dsa_w_uk, dsa_w_uv,
           dsa_q_gain, dsa_k_gain, dsa_idxk_gain, nsa_pos_k, nsa_pos_v, nsa_k_w1, nsa_k_w2,
           nsa_v_w1, nsa_v_w2, nsa_q_gain, nsa_k_gains, ml_conv_w, ml_conv_b, ml_i_bias,
           ml_f_bias, ml_o_gain, norm_xa, norm_mem, xa_wq, xa_wkv, xa_wo, xa_q_gain, xa_k_gain,
           norm_ffn, ffn_w13, ffn_w2):
    b_, l_, d = x.shape
    lb_all = jnp.cumsum(jax.nn.softmax(lb_param.astype(F32), axis=0), axis=0)
    lb_all = lb_all - lb_all[:1]
    x2d = x.reshape(b_ * l_, d)
    mem2d = mem.reshape(b_ * N_MEM, d)
    rope = rope_tables(l_)
    for l in range(DEPTH):
        cols = norm_matmul(x2d, norm_mix[l], pack_w_in_bf16(w_in[l]), tm=256)
        mixers = (
            hgrn2_mixer_pallas(cols, b_, l_, lb_all[l], hg_o_gain[l]),
            dsa_mixer_pallas(cols, b_, l_, rope, dsa_kv_gain[l], dsa_w_uk[l], dsa_w_uv[l],
                             dsa_q_gain[l], dsa_k_gain[l], dsa_idxk_gain[l]),
            nsa_mixer_pallas(cols, b_, l_, rope, nsa_pos_k[l], nsa_pos_v[l], nsa_k_w1[l],
                             nsa_k_w2[l], nsa_v_w1[l], nsa_v_w2[l], nsa_q_gain[l],
                             nsa_k_gains[l]),
            mlstm_mixer_pallas(cols, b_, l_, ml_conv_w[l], ml_conv_b[l], ml_i_bias[l],
                               ml_f_bias[l], ml_o_gain[l]),
        )
        k_mem, v_mem = mem_kv(mem2d, norm_mem[l], xa_wkv[l].astype(BF16), xa_k_gain[l])
        x2d = post_block(x2d, mixers, l_, w_out[l].astype(BF16), norm_xa[l],
                         xa_wq[l].astype(BF16), xa_q_gain[l], k_mem, v_mem,
                         xa_wo[l].astype(BF16), norm_ffn[l], ffn_w13[l].astype(BF16),
                         ffn_w2[l].astype(BF16), tm=512)
    return x2d.reshape(b_, l_, d)
```

```python
import functools

import jax
import jax.numpy as jnp
from jax import lax
import numpy as np
from jax.experimental import pallas as pl
from jax.experimental.pallas import tpu as pltpu

F32 = jnp.float32
BF16 = jnp.bfloat16

D_MODEL = 1024
DEPTH = 2
HEAD_DIM = 64
GROUP_HEADS = 4
GROUP_WIDTH = GROUP_HEADS * HEAD_DIM
ROPE_THETA = 500000.0
EPS = 1e-6
N_MEM = 256
XA_HEADS = 4
XA_WIDTH = XA_HEADS * HEAD_DIM
HG_CHUNK = 64
DSA_LATENT = 128
IDX_HEADS = 8
IDX_DIM = 32
DSA_TOPK_MAX = 256
CMP_BLOCK = 32
CMP_STRIDE = 16
SLC_BLOCK = 64
SLC_SHIFT = 6
SLC_TOPN = 16
WINDOW = 512
CONV_WIDTH = 4
D_FF = 2816

HG_SPLITS = (GROUP_WIDTH,) * 4
DSA_SPLITS = (GROUP_WIDTH, DSA_LATENT, IDX_HEADS * IDX_DIM, IDX_DIM, IDX_HEADS)
NSA_SPLITS = (GROUP_WIDTH,) + (HEAD_DIM,) * 6 + (3 * GROUP_HEADS,)
ML_SPLITS = (2 * GROUP_WIDTH, GROUP_WIDTH, GROUP_WIDTH, GROUP_HEADS, GROUP_HEADS)
GROUP_COLS = (sum(HG_SPLITS), sum(DSA_SPLITS), sum(NSA_SPLITS), sum(ML_SPLITS))
IN_COLS = sum(GROUP_COLS)

V7X_VMEM_LIMIT_BYTES = 56 * 1024 * 1024
LANES = 128
FF_CHUNK = 256


def _round_up(n, m):
    return -(-n // m) * m


def _rms_rows(t, g):
    return t * lax.rsqrt(jnp.mean(t * t, axis=-1, keepdims=True) + EPS) * g


def _const_spec(shape):
    return pl.BlockSpec(shape, lambda *_: (0,) * len(shape), pipeline_mode=pl.Buffered(1))


def _norm_matmul_kernel(x_ref, g_ref, w_ref, o_ref):
    h = _rms_rows(x_ref[...], g_ref[...]).astype(BF16)
    o_ref[...] = jnp.dot(h, w_ref[...], preferred_element_type=F32)


def norm_matmul(x2d, gain, w_bf16, tm):
    m, k = x2d.shape
    n = w_bf16.shape[1]
    return pl.pallas_call(
        _norm_matmul_kernel,
        out_shape=jax.ShapeDtypeStruct((m, n), F32),
        grid=(m // tm,),
        in_specs=[
            pl.BlockSpec((tm, k), lambda i: (i, 0)),
            _const_spec((1, k)),
            _const_spec((k, n)),
        ],
        out_specs=pl.BlockSpec((tm, n), lambda i: (i, 0)),
        compiler_params=pltpu.CompilerParams(
            dimension_semantics=("arbitrary",), vmem_limit_bytes=V7X_VMEM_LIMIT_BYTES),
        name="norm_matmul",
    )(x2d, gain.reshape(1, k), w_bf16)


def _mem_kv_kernel(m_ref, g_ref, w_ref, kg_ref, k_ref, v_ref):
    mn = _rms_rows(m_ref[...], g_ref[...]).astype(BF16)
    kv = jnp.dot(mn, w_ref[...], preferred_element_type=F32)
    kg = kg_ref[...]
    ks = []
    for h in range(XA_HEADS):
        kh = kv[:, h * HEAD_DIM:(h + 1) * HEAD_DIM]
        ks.append(_rms_rows(kh, kg))
    k_ref[...] = jnp.concatenate(ks, axis=-1).astype(BF16)
    v_ref[...] = kv[:, XA_WIDTH:].astype(BF16)


def mem_kv(mem2d, gain, wkv_bf16, k_gain):
    m, k = mem2d.shape
    return pl.pallas_call(
        _mem_kv_kernel,
        out_shape=(jax.ShapeDtypeStruct((m, XA_WIDTH), BF16),
                   jax.ShapeDtypeStruct((m, XA_WIDTH), BF16)),
        grid=(m // N_MEM,),
        in_specs=[
            pl.BlockSpec((N_MEM, k), lambda i: (i, 0)),
            _const_spec((1, k)),
            _const_spec((k, 2 * XA_WIDTH)),
            _const_spec((1, HEAD_DIM)),
        ],
        out_specs=(pl.BlockSpec((N_MEM, XA_WIDTH), lambda i: (i, 0)),
                   pl.BlockSpec((N_MEM, XA_WIDTH), lambda i: (i, 0))),
        compiler_params=pltpu.CompilerParams(
            dimension_semantics=("arbitrary",), vmem_limit_bytes=V7X_VMEM_LIMIT_BYTES),
        name="mem_kv",
    )(mem2d, gain.reshape(1, k), wkv_bf16, k_gain.reshape(1, HEAD_DIM))


def _post_kernel(x_ref, mhg_ref, mdsa_ref, mnsa_ref, mml_ref, wout_ref, gxa_ref, wq_ref, qg_ref,
                 k_ref, v_ref, wo_ref, gffn_ref, w13_ref, w2_ref, o_ref):
    x = x_ref[...]
    for g, mix_ref in enumerate((mhg_ref, mdsa_ref, mnsa_ref, mml_ref)):
        x = x + _dot(mix_ref[...].astype(BF16),
                     wout_ref[g * GROUP_WIDTH:(g + 1) * GROUP_WIDTH, :])
    h = _rms_rows(x, gxa_ref[...]).astype(BF16)
    q = _dot(h, wq_ref[...])
    qg = qg_ref[...] * (HEAD_DIM ** -0.5)
    k = k_ref[...]
    v = v_ref[...]
    heads = [slice(hd * HEAD_DIM, (hd + 1) * HEAD_DIM) for hd in range(XA_HEADS)]
    scores = [_dot_nt(_rms_rows(q[:, sl], qg).astype(BF16), k[:, sl]) for sl in heads]
    probs = []
    for s in scores:
        e = jnp.exp(s - jnp.max(s, axis=-1, keepdims=True))
        probs.append((e / jnp.sum(e, axis=-1, keepdims=True)).astype(BF16))
    o = jnp.concatenate([_dot(p, v[:, sl]) for p, sl in zip(probs, heads)], axis=-1)
    x = x + _dot(o.astype(BF16), wo_ref[...])
    h = _rms_rows(x, gffn_ref[...]).astype(BF16)

    def up(c):
        return (_dot(h, w13_ref[:, c * FF_CHUNK:(c + 1) * FF_CHUNK]),
                _dot(h, w13_ref[:, D_FF + c * FF_CHUNK:D_FF + (c + 1) * FF_CHUNK]))

    n_chunks = D_FF // FF_CHUNK
    acc = x
    a, b = up(0)
    for c in range(n_chunks):
        nxt = up(c + 1) if c + 1 < n_chunks else None
        act = (a * jax.nn.sigmoid(a) * b).astype(BF16)
        acc = acc + _dot(act, w2_ref[c * FF_CHUNK:(c + 1) * FF_CHUNK, :])
        if nxt is not None:
            a, b = nxt
    o_ref[...] = acc


def post_block(x2d, mixers, seq, wout, gxa, wq, qg, k_mem, v_mem, wo, gffn, w13, w2, tm):
    m, d = x2d.shape
    tiles_per_batch = seq // tm
    row = lambda i: (i, 0)
    mem_row = lambda i: (i // tiles_per_batch, 0)
    return pl.pallas_call(
        _post_kernel,
        out_shape=jax.ShapeDtypeStruct((m, d), F32),
        grid=(m // tm,),
        in_specs=[
            pl.BlockSpec((tm, d), row),
            *[pl.BlockSpec((tm, GROUP_WIDTH), row) for _ in mixers],
            _const_spec(wout.shape),
            _const_spec((1, d)),
            _const_spec(wq.shape),
            _const_spec((1, HEAD_DIM)),
            pl.BlockSpec((N_MEM, XA_WIDTH), mem_row),
            pl.BlockSpec((N_MEM, XA_WIDTH), mem_row),
            _const_spec(wo.shape),
            _const_spec((1, d)),
            _const_spec(w13.shape),
            _const_spec(w2.shape),
        ],
        out_specs=pl.BlockSpec((tm, d), row),
        compiler_params=pltpu.CompilerParams(
            dimension_semantics=("arbitrary",), vmem_limit_bytes=V7X_VMEM_LIMIT_BYTES),
        name="post_block",
    )(x2d, *mixers, wout, gxa.reshape(1, d), wq, qg.reshape(1, HEAD_DIM), k_mem, v_mem, wo,
      gffn.reshape(1, d), w13, w2)


C_HG = 0
C_DSA_Q = 1024
C_DSA_IQ = 1280
C_NSA_Q = 1536
C_ML_V = 1792
C_ML_QK = 2048
C_ML_OG = 2560
C_DSA_CKV = 2816
C_DSA_IK = 2944
C_DSA_IW = 3072
C_NSA_GATE = 3200
C_NSA_CMP = 3328
C_NSA_SLC = 3456
C_NSA_WIN = 3584
C_ML_GATE = 3712
IN_COLS_PACKED = 3840


def _packed_source_columns():
    dsa0 = GROUP_COLS[0]
    nsa0 = dsa0 + GROUP_COLS[1]
    ml0 = nsa0 + GROUP_COLS[2]
    segs = [(0, 1024, 1024), (dsa0, 256, 256), (dsa0 + 384, 256, 256), (nsa0, 256, 256),
            (ml0 + 512, 256, 256), (ml0, 512, 512), (ml0 + 768, 256, 256), (dsa0 + 256, 128, 128),
            (dsa0 + 640, 32, LANES), (dsa0 + 672, 8, LANES), (nsa0 + 640, 12, LANES),
            (nsa0 + 256, 128, 128), (nsa0 + 384, 128, 128), (nsa0 + 512, 128, 128),
            (ml0 + 1024, 8, LANES)]
    src = np.concatenate([np.concatenate([np.arange(a, a + n), np.full(width - n, -1)])
                          for a, n, width in segs]).astype(np.int32)
    assert src.shape == (IN_COLS_PACKED,)
    return src


PACK_TN = 256


def _pack_w_kernel(w_ref, src_ref, o_ref, wb_ref):
    @pl.when(pl.program_id(0) == 0)
    def _():
        wb_ref[...] = w_ref[...].astype(BF16)

    row = lax.broadcasted_iota(jnp.int32, (wb_ref.shape[1], PACK_TN), 0)
    sel = jnp.where(row == src_ref[...], 1.0, 0.0).astype(BF16)
    o_ref[...] = _dot(wb_ref[...], sel).astype(BF16)


def pack_w_in_bf16(w):
    d, n = w.shape
    n_pad = _round_up(n, LANES)
    w_pad = jnp.pad(w, ((0, 0), (0, n_pad - n)))
    src = jnp.asarray(_packed_source_columns()).reshape(1, IN_COLS_PACKED)
    return pl.pallas_call(
        _pack_w_kernel,
        out_shape=jax.ShapeDtypeStruct((d, IN_COLS_PACKED), BF16),
        grid=(IN_COLS_PACKED // PACK_TN,),
        in_specs=[_const_spec((d, n_pad)), pl.BlockSpec((1, PACK_TN), lambda j: (0, j))],
        out_specs=pl.BlockSpec((d, PACK_TN), lambda j: (0, j)),
        scratch_shapes=[pltpu.VMEM((d, n_pad), BF16)],
        compiler_params=pltpu.CompilerParams(
            dimension_semantics=("arbitrary",), vmem_limit_bytes=V7X_VMEM_LIMIT_BYTES),
        name="pack_w_in",
    )(w_pad, src)


def rope_tables(l_, d=HEAD_DIM):
    rd = d // 4
    half = rd // 2
    inv = ROPE_THETA ** (-jnp.arange(half, dtype=F32) * 2.0 / rd)
    ang = jnp.arange(l_).astype(F32)[:, None] * inv[None, :]
    cos, sin = lax.optimization_barrier((jnp.cos(ang), jnp.sin(ang)))
    zh = jnp.zeros((l_, half), F32)
    rest0 = jnp.zeros((l_, d - rd), F32)
    c = jnp.concatenate([cos, cos, rest0 + 1.0], axis=1)
    s1 = jnp.concatenate([-sin, zh, rest0], axis=1)
    s2 = jnp.concatenate([zh, sin, rest0], axis=1)
    return jnp.stack([c, s1, s2])


def rope_tables_pad(rt, width):
    n = width - rt.shape[-1]
    ident = jnp.stack([jnp.ones(rt.shape[1:2] + (n,), F32), jnp.zeros(rt.shape[1:2] + (n,), F32),
                       jnp.zeros(rt.shape[1:2] + (n,), F32)])
    return jnp.concatenate([rt, ident], axis=-1)


def rope_tables_kv_pair(rt):
    return rope_tables_pad(rt, 2 * rt.shape[-1])


def _apply_rope(t, rope_ref, half=HEAD_DIM // 8):
    w = t.shape[-1]
    return (t * rope_ref[0] + pltpu.roll(t, w - half, 1) * rope_ref[1]
            + pltpu.roll(t, half, 1) * rope_ref[2])


_NT = (((1,), (1,)), ((), ()))


def _dot_nt(a, b):
    return lax.dot_general(a, b, _NT, preferred_element_type=F32)


def _dot(a, b):
    return jnp.dot(a, b, preferred_element_type=F32)


NEG_BIG = -(2.0 ** 30)
NSA_KEY_TILE = 512
NSA_TQ = 128
NSA_SEL_TQ = 256
NSA_PREP_TK = 512


def _nsa_kv_prep_kernel(ps_ref, pw_ref, rope_ref, gs_ref, gw_ref, kaug_ref, vs_ref, kw_ref, vw_ref):
    tk = ps_ref.shape[0]
    lane = lax.broadcasted_iota(jnp.int32, (tk, LANES), 1)
    is_k = lane < HEAD_DIM

    def norm_rope(p, g):
        ms = jnp.sum(jnp.where(is_k, p * p, 0.0), axis=-1, keepdims=True) * (1.0 / HEAD_DIM)
        y = jnp.where(is_k, p * lax.rsqrt(ms + EPS) * g, p)
        return _apply_rope(y, rope_ref)

    ys = norm_rope(ps_ref[...], gs_ref[...])
    yw = norm_rope(pw_ref[...], gw_ref[...])
    row = pl.program_id(1) * tk + lax.broadcasted_iota(jnp.int32, (tk, LANES), 0)
    ind = jnp.where(jnp.right_shift(row, SLC_SHIFT) == (lane - HEAD_DIM), 1.0, 0.0)
    kaug_ref[0] = jnp.where(is_k, ys, ind).astype(BF16)
    vs_ref[0] = ys.T[HEAD_DIM:, :].astype(BF16)
    kw_ref[0] = yw[:, :HEAD_DIM].astype(BF16)
    vw_ref[0] = yw.T[HEAD_DIM:, :].astype(BF16)


def nsa_kv_prep(cols, b_, l_, rope_pair, g_slc, g_win):
    tk = NSA_PREP_TK
    nt = l_ // tk
    ones = jnp.ones((HEAD_DIM,), F32)
    gs = jnp.concatenate([g_slc, ones]).reshape(1, LANES)
    gw = jnp.concatenate([g_win, ones]).reshape(1, LANES)
    kv = lambda w: jax.ShapeDtypeStruct((b_, l_, w), BF16)
    kv_t = jax.ShapeDtypeStruct((b_, HEAD_DIM, l_), BF16)
    out_blk = lambda w: pl.BlockSpec((1, tk, w), lambda b, i: (b, i, 0))
    out_t = pl.BlockSpec((1, HEAD_DIM, tk), lambda b, i: (b, 0, i))
    return pl.pallas_call(
        _nsa_kv_prep_kernel,
        out_shape=(kv(LANES), kv_t, kv(HEAD_DIM), kv_t),
        grid=(b_, nt),
        in_specs=[
            pl.BlockSpec((tk, LANES), lambda b, i: (b * nt + i, C_NSA_SLC // LANES)),
            pl.BlockSpec((tk, LANES), lambda b, i: (b * nt + i, C_NSA_WIN // LANES)),
            pl.BlockSpec((3, tk, LANES), lambda b, i: (0, i, 0)),
            pl.BlockSpec((1, LANES), lambda b, i: (0, 0)),
            pl.BlockSpec((1, LANES), lambda b, i: (0, 0)),
        ],
        out_specs=(out_blk(LANES), out_t, out_blk(HEAD_DIM), out_t),
        compiler_params=pltpu.CompilerParams(
            dimension_semantics=("arbitrary", "arbitrary"),
            vmem_limit_bytes=V7X_VMEM_LIMIT_BYTES),
        name="nsa_kv_prep",
    )(cols, cols, rope_pair, gs, gw)


def _nsa_compress_kernel(r_ref, pea_ref, peb_ref, w1a_ref, w1b_ref, w2k_ref, w2v_ref, kg_ref,
                         kc_ref, vc_ref):
    r = r_ref[0]
    n = r.shape[0]
    a = _dot((r + pea_ref[...]).astype(BF16), w1a_ref[...])
    bm = _dot((r + peb_ref[...]).astype(BF16), w1b_ref[...])
    row = lax.broadcasted_iota(jnp.int32, bm.shape, 0)
    bm_up = jnp.where(row < n - 1, pltpu.roll(bm, n - 1, 0), 0.0)
    h = jnp.maximum(a + bm_up, 0.0).astype(BF16)
    hid = w2k_ref.shape[0]
    ck = _dot(h[:, :hid], w2k_ref[...])
    cv = _dot(h[:, hid:], w2v_ref[...])
    kc_ref[0] = _rms_rows(ck, kg_ref[...]).astype(BF16)
    vc_ref[0] = cv.astype(BF16)


def nsa_compress(cols, b_, l_, pos_k, pos_v, k_w1, k_w2, v_w1, v_w2, k_gain):
    rows = l_ // CMP_STRIDE
    hid = k_w1.shape[1]
    pair = cols[:, C_NSA_CMP:C_NSA_CMP + LANES].reshape(b_, rows, CMP_STRIDE * LANES)

    def interleave_pe(lo):
        pe = jnp.concatenate([pos_k[lo:lo + CMP_STRIDE], pos_v[lo:lo + CMP_STRIDE]], axis=1)
        return pe.reshape(1, CMP_STRIDE * LANES)

    def interleave_w(lo):
        wk = k_w1[lo * HEAD_DIM:(lo + CMP_STRIDE) * HEAD_DIM].reshape(CMP_STRIDE, HEAD_DIM, hid)
        wv = v_w1[lo * HEAD_DIM:(lo + CMP_STRIDE) * HEAD_DIM].reshape(CMP_STRIDE, HEAD_DIM, hid)
        z = jnp.zeros_like(wk)
        top = jnp.concatenate([wk, z], axis=2)
        bot = jnp.concatenate([z, wv], axis=2)
        return jnp.concatenate([top, bot], axis=1).reshape(CMP_STRIDE * LANES, 2 * hid).astype(BF16)

    out = jax.ShapeDtypeStruct((b_, rows, HEAD_DIM), BF16)
    return pl.pallas_call(
        _nsa_compress_kernel,
        out_shape=(out, out),
        grid=(b_,),
        in_specs=[
            pl.BlockSpec((1, rows, CMP_STRIDE * LANES), lambda b: (b, 0, 0)),
            _const_spec((1, CMP_STRIDE * LANES)),
            _const_spec((1, CMP_STRIDE * LANES)),
            _const_spec((CMP_STRIDE * LANES, 2 * hid)),
            _const_spec((CMP_STRIDE * LANES, 2 * hid)),
            _const_spec((hid, HEAD_DIM)),
            _const_spec((hid, HEAD_DIM)),
            _const_spec((1, HEAD_DIM)),
        ],
        out_specs=(pl.BlockSpec((1, rows, HEAD_DIM), lambda b: (b, 0, 0)),
                   pl.BlockSpec((1, rows, HEAD_DIM), lambda b: (b, 0, 0))),
        compiler_params=pltpu.CompilerParams(
            dimension_semantics=("arbitrary",), vmem_limit_bytes=V7X_VMEM_LIMIT_BYTES),
        name="nsa_compress",
    )(pair, interleave_pe(0), interleave_pe(CMP_STRIDE), interleave_w(0),
      interleave_w(CMP_STRIDE), k_w2.astype(BF16), v_w2.astype(BF16),
      k_gain.reshape(1, HEAD_DIM))


def _masked_softmax(s, valid, axis):
    m = jnp.max(jnp.where(valid, s, -jnp.inf), axis=axis, keepdims=True)
    m = jnp.where(m == -jnp.inf, 0.0, m)
    e = jnp.where(valid, jnp.exp(s - m), 0.0)
    den = jnp.sum(e, axis=axis, keepdims=True)
    return e / jnp.where(den > 0, den, 1.0)


def _nsa_select_kernel(q_ref, kc_ref, vc_ref, ovt_ref, rope_ref, qg_ref, qaug_ref, ocmp_ref,
                       *, n_sel):
    tq = q_ref.shape[0]
    ncr = kc_ref.shape[1]
    n_slc = ovt_ref.shape[0]
    t0 = pl.program_id(1) * tq
    scale = HEAD_DIM ** -0.5
    q = q_ref[...]
    g = qg_ref[...]
    kc = kc_ref[0]
    vc = vc_ref[0]
    vis_t = (lax.broadcasted_iota(jnp.int32, (ncr, tq), 0) * CMP_STRIDE + (CMP_BLOCK - 1)
             <= t0 + lax.broadcasted_iota(jnp.int32, (ncr, tq), 1))
    qn_heads, o_heads = [], []
    psum_t = jnp.zeros((ncr, tq), F32)
    for h in range(GROUP_HEADS):
        qn = _rms_rows(q[:, h * HEAD_DIM:(h + 1) * HEAD_DIM], g)
        qn_heads.append(qn)
        p_t = _masked_softmax(_dot_nt(kc, (qn * scale).astype(BF16)), vis_t, 0)
        o_heads.append(_dot_tn(vc, p_t.astype(BF16)))
        psum_t = psum_t + p_t
    ocmp_ref[0] = jnp.concatenate(o_heads, axis=0)

    hi = psum_t.astype(BF16)
    lo = (psum_t - hi.astype(F32)).astype(BF16)
    imp = _dot(ovt_ref[...], hi) + _dot(ovt_ref[...], lo)
    blk = lax.broadcasted_iota(jnp.int32, (n_slc, tq), 0)
    cur = jnp.right_shift(t0 + lax.broadcasted_iota(jnp.int32, (n_slc, tq), 1), SLC_SHIFT)
    forced = (blk == 0) | (blk == cur) | (blk == cur - 1)
    imp = jnp.where(forced, jnp.inf, jnp.where(blk > cur, -jnp.inf, imp))
    n_grp = n_slc // SUBLANES
    groups = [imp[g * SUBLANES:(g + 1) * SUBLANES] for g in range(n_grp)]
    cnts = [jnp.zeros((SUBLANES, tq), F32) for _ in range(n_grp)]
    sub = lax.broadcasted_iota(jnp.int32, (SUBLANES, tq), 0)
    for m in range(n_slc):
        row = imp[m:m + 1, :]
        gm, rm = divmod(m, SUBLANES)
        for g in range(n_grp):
            if g < gm:
                beats = row > groups[g]
            elif g > gm:
                beats = row >= groups[g]
            else:
                beats = (row > groups[g]) | ((row == groups[g]) & (sub > rm))
            cnts[g] = cnts[g] + jnp.where(beats, 1.0, 0.0)
    cnt = jnp.concatenate(cnts, axis=0)
    mt = jnp.where(cnt < n_sel, 0.0, NEG_BIG)
    pad = jnp.zeros((HEAD_DIM - n_slc, tq), F32)
    mt = jnp.concatenate([mt, pad, mt, pad], axis=0) if n_slc < HEAD_DIM else jnp.concatenate(
        [mt, mt], axis=0)
    mt = mt.T

    qr = _apply_rope(jnp.concatenate(qn_heads, axis=-1), rope_ref) * scale
    lane = lax.broadcasted_iota(jnp.int32, (tq, LANES), 1)
    for j in range(GROUP_HEADS // 2):
        pair = qr[:, j * LANES:(j + 1) * LANES]
        swapped = pltpu.roll(pair, HEAD_DIM, 1)
        qaug_ref[0, :, (2 * j) * LANES:(2 * j + 1) * LANES] = jnp.where(
            lane < HEAD_DIM, pair, mt).astype(BF16)
        qaug_ref[0, :, (2 * j + 1) * LANES:(2 * j + 2) * LANES] = jnp.where(
            lane < HEAD_DIM, swapped, mt).astype(BF16)


def nsa_select(cols, b_, l_, k_cmp, v_cmp, rope_q, q_gain):
    tq = min(NSA_SEL_TQ, l_)
    nt = l_ // tq
    ncr = l_ // CMP_STRIDE
    n_slc = l_ // SLC_BLOCK
    n_sel = min(SLC_TOPN, n_slc)
    st_c = np.arange(ncr) * CMP_STRIDE
    st_s = np.arange(n_slc) * SLC_BLOCK
    ovt = ((st_c[None, :] < st_s[:, None] + SLC_BLOCK)
           & (st_c[None, :] + CMP_BLOCK > st_s[:, None])).astype(np.float32)
    return pl.pallas_call(
        functools.partial(_nsa_select_kernel, n_sel=n_sel),
        out_shape=(jax.ShapeDtypeStruct((b_, l_, GROUP_HEADS * LANES), BF16),
                   jax.ShapeDtypeStruct((b_, GROUP_WIDTH, l_), F32)),
        grid=(b_, nt),
        in_specs=[
            pl.BlockSpec((tq, GROUP_WIDTH), lambda b, i: (b * nt + i, C_NSA_Q // GROUP_WIDTH)),
            pl.BlockSpec((1, ncr, HEAD_DIM), lambda b, i: (b, 0, 0)),
            pl.BlockSpec((1, ncr, HEAD_DIM), lambda b, i: (b, 0, 0)),
            pl.BlockSpec((n_slc, ncr), lambda b, i: (0, 0)),
            pl.BlockSpec((3, tq, GROUP_WIDTH), lambda b, i: (0, i, 0)),
            pl.BlockSpec((1, HEAD_DIM), lambda b, i: (0, 0)),
        ],
        out_specs=(pl.BlockSpec((1, tq, GROUP_HEADS * LANES), lambda b, i: (b, i, 0)),
                   pl.BlockSpec((1, GROUP_WIDTH, tq), lambda b, i: (b, 0, i))),
        compiler_params=pltpu.CompilerParams(
            dimension_semantics=("arbitrary", "arbitrary"),
            vmem_limit_bytes=V7X_VMEM_LIMIT_BYTES),
        name="nsa_select",
    )(cols, k_cmp, v_cmp, jnp.asarray(ovt, BF16), rope_q, q_gain.reshape(1, HEAD_DIM))


def _softmax_stats_update(s, m, l):
    m_new = jnp.maximum(m, jnp.max(s, axis=0, keepdims=True))
    alpha = jnp.exp(m - m_new)
    p = jnp.exp(s - m_new)
    return m_new, alpha, p, alpha * l + jnp.sum(p, axis=0, keepdims=True)


def _nsa_attend_t_kernel(qaug_ref, kaug_ref, vst_ref, kw_ref, vwt_ref, ocmp_ref, gate_ref, o_ref):
    tq = qaug_ref.shape[1]
    nh = GROUP_HEADS
    ts = NSA_KEY_TILE
    t0 = pl.program_id(1) * tq
    qa = qaug_ref[0]
    qs = jnp.concatenate([qa[:, h * LANES:(h + 1) * LANES] for h in range(nh)], axis=0)
    n = nh * tq
    qpos_tile = t0 + lax.broadcasted_iota(jnp.int32, (1, tq), 1)
    qpos = jnp.concatenate([qpos_tile] * nh, axis=1)

    def key_tile(kt, carry):
        m, l, acc = carry
        k0 = pl.multiple_of(kt * ts, ts)
        s = _dot_nt(kaug_ref[0, pl.ds(k0, ts), :], qs)
        kpos = k0 + lax.broadcasted_iota(jnp.int32, (ts, n), 0)
        s = jnp.where(kpos <= qpos, s, NEG_BIG)
        m, alpha, p, l = _softmax_stats_update(s, m, l)
        acc = alpha * acc + _dot(vst_ref[0, :, pl.ds(k0, ts)], p.astype(BF16))
        return m, l, acc

    n_kt = (t0 + tq + ts - 1) // ts
    init = (jnp.full((1, n), -jnp.inf, F32), jnp.zeros((1, n), F32),
            jnp.zeros((HEAD_DIM, n), F32))
    _, l, acc = lax.fori_loop(0, n_kt, key_tile, init)
    o_slc = acc / l

    wlen = WINDOW + tq
    start = pl.multiple_of(jnp.maximum(t0 - WINDOW, 0), tq)
    sw = _dot_nt(kw_ref[0, pl.ds(start, wlen), :], qs[:, :HEAD_DIM])
    dist = qpos - (start + lax.broadcasted_iota(jnp.int32, (wlen, n), 0))
    sw = jnp.where((dist >= 0) & (dist < WINDOW), sw, -jnp.inf)
    e = jnp.exp(sw - jnp.max(sw, axis=0, keepdims=True))
    o_swa = _dot(vwt_ref[0, :, pl.ds(start, wlen)], e.astype(BF16)) / jnp.sum(
        e, axis=0, keepdims=True)

    g = jax.nn.sigmoid(gate_ref[...]).T
    oc = ocmp_ref[0]
    outs = []
    for h in range(nh):
        cols_h = slice(h * tq, (h + 1) * tq)
        outs.append(g[h:h + 1, :] * oc[h * HEAD_DIM:(h + 1) * HEAD_DIM, :]
                    + g[nh + h:nh + h + 1, :] * o_slc[:, cols_h]
                    + g[2 * nh + h:2 * nh + h + 1, :] * o_swa[:, cols_h])
    o_ref[...] = jnp.concatenate(outs, axis=0).T


def nsa_attend(cols, b_, l_, q_aug, k_aug, v_slc, k_win, v_win, o_cmp):
    tq = NSA_TQ
    nt = l_ // tq
    seq = lambda w: pl.BlockSpec((1, l_, w), lambda b, i: (b, 0, 0))
    seq_t = pl.BlockSpec((1, HEAD_DIM, l_), lambda b, i: (b, 0, 0))
    return pl.pallas_call(
        _nsa_attend_t_kernel,
        out_shape=jax.ShapeDtypeStruct((b_ * l_, GROUP_WIDTH), F32),
        grid=(b_, nt),
        in_specs=[
            pl.BlockSpec((1, tq, GROUP_HEADS * LANES), lambda b, i: (b, i, 0)),
            seq(LANES), seq_t, seq(HEAD_DIM), seq_t,
            pl.BlockSpec((1, GROUP_WIDTH, tq), lambda b, i: (b, 0, i)),
            pl.BlockSpec((tq, LANES), lambda b, i: (b * nt + i, C_NSA_GATE // LANES)),
        ],
        out_specs=pl.BlockSpec((tq, GROUP_WIDTH), lambda b, i: (b * nt + i, 0)),
        compiler_params=pltpu.CompilerParams(
            dimension_semantics=("arbitrary", "arbitrary"),
            vmem_limit_bytes=V7X_VMEM_LIMIT_BYTES),
        name="nsa_attend",
    )(q_aug, k_aug, v_slc, k_win, v_win, o_cmp, cols)


def nsa_mixer_pallas(cols, b_, l_, rope, pos_k, pos_v, k_w1, k_w2, v_w1, v_w2, q_gain, k_gains):
    k_aug, v_slc, k_win, v_win = nsa_kv_prep(cols, b_, l_, rope_tables_kv_pair(rope),
                                             k_gains[1], k_gains[2])
    k_cmp, v_cmp = nsa_compress(cols, b_, l_, pos_k, pos_v, k_w1, k_w2, v_w1, v_w2, k_gains[0])
    q_aug, o_cmp = nsa_select(cols, b_, l_, k_cmp, v_cmp, jnp.tile(rope, (1, 1, GROUP_HEADS)),
                              q_gain)
    return nsa_attend(cols, b_, l_, q_aug, k_aug, v_slc, k_win, v_win, o_cmp)


DSA_TQ = 128
DSA_KEY_TILE = 512
DSA_PREP_TK = 512
IDX_PACK = LANES
INT_MIN = -2 ** 31
MASKED_SCORE = -1e30


def _split_hi_lo(t):
    hi = t.astype(BF16)
    lo = (t - hi.astype(F32)).astype(BF16)
    return hi, lo


def _placement(rows, cols, pairs):
    p = np.zeros((rows, cols), np.float32)
    for r, c in pairs:
        p[r, c] = 1.0
    return jnp.asarray(p, BF16)


def _dsa_kv_prep_kernel(ckv_ref, ik_ref, rope_ref, ropei_ref, kvg_ref, wkv_ref, kg_ref, ikg_ref,
                        pkh_ref, pkl_ref, k_ref, vt_ref, ik3_ref):
    tk = ckv_ref.shape[0]
    lane = lax.broadcasted_iota(jnp.int32, (tk, LANES), 1)
    ckv = _rms_rows(ckv_ref[...], kvg_ref[...]).astype(BF16)
    kv = _dot(ckv, wkv_ref[...])
    is_k = lane < HEAD_DIM
    ms = jnp.sum(jnp.where(is_k, kv * kv, 0.0), axis=-1, keepdims=True) * (1.0 / HEAD_DIM)
    y = _apply_rope(jnp.where(is_k, kv * lax.rsqrt(ms + EPS) * kg_ref[...], kv), rope_ref)
    k_ref[0] = y[:, :HEAD_DIM].astype(BF16)
    vt_ref[0] = y.T[HEAD_DIM:, :].astype(BF16)
    ik = ik_ref[...]
    ms = jnp.sum(ik * ik, axis=-1, keepdims=True) * (1.0 / IDX_DIM)
    ikn = _apply_rope(ik * lax.rsqrt(ms + EPS) * ikg_ref[...], ropei_ref, IDX_DIM // 8)
    hi, lo = _split_hi_lo(ikn)
    ik3_ref[0] = (_dot(hi, pkh_ref[...]) + _dot(lo, pkl_ref[...])).astype(BF16)


def dsa_kv_prep(cols, b_, l_, rope_pair, rope_idx, kv_gain, w_uk, w_uv, k_gain, idxk_gain):
    tk = DSA_PREP_TK
    nt = l_ // tk
    ones = jnp.ones((HEAD_DIM,), F32)
    kg = jnp.concatenate([k_gain, ones]).reshape(1, LANES)
    ikg = jnp.pad(idxk_gain, (0, LANES - IDX_DIM)).reshape(1, LANES)
    wkv = jnp.concatenate([w_uk, w_uv], axis=1).astype(BF16)
    d = range(IDX_DIM)
    pkh = _placement(LANES, IDX_PACK, [(i, i) for i in d] + [(i, 2 * IDX_DIM + i) for i in d])
    pkl = _placement(LANES, IDX_PACK, [(i, IDX_DIM + i) for i in d])
    const = lambda shape: pl.BlockSpec(shape, lambda b, i: (0,) * len(shape))
    out = lambda w: pl.BlockSpec((1, tk, w), lambda b, i: (b, i, 0))
    return pl.pallas_call(
        _dsa_kv_prep_kernel,
        out_shape=(jax.ShapeDtypeStruct((b_, l_, HEAD_DIM), BF16),
                   jax.ShapeDtypeStruct((b_, HEAD_DIM, l_), BF16),
                   jax.ShapeDtypeStruct((b_, l_, IDX_PACK), BF16)),
        grid=(b_, nt),
        in_specs=[
            pl.BlockSpec((tk, LANES), lambda b, i: (b * nt + i, C_DSA_CKV // LANES)),
            pl.BlockSpec((tk, LANES), lambda b, i: (b * nt + i, C_DSA_IK // LANES)),
            pl.BlockSpec((3, tk, LANES), lambda b, i: (0, i, 0)),
            pl.BlockSpec((3, tk, LANES), lambda b, i: (0, i, 0)),
            const((1, LANES)), const((DSA_LATENT, LANES)), const((1, LANES)), const((1, LANES)),
            const((LANES, IDX_PACK)), const((LANES, IDX_PACK)),
        ],
        out_specs=(out(HEAD_DIM), pl.BlockSpec((1, HEAD_DIM, tk), lambda b, i: (b, 0, i)),
                   out(IDX_PACK)),
        compiler_params=pltpu.CompilerParams(
            dimension_semantics=("arbitrary", "arbitrary"),
            vmem_limit_bytes=V7X_VMEM_LIMIT_BYTES),
        name="dsa_kv_prep",
    )(cols, cols, rope_pair, rope_idx, kv_gain.reshape(1, DSA_LATENT), wkv, kg, ikg, pkh, pkl)


def _dsa_q_prep_kernel(q_ref, iq_ref, rope_ref, ropei_ref, qg_ref, pqh_ref, pql_ref,
                       qh_ref, iq3_ref):
    g = qg_ref[...]
    q = q_ref[...]
    qn = jnp.concatenate([_rms_rows(q[:, h * HEAD_DIM:(h + 1) * HEAD_DIM], g)
                          for h in range(GROUP_HEADS)], axis=-1)
    qr = _apply_rope(qn, rope_ref) * (HEAD_DIM ** -0.5)
    for h in range(GROUP_HEADS):
        qh_ref[0, h] = qr[:, h * HEAD_DIM:(h + 1) * HEAD_DIM].astype(BF16)
    hi, lo = _split_hi_lo(_apply_rope(iq_ref[...], ropei_ref, IDX_DIM // 8))
    iq3_ref[0] = (_dot(hi, pqh_ref[...]) + _dot(lo, pql_ref[...])).astype(BF16)


def dsa_q_prep(cols, b_, l_, rope_q, rope_iq, q_gain):
    tq = 256
    nt = l_ // tq
    w = IDX_HEADS * IDX_DIM
    hd = [(h, i) for h in range(IDX_HEADS) for i in range(IDX_DIM)]
    pqh = _placement(w, IDX_HEADS * IDX_PACK,
                     [(IDX_DIM * h + i, IDX_PACK * h + i) for h, i in hd]
                     + [(IDX_DIM * h + i, IDX_PACK * h + IDX_DIM + i) for h, i in hd])
    pql = _placement(w, IDX_HEADS * IDX_PACK,
                     [(IDX_DIM * h + i, IDX_PACK * h + 2 * IDX_DIM + i) for h, i in hd])
    const = lambda shape: pl.BlockSpec(shape, lambda b, i: (0,) * len(shape))
    return pl.pallas_call(
        _dsa_q_prep_kernel,
        out_shape=(jax.ShapeDtypeStruct((b_, GROUP_HEADS, l_, HEAD_DIM), BF16),
                   jax.ShapeDtypeStruct((b_, l_, IDX_HEADS * IDX_PACK), BF16)),
        grid=(b_, nt),
        in_specs=[
            pl.BlockSpec((tq, GROUP_WIDTH), lambda b, i: (b * nt + i, C_DSA_Q // GROUP_WIDTH)),
            pl.BlockSpec((tq, w), lambda b, i: (b * nt + i, C_DSA_IQ // w)),
            pl.BlockSpec((3, tq, GROUP_WIDTH), lambda b, i: (0, i, 0)),
            pl.BlockSpec((3, tq, w), lambda b, i: (0, i, 0)),
            const((1, HEAD_DIM)), const(pqh.shape), const(pql.shape),
        ],
        out_specs=(pl.BlockSpec((1, GROUP_HEADS, tq, HEAD_DIM), lambda b, i: (b, 0, i, 0)),
                   pl.BlockSpec((1, tq, IDX_HEADS * IDX_PACK), lambda b, i: (b, i, 0))),
        compiler_params=pltpu.CompilerParams(
            dimension_semantics=("arbitrary", "arbitrary"),
            vmem_limit_bytes=V7X_VMEM_LIMIT_BYTES),
        name="dsa_q_prep",
    )(cols, cols, rope_q, rope_iq, q_gain.reshape(1, HEAD_DIM), pqh, pql)


SUBLANES = 8


def _fold_rows(t, op, group=SUBLANES):
    parts = [t[i * group:(i + 1) * group] for i in range(t.shape[0] // group)]
    while len(parts) > 1:
        parts = [op(parts[i], parts[i + 1]) if i + 1 < len(parts) else parts[i]
                 for i in range(0, len(parts), 2)]
    return parts[0]


def _dsa_attend_t_kernel(qh_ref, iq3_ref, iw_ref, k_ref, vt_ref, ik3_ref, o_ref, sc_ref, *,
                         topk, idx_bits):
    tq = iq3_ref.shape[1]
    ts = DSA_KEY_TILE
    nh = GROUP_HEADS
    t0 = pl.program_id(1) * tq
    n_kt = (t0 + tq + ts - 1) // ts
    qpos = t0 + lax.broadcasted_iota(jnp.int32, (ts, tq), 1)
    krow = lax.broadcasted_iota(jnp.int32, (ts, tq), 0)

    iq3 = iq3_ref[0]
    iq_rows = jnp.concatenate([iq3[:, h * IDX_PACK:(h + 1) * IDX_PACK]
                               for h in range(IDX_HEADS)], axis=0)
    iw_t = (iw_ref[...] * (IDX_HEADS ** -0.5 * IDX_DIM ** -0.5)).T

    def score_tile(kt, _):
        k0 = pl.multiple_of(kt * ts, ts)
        rel = jnp.maximum(_dot_nt(ik3_ref[0, pl.ds(k0, ts), :], iq_rows), 0.0)
        sc = rel[:, 0:tq] * iw_t[0:1, :]
        for h in range(1, IDX_HEADS):
            sc = sc + rel[:, h * tq:(h + 1) * tq] * iw_t[h:h + 1, :]
        sc_ref[pl.ds(k0, ts), :] = jnp.where(k0 + krow <= qpos, sc, -jnp.inf)
        return 0

    lax.fori_loop(0, n_kt, score_tile, 0)

    def count(pred):
        def tile(kt, c):
            k0 = pl.multiple_of(kt * ts, ts)
            hit = jnp.where(pred(sc_ref[pl.ds(k0, ts), :], k0), 1.0, 0.0)
            return c + _fold_rows(hit, jnp.add)
        c = lax.fori_loop(0, n_kt, tile, jnp.zeros((SUBLANES, tq), F32))
        return jnp.sum(c, axis=0, keepdims=True)

    def key_to_float(key):
        return pltpu.bitcast(jnp.where(key >= 0, key, key ^ jnp.int32(0x7FFFFFFF)), F32)

    def value_bit(i, carry):
        thr_key, n_at = carry
        cand = thr_key | jnp.left_shift(jnp.int32(1), 31 - i)
        cand_f = key_to_float(cand ^ jnp.int32(INT_MIN))
        n = count(lambda sc, k0: sc >= cand_f)
        keep = n >= topk
        return jnp.where(keep, cand, thr_key), jnp.where(keep, n, n_at)

    thr_key, n_at = lax.fori_loop(
        0, 32, value_bit, (jnp.zeros((1, tq), jnp.int32), jnp.zeros((1, tq), F32)))
    few = t0 + lax.broadcasted_iota(jnp.int32, (1, tq), 1) + 1 < topk
    thr = jnp.where(few, jnp.finfo(F32).min, key_to_float(thr_key ^ jnp.int32(INT_MIN)))
    tie_break = jnp.max(jnp.where((n_at > topk) & jnp.logical_not(few), 1.0, 0.0)) > 0.0

    def last_tie_position():
        need = topk - count(lambda sc, k0: sc > thr)

        def index_bit(i, last):
            cand = last | jnp.left_shift(jnp.int32(1), idx_bits - 1 - i)
            tied_below = lambda sc, k0: (sc == thr) & (k0 + krow < cand)
            return jnp.where(count(tied_below) < need, cand, last)

        return lax.fori_loop(0, idx_bits, index_bit, jnp.zeros((1, tq), jnp.int32))

    qs = qh_ref[0].reshape(nh * tq, HEAD_DIM)

    def attend(selected):
        def key_tile(kt, carry):
            m, l, acc = carry
            k0 = pl.multiple_of(kt * ts, ts)
            bias = jnp.where(selected(sc_ref[pl.ds(k0, ts), :], k0), 0.0, MASKED_SCORE)
            s = _dot_nt(k_ref[0, pl.ds(k0, ts), :], qs) + jnp.concatenate([bias] * nh, axis=1)
            m, alpha, p, l = _softmax_stats_update(s, m, l)
            acc = alpha * acc + _dot(vt_ref[0, :, pl.ds(k0, ts)], p.astype(BF16))
            return m, l, acc

        init = (jnp.full((1, nh * tq), MASKED_SCORE, F32), jnp.zeros((1, nh * tq), F32),
                jnp.zeros((HEAD_DIM, nh * tq), F32))
        _, l, acc = lax.fori_loop(0, n_kt, key_tile, init)
        return l, acc

    def attend_with_ties():
        last = last_tie_position()
        return attend(lambda sc, k0: (sc > thr) | ((sc == thr) & (k0 + krow <= last)))

    l, acc = lax.cond(tie_break, attend_with_ties, lambda: attend(lambda sc, k0: sc >= thr))
    o_t = acc / l
    o_ref[...] = jnp.concatenate([o_t[:, h * tq:(h + 1) * tq] for h in range(nh)], axis=0).T


def dsa_attend(cols, b_, l_, qh, iq3, k, v, ik3):
    tq = DSA_TQ
    nt = l_ // tq
    topk = min(DSA_TOPK_MAX, l_ // 4)
    idx_bits = int(np.log2(l_))
    assert 2 ** idx_bits == l_ and l_ % DSA_KEY_TILE == 0 and topk <= DSA_KEY_TILE
    seq = lambda w: pl.BlockSpec((1, l_, w), lambda b, i: (b, 0, 0))
    return pl.pallas_call(
        functools.partial(_dsa_attend_t_kernel, topk=topk, idx_bits=idx_bits),
        out_shape=jax.ShapeDtypeStruct((b_ * l_, GROUP_WIDTH), F32),
        grid=(b_, nt),
        in_specs=[
            pl.BlockSpec((1, GROUP_HEADS, tq, HEAD_DIM), lambda b, i: (b, 0, i, 0)),
            pl.BlockSpec((1, tq, IDX_HEADS * IDX_PACK), lambda b, i: (b, i, 0)),
            pl.BlockSpec((tq, LANES), lambda b, i: (b * nt + i, C_DSA_IW // LANES)),
            seq(HEAD_DIM), pl.BlockSpec((1, HEAD_DIM, l_), lambda b, i: (b, 0, 0)), seq(IDX_PACK),
        ],
        out_specs=pl.BlockSpec((tq, GROUP_WIDTH), lambda b, i: (b * nt + i, 0)),
        scratch_shapes=[pltpu.VMEM((l_, tq), F32)],
        compiler_params=pltpu.CompilerParams(
            dimension_semantics=("arbitrary", "arbitrary"),
            vmem_limit_bytes=V7X_VMEM_LIMIT_BYTES),
        name="dsa_attend",
    )(qh, iq3, cols, k, v, ik3)


def dsa_mixer_pallas(cols, b_, l_, rope, kv_gain, w_uk, w_uv, q_gain, k_gain, idxk_gain):
    rope_i = rope_tables(l_, IDX_DIM)
    k, v, ik3 = dsa_kv_prep(cols, b_, l_, rope_tables_kv_pair(rope), rope_tables_pad(rope_i, LANES),
                            kv_gain, w_uk, w_uv, k_gain, idxk_gain)
    qh, iq3 = dsa_q_prep(cols, b_, l_, jnp.tile(rope, (1, 1, GROUP_HEADS)),
                         jnp.tile(rope_i, (1, 1, IDX_HEADS)), q_gain)
    return dsa_attend(cols, b_, l_, qh, iq3, k, v, ik3)


_TN = (((0,), (0,)), ((), ()))


def _dot_tn(a, b):
    return lax.dot_general(a, b, _TN, preferred_element_type=F32)


def _split3(t):
    hi = t.astype(BF16)
    r = t - hi.astype(F32)
    mid = r.astype(BF16)
    lo = (r - mid.astype(F32)).astype(BF16)
    return hi, mid, lo


def _tri_cumsum(tri, t):
    hi, mid, lo = _split3(t)
    return _dot(tri, hi) + _dot(tri, mid) + _dot(tri, lo)


def _cumsum_tri_rows(t, tri_u):
    hi, mid, lo = _split3(t)
    return _dot(hi, tri_u) + _dot(mid, tri_u) + _dot(lo, tri_u)


def _head_rms(o, gain):
    return jnp.concatenate([_rms_rows(o[:, h * HEAD_DIM:(h + 1) * HEAD_DIM], gain)
                            for h in range(GROUP_HEADS)], axis=-1)


HG_SUB = 8
HG_BLOCK = 16
LOG2_E = 1.4426950408889634


def _hgrn2_kernel(q_ref, f_ref, i_ref, g_ref, lb_ref, gain_ref, tri_ref, ones_ref, bd_ref,
                  hm_ref, jm_ref, o_ref, st_ref, b_ref, kk_ref, v_ref):
    c = HG_CHUNK
    w = GROUP_WIDTH

    @pl.when(pl.program_id(1) == 0)
    def _():
        st_ref[...] = jnp.zeros_like(st_ref)

    lb = lb_ref[...]
    q = q_ref[...]
    qs = q * jax.nn.sigmoid(q) * (HEAD_DIM ** -0.5)
    forget = lb + (1.0 - lb) * jax.nn.sigmoid(f_ref[...])
    kk = 1.0 - forget
    bcum = _tri_cumsum(tri_ref[...], jnp.log(forget))
    v = i_ref[...]
    b2 = bcum * LOG2_E
    b_ref[...] = b2
    kk_ref[...] = kk
    v_ref[...] = v

    out = _dot_nt((qs * jnp.exp(bcum)).astype(BF16), st_ref[...].astype(BF16))

    ones_bd = ones_ref[...]
    pieces = []
    for g in range(c // HG_SUB):
        r0 = g * HG_SUB
        nr = (r0 // HG_BLOCK + 1) * HG_BLOCK - r0
        qg = qs[r0:r0 + nr, :]
        bg = b2[r0:r0 + nr, :]
        trow = r0 + lax.broadcasted_iota(jnp.int32, (HG_SUB, w), 0)
        terms = []
        for j in range(HG_SUB):
            s = r0 + j
            d = qg * kk_ref[s:s + 1, :] * jnp.exp2(bg - b_ref[s:s + 1, :])
            if j > 0:
                head = jnp.where(trow >= s, d[:HG_SUB], 0.0)
                d = jnp.concatenate([head, d[HG_SUB:]], axis=0) if nr > HG_SUB else head
            terms.append(d.astype(BF16))
        red = _dot(jnp.concatenate(terms, axis=0), ones_bd)
        acc = red[0:nr] * v_ref[r0:r0 + 1, :]
        for j in range(1, HG_SUB):
            acc = acc + red[j * nr:(j + 1) * nr] * v_ref[r0 + j:r0 + j + 1, :]
        pieces.append((r0, acc))

    nb = c // HG_BLOCK
    hm = hm_ref[...]
    q_rows, k_rows, v_rows = [], [], []
    for j in range(nb - 1):
        blk = slice(j * HG_BLOCK, (j + 1) * HG_BLOCK)
        r_j = b_ref[(j + 1) * HG_BLOCK - 1:(j + 1) * HG_BLOCK, :]
        later = slice((j + 1) * HG_BLOCK, c)
        q_rows.append(qs[later] * jnp.exp2(b2[later] - r_j))
        k_blk = kk[blk] * jnp.exp2(r_j - b2[blk])
        k_rows.append(jnp.concatenate([k_blk] * GROUP_HEADS, axis=0) * hm)
        v_rows.append(jnp.concatenate([v[blk]] * GROUP_HEADS, axis=0) * hm)
    scores = _dot_nt(jnp.concatenate(q_rows, axis=0).astype(BF16),
                     jnp.concatenate(k_rows, axis=0).astype(BF16))
    scores = jnp.where(jm_ref[...] > 0.0, scores, 0.0)
    off = _dot(scores.astype(BF16), jnp.concatenate(v_rows, axis=0).astype(BF16))
    row0 = 0
    for j in range(nb - 1):
        n_later = c - (j + 1) * HG_BLOCK
        pieces.append(((j + 1) * HG_BLOCK, off[row0:row0 + n_later]))
        row0 += n_later

    for r0, piece in pieces:
        parts = [piece]
        if r0 > 0:
            parts.insert(0, jnp.zeros((r0, w), F32))
        if r0 + piece.shape[0] < c:
            parts.append(jnp.zeros((c - r0 - piece.shape[0], w), F32))
        out = out + (jnp.concatenate(parts, axis=0) if len(parts) > 1 else piece)

    b_last = bcum[c - 1:c, :]
    kt = (kk * jnp.exp(b_last - bcum)).astype(BF16)
    st_ref[...] = jnp.exp(b_last) * st_ref[...] + _dot_tn(v.astype(BF16), kt) * bd_ref[...]

    g_in = g_ref[...]
    o_ref[...] = _head_rms(out, gain_ref[...]) * (g_in * jax.nn.sigmoid(g_in))


def hgrn2_mixer_pallas(cols, b_, l_, lb, o_gain):
    c = HG_CHUNK
    nt = l_ // c
    w = GROUP_WIDTH
    head = np.arange(w) // HEAD_DIM
    same = (head[:, None] == head[None, :]).astype(np.float32)
    tri = np.tril(np.ones((c, c), np.float32))
    nb = c // HG_BLOCK
    row_head = np.arange(GROUP_HEADS * HG_BLOCK) // HG_BLOCK
    hm = (row_head[:, None] == head[None, :]).astype(np.float32)
    q_block = np.concatenate([np.full(c - (j + 1) * HG_BLOCK, j) for j in range(nb - 1)])
    k_block = np.arange((nb - 1) * GROUP_HEADS * HG_BLOCK) // (GROUP_HEADS * HG_BLOCK)
    jm = (q_block[:, None] == k_block[None, :]).astype(np.float32)
    col = lambda j: pl.BlockSpec((c, w), lambda b, i: (b * nt + i, C_HG // w + j))
    const = lambda shape: pl.BlockSpec(shape, lambda b, i: (0,) * len(shape))
    return pl.pallas_call(
        _hgrn2_kernel,
        out_shape=jax.ShapeDtypeStruct((b_ * l_, w), F32),
        grid=(b_, nt),
        in_specs=[col(0), col(1), col(2), col(3), const((1, w)), const((1, HEAD_DIM)),
                  const((c, c)), const((w, w)), const((w, w)), const(hm.shape), const(jm.shape)],
        out_specs=pl.BlockSpec((c, w), lambda b, i: (b * nt + i, 0)),
        scratch_shapes=[pltpu.VMEM((w, w), F32), pltpu.VMEM((c, w), F32),
                        pltpu.VMEM((c, w), F32), pltpu.VMEM((c, w), F32)],
        compiler_params=pltpu.CompilerParams(
            dimension_semantics=("arbitrary", "arbitrary"),
            vmem_limit_bytes=V7X_VMEM_LIMIT_BYTES),
        name="hgrn2",
    )(cols, cols, cols, cols, lb.reshape(1, w), o_gain.reshape(1, HEAD_DIM),
      jnp.asarray(tri, BF16), jnp.asarray(same, BF16), jnp.asarray(same, F32),
      jnp.asarray(hm), jnp.asarray(jm))


ML_TC = 256
ML_M_INIT = -1e30


def _mlstm_kernel(gate_ref, qk_ref, v_ref, og_ref, cw_ref, cb_ref, gb_ref, gain_ref, tril_ref,
                  triu_ref, o_ref, xprev_ref, cmat_ref, nvec_ref, m_ref):
    c = ML_TC
    nh = GROUP_HEADS
    w = GROUP_WIDTH

    @pl.when(pl.program_id(1) == 0)
    def _():
        xprev_ref[...] = jnp.zeros_like(xprev_ref)
        cmat_ref[...] = jnp.zeros_like(cmat_ref)
        nvec_ref[...] = jnp.zeros_like(nvec_ref)
        m_ref[...] = jnp.full(m_ref.shape, ML_M_INIT, F32)

    x = qk_ref[...]
    prev = xprev_ref[...]
    row = lax.broadcasted_iota(jnp.int32, x.shape, 0)
    acc = x * cw_ref[CONV_WIDTH - 1:CONV_WIDTH, :] + cb_ref[...]
    for j in range(1, CONV_WIDTH):
        shifted = jnp.where(row < j, pltpu.roll(prev, j, 0), pltpu.roll(x, j, 0))
        acc = acc + shifted * cw_ref[CONV_WIDTH - 1 - j:CONV_WIDTH - j, :]
    xprev_ref[...] = x
    qk = acc * jax.nn.sigmoid(acc)
    q = qk[:, :w]
    k = qk[:, w:] * (HEAD_DIM ** -0.5)
    v = v_ref[...]

    pre = gate_ref[...] + gb_ref[...]
    lane = lax.broadcasted_iota(jnp.int32, pre.shape, 1)
    log_f = jnp.minimum(pre, 0.0) - jnp.log1p(jnp.exp(-jnp.abs(pre)))
    log_f = jnp.where((lane >= nh) & (lane < 2 * nh), log_f, 0.0)
    bcum_c = _tri_cumsum(tril_ref[...], log_f)
    bcum_r = _cumsum_tri_rows(log_f.T, triu_ref[...])
    pre_r = pre.T
    tri = (lax.broadcasted_iota(jnp.int32, (c, c), 0) >= lax.broadcasted_iota(jnp.int32, (c, c), 1))

    outs = []
    for h in range(nh):
        sl = slice(h * HEAD_DIM, (h + 1) * HEAD_DIM)
        qh, kh, vh = q[:, sl], k[:, sl], v[:, sl]
        bc = bcum_c[:, nh + h:nh + h + 1]
        li_c = pre[:, h:h + 1]
        a_r = pre_r[h:h + 1, :] - bcum_r[nh + h:nh + h + 1, :]
        m_prev = m_ref[h:h + 1, 0:1]
        log_d = jnp.where(tri, bc + a_r, -jnp.inf)
        inter = bc + m_prev
        m_t = jnp.maximum(inter, jnp.max(log_d, axis=-1, keepdims=True))
        d_mat = jnp.exp(log_d - m_t)
        w_inter = jnp.exp(inter - m_t)
        qb = qh.astype(BF16)
        s = _dot_nt(qb, kh.astype(BF16)) * d_mat
        num = w_inter * _dot(qb, cmat_ref[h].astype(BF16)) + _dot(s.astype(BF16), vh.astype(BF16))
        den = (w_inter * jnp.sum(qh * nvec_ref[h], axis=-1, keepdims=True)
               + jnp.sum(s, axis=-1, keepdims=True))
        outs.append(num / jnp.maximum(jnp.abs(den), jnp.exp(-m_t)))
        b_last = bc[c - 1:c, :]
        log_w = b_last + (li_c - bc)
        m_new = jnp.maximum(b_last + m_prev, jnp.max(log_w, axis=0, keepdims=True))
        kw = kh * jnp.exp(log_w - m_new)
        decay = jnp.exp(b_last + m_prev - m_new)
        cmat_ref[h] = decay * cmat_ref[h] + _dot_tn(kw.astype(BF16), vh.astype(BF16))
        nvec_ref[h] = decay * nvec_ref[h] + jnp.sum(kw, axis=0, keepdims=True)
        m_ref[h:h + 1, :] = jnp.broadcast_to(m_new, (1, LANES))

    hh = _head_rms(jnp.concatenate(outs, axis=-1), gain_ref[...])
    o_ref[...] = hh * jax.nn.sigmoid(og_ref[...])


def mlstm_mixer_pallas(cols, b_, l_, conv_w, conv_b, i_bias, f_bias, o_gain):
    c = ML_TC
    nt = l_ // c
    w = GROUP_WIDTH
    gb = jnp.pad(jnp.concatenate([i_bias, f_bias]), (0, LANES - 2 * GROUP_HEADS)).reshape(1, LANES)
    tril = np.tril(np.ones((c, c), np.float32))
    const = lambda shape: pl.BlockSpec(shape, lambda b, i: (0,) * len(shape))
    blk = lambda width, off: pl.BlockSpec((c, width), lambda b, i: (b * nt + i, off // width))
    return pl.pallas_call(
        _mlstm_kernel,
        out_shape=jax.ShapeDtypeStruct((b_ * l_, w), F32),
        grid=(b_, nt),
        in_specs=[blk(LANES, C_ML_GATE), blk(2 * w, C_ML_QK), blk(w, C_ML_V), blk(w, C_ML_OG),
                  const((CONV_WIDTH, 2 * w)), const((1, 2 * w)), const((1, LANES)),
                  const((1, HEAD_DIM)), const((c, c)), const((c, c))],
        out_specs=pl.BlockSpec((c, w), lambda b, i: (b * nt + i, 0)),
        scratch_shapes=[pltpu.VMEM((c, 2 * w), F32),
                        pltpu.VMEM((GROUP_HEADS, HEAD_DIM, HEAD_DIM), F32),
                        pltpu.VMEM((GROUP_HEADS, 1, HEAD_DIM), F32),
                        pltpu.VMEM((8, LANES), F32)],
        compiler_params=pltpu.CompilerParams(
            dimension_semantics=("arbitrary", "arbitrary"),
            vmem_limit_bytes=V7X_VMEM_LIMIT_BYTES),
        name="mlstm",
    )(cols, cols, cols, cols, conv_w, conv_b.reshape(1, 2 * w), gb, o_gain.reshape(1, HEAD_DIM),
      jnp.asarray(tril, BF16), jnp.asarray(tril.T, BF16))


def kernel(x, mem, lb_param, norm_mix, w_in, w_out, hg_o_gain, dsa_kv_gain, dsa_w_uk, dsa_w_uv,
           dsa_q_gain, dsa_k_gain, dsa_idxk_gain, nsa_pos_k, nsa_pos_v, nsa_k_w1, nsa_k_w2,
           nsa_v_w1, nsa_v_w2, nsa_q_gain, nsa_k_gains, ml_conv_w, ml_conv_b, ml_i_bias,
           ml_f_bias, ml_o_gain, norm_xa, norm_mem, xa_wq, xa_wkv, xa_wo, xa_q_gain, xa_k_gain,
           norm_ffn, ffn_w13, ffn_w2):
    b_, l_, d = x.shape
    lb_all = jnp.cumsum(jax.nn.softmax(lb_param.astype(F32), axis=0), axis=0)
    lb_all = lb_all - lb_all[:1]
    x2d = x.reshape(b_ * l_, d)
    mem2d = mem.reshape(b_ * N_MEM, d)
    rope = rope_tables(l_)
    for l in range(DEPTH):
        cols = norm_matmul(x2d, norm_mix[l], pack_w_in_bf16(w_in[l]), tm=256)
        mixers = (
            hgrn2_mixer_pallas(cols, b_, l_, lb_all[l], hg_o_gain[l]),
            dsa_mixer_pallas(cols, b_, l_, rope, dsa_kv_gain[l], dsa_w_uk[l], dsa_w_uv[l],
                             dsa_q_gain[l], dsa_k_gain[l], dsa_idxk_gain[l]),
            nsa_mixer_pallas(cols, b_, l_, rope, nsa_pos_k[l], nsa_pos_v[l], nsa_k_w1[l],
                             nsa_k_w2[l], nsa_v_w1[l], nsa_v_w2[l], nsa_q_gain[l],
                             nsa_k_gains[l]),
            mlstm_mixer_pallas(cols, b_, l_, ml_conv_w[l], ml_conv_b[l], ml_i_bias[l],
                               ml_f_bias[l], ml_o_gain[l]),
        )
        k_mem, v_mem = mem_kv(mem2d, norm_mem[l], xa_wkv[l].astype(BF16), xa_k_gain[l])
        x2d = post_block(x2d, mixers, l_, w_out[l].astype(BF16), norm_xa[l],
                         xa_wq[l].astype(BF16), xa_q_gain[l], k_mem, v_mem,
                         xa_wo[l].astype(BF16), norm_ffn[l], ffn_w13[l].astype(BF16),
                         ffn_w2[l].astype(BF16), tm=512)
    return x2d.reshape(b_, l_, d)
```

```python
import functools

import jax
import jax.numpy as jnp
from jax import lax
import numpy as np
from jax.experimental import pallas as pl
from jax.experimental.pallas import tpu as pltpu

F32 = jnp.float32
BF16 = jnp.bfloat16

D_MODEL = 1024
DEPTH = 2
HEAD_DIM = 64
GROUP_HEADS = 4
GROUP_WIDTH = GROUP_HEADS * HEAD_DIM
ROPE_THETA = 500000.0
EPS = 1e-6
N_MEM = 256
XA_HEADS = 4
XA_WIDTH = XA_HEADS * HEAD_DIM
HG_CHUNK = 64
DSA_LATENT = 128
IDX_HEADS = 8
IDX_DIM = 32
DSA_TOPK_MAX = 256
CMP_BLOCK = 32
CMP_STRIDE = 16
SLC_BLOCK = 64
SLC_SHIFT = 6
SLC_TOPN = 16
WINDOW = 512
CONV_WIDTH = 4
D_FF = 2816

HG_SPLITS = (GROUP_WIDTH,) * 4
DSA_SPLITS = (GROUP_WIDTH, DSA_LATENT, IDX_HEADS * IDX_DIM, IDX_DIM, IDX_HEADS)
NSA_SPLITS = (GROUP_WIDTH,) + (HEAD_DIM,) * 6 + (3 * GROUP_HEADS,)
ML_SPLITS = (2 * GROUP_WIDTH, GROUP_WIDTH, GROUP_WIDTH, GROUP_HEADS, GROUP_HEADS)
GROUP_COLS = (sum(HG_SPLITS), sum(DSA_SPLITS), sum(NSA_SPLITS), sum(ML_SPLITS))
IN_COLS = sum(GROUP_COLS)

V7X_VMEM_LIMIT_BYTES = 56 * 1024 * 1024
LANES = 128
FF_CHUNK = 256


def _round_up(n, m):
    return -(-n // m) * m


def _rms_rows(t, g):
    return t * lax.rsqrt(jnp.mean(t * t, axis=-1, keepdims=True) + EPS) * g


def _const_spec(shape):
    return pl.BlockSpec(shape, lambda *_: (0,) * len(shape), pipeline_mode=pl.Buffered(1))


def _norm_matmul_kernel(x_ref, g_ref, w_ref, o_ref):
    h = _rms_rows(x_ref[...], g_ref[...]).astype(BF16)
    o_ref[...] = jnp.dot(h, w_ref[...], preferred_element_type=F32)


def norm_matmul(x2d, gain, w_bf16, tm):
    m, k = x2d.shape
    n = w_bf16.shape[1]
    return pl.pallas_call(
        _norm_matmul_kernel,
        out_shape=jax.ShapeDtypeStruct((m, n), F32),
        grid=(m // tm,),
        in_specs=[
            pl.BlockSpec((tm, k), lambda i: (i, 0)),
            _const_spec((1, k)),
            _const_spec((k, n)),
        ],
        out_specs=pl.BlockSpec((tm, n), lambda i: (i, 0)),
        compiler_params=pltpu.CompilerParams(
            dimension_semantics=("arbitrary",), vmem_limit_bytes=V7X_VMEM_LIMIT_BYTES),
        name="norm_matmul",
    )(x2d, gain.reshape(1, k), w_bf16)


def _mem_kv_kernel(m_ref, g_ref, w_ref, kg_ref, k_ref, v_ref):
    mn = _rms_rows(m_ref[...], g_ref[...]).astype(BF16)
    kv = jnp.dot(mn, w_ref[...], preferred_element_type=F32)
    kg = kg_ref[...]
    ks = []
    for h in range(XA_HEADS):
        kh = kv[:, h * HEAD_DIM:(h + 1) * HEAD_DIM]
        ks.append(_rms_rows(kh, kg))
    k_ref[...] = jnp.concatenate(ks, axis=-1).astype(BF16)
    v_ref[...] = kv[:, XA_WIDTH:].astype(BF16)


def mem_kv(mem2d, gain, wkv_bf16, k_gain):
    m, k = mem2d.shape
    return pl.pallas_call(
        _mem_kv_kernel,
        out_shape=(jax.ShapeDtypeStruct((m, XA_WIDTH), BF16),
                   jax.ShapeDtypeStruct((m, XA_WIDTH), BF16)),
        grid=(m // N_MEM,),
        in_specs=[
            pl.BlockSpec((N_MEM, k), lambda i: (i, 0)),
            _const_spec((1, k)),
            _const_spec((k, 2 * XA_WIDTH)),
            _const_spec((1, HEAD_DIM)),
        ],
        out_specs=(pl.BlockSpec((N_MEM, XA_WIDTH), lambda i: (i, 0)),
                   pl.BlockSpec((N_MEM, XA_WIDTH), lambda i: (i, 0))),
        compiler_params=pltpu.CompilerParams(
            dimension_semantics=("arbitrary",), vmem_limit_bytes=V7X_VMEM_LIMIT_BYTES),
        name="mem_kv",
    )(mem2d, gain.reshape(1, k), wkv_bf16, k_gain.reshape(1, HEAD_DIM))


def _post_kernel(x_ref, mhg_ref, mdsa_ref, mnsa_ref, mml_ref, wout_ref, gxa_ref, wq_ref, qg_ref,
                 k_ref, v_ref, wo_ref, gffn_ref, w13_ref, w2_ref, o_ref):
    x = x_ref[...]
    for g, mix_ref in enumerate((mhg_ref, mdsa_ref, mnsa_ref, mml_ref)):
        x = x + _dot(mix_ref[...].astype(BF16),
                     wout_ref[g * GROUP_WIDTH:(g + 1) * GROUP_WIDTH, :])
    h = _rms_rows(x, gxa_ref[...]).astype(BF16)
    q = _dot(h, wq_ref[...])
    qg = qg_ref[...] * (HEAD_DIM ** -0.5)
    k = k_ref[...]
    v = v_ref[...]
    heads = [slice(hd * HEAD_DIM, (hd + 1) * HEAD_DIM) for hd in range(XA_HEADS)]
    scores = [_dot_nt(_rms_rows(q[:, sl], qg).astype(BF16), k[:, sl]) for sl in heads]
    probs = []
    for s in scores:
        e = jnp.exp(s - jnp.max(s, axis=-1, keepdims=True))
        probs.append((e / jnp.sum(e, axis=-1, keepdims=True)).astype(BF16))
    o = jnp.concatenate([_dot(p, v[:, sl]) for p, sl in zip(probs, heads)], axis=-1)
    x = x + _dot(o.astype(BF16), wo_ref[...])
    h = _rms_rows(x, gffn_ref[...]).astype(BF16)

    def up(c):
        return (_dot(h, w13_ref[:, c * FF_CHUNK:(c + 1) * FF_CHUNK]),
                _dot(h, w13_ref[:, D_FF + c * FF_CHUNK:D_FF + (c + 1) * FF_CHUNK]))

    n_chunks = D_FF // FF_CHUNK
    acc = x
    a, b = up(0)
    for c in range(n_chunks):
        nxt = up(c + 1) if c + 1 < n_chunks else None
        act = (a * jax.nn.sigmoid(a) * b).astype(BF16)
        acc = acc + _dot(act, w2_ref[c * FF_CHUNK:(c + 1) * FF_CHUNK, :])
        if nxt is not None:
            a, b = nxt
    o_ref[...] = acc


def post_block(x2d, mixers, seq, wout, gxa, wq, qg, k_mem, v_mem, wo, gffn, w13, w2, tm):
    m, d = x2d.shape
    tiles_per_batch = seq // tm
    row = lambda i: (i, 0)
    mem_row = lambda i: (i // tiles_per_batch, 0)
    return pl.pallas_call(
        _post_kernel,
        out_shape=jax.ShapeDtypeStruct((m, d), F32),
        grid=(m // tm,),
        in_specs=[
            pl.BlockSpec((tm, d), row),
            *[pl.BlockSpec((tm, GROUP_WIDTH), row) for _ in mixers],
            _const_spec(wout.shape),
            _const_spec((1, d)),
            _const_spec(wq.shape),
            _const_spec((1, HEAD_DIM)),
            pl.BlockSpec((N_MEM, XA_WIDTH), mem_row),
            pl.BlockSpec((N_MEM, XA_WIDTH), mem_row),
            _const_spec(wo.shape),
            _const_spec((1, d)),
            _const_spec(w13.shape),
            _const_spec(w2.shape),
        ],
        out_specs=pl.BlockSpec((tm, d), row),
        compiler_params=pltpu.CompilerParams(
            dimension_semantics=("arbitrary",), vmem_limit_bytes=V7X_VMEM_LIMIT_BYTES),
        name="post_block",
    )(x2d, *mixers, wout, gxa.reshape(1, d), wq, qg.reshape(1, HEAD_DIM), k_mem, v_mem, wo,
      gffn.reshape(1, d), w13, w2)


C_HG = 0
C_DSA_Q = 1024
C_DSA_IQ = 1280
C_NSA_Q = 1536
C_ML_V = 1792
C_ML_QK = 2048
C_ML_OG = 2560
C_DSA_CKV = 2816
C_DSA_IK = 2944
C_DSA_IW = 3072
C_NSA_GATE = 3200
C_NSA_CMP = 3328
C_NSA_SLC = 3456
C_NSA_WIN = 3584
C_ML_GATE = 3712
IN_COLS_PACKED = 3840


def _packed_source_columns():
    dsa0 = GROUP_COLS[0]
    nsa0 = dsa0 + GROUP_COLS[1]
    ml0 = nsa0 + GROUP_COLS[2]
    segs = [(0, 1024, 1024), (dsa0, 256, 256), (dsa0 + 384, 256, 256), (nsa0, 256, 256),
            (ml0 + 512, 256, 256), (ml0, 512, 512), (ml0 + 768, 256, 256), (dsa0 + 256, 128, 128),
            (dsa0 + 640, 32, LANES), (dsa0 + 672, 8, LANES), (nsa0 + 640, 12, LANES),
            (nsa0 + 256, 128, 128), (nsa0 + 384, 128, 128), (nsa0 + 512, 128, 128),
            (ml0 + 1024, 8, LANES)]
    src = np.concatenate([np.concatenate([np.arange(a, a + n), np.full(width - n, -1)])
                          for a, n, width in segs]).astype(np.int32)
    assert src.shape == (IN_COLS_PACKED,)
    return src


PACK_TN = 256


def _pack_w_kernel(w_ref, src_ref, o_ref, wb_ref):
    @pl.when(pl.program_id(0) == 0)
    def _():
        wb_ref[...] = w_ref[...].astype(BF16)

    row = lax.broadcasted_iota(jnp.int32, (wb_ref.shape[1], PACK_TN), 0)
    sel = jnp.where(row == src_ref[...], 1.0, 0.0).astype(BF16)
    o_ref[...] = _dot(wb_ref[...], sel).astype(BF16)


def pack_w_in_bf16(w):
    d, n = w.shape
    n_pad = _round_up(n, LANES)
    w_pad = jnp.pad(w, ((0, 0), (0, n_pad - n)))
    src = jnp.asarray(_packed_source_columns()).reshape(1, IN_COLS_PACKED)
    return pl.pallas_call(
        _pack_w_kernel,
        out_shape=jax.ShapeDtypeStruct((d, IN_COLS_PACKED), BF16),
        grid=(IN_COLS_PACKED // PACK_TN,),
        in_specs=[_const_spec((d, n_pad)), pl.BlockSpec((1, PACK_TN), lambda j: (0, j))],
        out_specs=pl.BlockSpec((d, PACK_TN), lambda j: (0, j)),
        scratch_shapes=[pltpu.VMEM((d, n_pad), BF16)],
        compiler_params=pltpu.CompilerParams(
            dimension_semantics=("arbitrary",), vmem_limit_bytes=V7X_VMEM_LIMIT_BYTES),
        name="pack_w_in",
    )(w_pad, src)


def rope_tables(l_, d=HEAD_DIM):
    rd = d // 4
    half = rd // 2
    inv = ROPE_THETA ** (-jnp.arange(half, dtype=F32) * 2.0 / rd)
    ang = jnp.arange(l_).astype(F32)[:, None] * inv[None, :]
    cos, sin = lax.optimization_barrier((jnp.cos(ang), jnp.sin(ang)))
    zh = jnp.zeros((l_, half), F32)
    rest0 = jnp.zeros((l_, d - rd), F32)
    c = jnp.concatenate([cos, cos, rest0 + 1.0], axis=1)
    s1 = jnp.concatenate([-sin, zh, rest0], axis=1)
    s2 = jnp.concatenate([zh, sin, rest0], axis=1)
    return jnp.stack([c, s1, s2])


def rope_tables_pad(rt, width):
    n = width - rt.shape[-1]
    ident = jnp.stack([jnp.ones(rt.shape[1:2] + (n,), F32), jnp.zeros(rt.shape[1:2] + (n,), F32),
                       jnp.zeros(rt.shape[1:2] + (n,), F32)])
    return jnp.concatenate([rt, ident], axis=-1)


def rope_tables_kv_pair(rt):
    return rope_tables_pad(rt, 2 * rt.shape[-1])


def _apply_rope(t, rope_ref, half=HEAD_DIM // 8):
    w = t.shape[-1]
    return (t * rope_ref[0] + pltpu.roll(t, w - half, 1) * rope_ref[1]
            + pltpu.roll(t, half, 1) * rope_ref[2])


_NT = (((1,), (1,)), ((), ()))


def _dot_nt(a, b):
    return lax.dot_general(a, b, _NT, preferred_element_type=F32)


def _dot(a, b):
    return jnp.dot(a, b, preferred_element_type=F32)


NEG_BIG = -(2.0 ** 30)
NSA_KEY_TILE = 512
NSA_TQ = 128
NSA_SEL_TQ = 256
NSA_PREP_TK = 512


def _nsa_kv_prep_kernel(ps_ref, pw_ref, rope_ref, gs_ref, gw_ref, kaug_ref, vs_ref, kw_ref, vw_ref):
    tk = ps_ref.shape[0]
    lane = lax.broadcasted_iota(jnp.int32, (tk, LANES), 1)
    is_k = lane < HEAD_DIM

    def norm_rope(p, g):
        ms = jnp.sum(jnp.where(is_k, p * p, 0.0), axis=-1, keepdims=True) * (1.0 / HEAD_DIM)
        y = jnp.where(is_k, p * lax.rsqrt(ms + EPS) * g, p)
        return _apply_rope(y, rope_ref)

    ys = norm_rope(ps_ref[...], gs_ref[...])
    yw = norm_rope(pw_ref[...], gw_ref[...])
    row = pl.program_id(1) * tk + lax.broadcasted_iota(jnp.int32, (tk, LANES), 0)
    ind = jnp.where(jnp.right_shift(row, SLC_SHIFT) == (lane - HEAD_DIM), 1.0, 0.0)
    kaug_ref[0] = jnp.where(is_k, ys, ind).astype(BF16)
    vs_ref[0] = ys.T[HEAD_DIM:, :].astype(BF16)
    kw_ref[0] = yw[:, :HEAD_DIM].astype(BF16)
    vw_ref[0] = yw.T[HEAD_DIM:, :].astype(BF16)


def nsa_kv_prep(cols, b_, l_, rope_pair, g_slc, g_win):
    tk = NSA_PREP_TK
    nt = l_ // tk
    ones = jnp.ones((HEAD_DIM,), F32)
    gs = jnp.concatenate([g_slc, ones]).reshape(1, LANES)
    gw = jnp.concatenate([g_win, ones]).reshape(1, LANES)
    kv = lambda w: jax.ShapeDtypeStruct((b_, l_, w), BF16)
    kv_t = jax.ShapeDtypeStruct((b_, HEAD_DIM, l_), BF16)
    out_blk = lambda w: pl.BlockSpec((1, tk, w), lambda b, i: (b, i, 0))
    out_t = pl.BlockSpec((1, HEAD_DIM, tk), lambda b, i: (b, 0, i))
    return pl.pallas_call(
        _nsa_kv_prep_kernel,
        out_shape=(kv(LANES), kv_t, kv(HEAD_DIM), kv_t),
        grid=(b_, nt),
        in_specs=[
            pl.BlockSpec((tk, LANES), lambda b, i: (b * nt + i, C_NSA_SLC // LANES)),
            pl.BlockSpec((tk, LANES), lambda b, i: (b * nt + i, C_NSA_WIN // LANES)),
            pl.BlockSpec((3, tk, LANES), lambda b, i: (0, i, 0)),
            pl.BlockSpec((1, LANES), lambda b, i: (0, 0)),
            pl.BlockSpec((1, LANES), lambda b, i: (0, 0)),
        ],
        out_specs=(out_blk(LANES), out_t, out_blk(HEAD_DIM), out_t),
        compiler_params=pltpu.CompilerParams(
            dimension_semantics=("arbitrary", "arbitrary"),
            vmem_limit_bytes=V7X_VMEM_LIMIT_BYTES),
        name="nsa_kv_prep",
    )(cols, cols, rope_pair, gs, gw)


def _nsa_compress_kernel(r_ref, pea_ref, peb_ref, w1a_ref, w1b_ref, w2k_ref, w2v_ref, kg_ref,
                         kc_ref, vc_ref):
    r = r_ref[0]
    n = r.shape[0]
    a = _dot((r + pea_ref[...]).astype(BF16), w1a_ref[...])
    bm = _dot((r + peb_ref[...]).astype(BF16), w1b_ref[...])
    row = lax.broadcasted_iota(jnp.int32, bm.shape, 0)
    bm_up = jnp.where(row < n - 1, pltpu.roll(bm, n - 1, 0), 0.0)
    h = jnp.maximum(a + bm_up, 0.0).astype(BF16)
    hid = w2k_ref.shape[0]
    ck = _dot(h[:, :hid], w2k_ref[...])
    cv = _dot(h[:, hid:], w2v_ref[...])
    kc_ref[0] = _rms_rows(ck, kg_ref[...]).astype(BF16)
    vc_ref[0] = cv.astype(BF16)


def nsa_compress(cols, b_, l_, pos_k, pos_v, k_w1, k_w2, v_w1, v_w2, k_gain):
    rows = l_ // CMP_STRIDE
    hid = k_w1.shape[1]
    pair = cols[:, C_NSA_CMP:C_NSA_CMP + LANES].reshape(b_, rows, CMP_STRIDE * LANES)

    def interleave_pe(lo):
        pe = jnp.concatenate([pos_k[lo:lo + CMP_STRIDE], pos_v[lo:lo + CMP_STRIDE]], axis=1)
        return pe.reshape(1, CMP_STRIDE * LANES)

    def interleave_w(lo):
        wk = k_w1[lo * HEAD_DIM:(lo + CMP_STRIDE) * HEAD_DIM].reshape(CMP_STRIDE, HEAD_DIM, hid)
        wv = v_w1[lo * HEAD_DIM:(lo + CMP_STRIDE) * HEAD_DIM].reshape(CMP_STRIDE, HEAD_DIM, hid)
        z = jnp.zeros_like(wk)
        top = jnp.concatenate([wk, z], axis=2)
        bot = jnp.concatenate([z, wv], axis=2)
        return jnp.concatenate([top, bot], axis=1).reshape(CMP_STRIDE * LANES, 2 * hid).astype(BF16)

    out = jax.ShapeDtypeStruct((b_, rows, HEAD_DIM), BF16)
    return pl.pallas_call(
        _nsa_compress_kernel,
        out_shape=(out, out),
        grid=(b_,),
        in_specs=[
            pl.BlockSpec((1, rows, CMP_STRIDE * LANES), lambda b: (b, 0, 0)),
            _const_spec((1, CMP_STRIDE * LANES)),
            _const_spec((1, CMP_STRIDE * LANES)),
            _const_spec((CMP_STRIDE * LANES, 2 * hid)),
            _const_spec((CMP_STRIDE * LANES, 2 * hid)),
            _const_spec((hid, HEAD_DIM)),
            _const_spec((hid, HEAD_DIM)),
            _const_spec((1, HEAD_DIM)),
        ],
        out_specs=(pl.BlockSpec((1, rows, HEAD_DIM), lambda b: (b, 0, 0)),
                   pl.BlockSpec((1, rows, HEAD_DIM), lambda b: (b, 0, 0))),
        compiler_params=pltpu.CompilerParams(
            dimension_semantics=("arbitrary",), vmem_limit_bytes=V7X_VMEM_LIMIT_BYTES),
        name="nsa_compress",
    )(pair, interleave_pe(0), interleave_pe(CMP_STRIDE), interleave_w(0),
      interleave_w(CMP_STRIDE), k_w2.astype(BF16), v_w2.astype(BF16),
      k_gain.reshape(1, HEAD_DIM))


def _masked_softmax(s, valid, axis):
    m = jnp.max(jnp.where(valid, s, -jnp.inf), axis=axis, keepdims=True)
    m = jnp.where(m == -jnp.inf, 0.0, m)
    e = jnp.where(valid, jnp.exp(s - m), 0.0)
    den = jnp.sum(e, axis=axis, keepdims=True)
    return e / jnp.where(den > 0, den, 1.0)


def _nsa_select_kernel(q_ref, kc_ref, vc_ref, ovt_ref, rope_ref, qg_ref, qaug_ref, ocmp_ref,
                       *, n_sel):
    tq = q_ref.shape[0]
    ncr = kc_ref.shape[1]
    n_slc = ovt_ref.shape[0]
    t0 = pl.program_id(1) * tq
    scale = HEAD_DIM ** -0.5
    q = q_ref[...]
    g = qg_ref[...]
    kc = kc_ref[0]
    vc = vc_ref[0]
    vis_t = (lax.broadcasted_iota(jnp.int32, (ncr, tq), 0) * CMP_STRIDE + (CMP_BLOCK - 1)
             <= t0 + lax.broadcasted_iota(jnp.int32, (ncr, tq), 1))
    qn_heads, o_heads = [], []
    psum_t = jnp.zeros((ncr, tq), F32)
    for h in range(GROUP_HEADS):
        qn = _rms_rows(q[:, h * HEAD_DIM:(h + 1) * HEAD_DIM], g)
        qn_heads.append(qn)
        p_t = _masked_softmax(_dot_nt(kc, (qn * scale).astype(BF16)), vis_t, 0)
        o_heads.append(_dot_tn(vc, p_t.astype(BF16)))
        psum_t = psum_t + p_t
    ocmp_ref[0] = jnp.concatenate(o_heads, axis=0)

    hi = psum_t.astype(BF16)
    lo = (psum_t - hi.astype(F32)).astype(BF16)
    imp = _dot(ovt_ref[...], hi) + _dot(ovt_ref[...], lo)
    blk = lax.broadcasted_iota(jnp.int32, (n_slc, tq), 0)
    cur = jnp.right_shift(t0 + lax.broadcasted_iota(jnp.int32, (n_slc, tq), 1), SLC_SHIFT)
    forced = (blk == 0) | (blk == cur) | (blk == cur - 1)
    imp = jnp.where(forced, jnp.inf, jnp.where(blk > cur, -jnp.inf, imp))
    n_grp = n_slc // SUBLANES
    groups = [imp[g * SUBLANES:(g + 1) * SUBLANES] for g in range(n_grp)]
    cnts = [jnp.zeros((SUBLANES, tq), F32) for _ in range(n_grp)]
    sub = lax.broadcasted_iota(jnp.int32, (SUBLANES, tq), 0)
    for m in range(n_slc):
        row = imp[m:m + 1, :]
        gm, rm = divmod(m, SUBLANES)
        for g in range(n_grp):
            if g < gm:
                beats = row > groups[g]
            elif g > gm:
                beats = row >= groups[g]
            else:
                beats = (row > groups[g]) | ((row == groups[g]) & (sub > rm))
            cnts[g] = cnts[g] + jnp.where(beats, 1.0, 0.0)
    cnt = jnp.concatenate(cnts, axis=0)
    mt = jnp.where(cnt < n_sel, 0.0, NEG_BIG)
    pad = jnp.zeros((HEAD_DIM - n_slc, tq), F32)
    mt = jnp.concatenate([mt, pad, mt, pad], axis=0) if n_slc < HEAD_DIM else jnp.concatenate(
        [mt, mt], axis=0)
    mt = mt.T

    qr = _apply_rope(jnp.concatenate(qn_heads, axis=-1), rope_ref) * scale
    lane = lax.broadcasted_iota(jnp.int32, (tq, LANES), 1)
    for j in range(GROUP_HEADS // 2):
        pair = qr[:, j * LANES:(j + 1) * LANES]
        swapped = pltpu.roll(pair, HEAD_DIM, 1)
        qaug_ref[0, :, (2 * j) * LANES:(2 * j + 1) * LANES] = jnp.where(
            lane < HEAD_DIM, pair, mt).astype(BF16)
        qaug_ref[0, :, (2 * j + 1) * LANES:(2 * j + 2) * LANES] = jnp.where(
            lane < HEAD_DIM, swapped, mt).astype(BF16)


def nsa_select(cols, b_, l_, k_cmp, v_cmp, rope_q, q_gain):
    tq = min(NSA_SEL_TQ, l_)
    nt = l_ // tq
    ncr = l_ // CMP_STRIDE
    n_slc = l_ // SLC_BLOCK
    n_sel = min(SLC_TOPN, n_slc)
    st_c = np.arange(ncr) * CMP_STRIDE
    st_s = np.arange(n_slc) * SLC_BLOCK
    ovt = ((st_c[None, :] < st_s[:, None] + SLC_BLOCK)
           & (st_c[None, :] + CMP_BLOCK > st_s[:, None])).astype(np.float32)
    return pl.pallas_call(
        functools.partial(_nsa_select_kernel, n_sel=n_sel),
        out_shape=(jax.ShapeDtypeStruct((b_, l_, GROUP_HEADS * LANES), BF16),
                   jax.ShapeDtypeStruct((b_, GROUP_WIDTH, l_), F32)),
        grid=(b_, nt),
        in_specs=[
            pl.BlockSpec((tq, GROUP_WIDTH), lambda b, i: (b * nt + i, C_NSA_Q // GROUP_WIDTH)),
            pl.BlockSpec((1, ncr, HEAD_DIM), lambda b, i: (b, 0, 0)),
            pl.BlockSpec((1, ncr, HEAD_DIM), lambda b, i: (b, 0, 0)),
            pl.BlockSpec((n_slc, ncr), lambda b, i: (0, 0)),
            pl.BlockSpec((3, tq, GROUP_WIDTH), lambda b, i: (0, i, 0)),
            pl.BlockSpec((1, HEAD_DIM), lambda b, i: (0, 0)),
        ],
        out_specs=(pl.BlockSpec((1, tq, GROUP_HEADS * LANES), lambda b, i: (b, i, 0)),
                   pl.BlockSpec((1, GROUP_WIDTH, tq), lambda b, i: (b, 0, i))),
        compiler_params=pltpu.CompilerParams(
            dimension_semantics=("arbitrary", "arbitrary"),
            vmem_limit_bytes=V7X_VMEM_LIMIT_BYTES),
        name="nsa_select",
    )(cols, k_cmp, v_cmp, jnp.asarray(ovt, BF16), rope_q, q_gain.reshape(1, HEAD_DIM))


def _softmax_stats_update(s, m, l):
    m_new = jnp.maximum(m, jnp.max(s, axis=0, keepdims=True))
    alpha = jnp.exp(m - m_new)
    p = jnp.exp(s - m_new)
    return m_new, alpha, p, alpha * l + jnp.sum(p, axis=0, keepdims=True)


def _nsa_attend_t_kernel(qaug_ref, kaug_ref, vst_ref, kw_ref, vwt_ref, ocmp_ref, gate_ref, o_ref):
    tq = qaug_ref.shape[1]
    nh = GROUP_HEADS
    ts = NSA_KEY_TILE
    t0 = pl.program_id(1) * tq
    qa = qaug_ref[0]
    qs = jnp.concatenate([qa[:, h * LANES:(h + 1) * LANES] for h in range(nh)], axis=0)
    n = nh * tq
    qpos_tile = t0 + lax.broadcasted_iota(jnp.int32, (1, tq), 1)
    qpos = jnp.concatenate([qpos_tile] * nh, axis=1)

    def key_tile(kt, carry):
        m, l, acc = carry
        k0 = pl.multiple_of(kt * ts, ts)
        s = _dot_nt(kaug_ref[0, pl.ds(k0, ts), :], qs)
        kpos = k0 + lax.broadcasted_iota(jnp.int32, (ts, n), 0)
        s = jnp.where(kpos <= qpos, s, NEG_BIG)
        m, alpha, p, l = _softmax_stats_update(s, m, l)
        acc = alpha * acc + _dot(vst_ref[0, :, pl.ds(k0, ts)], p.astype(BF16))
        return m, l, acc

    n_kt = (t0 + tq + ts - 1) // ts
    init = (jnp.full((1, n), -jnp.inf, F32), jnp.zeros((1, n), F32),
            jnp.zeros((HEAD_DIM, n), F32))
    _, l, acc = lax.fori_loop(0, n_kt, key_tile, init)
    o_slc = acc / l

    wlen = WINDOW + tq
    start = pl.multiple_of(jnp.maximum(t0 - WINDOW, 0), tq)
    sw = _dot_nt(kw_ref[0, pl.ds(start, wlen), :], qs[:, :HEAD_DIM])
    dist = qpos - (start + lax.broadcasted_iota(jnp.int32, (wlen, n), 0))
    sw = jnp.where((dist >= 0) & (dist < WINDOW), sw, -jnp.inf)
    e = jnp.exp(sw - jnp.max(sw, axis=0, keepdims=True))
    o_swa = _dot(vwt_ref[0, :, pl.ds(start, wlen)], e.astype(BF16)) / jnp.sum(
        e, axis=0, keepdims=True)

    g = jax.nn.sigmoid(gate_ref[...]).T
    oc = ocmp_ref[0]
    outs = []
    for h in range(nh):
        cols_h = slice(h * tq, (h + 1) * tq)
        outs.append(g[h:h + 1, :] * oc[h * HEAD_DIM:(h + 1) * HEAD_DIM, :]
                    + g[nh + h:nh + h + 1, :] * o_slc[:, cols_h]
                    + g[2 * nh + h:2 * nh + h + 1, :] * o_swa[:, cols_h])
    o_ref[...] = jnp.concatenate(outs, axis=0).T


def nsa_attend(cols, b_, l_, q_aug, k_aug, v_slc, k_win, v_win, o_cmp):
    tq = NSA_TQ
    nt = l_ // tq
    seq = lambda w: pl.BlockSpec((1, l_, w), lambda b, i: (b, 0, 0))
    seq_t = pl.BlockSpec((1, HEAD_DIM, l_), lambda b, i: (b, 0, 0))
    return pl.pallas_call(
        _nsa_attend_t_kernel,
        out_shape=jax.ShapeDtypeStruct((b_ * l_, GROUP_WIDTH), F32),
        grid=(b_, nt),
        in_specs=[
            pl.BlockSpec((1, tq, GROUP_HEADS * LANES), lambda b, i: (b, i, 0)),
            seq(LANES), seq_t, seq(HEAD_DIM), seq_t,
            pl.BlockSpec((1, GROUP_WIDTH, tq), lambda b, i: (b, 0, i)),
            pl.BlockSpec((tq, LANES), lambda b, i: (b * nt + i, C_NSA_GATE // LANES)),
        ],
        out_specs=pl.BlockSpec((tq, GROUP_WIDTH), lambda b, i: (b * nt + i, 0)),
        compiler_params=pltpu.CompilerParams(
            dimension_semantics=("arbitrary", "arbitrary"),
            vmem_limit_bytes=V7X_VMEM_LIMIT_BYTES),
        name="nsa_attend",
    )(q_aug, k_aug, v_slc, k_win, v_win, o_cmp, cols)


def nsa_mixer_pallas(cols, b_, l_, rope, pos_k, pos_v, k_w1, k_w2, v_w1, v_w2, q_gain, k_gains):
    k_aug, v_slc, k_win, v_win = nsa_kv_prep(cols, b_, l_, rope_tables_kv_pair(rope),
                                             k_gains[1], k_gains[2])
    k_cmp, v_cmp = nsa_compress(cols, b_, l_, pos_k, pos_v, k_w1, k_w2, v_w1, v_w2, k_gains[0])
    q_aug, o_cmp = nsa_select(cols, b_, l_, k_cmp, v_cmp, jnp.tile(rope, (1, 1, GROUP_HEADS)),
                              q_gain)
    return nsa_attend(cols, b_, l_, q_aug, k_aug, v_slc, k_win, v_win, o_cmp)


DSA_TQ = 128
DSA_KEY_TILE = 512
DSA_PREP_TK = 512
IDX_PACK = LANES
INT_MIN = -2 ** 31
MASKED_SCORE = -1e30


def _split_hi_lo(t):
    hi = t.astype(BF16)
    lo = (t - hi.astype(F32)).astype(BF16)
    return hi, lo


def _placement(rows, cols, pairs):
    p = np.zeros((rows, cols), np.float32)
    for r, c in pairs:
        p[r, c] = 1.0
    return jnp.asarray(p, BF16)


def _dsa_kv_prep_kernel(ckv_ref, ik_ref, rope_ref, ropei_ref, kvg_ref, wkv_ref, kg_ref, ikg_ref,
                        pkh_ref, pkl_ref, k_ref, vt_ref, ik3_ref):
    tk = ckv_ref.shape[0]
    lane = lax.broadcasted_iota(jnp.int32, (tk, LANES), 1)
    ckv = _rms_rows(ckv_ref[...], kvg_ref[...]).astype(BF16)
    kv = _dot(ckv, wkv_ref[...])
    is_k = lane < HEAD_DIM
    ms = jnp.sum(jnp.where(is_k, kv * kv, 0.0), axis=-1, keepdims=True) * (1.0 / HEAD_DIM)
    y = _apply_rope(jnp.where(is_k, kv * lax.rsqrt(ms + EPS) * kg_ref[...], kv), rope_ref)
    k_ref[0] = y[:, :HEAD_DIM].astype(BF16)
    vt_ref[0] = y.T[HEAD_DIM:, :].astype(BF16)
    ik = ik_ref[...]
    ms = jnp.sum(ik * ik, axis=-1, keepdims=True) * (1.0 / IDX_DIM)
    ikn = _apply_rope(ik * lax.rsqrt(ms + EPS) * ikg_ref[...], ropei_ref, IDX_DIM // 8)
    hi, lo = _split_hi_lo(ikn)
    ik3_ref[0] = (_dot(hi, pkh_ref[...]) + _dot(lo, pkl_ref[...])).astype(BF16)


def dsa_kv_prep(cols, b_, l_, rope_pair, rope_idx, kv_gain, w_uk, w_uv, k_gain, idxk_gain):
    tk = DSA_PREP_TK
    nt = l_ // tk
    ones = jnp.ones((HEAD_DIM,), F32)
    kg = jnp.concatenate([k_gain, ones]).reshape(1, LANES)
    ikg = jnp.pad(idxk_gain, (0, LANES - IDX_DIM)).reshape(1, LANES)
    wkv = jnp.concatenate([w_uk, w_uv], axis=1).astype(BF16)
    d = range(IDX_DIM)
    pkh = _placement(LANES, IDX_PACK, [(i, i) for i in d] + [(i, 2 * IDX_DIM + i) for i in d])
    pkl = _placement(LANES, IDX_PACK, [(i, IDX_DIM + i) for i in d])
    const = lambda shape: pl.BlockSpec(shape, lambda b, i: (0,) * len(shape))
    out = lambda w: pl.BlockSpec((1, tk, w), lambda b, i: (b, i, 0))
    return pl.pallas_call(
        _dsa_kv_prep_kernel,
        out_shape=(jax.ShapeDtypeStruct((b_, l_, HEAD_DIM), BF16),
                   jax.ShapeDtypeStruct((b_, HEAD_DIM, l_), BF16),
                   jax.ShapeDtypeStruct((b_, l_, IDX_PACK), BF16)),
        grid=(b_, nt),
        in_specs=[
            pl.BlockSpec((tk, LANES), lambda b, i: (b * nt + i, C_DSA_CKV // LANES)),
            pl.BlockSpec((tk, LANES), lambda b, i: (b * nt + i, C_DSA_IK // LANES)),
            pl.BlockSpec((3, tk, LANES), lambda b, i: (0, i, 0)),
            pl.BlockSpec((3, tk, LANES), lambda b, i: (0, i, 0)),
            const((1, LANES)), const((DSA_LATENT, LANES)), const((1, LANES)), const((1, LANES)),
            const((LANES, IDX_PACK)), const((LANES, IDX_PACK)),
        ],
        out_specs=(out(HEAD_DIM), pl.BlockSpec((1, HEAD_DIM, tk), lambda b, i: (b, 0, i)),
                   out(IDX_PACK)),
        compiler_params=pltpu.CompilerParams(
            dimension_semantics=("arbitrary", "arbitrary"),
            vmem_limit_bytes=V7X_VMEM_LIMIT_BYTES),
        name="dsa_kv_prep",
    )(cols, cols, rope_pair, rope_idx, kv_gain.reshape(1, DSA_LATENT), wkv, kg, ikg, pkh, pkl)


def _dsa_q_prep_kernel(q_ref, iq_ref, rope_ref, ropei_ref, qg_ref, pqh_ref, pql_ref,
                       qh_ref, iq3_ref):
    g = qg_ref[...]
    q = q_ref[...]
    qn = jnp.concatenate([_rms_rows(q[:, h * HEAD_DIM:(h + 1) * HEAD_DIM], g)
                          for h in range(GROUP_HEADS)], axis=-1)
    qr = _apply_rope(qn, rope_ref) * (HEAD_DIM ** -0.5)
    for h in range(GROUP_HEADS):
        qh_ref[0, h] = qr[:, h * HEAD_DIM:(h + 1) * HEAD_DIM].astype(BF16)
    hi, lo = _split_hi_lo(_apply_rope(iq_ref[...], ropei_ref, IDX_DIM // 8))
    iq3_ref[0] = (_dot(hi, pqh_ref[...]) + _dot(lo, pql_ref[...])).astype(BF16)


def dsa_q_prep(cols, b_, l_, rope_q, rope_iq, q_gain):
    tq = 256
    nt = l_ // tq
    w = IDX_HEADS * IDX_DIM
    hd = [(h, i) for h in range(IDX_HEADS) for i in range(IDX_DIM)]
    pqh = _placement(w, IDX_HEADS * IDX_PACK,
                     [(IDX_DIM * h + i, IDX_PACK * h + i) for h, i in hd]
                     + [(IDX_DIM * h + i, IDX_PACK * h + IDX_DIM + i) for h, i in hd])
    pql = _placement(w, IDX_HEADS * IDX_PACK,
                     [(IDX_DIM * h + i, IDX_PACK * h + 2 * IDX_DIM + i) for h, i in hd])
    const = lambda shape: pl.BlockSpec(shape, lambda b, i: (0,) * len(shape))
    return pl.pallas_call(
        _dsa_q_prep_kernel,
        out_shape=(jax.ShapeDtypeStruct((b_, GROUP_HEADS, l_, HEAD_DIM), BF16),
                   jax.ShapeDtypeStruct((b_, l_, IDX_HEADS * IDX_PACK), BF16)),
        grid=(b_, nt),
        in_specs=[
            pl.BlockSpec((tq, GROUP_WIDTH), lambda b, i: (b * nt + i, C_DSA_Q // GROUP_WIDTH)),
            pl.BlockSpec((tq, w), lambda b, i: (b * nt + i, C_DSA_IQ // w)),
            pl.BlockSpec((3, tq, GROUP_WIDTH), lambda b, i: (0, i, 0)),
            pl.BlockSpec((3, tq, w), lambda b, i: (0, i, 0)),
            const((1, HEAD_DIM)), const(pqh.shape), const(pql.shape),
        ],
        out_specs=(pl.BlockSpec((1, GROUP_HEADS, tq, HEAD_DIM), lambda b, i: (b, 0, i, 0)),
                   pl.BlockSpec((1, tq, IDX_HEADS * IDX_PACK), lambda b, i: (b, i, 0))),
        compiler_params=pltpu.CompilerParams(
            dimension_semantics=("arbitrary", "arbitrary"),
            vmem_limit_bytes=V7X_VMEM_LIMIT_BYTES),
        name="dsa_q_prep",
    )(cols, cols, rope_q, rope_iq, q_gain.reshape(1, HEAD_DIM), pqh, pql)


SUBLANES = 8


def _fold_rows(t, op, group=SUBLANES):
    parts = [t[i * group:(i + 1) * group] for i in range(t.shape[0] // group)]
    while len(parts) > 1:
        parts = [op(parts[i], parts[i + 1]) if i + 1 < len(parts) else parts[i]
                 for i in range(0, len(parts), 2)]
    return parts[0]


def _dsa_attend_t_kernel(qh_ref, iq3_ref, iw_ref, k_ref, vt_ref, ik3_ref, o_ref, sc_ref, *,
                         topk, idx_bits):
    tq = iq3_ref.shape[1]
    ts = DSA_KEY_TILE
    nh = GROUP_HEADS
    t0 = pl.program_id(1) * tq
    n_kt = (t0 + tq + ts - 1) // ts
    qpos = t0 + lax.broadcasted_iota(jnp.int32, (ts, tq), 1)
    krow = lax.broadcasted_iota(jnp.int32, (ts, tq), 0)

    iq3 = iq3_ref[0]
    iq_rows = jnp.concatenate([iq3[:, h * IDX_PACK:(h + 1) * IDX_PACK]
                               for h in range(IDX_HEADS)], axis=0)
    iw_t = (iw_ref[...] * (IDX_HEADS ** -0.5 * IDX_DIM ** -0.5)).T

    def score_tile(kt, _):
        k0 = pl.multiple_of(kt * ts, ts)
        rel = jnp.maximum(_dot_nt(ik3_ref[0, pl.ds(k0, ts), :], iq_rows), 0.0)
        sc = rel[:, 0:tq] * iw_t[0:1, :]
        for h in range(1, IDX_HEADS):
            sc = sc + rel[:, h * tq:(h + 1) * tq] * iw_t[h:h + 1, :]
        sc_ref[pl.ds(k0, ts), :] = jnp.where(k0 + krow <= qpos, sc, -jnp.inf)
        return 0

    lax.fori_loop(0, n_kt, score_tile, 0)

    def count(pred):
        def tile(kt, c):
            k0 = pl.multiple_of(kt * ts, ts)
            hit = jnp.where(pred(sc_ref[pl.ds(k0, ts), :], k0), 1.0, 0.0)
            return c + _fold_rows(hit, jnp.add)
        c = lax.fori_loop(0, n_kt, tile, jnp.zeros((SUBLANES, tq), F32))
        return jnp.sum(c, axis=0, keepdims=True)

    def key_to_float(key):
        return pltpu.bitcast(jnp.where(key >= 0, key, key ^ jnp.int32(0x7FFFFFFF)), F32)

    def value_bit(i, carry):
        thr_key, n_at = carry
        cand = thr_key | jnp.left_shift(jnp.int32(1), 31 - i)
        cand_f = key_to_float(cand ^ jnp.int32(INT_MIN))
        n = count(lambda sc, k0: sc >= cand_f)
        keep = n >= topk
        return jnp.where(keep, cand, thr_key), jnp.where(keep, n, n_at)

    thr_key, n_at = lax.fori_loop(
        0, 32, value_bit, (jnp.zeros((1, tq), jnp.int32), jnp.zeros((1, tq), F32)))
    few = t0 + lax.broadcasted_iota(jnp.int32, (1, tq), 1) + 1 < topk
    thr = jnp.where(few, jnp.finfo(F32).min, key_to_float(thr_key ^ jnp.int32(INT_MIN)))
    tie_break = jnp.max(jnp.where((n_at > topk) & jnp.logical_not(few), 1.0, 0.0)) > 0.0

    def last_tie_position():
        need = topk - count(lambda sc, k0: sc > thr)

        def index_bit(i, last):
            cand = last | jnp.left_shift(jnp.int32(1), idx_bits - 1 - i)
            tied_below = lambda sc, k0: (sc == thr) & (k0 + krow < cand)
            return jnp.where(count(tied_below) < need, cand, last)

        return lax.fori_loop(0, idx_bits, index_bit, jnp.zeros((1, tq), jnp.int32))

    qs = qh_ref[0].reshape(nh * tq, HEAD_DIM)

    def attend(selected):
        def key_tile(kt, carry):
            m, l, acc = carry
            k0 = pl.multiple_of(kt * ts, ts)
            bias = jnp.where(selected(sc_ref[pl.ds(k0, ts), :], k0), 0.0, MASKED_SCORE)
            s = _dot_nt(k_ref[0, pl.ds(k0, ts), :], qs) + jnp.concatenate([bias] * nh, axis=1)
            m, alpha, p, l = _softmax_stats_update(s, m, l)
            acc = alpha * acc + _dot(vt_ref[0, :, pl.ds(k0, ts)], p.astype(BF16))
            return m, l, acc

        init = (jnp.full((1, nh * tq), MASKED_SCORE, F32), jnp.zeros((1, nh * tq), F32),
                jnp.zeros((HEAD_DIM, nh * tq), F32))
        _, l, acc = lax.fori_loop(0, n_kt, key_tile, init)
        return l, acc

    def attend_with_ties():
        last = last_tie_position()
        return attend(lambda sc, k0: (sc > thr) | ((sc == thr) & (k0 + krow <= last)))

    l, acc = lax.cond(tie_break, attend_with_ties, lambda: attend(lambda sc, k0: sc >= thr))
    o_t = acc / l
    o_ref[...] = jnp.concatenate([o_t[:, h * tq:(h + 1) * tq] for h in range(nh)], axis=0).T


def dsa_attend(cols, b_, l_, qh, iq3, k, v, ik3):
    tq = DSA_TQ
    nt = l_ // tq
    topk = min(DSA_TOPK_MAX, l_ // 4)
    idx_bits = int(np.log2(l_))
    assert 2 ** idx_bits == l_ and l_ % DSA_KEY_TILE == 0 and topk <= DSA_KEY_TILE
    seq = lambda w: pl.BlockSpec((1, l_, w), lambda b, i: (b, 0, 0))
    return pl.pallas_call(
        functools.partial(_dsa_attend_t_kernel, topk=topk, idx_bits=idx_bits),
        out_shape=jax.ShapeDtypeStruct((b_ * l_, GROUP_WIDTH), F32),
        grid=(b_, nt),
        in_specs=[
            pl.BlockSpec((1, GROUP_HEADS, tq, HEAD_DIM), lambda b, i: (b, 0, i, 0)),
            pl.BlockSpec((1, tq, IDX_HEADS * IDX_PACK), lambda b, i: (b, i, 0)),
            pl.BlockSpec((tq, LANES), lambda b, i: (b * nt + i, C_DSA_IW // LANES)),
            seq(HEAD_DIM), pl.BlockSpec((1, HEAD_DIM, l_), lambda b, i: (b, 0, 0)), seq(IDX_PACK),
        ],
        out_specs=pl.BlockSpec((tq, GROUP_WIDTH), lambda b, i: (b * nt + i, 0)),
        scratch_shapes=[pltpu.VMEM((l_, tq), F32)],
        compiler_params=pltpu.CompilerParams(
            dimension_semantics=("arbitrary", "arbitrary"),
            vmem_limit_bytes=V7X_VMEM_LIMIT_BYTES),
        name="dsa_attend",
    )(qh, iq3, cols, k, v, ik3)


def dsa_mixer_pallas(cols, b_, l_, rope, kv_gain, w_uk, w_uv, q_gain, k_gain, idxk_gain):
    rope_i = rope_tables(l_, IDX_DIM)
    k, v, ik3 = dsa_kv_prep(cols, b_, l_, rope_tables_kv_pair(rope), rope_tables_pad(rope_i, LANES),
                            kv_gain, w_uk, w_uv, k_gain, idxk_gain)
    qh, iq3 = dsa_q_prep(cols, b_, l_, jnp.tile(rope, (1, 1, GROUP_HEADS)),
                         jnp.tile(rope_i, (1, 1, IDX_HEADS)), q_gain)
    return dsa_attend(cols, b_, l_, qh, iq3, k, v, ik3)


_TN = (((0,), (0,)), ((), ()))


def _dot_tn(a, b):
    return lax.dot_general(a, b, _TN, preferred_element_type=F32)


def _split3(t):
    hi = t.astype(BF16)
    r = t - hi.astype(F32)
    mid = r.astype(BF16)
    lo = (r - mid.astype(F32)).astype(BF16)
    return hi, mid, lo


def _tri_cumsum(tri, t):
    hi, mid, lo = _split3(t)
    return _dot(tri, hi) + _dot(tri, mid) + _dot(tri, lo)


def _cumsum_tri_rows(t, tri_u):
    hi, mid, lo = _split3(t)
    return _dot(hi, tri_u) + _dot(mid, tri_u) + _dot(lo, tri_u)


def _head_rms(o, gain):
    return jnp.concatenate([_rms_rows(o[:, h * HEAD_DIM:(h + 1) * HEAD_DIM], gain)
                            for h in range(GROUP_HEADS)], axis=-1)


HG_SUB = 8
HG_BLOCK = 16
HG_CHUNKS_PER_STEP = 8
LOG2_E = 1.4426950408889634


def _hgrn2_kernel(q_ref, f_ref, i_ref, g_ref, lb_ref, gain_ref, tri_ref, ones_ref, bd_ref,
                  hm_ref, jm_ref, o_ref, st_ref, b_ref, kk_ref, v_ref):
    @pl.when(pl.program_id(1) == 0)
    def _():
        st_ref[...] = jnp.zeros_like(st_ref)

    for ci in range(HG_CHUNKS_PER_STEP):
        rows = pl.ds(ci * HG_CHUNK, HG_CHUNK)
        _hgrn2_chunk(q_ref.at[rows], f_ref.at[rows], i_ref.at[rows], g_ref.at[rows], lb_ref,
                     gain_ref, tri_ref, ones_ref, bd_ref, hm_ref, jm_ref, o_ref.at[rows], st_ref,
                     b_ref, kk_ref, v_ref)


def _hgrn2_chunk(q_ref, f_ref, i_ref, g_ref, lb_ref, gain_ref, tri_ref, ones_ref, bd_ref,
                 hm_ref, jm_ref, o_ref, st_ref, b_ref, kk_ref, v_ref):
    c = HG_CHUNK
    w = GROUP_WIDTH
    lb = lb_ref[...]
    q = q_ref[...]
    qs = q * jax.nn.sigmoid(q) * (HEAD_DIM ** -0.5)
    forget = lb + (1.0 - lb) * jax.nn.sigmoid(f_ref[...])
    kk = 1.0 - forget
    bcum = _tri_cumsum(tri_ref[...], jnp.log(forget))
    v = i_ref[...]
    b2 = bcum * LOG2_E
    b_ref[...] = b2
    kk_ref[...] = kk
    v_ref[...] = v

    out = _dot_nt((qs * jnp.exp(bcum)).astype(BF16), st_ref[...].astype(BF16))

    ones_bd = ones_ref[...]
    pieces = []
    for g in range(c // HG_SUB):
        r0 = g * HG_SUB
        nr = (r0 // HG_BLOCK + 1) * HG_BLOCK - r0
        qg = qs[r0:r0 + nr, :]
        bg = b2[r0:r0 + nr, :]
        trow = r0 + lax.broadcasted_iota(jnp.int32, (HG_SUB, w), 0)
        terms = []
        for j in range(HG_SUB):
            s = r0 + j
            d = qg * kk_ref[s:s + 1, :] * jnp.exp2(bg - b_ref[s:s + 1, :])
            if j > 0:
                head = jnp.where(trow >= s, d[:HG_SUB], 0.0)
                d = jnp.concatenate([head, d[HG_SUB:]], axis=0) if nr > HG_SUB else head
            terms.append(d.astype(BF16))
        red = _dot(jnp.concatenate(terms, axis=0), ones_bd)
        acc = red[0:nr] * v_ref[r0:r0 + 1, :]
        for j in range(1, HG_SUB):
            acc = acc + red[j * nr:(j + 1) * nr] * v_ref[r0 + j:r0 + j + 1, :]
        pieces.append((r0, acc))

    nb = c // HG_BLOCK
    hm = hm_ref[...]
    q_rows, k_rows, v_rows = [], [], []
    for j in range(nb - 1):
        blk = slice(j * HG_BLOCK, (j + 1) * HG_BLOCK)
        r_j = b_ref[(j + 1) * HG_BLOCK - 1:(j + 1) * HG_BLOCK, :]
        later = slice((j + 1) * HG_BLOCK, c)
        q_rows.append(qs[later] * jnp.exp2(b2[later] - r_j))
        k_blk = kk[blk] * jnp.exp2(r_j - b2[blk])
        k_rows.append(jnp.concatenate([k_blk] * GROUP_HEADS, axis=0) * hm)
        v_rows.append(jnp.concatenate([v[blk]] * GROUP_HEADS, axis=0) * hm)
    scores = _dot_nt(jnp.concatenate(q_rows, axis=0).astype(BF16),
                     jnp.concatenate(k_rows, axis=0).astype(BF16))
    scores = jnp.where(jm_ref[...] > 0.0, scores, 0.0)
    off = _dot(scores.astype(BF16), jnp.concatenate(v_rows, axis=0).astype(BF16))
    row0 = 0
    for j in range(nb - 1):
        n_later = c - (j + 1) * HG_BLOCK
        pieces.append(((j + 1) * HG_BLOCK, off[row0:row0 + n_later]))
        row0 += n_later

    for r0, piece in pieces:
        parts = [piece]
        if r0 > 0:
            parts.insert(0, jnp.zeros((r0, w), F32))
        if r0 + piece.shape[0] < c:
            parts.append(jnp.zeros((c - r0 - piece.shape[0], w), F32))
        out = out + (jnp.concatenate(parts, axis=0) if len(parts) > 1 else piece)

    b_last = bcum[c - 1:c, :]
    kt = (kk * jnp.exp(b_last - bcum)).astype(BF16)
    st_ref[...] = jnp.exp(b_last) * st_ref[...] + _dot_tn(v.astype(BF16), kt) * bd_ref[...]

    g_in = g_ref[...]
    o_ref[...] = _head_rms(out, gain_ref[...]) * (g_in * jax.nn.sigmoid(g_in))


def hgrn2_mixer_pallas(cols, b_, l_, lb, o_gain):
    c = HG_CHUNK
    rows = c * HG_CHUNKS_PER_STEP
    nt = l_ // rows
    w = GROUP_WIDTH
    head = np.arange(w) // HEAD_DIM
    same = (head[:, None] == head[None, :]).astype(np.float32)
    tri = np.tril(np.ones((c, c), np.float32))
    nb = c // HG_BLOCK
    row_head = np.arange(GROUP_HEADS * HG_BLOCK) // HG_BLOCK
    hm = (row_head[:, None] == head[None, :]).astype(np.float32)
    q_block = np.concatenate([np.full(c - (j + 1) * HG_BLOCK, j) for j in range(nb - 1)])
    k_block = np.arange((nb - 1) * GROUP_HEADS * HG_BLOCK) // (GROUP_HEADS * HG_BLOCK)
    jm = (q_block[:, None] == k_block[None, :]).astype(np.float32)
    col = lambda j: pl.BlockSpec((rows, w), lambda b, i: (b * nt + i, C_HG // w + j))
    const = lambda shape: pl.BlockSpec(shape, lambda b, i: (0,) * len(shape))
    return pl.pallas_call(
        _hgrn2_kernel,
        out_shape=jax.ShapeDtypeStruct((b_ * l_, w), F32),
        grid=(b_, nt),
        in_specs=[col(0), col(1), col(2), col(3), const((1, w)), const((1, HEAD_DIM)),
                  const((c, c)), const((w, w)), const((w, w)), const(hm.shape), const(jm.shape)],
        out_specs=pl.BlockSpec((rows, w), lambda b, i: (b * nt + i, 0)),
        scratch_shapes=[pltpu.VMEM((w, w), F32), pltpu.VMEM((c, w), F32),
                        pltpu.VMEM((c, w), F32), pltpu.VMEM((c, w), F32)],
        compiler_params=pltpu.CompilerParams(
            dimension_semantics=("arbitrary", "arbitrary"),
            vmem_limit_bytes=V7X_VMEM_LIMIT_BYTES),
        name="hgrn2",
    )(cols, cols, cols, cols, lb.reshape(1, w), o_gain.reshape(1, HEAD_DIM),
      jnp.asarray(tri, BF16), jnp.asarray(same, BF16), jnp.asarray(same, F32),
      jnp.asarray(hm), jnp.asarray(jm))


ML_TC = 256
ML_M_INIT = -1e30
ML_CHUNKS_PER_STEP = 2


def _mlstm_kernel(gate_ref, qk_ref, v_ref, og_ref, cw_ref, cb_ref, gb_ref, gain_ref, tril_ref,
                  triu_ref, o_ref, xprev_ref, cmat_ref, nvec_ref, m_ref):
    @pl.when(pl.program_id(1) == 0)
    def _():
        xprev_ref[...] = jnp.zeros_like(xprev_ref)
        cmat_ref[...] = jnp.zeros_like(cmat_ref)
        nvec_ref[...] = jnp.zeros_like(nvec_ref)
        m_ref[...] = jnp.full(m_ref.shape, ML_M_INIT, F32)

    for ci in range(ML_CHUNKS_PER_STEP):
        rows = pl.ds(ci * ML_TC, ML_TC)
        _mlstm_chunk(gate_ref.at[rows], qk_ref.at[rows], v_ref.at[rows], og_ref.at[rows], cw_ref,
                     cb_ref, gb_ref, gain_ref, tril_ref, triu_ref, o_ref.at[rows], xprev_ref,
                     cmat_ref, nvec_ref, m_ref)


def _mlstm_chunk(gate_ref, qk_ref, v_ref, og_ref, cw_ref, cb_ref, gb_ref, gain_ref, tril_ref,
                 triu_ref, o_ref, xprev_ref, cmat_ref, nvec_ref, m_ref):
    c = ML_TC
    nh = GROUP_HEADS
    w = GROUP_WIDTH

    x = qk_ref[...]
    prev = xprev_ref[...]
    row = lax.broadcasted_iota(jnp.int32, x.shape, 0)
    acc = x * cw_ref[CONV_WIDTH - 1:CONV_WIDTH, :] + cb_ref[...]
    for j in range(1, CONV_WIDTH):
        shifted = jnp.where(row < j, pltpu.roll(prev, j, 0), pltpu.roll(x, j, 0))
        acc = acc + shifted * cw_ref[CONV_WIDTH - 1 - j:CONV_WIDTH - j, :]
    xprev_ref[...] = x
    qk = acc * jax.nn.sigmoid(acc)
    q = qk[:, :w]
    k = qk[:, w:] * (HEAD_DIM ** -0.5)
    v = v_ref[...]

    pre = gate_ref[...] + gb_ref[...]
    lane = lax.broadcasted_iota(jnp.int32, pre.shape, 1)
    log_f = jnp.minimum(pre, 0.0) - jnp.log1p(jnp.exp(-jnp.abs(pre)))
    log_f = jnp.where((lane >= nh) & (lane < 2 * nh), log_f, 0.0)
    bcum_c = _tri_cumsum(tril_ref[...], log_f)
    bcum_r = _cumsum_tri_rows(log_f.T, triu_ref[...])
    pre_r = pre.T
    tri = (lax.broadcasted_iota(jnp.int32, (c, c), 0) >= lax.broadcasted_iota(jnp.int32, (c, c), 1))

    heads = [slice(h * HEAD_DIM, (h + 1) * HEAD_DIM) for h in range(nh)]
    qb = [q[:, sl].astype(BF16) for sl in heads]
    vb = [v[:, sl].astype(BF16) for sl in heads]
    qk_raw = [_dot_nt(qb[h], k[:, heads[h]].astype(BF16)) for h in range(nh)]
    q_state = [_dot(qb[h], cmat_ref[h].astype(BF16)) for h in range(nh)]

    s_all, m_all, w_all, upd = [], [], [], []
    for h in range(nh):
        bc = bcum_c[:, nh + h:nh + h + 1]
        li_c = pre[:, h:h + 1]
        a_r = pre_r[h:h + 1, :] - bcum_r[nh + h:nh + h + 1, :]
        m_prev = m_ref[h:h + 1, 0:1]
        log_d = jnp.where(tri, bc + a_r, -jnp.inf)
        inter = bc + m_prev
        m_t = jnp.maximum(inter, jnp.max(log_d, axis=-1, keepdims=True))
        s_all.append(qk_raw[h] * jnp.exp(log_d - m_t))
        m_all.append(m_t)
        w_all.append(jnp.exp(inter - m_t))
        b_last = bc[c - 1:c, :]
        log_w = b_last + (li_c - bc)
        m_new = jnp.maximum(b_last + m_prev, jnp.max(log_w, axis=0, keepdims=True))
        kw = k[:, heads[h]] * jnp.exp(log_w - m_new)
        upd.append((m_new, kw, jnp.exp(b_last + m_prev - m_new)))

    sv = [_dot(s_all[h].astype(BF16), vb[h]) for h in range(nh)]
    kv = [_dot_tn(upd[h][1].astype(BF16), vb[h]) for h in range(nh)]

    outs = []
    for h in range(nh):
        num = w_all[h] * q_state[h] + sv[h]
        den = (w_all[h] * jnp.sum(q[:, heads[h]] * nvec_ref[h], axis=-1, keepdims=True)
               + jnp.sum(s_all[h], axis=-1, keepdims=True))
        outs.append(num / jnp.maximum(jnp.abs(den), jnp.exp(-m_all[h])))
        m_new, kw, decay = upd[h]
        cmat_ref[h] = decay * cmat_ref[h] + kv[h]
        nvec_ref[h] = decay * nvec_ref[h] + jnp.sum(kw, axis=0, keepdims=True)
        m_ref[h:h + 1, :] = jnp.broadcast_to(m_new, (1, LANES))

    hh = _head_rms(jnp.concatenate(outs, axis=-1), gain_ref[...])
    o_ref[...] = hh * jax.nn.sigmoid(og_ref[...])


def mlstm_mixer_pallas(cols, b_, l_, conv_w, conv_b, i_bias, f_bias, o_gain):
    c = ML_TC
    rows = c * ML_CHUNKS_PER_STEP
    nt = l_ // rows
    w = GROUP_WIDTH
    gb = jnp.pad(jnp.concatenate([i_bias, f_bias]), (0, LANES - 2 * GROUP_HEADS)).reshape(1, LANES)
    tril = np.tril(np.ones((c, c), np.float32))
    const = lambda shape: pl.BlockSpec(shape, lambda b, i: (0,) * len(shape))
    blk = lambda width, off: pl.BlockSpec((rows, width), lambda b, i: (b * nt + i, off // width))
    return pl.pallas_call(
        _mlstm_kernel,
        out_shape=jax.ShapeDtypeStruct((b_ * l_, w), F32),
        grid=(b_, nt),
        in_specs=[blk(LANES, C_ML_GATE), blk(2 * w, C_ML_QK), blk(w, C_ML_V), blk(w, C_ML_OG),
                  const((CONV_WIDTH, 2 * w)), const((1, 2 * w)), const((1, LANES)),
                  const((1, HEAD_DIM)), const((c, c)), const((c, c))],
        out_specs=pl.BlockSpec((rows, w), lambda b, i: (b * nt + i, 0)),
        scratch_shapes=[pltpu.VMEM((c, 2 * w), F32),
                        pltpu.VMEM((GROUP_HEADS, HEAD_DIM, HEAD_DIM), F32),
                        pltpu.VMEM((GROUP_HEADS, 1, HEAD_DIM), F32),
                        pltpu.VMEM((8, LANES), F32)],
        compiler_params=pltpu.CompilerParams(
            dimension_semantics=("arbitrary", "arbitrary"),
            vmem_limit_bytes=V7X_VMEM_LIMIT_BYTES),
        name="mlstm",
    )(cols, cols, cols, cols, conv_w, conv_b.reshape(1, 2 * w), gb, o_gain.reshape(1, HEAD_DIM),
      jnp.asarray(tril, BF16), jnp.asarray(tril.T, BF16))


def kernel(x, mem, lb_param, norm_mix, w_in, w_out, hg_o_gain, dsa_kv_gain, dsa_w_uk, dsa_w_uv,
           dsa_q_gain, dsa_k_gain, dsa_idxk_gain, nsa_pos_k, nsa_pos_v, nsa_k_w1, nsa_k_w2,
           nsa_v_w1, nsa_v_w2, nsa_q_gain, nsa_k_gains, ml_conv_w, ml_conv_b, ml_i_bias,
           ml_f_bias, ml_o_gain, norm_xa, norm_mem, xa_wq, xa_wkv, xa_wo, xa_q_gain, xa_k_gain,
           norm_ffn, ffn_w13, ffn_w2):
    b_, l_, d = x.shape
    lb_all = jnp.cumsum(jax.nn.softmax(lb_param.astype(F32), axis=0), axis=0)
    lb_all = lb_all - lb_all[:1]
    x2d = x.reshape(b_ * l_, d)
    mem2d = mem.reshape(b_ * N_MEM, d)
    rope = rope_tables(l_)
    for l in range(DEPTH):
        cols = norm_matmul(x2d, norm_mix[l], pack_w_in_bf16(w_in[l]), tm=256)
        mixers = (
            hgrn2_mixer_pallas(cols, b_, l_, lb_all[l], hg_o_gain[l]),
            dsa_mixer_pallas(cols, b_, l_, rope, dsa_kv_gain[l], dsa_w_uk[l], dsa_w_uv[l],
                             dsa_q_gain[l], dsa_k_gain[l], dsa_idxk_gain[l]),
            nsa_mixer_pallas(cols, b_, l_, rope, nsa_pos_k[l], nsa_pos_v[l], nsa_k_w1[l],
                             nsa_k_w2[l], nsa_v_w1[l], nsa_v_w2[l], nsa_q_gain[l],
                             nsa_k_gains[l]),
            mlstm_mixer_pallas(cols, b_, l_, ml_conv_w[l], ml_conv_b[l], ml_i_bias[l],
                               ml_f_bias[l], ml_o_gain[l]),
        )
        k_mem, v_mem = mem_kv(mem2d, norm_mem[l], xa_wkv[l].astype(BF16), xa_k_gain[l])
        x2d = post_block(x2d, mixers, l_, w_out[l].astype(BF16), norm_xa[l],
                         xa_wq[l].astype(BF16), xa_q_gain[l], k_mem, v_mem,
                         xa_wo[l].astype(BF16), norm_ffn[l], ffn_w13[l].astype(BF16),
                         ffn_w2[l].astype(BF16), tm=512)
    return x2d.reshape(b_, l_, d)
```

```python
import functools

import jax
import jax.numpy as jnp
from jax import lax
import numpy as np
from jax.experimental import pallas as pl
from jax.experimental.pallas import tpu as pltpu

F32 = jnp.float32
BF16 = jnp.bfloat16

D_MODEL = 1024
DEPTH = 2
HEAD_DIM = 64
GROUP_HEADS = 4
GROUP_WIDTH = GROUP_HEADS * HEAD_DIM
ROPE_THETA = 500000.0
EPS = 1e-6
N_MEM = 256
XA_HEADS = 4
XA_WIDTH = XA_HEADS * HEAD_DIM
HG_CHUNK = 64
DSA_LATENT = 128
IDX_HEADS = 8
IDX_DIM = 32
DSA_TOPK_MAX = 256
CMP_BLOCK = 32
CMP_STRIDE = 16
SLC_BLOCK = 64
SLC_SHIFT = 6
SLC_TOPN = 16
WINDOW = 512
CONV_WIDTH = 4
D_FF = 2816

HG_SPLITS = (GROUP_WIDTH,) * 4
DSA_SPLITS = (GROUP_WIDTH, DSA_LATENT, IDX_HEADS * IDX_DIM, IDX_DIM, IDX_HEADS)
NSA_SPLITS = (GROUP_WIDTH,) + (HEAD_DIM,) * 6 + (3 * GROUP_HEADS,)
ML_SPLITS = (2 * GROUP_WIDTH, GROUP_WIDTH, GROUP_WIDTH, GROUP_HEADS, GROUP_HEADS)
GROUP_COLS = (sum(HG_SPLITS), sum(DSA_SPLITS), sum(NSA_SPLITS), sum(ML_SPLITS))
IN_COLS = sum(GROUP_COLS)

V7X_VMEM_LIMIT_BYTES = 56 * 1024 * 1024
LANES = 128
FF_CHUNK = 256


def _round_up(n, m):
    return -(-n // m) * m


def _rms_rows(t, g):
    return t * lax.rsqrt(jnp.mean(t * t, axis=-1, keepdims=True) + EPS) * g


def _const_spec(shape):
    return pl.BlockSpec(shape, lambda *_: (0,) * len(shape), pipeline_mode=pl.Buffered(1))


def _norm_matmul_kernel(x_ref, g_ref, w_ref, o_ref):
    h = _rms_rows(x_ref[...], g_ref[...]).astype(BF16)
    o_ref[...] = jnp.dot(h, w_ref[...], preferred_element_type=F32)


def norm_matmul(x2d, gain, w_bf16, tm):
    m, k = x2d.shape
    n = w_bf16.shape[1]
    return pl.pallas_call(
        _norm_matmul_kernel,
        out_shape=jax.ShapeDtypeStruct((m, n), F32),
        grid=(m // tm,),
        in_specs=[
            pl.BlockSpec((tm, k), lambda i: (i, 0)),
            _const_spec((1, k)),
            _const_spec((k, n)),
        ],
        out_specs=pl.BlockSpec((tm, n), lambda i: (i, 0)),
        compiler_params=pltpu.CompilerParams(
            dimension_semantics=("arbitrary",), vmem_limit_bytes=V7X_VMEM_LIMIT_BYTES),
        name="norm_matmul",
    )(x2d, gain.reshape(1, k), w_bf16)


def _mem_kv_kernel(m_ref, g_ref, w_ref, kg_ref, k_ref, v_ref):
    mn = _rms_rows(m_ref[...], g_ref[...]).astype(BF16)
    kv = jnp.dot(mn, w_ref[...], preferred_element_type=F32)
    kg = kg_ref[...]
    ks = []
    for h in range(XA_HEADS):
        kh = kv[:, h * HEAD_DIM:(h + 1) * HEAD_DIM]
        ks.append(_rms_rows(kh, kg))
    k_ref[...] = jnp.concatenate(ks, axis=-1).astype(BF16)
    v_ref[...] = kv[:, XA_WIDTH:].astype(BF16)


def mem_kv(mem2d, gain, wkv_bf16, k_gain):
    m, k = mem2d.shape
    return pl.pallas_call(
        _mem_kv_kernel,
        out_shape=(jax.ShapeDtypeStruct((m, XA_WIDTH), BF16),
                   jax.ShapeDtypeStruct((m, XA_WIDTH), BF16)),
        grid=(m // N_MEM,),
        in_specs=[
            pl.BlockSpec((N_MEM, k), lambda i: (i, 0)),
            _const_spec((1, k)),
            _const_spec((k, 2 * XA_WIDTH)),
            _const_spec((1, HEAD_DIM)),
        ],
        out_specs=(pl.BlockSpec((N_MEM, XA_WIDTH), lambda i: (i, 0)),
                   pl.BlockSpec((N_MEM, XA_WIDTH), lambda i: (i, 0))),
        compiler_params=pltpu.CompilerParams(
            dimension_semantics=("arbitrary",), vmem_limit_bytes=V7X_VMEM_LIMIT_BYTES),
        name="mem_kv",
    )(mem2d, gain.reshape(1, k), wkv_bf16, k_gain.reshape(1, HEAD_DIM))


def _post_kernel(x_ref, mhg_ref, mdsa_ref, mnsa_ref, mml_ref, wout_ref, gxa_ref, wq_ref, qg_ref,
                 k_ref, v_ref, wo_ref, gffn_ref, w13_ref, w2_ref, o_ref):
    x = x_ref[...]
    for g, mix_ref in enumerate((mhg_ref, mdsa_ref, mnsa_ref, mml_ref)):
        x = x + _dot(mix_ref[...].astype(BF16),
                     wout_ref[g * GROUP_WIDTH:(g + 1) * GROUP_WIDTH, :])
    h = _rms_rows(x, gxa_ref[...]).astype(BF16)
    q = _dot(h, wq_ref[...])
    qg = qg_ref[...] * (HEAD_DIM ** -0.5)
    k = k_ref[...]
    v = v_ref[...]
    heads = [slice(hd * HEAD_DIM, (hd + 1) * HEAD_DIM) for hd in range(XA_HEADS)]
    scores = [_dot_nt(_rms_rows(q[:, sl], qg).astype(BF16), k[:, sl]) for sl in heads]
    probs = []
    for s in scores:
        e = jnp.exp(s - jnp.max(s, axis=-1, keepdims=True))
        probs.append((e / jnp.sum(e, axis=-1, keepdims=True)).astype(BF16))
    o = jnp.concatenate([_dot(p, v[:, sl]) for p, sl in zip(probs, heads)], axis=-1)
    x = x + _dot(o.astype(BF16), wo_ref[...])
    h = _rms_rows(x, gffn_ref[...]).astype(BF16)

    def up(c):
        return (_dot(h, w13_ref[:, c * FF_CHUNK:(c + 1) * FF_CHUNK]),
                _dot(h, w13_ref[:, D_FF + c * FF_CHUNK:D_FF + (c + 1) * FF_CHUNK]))

    n_chunks = D_FF // FF_CHUNK
    acc = x
    a, b = up(0)
    for c in range(n_chunks):
        nxt = up(c + 1) if c + 1 < n_chunks else None
        act = (a * jax.nn.sigmoid(a) * b).astype(BF16)
        acc = acc + _dot(act, w2_ref[c * FF_CHUNK:(c + 1) * FF_CHUNK, :])
        if nxt is not None:
            a, b = nxt
    o_ref[...] = acc


def post_block(x2d, mixers, seq, wout, gxa, wq, qg, k_mem, v_mem, wo, gffn, w13, w2, tm):
    m, d = x2d.shape
    tiles_per_batch = seq // tm
    row = lambda i: (i, 0)
    mem_row = lambda i: (i // tiles_per_batch, 0)
    return pl.pallas_call(
        _post_kernel,
        out_shape=jax.ShapeDtypeStruct((m, d), F32),
        grid=(m // tm,),
        in_specs=[
            pl.BlockSpec((tm, d), row),
            *[pl.BlockSpec((tm, GROUP_WIDTH), row) for _ in mixers],
            _const_spec(wout.shape),
            _const_spec((1, d)),
            _const_spec(wq.shape),
            _const_spec((1, HEAD_DIM)),
            pl.BlockSpec((N_MEM, XA_WIDTH), mem_row),
            pl.BlockSpec((N_MEM, XA_WIDTH), mem_row),
            _const_spec(wo.shape),
            _const_spec((1, d)),
            _const_spec(w13.shape),
            _const_spec(w2.shape),
        ],
        out_specs=pl.BlockSpec((tm, d), row),
        compiler_params=pltpu.CompilerParams(
            dimension_semantics=("arbitrary",), vmem_limit_bytes=V7X_VMEM_LIMIT_BYTES),
        name="post_block",
    )(x2d, *mixers, wout, gxa.reshape(1, d), wq, qg.reshape(1, HEAD_DIM), k_mem, v_mem, wo,
      gffn.reshape(1, d), w13, w2)


C_HG = 0
C_DSA_Q = 1024
C_DSA_IQ = 1280
C_NSA_Q = 1536
C_ML_V = 1792
C_ML_QK = 2048
C_ML_OG = 2560
C_DSA_CKV = 2816
C_DSA_IK = 2944
C_DSA_IW = 3072
C_NSA_GATE = 3200
C_NSA_CMP = 3328
C_NSA_SLC = 3456
C_NSA_WIN = 3584
C_ML_GATE = 3712
IN_COLS_PACKED = 3840


def _packed_source_columns():
    dsa0 = GROUP_COLS[0]
    nsa0 = dsa0 + GROUP_COLS[1]
    ml0 = nsa0 + GROUP_COLS[2]
    segs = [(0, 1024, 1024), (dsa0, 256, 256), (dsa0 + 384, 256, 256), (nsa0, 256, 256),
            (ml0 + 512, 256, 256), (ml0, 512, 512), (ml0 + 768, 256, 256), (dsa0 + 256, 128, 128),
            (dsa0 + 640, 32, LANES), (dsa0 + 672, 8, LANES), (nsa0 + 640, 12, LANES),
            (nsa0 + 256, 128, 128), (nsa0 + 384, 128, 128), (nsa0 + 512, 128, 128),
            (ml0 + 1024, 8, LANES)]
    src = np.concatenate([np.concatenate([np.arange(a, a + n), np.full(width - n, -1)])
                          for a, n, width in segs]).astype(np.int32)
    assert src.shape == (IN_COLS_PACKED,)
    return src


PACK_TN = 256


def _pack_w_kernel(w_ref, src_ref, o_ref, wb_ref):
    @pl.when(pl.program_id(0) == 0)
    def _():
        wb_ref[...] = w_ref[...].astype(BF16)

    row = lax.broadcasted_iota(jnp.int32, (wb_ref.shape[1], PACK_TN), 0)
    sel = jnp.where(row == src_ref[...], 1.0, 0.0).astype(BF16)
    o_ref[...] = _dot(wb_ref[...], sel).astype(BF16)


def pack_w_in_bf16(w):
    d, n = w.shape
    n_pad = _round_up(n, LANES)
    w_pad = jnp.pad(w, ((0, 0), (0, n_pad - n)))
    src = jnp.asarray(_packed_source_columns()).reshape(1, IN_COLS_PACKED)
    return pl.pallas_call(
        _pack_w_kernel,
        out_shape=jax.ShapeDtypeStruct((d, IN_COLS_PACKED), BF16),
        grid=(IN_COLS_PACKED // PACK_TN,),
        in_specs=[_const_spec((d, n_pad)), pl.BlockSpec((1, PACK_TN), lambda j: (0, j))],
        out_specs=pl.BlockSpec((d, PACK_TN), lambda j: (0, j)),
        scratch_shapes=[pltpu.VMEM((d, n_pad), BF16)],
        compiler_params=pltpu.CompilerParams(
            dimension_semantics=("arbitrary",), vmem_limit_bytes=V7X_VMEM_LIMIT_BYTES),
        name="pack_w_in",
    )(w_pad, src)


def rope_tables(l_, d=HEAD_DIM):
    rd = d // 4
    half = rd // 2
    inv = ROPE_THETA ** (-jnp.arange(half, dtype=F32) * 2.0 / rd)
    ang = jnp.arange(l_).astype(F32)[:, None] * inv[None, :]
    cos, sin = lax.optimization_barrier((jnp.cos(ang), jnp.sin(ang)))
    zh = jnp.zeros((l_, half), F32)
    rest0 = jnp.zeros((l_, d - rd), F32)
    c = jnp.concatenate([cos, cos, rest0 + 1.0], axis=1)
    s1 = jnp.concatenate([-sin, zh, rest0], axis=1)
    s2 = jnp.concatenate([zh, sin, rest0], axis=1)
    return jnp.stack([c, s1, s2])


def rope_tables_pad(rt, width):
    n = width - rt.shape[-1]
    ident = jnp.stack([jnp.ones(rt.shape[1:2] + (n,), F32), jnp.zeros(rt.shape[1:2] + (n,), F32),
                       jnp.zeros(rt.shape[1:2] + (n,), F32)])
    return jnp.concatenate([rt, ident], axis=-1)


def rope_tables_kv_pair(rt):
    return rope_tables_pad(rt, 2 * rt.shape[-1])


def _apply_rope(t, rope_ref, half=HEAD_DIM // 8):
    w = t.shape[-1]
    return (t * rope_ref[0] + pltpu.roll(t, w - half, 1) * rope_ref[1]
            + pltpu.roll(t, half, 1) * rope_ref[2])


_NT = (((1,), (1,)), ((), ()))


def _dot_nt(a, b):
    return lax.dot_general(a, b, _NT, preferred_element_type=F32)


def _dot(a, b):
    return jnp.dot(a, b, preferred_element_type=F32)


NEG_BIG = -(2.0 ** 30)
NSA_KEY_TILE = 512
NSA_TQ = 128
NSA_SEL_TQ = 256
NSA_PREP_TK = 512


def _nsa_kv_prep_kernel(ps_ref, pw_ref, rope_ref, gs_ref, gw_ref, kaug_ref, vs_ref, kw_ref, vw_ref):
    tk = ps_ref.shape[0]
    lane = lax.broadcasted_iota(jnp.int32, (tk, LANES), 1)
    is_k = lane < HEAD_DIM

    def norm_rope(p, g):
        ms = jnp.sum(jnp.where(is_k, p * p, 0.0), axis=-1, keepdims=True) * (1.0 / HEAD_DIM)
        y = jnp.where(is_k, p * lax.rsqrt(ms + EPS) * g, p)
        return _apply_rope(y, rope_ref)

    ys = norm_rope(ps_ref[...], gs_ref[...])
    yw = norm_rope(pw_ref[...], gw_ref[...])
    row = pl.program_id(1) * tk + lax.broadcasted_iota(jnp.int32, (tk, LANES), 0)
    ind = jnp.where(jnp.right_shift(row, SLC_SHIFT) == (lane - HEAD_DIM), 1.0, 0.0)
    kaug_ref[0] = jnp.where(is_k, ys, ind).astype(BF16)
    vs_ref[0] = ys.T[HEAD_DIM:, :].astype(BF16)
    kw_ref[0] = yw[:, :HEAD_DIM].astype(BF16)
    vw_ref[0] = yw.T[HEAD_DIM:, :].astype(BF16)


def nsa_kv_prep(cols, b_, l_, rope_pair, g_slc, g_win):
    tk = NSA_PREP_TK
    nt = l_ // tk
    ones = jnp.ones((HEAD_DIM,), F32)
    gs = jnp.concatenate([g_slc, ones]).reshape(1, LANES)
    gw = jnp.concatenate([g_win, ones]).reshape(1, LANES)
    kv = lambda w: jax.ShapeDtypeStruct((b_, l_, w), BF16)
    kv_t = jax.ShapeDtypeStruct((b_, HEAD_DIM, l_), BF16)
    out_blk = lambda w: pl.BlockSpec((1, tk, w), lambda b, i: (b, i, 0))
    out_t = pl.BlockSpec((1, HEAD_DIM, tk), lambda b, i: (b, 0, i))
    return pl.pallas_call(
        _nsa_kv_prep_kernel,
        out_shape=(kv(LANES), kv_t, kv(HEAD_DIM), kv_t),
        grid=(b_, nt),
        in_specs=[
            pl.BlockSpec((tk, LANES), lambda b, i: (b * nt + i, C_NSA_SLC // LANES)),
            pl.BlockSpec((tk, LANES), lambda b, i: (b * nt + i, C_NSA_WIN // LANES)),
            pl.BlockSpec((3, tk, LANES), lambda b, i: (0, i, 0)),
            pl.BlockSpec((1, LANES), lambda b, i: (0, 0)),
            pl.BlockSpec((1, LANES), lambda b, i: (0, 0)),
        ],
        out_specs=(out_blk(LANES), out_t, out_blk(HEAD_DIM), out_t),
        compiler_params=pltpu.CompilerParams(
            dimension_semantics=("arbitrary", "arbitrary"),
            vmem_limit_bytes=V7X_VMEM_LIMIT_BYTES),
        name="nsa_kv_prep",
    )(cols, cols, rope_pair, gs, gw)


def _nsa_compress_kernel(r_ref, pea_ref, peb_ref, w1a_ref, w1b_ref, w2k_ref, w2v_ref, kg_ref,
                         kc_ref, vc_ref):
    r = r_ref[0]
    n = r.shape[0]
    a = _dot((r + pea_ref[...]).astype(BF16), w1a_ref[...])
    bm = _dot((r + peb_ref[...]).astype(BF16), w1b_ref[...])
    row = lax.broadcasted_iota(jnp.int32, bm.shape, 0)
    bm_up = jnp.where(row < n - 1, pltpu.roll(bm, n - 1, 0), 0.0)
    h = jnp.maximum(a + bm_up, 0.0).astype(BF16)
    hid = w2k_ref.shape[0]
    ck = _dot(h[:, :hid], w2k_ref[...])
    cv = _dot(h[:, hid:], w2v_ref[...])
    kc_ref[0] = _rms_rows(ck, kg_ref[...]).astype(BF16)
    vc_ref[0] = cv.astype(BF16)


def nsa_compress(cols, b_, l_, pos_k, pos_v, k_w1, k_w2, v_w1, v_w2, k_gain):
    rows = l_ // CMP_STRIDE
    hid = k_w1.shape[1]
    pair = cols[:, C_NSA_CMP:C_NSA_CMP + LANES].reshape(b_, rows, CMP_STRIDE * LANES)

    def interleave_pe(lo):
        pe = jnp.concatenate([pos_k[lo:lo + CMP_STRIDE], pos_v[lo:lo + CMP_STRIDE]], axis=1)
        return pe.reshape(1, CMP_STRIDE * LANES)

    def interleave_w(lo):
        wk = k_w1[lo * HEAD_DIM:(lo + CMP_STRIDE) * HEAD_DIM].reshape(CMP_STRIDE, HEAD_DIM, hid)
        wv = v_w1[lo * HEAD_DIM:(lo + CMP_STRIDE) * HEAD_DIM].reshape(CMP_STRIDE, HEAD_DIM, hid)
        z = jnp.zeros_like(wk)
        top = jnp.concatenate([wk, z], axis=2)
        bot = jnp.concatenate([z, wv], axis=2)
        return jnp.concatenate([top, bot], axis=1).reshape(CMP_STRIDE * LANES, 2 * hid).astype(BF16)

    out = jax.ShapeDtypeStruct((b_, rows, HEAD_DIM), BF16)
    return pl.pallas_call(
        _nsa_compress_kernel,
        out_shape=(out, out),
        grid=(b_,),
        in_specs=[
            pl.BlockSpec((1, rows, CMP_STRIDE * LANES), lambda b: (b, 0, 0)),
            _const_spec((1, CMP_STRIDE * LANES)),
            _const_spec((1, CMP_STRIDE * LANES)),
            _const_spec((CMP_STRIDE * LANES, 2 * hid)),
            _const_spec((CMP_STRIDE * LANES, 2 * hid)),
            _const_spec((hid, HEAD_DIM)),
            _const_spec((hid, HEAD_DIM)),
            _const_spec((1, HEAD_DIM)),
        ],
        out_specs=(pl.BlockSpec((1, rows, HEAD_DIM), lambda b: (b, 0, 0)),
                   pl.BlockSpec((1, rows, HEAD_DIM), lambda b: (b, 0, 0))),
        compiler_params=pltpu.CompilerParams(
            dimension_semantics=("arbitrary",), vmem_limit_bytes=V7X_VMEM_LIMIT_BYTES),
        name="nsa_compress",
    )(pair, interleave_pe(0), interleave_pe(CMP_STRIDE), interleave_w(0),
      interleave_w(CMP_STRIDE), k_w2.astype(BF16), v_w2.astype(BF16),
      k_gain.reshape(1, HEAD_DIM))


def _masked_softmax(s, valid, axis):
    m = jnp.max(jnp.where(valid, s, -jnp.inf), axis=axis, keepdims=True)
    m = jnp.where(m == -jnp.inf, 0.0, m)
    e = jnp.where(valid, jnp.exp(s - m), 0.0)
    den = jnp.sum(e, axis=axis, keepdims=True)
    return e / jnp.where(den > 0, den, 1.0)


def _nsa_select_kernel(q_ref, kc_ref, vc_ref, ovt_ref, rope_ref, qg_ref, qaug_ref, ocmp_ref,
                       *, n_sel):
    tq = q_ref.shape[0]
    ncr = kc_ref.shape[1]
    n_slc = ovt_ref.shape[0]
    t0 = pl.program_id(1) * tq
    scale = HEAD_DIM ** -0.5
    q = q_ref[...]
    g = qg_ref[...]
    kc = kc_ref[0]
    vc = vc_ref[0]
    vis_t = (lax.broadcasted_iota(jnp.int32, (ncr, tq), 0) * CMP_STRIDE + (CMP_BLOCK - 1)
             <= t0 + lax.broadcasted_iota(jnp.int32, (ncr, tq), 1))
    qn_heads, o_heads = [], []
    psum_t = jnp.zeros((ncr, tq), F32)
    for h in range(GROUP_HEADS):
        qn = _rms_rows(q[:, h * HEAD_DIM:(h + 1) * HEAD_DIM], g)
        qn_heads.append(qn)
        p_t = _masked_softmax(_dot_nt(kc, (qn * scale).astype(BF16)), vis_t, 0)
        o_heads.append(_dot_tn(vc, p_t.astype(BF16)))
        psum_t = psum_t + p_t
    ocmp_ref[0] = jnp.concatenate(o_heads, axis=0)

    hi = psum_t.astype(BF16)
    lo = (psum_t - hi.astype(F32)).astype(BF16)
    imp = _dot(ovt_ref[...], hi) + _dot(ovt_ref[...], lo)
    blk = lax.broadcasted_iota(jnp.int32, (n_slc, tq), 0)
    cur = jnp.right_shift(t0 + lax.broadcasted_iota(jnp.int32, (n_slc, tq), 1), SLC_SHIFT)
    forced = (blk == 0) | (blk == cur) | (blk == cur - 1)
    imp = jnp.where(forced, jnp.inf, jnp.where(blk > cur, -jnp.inf, imp))
    n_grp = n_slc // SUBLANES
    groups = [imp[g * SUBLANES:(g + 1) * SUBLANES] for g in range(n_grp)]
    cnts = [jnp.zeros((SUBLANES, tq), F32) for _ in range(n_grp)]
    sub = lax.broadcasted_iota(jnp.int32, (SUBLANES, tq), 0)
    for m in range(n_slc):
        row = imp[m:m + 1, :]
        gm, rm = divmod(m, SUBLANES)
        for g in range(n_grp):
            if g < gm:
                beats = row > groups[g]
            elif g > gm:
                beats = row >= groups[g]
            else:
                beats = (row > groups[g]) | ((row == groups[g]) & (sub > rm))
            cnts[g] = cnts[g] + jnp.where(beats, 1.0, 0.0)
    cnt = jnp.concatenate(cnts, axis=0)
    mt = jnp.where(cnt < n_sel, 0.0, NEG_BIG)
    pad = jnp.zeros((HEAD_DIM - n_slc, tq), F32)
    mt = jnp.concatenate([mt, pad, mt, pad], axis=0) if n_slc < HEAD_DIM else jnp.concatenate(
        [mt, mt], axis=0)
    mt = mt.T

    qr = _apply_rope(jnp.concatenate(qn_heads, axis=-1), rope_ref) * scale
    lane = lax.broadcasted_iota(jnp.int32, (tq, LANES), 1)
    for j in range(GROUP_HEADS // 2):
        pair = qr[:, j * LANES:(j + 1) * LANES]
        swapped = pltpu.roll(pair, HEAD_DIM, 1)
        qaug_ref[0, :, (2 * j) * LANES:(2 * j + 1) * LANES] = jnp.where(
            lane < HEAD_DIM, pair, mt).astype(BF16)
        qaug_ref[0, :, (2 * j + 1) * LANES:(2 * j + 2) * LANES] = jnp.where(
            lane < HEAD_DIM, swapped, mt).astype(BF16)


def nsa_select(cols, b_, l_, k_cmp, v_cmp, rope_q, q_gain):
    tq = min(NSA_SEL_TQ, l_)
    nt = l_ // tq
    ncr = l_ // CMP_STRIDE
    n_slc = l_ // SLC_BLOCK
    n_sel = min(SLC_TOPN, n_slc)
    st_c = np.arange(ncr) * CMP_STRIDE
    st_s = np.arange(n_slc) * SLC_BLOCK
    ovt = ((st_c[None, :] < st_s[:, None] + SLC_BLOCK)
           & (st_c[None, :] + CMP_BLOCK > st_s[:, None])).astype(np.float32)
    return pl.pallas_call(
        functools.partial(_nsa_select_kernel, n_sel=n_sel),
        out_shape=(jax.ShapeDtypeStruct((b_, l_, GROUP_HEADS * LANES), BF16),
                   jax.ShapeDtypeStruct((b_, GROUP_WIDTH, l_), F32)),
        grid=(b_, nt),
        in_specs=[
            pl.BlockSpec((tq, GROUP_WIDTH), lambda b, i: (b * nt + i, C_NSA_Q // GROUP_WIDTH)),
            pl.BlockSpec((1, ncr, HEAD_DIM), lambda b, i: (b, 0, 0)),
            pl.BlockSpec((1, ncr, HEAD_DIM), lambda b, i: (b, 0, 0)),
            pl.BlockSpec((n_slc, ncr), lambda b, i: (0, 0)),
            pl.BlockSpec((3, tq, GROUP_WIDTH), lambda b, i: (0, i, 0)),
            pl.BlockSpec((1, HEAD_DIM), lambda b, i: (0, 0)),
        ],
        out_specs=(pl.BlockSpec((1, tq, GROUP_HEADS * LANES), lambda b, i: (b, i, 0)),
                   pl.BlockSpec((1, GROUP_WIDTH, tq), lambda b, i: (b, 0, i))),
        compiler_params=pltpu.CompilerParams(
            dimension_semantics=("arbitrary", "arbitrary"),
            vmem_limit_bytes=V7X_VMEM_LIMIT_BYTES),
        name="nsa_select",
    )(cols, k_cmp, v_cmp, jnp.asarray(ovt, BF16), rope_q, q_gain.reshape(1, HEAD_DIM))


def _softmax_stats_update(s, m, l):
    m_new = jnp.maximum(m, jnp.max(s, axis=0, keepdims=True))
    alpha = jnp.exp(m - m_new)
    p = jnp.exp(s - m_new)
    return m_new, alpha, p, alpha * l + jnp.sum(p, axis=0, keepdims=True)


def _nsa_attend_t_kernel(qaug_ref, kaug_ref, vst_ref, kw_ref, vwt_ref, ocmp_ref, gate_ref, o_ref):
    tq = qaug_ref.shape[1]
    nh = GROUP_HEADS
    ts = NSA_KEY_TILE
    t0 = pl.program_id(1) * tq
    qa = qaug_ref[0]
    qs = jnp.concatenate([qa[:, h * LANES:(h + 1) * LANES] for h in range(nh)], axis=0)
    n = nh * tq
    qpos_tile = t0 + lax.broadcasted_iota(jnp.int32, (1, tq), 1)
    qpos = jnp.concatenate([qpos_tile] * nh, axis=1)

    def key_tile(kt, carry):
        m, l, acc = carry
        k0 = pl.multiple_of(kt * ts, ts)
        s = _dot_nt(kaug_ref[0, pl.ds(k0, ts), :], qs)
        kpos = k0 + lax.broadcasted_iota(jnp.int32, (ts, n), 0)
        s = jnp.where(kpos <= qpos, s, NEG_BIG)
        m, alpha, p, l = _softmax_stats_update(s, m, l)
        acc = alpha * acc + _dot(vst_ref[0, :, pl.ds(k0, ts)], p.astype(BF16))
        return m, l, acc

    n_kt = (t0 + tq + ts - 1) // ts
    init = (jnp.full((1, n), -jnp.inf, F32), jnp.zeros((1, n), F32),
            jnp.zeros((HEAD_DIM, n), F32))
    _, l, acc = lax.fori_loop(0, n_kt, key_tile, init)
    o_slc = acc / l

    wlen = WINDOW + tq
    start = pl.multiple_of(jnp.maximum(t0 - WINDOW, 0), tq)
    sw = _dot_nt(kw_ref[0, pl.ds(start, wlen), :], qs[:, :HEAD_DIM])
    dist = qpos - (start + lax.broadcasted_iota(jnp.int32, (wlen, n), 0))
    sw = jnp.where((dist >= 0) & (dist < WINDOW), sw, -jnp.inf)
    e = jnp.exp(sw - jnp.max(sw, axis=0, keepdims=True))
    o_swa = _dot(vwt_ref[0, :, pl.ds(start, wlen)], e.astype(BF16)) / jnp.sum(
        e, axis=0, keepdims=True)

    g = jax.nn.sigmoid(gate_ref[...]).T
    oc = ocmp_ref[0]
    outs = []
    for h in range(nh):
        cols_h = slice(h * tq, (h + 1) * tq)
        outs.append(g[h:h + 1, :] * oc[h * HEAD_DIM:(h + 1) * HEAD_DIM, :]
                    + g[nh + h:nh + h + 1, :] * o_slc[:, cols_h]
                    + g[2 * nh + h:2 * nh + h + 1, :] * o_swa[:, cols_h])
    o_ref[...] = jnp.concatenate(outs, axis=0).T


def nsa_attend(cols, b_, l_, q_aug, k_aug, v_slc, k_win, v_win, o_cmp):
    tq = NSA_TQ
    nt = l_ // tq
    seq = lambda w: pl.BlockSpec((1, l_, w), lambda b, i: (b, 0, 0))
    seq_t = pl.BlockSpec((1, HEAD_DIM, l_), lambda b, i: (b, 0, 0))
    return pl.pallas_call(
        _nsa_attend_t_kernel,
        out_shape=jax.ShapeDtypeStruct((b_ * l_, GROUP_WIDTH), F32),
        grid=(b_, nt),
        in_specs=[
            pl.BlockSpec((1, tq, GROUP_HEADS * LANES), lambda b, i: (b, i, 0)),
            seq(LANES), seq_t, seq(HEAD_DIM), seq_t,
            pl.BlockSpec((1, GROUP_WIDTH, tq), lambda b, i: (b, 0, i)),
            pl.BlockSpec((tq, LANES), lambda b, i: (b * nt + i, C_NSA_GATE // LANES)),
        ],
        out_specs=pl.BlockSpec((tq, GROUP_WIDTH), lambda b, i: (b * nt + i, 0)),
        compiler_params=pltpu.CompilerParams(
            dimension_semantics=("arbitrary", "arbitrary"),
            vmem_limit_bytes=V7X_VMEM_LIMIT_BYTES),
        name="nsa_attend",
    )(q_aug, k_aug, v_slc, k_win, v_win, o_cmp, cols)


def nsa_mixer_pallas(cols, b_, l_, rope, pos_k, pos_v, k_w1, k_w2, v_w1, v_w2, q_gain, k_gains):
    k_aug, v_slc, k_win, v_win = nsa_kv_prep(cols, b_, l_, rope_tables_kv_pair(rope),
                                             k_gains[1], k_gains[2])
    k_cmp, v_cmp = nsa_compress(cols, b_, l_, pos_k, pos_v, k_w1, k_w2, v_w1, v_w2, k_gains[0])
    q_aug, o_cmp = nsa_select(cols, b_, l_, k_cmp, v_cmp, jnp.tile(rope, (1, 1, GROUP_HEADS)),
                              q_gain)
    return nsa_attend(cols, b_, l_, q_aug, k_aug, v_slc, k_win, v_win, o_cmp)


DSA_TQ = 256
DSA_KEY_TILE = 512
DSA_PREP_TK = 512
IDX_PACK = LANES
INT_MIN = -2 ** 31
MASKED_SCORE = -1e30


def _split_hi_lo(t):
    hi = t.astype(BF16)
    lo = (t - hi.astype(F32)).astype(BF16)
    return hi, lo


def _placement(rows, cols, pairs):
    p = np.zeros((rows, cols), np.float32)
    for r, c in pairs:
        p[r, c] = 1.0
    return jnp.asarray(p, BF16)


def _dsa_kv_prep_kernel(ckv_ref, ik_ref, rope_ref, ropei_ref, kvg_ref, wkv_ref, kg_ref, ikg_ref,
                        pkh_ref, pkl_ref, k_ref, vt_ref, ik3_ref):
    tk = ckv_ref.shape[0]
    lane = lax.broadcasted_iota(jnp.int32, (tk, LANES), 1)
    ckv = _rms_rows(ckv_ref[...], kvg_ref[...]).astype(BF16)
    kv = _dot(ckv, wkv_ref[...])
    is_k = lane < HEAD_DIM
    ms = jnp.sum(jnp.where(is_k, kv * kv, 0.0), axis=-1, keepdims=True) * (1.0 / HEAD_DIM)
    y = _apply_rope(jnp.where(is_k, kv * lax.rsqrt(ms + EPS) * kg_ref[...], kv), rope_ref)
    k_ref[0] = y[:, :HEAD_DIM].astype(BF16)
    vt_ref[0] = y.T[HEAD_DIM:, :].astype(BF16)
    ik = ik_ref[...]
    ms = jnp.sum(ik * ik, axis=-1, keepdims=True) * (1.0 / IDX_DIM)
    ikn = _apply_rope(ik * lax.rsqrt(ms + EPS) * ikg_ref[...], ropei_ref, IDX_DIM // 8)
    hi, lo = _split_hi_lo(ikn)
    ik3_ref[0] = (_dot(hi, pkh_ref[...]) + _dot(lo, pkl_ref[...])).astype(BF16)


def dsa_kv_prep(cols, b_, l_, rope_pair, rope_idx, kv_gain, w_uk, w_uv, k_gain, idxk_gain):
    tk = DSA_PREP_TK
    nt = l_ // tk
    ones = jnp.ones((HEAD_DIM,), F32)
    kg = jnp.concatenate([k_gain, ones]).reshape(1, LANES)
    ikg = jnp.pad(idxk_gain, (0, LANES - IDX_DIM)).reshape(1, LANES)
    wkv = jnp.concatenate([w_uk, w_uv], axis=1).astype(BF16)
    d = range(IDX_DIM)
    pkh = _placement(LANES, IDX_PACK, [(i, i) for i in d] + [(i, 2 * IDX_DIM + i) for i in d])
    pkl = _placement(LANES, IDX_PACK, [(i, IDX_DIM + i) for i in d])
    const = lambda shape: pl.BlockSpec(shape, lambda b, i: (0,) * len(shape))
    out = lambda w: pl.BlockSpec((1, tk, w), lambda b, i: (b, i, 0))
    return pl.pallas_call(
        _dsa_kv_prep_kernel,
        out_shape=(jax.ShapeDtypeStruct((b_, l_, HEAD_DIM), BF16),
                   jax.ShapeDtypeStruct((b_, HEAD_DIM, l_), BF16),
                   jax.ShapeDtypeStruct((b_, l_, IDX_PACK), BF16)),
        grid=(b_, nt),
        in_specs=[
            pl.BlockSpec((tk, LANES), lambda b, i: (b * nt + i, C_DSA_CKV // LANES)),
            pl.BlockSpec((tk, LANES), lambda b, i: (b * nt + i, C_DSA_IK // LANES)),
            pl.BlockSpec((3, tk, LANES), lambda b, i: (0, i, 0)),
            pl.BlockSpec((3, tk, LANES), lambda b, i: (0, i, 0)),
            const((1, LANES)), const((DSA_LATENT, LANES)), const((1, LANES)), const((1, LANES)),
            const((LANES, IDX_PACK)), const((LANES, IDX_PACK)),
        ],
        out_specs=(out(HEAD_DIM), pl.BlockSpec((1, HEAD_DIM, tk), lambda b, i: (b, 0, i)),
                   out(IDX_PACK)),
        compiler_params=pltpu.CompilerParams(
            dimension_semantics=("arbitrary", "arbitrary"),
            vmem_limit_bytes=V7X_VMEM_LIMIT_BYTES),
        name="dsa_kv_prep",
    )(cols, cols, rope_pair, rope_idx, kv_gain.reshape(1, DSA_LATENT), wkv, kg, ikg, pkh, pkl)


def _dsa_q_prep_kernel(q_ref, iq_ref, rope_ref, ropei_ref, qg_ref, pqh_ref, pql_ref,
                       qh_ref, iq3_ref):
    g = qg_ref[...]
    q = q_ref[...]
    qn = jnp.concatenate([_rms_rows(q[:, h * HEAD_DIM:(h + 1) * HEAD_DIM], g)
                          for h in range(GROUP_HEADS)], axis=-1)
    qr = _apply_rope(qn, rope_ref) * (HEAD_DIM ** -0.5)
    for h in range(GROUP_HEADS):
        qh_ref[0, h] = qr[:, h * HEAD_DIM:(h + 1) * HEAD_DIM].astype(BF16)
    hi, lo = _split_hi_lo(_apply_rope(iq_ref[...], ropei_ref, IDX_DIM // 8))
    iq3_ref[0] = (_dot(hi, pqh_ref[...]) + _dot(lo, pql_ref[...])).astype(BF16)


def dsa_q_prep(cols, b_, l_, rope_q, rope_iq, q_gain):
    tq = 256
    nt = l_ // tq
    w = IDX_HEADS * IDX_DIM
    hd = [(h, i) for h in range(IDX_HEADS) for i in range(IDX_DIM)]
    pqh = _placement(w, IDX_HEADS * IDX_PACK,
                     [(IDX_DIM * h + i, IDX_PACK * h + i) for h, i in hd]
                     + [(IDX_DIM * h + i, IDX_PACK * h + IDX_DIM + i) for h, i in hd])
    pql = _placement(w, IDX_HEADS * IDX_PACK,
                     [(IDX_DIM * h + i, IDX_PACK * h + 2 * IDX_DIM + i) for h, i in hd])
    const = lambda shape: pl.BlockSpec(shape, lambda b, i: (0,) * len(shape))
    return pl.pallas_call(
        _dsa_q_prep_kernel,
        out_shape=(jax.ShapeDtypeStruct((b_, GROUP_HEADS, l_, HEAD_DIM), BF16),
                   jax.ShapeDtypeStruct((b_, l_, IDX_HEADS * IDX_PACK), BF16)),
        grid=(b_, nt),
        in_specs=[
            pl.BlockSpec((tq, GROUP_WIDTH), lambda b, i: (b * nt + i, C_DSA_Q // GROUP_WIDTH)),
            pl.BlockSpec((tq, w), lambda b, i: (b * nt + i, C_DSA_IQ // w)),
            pl.BlockSpec((3, tq, GROUP_WIDTH), lambda b, i: (0, i, 0)),
            pl.BlockSpec((3, tq, w), lambda b, i: (0, i, 0)),
            const((1, HEAD_DIM)), const(pqh.shape), const(pql.shape),
        ],
        out_specs=(pl.BlockSpec((1, GROUP_HEADS, tq, HEAD_DIM), lambda b, i: (b, 0, i, 0)),
                   pl.BlockSpec((1, tq, IDX_HEADS * IDX_PACK), lambda b, i: (b, i, 0))),
        compiler_params=pltpu.CompilerParams(
            dimension_semantics=("arbitrary", "arbitrary"),
            vmem_limit_bytes=V7X_VMEM_LIMIT_BYTES),
        name="dsa_q_prep",
    )(cols, cols, rope_q, rope_iq, q_gain.reshape(1, HEAD_DIM), pqh, pql)


SUBLANES = 8
FOLD_CHAINS = 4


def _fold_rows(t, op, group=SUBLANES):
    parts = [t[i * group:(i + 1) * group] for i in range(t.shape[0] // group)]
    lanes = parts[:FOLD_CHAINS]
    for i, p in enumerate(parts[FOLD_CHAINS:]):
        lanes[i % FOLD_CHAINS] = op(lanes[i % FOLD_CHAINS], p)
    while len(lanes) > 1:
        lanes = [op(lanes[i], lanes[i + 1]) if i + 1 < len(lanes) else lanes[i]
                 for i in range(0, len(lanes), 2)]
    return lanes[0]


def _dsa_attend_t_kernel(qh_ref, iq3_ref, iw_ref, k_ref, vt_ref, ik3_ref, o_ref, sc_ref, *,
                         topk, idx_bits):
    tq = iq3_ref.shape[1]
    ts = DSA_KEY_TILE
    nh = GROUP_HEADS
    t0 = pl.program_id(1) * tq
    n_kt = (t0 + tq + ts - 1) // ts
    qpos = t0 + lax.broadcasted_iota(jnp.int32, (ts, tq), 1)
    krow = lax.broadcasted_iota(jnp.int32, (ts, tq), 0)
    sub_row = lax.broadcasted_iota(jnp.int32, (SUBLANES, tq), 0)

    iq3 = iq3_ref[0]
    iq_rows = jnp.concatenate([iq3[:, h * IDX_PACK:(h + 1) * IDX_PACK]
                               for h in range(IDX_HEADS)], axis=0)
    iw_t = (iw_ref[...] * (IDX_HEADS ** -0.5 * IDX_DIM ** -0.5)).T

    def score_tile(kt, _):
        k0 = pl.multiple_of(kt * ts, ts)
        rel = jnp.maximum(_dot_nt(ik3_ref[0, pl.ds(k0, ts), :], iq_rows), 0.0)
        sc = rel[:, 0:tq] * iw_t[0:1, :]
        for h in range(1, IDX_HEADS):
            sc = sc + rel[:, h * tq:(h + 1) * tq] * iw_t[h:h + 1, :]
        sc_ref[pl.ds(k0, ts), :] = jnp.where(k0 + krow <= qpos, sc, -jnp.inf)
        return 0

    lax.fori_loop(0, n_kt, score_tile, 0)

    def count(pred):
        def tile(kt, c):
            k0 = pl.multiple_of(kt * ts, ts)
            tile_ref = sc_ref.at[pl.ds(k0, ts)]
            sums = [None] * FOLD_CHAINS
            for i in range(ts // SUBLANES):
                rows = slice(i * SUBLANES, (i + 1) * SUBLANES)
                hit = jnp.where(pred(tile_ref[rows, :], k0 + i * SUBLANES), 1.0, 0.0)
                j = i % FOLD_CHAINS
                sums[j] = hit if sums[j] is None else sums[j] + hit
            return c + _fold_rows(jnp.concatenate(sums, axis=0), jnp.add)
        c = lax.fori_loop(0, n_kt, tile, jnp.zeros((SUBLANES, tq), F32))
        return jnp.sum(c, axis=0, keepdims=True)

    def key_to_float(key):
        return pltpu.bitcast(jnp.where(key >= 0, key, key ^ jnp.int32(0x7FFFFFFF)), F32)

    def value_bit(i, carry):
        thr_key, n_at = carry
        cand = thr_key | jnp.left_shift(jnp.int32(1), 31 - i)
        cand_f = key_to_float(cand ^ jnp.int32(INT_MIN))
        n = count(lambda sc, k0: sc >= cand_f)
        keep = n >= topk
        return jnp.where(keep, cand, thr_key), jnp.where(keep, n, n_at)

    thr_key, n_at = lax.fori_loop(
        0, 32, value_bit, (jnp.zeros((1, tq), jnp.int32), jnp.zeros((1, tq), F32)))
    few = t0 + lax.broadcasted_iota(jnp.int32, (1, tq), 1) + 1 < topk
    thr = jnp.where(few, jnp.finfo(F32).min, key_to_float(thr_key ^ jnp.int32(INT_MIN)))
    tie_break = jnp.max(jnp.where((n_at > topk) & jnp.logical_not(few), 1.0, 0.0)) > 0.0

    def last_tie_position():
        need = topk - count(lambda sc, k0: sc > thr)

        def index_bit(i, last):
            cand = last | jnp.left_shift(jnp.int32(1), idx_bits - 1 - i)
            tied_below = lambda sc, r0: (sc == thr) & (r0 + sub_row < cand)
            return jnp.where(count(tied_below) < need, cand, last)

        return lax.fori_loop(0, idx_bits, index_bit, jnp.zeros((1, tq), jnp.int32))

    qs = qh_ref[0].reshape(nh * tq, HEAD_DIM)

    def attend(selected):
        def key_tile(kt, carry):
            m, l, acc = carry
            k0 = pl.multiple_of(kt * ts, ts)
            bias = jnp.where(selected(sc_ref[pl.ds(k0, ts), :], k0), 0.0, MASKED_SCORE)
            s = _dot_nt(k_ref[0, pl.ds(k0, ts), :], qs) + jnp.concatenate([bias] * nh, axis=1)
            m, alpha, p, l = _softmax_stats_update(s, m, l)
            acc = alpha * acc + _dot(vt_ref[0, :, pl.ds(k0, ts)], p.astype(BF16))
            return m, l, acc

        init = (jnp.full((1, nh * tq), MASKED_SCORE, F32), jnp.zeros((1, nh * tq), F32),
                jnp.zeros((HEAD_DIM, nh * tq), F32))
        _, l, acc = lax.fori_loop(0, n_kt, key_tile, init)
        return l, acc

    def attend_with_ties():
        last = last_tie_position()
        return attend(lambda sc, k0: (sc > thr) | ((sc == thr) & (k0 + krow <= last)))

    l, acc = lax.cond(tie_break, attend_with_ties, lambda: attend(lambda sc, k0: sc >= thr))
    o_t = acc / l
    o_ref[...] = jnp.concatenate([o_t[:, h * tq:(h + 1) * tq] for h in range(nh)], axis=0).T


def dsa_attend(cols, b_, l_, qh, iq3, k, v, ik3):
    tq = DSA_TQ
    nt = l_ // tq
    topk = min(DSA_TOPK_MAX, l_ // 4)
    idx_bits = int(np.log2(l_))
    assert 2 ** idx_bits == l_ and l_ % DSA_KEY_TILE == 0 and topk <= DSA_KEY_TILE
    seq = lambda w: pl.BlockSpec((1, l_, w), lambda b, i: (b, 0, 0))
    return pl.pallas_call(
        functools.partial(_dsa_attend_t_kernel, topk=topk, idx_bits=idx_bits),
        out_shape=jax.ShapeDtypeStruct((b_ * l_, GROUP_WIDTH), F32),
        grid=(b_, nt),
        in_specs=[
            pl.BlockSpec((1, GROUP_HEADS, tq, HEAD_DIM), lambda b, i: (b, 0, i, 0)),
            pl.BlockSpec((1, tq, IDX_HEADS * IDX_PACK), lambda b, i: (b, i, 0)),
            pl.BlockSpec((tq, LANES), lambda b, i: (b * nt + i, C_DSA_IW // LANES)),
            seq(HEAD_DIM), pl.BlockSpec((1, HEAD_DIM, l_), lambda b, i: (b, 0, 0)), seq(IDX_PACK),
        ],
        out_specs=pl.BlockSpec((tq, GROUP_WIDTH), lambda b, i: (b * nt + i, 0)),
        scratch_shapes=[pltpu.VMEM((l_, tq), F32)],
        compiler_params=pltpu.CompilerParams(
            dimension_semantics=("arbitrary", "arbitrary"),
            vmem_limit_bytes=V7X_VMEM_LIMIT_BYTES),
        name="dsa_attend",
    )(qh, iq3, cols, k, v, ik3)


def dsa_mixer_pallas(cols, b_, l_, rope, kv_gain, w_uk, w_uv, q_gain, k_gain, idxk_gain):
    rope_i = rope_tables(l_, IDX_DIM)
    k, v, ik3 = dsa_kv_prep(cols, b_, l_, rope_tables_kv_pair(rope), rope_tables_pad(rope_i, LANES),
                            kv_gain, w_uk, w_uv, k_gain, idxk_gain)
    qh, iq3 = dsa_q_prep(cols, b_, l_, jnp.tile(rope, (1, 1, GROUP_HEADS)),
                         jnp.tile(rope_i, (1, 1, IDX_HEADS)), q_gain)
    return dsa_attend(cols, b_, l_, qh, iq3, k, v, ik3)


_TN = (((0,), (0,)), ((), ()))


def _dot_tn(a, b):
    return lax.dot_general(a, b, _TN, preferred_element_type=F32)


def _split3(t):
    hi = t.astype(BF16)
    r = t - hi.astype(F32)
    mid = r.astype(BF16)
    lo = (r - mid.astype(F32)).astype(BF16)
    return hi, mid, lo


def _tri_cumsum(tri, t):
    hi, mid, lo = _split3(t)
    return _dot(tri, hi) + _dot(tri, mid) + _dot(tri, lo)


def _cumsum_tri_rows(t, tri_u):
    hi, mid, lo = _split3(t)
    return _dot(hi, tri_u) + _dot(mid, tri_u) + _dot(lo, tri_u)


def _head_rms(o, gain):
    return jnp.concatenate([_rms_rows(o[:, h * HEAD_DIM:(h + 1) * HEAD_DIM], gain)
                            for h in range(GROUP_HEADS)], axis=-1)


HG_SUB = 8
HG_BLOCK = 16
HG_CHUNKS_PER_STEP = 8
LOG2_E = 1.4426950408889634


def _hgrn2_kernel(q_ref, f_ref, i_ref, g_ref, lb_ref, gain_ref, tri_ref, ones_ref, bd_ref,
                  hm_ref, jm_ref, o_ref, st_ref, b_ref, kk_ref, v_ref):
    @pl.when(pl.program_id(1) == 0)
    def _():
        st_ref[...] = jnp.zeros_like(st_ref)

    for ci in range(HG_CHUNKS_PER_STEP):
        rows = pl.ds(ci * HG_CHUNK, HG_CHUNK)
        _hgrn2_chunk(q_ref.at[rows], f_ref.at[rows], i_ref.at[rows], g_ref.at[rows], lb_ref,
                     gain_ref, tri_ref, ones_ref, bd_ref, hm_ref, jm_ref, o_ref.at[rows], st_ref,
                     b_ref, kk_ref, v_ref)


def _hgrn2_chunk(q_ref, f_ref, i_ref, g_ref, lb_ref, gain_ref, tri_ref, ones_ref, bd_ref,
                 hm_ref, jm_ref, o_ref, st_ref, b_ref, kk_ref, v_ref):
    c = HG_CHUNK
    w = GROUP_WIDTH
    lb = lb_ref[...]
    q = q_ref[...]
    qs = q * jax.nn.sigmoid(q) * (HEAD_DIM ** -0.5)
    forget = lb + (1.0 - lb) * jax.nn.sigmoid(f_ref[...])
    kk = 1.0 - forget
    bcum = _tri_cumsum(tri_ref[...], jnp.log(forget))
    v = i_ref[...]
    b2 = bcum * LOG2_E
    b_ref[...] = b2
    kk_ref[...] = kk
    v_ref[...] = v

    out = _dot_nt((qs * jnp.exp(bcum)).astype(BF16), st_ref[...].astype(BF16))

    ones_bd = ones_ref[...]
    pieces = []
    for g in range(c // HG_SUB):
        r0 = g * HG_SUB
        nr = (r0 // HG_BLOCK + 1) * HG_BLOCK - r0
        qg = qs[r0:r0 + nr, :]
        bg = b2[r0:r0 + nr, :]
        trow = r0 + lax.broadcasted_iota(jnp.int32, (HG_SUB, w), 0)
        terms = []
        for j in range(HG_SUB):
            s = r0 + j
            d = qg * kk_ref[s:s + 1, :] * jnp.exp2(bg - b_ref[s:s + 1, :])
            if j > 0:
                head = jnp.where(trow >= s, d[:HG_SUB], 0.0)
                d = jnp.concatenate([head, d[HG_SUB:]], axis=0) if nr > HG_SUB else head
            terms.append(d.astype(BF16))
        red = _dot(jnp.concatenate(terms, axis=0), ones_bd)
        acc = red[0:nr] * v_ref[r0:r0 + 1, :]
        for j in range(1, HG_SUB):
            acc = acc + red[j * nr:(j + 1) * nr] * v_ref[r0 + j:r0 + j + 1, :]
        pieces.append((r0, acc))

    nb = c // HG_BLOCK
    hm = hm_ref[...]
    q_rows, k_rows, v_rows = [], [], []
    for j in range(nb - 1):
        blk = slice(j * HG_BLOCK, (j + 1) * HG_BLOCK)
        r_j = b_ref[(j + 1) * HG_BLOCK - 1:(j + 1) * HG_BLOCK, :]
        later = slice((j + 1) * HG_BLOCK, c)
        q_rows.append(qs[later] * jnp.exp2(b2[later] - r_j))
        k_blk = kk[blk] * jnp.exp2(r_j - b2[blk])
        k_rows.append(jnp.concatenate([k_blk] * GROUP_HEADS, axis=0) * hm)
        v_rows.append(jnp.concatenate([v[blk]] * GROUP_HEADS, axis=0) * hm)
    scores = _dot_nt(jnp.concatenate(q_rows, axis=0).astype(BF16),
                     jnp.concatenate(k_rows, axis=0).astype(BF16))
    scores = jnp.where(jm_ref[...] > 0.0, scores, 0.0)
    off = _dot(scores.astype(BF16), jnp.concatenate(v_rows, axis=0).astype(BF16))
    row0 = 0
    for j in range(nb - 1):
        n_later = c - (j + 1) * HG_BLOCK
        pieces.append(((j + 1) * HG_BLOCK, off[row0:row0 + n_later]))
        row0 += n_later

    for r0, piece in pieces:
        parts = [piece]
        if r0 > 0:
            parts.insert(0, jnp.zeros((r0, w), F32))
        if r0 + piece.shape[0] < c:
            parts.append(jnp.zeros((c - r0 - piece.shape[0], w), F32))
        out = out + (jnp.concatenate(parts, axis=0) if len(parts) > 1 else piece)

    b_last = bcum[c - 1:c, :]
    kt = (kk * jnp.exp(b_last - bcum)).astype(BF16)
    st_ref[...] = jnp.exp(b_last) * st_ref[...] + _dot_tn(v.astype(BF16), kt) * bd_ref[...]

    g_in = g_ref[...]
    o_ref[...] = _head_rms(out, gain_ref[...]) * (g_in * jax.nn.sigmoid(g_in))


def hgrn2_mixer_pallas(cols, b_, l_, lb, o_gain):
    c = HG_CHUNK
    rows = c * HG_CHUNKS_PER_STEP
    nt = l_ // rows
    w = GROUP_WIDTH
    head = np.arange(w) // HEAD_DIM
    same = (head[:, None] == head[None, :]).astype(np.float32)
    tri = np.tril(np.ones((c, c), np.float32))
    nb = c // HG_BLOCK
    row_head = np.arange(GROUP_HEADS * HG_BLOCK) // HG_BLOCK
    hm = (row_head[:, None] == head[None, :]).astype(np.float32)
    q_block = np.concatenate([np.full(c - (j + 1) * HG_BLOCK, j) for j in range(nb - 1)])
    k_block = np.arange((nb - 1) * GROUP_HEADS * HG_BLOCK) // (GROUP_HEADS * HG_BLOCK)
    jm = (q_block[:, None] == k_block[None, :]).astype(np.float32)
    col = lambda j: pl.BlockSpec((rows, w), lambda b, i: (b * nt + i, C_HG // w + j))
    const = lambda shape: pl.BlockSpec(shape, lambda b, i: (0,) * len(shape))
    return pl.pallas_call(
        _hgrn2_kernel,
        out_shape=jax.ShapeDtypeStruct((b_ * l_, w), F32),
        grid=(b_, nt),
        in_specs=[col(0), col(1), col(2), col(3), const((1, w)), const((1, HEAD_DIM)),
                  const((c, c)), const((w, w)), const((w, w)), const(hm.shape), const(jm.shape)],
        out_specs=pl.BlockSpec((rows, w), lambda b, i: (b * nt + i, 0)),
        scratch_shapes=[pltpu.VMEM((w, w), F32), pltpu.VMEM((c, w), F32),
                        pltpu.VMEM((c, w), F32), pltpu.VMEM((c, w), F32)],
        compiler_params=pltpu.CompilerParams(
            dimension_semantics=("arbitrary", "arbitrary"),
            vmem_limit_bytes=V7X_VMEM_LIMIT_BYTES),
        name="hgrn2",
    )(cols, cols, cols, cols, lb.reshape(1, w), o_gain.reshape(1, HEAD_DIM),
      jnp.asarray(tri, BF16), jnp.asarray(same, BF16), jnp.asarray(same, F32),
      jnp.asarray(hm), jnp.asarray(jm))


ML_TC = 256
ML_M_INIT = -1e30
ML_CHUNKS_PER_STEP = 2


def _mlstm_kernel(gate_ref, qk_ref, v_ref, og_ref, cw_ref, cb_ref, gb_ref, gain_ref, tril_ref,
                  triu_ref, o_ref, xprev_ref, cmat_ref, nvec_ref, m_ref):
    @pl.when(pl.program_id(1) == 0)
    def _():
        xprev_ref[...] = jnp.zeros_like(xprev_ref)
        cmat_ref[...] = jnp.zeros_like(cmat_ref)
        nvec_ref[...] = jnp.zeros_like(nvec_ref)
        m_ref[...] = jnp.full(m_ref.shape, ML_M_INIT, F32)

    for ci in range(ML_CHUNKS_PER_STEP):
        rows = pl.ds(ci * ML_TC, ML_TC)
        _mlstm_chunk(gate_ref.at[rows], qk_ref.at[rows], v_ref.at[rows], og_ref.at[rows], cw_ref,
                     cb_ref, gb_ref, gain_ref, tril_ref, triu_ref, o_ref.at[rows], xprev_ref,
                     cmat_ref, nvec_ref, m_ref)


def _mlstm_chunk(gate_ref, qk_ref, v_ref, og_ref, cw_ref, cb_ref, gb_ref, gain_ref, tril_ref,
                 triu_ref, o_ref, xprev_ref, cmat_ref, nvec_ref, m_ref):
    c = ML_TC
    nh = GROUP_HEADS
    w = GROUP_WIDTH

    x = qk_ref[...]
    prev = xprev_ref[...]
    row = lax.broadcasted_iota(jnp.int32, x.shape, 0)
    acc = x * cw_ref[CONV_WIDTH - 1:CONV_WIDTH, :] + cb_ref[...]
    for j in range(1, CONV_WIDTH):
        shifted = jnp.where(row < j, pltpu.roll(prev, j, 0), pltpu.roll(x, j, 0))
        acc = acc + shifted * cw_ref[CONV_WIDTH - 1 - j:CONV_WIDTH - j, :]
    xprev_ref[...] = x
    qk = acc * jax.nn.sigmoid(acc)
    q = qk[:, :w]
    k = qk[:, w:] * (HEAD_DIM ** -0.5)
    v = v_ref[...]

    pre = gate_ref[...] + gb_ref[...]
    lane = lax.broadcasted_iota(jnp.int32, pre.shape, 1)
    log_f = jnp.minimum(pre, 0.0) - jnp.log1p(jnp.exp(-jnp.abs(pre)))
    log_f = jnp.where((lane >= nh) & (lane < 2 * nh), log_f, 0.0)
    bcum_c = _tri_cumsum(tril_ref[...], log_f)
    bcum_r = _cumsum_tri_rows(log_f.T, triu_ref[...])
    pre_r = pre.T
    tri = (lax.broadcasted_iota(jnp.int32, (c, c), 0) >= lax.broadcasted_iota(jnp.int32, (c, c), 1))

    heads = [slice(h * HEAD_DIM, (h + 1) * HEAD_DIM) for h in range(nh)]
    qb = [q[:, sl].astype(BF16) for sl in heads]
    vb = [v[:, sl].astype(BF16) for sl in heads]
    qk_raw = [_dot_nt(qb[h], k[:, heads[h]].astype(BF16)) for h in range(nh)]
    q_state = [_dot(qb[h], cmat_ref[h].astype(BF16)) for h in range(nh)]

    s_all, m_all, w_all, upd = [], [], [], []
    for h in range(nh):
        bc = bcum_c[:, nh + h:nh + h + 1]
        li_c = pre[:, h:h + 1]
        a_r = pre_r[h:h + 1, :] - bcum_r[nh + h:nh + h + 1, :]
        m_prev = m_ref[h:h + 1, 0:1]
        log_d = jnp.where(tri, bc + a_r, -jnp.inf)
        inter = bc + m_prev
        m_t = jnp.maximum(inter, jnp.max(log_d, axis=-1, keepdims=True))
        s_all.append(qk_raw[h] * jnp.exp(log_d - m_t))
        m_all.append(m_t)
        w_all.append(jnp.exp(inter - m_t))
        b_last = bc[c - 1:c, :]
        log_w = b_last + (li_c - bc)
        m_new = jnp.maximum(b_last + m_prev, jnp.max(log_w, axis=0, keepdims=True))
        kw = k[:, heads[h]] * jnp.exp(log_w - m_new)
        upd.append((m_new, kw, jnp.exp(b_last + m_prev - m_new)))

    sv = [_dot(s_all[h].astype(BF16), vb[h]) for h in range(nh)]
    kv = [_dot_tn(upd[h][1].astype(BF16), vb[h]) for h in range(nh)]

    outs = []
    for h in range(nh):
        num = w_all[h] * q_state[h] + sv[h]
        den = (w_all[h] * jnp.sum(q[:, heads[h]] * nvec_ref[h], axis=-1, keepdims=True)
               + jnp.sum(s_all[h], axis=-1, keepdims=True))
        outs.append(num / jnp.maximum(jnp.abs(den), jnp.exp(-m_all[h])))
        m_new, kw, decay = upd[h]
        cmat_ref[h] = decay * cmat_ref[h] + kv[h]
        nvec_ref[h] = decay * nvec_ref[h] + jnp.sum(kw, axis=0, keepdims=True)
        m_ref[h:h + 1, :] = jnp.broadcast_to(m_new, (1, LANES))

    hh = _head_rms(jnp.concatenate(outs, axis=-1), gain_ref[...])
    o_ref[...] = hh * jax.nn.sigmoid(og_ref[...])


def mlstm_mixer_pallas(cols, b_, l_, conv_w, conv_b, i_bias, f_bias, o_gain):
    c = ML_TC
    rows = c * ML_CHUNKS_PER_STEP
    nt = l_ // rows
    w = GROUP_WIDTH
    gb = jnp.pad(jnp.concatenate([i_bias, f_bias]), (0, LANES - 2 * GROUP_HEADS)).reshape(1, LANES)
    tril = np.tril(np.ones((c, c), np.float32))
    const = lambda shape: pl.BlockSpec(shape, lambda b, i: (0,) * len(shape))
    blk = lambda width, off: pl.BlockSpec((rows, width), lambda b, i: (b * nt + i, off // width))
    return pl.pallas_call(
        _mlstm_kernel,
        out_shape=jax.ShapeDtypeStruct((b_ * l_, w), F32),
        grid=(b_, nt),
        in_specs=[blk(LANES, C_ML_GATE), blk(2 * w, C_ML_QK), blk(w, C_ML_V), blk(w, C_ML_OG),
                  const((CONV_WIDTH, 2 * w)), const((1, 2 * w)), const((1, LANES)),
                  const((1, HEAD_DIM)), const((c, c)), const((c, c))],
        out_specs=pl.BlockSpec((rows, w), lambda b, i: (b * nt + i, 0)),
        scratch_shapes=[pltpu.VMEM((c, 2 * w), F32),
                        pltpu.VMEM((GROUP_HEADS, HEAD_DIM, HEAD_DIM), F32),
                        pltpu.VMEM((GROUP_HEADS, 1, HEAD_DIM), F32),
                        pltpu.VMEM((8, LANES), F32)],
        compiler_params=pltpu.CompilerParams(
            dimension_semantics=("arbitrary", "arbitrary"),
            vmem_limit_bytes=V7X_VMEM_LIMIT_BYTES),
        name="mlstm",
    )(cols, cols, cols, cols, conv_w, conv_b.reshape(1, 2 * w), gb, o_gain.reshape(1, HEAD_DIM),
      jnp.asarray(tril, BF16), jnp.asarray(tril.T, BF16))


def kernel(x, mem, lb_param, norm_mix, w_in, w_out, hg_o_gain, dsa_kv_gain, dsa_w_uk, dsa_w_uv,
           dsa_q_gain, dsa_k_gain, dsa_idxk_gain, nsa_pos_k, nsa_pos_v, nsa_k_w1, nsa_k_w2,
           nsa_v_w1, nsa_v_w2, nsa_q_gain, nsa_k_gains, ml_conv_w, ml_conv_b, ml_i_bias,
           ml_f_bias, ml_o_gain, norm_xa, norm_mem, xa_wq, xa_wkv, xa_wo, xa_q_gain, xa_k_gain,
           norm_ffn, ffn_w13, ffn_w2):
    b_, l_, d = x.shape
    lb_all = jnp.cumsum(jax.nn.softmax(lb_param.astype(F32), axis=0), axis=0)
    lb_all = lb_all - lb_all[:1]
    x2d = x.reshape(b_ * l_, d)
    mem2d = mem.reshape(b_ * N_MEM, d)
    rope = rope_tables(l_)
    for l in range(DEPTH):
        cols = norm_matmul(x2d, norm_mix[l], pack_w_in_bf16(w_in[l]), tm=256)
        mixers = (
            hgrn2_mixer_pallas(cols, b_, l_, lb_all[l], hg_o_gain[l]),
            dsa_mixer_pallas(cols, b_, l_, rope, dsa_kv_gain[l], dsa_w_uk[l], dsa_w_uv[l],
                             dsa_q_gain[l], dsa_k_gain[l], dsa_idxk_gain[l]),
            nsa_mixer_pallas(cols, b_, l_, rope, nsa_pos_k[l], nsa_pos_v[l], nsa_k_w1[l],
                             nsa_k_w2[l], nsa_v_w1[l], nsa_v_w2[l], nsa_q_gain[l],
                             nsa_k_gains[l]),
            mlstm_mixer_pallas(cols, b_, l_, ml_conv_w[l], ml_conv_b[l], ml_i_bias[l],
                               ml_f_bias[l], ml_o_gain[l]),
        )
        k_mem, v_mem = mem_kv(mem2d, norm_mem[l], xa_wkv[l].astype(BF16), xa_k_gain[l])
        x2d = post_block(x2d, mixers, l_, w_out[l].astype(BF16), norm_xa[l],
                         xa_wq[l].astype(BF16), xa_q_gain[l], k_mem, v_mem,
                         xa_wo[l].astype(BF16), norm_ffn[l], ffn_w13[l].astype(BF16),
                         ffn_w2[l].astype(BF16), tm=512)
    return x2d.reshape(b_, l_, d)
```

```python
import functools

import jax
import jax.numpy as jnp
from jax import lax
import numpy as np
from jax.experimental import pallas as pl
from jax.experimental.pallas import tpu as pltpu

F32 = jnp.float32
BF16 = jnp.bfloat16

D_MODEL = 1024
DEPTH = 2
HEAD_DIM = 64
GROUP_HEADS = 4
GROUP_WIDTH = GROUP_HEADS * HEAD_DIM
ROPE_THETA = 500000.0
EPS = 1e-6
N_MEM = 256
XA_HEADS = 4
XA_WIDTH = XA_HEADS * HEAD_DIM
HG_CHUNK = 64
DSA_LATENT = 128
IDX_HEADS = 8
IDX_DIM = 32
DSA_TOPK_MAX = 256
CMP_BLOCK = 32
CMP_STRIDE = 16
SLC_BLOCK = 64
SLC_SHIFT = 6
SLC_TOPN = 16
WINDOW = 512
CONV_WIDTH = 4
D_FF = 2816

HG_SPLITS = (GROUP_WIDTH,) * 4
DSA_SPLITS = (GROUP_WIDTH, DSA_LATENT, IDX_HEADS * IDX_DIM, IDX_DIM, IDX_HEADS)
NSA_SPLITS = (GROUP_WIDTH,) + (HEAD_DIM,) * 6 + (3 * GROUP_HEADS,)
ML_SPLITS = (2 * GROUP_WIDTH, GROUP_WIDTH, GROUP_WIDTH, GROUP_HEADS, GROUP_HEADS)
GROUP_COLS = (sum(HG_SPLITS), sum(DSA_SPLITS), sum(NSA_SPLITS), sum(ML_SPLITS))
IN_COLS = sum(GROUP_COLS)

V7X_VMEM_LIMIT_BYTES = 56 * 1024 * 1024
LANES = 128
FF_CHUNK = 256


def _round_up(n, m):
    return -(-n // m) * m


def _rms_rows(t, g):
    return t * lax.rsqrt(jnp.mean(t * t, axis=-1, keepdims=True) + EPS) * g


def _const_spec(shape):
    return pl.BlockSpec(shape, lambda *_: (0,) * len(shape), pipeline_mode=pl.Buffered(1))


def _norm_matmul_kernel(x_ref, g_ref, w_ref, o_ref):
    h = _rms_rows(x_ref[...], g_ref[...]).astype(BF16)
    o_ref[...] = jnp.dot(h, w_ref[...], preferred_element_type=F32)


def norm_matmul(x2d, gain, w_bf16, tm):
    m, k = x2d.shape
    n = w_bf16.shape[1]
    return pl.pallas_call(
        _norm_matmul_kernel,
        out_shape=jax.ShapeDtypeStruct((m, n), F32),
        grid=(m // tm,),
        in_specs=[
            pl.BlockSpec((tm, k), lambda i: (i, 0)),
            _const_spec((1, k)),
            _const_spec((k, n)),
        ],
        out_specs=pl.BlockSpec((tm, n), lambda i: (i, 0)),
        compiler_params=pltpu.CompilerParams(
            dimension_semantics=("arbitrary",), vmem_limit_bytes=V7X_VMEM_LIMIT_BYTES),
        name="norm_matmul",
    )(x2d, gain.reshape(1, k), w_bf16)


def _mem_kv_kernel(m_ref, g_ref, w_ref, kg_ref, k_ref, v_ref):
    mn = _rms_rows(m_ref[...], g_ref[...]).astype(BF16)
    kv = jnp.dot(mn, w_ref[...], preferred_element_type=F32)
    kg = kg_ref[...]
    ks = []
    for h in range(XA_HEADS):
        kh = kv[:, h * HEAD_DIM:(h + 1) * HEAD_DIM]
        ks.append(_rms_rows(kh, kg))
    k_ref[...] = jnp.concatenate(ks, axis=-1).astype(BF16)
    v_ref[...] = kv[:, XA_WIDTH:].astype(BF16)


def mem_kv(mem2d, gain, wkv_bf16, k_gain):
    m, k = mem2d.shape
    return pl.pallas_call(
        _mem_kv_kernel,
        out_shape=(jax.ShapeDtypeStruct((m, XA_WIDTH), BF16),
                   jax.ShapeDtypeStruct((m, XA_WIDTH), BF16)),
        grid=(m // N_MEM,),
        in_specs=[
            pl.BlockSpec((N_MEM, k), lambda i: (i, 0)),
            _const_spec((1, k)),
            _const_spec((k, 2 * XA_WIDTH)),
            _const_spec((1, HEAD_DIM)),
        ],
        out_specs=(pl.BlockSpec((N_MEM, XA_WIDTH), lambda i: (i, 0)),
                   pl.BlockSpec((N_MEM, XA_WIDTH), lambda i: (i, 0))),
        compiler_params=pltpu.CompilerParams(
            dimension_semantics=("arbitrary",), vmem_limit_bytes=V7X_VMEM_LIMIT_BYTES),
        name="mem_kv",
    )(mem2d, gain.reshape(1, k), wkv_bf16, k_gain.reshape(1, HEAD_DIM))


def _post_kernel(x_ref, mhg_ref, mdsa_ref, mnsa_ref, mml_ref, wout_ref, gxa_ref, wq_ref, qg_ref,
                 k_ref, v_ref, wo_ref, gffn_ref, w13_ref, w2_ref, o_ref):
    x = x_ref[...]
    for g, mix_ref in enumerate((mhg_ref, mdsa_ref, mnsa_ref, mml_ref)):
        x = x + _dot(mix_ref[...].astype(BF16),
                     wout_ref[g * GROUP_WIDTH:(g + 1) * GROUP_WIDTH, :])
    h = _rms_rows(x, gxa_ref[...]).astype(BF16)
    q = _dot(h, wq_ref[...])
    qg = qg_ref[...] * (HEAD_DIM ** -0.5)
    k = k_ref[...]
    v = v_ref[...]
    heads = [slice(hd * HEAD_DIM, (hd + 1) * HEAD_DIM) for hd in range(XA_HEADS)]
    scores = [_dot_nt(_rms_rows(q[:, sl], qg).astype(BF16), k[:, sl]) for sl in heads]
    probs = []
    for s in scores:
        e = jnp.exp(s - jnp.max(s, axis=-1, keepdims=True))
        probs.append((e / jnp.sum(e, axis=-1, keepdims=True)).astype(BF16))
    o = jnp.concatenate([_dot(p, v[:, sl]) for p, sl in zip(probs, heads)], axis=-1)
    x = x + _dot(o.astype(BF16), wo_ref[...])
    h = _rms_rows(x, gffn_ref[...]).astype(BF16)

    def up(c):
        return (_dot(h, w13_ref[:, c * FF_CHUNK:(c + 1) * FF_CHUNK]),
                _dot(h, w13_ref[:, D_FF + c * FF_CHUNK:D_FF + (c + 1) * FF_CHUNK]))

    n_chunks = D_FF // FF_CHUNK
    acc = x
    a, b = up(0)
    for c in range(n_chunks):
        nxt = up(c + 1) if c + 1 < n_chunks else None
        act = (a * jax.nn.sigmoid(a) * b).astype(BF16)
        acc = acc + _dot(act, w2_ref[c * FF_CHUNK:(c + 1) * FF_CHUNK, :])
        if nxt is not None:
            a, b = nxt
    o_ref[...] = acc


def post_block(x2d, mixers, seq, wout, gxa, wq, qg, k_mem, v_mem, wo, gffn, w13, w2, tm):
    m, d = x2d.shape
    tiles_per_batch = seq // tm
    row = lambda i: (i, 0)
    mem_row = lambda i: (i // tiles_per_batch, 0)
    return pl.pallas_call(
        _post_kernel,
        out_shape=jax.ShapeDtypeStruct((m, d), F32),
        grid=(m // tm,),
        in_specs=[
            pl.BlockSpec((tm, d), row),
            *[pl.BlockSpec((tm, GROUP_WIDTH), row) for _ in mixers],
            _const_spec(wout.shape),
            _const_spec((1, d)),
            _const_spec(wq.shape),
            _const_spec((1, HEAD_DIM)),
            pl.BlockSpec((N_MEM, XA_WIDTH), mem_row),
            pl.BlockSpec((N_MEM, XA_WIDTH), mem_row),
            _const_spec(wo.shape),
            _const_spec((1, d)),
            _const_spec(w13.shape),
            _const_spec(w2.shape),
        ],
        out_specs=pl.BlockSpec((tm, d), row),
        compiler_params=pltpu.CompilerParams(
            dimension_semantics=("arbitrary",), vmem_limit_bytes=V7X_VMEM_LIMIT_BYTES),
        name="post_block",
    )(x2d, *mixers, wout, gxa.reshape(1, d), wq, qg.reshape(1, HEAD_DIM), k_mem, v_mem, wo,
      gffn.reshape(1, d), w13, w2)


C_HG = 0
C_DSA_Q = 1024
C_DSA_IQ = 1280
C_NSA_Q = 1536
C_ML_V = 1792
C_ML_QK = 2048
C_ML_OG = 2560
C_DSA_CKV = 2816
C_DSA_IK = 2944
C_DSA_IW = 3072
C_NSA_GATE = 3200
C_NSA_CMP = 3328
C_NSA_SLC = 3456
C_NSA_WIN = 3584
C_ML_GATE = 3712
IN_COLS_PACKED = 3840


def _packed_source_columns():
    dsa0 = GROUP_COLS[0]
    nsa0 = dsa0 + GROUP_COLS[1]
    ml0 = nsa0 + GROUP_COLS[2]
    segs = [(0, 1024, 1024), (dsa0, 256, 256), (dsa0 + 384, 256, 256), (nsa0, 256, 256),
            (ml0 + 512, 256, 256), (ml0, 512, 512), (ml0 + 768, 256, 256), (dsa0 + 256, 128, 128),
            (dsa0 + 640, 32, LANES), (dsa0 + 672, 8, LANES), (nsa0 + 640, 12, LANES),
            (nsa0 + 256, 128, 128), (nsa0 + 384, 128, 128), (nsa0 + 512, 128, 128),
            (ml0 + 1024, 8, LANES)]
    src = np.concatenate([np.concatenate([np.arange(a, a + n), np.full(width - n, -1)])
                          for a, n, width in segs]).astype(np.int32)
    assert src.shape == (IN_COLS_PACKED,)
    return src


PACK_TN = 256


def _pack_w_kernel(w_ref, src_ref, o_ref, wb_ref):
    @pl.when(pl.program_id(0) == 0)
    def _():
        wb_ref[...] = w_ref[...].astype(BF16)

    row = lax.broadcasted_iota(jnp.int32, (wb_ref.shape[1], PACK_TN), 0)
    sel = jnp.where(row == src_ref[...], 1.0, 0.0).astype(BF16)
    o_ref[...] = _dot(wb_ref[...], sel).astype(BF16)


def pack_w_in_bf16(w):
    d, n = w.shape
    n_pad = _round_up(n, LANES)
    w_pad = jnp.pad(w, ((0, 0), (0, n_pad - n)))
    src = jnp.asarray(_packed_source_columns()).reshape(1, IN_COLS_PACKED)
    return pl.pallas_call(
        _pack_w_kernel,
        out_shape=jax.ShapeDtypeStruct((d, IN_COLS_PACKED), BF16),
        grid=(IN_COLS_PACKED // PACK_TN,),
        in_specs=[_const_spec((d, n_pad)), pl.BlockSpec((1, PACK_TN), lambda j: (0, j))],
        out_specs=pl.BlockSpec((d, PACK_TN), lambda j: (0, j)),
        scratch_shapes=[pltpu.VMEM((d, n_pad), BF16)],
        compiler_params=pltpu.CompilerParams(
            dimension_semantics=("arbitrary",), vmem_limit_bytes=V7X_VMEM_LIMIT_BYTES),
        name="pack_w_in",
    )(w_pad, src)


def rope_tables(l_, d=HEAD_DIM):
    rd = d // 4
    half = rd // 2
    inv = ROPE_THETA ** (-jnp.arange(half, dtype=F32) * 2.0 / rd)
    ang = jnp.arange(l_).astype(F32)[:, None] * inv[None, :]
    cos, sin = lax.optimization_barrier((jnp.cos(ang), jnp.sin(ang)))
    zh = jnp.zeros((l_, half), F32)
    rest0 = jnp.zeros((l_, d - rd), F32)
    c = jnp.concatenate([cos, cos, rest0 + 1.0], axis=1)
    s1 = jnp.concatenate([-sin, zh, rest0], axis=1)
    s2 = jnp.concatenate([zh, sin, rest0], axis=1)
    return jnp.stack([c, s1, s2])


def rope_tables_pad(rt, width):
    n = width - rt.shape[-1]
    ident = jnp.stack([jnp.ones(rt.shape[1:2] + (n,), F32), jnp.zeros(rt.shape[1:2] + (n,), F32),
                       jnp.zeros(rt.shape[1:2] + (n,), F32)])
    return jnp.concatenate([rt, ident], axis=-1)


def rope_tables_kv_pair(rt):
    return rope_tables_pad(rt, 2 * rt.shape[-1])


def _apply_rope(t, rope_ref, half=HEAD_DIM // 8):
    w = t.shape[-1]
    return (t * rope_ref[0] + pltpu.roll(t, w - half, 1) * rope_ref[1]
            + pltpu.roll(t, half, 1) * rope_ref[2])


_NT = (((1,), (1,)), ((), ()))


def _dot_nt(a, b):
    return lax.dot_general(a, b, _NT, preferred_element_type=F32)


def _dot(a, b):
    return jnp.dot(a, b, preferred_element_type=F32)


NEG_BIG = -(2.0 ** 30)
NSA_KEY_TILE = 512
NSA_TQ = 256
NSA_SEL_TQ = 256
NSA_PREP_TK = 512


def _nsa_kv_prep_kernel(ps_ref, pw_ref, rope_ref, gs_ref, gw_ref, kaug_ref, vs_ref, kw_ref, vw_ref):
    tk = ps_ref.shape[0]
    lane = lax.broadcasted_iota(jnp.int32, (tk, LANES), 1)
    is_k = lane < HEAD_DIM

    def norm_rope(p, g):
        ms = jnp.sum(jnp.where(is_k, p * p, 0.0), axis=-1, keepdims=True) * (1.0 / HEAD_DIM)
        y = jnp.where(is_k, p * lax.rsqrt(ms + EPS) * g, p)
        return _apply_rope(y, rope_ref)

    ys = norm_rope(ps_ref[...], gs_ref[...])
    yw = norm_rope(pw_ref[...], gw_ref[...])
    row = pl.program_id(1) * tk + lax.broadcasted_iota(jnp.int32, (tk, LANES), 0)
    ind = jnp.where(jnp.right_shift(row, SLC_SHIFT) == (lane - HEAD_DIM), 1.0, 0.0)
    kaug_ref[0] = jnp.where(is_k, ys, ind).astype(BF16)
    vs_ref[0] = ys.T[HEAD_DIM:, :].astype(BF16)
    kw_ref[0] = yw[:, :HEAD_DIM].astype(BF16)
    vw_ref[0] = yw.T[HEAD_DIM:, :].astype(BF16)


def nsa_kv_prep(cols, b_, l_, rope_pair, g_slc, g_win):
    tk = NSA_PREP_TK
    nt = l_ // tk
    ones = jnp.ones((HEAD_DIM,), F32)
    gs = jnp.concatenate([g_slc, ones]).reshape(1, LANES)
    gw = jnp.concatenate([g_win, ones]).reshape(1, LANES)
    kv = lambda w: jax.ShapeDtypeStruct((b_, l_, w), BF16)
    kv_t = jax.ShapeDtypeStruct((b_, HEAD_DIM, l_), BF16)
    out_blk = lambda w: pl.BlockSpec((1, tk, w), lambda b, i: (b, i, 0))
    out_t = pl.BlockSpec((1, HEAD_DIM, tk), lambda b, i: (b, 0, i))
    return pl.pallas_call(
        _nsa_kv_prep_kernel,
        out_shape=(kv(LANES), kv_t, kv(HEAD_DIM), kv_t),
        grid=(b_, nt),
        in_specs=[
            pl.BlockSpec((tk, LANES), lambda b, i: (b * nt + i, C_NSA_SLC // LANES)),
            pl.BlockSpec((tk, LANES), lambda b, i: (b * nt + i, C_NSA_WIN // LANES)),
            pl.BlockSpec((3, tk, LANES), lambda b, i: (0, i, 0)),
            pl.BlockSpec((1, LANES), lambda b, i: (0, 0)),
            pl.BlockSpec((1, LANES), lambda b, i: (0, 0)),
        ],
        out_specs=(out_blk(LANES), out_t, out_blk(HEAD_DIM), out_t),
        compiler_params=pltpu.CompilerParams(
            dimension_semantics=("arbitrary", "arbitrary"),
            vmem_limit_bytes=V7X_VMEM_LIMIT_BYTES),
        name="nsa_kv_prep",
    )(cols, cols, rope_pair, gs, gw)


def _nsa_compress_kernel(r_ref, pea_ref, peb_ref, w1a_ref, w1b_ref, w2k_ref, w2v_ref, kg_ref,
                         kc_ref, vc_ref):
    r = r_ref[0]
    n = r.shape[0]
    a = _dot((r + pea_ref[...]).astype(BF16), w1a_ref[...])
    bm = _dot((r + peb_ref[...]).astype(BF16), w1b_ref[...])
    row = lax.broadcasted_iota(jnp.int32, bm.shape, 0)
    bm_up = jnp.where(row < n - 1, pltpu.roll(bm, n - 1, 0), 0.0)
    h = jnp.maximum(a + bm_up, 0.0).astype(BF16)
    hid = w2k_ref.shape[0]
    ck = _dot(h[:, :hid], w2k_ref[...])
    cv = _dot(h[:, hid:], w2v_ref[...])
    kc_ref[0] = _rms_rows(ck, kg_ref[...]).astype(BF16)
    vc_ref[0] = cv.astype(BF16)


def nsa_compress(cols, b_, l_, pos_k, pos_v, k_w1, k_w2, v_w1, v_w2, k_gain):
    rows = l_ // CMP_STRIDE
    hid = k_w1.shape[1]
    pair = cols[:, C_NSA_CMP:C_NSA_CMP + LANES].reshape(b_, rows, CMP_STRIDE * LANES)

    def interleave_pe(lo):
        pe = jnp.concatenate([pos_k[lo:lo + CMP_STRIDE], pos_v[lo:lo + CMP_STRIDE]], axis=1)
        return pe.reshape(1, CMP_STRIDE * LANES)

    def interleave_w(lo):
        wk = k_w1[lo * HEAD_DIM:(lo + CMP_STRIDE) * HEAD_DIM].reshape(CMP_STRIDE, HEAD_DIM, hid)
        wv = v_w1[lo * HEAD_DIM:(lo + CMP_STRIDE) * HEAD_DIM].reshape(CMP_STRIDE, HEAD_DIM, hid)
        z = jnp.zeros_like(wk)
        top = jnp.concatenate([wk, z], axis=2)
        bot = jnp.concatenate([z, wv], axis=2)
        return jnp.concatenate([top, bot], axis=1).reshape(CMP_STRIDE * LANES, 2 * hid).astype(BF16)

    out = jax.ShapeDtypeStruct((b_, rows, HEAD_DIM), BF16)
    return pl.pallas_call(
        _nsa_compress_kernel,
        out_shape=(out, out),
        grid=(b_,),
        in_specs=[
            pl.BlockSpec((1, rows, CMP_STRIDE * LANES), lambda b: (b, 0, 0)),
            _const_spec((1, CMP_STRIDE * LANES)),
            _const_spec((1, CMP_STRIDE * LANES)),
            _const_spec((CMP_STRIDE * LANES, 2 * hid)),
            _const_spec((CMP_STRIDE * LANES, 2 * hid)),
            _const_spec((hid, HEAD_DIM)),
            _const_spec((hid, HEAD_DIM)),
            _const_spec((1, HEAD_DIM)),
        ],
        out_specs=(pl.BlockSpec((1, rows, HEAD_DIM), lambda b: (b, 0, 0)),
                   pl.BlockSpec((1, rows, HEAD_DIM), lambda b: (b, 0, 0))),
        compiler_params=pltpu.CompilerParams(
            dimension_semantics=("arbitrary",), vmem_limit_bytes=V7X_VMEM_LIMIT_BYTES),
        name="nsa_compress",
    )(pair, interleave_pe(0), interleave_pe(CMP_STRIDE), interleave_w(0),
      interleave_w(CMP_STRIDE), k_w2.astype(BF16), v_w2.astype(BF16),
      k_gain.reshape(1, HEAD_DIM))


def _masked_softmax(s, valid, axis):
    m = jnp.max(jnp.where(valid, s, -jnp.inf), axis=axis, keepdims=True)
    m = jnp.where(m == -jnp.inf, 0.0, m)
    e = jnp.where(valid, jnp.exp(s - m), 0.0)
    den = jnp.sum(e, axis=axis, keepdims=True)
    return e / jnp.where(den > 0, den, 1.0)


def _nsa_select_kernel(q_ref, kc_ref, vc_ref, ovt_ref, rope_ref, qg_ref, qaug_ref, ocmp_ref,
                       *, n_sel):
    tq = q_ref.shape[0]
    ncr = kc_ref.shape[1]
    n_slc = ovt_ref.shape[0]
    t0 = pl.program_id(1) * tq
    scale = HEAD_DIM ** -0.5
    q = q_ref[...]
    g = qg_ref[...]
    kc = kc_ref[0]
    vc = vc_ref[0]
    vis_t = (lax.broadcasted_iota(jnp.int32, (ncr, tq), 0) * CMP_STRIDE + (CMP_BLOCK - 1)
             <= t0 + lax.broadcasted_iota(jnp.int32, (ncr, tq), 1))
    qn_heads, o_heads = [], []
    psum_t = jnp.zeros((ncr, tq), F32)
    for h in range(GROUP_HEADS):
        qn = _rms_rows(q[:, h * HEAD_DIM:(h + 1) * HEAD_DIM], g)
        qn_heads.append(qn)
        p_t = _masked_softmax(_dot_nt(kc, (qn * scale).astype(BF16)), vis_t, 0)
        o_heads.append(_dot_tn(vc, p_t.astype(BF16)))
        psum_t = psum_t + p_t
    ocmp_ref[0] = jnp.concatenate(o_heads, axis=0)

    hi = psum_t.astype(BF16)
    lo = (psum_t - hi.astype(F32)).astype(BF16)
    imp = _dot(ovt_ref[...], hi) + _dot(ovt_ref[...], lo)
    blk = lax.broadcasted_iota(jnp.int32, (n_slc, tq), 0)
    cur = jnp.right_shift(t0 + lax.broadcasted_iota(jnp.int32, (n_slc, tq), 1), SLC_SHIFT)
    forced = (blk == 0) | (blk == cur) | (blk == cur - 1)
    imp = jnp.where(forced, jnp.inf, jnp.where(blk > cur, -jnp.inf, imp))
    n_grp = n_slc // SUBLANES
    groups = [imp[g * SUBLANES:(g + 1) * SUBLANES] for g in range(n_grp)]
    cnts = [jnp.zeros((SUBLANES, tq), F32) for _ in range(n_grp)]
    sub = lax.broadcasted_iota(jnp.int32, (SUBLANES, tq), 0)
    for m in range(n_slc):
        row = imp[m:m + 1, :]
        gm, rm = divmod(m, SUBLANES)
        for g in range(n_grp):
            if g < gm:
                beats = row > groups[g]
            elif g > gm:
                beats = row >= groups[g]
            else:
                beats = (row > groups[g]) | ((row == groups[g]) & (sub > rm))
            cnts[g] = cnts[g] + jnp.where(beats, 1.0, 0.0)
    cnt = jnp.concatenate(cnts, axis=0)
    mt = jnp.where(cnt < n_sel, 0.0, NEG_BIG)
    pad = jnp.zeros((HEAD_DIM - n_slc, tq), F32)
    mt = jnp.concatenate([mt, pad, mt, pad], axis=0) if n_slc < HEAD_DIM else jnp.concatenate(
        [mt, mt], axis=0)
    mt = mt.T

    qr = _apply_rope(jnp.concatenate(qn_heads, axis=-1), rope_ref) * scale
    lane = lax.broadcasted_iota(jnp.int32, (tq, LANES), 1)
    for j in range(GROUP_HEADS // 2):
        pair = qr[:, j * LANES:(j + 1) * LANES]
        swapped = pltpu.roll(pair, HEAD_DIM, 1)
        qaug_ref[0, :, (2 * j) * LANES:(2 * j + 1) * LANES] = jnp.where(
            lane < HEAD_DIM, pair, mt).astype(BF16)
        qaug_ref[0, :, (2 * j + 1) * LANES:(2 * j + 2) * LANES] = jnp.where(
            lane < HEAD_DIM, swapped, mt).astype(BF16)


def nsa_select(cols, b_, l_, k_cmp, v_cmp, rope_q, q_gain):
    tq = min(NSA_SEL_TQ, l_)
    nt = l_ // tq
    ncr = l_ // CMP_STRIDE
    n_slc = l_ // SLC_BLOCK
    n_sel = min(SLC_TOPN, n_slc)
    st_c = np.arange(ncr) * CMP_STRIDE
    st_s = np.arange(n_slc) * SLC_BLOCK
    ovt = ((st_c[None, :] < st_s[:, None] + SLC_BLOCK)
           & (st_c[None, :] + CMP_BLOCK > st_s[:, None])).astype(np.float32)
    return pl.pallas_call(
        functools.partial(_nsa_select_kernel, n_sel=n_sel),
        out_shape=(jax.ShapeDtypeStruct((b_, l_, GROUP_HEADS * LANES), BF16),
                   jax.ShapeDtypeStruct((b_, GROUP_WIDTH, l_), F32)),
        grid=(b_, nt),
        in_specs=[
            pl.BlockSpec((tq, GROUP_WIDTH), lambda b, i: (b * nt + i, C_NSA_Q // GROUP_WIDTH)),
            pl.BlockSpec((1, ncr, HEAD_DIM), lambda b, i: (b, 0, 0)),
            pl.BlockSpec((1, ncr, HEAD_DIM), lambda b, i: (b, 0, 0)),
            pl.BlockSpec((n_slc, ncr), lambda b, i: (0, 0)),
            pl.BlockSpec((3, tq, GROUP_WIDTH), lambda b, i: (0, i, 0)),
            pl.BlockSpec((1, HEAD_DIM), lambda b, i: (0, 0)),
        ],
        out_specs=(pl.BlockSpec((1, tq, GROUP_HEADS * LANES), lambda b, i: (b, i, 0)),
                   pl.BlockSpec((1, GROUP_WIDTH, tq), lambda b, i: (b, 0, i))),
        compiler_params=pltpu.CompilerParams(
            dimension_semantics=("arbitrary", "arbitrary"),
            vmem_limit_bytes=V7X_VMEM_LIMIT_BYTES),
        name="nsa_select",
    )(cols, k_cmp, v_cmp, jnp.asarray(ovt, BF16), rope_q, q_gain.reshape(1, HEAD_DIM))


def _softmax_stats_update(s, m, l):
    m_new = jnp.maximum(m, jnp.max(s, axis=0, keepdims=True))
    alpha = jnp.exp(m - m_new)
    p = jnp.exp(s - m_new)
    return m_new, alpha, p, alpha * l + jnp.sum(p, axis=0, keepdims=True)


def _nsa_attend_t_kernel(qaug_ref, kaug_ref, vst_ref, kw_ref, vwt_ref, ocmp_ref, gate_ref, o_ref):
    tq = qaug_ref.shape[1]
    nh = GROUP_HEADS
    ts = NSA_KEY_TILE
    t0 = pl.program_id(1) * tq
    qa = qaug_ref[0]
    qs = jnp.concatenate([qa[:, h * LANES:(h + 1) * LANES] for h in range(nh)], axis=0)
    n = nh * tq
    qpos_tile = t0 + lax.broadcasted_iota(jnp.int32, (1, tq), 1)
    qpos = jnp.concatenate([qpos_tile] * nh, axis=1)

    def key_tile(kt, carry):
        m, l, acc = carry
        k0 = pl.multiple_of(kt * ts, ts)
        s = _dot_nt(kaug_ref[0, pl.ds(k0, ts), :], qs)
        kpos = k0 + lax.broadcasted_iota(jnp.int32, (ts, n), 0)
        s = jnp.where(kpos <= qpos, s, NEG_BIG)
        m, alpha, p, l = _softmax_stats_update(s, m, l)
        acc = alpha * acc + _dot(vst_ref[0, :, pl.ds(k0, ts)], p.astype(BF16))
        return m, l, acc

    n_kt = (t0 + tq + ts - 1) // ts
    init = (jnp.full((1, n), -jnp.inf, F32), jnp.zeros((1, n), F32),
            jnp.zeros((HEAD_DIM, n), F32))
    _, l, acc = lax.fori_loop(0, n_kt, key_tile, init)
    o_slc = acc / l

    wlen = WINDOW + tq
    start = pl.multiple_of(jnp.maximum(t0 - WINDOW, 0), tq)
    sw = _dot_nt(kw_ref[0, pl.ds(start, wlen), :], qs[:, :HEAD_DIM])
    dist = qpos - (start + lax.broadcasted_iota(jnp.int32, (wlen, n), 0))
    sw = jnp.where((dist >= 0) & (dist < WINDOW), sw, -jnp.inf)
    e = jnp.exp(sw - jnp.max(sw, axis=0, keepdims=True))
    o_swa = _dot(vwt_ref[0, :, pl.ds(start, wlen)], e.astype(BF16)) / jnp.sum(
        e, axis=0, keepdims=True)

    g = jax.nn.sigmoid(gate_ref[...]).T
    oc = ocmp_ref[0]
    outs = []
    for h in range(nh):
        cols_h = slice(h * tq, (h + 1) * tq)
        outs.append(g[h:h + 1, :] * oc[h * HEAD_DIM:(h + 1) * HEAD_DIM, :]
                    + g[nh + h:nh + h + 1, :] * o_slc[:, cols_h]
                    + g[2 * nh + h:2 * nh + h + 1, :] * o_swa[:, cols_h])
    o_ref[...] = jnp.concatenate(outs, axis=0).T


def nsa_attend(cols, b_, l_, q_aug, k_aug, v_slc, k_win, v_win, o_cmp):
    tq = NSA_TQ
    nt = l_ // tq
    seq = lambda w: pl.BlockSpec((1, l_, w), lambda b, i: (b, 0, 0))
    seq_t = pl.BlockSpec((1, HEAD_DIM, l_), lambda b, i: (b, 0, 0))
    return pl.pallas_call(
        _nsa_attend_t_kernel,
        out_shape=jax.ShapeDtypeStruct((b_ * l_, GROUP_WIDTH), F32),
        grid=(b_, nt),
        in_specs=[
            pl.BlockSpec((1, tq, GROUP_HEADS * LANES), lambda b, i: (b, i, 0)),
            seq(LANES), seq_t, seq(HEAD_DIM), seq_t,
            pl.BlockSpec((1, GROUP_WIDTH, tq), lambda b, i: (b, 0, i)),
            pl.BlockSpec((tq, LANES), lambda b, i: (b * nt + i, C_NSA_GATE // LANES)),
        ],
        out_specs=pl.BlockSpec((tq, GROUP_WIDTH), lambda b, i: (b * nt + i, 0)),
        compiler_params=pltpu.CompilerParams(
            dimension_semantics=("arbitrary", "arbitrary"),
            vmem_limit_bytes=V7X_VMEM_LIMIT_BYTES),
        name="nsa_attend",
    )(q_aug, k_aug, v_slc, k_win, v_win, o_cmp, cols)


def nsa_mixer_pallas(cols, b_, l_, rope, pos_k, pos_v, k_w1, k_w2, v_w1, v_w2, q_gain, k_gains):
    k_aug, v_slc, k_win, v_win = nsa_kv_prep(cols, b_, l_, rope_tables_kv_pair(rope),
                                             k_gains[1], k_gains[2])
    k_cmp, v_cmp = nsa_compress(cols, b_, l_, pos_k, pos_v, k_w1, k_w2, v_w1, v_w2, k_gains[0])
    q_aug, o_cmp = nsa_select(cols, b_, l_, k_cmp, v_cmp, jnp.tile(rope, (1, 1, GROUP_HEADS)),
                              q_gain)
    return nsa_attend(cols, b_, l_, q_aug, k_aug, v_slc, k_win, v_win, o_cmp)


DSA_TQ = 512
DSA_KEY_TILE = 512
DSA_PREP_TK = 512
IDX_PACK = LANES
INT_MIN = -2 ** 31
MASKED_SCORE = -1e30


def _split_hi_lo(t):
    hi = t.astype(BF16)
    lo = (t - hi.astype(F32)).astype(BF16)
    return hi, lo


def _placement(rows, cols, pairs):
    p = np.zeros((rows, cols), np.float32)
    for r, c in pairs:
        p[r, c] = 1.0
    return jnp.asarray(p, BF16)


def _dsa_kv_prep_kernel(ckv_ref, ik_ref, rope_ref, ropei_ref, kvg_ref, wkv_ref, kg_ref, ikg_ref,
                        pkh_ref, pkl_ref, k_ref, vt_ref, ik3_ref):
    tk = ckv_ref.shape[0]
    lane = lax.broadcasted_iota(jnp.int32, (tk, LANES), 1)
    ckv = _rms_rows(ckv_ref[...], kvg_ref[...]).astype(BF16)
    kv = _dot(ckv, wkv_ref[...])
    is_k = lane < HEAD_DIM
    ms = jnp.sum(jnp.where(is_k, kv * kv, 0.0), axis=-1, keepdims=True) * (1.0 / HEAD_DIM)
    y = _apply_rope(jnp.where(is_k, kv * lax.rsqrt(ms + EPS) * kg_ref[...], kv), rope_ref)
    k_ref[0] = y[:, :HEAD_DIM].astype(BF16)
    vt_ref[0] = y.T[HEAD_DIM:, :].astype(BF16)
    ik = ik_ref[...]
    ms = jnp.sum(ik * ik, axis=-1, keepdims=True) * (1.0 / IDX_DIM)
    ikn = _apply_rope(ik * lax.rsqrt(ms + EPS) * ikg_ref[...], ropei_ref, IDX_DIM // 8)
    hi, lo = _split_hi_lo(ikn)
    ik3_ref[0] = (_dot(hi, pkh_ref[...]) + _dot(lo, pkl_ref[...])).astype(BF16)


def dsa_kv_prep(cols, b_, l_, rope_pair, rope_idx, kv_gain, w_uk, w_uv, k_gain, idxk_gain):
    tk = DSA_PREP_TK
    nt = l_ // tk
    ones = jnp.ones((HEAD_DIM,), F32)
    kg = jnp.concatenate([k_gain, ones]).reshape(1, LANES)
    ikg = jnp.pad(idxk_gain, (0, LANES - IDX_DIM)).reshape(1, LANES)
    wkv = jnp.concatenate([w_uk, w_uv], axis=1).astype(BF16)
    d = range(IDX_DIM)
    pkh = _placement(LANES, IDX_PACK, [(i, i) for i in d] + [(i, 2 * IDX_DIM + i) for i in d])
    pkl = _placement(LANES, IDX_PACK, [(i, IDX_DIM + i) for i in d])
    const = lambda shape: pl.BlockSpec(shape, lambda b, i: (0,) * len(shape))
    out = lambda w: pl.BlockSpec((1, tk, w), lambda b, i: (b, i, 0))
    return pl.pallas_call(
        _dsa_kv_prep_kernel,
        out_shape=(jax.ShapeDtypeStruct((b_, l_, HEAD_DIM), BF16),
                   jax.ShapeDtypeStruct((b_, HEAD_DIM, l_), BF16),
                   jax.ShapeDtypeStruct((b_, l_, IDX_PACK), BF16)),
        grid=(b_, nt),
        in_specs=[
            pl.BlockSpec((tk, LANES), lambda b, i: (b * nt + i, C_DSA_CKV // LANES)),
            pl.BlockSpec((tk, LANES), lambda b, i: (b * nt + i, C_DSA_IK // LANES)),
            pl.BlockSpec((3, tk, LANES), lambda b, i: (0, i, 0)),
            pl.BlockSpec((3, tk, LANES), lambda b, i: (0, i, 0)),
            const((1, LANES)), const((DSA_LATENT, LANES)), const((1, LANES)), const((1, LANES)),
            const((LANES, IDX_PACK)), const((LANES, IDX_PACK)),
        ],
        out_specs=(out(HEAD_DIM), pl.BlockSpec((1, HEAD_DIM, tk), lambda b, i: (b, 0, i)),
                   out(IDX_PACK)),
        compiler_params=pltpu.CompilerParams(
            dimension_semantics=("arbitrary", "arbitrary"),
            vmem_limit_bytes=V7X_VMEM_LIMIT_BYTES),
        name="dsa_kv_prep",
    )(cols, cols, rope_pair, rope_idx, kv_gain.reshape(1, DSA_LATENT), wkv, kg, ikg, pkh, pkl)


def _dsa_q_prep_kernel(q_ref, iq_ref, rope_ref, ropei_ref, qg_ref, pqh_ref, pql_ref,
                       qh_ref, iq3_ref):
    g = qg_ref[...]
    q = q_ref[...]
    qn = jnp.concatenate([_rms_rows(q[:, h * HEAD_DIM:(h + 1) * HEAD_DIM], g)
                          for h in range(GROUP_HEADS)], axis=-1)
    qr = _apply_rope(qn, rope_ref) * (HEAD_DIM ** -0.5)
    for h in range(GROUP_HEADS):
        qh_ref[0, h] = qr[:, h * HEAD_DIM:(h + 1) * HEAD_DIM].astype(BF16)
    hi, lo = _split_hi_lo(_apply_rope(iq_ref[...], ropei_ref, IDX_DIM // 8))
    iq3_ref[0] = (_dot(hi, pqh_ref[...]) + _dot(lo, pql_ref[...])).astype(BF16)


def dsa_q_prep(cols, b_, l_, rope_q, rope_iq, q_gain):
    tq = 256
    nt = l_ // tq
    w = IDX_HEADS * IDX_DIM
    hd = [(h, i) for h in range(IDX_HEADS) for i in range(IDX_DIM)]
    pqh = _placement(w, IDX_HEADS * IDX_PACK,
                     [(IDX_DIM * h + i, IDX_PACK * h + i) for h, i in hd]
                     + [(IDX_DIM * h + i, IDX_PACK * h + IDX_DIM + i) for h, i in hd])
    pql = _placement(w, IDX_HEADS * IDX_PACK,
                     [(IDX_DIM * h + i, IDX_PACK * h + 2 * IDX_DIM + i) for h, i in hd])
    const = lambda shape: pl.BlockSpec(shape, lambda b, i: (0,) * len(shape))
    return pl.pallas_call(
        _dsa_q_prep_kernel,
        out_shape=(jax.ShapeDtypeStruct((b_, GROUP_HEADS, l_, HEAD_DIM), BF16),
                   jax.ShapeDtypeStruct((b_, l_, IDX_HEADS * IDX_PACK), BF16)),
        grid=(b_, nt),
        in_specs=[
            pl.BlockSpec((tq, GROUP_WIDTH), lambda b, i: (b * nt + i, C_DSA_Q // GROUP_WIDTH)),
            pl.BlockSpec((tq, w), lambda b, i: (b * nt + i, C_DSA_IQ // w)),
            pl.BlockSpec((3, tq, GROUP_WIDTH), lambda b, i: (0, i, 0)),
            pl.BlockSpec((3, tq, w), lambda b, i: (0, i, 0)),
            const((1, HEAD_DIM)), const(pqh.shape), const(pql.shape),
        ],
        out_specs=(pl.BlockSpec((1, GROUP_HEADS, tq, HEAD_DIM), lambda b, i: (b, 0, i, 0)),
                   pl.BlockSpec((1, tq, IDX_HEADS * IDX_PACK), lambda b, i: (b, i, 0))),
        compiler_params=pltpu.CompilerParams(
            dimension_semantics=("arbitrary", "arbitrary"),
            vmem_limit_bytes=V7X_VMEM_LIMIT_BYTES),
        name="dsa_q_prep",
    )(cols, cols, rope_q, rope_iq, q_gain.reshape(1, HEAD_DIM), pqh, pql)


SUBLANES = 8
FOLD_CHAINS = 4


def _fold_rows(t, op, group=SUBLANES):
    parts = [t[i * group:(i + 1) * group] for i in range(t.shape[0] // group)]
    lanes = parts[:FOLD_CHAINS]
    for i, p in enumerate(parts[FOLD_CHAINS:]):
        lanes[i % FOLD_CHAINS] = op(lanes[i % FOLD_CHAINS], p)
    while len(lanes) > 1:
        lanes = [op(lanes[i], lanes[i + 1]) if i + 1 < len(lanes) else lanes[i]
                 for i in range(0, len(lanes), 2)]
    return lanes[0]


def _dsa_attend_t_kernel(qh_ref, iq3_ref, iw_ref, k_ref, vt_ref, ik3_ref, o_ref, sc_ref, *,
                         topk, idx_bits):
    tq = iq3_ref.shape[1]
    ts = DSA_KEY_TILE
    nh = GROUP_HEADS
    t0 = pl.program_id(1) * tq
    n_kt = (t0 + tq + ts - 1) // ts
    qpos = t0 + lax.broadcasted_iota(jnp.int32, (ts, tq), 1)
    krow = lax.broadcasted_iota(jnp.int32, (ts, tq), 0)
    sub_row = lax.broadcasted_iota(jnp.int32, (SUBLANES, tq), 0)

    iq3 = iq3_ref[0]
    iq_rows = jnp.concatenate([iq3[:, h * IDX_PACK:(h + 1) * IDX_PACK]
                               for h in range(IDX_HEADS)], axis=0)
    iw_t = (iw_ref[...] * (IDX_HEADS ** -0.5 * IDX_DIM ** -0.5)).T

    def score_tile(kt, _):
        k0 = pl.multiple_of(kt * ts, ts)
        rel = jnp.maximum(_dot_nt(ik3_ref[0, pl.ds(k0, ts), :], iq_rows), 0.0)
        sc = rel[:, 0:tq] * iw_t[0:1, :]
        for h in range(1, IDX_HEADS):
            sc = sc + rel[:, h * tq:(h + 1) * tq] * iw_t[h:h + 1, :]
        sc_ref[pl.ds(k0, ts), :] = jnp.where(k0 + krow <= qpos, sc, -jnp.inf)
        return 0

    lax.fori_loop(0, n_kt, score_tile, 0)

    def count(pred):
        def tile(kt, c):
            k0 = pl.multiple_of(kt * ts, ts)
            tile_ref = sc_ref.at[pl.ds(k0, ts)]
            sums = [None] * FOLD_CHAINS
            for i in range(ts // SUBLANES):
                rows = slice(i * SUBLANES, (i + 1) * SUBLANES)
                hit = jnp.where(pred(tile_ref[rows, :], k0 + i * SUBLANES), 1.0, 0.0)
                j = i % FOLD_CHAINS
                sums[j] = hit if sums[j] is None else sums[j] + hit
            return c + _fold_rows(jnp.concatenate(sums, axis=0), jnp.add)
        c = lax.fori_loop(0, n_kt, tile, jnp.zeros((SUBLANES, tq), F32))
        return jnp.sum(c, axis=0, keepdims=True)

    def key_to_float(key):
        return pltpu.bitcast(jnp.where(key >= 0, key, key ^ jnp.int32(0x7FFFFFFF)), F32)

    def value_bit(i, carry):
        thr_key, n_at = carry
        cand = thr_key | jnp.left_shift(jnp.int32(1), 31 - i)
        cand_f = key_to_float(cand ^ jnp.int32(INT_MIN))
        n = count(lambda sc, k0: sc >= cand_f)
        keep = n >= topk
        return jnp.where(keep, cand, thr_key), jnp.where(keep, n, n_at)

    thr_key, n_at = lax.fori_loop(
        0, 32, value_bit, (jnp.zeros((1, tq), jnp.int32), jnp.zeros((1, tq), F32)))
    few = t0 + lax.broadcasted_iota(jnp.int32, (1, tq), 1) + 1 < topk
    thr = jnp.where(few, jnp.finfo(F32).min, key_to_float(thr_key ^ jnp.int32(INT_MIN)))
    tie_break = jnp.max(jnp.where((n_at > topk) & jnp.logical_not(few), 1.0, 0.0)) > 0.0

    def last_tie_position():
        need = topk - count(lambda sc, k0: sc > thr)

        def index_bit(i, last):
            cand = last | jnp.left_shift(jnp.int32(1), idx_bits - 1 - i)
            tied_below = lambda sc, r0: (sc == thr) & (r0 + sub_row < cand)
            return jnp.where(count(tied_below) < need, cand, last)

        return lax.fori_loop(0, idx_bits, index_bit, jnp.zeros((1, tq), jnp.int32))

    qs = qh_ref[0].reshape(nh * tq, HEAD_DIM)

    def attend(selected):
        def key_tile(kt, carry):
            m, l, acc = carry
            k0 = pl.multiple_of(kt * ts, ts)
            bias = jnp.where(selected(sc_ref[pl.ds(k0, ts), :], k0), 0.0, MASKED_SCORE)
            s = _dot_nt(k_ref[0, pl.ds(k0, ts), :], qs) + jnp.concatenate([bias] * nh, axis=1)
            m, alpha, p, l = _softmax_stats_update(s, m, l)
            acc = alpha * acc + _dot(vt_ref[0, :, pl.ds(k0, ts)], p.astype(BF16))
            return m, l, acc

        init = (jnp.full((1, nh * tq), MASKED_SCORE, F32), jnp.zeros((1, nh * tq), F32),
                jnp.zeros((HEAD_DIM, nh * tq), F32))
        _, l, acc = lax.fori_loop(0, n_kt, key_tile, init)
        return l, acc

    def attend_with_ties():
        last = last_tie_position()
        return attend(lambda sc, k0: (sc > thr) | ((sc == thr) & (k0 + krow <= last)))

    l, acc = lax.cond(tie_break, attend_with_ties, lambda: attend(lambda sc, k0: sc >= thr))
    o_t = acc / l
    o_ref[...] = jnp.concatenate([o_t[:, h * tq:(h + 1) * tq] for h in range(nh)], axis=0).T


def dsa_attend(cols, b_, l_, qh, iq3, k, v, ik3):
    tq = DSA_TQ
    nt = l_ // tq
    topk = min(DSA_TOPK_MAX, l_ // 4)
    idx_bits = int(np.log2(l_))
    assert 2 ** idx_bits == l_ and l_ % DSA_KEY_TILE == 0 and topk <= DSA_KEY_TILE
    seq = lambda w: pl.BlockSpec((1, l_, w), lambda b, i: (b, 0, 0))
    return pl.pallas_call(
        functools.partial(_dsa_attend_t_kernel, topk=topk, idx_bits=idx_bits),
        out_shape=jax.ShapeDtypeStruct((b_ * l_, GROUP_WIDTH), F32),
        grid=(b_, nt),
        in_specs=[
            pl.BlockSpec((1, GROUP_HEADS, tq, HEAD_DIM), lambda b, i: (b, 0, i, 0)),
            pl.BlockSpec((1, tq, IDX_HEADS * IDX_PACK), lambda b, i: (b, i, 0)),
            pl.BlockSpec((tq, LANES), lambda b, i: (b * nt + i, C_DSA_IW // LANES)),
            seq(HEAD_DIM), pl.BlockSpec((1, HEAD_DIM, l_), lambda b, i: (b, 0, 0)), seq(IDX_PACK),
        ],
        out_specs=pl.BlockSpec((tq, GROUP_WIDTH), lambda b, i: (b * nt + i, 0)),
        scratch_shapes=[pltpu.VMEM((l_, tq), F32)],
        compiler_params=pltpu.CompilerParams(
            dimension_semantics=("arbitrary", "arbitrary"),
            vmem_limit_bytes=V7X_VMEM_LIMIT_BYTES),
        name="dsa_attend",
    )(qh, iq3, cols, k, v, ik3)


def dsa_mixer_pallas(cols, b_, l_, rope, kv_gain, w_uk, w_uv, q_gain, k_gain, idxk_gain):
    rope_i = rope_tables(l_, IDX_DIM)
    k, v, ik3 = dsa_kv_prep(cols, b_, l_, rope_tables_kv_pair(rope), rope_tables_pad(rope_i, LANES),
                            kv_gain, w_uk, w_uv, k_gain, idxk_gain)
    qh, iq3 = dsa_q_prep(cols, b_, l_, jnp.tile(rope, (1, 1, GROUP_HEADS)),
                         jnp.tile(rope_i, (1, 1, IDX_HEADS)), q_gain)
    return dsa_attend(cols, b_, l_, qh, iq3, k, v, ik3)


_TN = (((0,), (0,)), ((), ()))


def _dot_tn(a, b):
    return lax.dot_general(a, b, _TN, preferred_element_type=F32)


def _split3(t):
    hi = t.astype(BF16)
    r = t - hi.astype(F32)
    mid = r.astype(BF16)
    lo = (r - mid.astype(F32)).astype(BF16)
    return hi, mid, lo


def _tri_cumsum(tri, t):
    hi, mid, lo = _split3(t)
    return _dot(tri, hi) + _dot(tri, mid) + _dot(tri, lo)


def _cumsum_tri_rows(t, tri_u):
    hi, mid, lo = _split3(t)
    return _dot(hi, tri_u) + _dot(mid, tri_u) + _dot(lo, tri_u)


def _head_rms(o, gain):
    return jnp.concatenate([_rms_rows(o[:, h * HEAD_DIM:(h + 1) * HEAD_DIM], gain)
                            for h in range(GROUP_HEADS)], axis=-1)


HG_SUB = 8
HG_BLOCK = 16
HG_CHUNKS_PER_STEP = 8
LOG2_E = 1.4426950408889634


def _hgrn2_kernel(q_ref, f_ref, i_ref, g_ref, lb_ref, gain_ref, tri_ref, ones_ref, bd_ref,
                  hm_ref, jm_ref, o_ref, st_ref, b_ref, kk_ref, v_ref):
    @pl.when(pl.program_id(1) == 0)
    def _():
        st_ref[...] = jnp.zeros_like(st_ref)

    for ci in range(HG_CHUNKS_PER_STEP):
        rows = pl.ds(ci * HG_CHUNK, HG_CHUNK)
        _hgrn2_chunk(q_ref.at[rows], f_ref.at[rows], i_ref.at[rows], g_ref.at[rows], lb_ref,
                     gain_ref, tri_ref, ones_ref, bd_ref, hm_ref, jm_ref, o_ref.at[rows], st_ref,
                     b_ref, kk_ref, v_ref)


def _hgrn2_chunk(q_ref, f_ref, i_ref, g_ref, lb_ref, gain_ref, tri_ref, ones_ref, bd_ref,
                 hm_ref, jm_ref, o_ref, st_ref, b_ref, kk_ref, v_ref):
    c = HG_CHUNK
    w = GROUP_WIDTH
    lb = lb_ref[...]
    q = q_ref[...]
    qs = q * jax.nn.sigmoid(q) * (HEAD_DIM ** -0.5)
    forget = lb + (1.0 - lb) * jax.nn.sigmoid(f_ref[...])
    kk = 1.0 - forget
    bcum = _tri_cumsum(tri_ref[...], jnp.log(forget))
    v = i_ref[...]
    b2 = bcum * LOG2_E
    b_ref[...] = b2
    kk_ref[...] = kk
    v_ref[...] = v

    out = _dot_nt((qs * jnp.exp(bcum)).astype(BF16), st_ref[...].astype(BF16))

    ones_bd = ones_ref[...]
    pieces = []
    for g in range(c // HG_SUB):
        r0 = g * HG_SUB
        nr = (r0 // HG_BLOCK + 1) * HG_BLOCK - r0
        qg = qs[r0:r0 + nr, :]
        bg = b2[r0:r0 + nr, :]
        trow = r0 + lax.broadcasted_iota(jnp.int32, (HG_SUB, w), 0)
        terms = []
        for j in range(HG_SUB):
            s = r0 + j
            d = qg * kk_ref[s:s + 1, :] * jnp.exp2(bg - b_ref[s:s + 1, :])
            if j > 0:
                head = jnp.where(trow >= s, d[:HG_SUB], 0.0)
                d = jnp.concatenate([head, d[HG_SUB:]], axis=0) if nr > HG_SUB else head
            terms.append(d.astype(BF16))
        red = _dot(jnp.concatenate(terms, axis=0), ones_bd)
        acc = red[0:nr] * v_ref[r0:r0 + 1, :]
        for j in range(1, HG_SUB):
            acc = acc + red[j * nr:(j + 1) * nr] * v_ref[r0 + j:r0 + j + 1, :]
        pieces.append((r0, acc))

    nb = c // HG_BLOCK
    hm = hm_ref[...]
    q_rows, k_rows, v_rows = [], [], []
    for j in range(nb - 1):
        blk = slice(j * HG_BLOCK, (j + 1) * HG_BLOCK)
        r_j = b_ref[(j + 1) * HG_BLOCK - 1:(j + 1) * HG_BLOCK, :]
        later = slice((j + 1) * HG_BLOCK, c)
        q_rows.append(qs[later] * jnp.exp2(b2[later] - r_j))
        k_blk = kk[blk] * jnp.exp2(r_j - b2[blk])
        k_rows.append(jnp.concatenate([k_blk] * GROUP_HEADS, axis=0) * hm)
        v_rows.append(jnp.concatenate([v[blk]] * GROUP_HEADS, axis=0) * hm)
    scores = _dot_nt(jnp.concatenate(q_rows, axis=0).astype(BF16),
                     jnp.concatenate(k_rows, axis=0).astype(BF16))
    scores = jnp.where(jm_ref[...] > 0.0, scores, 0.0)
    off = _dot(scores.astype(BF16), jnp.concatenate(v_rows, axis=0).astype(BF16))
    row0 = 0
    for j in range(nb - 1):
        n_later = c - (j + 1) * HG_BLOCK
        pieces.append(((j + 1) * HG_BLOCK, off[row0:row0 + n_later]))
        row0 += n_later

    for r0, piece in pieces:
        parts = [piece]
        if r0 > 0:
            parts.insert(0, jnp.zeros((r0, w), F32))
        if r0 + piece.shape[0] < c:
            parts.append(jnp.zeros((c - r0 - piece.shape[0], w), F32))
        out = out + (jnp.concatenate(parts, axis=0) if len(parts) > 1 else piece)

    b_last = bcum[c - 1:c, :]
    kt = (kk * jnp.exp(b_last - bcum)).astype(BF16)
    st_ref[...] = jnp.exp(b_last) * st_ref[...] + _dot_tn(v.astype(BF16), kt) * bd_ref[...]

    g_in = g_ref[...]
    o_ref[...] = _head_rms(out, gain_ref[...]) * (g_in * jax.nn.sigmoid(g_in))


def hgrn2_mixer_pallas(cols, b_, l_, lb, o_gain):
    c = HG_CHUNK
    rows = c * HG_CHUNKS_PER_STEP
    nt = l_ // rows
    w = GROUP_WIDTH
    head = np.arange(w) // HEAD_DIM
    same = (head[:, None] == head[None, :]).astype(np.float32)
    tri = np.tril(np.ones((c, c), np.float32))
    nb = c // HG_BLOCK
    row_head = np.arange(GROUP_HEADS * HG_BLOCK) // HG_BLOCK
    hm = (row_head[:, None] == head[None, :]).astype(np.float32)
    q_block = np.concatenate([np.full(c - (j + 1) * HG_BLOCK, j) for j in range(nb - 1)])
    k_block = np.arange((nb - 1) * GROUP_HEADS * HG_BLOCK) // (GROUP_HEADS * HG_BLOCK)
    jm = (q_block[:, None] == k_block[None, :]).astype(np.float32)
    col = lambda j: pl.BlockSpec((rows, w), lambda b, i: (b * nt + i, C_HG // w + j))
    const = lambda shape: pl.BlockSpec(shape, lambda b, i: (0,) * len(shape))
    return pl.pallas_call(
        _hgrn2_kernel,
        out_shape=jax.ShapeDtypeStruct((b_ * l_, w), F32),
        grid=(b_, nt),
        in_specs=[col(0), col(1), col(2), col(3), const((1, w)), const((1, HEAD_DIM)),
                  const((c, c)), const((w, w)), const((w, w)), const(hm.shape), const(jm.shape)],
        out_specs=pl.BlockSpec((rows, w), lambda b, i: (b * nt + i, 0)),
        scratch_shapes=[pltpu.VMEM((w, w), F32), pltpu.VMEM((c, w), F32),
                        pltpu.VMEM((c, w), F32), pltpu.VMEM((c, w), F32)],
        compiler_params=pltpu.CompilerParams(
            dimension_semantics=("arbitrary", "arbitrary"),
            vmem_limit_bytes=V7X_VMEM_LIMIT_BYTES),
        name="hgrn2",
    )(cols, cols, cols, cols, lb.reshape(1, w), o_gain.reshape(1, HEAD_DIM),
      jnp.asarray(tri, BF16), jnp.asarray(same, BF16), jnp.asarray(same, F32),
      jnp.asarray(hm), jnp.asarray(jm))


ML_TC = 256
ML_M_INIT = -1e30
ML_CHUNKS_PER_STEP = 2


def _mlstm_kernel(gate_ref, qk_ref, v_ref, og_ref, cw_ref, cb_ref, gb_ref, gain_ref, tril_ref,
                  triu_ref, o_ref, xprev_ref, cmat_ref, nvec_ref, m_ref):
    @pl.when(pl.program_id(1) == 0)
    def _():
        xprev_ref[...] = jnp.zeros_like(xprev_ref)
        cmat_ref[...] = jnp.zeros_like(cmat_ref)
        nvec_ref[...] = jnp.zeros_like(nvec_ref)
        m_ref[...] = jnp.full(m_ref.shape, ML_M_INIT, F32)

    for ci in range(ML_CHUNKS_PER_STEP):
        rows = pl.ds(ci * ML_TC, ML_TC)
        _mlstm_chunk(gate_ref.at[rows], qk_ref.at[rows], v_ref.at[rows], og_ref.at[rows], cw_ref,
                     cb_ref, gb_ref, gain_ref, tril_ref, triu_ref, o_ref.at[rows], xprev_ref,
                     cmat_ref, nvec_ref, m_ref)


def _mlstm_chunk(gate_ref, qk_ref, v_ref, og_ref, cw_ref, cb_ref, gb_ref, gain_ref, tril_ref,
                 triu_ref, o_ref, xprev_ref, cmat_ref, nvec_ref, m_ref):
    c = ML_TC
    nh = GROUP_HEADS
    w = GROUP_WIDTH

    x = qk_ref[...]
    prev = xprev_ref[...]
    row = lax.broadcasted_iota(jnp.int32, x.shape, 0)
    acc = x * cw_ref[CONV_WIDTH - 1:CONV_WIDTH, :] + cb_ref[...]
    for j in range(1, CONV_WIDTH):
        shifted = jnp.where(row < j, pltpu.roll(prev, j, 0), pltpu.roll(x, j, 0))
        acc = acc + shifted * cw_ref[CONV_WIDTH - 1 - j:CONV_WIDTH - j, :]
    xprev_ref[...] = x
    qk = acc * jax.nn.sigmoid(acc)
    q = qk[:, :w]
    k = qk[:, w:] * (HEAD_DIM ** -0.5)
    v = v_ref[...]

    pre = gate_ref[...] + gb_ref[...]
    lane = lax.broadcasted_iota(jnp.int32, pre.shape, 1)
    log_f = jnp.minimum(pre, 0.0) - jnp.log1p(jnp.exp(-jnp.abs(pre)))
    log_f = jnp.where((lane >= nh) & (lane < 2 * nh), log_f, 0.0)
    bcum_c = _tri_cumsum(tril_ref[...], log_f)
    bcum_r = _cumsum_tri_rows(log_f.T, triu_ref[...])
    pre_r = pre.T
    tri = (lax.broadcasted_iota(jnp.int32, (c, c), 0) >= lax.broadcasted_iota(jnp.int32, (c, c), 1))

    heads = [slice(h * HEAD_DIM, (h + 1) * HEAD_DIM) for h in range(nh)]
    qb = [q[:, sl].astype(BF16) for sl in heads]
    vb = [v[:, sl].astype(BF16) for sl in heads]
    qk_raw = [_dot_nt(qb[h], k[:, heads[h]].astype(BF16)) for h in range(nh)]
    q_state = [_dot(qb[h], cmat_ref[h].astype(BF16)) for h in range(nh)]

    s_all, m_all, w_all, upd = [], [], [], []
    for h in range(nh):
        bc = bcum_c[:, nh + h:nh + h + 1]
        li_c = pre[:, h:h + 1]
        a_r = pre_r[h:h + 1, :] - bcum_r[nh + h:nh + h + 1, :]
        m_prev = m_ref[h:h + 1, 0:1]
        log_d = jnp.where(tri, bc + a_r, -jnp.inf)
        inter = bc + m_prev
        m_t = jnp.maximum(inter, jnp.max(log_d, axis=-1, keepdims=True))
        s_all.append(qk_raw[h] * jnp.exp(log_d - m_t))
        m_all.append(m_t)
        w_all.append(jnp.exp(inter - m_t))
        b_last = bc[c - 1:c, :]
        log_w = b_last + (li_c - bc)
        m_new = jnp.maximum(b_last + m_prev, jnp.max(log_w, axis=0, keepdims=True))
        kw = k[:, heads[h]] * jnp.exp(log_w - m_new)
        upd.append((m_new, kw, jnp.exp(b_last + m_prev - m_new)))

    sv = [_dot(s_all[h].astype(BF16), vb[h]) for h in range(nh)]
    kv = [_dot_tn(upd[h][1].astype(BF16), vb[h]) for h in range(nh)]

    outs = []
    for h in range(nh):
        num = w_all[h] * q_state[h] + sv[h]
        den = (w_all[h] * jnp.sum(q[:, heads[h]] * nvec_ref[h], axis=-1, keepdims=True)
               + jnp.sum(s_all[h], axis=-1, keepdims=True))
        outs.append(num / jnp.maximum(jnp.abs(den), jnp.exp(-m_all[h])))
        m_new, kw, decay = upd[h]
        cmat_ref[h] = decay * cmat_ref[h] + kv[h]
        nvec_ref[h] = decay * nvec_ref[h] + jnp.sum(kw, axis=0, keepdims=True)
        m_ref[h:h + 1, :] = jnp.broadcast_to(m_new, (1, LANES))

    hh = _head_rms(jnp.concatenate(outs, axis=-1), gain_ref[...])
    o_ref[...] = hh * jax.nn.sigmoid(og_ref[...])


def mlstm_mixer_pallas(cols, b_, l_, conv_w, conv_b, i_bias, f_bias, o_gain):
    c = ML_TC
    rows = c * ML_CHUNKS_PER_STEP
    nt = l_ // rows
    w = GROUP_WIDTH
    gb = jnp.pad(jnp.concatenate([i_bias, f_bias]), (0, LANES - 2 * GROUP_HEADS)).reshape(1, LANES)
    tril = np.tril(np.ones((c, c), np.float32))
    const = lambda shape: pl.BlockSpec(shape, lambda b, i: (0,) * len(shape))
    blk = lambda width, off: pl.BlockSpec((rows, width), lambda b, i: (b * nt + i, off // width))
    return pl.pallas_call(
        _mlstm_kernel,
        out_shape=jax.ShapeDtypeStruct((b_ * l_, w), F32),
        grid=(b_, nt),
        in_specs=[blk(LANES, C_ML_GATE), blk(2 * w, C_ML_QK), blk(w, C_ML_V), blk(w, C_ML_OG),
                  const((CONV_WIDTH, 2 * w)), const((1, 2 * w)), const((1, LANES)),
                  const((1, HEAD_DIM)), const((c, c)), const((c, c))],
        out_specs=pl.BlockSpec((rows, w), lambda b, i: (b * nt + i, 0)),
        scratch_shapes=[pltpu.VMEM((c, 2 * w), F32),
                        pltpu.VMEM((GROUP_HEADS, HEAD_DIM, HEAD_DIM), F32),
                        pltpu.VMEM((GROUP_HEADS, 1, HEAD_DIM), F32),
                        pltpu.VMEM((8, LANES), F32)],
        compiler_params=pltpu.CompilerParams(
            dimension_semantics=("arbitrary", "arbitrary"),
            vmem_limit_bytes=V7X_VMEM_LIMIT_BYTES),
        name="mlstm",
    )(cols, cols, cols, cols, conv_w, conv_b.reshape(1, 2 * w), gb, o_gain.reshape(1, HEAD_DIM),
      jnp.asarray(tril, BF16), jnp.asarray(tril.T, BF16))


def kernel(x, mem, lb_param, norm_mix, w_in, w_out, hg_o_gain, dsa_kv_gain, dsa_w_uk, dsa_w_uv,
           dsa_q_gain, dsa_k_gain, dsa_idxk_gain, nsa_pos_k, nsa_pos_v, nsa_k_w1, nsa_k_w2,
           nsa_v_w1, nsa_v_w2, nsa_q_gain, nsa_k_gains, ml_conv_w, ml_conv_b, ml_i_bias,
           ml_f_bias, ml_o_gain, norm_xa, norm_mem, xa_wq, xa_wkv, xa_wo, xa_q_gain, xa_k_gain,
           norm_ffn, ffn_w13, ffn_w2):
    b_, l_, d = x.shape
    lb_all = jnp.cumsum(jax.nn.softmax(lb_param.astype(F32), axis=0), axis=0)
    lb_all = lb_all - lb_all[:1]
    x2d = x.reshape(b_ * l_, d)
    mem2d = mem.reshape(b_ * N_MEM, d)
    rope = rope_tables(l_)
    for l in range(DEPTH):
        cols = norm_matmul(x2d, norm_mix[l], pack_w_in_bf16(w_in[l]), tm=256)
        mixers = (
            hgrn2_mixer_pallas(cols, b_, l_, lb_all[l], hg_o_gain[l]),
            dsa_mixer_pallas(cols, b_, l_, rope, dsa_kv_gain[l], dsa_w_uk[l], dsa_w_uv[l],
                             dsa_q_gain[l], dsa_k_gain[l], dsa_idxk_gain[l]),
            nsa_mixer_pallas(cols, b_, l_, rope, nsa_pos_k[l], nsa_pos_v[l], nsa_k_w1[l],
                             nsa_k_w2[l], nsa_v_w1[l], nsa_v_w2[l], nsa_q_gain[l],
                             nsa_k_gains[l]),
            mlstm_mixer_pallas(cols, b_, l_, ml_conv_w[l], ml_conv_b[l], ml_i_bias[l],
                               ml_f_bias[l], ml_o_gain[l]),
        )
        k_mem, v_mem = mem_kv(mem2d, norm_mem[l], xa_wkv[l].astype(BF16), xa_k_gain[l])
        x2d = post_block(x2d, mixers, l_, w_out[l].astype(BF16), norm_xa[l],
                         xa_wq[l].astype(BF16), xa_q_gain[l], k_mem, v_mem,
                         xa_wo[l].astype(BF16), norm_ffn[l], ffn_w13[l].astype(BF16),
                         ffn_w2[l].astype(BF16), tm=512)
    return x2d.reshape(b_, l_, d)
```

```python
import functools

import jax
import jax.numpy as jnp
from jax import lax
import numpy as np
from jax.experimental import pallas as pl
from jax.experimental.pallas import tpu as pltpu

F32 = jnp.float32
BF16 = jnp.bfloat16

D_MODEL = 1024
DEPTH = 2
HEAD_DIM = 64
GROUP_HEADS = 4
GROUP_WIDTH = GROUP_HEADS * HEAD_DIM
ROPE_THETA = 500000.0
EPS = 1e-6
N_MEM = 256
XA_HEADS = 4
XA_WIDTH = XA_HEADS * HEAD_DIM
HG_CHUNK = 64
DSA_LATENT = 128
IDX_HEADS = 8
IDX_DIM = 32
DSA_TOPK_MAX = 256
CMP_BLOCK = 32
CMP_STRIDE = 16
SLC_BLOCK = 64
SLC_SHIFT = 6
SLC_TOPN = 16
WINDOW = 512
CONV_WIDTH = 4
D_FF = 2816

HG_SPLITS = (GROUP_WIDTH,) * 4
DSA_SPLITS = (GROUP_WIDTH, DSA_LATENT, IDX_HEADS * IDX_DIM, IDX_DIM, IDX_HEADS)
NSA_SPLITS = (GROUP_WIDTH,) + (HEAD_DIM,) * 6 + (3 * GROUP_HEADS,)
ML_SPLITS = (2 * GROUP_WIDTH, GROUP_WIDTH, GROUP_WIDTH, GROUP_HEADS, GROUP_HEADS)
GROUP_COLS = (sum(HG_SPLITS), sum(DSA_SPLITS), sum(NSA_SPLITS), sum(ML_SPLITS))
IN_COLS = sum(GROUP_COLS)

V7X_VMEM_LIMIT_BYTES = 56 * 1024 * 1024
LANES = 128
FF_CHUNK = 256


def _round_up(n, m):
    return -(-n // m) * m


def _rms_rows(t, g):
    return t * lax.rsqrt(jnp.mean(t * t, axis=-1, keepdims=True) + EPS) * g


def _const_spec(shape):
    return pl.BlockSpec(shape, lambda *_: (0,) * len(shape), pipeline_mode=pl.Buffered(1))


def _norm_matmul_kernel(x_ref, g_ref, w_ref, o_ref):
    h = _rms_rows(x_ref[...], g_ref[...]).astype(BF16)
    o_ref[...] = jnp.dot(h, w_ref[...], preferred_element_type=F32)


def norm_matmul(x2d, gain, w_bf16, tm):
    m, k = x2d.shape
    n = w_bf16.shape[1]
    return pl.pallas_call(
        _norm_matmul_kernel,
        out_shape=jax.ShapeDtypeStruct((m, n), F32),
        grid=(m // tm,),
        in_specs=[
            pl.BlockSpec((tm, k), lambda i: (i, 0)),
            _const_spec((1, k)),
            _const_spec((k, n)),
        ],
        out_specs=pl.BlockSpec((tm, n), lambda i: (i, 0)),
        compiler_params=pltpu.CompilerParams(
            dimension_semantics=("arbitrary",), vmem_limit_bytes=V7X_VMEM_LIMIT_BYTES),
        name="norm_matmul",
    )(x2d, gain.reshape(1, k), w_bf16)


def _mem_kv_kernel(m_ref, g_ref, w_ref, kg_ref, k_ref, v_ref):
    mn = _rms_rows(m_ref[...], g_ref[...]).astype(BF16)
    kv = jnp.dot(mn, w_ref[...], preferred_element_type=F32)
    kg = kg_ref[...]
    ks = []
    for h in range(XA_HEADS):
        kh = kv[:, h * HEAD_DIM:(h + 1) * HEAD_DIM]
        ks.append(_rms_rows(kh, kg))
    k_ref[...] = jnp.concatenate(ks, axis=-1).astype(BF16)
    v_ref[...] = kv[:, XA_WIDTH:].astype(BF16)


def mem_kv(mem2d, gain, wkv_bf16, k_gain):
    m, k = mem2d.shape
    return pl.pallas_call(
        _mem_kv_kernel,
        out_shape=(jax.ShapeDtypeStruct((m, XA_WIDTH), BF16),
                   jax.ShapeDtypeStruct((m, XA_WIDTH), BF16)),
        grid=(m // N_MEM,),
        in_specs=[
            pl.BlockSpec((N_MEM, k), lambda i: (i, 0)),
            _const_spec((1, k)),
            _const_spec((k, 2 * XA_WIDTH)),
            _const_spec((1, HEAD_DIM)),
        ],
        out_specs=(pl.BlockSpec((N_MEM, XA_WIDTH), lambda i: (i, 0)),
                   pl.BlockSpec((N_MEM, XA_WIDTH), lambda i: (i, 0))),
        compiler_params=pltpu.CompilerParams(
            dimension_semantics=("arbitrary",), vmem_limit_bytes=V7X_VMEM_LIMIT_BYTES),
        name="mem_kv",
    )(mem2d, gain.reshape(1, k), wkv_bf16, k_gain.reshape(1, HEAD_DIM))


def _post_kernel(x_ref, mhg_ref, mdsa_ref, mnsa_ref, mml_ref, wout_ref, gxa_ref, wq_ref, qg_ref,
                 k_ref, v_ref, wo_ref, gffn_ref, w13_ref, w2_ref, o_ref):
    x = x_ref[...]
    for g, mix_ref in enumerate((mhg_ref, mdsa_ref, mnsa_ref, mml_ref)):
        x = x + _dot(mix_ref[...].astype(BF16),
                     wout_ref[g * GROUP_WIDTH:(g + 1) * GROUP_WIDTH, :])
    h = _rms_rows(x, gxa_ref[...]).astype(BF16)
    q = _dot(h, wq_ref[...])
    qg = qg_ref[...] * (HEAD_DIM ** -0.5)
    k = k_ref[...]
    v = v_ref[...]
    heads = [slice(hd * HEAD_DIM, (hd + 1) * HEAD_DIM) for hd in range(XA_HEADS)]
    scores = [_dot_nt(_rms_rows(q[:, sl], qg).astype(BF16), k[:, sl]) for sl in heads]
    probs = []
    for s in scores:
        e = jnp.exp(s - jnp.max(s, axis=-1, keepdims=True))
        probs.append((e / jnp.sum(e, axis=-1, keepdims=True)).astype(BF16))
    o = jnp.concatenate([_dot(p, v[:, sl]) for p, sl in zip(probs, heads)], axis=-1)
    x = x + _dot(o.astype(BF16), wo_ref[...])
    h = _rms_rows(x, gffn_ref[...]).astype(BF16)

    def up(c):
        return (_dot(h, w13_ref[:, c * FF_CHUNK:(c + 1) * FF_CHUNK]),
                _dot(h, w13_ref[:, D_FF + c * FF_CHUNK:D_FF + (c + 1) * FF_CHUNK]))

    n_chunks = D_FF // FF_CHUNK
    acc = x
    a, b = up(0)
    for c in range(n_chunks):
        nxt = up(c + 1) if c + 1 < n_chunks else None
        act = (a * jax.nn.sigmoid(a) * b).astype(BF16)
        acc = acc + _dot(act, w2_ref[c * FF_CHUNK:(c + 1) * FF_CHUNK, :])
        if nxt is not None:
            a, b = nxt
    o_ref[...] = acc


def post_block(x2d, mixers, seq, wout, gxa, wq, qg, k_mem, v_mem, wo, gffn, w13, w2, tm):
    m, d = x2d.shape
    tiles_per_batch = seq // tm
    row = lambda i: (i, 0)
    mem_row = lambda i: (i // tiles_per_batch, 0)
    return pl.pallas_call(
        _post_kernel,
        out_shape=jax.ShapeDtypeStruct((m, d), F32),
        grid=(m // tm,),
        in_specs=[
            pl.BlockSpec((tm, d), row),
            *[pl.BlockSpec((tm, GROUP_WIDTH), row) for _ in mixers],
            _const_spec(wout.shape),
            _const_spec((1, d)),
            _const_spec(wq.shape),
            _const_spec((1, HEAD_DIM)),
            pl.BlockSpec((N_MEM, XA_WIDTH), mem_row),
            pl.BlockSpec((N_MEM, XA_WIDTH), mem_row),
            _const_spec(wo.shape),
            _const_spec((1, d)),
            _const_spec(w13.shape),
            _const_spec(w2.shape),
        ],
        out_specs=pl.BlockSpec((tm, d), row),
        compiler_params=pltpu.CompilerParams(
            dimension_semantics=("arbitrary",), vmem_limit_bytes=V7X_VMEM_LIMIT_BYTES),
        name="post_block",
    )(x2d, *mixers, wout, gxa.reshape(1, d), wq, qg.reshape(1, HEAD_DIM), k_mem, v_mem, wo,
      gffn.reshape(1, d), w13, w2)


C_HG = 0
C_DSA_Q = 1024
C_DSA_IQ = 1280
C_NSA_Q = 1536
C_ML_V = 1792
C_ML_QK = 2048
C_ML_OG = 2560
C_DSA_CKV = 2816
C_DSA_IK = 2944
C_DSA_IW = 3072
C_NSA_GATE = 3200
C_NSA_CMP = 3328
C_NSA_SLC = 3456
C_NSA_WIN = 3584
C_ML_GATE = 3712
IN_COLS_PACKED = 3840


def _packed_source_columns():
    dsa0 = GROUP_COLS[0]
    nsa0 = dsa0 + GROUP_COLS[1]
    ml0 = nsa0 + GROUP_COLS[2]
    segs = [(0, 1024, 1024), (dsa0, 256, 256), (dsa0 + 384, 256, 256), (nsa0, 256, 256),
            (ml0 + 512, 256, 256), (ml0, 512, 512), (ml0 + 768, 256, 256), (dsa0 + 256, 128, 128),
            (dsa0 + 640, 32, LANES), (dsa0 + 672, 8, LANES), (nsa0 + 640, 12, LANES),
            (nsa0 + 256, 128, 128), (nsa0 + 384, 128, 128), (nsa0 + 512, 128, 128),
            (ml0 + 1024, 8, LANES)]
    src = np.concatenate([np.concatenate([np.arange(a, a + n), np.full(width - n, -1)])
                          for a, n, width in segs]).astype(np.int32)
    assert src.shape == (IN_COLS_PACKED,)
    return src


PACK_TN = 256


def _pack_w_kernel(w_ref, src_ref, o_ref, wb_ref):
    @pl.when(pl.program_id(0) == 0)
    def _():
        wb_ref[...] = w_ref[...].astype(BF16)

    row = lax.broadcasted_iota(jnp.int32, (wb_ref.shape[1], PACK_TN), 0)
    sel = jnp.where(row == src_ref[...], 1.0, 0.0).astype(BF16)
    o_ref[...] = _dot(wb_ref[...], sel).astype(BF16)


def pack_w_in_bf16(w):
    d, n = w.shape
    n_pad = _round_up(n, LANES)
    w_pad = jnp.pad(w, ((0, 0), (0, n_pad - n)))
    src = jnp.asarray(_packed_source_columns()).reshape(1, IN_COLS_PACKED)
    return pl.pallas_call(
        _pack_w_kernel,
        out_shape=jax.ShapeDtypeStruct((d, IN_COLS_PACKED), BF16),
        grid=(IN_COLS_PACKED // PACK_TN,),
        in_specs=[_const_spec((d, n_pad)), pl.BlockSpec((1, PACK_TN), lambda j: (0, j))],
        out_specs=pl.BlockSpec((d, PACK_TN), lambda j: (0, j)),
        scratch_shapes=[pltpu.VMEM((d, n_pad), BF16)],
        compiler_params=pltpu.CompilerParams(
            dimension_semantics=("arbitrary",), vmem_limit_bytes=V7X_VMEM_LIMIT_BYTES),
        name="pack_w_in",
    )(w_pad, src)


def rope_tables(l_, d=HEAD_DIM):
    rd = d // 4
    half = rd // 2
    inv = ROPE_THETA ** (-jnp.arange(half, dtype=F32) * 2.0 / rd)
    ang = jnp.arange(l_).astype(F32)[:, None] * inv[None, :]
    cos, sin = lax.optimization_barrier((jnp.cos(ang), jnp.sin(ang)))
    zh = jnp.zeros((l_, half), F32)
    rest0 = jnp.zeros((l_, d - rd), F32)
    c = jnp.concatenate([cos, cos, rest0 + 1.0], axis=1)
    s1 = jnp.concatenate([-sin, zh, rest0], axis=1)
    s2 = jnp.concatenate([zh, sin, rest0], axis=1)
    return jnp.stack([c, s1, s2])


def rope_tables_pad(rt, width):
    n = width - rt.shape[-1]
    ident = jnp.stack([jnp.ones(rt.shape[1:2] + (n,), F32), jnp.zeros(rt.shape[1:2] + (n,), F32),
                       jnp.zeros(rt.shape[1:2] + (n,), F32)])
    return jnp.concatenate([rt, ident], axis=-1)


def rope_tables_kv_pair(rt):
    return rope_tables_pad(rt, 2 * rt.shape[-1])


def _apply_rope(t, rope_ref, half=HEAD_DIM // 8):
    w = t.shape[-1]
    return (t * rope_ref[0] + pltpu.roll(t, w - half, 1) * rope_ref[1]
            + pltpu.roll(t, half, 1) * rope_ref[2])


_NT = (((1,), (1,)), ((), ()))


def _dot_nt(a, b):
    return lax.dot_general(a, b, _NT, preferred_element_type=F32)


def _dot(a, b):
    return jnp.dot(a, b, preferred_element_type=F32)


NEG_BIG = -(2.0 ** 30)
NSA_KEY_TILE = 512
NSA_TQ = 256
NSA_SEL_TQ = 512
NSA_PREP_TK = 512


def _nsa_kv_prep_kernel(ps_ref, pw_ref, rope_ref, gs_ref, gw_ref, kaug_ref, vs_ref, kw_ref, vw_ref):
    tk = ps_ref.shape[0]
    lane = lax.broadcasted_iota(jnp.int32, (tk, LANES), 1)
    is_k = lane < HEAD_DIM

    def norm_rope(p, g):
        ms = jnp.sum(jnp.where(is_k, p * p, 0.0), axis=-1, keepdims=True) * (1.0 / HEAD_DIM)
        y = jnp.where(is_k, p * lax.rsqrt(ms + EPS) * g, p)
        return _apply_rope(y, rope_ref)

    ys = norm_rope(ps_ref[...], gs_ref[...])
    yw = norm_rope(pw_ref[...], gw_ref[...])
    row = pl.program_id(1) * tk + lax.broadcasted_iota(jnp.int32, (tk, LANES), 0)
    ind = jnp.where(jnp.right_shift(row, SLC_SHIFT) == (lane - HEAD_DIM), 1.0, 0.0)
    kaug_ref[0] = jnp.where(is_k, ys, ind).astype(BF16)
    vs_ref[0] = ys.T[HEAD_DIM:, :].astype(BF16)
    kw_ref[0] = yw[:, :HEAD_DIM].astype(BF16)
    vw_ref[0] = yw.T[HEAD_DIM:, :].astype(BF16)


def nsa_kv_prep(cols, b_, l_, rope_pair, g_slc, g_win):
    tk = NSA_PREP_TK
    nt = l_ // tk
    ones = jnp.ones((HEAD_DIM,), F32)
    gs = jnp.concatenate([g_slc, ones]).reshape(1, LANES)
    gw = jnp.concatenate([g_win, ones]).reshape(1, LANES)
    kv = lambda w: jax.ShapeDtypeStruct((b_, l_, w), BF16)
    kv_t = jax.ShapeDtypeStruct((b_, HEAD_DIM, l_), BF16)
    out_blk = lambda w: pl.BlockSpec((1, tk, w), lambda b, i: (b, i, 0))
    out_t = pl.BlockSpec((1, HEAD_DIM, tk), lambda b, i: (b, 0, i))
    return pl.pallas_call(
        _nsa_kv_prep_kernel,
        out_shape=(kv(LANES), kv_t, kv(HEAD_DIM), kv_t),
        grid=(b_, nt),
        in_specs=[
            pl.BlockSpec((tk, LANES), lambda b, i: (b * nt + i, C_NSA_SLC // LANES)),
            pl.BlockSpec((tk, LANES), lambda b, i: (b * nt + i, C_NSA_WIN // LANES)),
            pl.BlockSpec((3, tk, LANES), lambda b, i: (0, i, 0)),
            pl.BlockSpec((1, LANES), lambda b, i: (0, 0)),
            pl.BlockSpec((1, LANES), lambda b, i: (0, 0)),
        ],
        out_specs=(out_blk(LANES), out_t, out_blk(HEAD_DIM), out_t),
        compiler_params=pltpu.CompilerParams(
            dimension_semantics=("arbitrary", "arbitrary"),
            vmem_limit_bytes=V7X_VMEM_LIMIT_BYTES),
        name="nsa_kv_prep",
    )(cols, cols, rope_pair, gs, gw)


def _nsa_compress_kernel(r_ref, pea_ref, peb_ref, w1a_ref, w1b_ref, w2k_ref, w2v_ref, kg_ref,
                         kc_ref, vc_ref):
    r = r_ref[0]
    n = r.shape[0]
    a = _dot((r + pea_ref[...]).astype(BF16), w1a_ref[...])
    bm = _dot((r + peb_ref[...]).astype(BF16), w1b_ref[...])
    row = lax.broadcasted_iota(jnp.int32, bm.shape, 0)
    bm_up = jnp.where(row < n - 1, pltpu.roll(bm, n - 1, 0), 0.0)
    h = jnp.maximum(a + bm_up, 0.0).astype(BF16)
    hid = w2k_ref.shape[0]
    ck = _dot(h[:, :hid], w2k_ref[...])
    cv = _dot(h[:, hid:], w2v_ref[...])
    kc_ref[0] = _rms_rows(ck, kg_ref[...]).astype(BF16)
    vc_ref[0] = cv.astype(BF16)


def nsa_compress(cols, b_, l_, pos_k, pos_v, k_w1, k_w2, v_w1, v_w2, k_gain):
    rows = l_ // CMP_STRIDE
    hid = k_w1.shape[1]
    pair = cols[:, C_NSA_CMP:C_NSA_CMP + LANES].reshape(b_, rows, CMP_STRIDE * LANES)

    def interleave_pe(lo):
        pe = jnp.concatenate([pos_k[lo:lo + CMP_STRIDE], pos_v[lo:lo + CMP_STRIDE]], axis=1)
        return pe.reshape(1, CMP_STRIDE * LANES)

    def interleave_w(lo):
        wk = k_w1[lo * HEAD_DIM:(lo + CMP_STRIDE) * HEAD_DIM].reshape(CMP_STRIDE, HEAD_DIM, hid)
        wv = v_w1[lo * HEAD_DIM:(lo + CMP_STRIDE) * HEAD_DIM].reshape(CMP_STRIDE, HEAD_DIM, hid)
        z = jnp.zeros_like(wk)
        top = jnp.concatenate([wk, z], axis=2)
        bot = jnp.concatenate([z, wv], axis=2)
        return jnp.concatenate([top, bot], axis=1).reshape(CMP_STRIDE * LANES, 2 * hid).astype(BF16)

    out = jax.ShapeDtypeStruct((b_, rows, HEAD_DIM), BF16)
    return pl.pallas_call(
        _nsa_compress_kernel,
        out_shape=(out, out),
        grid=(b_,),
        in_specs=[
            pl.BlockSpec((1, rows, CMP_STRIDE * LANES), lambda b: (b, 0, 0)),
            _const_spec((1, CMP_STRIDE * LANES)),
            _const_spec((1, CMP_STRIDE * LANES)),
            _const_spec((CMP_STRIDE * LANES, 2 * hid)),
            _const_spec((CMP_STRIDE * LANES, 2 * hid)),
            _const_spec((hid, HEAD_DIM)),
            _const_spec((hid, HEAD_DIM)),
            _const_spec((1, HEAD_DIM)),
        ],
        out_specs=(pl.BlockSpec((1, rows, HEAD_DIM), lambda b: (b, 0, 0)),
                   pl.BlockSpec((1, rows, HEAD_DIM), lambda b: (b, 0, 0))),
        compiler_params=pltpu.CompilerParams(
            dimension_semantics=("arbitrary",), vmem_limit_bytes=V7X_VMEM_LIMIT_BYTES),
        name="nsa_compress",
    )(pair, interleave_pe(0), interleave_pe(CMP_STRIDE), interleave_w(0),
      interleave_w(CMP_STRIDE), k_w2.astype(BF16), v_w2.astype(BF16),
      k_gain.reshape(1, HEAD_DIM))


def _masked_softmax(s, valid, axis):
    m = jnp.max(jnp.where(valid, s, -jnp.inf), axis=axis, keepdims=True)
    m = jnp.where(m == -jnp.inf, 0.0, m)
    e = jnp.where(valid, jnp.exp(s - m), 0.0)
    den = jnp.sum(e, axis=axis, keepdims=True)
    return e / jnp.where(den > 0, den, 1.0)


def _nsa_select_kernel(q_ref, kc_ref, vc_ref, ovt_ref, rope_ref, qg_ref, qaug_ref, ocmp_ref,
                       *, n_sel):
    tq = q_ref.shape[0]
    ncr = kc_ref.shape[1]
    n_slc = ovt_ref.shape[0]
    t0 = pl.program_id(1) * tq
    scale = HEAD_DIM ** -0.5
    q = q_ref[...]
    g = qg_ref[...]
    kc = kc_ref[0]
    vc = vc_ref[0]
    vis_t = (lax.broadcasted_iota(jnp.int32, (ncr, tq), 0) * CMP_STRIDE + (CMP_BLOCK - 1)
             <= t0 + lax.broadcasted_iota(jnp.int32, (ncr, tq), 1))
    qn_heads, o_heads = [], []
    psum_t = jnp.zeros((ncr, tq), F32)
    for h in range(GROUP_HEADS):
        qn = _rms_rows(q[:, h * HEAD_DIM:(h + 1) * HEAD_DIM], g)
        qn_heads.append(qn)
        p_t = _masked_softmax(_dot_nt(kc, (qn * scale).astype(BF16)), vis_t, 0)
        o_heads.append(_dot_tn(vc, p_t.astype(BF16)))
        psum_t = psum_t + p_t
    ocmp_ref[0] = jnp.concatenate(o_heads, axis=0)

    hi = psum_t.astype(BF16)
    lo = (psum_t - hi.astype(F32)).astype(BF16)
    imp = _dot(ovt_ref[...], hi) + _dot(ovt_ref[...], lo)
    blk = lax.broadcasted_iota(jnp.int32, (n_slc, tq), 0)
    cur = jnp.right_shift(t0 + lax.broadcasted_iota(jnp.int32, (n_slc, tq), 1), SLC_SHIFT)
    forced = (blk == 0) | (blk == cur) | (blk == cur - 1)
    imp = jnp.where(forced, jnp.inf, jnp.where(blk > cur, -jnp.inf, imp))
    n_grp = n_slc // SUBLANES
    groups = [imp[g * SUBLANES:(g + 1) * SUBLANES] for g in range(n_grp)]
    cnts = [jnp.zeros((SUBLANES, tq), F32) for _ in range(n_grp)]
    sub = lax.broadcasted_iota(jnp.int32, (SUBLANES, tq), 0)
    for m in range(n_slc):
        row = imp[m:m + 1, :]
        gm, rm = divmod(m, SUBLANES)
        for g in range(n_grp):
            if g < gm:
                beats = row > groups[g]
            elif g > gm:
                beats = row >= groups[g]
            else:
                beats = (row > groups[g]) | ((row == groups[g]) & (sub > rm))
            cnts[g] = cnts[g] + jnp.where(beats, 1.0, 0.0)
    cnt = jnp.concatenate(cnts, axis=0)
    mt = jnp.where(cnt < n_sel, 0.0, NEG_BIG)
    pad = jnp.zeros((HEAD_DIM - n_slc, tq), F32)
    mt = jnp.concatenate([mt, pad, mt, pad], axis=0) if n_slc < HEAD_DIM else jnp.concatenate(
        [mt, mt], axis=0)
    mt = mt.T

    qr = _apply_rope(jnp.concatenate(qn_heads, axis=-1), rope_ref) * scale
    lane = lax.broadcasted_iota(jnp.int32, (tq, LANES), 1)
    for j in range(GROUP_HEADS // 2):
        pair = qr[:, j * LANES:(j + 1) * LANES]
        swapped = pltpu.roll(pair, HEAD_DIM, 1)
        qaug_ref[0, :, (2 * j) * LANES:(2 * j + 1) * LANES] = jnp.where(
            lane < HEAD_DIM, pair, mt).astype(BF16)
        qaug_ref[0, :, (2 * j + 1) * LANES:(2 * j + 2) * LANES] = jnp.where(
            lane < HEAD_DIM, swapped, mt).astype(BF16)


def nsa_select(cols, b_, l_, k_cmp, v_cmp, rope_q, q_gain):
    tq = min(NSA_SEL_TQ, l_)
    nt = l_ // tq
    ncr = l_ // CMP_STRIDE
    n_slc = l_ // SLC_BLOCK
    n_sel = min(SLC_TOPN, n_slc)
    st_c = np.arange(ncr) * CMP_STRIDE
    st_s = np.arange(n_slc) * SLC_BLOCK
    ovt = ((st_c[None, :] < st_s[:, None] + SLC_BLOCK)
           & (st_c[None, :] + CMP_BLOCK > st_s[:, None])).astype(np.float32)
    return pl.pallas_call(
        functools.partial(_nsa_select_kernel, n_sel=n_sel),
        out_shape=(jax.ShapeDtypeStruct((b_, l_, GROUP_HEADS * LANES), BF16),
                   jax.ShapeDtypeStruct((b_, GROUP_WIDTH, l_), F32)),
        grid=(b_, nt),
        in_specs=[
            pl.BlockSpec((tq, GROUP_WIDTH), lambda b, i: (b * nt + i, C_NSA_Q // GROUP_WIDTH)),
            pl.BlockSpec((1, ncr, HEAD_DIM), lambda b, i: (b, 0, 0)),
            pl.BlockSpec((1, ncr, HEAD_DIM), lambda b, i: (b, 0, 0)),
            pl.BlockSpec((n_slc, ncr), lambda b, i: (0, 0)),
            pl.BlockSpec((3, tq, GROUP_WIDTH), lambda b, i: (0, i, 0)),
            pl.BlockSpec((1, HEAD_DIM), lambda b, i: (0, 0)),
        ],
        out_specs=(pl.BlockSpec((1, tq, GROUP_HEADS * LANES), lambda b, i: (b, i, 0)),
                   pl.BlockSpec((1, GROUP_WIDTH, tq), lambda b, i: (b, 0, i))),
        compiler_params=pltpu.CompilerParams(
            dimension_semantics=("arbitrary", "arbitrary"),
            vmem_limit_bytes=V7X_VMEM_LIMIT_BYTES),
        name="nsa_select",
    )(cols, k_cmp, v_cmp, jnp.asarray(ovt, BF16), rope_q, q_gain.reshape(1, HEAD_DIM))


def _softmax_stats_update(s, m, l):
    m_new = jnp.maximum(m, jnp.max(s, axis=0, keepdims=True))
    alpha = jnp.exp(m - m_new)
    p = jnp.exp(s - m_new)
    return m_new, alpha, p, alpha * l + jnp.sum(p, axis=0, keepdims=True)


def _nsa_attend_t_kernel(qaug_ref, kaug_ref, vst_ref, kw_ref, vwt_ref, ocmp_ref, gate_ref, o_ref):
    tq = qaug_ref.shape[1]
    nh = GROUP_HEADS
    ts = NSA_KEY_TILE
    t0 = pl.program_id(1) * tq
    qa = qaug_ref[0]
    qs = jnp.concatenate([qa[:, h * LANES:(h + 1) * LANES] for h in range(nh)], axis=0)
    n = nh * tq
    qpos_tile = t0 + lax.broadcasted_iota(jnp.int32, (1, tq), 1)
    qpos = jnp.concatenate([qpos_tile] * nh, axis=1)

    def key_tile(kt, carry):
        m, l, acc = carry
        k0 = pl.multiple_of(kt * ts, ts)
        s = _dot_nt(kaug_ref[0, pl.ds(k0, ts), :], qs)
        kpos = k0 + lax.broadcasted_iota(jnp.int32, (ts, n), 0)
        s = jnp.where(kpos <= qpos, s, NEG_BIG)
        m, alpha, p, l = _softmax_stats_update(s, m, l)
        acc = alpha * acc + _dot(vst_ref[0, :, pl.ds(k0, ts)], p.astype(BF16))
        return m, l, acc

    n_kt = (t0 + tq + ts - 1) // ts
    init = (jnp.full((1, n), -jnp.inf, F32), jnp.zeros((1, n), F32),
            jnp.zeros((HEAD_DIM, n), F32))
    _, l, acc = lax.fori_loop(0, n_kt, key_tile, init)
    o_slc = acc / l

    wlen = WINDOW + tq
    start = pl.multiple_of(jnp.maximum(t0 - WINDOW, 0), tq)
    sw = _dot_nt(kw_ref[0, pl.ds(start, wlen), :], qs[:, :HEAD_DIM])
    dist = qpos - (start + lax.broadcasted_iota(jnp.int32, (wlen, n), 0))
    sw = jnp.where((dist >= 0) & (dist < WINDOW), sw, -jnp.inf)
    e = jnp.exp(sw - jnp.max(sw, axis=0, keepdims=True))
    o_swa = _dot(vwt_ref[0, :, pl.ds(start, wlen)], e.astype(BF16)) / jnp.sum(
        e, axis=0, keepdims=True)

    g = jax.nn.sigmoid(gate_ref[...]).T
    oc = ocmp_ref[0]
    outs = []
    for h in range(nh):
        cols_h = slice(h * tq, (h + 1) * tq)
        outs.append(g[h:h + 1, :] * oc[h * HEAD_DIM:(h + 1) * HEAD_DIM, :]
                    + g[nh + h:nh + h + 1, :] * o_slc[:, cols_h]
                    + g[2 * nh + h:2 * nh + h + 1, :] * o_swa[:, cols_h])
    o_ref[...] = jnp.concatenate(outs, axis=0).T


def nsa_attend(cols, b_, l_, q_aug, k_aug, v_slc, k_win, v_win, o_cmp):
    tq = NSA_TQ
    nt = l_ // tq
    seq = lambda w: pl.BlockSpec((1, l_, w), lambda b, i: (b, 0, 0))
    seq_t = pl.BlockSpec((1, HEAD_DIM, l_), lambda b, i: (b, 0, 0))
    return pl.pallas_call(
        _nsa_attend_t_kernel,
        out_shape=jax.ShapeDtypeStruct((b_ * l_, GROUP_WIDTH), F32),
        grid=(b_, nt),
        in_specs=[
            pl.BlockSpec((1, tq, GROUP_HEADS * LANES), lambda b, i: (b, i, 0)),
            seq(LANES), seq_t, seq(HEAD_DIM), seq_t,
            pl.BlockSpec((1, GROUP_WIDTH, tq), lambda b, i: (b, 0, i)),
            pl.BlockSpec((tq, LANES), lambda b, i: (b * nt + i, C_NSA_GATE // LANES)),
        ],
        out_specs=pl.BlockSpec((tq, GROUP_WIDTH), lambda b, i: (b * nt + i, 0)),
        compiler_params=pltpu.CompilerParams(
            dimension_semantics=("arbitrary", "arbitrary"),
            vmem_limit_bytes=V7X_VMEM_LIMIT_BYTES),
        name="nsa_attend",
    )(q_aug, k_aug, v_slc, k_win, v_win, o_cmp, cols)


def nsa_mixer_pallas(cols, b_, l_, rope, pos_k, pos_v, k_w1, k_w2, v_w1, v_w2, q_gain, k_gains):
    k_aug, v_slc, k_win, v_win = nsa_kv_prep(cols, b_, l_, rope_tables_kv_pair(rope),
                                             k_gains[1], k_gains[2])
    k_cmp, v_cmp = nsa_compress(cols, b_, l_, pos_k, pos_v, k_w1, k_w2, v_w1, v_w2, k_gains[0])
    q_aug, o_cmp = nsa_select(cols, b_, l_, k_cmp, v_cmp, jnp.tile(rope, (1, 1, GROUP_HEADS)),
                              q_gain)
    return nsa_attend(cols, b_, l_, q_aug, k_aug, v_slc, k_win, v_win, o_cmp)


DSA_TQ = 512
DSA_KEY_TILE = 512
DSA_PREP_TK = 512
IDX_PACK = LANES
INT_MIN = -2 ** 31
MASKED_SCORE = -1e30


def _split_hi_lo(t):
    hi = t.astype(BF16)
    lo = (t - hi.astype(F32)).astype(BF16)
    return hi, lo


def _placement(rows, cols, pairs):
    p = np.zeros((rows, cols), np.float32)
    for r, c in pairs:
        p[r, c] = 1.0
    return jnp.asarray(p, BF16)


def _dsa_kv_prep_kernel(ckv_ref, ik_ref, rope_ref, ropei_ref, kvg_ref, wkv_ref, kg_ref, ikg_ref,
                        pkh_ref, pkl_ref, k_ref, vt_ref, ik3_ref):
    tk = ckv_ref.shape[0]
    lane = lax.broadcasted_iota(jnp.int32, (tk, LANES), 1)
    ckv = _rms_rows(ckv_ref[...], kvg_ref[...]).astype(BF16)
    kv = _dot(ckv, wkv_ref[...])
    is_k = lane < HEAD_DIM
    ms = jnp.sum(jnp.where(is_k, kv * kv, 0.0), axis=-1, keepdims=True) * (1.0 / HEAD_DIM)
    y = _apply_rope(jnp.where(is_k, kv * lax.rsqrt(ms + EPS) * kg_ref[...], kv), rope_ref)
    k_ref[0] = y[:, :HEAD_DIM].astype(BF16)
    vt_ref[0] = y.T[HEAD_DIM:, :].astype(BF16)
    ik = ik_ref[...]
    ms = jnp.sum(ik * ik, axis=-1, keepdims=True) * (1.0 / IDX_DIM)
    ikn = _apply_rope(ik * lax.rsqrt(ms + EPS) * ikg_ref[...], ropei_ref, IDX_DIM // 8)
    hi, lo = _split_hi_lo(ikn)
    ik3_ref[0] = (_dot(hi, pkh_ref[...]) + _dot(lo, pkl_ref[...])).astype(BF16)


def dsa_kv_prep(cols, b_, l_, rope_pair, rope_idx, kv_gain, w_uk, w_uv, k_gain, idxk_gain):
    tk = DSA_PREP_TK
    nt = l_ // tk
    ones = jnp.ones((HEAD_DIM,), F32)
    kg = jnp.concatenate([k_gain, ones]).reshape(1, LANES)
    ikg = jnp.pad(idxk_gain, (0, LANES - IDX_DIM)).reshape(1, LANES)
    wkv = jnp.concatenate([w_uk, w_uv], axis=1).astype(BF16)
    d = range(IDX_DIM)
    pkh = _placement(LANES, IDX_PACK, [(i, i) for i in d] + [(i, 2 * IDX_DIM + i) for i in d])
    pkl = _placement(LANES, IDX_PACK, [(i, IDX_DIM + i) for i in d])
    const = lambda shape: pl.BlockSpec(shape, lambda b, i: (0,) * len(shape))
    out = lambda w: pl.BlockSpec((1, tk, w), lambda b, i: (b, i, 0))
    return pl.pallas_call(
        _dsa_kv_prep_kernel,
        out_shape=(jax.ShapeDtypeStruct((b_, l_, HEAD_DIM), BF16),
                   jax.ShapeDtypeStruct((b_, HEAD_DIM, l_), BF16),
                   jax.ShapeDtypeStruct((b_, l_, IDX_PACK), BF16)),
        grid=(b_, nt),
        in_specs=[
            pl.BlockSpec((tk, LANES), lambda b, i: (b * nt + i, C_DSA_CKV // LANES)),
            pl.BlockSpec((tk, LANES), lambda b, i: (b * nt + i, C_DSA_IK // LANES)),
            pl.BlockSpec((3, tk, LANES), lambda b, i: (0, i, 0)),
            pl.BlockSpec((3, tk, LANES), lambda b, i: (0, i, 0)),
            const((1, LANES)), const((DSA_LATENT, LANES)), const((1, LANES)), const((1, LANES)),
            const((LANES, IDX_PACK)), const((LANES, IDX_PACK)),
        ],
        out_specs=(out(HEAD_DIM), pl.BlockSpec((1, HEAD_DIM, tk), lambda b, i: (b, 0, i)),
                   out(IDX_PACK)),
        compiler_params=pltpu.CompilerParams(
            dimension_semantics=("arbitrary", "arbitrary"),
            vmem_limit_bytes=V7X_VMEM_LIMIT_BYTES),
        name="dsa_kv_prep",
    )(cols, cols, rope_pair, rope_idx, kv_gain.reshape(1, DSA_LATENT), wkv, kg, ikg, pkh, pkl)


def _dsa_q_prep_kernel(q_ref, iq_ref, rope_ref, ropei_ref, qg_ref, pqh_ref, pql_ref,
                       qh_ref, iq3_ref):
    g = qg_ref[...]
    q = q_ref[...]
    qn = jnp.concatenate([_rms_rows(q[:, h * HEAD_DIM:(h + 1) * HEAD_DIM], g)
                          for h in range(GROUP_HEADS)], axis=-1)
    qr = _apply_rope(qn, rope_ref) * (HEAD_DIM ** -0.5)
    for h in range(GROUP_HEADS):
        qh_ref[0, h] = qr[:, h * HEAD_DIM:(h + 1) * HEAD_DIM].astype(BF16)
    hi, lo = _split_hi_lo(_apply_rope(iq_ref[...], ropei_ref, IDX_DIM // 8))
    iq3_ref[0] = (_dot(hi, pqh_ref[...]) + _dot(lo, pql_ref[...])).astype(BF16)


def dsa_q_prep(cols, b_, l_, rope_q, rope_iq, q_gain):
    tq = DSA_PREP_TK
    nt = l_ // tq
    w = IDX_HEADS * IDX_DIM
    hd = [(h, i) for h in range(IDX_HEADS) for i in range(IDX_DIM)]
    pqh = _placement(w, IDX_HEADS * IDX_PACK,
                     [(IDX_DIM * h + i, IDX_PACK * h + i) for h, i in hd]
                     + [(IDX_DIM * h + i, IDX_PACK * h + IDX_DIM + i) for h, i in hd])
    pql = _placement(w, IDX_HEADS * IDX_PACK,
                     [(IDX_DIM * h + i, IDX_PACK * h + 2 * IDX_DIM + i) for h, i in hd])
    const = lambda shape: pl.BlockSpec(shape, lambda b, i: (0,) * len(shape))
    return pl.pallas_call(
        _dsa_q_prep_kernel,
        out_shape=(jax.ShapeDtypeStruct((b_, GROUP_HEADS, l_, HEAD_DIM), BF16),
                   jax.ShapeDtypeStruct((b_, l_, IDX_HEADS * IDX_PACK), BF16)),
        grid=(b_, nt),
        in_specs=[
            pl.BlockSpec((tq, GROUP_WIDTH), lambda b, i: (b * nt + i, C_DSA_Q // GROUP_WIDTH)),
            pl.BlockSpec((tq, w), lambda b, i: (b * nt + i, C_DSA_IQ // w)),
            pl.BlockSpec((3, tq, GROUP_WIDTH), lambda b, i: (0, i, 0)),
            pl.BlockSpec((3, tq, w), lambda b, i: (0, i, 0)),
            const((1, HEAD_DIM)), const(pqh.shape), const(pql.shape),
        ],
        out_specs=(pl.BlockSpec((1, GROUP_HEADS, tq, HEAD_DIM), lambda b, i: (b, 0, i, 0)),
                   pl.BlockSpec((1, tq, IDX_HEADS * IDX_PACK), lambda b, i: (b, i, 0))),
        compiler_params=pltpu.CompilerParams(
            dimension_semantics=("arbitrary", "arbitrary"),
            vmem_limit_bytes=V7X_VMEM_LIMIT_BYTES),
        name="dsa_q_prep",
    )(cols, cols, rope_q, rope_iq, q_gain.reshape(1, HEAD_DIM), pqh, pql)


SUBLANES = 8
FOLD_CHAINS = 4


def _fold_rows(t, op, group=SUBLANES):
    parts = [t[i * group:(i + 1) * group] for i in range(t.shape[0] // group)]
    lanes = parts[:FOLD_CHAINS]
    for i, p in enumerate(parts[FOLD_CHAINS:]):
        lanes[i % FOLD_CHAINS] = op(lanes[i % FOLD_CHAINS], p)
    while len(lanes) > 1:
        lanes = [op(lanes[i], lanes[i + 1]) if i + 1 < len(lanes) else lanes[i]
                 for i in range(0, len(lanes), 2)]
    return lanes[0]


def _dsa_attend_t_kernel(qh_ref, iq3_ref, iw_ref, k_ref, vt_ref, ik3_ref, o_ref, sc_ref, *,
                         topk, idx_bits):
    tq = iq3_ref.shape[1]
    ts = DSA_KEY_TILE
    nh = GROUP_HEADS
    t0 = pl.program_id(1) * tq
    n_kt = (t0 + tq + ts - 1) // ts
    qpos = t0 + lax.broadcasted_iota(jnp.int32, (ts, tq), 1)
    krow = lax.broadcasted_iota(jnp.int32, (ts, tq), 0)
    sub_row = lax.broadcasted_iota(jnp.int32, (SUBLANES, tq), 0)

    iq3 = iq3_ref[0]
    iq_rows = jnp.concatenate([iq3[:, h * IDX_PACK:(h + 1) * IDX_PACK]
                               for h in range(IDX_HEADS)], axis=0)
    iw_t = (iw_ref[...] * (IDX_HEADS ** -0.5 * IDX_DIM ** -0.5)).T

    def score_tile(kt, _):
        k0 = pl.multiple_of(kt * ts, ts)
        rel = jnp.maximum(_dot_nt(ik3_ref[0, pl.ds(k0, ts), :], iq_rows), 0.0)
        sc = rel[:, 0:tq] * iw_t[0:1, :]
        for h in range(1, IDX_HEADS):
            sc = sc + rel[:, h * tq:(h + 1) * tq] * iw_t[h:h + 1, :]
        sc_ref[pl.ds(k0, ts), :] = jnp.where(k0 + krow <= qpos, sc, -jnp.inf)
        return 0

    lax.fori_loop(0, n_kt, score_tile, 0)

    def count(pred):
        def tile(kt, c):
            k0 = pl.multiple_of(kt * ts, ts)
            tile_ref = sc_ref.at[pl.ds(k0, ts)]
            sums = [None] * FOLD_CHAINS
            for i in range(ts // SUBLANES):
                rows = slice(i * SUBLANES, (i + 1) * SUBLANES)
                hit = jnp.where(pred(tile_ref[rows, :], k0 + i * SUBLANES), 1.0, 0.0)
                j = i % FOLD_CHAINS
                sums[j] = hit if sums[j] is None else sums[j] + hit
            return c + _fold_rows(jnp.concatenate(sums, axis=0), jnp.add)
        c = lax.fori_loop(0, n_kt, tile, jnp.zeros((SUBLANES, tq), F32))
        return jnp.sum(c, axis=0, keepdims=True)

    def key_to_float(key):
        return pltpu.bitcast(jnp.where(key >= 0, key, key ^ jnp.int32(0x7FFFFFFF)), F32)

    def value_bit(i, carry):
        thr_key, n_at = carry
        cand = thr_key | jnp.left_shift(jnp.int32(1), 31 - i)
        cand_f = key_to_float(cand ^ jnp.int32(INT_MIN))
        n = count(lambda sc, k0: sc >= cand_f)
        keep = n >= topk
        return jnp.where(keep, cand, thr_key), jnp.where(keep, n, n_at)

    thr_key, n_at = lax.fori_loop(
        0, 32, value_bit, (jnp.zeros((1, tq), jnp.int32), jnp.zeros((1, tq), F32)))
    few = t0 + lax.broadcasted_iota(jnp.int32, (1, tq), 1) + 1 < topk
    thr = jnp.where(few, jnp.finfo(F32).min, key_to_float(thr_key ^ jnp.int32(INT_MIN)))
    tie_break = jnp.max(jnp.where((n_at > topk) & jnp.logical_not(few), 1.0, 0.0)) > 0.0

    def last_tie_position():
        need = topk - count(lambda sc, k0: sc > thr)

        def index_bit(i, last):
            cand = last | jnp.left_shift(jnp.int32(1), idx_bits - 1 - i)
            tied_below = lambda sc, r0: (sc == thr) & (r0 + sub_row < cand)
            return jnp.where(count(tied_below) < need, cand, last)

        return lax.fori_loop(0, idx_bits, index_bit, jnp.zeros((1, tq), jnp.int32))

    qs = qh_ref[0].reshape(nh * tq, HEAD_DIM)

    def attend(selected):
        def key_tile(kt, carry):
            m, l, acc = carry
            k0 = pl.multiple_of(kt * ts, ts)
            bias = jnp.where(selected(sc_ref[pl.ds(k0, ts), :], k0), 0.0, MASKED_SCORE)
            s = _dot_nt(k_ref[0, pl.ds(k0, ts), :], qs) + jnp.concatenate([bias] * nh, axis=1)
            m, alpha, p, l = _softmax_stats_update(s, m, l)
            acc = alpha * acc + _dot(vt_ref[0, :, pl.ds(k0, ts)], p.astype(BF16))
            return m, l, acc

        init = (jnp.full((1, nh * tq), MASKED_SCORE, F32), jnp.zeros((1, nh * tq), F32),
                jnp.zeros((HEAD_DIM, nh * tq), F32))
        _, l, acc = lax.fori_loop(0, n_kt, key_tile, init)
        return l, acc

    def attend_with_ties():
        last = last_tie_position()
        return attend(lambda sc, k0: (sc > thr) | ((sc == thr) & (k0 + krow <= last)))

    l, acc = lax.cond(tie_break, attend_with_ties, lambda: attend(lambda sc, k0: sc >= thr))
    o_t = acc / l
    o_ref[...] = jnp.concatenate([o_t[:, h * tq:(h + 1) * tq] for h in range(nh)], axis=0).T


def dsa_attend(cols, b_, l_, qh, iq3, k, v, ik3):
    tq = DSA_TQ
    nt = l_ // tq
    topk = min(DSA_TOPK_MAX, l_ // 4)
    idx_bits = int(np.log2(l_))
    assert 2 ** idx_bits == l_ and l_ % DSA_KEY_TILE == 0 and topk <= DSA_KEY_TILE
    seq = lambda w: pl.BlockSpec((1, l_, w), lambda b, i: (b, 0, 0))
    return pl.pallas_call(
        functools.partial(_dsa_attend_t_kernel, topk=topk, idx_bits=idx_bits),
        out_shape=jax.ShapeDtypeStruct((b_ * l_, GROUP_WIDTH), F32),
        grid=(b_, nt),
        in_specs=[
            pl.BlockSpec((1, GROUP_HEADS, tq, HEAD_DIM), lambda b, i: (b, 0, i, 0)),
            pl.BlockSpec((1, tq, IDX_HEADS * IDX_PACK), lambda b, i: (b, i, 0)),
            pl.BlockSpec((tq, LANES), lambda b, i: (b * nt + i, C_DSA_IW // LANES)),
            seq(HEAD_DIM), pl.BlockSpec((1, HEAD_DIM, l_), lambda b, i: (b, 0, 0)), seq(IDX_PACK),
        ],
        out_specs=pl.BlockSpec((tq, GROUP_WIDTH), lambda b, i: (b * nt + i, 0)),
        scratch_shapes=[pltpu.VMEM((l_, tq), F32)],
        compiler_params=pltpu.CompilerParams(
            dimension_semantics=("arbitrary", "arbitrary"),
            vmem_limit_bytes=V7X_VMEM_LIMIT_BYTES),
        name="dsa_attend",
    )(qh, iq3, cols, k, v, ik3)


def dsa_mixer_pallas(cols, b_, l_, rope, kv_gain, w_uk, w_uv, q_gain, k_gain, idxk_gain):
    rope_i = rope_tables(l_, IDX_DIM)
    k, v, ik3 = dsa_kv_prep(cols, b_, l_, rope_tables_kv_pair(rope), rope_tables_pad(rope_i, LANES),
                            kv_gain, w_uk, w_uv, k_gain, idxk_gain)
    qh, iq3 = dsa_q_prep(cols, b_, l_, jnp.tile(rope, (1, 1, GROUP_HEADS)),
                         jnp.tile(rope_i, (1, 1, IDX_HEADS)), q_gain)
    return dsa_attend(cols, b_, l_, qh, iq3, k, v, ik3)


_TN = (((0,), (0,)), ((), ()))


def _dot_tn(a, b):
    return lax.dot_general(a, b, _TN, preferred_element_type=F32)


def _split3(t):
    hi = t.astype(BF16)
    r = t - hi.astype(F32)
    mid = r.astype(BF16)
    lo = (r - mid.astype(F32)).astype(BF16)
    return hi, mid, lo


def _tri_cumsum(tri, t):
    hi, mid, lo = _split3(t)
    return _dot(tri, hi) + _dot(tri, mid) + _dot(tri, lo)


def _cumsum_tri_rows(t, tri_u):
    hi, mid, lo = _split3(t)
    return _dot(hi, tri_u) + _dot(mid, tri_u) + _dot(lo, tri_u)


def _head_rms(o, gain):
    return jnp.concatenate([_rms_rows(o[:, h * HEAD_DIM:(h + 1) * HEAD_DIM], gain)
                            for h in range(GROUP_HEADS)], axis=-1)


HG_SUB = 8
HG_BLOCK = 16
HG_CHUNKS_PER_STEP = 8
LOG2_E = 1.4426950408889634


def _hgrn2_kernel(q_ref, f_ref, i_ref, g_ref, lb_ref, gain_ref, tri_ref, ones_ref, bd_ref,
                  hm_ref, jm_ref, o_ref, st_ref, b_ref, kk_ref, v_ref):
    @pl.when(pl.program_id(1) == 0)
    def _():
        st_ref[...] = jnp.zeros_like(st_ref)

    for ci in range(HG_CHUNKS_PER_STEP):
        rows = pl.ds(ci * HG_CHUNK, HG_CHUNK)
        _hgrn2_chunk(q_ref.at[rows], f_ref.at[rows], i_ref.at[rows], g_ref.at[rows], lb_ref,
                     gain_ref, tri_ref, ones_ref, bd_ref, hm_ref, jm_ref, o_ref.at[rows], st_ref,
                     b_ref, kk_ref, v_ref)


def _hgrn2_chunk(q_ref, f_ref, i_ref, g_ref, lb_ref, gain_ref, tri_ref, ones_ref, bd_ref,
                 hm_ref, jm_ref, o_ref, st_ref, b_ref, kk_ref, v_ref):
    c = HG_CHUNK
    w = GROUP_WIDTH
    lb = lb_ref[...]
    q = q_ref[...]
    qs = q * jax.nn.sigmoid(q) * (HEAD_DIM ** -0.5)
    forget = lb + (1.0 - lb) * jax.nn.sigmoid(f_ref[...])
    kk = 1.0 - forget
    bcum = _tri_cumsum(tri_ref[...], jnp.log(forget))
    v = i_ref[...]
    b2 = bcum * LOG2_E
    b_ref[...] = b2
    kk_ref[...] = kk
    v_ref[...] = v

    out = _dot_nt((qs * jnp.exp(bcum)).astype(BF16), st_ref[...].astype(BF16))

    ones_bd = ones_ref[...]
    pieces = []
    for g in range(c // HG_SUB):
        r0 = g * HG_SUB
        nr = (r0 // HG_BLOCK + 1) * HG_BLOCK - r0
        qg = qs[r0:r0 + nr, :]
        bg = b2[r0:r0 + nr, :]
        trow = r0 + lax.broadcasted_iota(jnp.int32, (HG_SUB, w), 0)
        terms = []
        for j in range(HG_SUB):
            s = r0 + j
            d = qg * kk_ref[s:s + 1, :] * jnp.exp2(bg - b_ref[s:s + 1, :])
            if j > 0:
                head = jnp.where(trow >= s, d[:HG_SUB], 0.0)
                d = jnp.concatenate([head, d[HG_SUB:]], axis=0) if nr > HG_SUB else head
            terms.append(d.astype(BF16))
        red = _dot(jnp.concatenate(terms, axis=0), ones_bd)
        acc = red[0:nr] * v_ref[r0:r0 + 1, :]
        for j in range(1, HG_SUB):
            acc = acc + red[j * nr:(j + 1) * nr] * v_ref[r0 + j:r0 + j + 1, :]
        pieces.append((r0, acc))

    nb = c // HG_BLOCK
    hm = hm_ref[...]
    q_rows, k_rows, v_rows = [], [], []
    for j in range(nb - 1):
        blk = slice(j * HG_BLOCK, (j + 1) * HG_BLOCK)
        r_j = b_ref[(j + 1) * HG_BLOCK - 1:(j + 1) * HG_BLOCK, :]
        later = slice((j + 1) * HG_BLOCK, c)
        q_rows.append(qs[later] * jnp.exp2(b2[later] - r_j))
        k_blk = kk[blk] * jnp.exp2(r_j - b2[blk])
        k_rows.append(jnp.concatenate([k_blk] * GROUP_HEADS, axis=0) * hm)
        v_rows.append(jnp.concatenate([v[blk]] * GROUP_HEADS, axis=0) * hm)
    scores = _dot_nt(jnp.concatenate(q_rows, axis=0).astype(BF16),
                     jnp.concatenate(k_rows, axis=0).astype(BF16))
    scores = jnp.where(jm_ref[...] > 0.0, scores, 0.0)
    off = _dot(scores.astype(BF16), jnp.concatenate(v_rows, axis=0).astype(BF16))
    row0 = 0
    for j in range(nb - 1):
        n_later = c - (j + 1) * HG_BLOCK
        pieces.append(((j + 1) * HG_BLOCK, off[row0:row0 + n_later]))
        row0 += n_later

    for r0, piece in pieces:
        parts = [piece]
        if r0 > 0:
            parts.insert(0, jnp.zeros((r0, w), F32))
        if r0 + piece.shape[0] < c:
            parts.append(jnp.zeros((c - r0 - piece.shape[0], w), F32))
        out = out + (jnp.concatenate(parts, axis=0) if len(parts) > 1 else piece)

    b_last = bcum[c - 1:c, :]
    kt = (kk * jnp.exp(b_last - bcum)).astype(BF16)
    st_ref[...] = jnp.exp(b_last) * st_ref[...] + _dot_tn(v.astype(BF16), kt) * bd_ref[...]

    g_in = g_ref[...]
    o_ref[...] = _head_rms(out, gain_ref[...]) * (g_in * jax.nn.sigmoid(g_in))


def hgrn2_mixer_pallas(cols, b_, l_, lb, o_gain):
    c = HG_CHUNK
    rows = c * HG_CHUNKS_PER_STEP
    nt = l_ // rows
    w = GROUP_WIDTH
    head = np.arange(w) // HEAD_DIM
    same = (head[:, None] == head[None, :]).astype(np.float32)
    tri = np.tril(np.ones((c, c), np.float32))
    nb = c // HG_BLOCK
    row_head = np.arange(GROUP_HEADS * HG_BLOCK) // HG_BLOCK
    hm = (row_head[:, None] == head[None, :]).astype(np.float32)
    q_block = np.concatenate([np.full(c - (j + 1) * HG_BLOCK, j) for j in range(nb - 1)])
    k_block = np.arange((nb - 1) * GROUP_HEADS * HG_BLOCK) // (GROUP_HEADS * HG_BLOCK)
    jm = (q_block[:, None] == k_block[None, :]).astype(np.float32)
    col = lambda j: pl.BlockSpec((rows, w), lambda b, i: (b * nt + i, C_HG // w + j))
    const = lambda shape: pl.BlockSpec(shape, lambda b, i: (0,) * len(shape))
    return pl.pallas_call(
        _hgrn2_kernel,
        out_shape=jax.ShapeDtypeStruct((b_ * l_, w), F32),
        grid=(b_, nt),
        in_specs=[col(0), col(1), col(2), col(3), const((1, w)), const((1, HEAD_DIM)),
                  const((c, c)), const((w, w)), const((w, w)), const(hm.shape), const(jm.shape)],
        out_specs=pl.BlockSpec((rows, w), lambda b, i: (b * nt + i, 0)),
        scratch_shapes=[pltpu.VMEM((w, w), F32), pltpu.VMEM((c, w), F32),
                        pltpu.VMEM((c, w), F32), pltpu.VMEM((c, w), F32)],
        compiler_params=pltpu.CompilerParams(
            dimension_semantics=("arbitrary", "arbitrary"),
            vmem_limit_bytes=V7X_VMEM_LIMIT_BYTES),
        name="hgrn2",
    )(cols, cols, cols, cols, lb.reshape(1, w), o_gain.reshape(1, HEAD_DIM),
      jnp.asarray(tri, BF16), jnp.asarray(same, BF16), jnp.asarray(same, F32),
      jnp.asarray(hm), jnp.asarray(jm))


ML_TC = 256
ML_M_INIT = -1e30
ML_CHUNKS_PER_STEP = 2


def _mlstm_kernel(gate_ref, qk_ref, v_ref, og_ref, cw_ref, cb_ref, gb_ref, gain_ref, tril_ref,
                  triu_ref, o_ref, xprev_ref, cmat_ref, nvec_ref, m_ref):
    @pl.when(pl.program_id(1) == 0)
    def _():
        xprev_ref[...] = jnp.zeros_like(xprev_ref)
        cmat_ref[...] = jnp.zeros_like(cmat_ref)
        nvec_ref[...] = jnp.zeros_like(nvec_ref)
        m_ref[...] = jnp.full(m_ref.shape, ML_M_INIT, F32)

    for ci in range(ML_CHUNKS_PER_STEP):
        rows = pl.ds(ci * ML_TC, ML_TC)
        _mlstm_chunk(gate_ref.at[rows], qk_ref.at[rows], v_ref.at[rows], og_ref.at[rows], cw_ref,
                     cb_ref, gb_ref, gain_ref, tril_ref, triu_ref, o_ref.at[rows], xprev_ref,
                     cmat_ref, nvec_ref, m_ref)


def _mlstm_chunk(gate_ref, qk_ref, v_ref, og_ref, cw_ref, cb_ref, gb_ref, gain_ref, tril_ref,
                 triu_ref, o_ref, xprev_ref, cmat_ref, nvec_ref, m_ref):
    c = ML_TC
    nh = GROUP_HEADS
    w = GROUP_WIDTH

    x = qk_ref[...]
    prev = xprev_ref[...]
    row = lax.broadcasted_iota(jnp.int32, x.shape, 0)
    acc = x * cw_ref[CONV_WIDTH - 1:CONV_WIDTH, :] + cb_ref[...]
    for j in range(1, CONV_WIDTH):
        shifted = jnp.where(row < j, pltpu.roll(prev, j, 0), pltpu.roll(x, j, 0))
        acc = acc + shifted * cw_ref[CONV_WIDTH - 1 - j:CONV_WIDTH - j, :]
    xprev_ref[...] = x
    qk = acc * jax.nn.sigmoid(acc)
    q = qk[:, :w]
    k = qk[:, w:] * (HEAD_DIM ** -0.5)
    v = v_ref[...]

    pre = gate_ref[...] + gb_ref[...]
    lane = lax.broadcasted_iota(jnp.int32, pre.shape, 1)
    log_f = jnp.minimum(pre, 0.0) - jnp.log1p(jnp.exp(-jnp.abs(pre)))
    log_f = jnp.where((lane >= nh) & (lane < 2 * nh), log_f, 0.0)
    bcum_c = _tri_cumsum(tril_ref[...], log_f)
    bcum_r = _cumsum_tri_rows(log_f.T, triu_ref[...])
    pre_r = pre.T
    tri = (lax.broadcasted_iota(jnp.int32, (c, c), 0) >= lax.broadcasted_iota(jnp.int32, (c, c), 1))

    heads = [slice(h * HEAD_DIM, (h + 1) * HEAD_DIM) for h in range(nh)]
    qb = [q[:, sl].astype(BF16) for sl in heads]
    vb = [v[:, sl].astype(BF16) for sl in heads]
    qk_raw = [_dot_nt(qb[h], k[:, heads[h]].astype(BF16)) for h in range(nh)]
    q_state = [_dot(qb[h], cmat_ref[h].astype(BF16)) for h in range(nh)]

    s_all, m_all, w_all, upd = [], [], [], []
    for h in range(nh):
        bc = bcum_c[:, nh + h:nh + h + 1]
        li_c = pre[:, h:h + 1]
        a_r = pre_r[h:h + 1, :] - bcum_r[nh + h:nh + h + 1, :]
        m_prev = m_ref[h:h + 1, 0:1]
        log_d = jnp.where(tri, bc + a_r, -jnp.inf)
        inter = bc + m_prev
        m_t = jnp.maximum(inter, jnp.max(log_d, axis=-1, keepdims=True))
        s_all.append(qk_raw[h] * jnp.exp(log_d - m_t))
        m_all.append(m_t)
        w_all.append(jnp.exp(inter - m_t))
        b_last = bc[c - 1:c, :]
        log_w = b_last + (li_c - bc)
        m_new = jnp.maximum(b_last + m_prev, jnp.max(log_w, axis=0, keepdims=True))
        kw = k[:, heads[h]] * jnp.exp(log_w - m_new)
        upd.append((m_new, kw, jnp.exp(b_last + m_prev - m_new)))

    sv = [_dot(s_all[h].astype(BF16), vb[h]) for h in range(nh)]
    kv = [_dot_tn(upd[h][1].astype(BF16), vb[h]) for h in range(nh)]

    outs = []
    for h in range(nh):
        num = w_all[h] * q_state[h] + sv[h]
        den = (w_all[h] * jnp.sum(q[:, heads[h]] * nvec_ref[h], axis=-1, keepdims=True)
               + jnp.sum(s_all[h], axis=-1, keepdims=True))
        outs.append(num / jnp.maximum(jnp.abs(den), jnp.exp(-m_all[h])))
        m_new, kw, decay = upd[h]
        cmat_ref[h] = decay * cmat_ref[h] + kv[h]
        nvec_ref[h] = decay * nvec_ref[h] + jnp.sum(kw, axis=0, keepdims=True)
        m_ref[h:h + 1, :] = jnp.broadcast_to(m_new, (1, LANES))

    hh = _head_rms(jnp.concatenate(outs, axis=-1), gain_ref[...])
    o_ref[...] = hh * jax.nn.sigmoid(og_ref[...])


def mlstm_mixer_pallas(cols, b_, l_, conv_w, conv_b, i_bias, f_bias, o_gain):
    c = ML_TC
    rows = c * ML_CHUNKS_PER_STEP
    nt = l_ // rows
    w = GROUP_WIDTH
    gb = jnp.pad(jnp.concatenate([i_bias, f_bias]), (0, LANES - 2 * GROUP_HEADS)).reshape(1, LANES)
    tril = np.tril(np.ones((c, c), np.float32))
    const = lambda shape: pl.BlockSpec(shape, lambda b, i: (0,) * len(shape))
    blk = lambda width, off: pl.BlockSpec((rows, width), lambda b, i: (b * nt + i, off // width))
    return pl.pallas_call(
        _mlstm_kernel,
        out_shape=jax.ShapeDtypeStruct((b_ * l_, w), F32),
        grid=(b_, nt),
        in_specs=[blk(LANES, C_ML_GATE), blk(2 * w, C_ML_QK), blk(w, C_ML_V), blk(w, C_ML_OG),
                  const((CONV_WIDTH, 2 * w)), const((1, 2 * w)), const((1, LANES)),
                  const((1, HEAD_DIM)), const((c, c)), const((c, c))],
        out_specs=pl.BlockSpec((rows, w), lambda b, i: (b * nt + i, 0)),
        scratch_shapes=[pltpu.VMEM((c, 2 * w), F32),
                        pltpu.VMEM((GROUP_HEADS, HEAD_DIM, HEAD_DIM), F32),
                        pltpu.VMEM((GROUP_HEADS, 1, HEAD_DIM), F32),
                        pltpu.VMEM((8, LANES), F32)],
        compiler_params=pltpu.CompilerParams(
            dimension_semantics=("arbitrary", "arbitrary"),
            vmem_limit_bytes=V7X_VMEM_LIMIT_BYTES),
        name="mlstm",
    )(cols, cols, cols, cols, conv_w, conv_b.reshape(1, 2 * w), gb, o_gain.reshape(1, HEAD_DIM),
      jnp.asarray(tril, BF16), jnp.asarray(tril.T, BF16))


def kernel(x, mem, lb_param, norm_mix, w_in, w_out, hg_o_gain, dsa_kv_gain, dsa_w_uk, dsa_w_uv,
           dsa_q_gain, dsa_k_gain, dsa_idxk_gain, nsa_pos_k, nsa_pos_v, nsa_k_w1, nsa_k_w2,
           nsa_v_w1, nsa_v_w2, nsa_q_gain, nsa_k_gains, ml_conv_w, ml_conv_b, ml_i_bias,
           ml_f_bias, ml_o_gain, norm_xa, norm_mem, xa_wq, xa_wkv, xa_wo, xa_q_gain, xa_k_gain,
           norm_ffn, ffn_w13, ffn_w2):
    b_, l_, d = x.shape
    lb_all = jnp.cumsum(jax.nn.softmax(lb_param.astype(F32), axis=0), axis=0)
    lb_all = lb_all - lb_all[:1]
    x2d = x.reshape(b_ * l_, d)
    mem2d = mem.reshape(b_ * N_MEM, d)
    rope = rope_tables(l_)
    for l in range(DEPTH):
        cols = norm_matmul(x2d, norm_mix[l], pack_w_in_bf16(w_in[l]), tm=512)
        mixers = (
            hgrn2_mixer_pallas(cols, b_, l_, lb_all[l], hg_o_gain[l]),
            dsa_mixer_pallas(cols, b_, l_, rope, dsa_kv_gain[l], dsa_w_uk[l], dsa_w_uv[l],
                             dsa_q_gain[l], dsa_k_gain[l], dsa_idxk_gain[l]),
            nsa_mixer_pallas(cols, b_, l_, rope, nsa_pos_k[l], nsa_pos_v[l], nsa_k_w1[l],
                             nsa_k_w2[l], nsa_v_w1[l], nsa_v_w2[l], nsa_q_gain[l],
                             nsa_k_gains[l]),
            mlstm_mixer_pallas(cols, b_, l_, ml_conv_w[l], ml_conv_b[l], ml_i_bias[l],
                               ml_f_bias[l], ml_o_gain[l]),
        )
        k_mem, v_mem = mem_kv(mem2d, norm_mem[l], xa_wkv[l].astype(BF16), xa_k_gain[l])
        x2d = post_block(x2d, mixers, l_, w_out[l].astype(BF16), norm_xa[l],
                         xa_wq[l].astype(BF16), xa_q_gain[l], k_mem, v_mem,
                         xa_wo[l].astype(BF16), norm_ffn[l], ffn_w13[l].astype(BF16),
                         ffn_w2[l].astype(BF16), tm=512)
    return x2d.reshape(b_, l_, d)
```

```python
import functools

import jax
import jax.numpy as jnp
from jax import lax
import numpy as np
from jax.experimental import pallas as pl
from jax.experimental.pallas import tpu as pltpu

F32 = jnp.float32
BF16 = jnp.bfloat16

D_MODEL = 1024
DEPTH = 2
HEAD_DIM = 64
GROUP_HEADS = 4
GROUP_WIDTH = GROUP_HEADS * HEAD_DIM
ROPE_THETA = 500000.0
EPS = 1e-6
N_MEM = 256
XA_HEADS = 4
XA_WIDTH = XA_HEADS * HEAD_DIM
HG_CHUNK = 64
DSA_LATENT = 128
IDX_HEADS = 8
IDX_DIM = 32
DSA_TOPK_MAX = 256
CMP_BLOCK = 32
CMP_STRIDE = 16
SLC_BLOCK = 64
SLC_SHIFT = 6
SLC_TOPN = 16
WINDOW = 512
CONV_WIDTH = 4
D_FF = 2816

HG_SPLITS = (GROUP_WIDTH,) * 4
DSA_SPLITS = (GROUP_WIDTH, DSA_LATENT, IDX_HEADS * IDX_DIM, IDX_DIM, IDX_HEADS)
NSA_SPLITS = (GROUP_WIDTH,) + (HEAD_DIM,) * 6 + (3 * GROUP_HEADS,)
ML_SPLITS = (2 * GROUP_WIDTH, GROUP_WIDTH, GROUP_WIDTH, GROUP_HEADS, GROUP_HEADS)
GROUP_COLS = (sum(HG_SPLITS), sum(DSA_SPLITS), sum(NSA_SPLITS), sum(ML_SPLITS))
IN_COLS = sum(GROUP_COLS)

V7X_VMEM_LIMIT_BYTES = 56 * 1024 * 1024
LANES = 128
FF_CHUNK = 256


def _round_up(n, m):
    return -(-n // m) * m


def _rms_rows(t, g):
    return t * lax.rsqrt(jnp.mean(t * t, axis=-1, keepdims=True) + EPS) * g


def _const_spec(shape):
    return pl.BlockSpec(shape, lambda *_: (0,) * len(shape), pipeline_mode=pl.Buffered(1))


def _norm_matmul_kernel(x_ref, g_ref, w_ref, o_ref):
    h = _rms_rows(x_ref[...], g_ref[...]).astype(BF16)
    o_ref[...] = jnp.dot(h, w_ref[...], preferred_element_type=F32)


def norm_matmul(x2d, gain, w_bf16, tm):
    m, k = x2d.shape
    n = w_bf16.shape[1]
    return pl.pallas_call(
        _norm_matmul_kernel,
        out_shape=jax.ShapeDtypeStruct((m, n), F32),
        grid=(m // tm,),
        in_specs=[
            pl.BlockSpec((tm, k), lambda i: (i, 0)),
            _const_spec((1, k)),
            _const_spec((k, n)),
        ],
        out_specs=pl.BlockSpec((tm, n), lambda i: (i, 0)),
        compiler_params=pltpu.CompilerParams(
            dimension_semantics=("arbitrary",), vmem_limit_bytes=V7X_VMEM_LIMIT_BYTES),
        name="norm_matmul",
    )(x2d, gain.reshape(1, k), w_bf16)


def _mem_kv_kernel(m_ref, g_ref, w_ref, kg_ref, k_ref, v_ref):
    mn = _rms_rows(m_ref[...], g_ref[...]).astype(BF16)
    kv = jnp.dot(mn, w_ref[...], preferred_element_type=F32)
    kg = kg_ref[...]
    ks = []
    for h in range(XA_HEADS):
        kh = kv[:, h * HEAD_DIM:(h + 1) * HEAD_DIM]
        ks.append(_rms_rows(kh, kg))
    k_ref[...] = jnp.concatenate(ks, axis=-1).astype(BF16)
    v_ref[...] = kv[:, XA_WIDTH:].astype(BF16)


def mem_kv(mem2d, gain, wkv_bf16, k_gain):
    m, k = mem2d.shape
    return pl.pallas_call(
        _mem_kv_kernel,
        out_shape=(jax.ShapeDtypeStruct((m, XA_WIDTH), BF16),
                   jax.ShapeDtypeStruct((m, XA_WIDTH), BF16)),
        grid=(m // N_MEM,),
        in_specs=[
            pl.BlockSpec((N_MEM, k), lambda i: (i, 0)),
            _const_spec((1, k)),
            _const_spec((k, 2 * XA_WIDTH)),
            _const_spec((1, HEAD_DIM)),
        ],
        out_specs=(pl.BlockSpec((N_MEM, XA_WIDTH), lambda i: (i, 0)),
                   pl.BlockSpec((N_MEM, XA_WIDTH), lambda i: (i, 0))),
        compiler_params=pltpu.CompilerParams(
            dimension_semantics=("arbitrary",), vmem_limit_bytes=V7X_VMEM_LIMIT_BYTES),
        name="mem_kv",
    )(mem2d, gain.reshape(1, k), wkv_bf16, k_gain.reshape(1, HEAD_DIM))


def _post_kernel(x_ref, mhg_ref, mdsa_ref, mnsa_ref, mml_ref, wout_ref, gxa_ref, wq_ref, qg_ref,
                 k_ref, v_ref, wo_ref, gffn_ref, w13_ref, w2_ref, o_ref):
    x = x_ref[...]
    for g, mix_ref in enumerate((mhg_ref, mdsa_ref, mnsa_ref, mml_ref)):
        x = x + _dot(mix_ref[...].astype(BF16),
                     wout_ref[g * GROUP_WIDTH:(g + 1) * GROUP_WIDTH, :])
    h = _rms_rows(x, gxa_ref[...]).astype(BF16)
    q = _dot(h, wq_ref[...])
    qg = qg_ref[...] * (HEAD_DIM ** -0.5)
    k = k_ref[...]
    v = v_ref[...]
    heads = [slice(hd * HEAD_DIM, (hd + 1) * HEAD_DIM) for hd in range(XA_HEADS)]
    scores = [_dot_nt(_rms_rows(q[:, sl], qg).astype(BF16), k[:, sl]) for sl in heads]
    probs = []
    for s in scores:
        e = jnp.exp(s - jnp.max(s, axis=-1, keepdims=True))
        probs.append((e / jnp.sum(e, axis=-1, keepdims=True)).astype(BF16))
    o = jnp.concatenate([_dot(p, v[:, sl]) for p, sl in zip(probs, heads)], axis=-1)
    x = x + _dot(o.astype(BF16), wo_ref[...])
    h = _rms_rows(x, gffn_ref[...]).astype(BF16)

    def up(c):
        return (_dot(h, w13_ref[:, c * FF_CHUNK:(c + 1) * FF_CHUNK]),
                _dot(h, w13_ref[:, D_FF + c * FF_CHUNK:D_FF + (c + 1) * FF_CHUNK]))

    n_chunks = D_FF // FF_CHUNK
    acc = x
    a, b = up(0)
    for c in range(n_chunks):
        nxt = up(c + 1) if c + 1 < n_chunks else None
        act = (a * jax.nn.sigmoid(a) * b).astype(BF16)
        acc = acc + _dot(act, w2_ref[c * FF_CHUNK:(c + 1) * FF_CHUNK, :])
        if nxt is not None:
            a, b = nxt
    o_ref[...] = acc


def post_block(x2d, mixers, seq, wout, gxa, wq, qg, k_mem, v_mem, wo, gffn, w13, w2, tm):
    m, d = x2d.shape
    tiles_per_batch = seq // tm
    row = lambda i: (i, 0)
    mem_row = lambda i: (i // tiles_per_batch, 0)
    return pl.pallas_call(
        _post_kernel,
        out_shape=jax.ShapeDtypeStruct((m, d), F32),
        grid=(m // tm,),
        in_specs=[
            pl.BlockSpec((tm, d), row),
            *[pl.BlockSpec((tm, GROUP_WIDTH), row) for _ in mixers],
            _const_spec(wout.shape),
            _const_spec((1, d)),
            _const_spec(wq.shape),
            _const_spec((1, HEAD_DIM)),
            pl.BlockSpec((N_MEM, XA_WIDTH), mem_row),
            pl.BlockSpec((N_MEM, XA_WIDTH), mem_row),
            _const_spec(wo.shape),
            _const_spec((1, d)),
            _const_spec(w13.shape),
            _const_spec(w2.shape),
        ],
        out_specs=pl.BlockSpec((tm, d), row),
        compiler_params=pltpu.CompilerParams(
            dimension_semantics=("arbitrary",), vmem_limit_bytes=V7X_VMEM_LIMIT_BYTES),
        name="post_block",
    )(x2d, *mixers, wout, gxa.reshape(1, d), wq, qg.reshape(1, HEAD_DIM), k_mem, v_mem, wo,
      gffn.reshape(1, d), w13, w2)


C_HG = 0
C_DSA_Q = 1024
C_DSA_IQ = 1280
C_NSA_Q = 1536
C_ML_V = 1792
C_ML_QK = 2048
C_ML_OG = 2560
C_DSA_CKV = 2816
C_DSA_IK = 2944
C_DSA_IW = 3072
C_NSA_GATE = 3200
C_NSA_CMP = 3328
C_NSA_SLC = 3456
C_NSA_WIN = 3584
C_ML_GATE = 3712
IN_COLS_PACKED = 3840


def _packed_source_columns():
    dsa0 = GROUP_COLS[0]
    nsa0 = dsa0 + GROUP_COLS[1]
    ml0 = nsa0 + GROUP_COLS[2]
    segs = [(0, 1024, 1024), (dsa0, 256, 256), (dsa0 + 384, 256, 256), (nsa0, 256, 256),
            (ml0 + 512, 256, 256), (ml0, 512, 512), (ml0 + 768, 256, 256), (dsa0 + 256, 128, 128),
            (dsa0 + 640, 32, LANES), (dsa0 + 672, 8, LANES), (nsa0 + 640, 12, LANES),
            (nsa0 + 256, 128, 128), (nsa0 + 384, 128, 128), (nsa0 + 512, 128, 128),
            (ml0 + 1024, 8, LANES)]
    src = np.concatenate([np.concatenate([np.arange(a, a + n), np.full(width - n, -1)])
                          for a, n, width in segs]).astype(np.int32)
    assert src.shape == (IN_COLS_PACKED,)
    return src


PACK_TN = 256


def _pack_w_kernel(w_ref, src_ref, o_ref, wb_ref):
    @pl.when(pl.program_id(0) == 0)
    def _():
        wb_ref[...] = w_ref[...].astype(BF16)

    row = lax.broadcasted_iota(jnp.int32, (wb_ref.shape[1], PACK_TN), 0)
    sel = jnp.where(row == src_ref[...], 1.0, 0.0).astype(BF16)
    o_ref[...] = _dot(wb_ref[...], sel).astype(BF16)


def pack_w_in_bf16(w):
    d, n = w.shape
    n_pad = _round_up(n, LANES)
    w_pad = jnp.pad(w, ((0, 0), (0, n_pad - n)))
    src = jnp.asarray(_packed_source_columns()).reshape(1, IN_COLS_PACKED)
    return pl.pallas_call(
        _pack_w_kernel,
        out_shape=jax.ShapeDtypeStruct((d, IN_COLS_PACKED), BF16),
        grid=(IN_COLS_PACKED // PACK_TN,),
        in_specs=[_const_spec((d, n_pad)), pl.BlockSpec((1, PACK_TN), lambda j: (0, j))],
        out_specs=pl.BlockSpec((d, PACK_TN), lambda j: (0, j)),
        scratch_shapes=[pltpu.VMEM((d, n_pad), BF16)],
        compiler_params=pltpu.CompilerParams(
            dimension_semantics=("arbitrary",), vmem_limit_bytes=V7X_VMEM_LIMIT_BYTES),
        name="pack_w_in",
    )(w_pad, src)


def rope_tables(l_, d=HEAD_DIM):
    rd = d // 4
    half = rd // 2
    inv = ROPE_THETA ** (-jnp.arange(half, dtype=F32) * 2.0 / rd)
    ang = jnp.arange(l_).astype(F32)[:, None] * inv[None, :]
    cos, sin = lax.optimization_barrier((jnp.cos(ang), jnp.sin(ang)))
    zh = jnp.zeros((l_, half), F32)
    rest0 = jnp.zeros((l_, d - rd), F32)
    c = jnp.concatenate([cos, cos, rest0 + 1.0], axis=1)
    s1 = jnp.concatenate([-sin, zh, rest0], axis=1)
    s2 = jnp.concatenate([zh, sin, rest0], axis=1)
    return jnp.stack([c, s1, s2])


def rope_tables_pad(rt, width):
    n = width - rt.shape[-1]
    ident = jnp.stack([jnp.ones(rt.shape[1:2] + (n,), F32), jnp.zeros(rt.shape[1:2] + (n,), F32),
                       jnp.zeros(rt.shape[1:2] + (n,), F32)])
    return jnp.concatenate([rt, ident], axis=-1)


def rope_tables_kv_pair(rt):
    return rope_tables_pad(rt, 2 * rt.shape[-1])


def _apply_rope(t, rope_ref, half=HEAD_DIM // 8):
    w = t.shape[-1]
    return (t * rope_ref[0] + pltpu.roll(t, w - half, 1) * rope_ref[1]
            + pltpu.roll(t, half, 1) * rope_ref[2])


_NT = (((1,), (1,)), ((), ()))


def _dot_nt(a, b):
    return lax.dot_general(a, b, _NT, preferred_element_type=F32)


def _dot(a, b):
    return jnp.dot(a, b, preferred_element_type=F32)


NEG_BIG = -(2.0 ** 30)
NSA_KEY_TILE = 512
NSA_TQ = 256
NSA_SEL_TQ = 512
NSA_PREP_TK = 1024


def _nsa_kv_prep_kernel(ps_ref, pw_ref, rope_ref, gs_ref, gw_ref, kaug_ref, vs_ref, kw_ref, vw_ref):
    tk = ps_ref.shape[0]
    lane = lax.broadcasted_iota(jnp.int32, (tk, LANES), 1)
    is_k = lane < HEAD_DIM

    def norm_rope(p, g):
        ms = jnp.sum(jnp.where(is_k, p * p, 0.0), axis=-1, keepdims=True) * (1.0 / HEAD_DIM)
        y = jnp.where(is_k, p * lax.rsqrt(ms + EPS) * g, p)
        return _apply_rope(y, rope_ref)

    ys = norm_rope(ps_ref[...], gs_ref[...])
    yw = norm_rope(pw_ref[...], gw_ref[...])
    row = pl.program_id(1) * tk + lax.broadcasted_iota(jnp.int32, (tk, LANES), 0)
    ind = jnp.where(jnp.right_shift(row, SLC_SHIFT) == (lane - HEAD_DIM), 1.0, 0.0)
    kaug_ref[0] = jnp.where(is_k, ys, ind).astype(BF16)
    vs_ref[0] = ys.T[HEAD_DIM:, :].astype(BF16)
    kw_ref[0] = yw[:, :HEAD_DIM].astype(BF16)
    vw_ref[0] = yw.T[HEAD_DIM:, :].astype(BF16)


def nsa_kv_prep(cols, b_, l_, rope_pair, g_slc, g_win):
    tk = NSA_PREP_TK
    nt = l_ // tk
    ones = jnp.ones((HEAD_DIM,), F32)
    gs = jnp.concatenate([g_slc, ones]).reshape(1, LANES)
    gw = jnp.concatenate([g_win, ones]).reshape(1, LANES)
    kv = lambda w: jax.ShapeDtypeStruct((b_, l_, w), BF16)
    kv_t = jax.ShapeDtypeStruct((b_, HEAD_DIM, l_), BF16)
    out_blk = lambda w: pl.BlockSpec((1, tk, w), lambda b, i: (b, i, 0))
    out_t = pl.BlockSpec((1, HEAD_DIM, tk), lambda b, i: (b, 0, i))
    return pl.pallas_call(
        _nsa_kv_prep_kernel,
        out_shape=(kv(LANES), kv_t, kv(HEAD_DIM), kv_t),
        grid=(b_, nt),
        in_specs=[
            pl.BlockSpec((tk, LANES), lambda b, i: (b * nt + i, C_NSA_SLC // LANES)),
            pl.BlockSpec((tk, LANES), lambda b, i: (b * nt + i, C_NSA_WIN // LANES)),
            pl.BlockSpec((3, tk, LANES), lambda b, i: (0, i, 0)),
            pl.BlockSpec((1, LANES), lambda b, i: (0, 0)),
            pl.BlockSpec((1, LANES), lambda b, i: (0, 0)),
        ],
        out_specs=(out_blk(LANES), out_t, out_blk(HEAD_DIM), out_t),
        compiler_params=pltpu.CompilerParams(
            dimension_semantics=("arbitrary", "arbitrary"),
            vmem_limit_bytes=V7X_VMEM_LIMIT_BYTES),
        name="nsa_kv_prep",
    )(cols, cols, rope_pair, gs, gw)


def _nsa_compress_kernel(r_ref, pea_ref, peb_ref, w1a_ref, w1b_ref, w2k_ref, w2v_ref, kg_ref,
                         kc_ref, vc_ref):
    r = r_ref[0]
    n = r.shape[0]
    a = _dot((r + pea_ref[...]).astype(BF16), w1a_ref[...])
    bm = _dot((r + peb_ref[...]).astype(BF16), w1b_ref[...])
    row = lax.broadcasted_iota(jnp.int32, bm.shape, 0)
    bm_up = jnp.where(row < n - 1, pltpu.roll(bm, n - 1, 0), 0.0)
    h = jnp.maximum(a + bm_up, 0.0).astype(BF16)
    hid = w2k_ref.shape[0]
    ck = _dot(h[:, :hid], w2k_ref[...])
    cv = _dot(h[:, hid:], w2v_ref[...])
    kc_ref[0] = _rms_rows(ck, kg_ref[...]).astype(BF16)
    vc_ref[0] = cv.astype(BF16)


def nsa_compress(cols, b_, l_, pos_k, pos_v, k_w1, k_w2, v_w1, v_w2, k_gain):
    rows = l_ // CMP_STRIDE
    hid = k_w1.shape[1]
    pair = cols[:, C_NSA_CMP:C_NSA_CMP + LANES].reshape(b_, rows, CMP_STRIDE * LANES)

    def interleave_pe(lo):
        pe = jnp.concatenate([pos_k[lo:lo + CMP_STRIDE], pos_v[lo:lo + CMP_STRIDE]], axis=1)
        return pe.reshape(1, CMP_STRIDE * LANES)

    def interleave_w(lo):
        wk = k_w1[lo * HEAD_DIM:(lo + CMP_STRIDE) * HEAD_DIM].reshape(CMP_STRIDE, HEAD_DIM, hid)
        wv = v_w1[lo * HEAD_DIM:(lo + CMP_STRIDE) * HEAD_DIM].reshape(CMP_STRIDE, HEAD_DIM, hid)
        z = jnp.zeros_like(wk)
        top = jnp.concatenate([wk, z], axis=2)
        bot = jnp.concatenate([z, wv], axis=2)
        return jnp.concatenate([top, bot], axis=1).reshape(CMP_STRIDE * LANES, 2 * hid).astype(BF16)

    out = jax.ShapeDtypeStruct((b_, rows, HEAD_DIM), BF16)
    return pl.pallas_call(
        _nsa_compress_kernel,
        out_shape=(out, out),
        grid=(b_,),
        in_specs=[
            pl.BlockSpec((1, rows, CMP_STRIDE * LANES), lambda b: (b, 0, 0)),
            _const_spec((1, CMP_STRIDE * LANES)),
            _const_spec((1, CMP_STRIDE * LANES)),
            _const_spec((CMP_STRIDE * LANES, 2 * hid)),
            _const_spec((CMP_STRIDE * LANES, 2 * hid)),
            _const_spec((hid, HEAD_DIM)),
            _const_spec((hid, HEAD_DIM)),
            _const_spec((1, HEAD_DIM)),
        ],
        out_specs=(pl.BlockSpec((1, rows, HEAD_DIM), lambda b: (b, 0, 0)),
                   pl.BlockSpec((1, rows, HEAD_DIM), lambda b: (b, 0, 0))),
        compiler_params=pltpu.CompilerParams(
            dimension_semantics=("arbitrary",), vmem_limit_bytes=V7X_VMEM_LIMIT_BYTES),
        name="nsa_compress",
    )(pair, interleave_pe(0), interleave_pe(CMP_STRIDE), interleave_w(0),
      interleave_w(CMP_STRIDE), k_w2.astype(BF16), v_w2.astype(BF16),
      k_gain.reshape(1, HEAD_DIM))


def _masked_softmax(s, valid, axis):
    m = jnp.max(jnp.where(valid, s, -jnp.inf), axis=axis, keepdims=True)
    m = jnp.where(m == -jnp.inf, 0.0, m)
    e = jnp.where(valid, jnp.exp(s - m), 0.0)
    den = jnp.sum(e, axis=axis, keepdims=True)
    return e / jnp.where(den > 0, den, 1.0)


def _nsa_select_kernel(q_ref, kc_ref, vc_ref, ovt_ref, rope_ref, qg_ref, qaug_ref, ocmp_ref,
                       *, n_sel):
    tq = q_ref.shape[0]
    ncr = kc_ref.shape[1]
    n_slc = ovt_ref.shape[0]
    t0 = pl.program_id(1) * tq
    scale = HEAD_DIM ** -0.5
    q = q_ref[...]
    g = qg_ref[...]
    kc = kc_ref[0]
    vc = vc_ref[0]
    vis_t = (lax.broadcasted_iota(jnp.int32, (ncr, tq), 0) * CMP_STRIDE + (CMP_BLOCK - 1)
             <= t0 + lax.broadcasted_iota(jnp.int32, (ncr, tq), 1))
    qn_heads, o_heads = [], []
    psum_t = jnp.zeros((ncr, tq), F32)
    for h in range(GROUP_HEADS):
        qn = _rms_rows(q[:, h * HEAD_DIM:(h + 1) * HEAD_DIM], g)
        qn_heads.append(qn)
        p_t = _masked_softmax(_dot_nt(kc, (qn * scale).astype(BF16)), vis_t, 0)
        o_heads.append(_dot_tn(vc, p_t.astype(BF16)))
        psum_t = psum_t + p_t
    ocmp_ref[0] = jnp.concatenate(o_heads, axis=0)

    hi = psum_t.astype(BF16)
    lo = (psum_t - hi.astype(F32)).astype(BF16)
    imp = _dot(ovt_ref[...], hi) + _dot(ovt_ref[...], lo)
    blk = lax.broadcasted_iota(jnp.int32, (n_slc, tq), 0)
    cur = jnp.right_shift(t0 + lax.broadcasted_iota(jnp.int32, (n_slc, tq), 1), SLC_SHIFT)
    forced = (blk == 0) | (blk == cur) | (blk == cur - 1)
    imp = jnp.where(forced, jnp.inf, jnp.where(blk > cur, -jnp.inf, imp))
    n_grp = n_slc // SUBLANES
    groups = [imp[g * SUBLANES:(g + 1) * SUBLANES] for g in range(n_grp)]
    cnts = [jnp.zeros((SUBLANES, tq), F32) for _ in range(n_grp)]
    sub = lax.broadcasted_iota(jnp.int32, (SUBLANES, tq), 0)
    for m in range(n_slc):
        row = imp[m:m + 1, :]
        gm, rm = divmod(m, SUBLANES)
        for g in range(n_grp):
            if g < gm:
                beats = row > groups[g]
            elif g > gm:
                beats = row >= groups[g]
            else:
                beats = (row > groups[g]) | ((row == groups[g]) & (sub > rm))
            cnts[g] = cnts[g] + jnp.where(beats, 1.0, 0.0)
    cnt = jnp.concatenate(cnts, axis=0)
    mt = jnp.where(cnt < n_sel, 0.0, NEG_BIG)
    pad = jnp.zeros((HEAD_DIM - n_slc, tq), F32)
    mt = jnp.concatenate([mt, pad, mt, pad], axis=0) if n_slc < HEAD_DIM else jnp.concatenate(
        [mt, mt], axis=0)
    mt = mt.T

    qr = _apply_rope(jnp.concatenate(qn_heads, axis=-1), rope_ref) * scale
    lane = lax.broadcasted_iota(jnp.int32, (tq, LANES), 1)
    for j in range(GROUP_HEADS // 2):
        pair = qr[:, j * LANES:(j + 1) * LANES]
        swapped = pltpu.roll(pair, HEAD_DIM, 1)
        qaug_ref[0, :, (2 * j) * LANES:(2 * j + 1) * LANES] = jnp.where(
            lane < HEAD_DIM, pair, mt).astype(BF16)
        qaug_ref[0, :, (2 * j + 1) * LANES:(2 * j + 2) * LANES] = jnp.where(
            lane < HEAD_DIM, swapped, mt).astype(BF16)


def nsa_select(cols, b_, l_, k_cmp, v_cmp, rope_q, q_gain):
    tq = min(NSA_SEL_TQ, l_)
    nt = l_ // tq
    ncr = l_ // CMP_STRIDE
    n_slc = l_ // SLC_BLOCK
    n_sel = min(SLC_TOPN, n_slc)
    st_c = np.arange(ncr) * CMP_STRIDE
    st_s = np.arange(n_slc) * SLC_BLOCK
    ovt = ((st_c[None, :] < st_s[:, None] + SLC_BLOCK)
           & (st_c[None, :] + CMP_BLOCK > st_s[:, None])).astype(np.float32)
    return pl.pallas_call(
        functools.partial(_nsa_select_kernel, n_sel=n_sel),
        out_shape=(jax.ShapeDtypeStruct((b_, l_, GROUP_HEADS * LANES), BF16),
                   jax.ShapeDtypeStruct((b_, GROUP_WIDTH, l_), F32)),
        grid=(b_, nt),
        in_specs=[
            pl.BlockSpec((tq, GROUP_WIDTH), lambda b, i: (b * nt + i, C_NSA_Q // GROUP_WIDTH)),
            pl.BlockSpec((1, ncr, HEAD_DIM), lambda b, i: (b, 0, 0)),
            pl.BlockSpec((1, ncr, HEAD_DIM), lambda b, i: (b, 0, 0)),
            pl.BlockSpec((n_slc, ncr), lambda b, i: (0, 0)),
            pl.BlockSpec((3, tq, GROUP_WIDTH), lambda b, i: (0, i, 0)),
            pl.BlockSpec((1, HEAD_DIM), lambda b, i: (0, 0)),
        ],
        out_specs=(pl.BlockSpec((1, tq, GROUP_HEADS * LANES), lambda b, i: (b, i, 0)),
                   pl.BlockSpec((1, GROUP_WIDTH, tq), lambda b, i: (b, 0, i))),
        compiler_params=pltpu.CompilerParams(
            dimension_semantics=("arbitrary", "arbitrary"),
            vmem_limit_bytes=V7X_VMEM_LIMIT_BYTES),
        name="nsa_select",
    )(cols, k_cmp, v_cmp, jnp.asarray(ovt, BF16), rope_q, q_gain.reshape(1, HEAD_DIM))


def _softmax_stats_update(s, m, l):
    m_new = jnp.maximum(m, jnp.max(s, axis=0, keepdims=True))
    alpha = jnp.exp(m - m_new)
    p = jnp.exp(s - m_new)
    return m_new, alpha, p, alpha * l + jnp.sum(p, axis=0, keepdims=True)


def _nsa_attend_t_kernel(qaug_ref, kaug_ref, vst_ref, kw_ref, vwt_ref, ocmp_ref, gate_ref, o_ref):
    tq = qaug_ref.shape[1]
    nh = GROUP_HEADS
    ts = NSA_KEY_TILE
    t0 = pl.program_id(1) * tq
    qa = qaug_ref[0]
    qs = jnp.concatenate([qa[:, h * LANES:(h + 1) * LANES] for h in range(nh)], axis=0)
    n = nh * tq
    qpos_tile = t0 + lax.broadcasted_iota(jnp.int32, (1, tq), 1)
    qpos = jnp.concatenate([qpos_tile] * nh, axis=1)

    def key_tile(kt, carry):
        m, l, acc = carry
        k0 = pl.multiple_of(kt * ts, ts)
        s = _dot_nt(kaug_ref[0, pl.ds(k0, ts), :], qs)
        kpos = k0 + lax.broadcasted_iota(jnp.int32, (ts, n), 0)
        s = jnp.where(kpos <= qpos, s, NEG_BIG)
        m, alpha, p, l = _softmax_stats_update(s, m, l)
        acc = alpha * acc + _dot(vst_ref[0, :, pl.ds(k0, ts)], p.astype(BF16))
        return m, l, acc

    n_kt = (t0 + tq + ts - 1) // ts
    init = (jnp.full((1, n), -jnp.inf, F32), jnp.zeros((1, n), F32),
            jnp.zeros((HEAD_DIM, n), F32))
    _, l, acc = lax.fori_loop(0, n_kt, key_tile, init)
    o_slc = acc / l

    wlen = WINDOW + tq
    start = pl.multiple_of(jnp.maximum(t0 - WINDOW, 0), tq)
    sw = _dot_nt(kw_ref[0, pl.ds(start, wlen), :], qs[:, :HEAD_DIM])
    dist = qpos - (start + lax.broadcasted_iota(jnp.int32, (wlen, n), 0))
    sw = jnp.where((dist >= 0) & (dist < WINDOW), sw, -jnp.inf)
    e = jnp.exp(sw - jnp.max(sw, axis=0, keepdims=True))
    o_swa = _dot(vwt_ref[0, :, pl.ds(start, wlen)], e.astype(BF16)) / jnp.sum(
        e, axis=0, keepdims=True)

    g = jax.nn.sigmoid(gate_ref[...]).T
    oc = ocmp_ref[0]
    outs = []
    for h in range(nh):
        cols_h = slice(h * tq, (h + 1) * tq)
        outs.append(g[h:h + 1, :] * oc[h * HEAD_DIM:(h + 1) * HEAD_DIM, :]
                    + g[nh + h:nh + h + 1, :] * o_slc[:, cols_h]
                    + g[2 * nh + h:2 * nh + h + 1, :] * o_swa[:, cols_h])
    o_ref[...] = jnp.concatenate(outs, axis=0).T


def nsa_attend(cols, b_, l_, q_aug, k_aug, v_slc, k_win, v_win, o_cmp):
    tq = NSA_TQ
    nt = l_ // tq
    seq = lambda w: pl.BlockSpec((1, l_, w), lambda b, i: (b, 0, 0))
    seq_t = pl.BlockSpec((1, HEAD_DIM, l_), lambda b, i: (b, 0, 0))
    return pl.pallas_call(
        _nsa_attend_t_kernel,
        out_shape=jax.ShapeDtypeStruct((b_ * l_, GROUP_WIDTH), F32),
        grid=(b_, nt),
        in_specs=[
            pl.BlockSpec((1, tq, GROUP_HEADS * LANES), lambda b, i: (b, i, 0)),
            seq(LANES), seq_t, seq(HEAD_DIM), seq_t,
            pl.BlockSpec((1, GROUP_WIDTH, tq), lambda b, i: (b, 0, i)),
            pl.BlockSpec((tq, LANES), lambda b, i: (b * nt + i, C_NSA_GATE // LANES)),
        ],
        out_specs=pl.BlockSpec((tq, GROUP_WIDTH), lambda b, i: (b * nt + i, 0)),
        compiler_params=pltpu.CompilerParams(
            dimension_semantics=("arbitrary", "arbitrary"),
            vmem_limit_bytes=V7X_VMEM_LIMIT_BYTES),
        name="nsa_attend",
    )(q_aug, k_aug, v_slc, k_win, v_win, o_cmp, cols)


def nsa_mixer_pallas(cols, b_, l_, rope, pos_k, pos_v, k_w1, k_w2, v_w1, v_w2, q_gain, k_gains):
    k_aug, v_slc, k_win, v_win = nsa_kv_prep(cols, b_, l_, rope_tables_kv_pair(rope),
                                             k_gains[1], k_gains[2])
    k_cmp, v_cmp = nsa_compress(cols, b_, l_, pos_k, pos_v, k_w1, k_w2, v_w1, v_w2, k_gains[0])
    q_aug, o_cmp = nsa_select(cols, b_, l_, k_cmp, v_cmp, jnp.tile(rope, (1, 1, GROUP_HEADS)),
                              q_gain)
    return nsa_attend(cols, b_, l_, q_aug, k_aug, v_slc, k_win, v_win, o_cmp)


DSA_TQ = 512
DSA_KEY_TILE = 512
DSA_PREP_TK = 512
IDX_PACK = LANES
INT_MIN = -2 ** 31
MASKED_SCORE = -1e30


def _split_hi_lo(t):
    hi = t.astype(BF16)
    lo = (t - hi.astype(F32)).astype(BF16)
    return hi, lo


def _placement(rows, cols, pairs):
    p = np.zeros((rows, cols), np.float32)
    for r, c in pairs:
        p[r, c] = 1.0
    return jnp.asarray(p, BF16)


def _dsa_kv_prep_kernel(ckv_ref, ik_ref, rope_ref, ropei_ref, kvg_ref, wkv_ref, kg_ref, ikg_ref,
                        pkh_ref, pkl_ref, k_ref, vt_ref, ik3_ref):
    tk = ckv_ref.shape[0]
    lane = lax.broadcasted_iota(jnp.int32, (tk, LANES), 1)
    ckv = _rms_rows(ckv_ref[...], kvg_ref[...]).astype(BF16)
    kv = _dot(ckv, wkv_ref[...])
    is_k = lane < HEAD_DIM
    ms = jnp.sum(jnp.where(is_k, kv * kv, 0.0), axis=-1, keepdims=True) * (1.0 / HEAD_DIM)
    y = _apply_rope(jnp.where(is_k, kv * lax.rsqrt(ms + EPS) * kg_ref[...], kv), rope_ref)
    k_ref[0] = y[:, :HEAD_DIM].astype(BF16)
    vt_ref[0] = y.T[HEAD_DIM:, :].astype(BF16)
    ik = ik_ref[...]
    ms = jnp.sum(ik * ik, axis=-1, keepdims=True) * (1.0 / IDX_DIM)
    ikn = _apply_rope(ik * lax.rsqrt(ms + EPS) * ikg_ref[...], ropei_ref, IDX_DIM // 8)
    hi, lo = _split_hi_lo(ikn)
    ik3_ref[0] = (_dot(hi, pkh_ref[...]) + _dot(lo, pkl_ref[...])).astype(BF16)


def dsa_kv_prep(cols, b_, l_, rope_pair, rope_idx, kv_gain, w_uk, w_uv, k_gain, idxk_gain):
    tk = DSA_PREP_TK
    nt = l_ // tk
    ones = jnp.ones((HEAD_DIM,), F32)
    kg = jnp.concatenate([k_gain, ones]).reshape(1, LANES)
    ikg = jnp.pad(idxk_gain, (0, LANES - IDX_DIM)).reshape(1, LANES)
    wkv = jnp.concatenate([w_uk, w_uv], axis=1).astype(BF16)
    d = range(IDX_DIM)
    pkh = _placement(LANES, IDX_PACK, [(i, i) for i in d] + [(i, 2 * IDX_DIM + i) for i in d])
    pkl = _placement(LANES, IDX_PACK, [(i, IDX_DIM + i) for i in d])
    const = lambda shape: pl.BlockSpec(shape, lambda b, i: (0,) * len(shape))
    out = lambda w: pl.BlockSpec((1, tk, w), lambda b, i: (b, i, 0))
    return pl.pallas_call(
        _dsa_kv_prep_kernel,
        out_shape=(jax.ShapeDtypeStruct((b_, l_, HEAD_DIM), BF16),
                   jax.ShapeDtypeStruct((b_, HEAD_DIM, l_), BF16),
                   jax.ShapeDtypeStruct((b_, l_, IDX_PACK), BF16)),
        grid=(b_, nt),
        in_specs=[
            pl.BlockSpec((tk, LANES), lambda b, i: (b * nt + i, C_DSA_CKV // LANES)),
            pl.BlockSpec((tk, LANES), lambda b, i: (b * nt + i, C_DSA_IK // LANES)),
            pl.BlockSpec((3, tk, LANES), lambda b, i: (0, i, 0)),
            pl.BlockSpec((3, tk, LANES), lambda b, i: (0, i, 0)),
            const((1, LANES)), const((DSA_LATENT, LANES)), const((1, LANES)), const((1, LANES)),
            const((LANES, IDX_PACK)), const((LANES, IDX_PACK)),
        ],
        out_specs=(out(HEAD_DIM), pl.BlockSpec((1, HEAD_DIM, tk), lambda b, i: (b, 0, i)),
                   out(IDX_PACK)),
        compiler_params=pltpu.CompilerParams(
            dimension_semantics=("arbitrary", "arbitrary"),
            vmem_limit_bytes=V7X_VMEM_LIMIT_BYTES),
        name="dsa_kv_prep",
    )(cols, cols, rope_pair, rope_idx, kv_gain.reshape(1, DSA_LATENT), wkv, kg, ikg, pkh, pkl)


def _dsa_q_prep_kernel(q_ref, iq_ref, rope_ref, ropei_ref, qg_ref, pqh_ref, pql_ref,
                       qh_ref, iq3_ref):
    g = qg_ref[...]
    q = q_ref[...]
    qn = jnp.concatenate([_rms_rows(q[:, h * HEAD_DIM:(h + 1) * HEAD_DIM], g)
                          for h in range(GROUP_HEADS)], axis=-1)
    qr = _apply_rope(qn, rope_ref) * (HEAD_DIM ** -0.5)
    for h in range(GROUP_HEADS):
        qh_ref[0, h] = qr[:, h * HEAD_DIM:(h + 1) * HEAD_DIM].astype(BF16)
    hi, lo = _split_hi_lo(_apply_rope(iq_ref[...], ropei_ref, IDX_DIM // 8))
    iq3_ref[0] = (_dot(hi, pqh_ref[...]) + _dot(lo, pql_ref[...])).astype(BF16)


def dsa_q_prep(cols, b_, l_, rope_q, rope_iq, q_gain):
    tq = DSA_PREP_TK
    nt = l_ // tq
    w = IDX_HEADS * IDX_DIM
    hd = [(h, i) for h in range(IDX_HEADS) for i in range(IDX_DIM)]
    pqh = _placement(w, IDX_HEADS * IDX_PACK,
                     [(IDX_DIM * h + i, IDX_PACK * h + i) for h, i in hd]
                     + [(IDX_DIM * h + i, IDX_PACK * h + IDX_DIM + i) for h, i in hd])
    pql = _placement(w, IDX_HEADS * IDX_PACK,
                     [(IDX_DIM * h + i, IDX_PACK * h + 2 * IDX_DIM + i) for h, i in hd])
    const = lambda shape: pl.BlockSpec(shape, lambda b, i: (0,) * len(shape))
    return pl.pallas_call(
        _dsa_q_prep_kernel,
        out_shape=(jax.ShapeDtypeStruct((b_, GROUP_HEADS, l_, HEAD_DIM), BF16),
                   jax.ShapeDtypeStruct((b_, l_, IDX_HEADS * IDX_PACK), BF16)),
        grid=(b_, nt),
        in_specs=[
            pl.BlockSpec((tq, GROUP_WIDTH), lambda b, i: (b * nt + i, C_DSA_Q // GROUP_WIDTH)),
            pl.BlockSpec((tq, w), lambda b, i: (b * nt + i, C_DSA_IQ // w)),
            pl.BlockSpec((3, tq, GROUP_WIDTH), lambda b, i: (0, i, 0)),
            pl.BlockSpec((3, tq, w), lambda b, i: (0, i, 0)),
            const((1, HEAD_DIM)), const(pqh.shape), const(pql.shape),
        ],
        out_specs=(pl.BlockSpec((1, GROUP_HEADS, tq, HEAD_DIM), lambda b, i: (b, 0, i, 0)),
                   pl.BlockSpec((1, tq, IDX_HEADS * IDX_PACK), lambda b, i: (b, i, 0))),
        compiler_params=pltpu.CompilerParams(
            dimension_semantics=("arbitrary", "arbitrary"),
            vmem_limit_bytes=V7X_VMEM_LIMIT_BYTES),
        name="dsa_q_prep",
    )(cols, cols, rope_q, rope_iq, q_gain.reshape(1, HEAD_DIM), pqh, pql)


SUBLANES = 8
FOLD_CHAINS = 4


def _fold_rows(t, op, group=SUBLANES):
    parts = [t[i * group:(i + 1) * group] for i in range(t.shape[0] // group)]
    lanes = parts[:FOLD_CHAINS]
    for i, p in enumerate(parts[FOLD_CHAINS:]):
        lanes[i % FOLD_CHAINS] = op(lanes[i % FOLD_CHAINS], p)
    while len(lanes) > 1:
        lanes = [op(lanes[i], lanes[i + 1]) if i + 1 < len(lanes) else lanes[i]
                 for i in range(0, len(lanes), 2)]
    return lanes[0]


def _dsa_attend_t_kernel(qh_ref, iq3_ref, iw_ref, k_ref, vt_ref, ik3_ref, o_ref, sc_ref, *,
                         topk, idx_bits):
    tq = iq3_ref.shape[1]
    ts = DSA_KEY_TILE
    nh = GROUP_HEADS
    t0 = pl.program_id(1) * tq
    n_kt = (t0 + tq + ts - 1) // ts
    qpos = t0 + lax.broadcasted_iota(jnp.int32, (ts, tq), 1)
    krow = lax.broadcasted_iota(jnp.int32, (ts, tq), 0)
    sub_row = lax.broadcasted_iota(jnp.int32, (SUBLANES, tq), 0)

    iq3 = iq3_ref[0]
    iq_rows = jnp.concatenate([iq3[:, h * IDX_PACK:(h + 1) * IDX_PACK]
                               for h in range(IDX_HEADS)], axis=0)
    iw_t = (iw_ref[...] * (IDX_HEADS ** -0.5 * IDX_DIM ** -0.5)).T

    def score_tile(kt, _):
        k0 = pl.multiple_of(kt * ts, ts)
        rel = jnp.maximum(_dot_nt(ik3_ref[0, pl.ds(k0, ts), :], iq_rows), 0.0)
        sc = rel[:, 0:tq] * iw_t[0:1, :]
        for h in range(1, IDX_HEADS):
            sc = sc + rel[:, h * tq:(h + 1) * tq] * iw_t[h:h + 1, :]
        sc_ref[pl.ds(k0, ts), :] = jnp.where(k0 + krow <= qpos, sc, -jnp.inf)
        return 0

    lax.fori_loop(0, n_kt, score_tile, 0)

    def count(pred):
        def tile(kt, c):
            k0 = pl.multiple_of(kt * ts, ts)
            tile_ref = sc_ref.at[pl.ds(k0, ts)]
            sums = [None] * FOLD_CHAINS
            for i in range(ts // SUBLANES):
                rows = slice(i * SUBLANES, (i + 1) * SUBLANES)
                hit = jnp.where(pred(tile_ref[rows, :], k0 + i * SUBLANES), 1.0, 0.0)
                j = i % FOLD_CHAINS
                sums[j] = hit if sums[j] is None else sums[j] + hit
            return c + _fold_rows(jnp.concatenate(sums, axis=0), jnp.add)
        c = lax.fori_loop(0, n_kt, tile, jnp.zeros((SUBLANES, tq), F32))
        return jnp.sum(c, axis=0, keepdims=True)

    def key_to_float(key):
        return pltpu.bitcast(jnp.where(key >= 0, key, key ^ jnp.int32(0x7FFFFFFF)), F32)

    def value_bit(i, carry):
        thr_key, n_at = carry
        cand = thr_key | jnp.left_shift(jnp.int32(1), 31 - i)
        cand_f = key_to_float(cand ^ jnp.int32(INT_MIN))
        n = count(lambda sc, k0: sc >= cand_f)
        keep = n >= topk
        return jnp.where(keep, cand, thr_key), jnp.where(keep, n, n_at)

    thr_key, n_at = lax.fori_loop(
        0, 32, value_bit, (jnp.zeros((1, tq), jnp.int32), jnp.zeros((1, tq), F32)))
    few = t0 + lax.broadcasted_iota(jnp.int32, (1, tq), 1) + 1 < topk
    thr = jnp.where(few, jnp.finfo(F32).min, key_to_float(thr_key ^ jnp.int32(INT_MIN)))
    tie_break = jnp.max(jnp.where((n_at > topk) & jnp.logical_not(few), 1.0, 0.0)) > 0.0

    def last_tie_position():
        need = topk - count(lambda sc, k0: sc > thr)

        def index_bit(i, last):
            cand = last | jnp.left_shift(jnp.int32(1), idx_bits - 1 - i)
            tied_below = lambda sc, r0: (sc == thr) & (r0 + sub_row < cand)
            return jnp.where(count(tied_below) < need, cand, last)

        return lax.fori_loop(0, idx_bits, index_bit, jnp.zeros((1, tq), jnp.int32))

    qs = qh_ref[0].reshape(nh * tq, HEAD_DIM)

    def attend(selected):
        def key_tile(kt, carry):
            m, l, acc = carry
            k0 = pl.multiple_of(kt * ts, ts)
            bias = jnp.where(selected(sc_ref[pl.ds(k0, ts), :], k0), 0.0, MASKED_SCORE)
            s = _dot_nt(k_ref[0, pl.ds(k0, ts), :], qs) + jnp.concatenate([bias] * nh, axis=1)
            m, alpha, p, l = _softmax_stats_update(s, m, l)
            acc = alpha * acc + _dot(vt_ref[0, :, pl.ds(k0, ts)], p.astype(BF16))
            return m, l, acc

        init = (jnp.full((1, nh * tq), MASKED_SCORE, F32), jnp.zeros((1, nh * tq), F32),
                jnp.zeros((HEAD_DIM, nh * tq), F32))
        _, l, acc = lax.fori_loop(0, n_kt, key_tile, init)
        return l, acc

    def attend_with_ties():
        last = last_tie_position()
        return attend(lambda sc, k0: (sc > thr) | ((sc == thr) & (k0 + krow <= last)))

    l, acc = lax.cond(tie_break, attend_with_ties, lambda: attend(lambda sc, k0: sc >= thr))
    o_t = acc / l
    o_ref[...] = jnp.concatenate([o_t[:, h * tq:(h + 1) * tq] for h in range(nh)], axis=0).T


def dsa_attend(cols, b_, l_, qh, iq3, k, v, ik3):
    tq = DSA_TQ
    nt = l_ // tq
    topk = min(DSA_TOPK_MAX, l_ // 4)
    idx_bits = int(np.log2(l_))
    assert 2 ** idx_bits == l_ and l_ % DSA_KEY_TILE == 0 and topk <= DSA_KEY_TILE
    seq = lambda w: pl.BlockSpec((1, l_, w), lambda b, i: (b, 0, 0))
    return pl.pallas_call(
        functools.partial(_dsa_attend_t_kernel, topk=topk, idx_bits=idx_bits),
        out_shape=jax.ShapeDtypeStruct((b_ * l_, GROUP_WIDTH), F32),
        grid=(b_, nt),
        in_specs=[
            pl.BlockSpec((1, GROUP_HEADS, tq, HEAD_DIM), lambda b, i: (b, 0, i, 0)),
            pl.BlockSpec((1, tq, IDX_HEADS * IDX_PACK), lambda b, i: (b, i, 0)),
            pl.BlockSpec((tq, LANES), lambda b, i: (b * nt + i, C_DSA_IW // LANES)),
            seq(HEAD_DIM), pl.BlockSpec((1, HEAD_DIM, l_), lambda b, i: (b, 0, 0)), seq(IDX_PACK),
        ],
        out_specs=pl.BlockSpec((tq, GROUP_WIDTH), lambda b, i: (b * nt + i, 0)),
        scratch_shapes=[pltpu.VMEM((l_, tq), F32)],
        compiler_params=pltpu.CompilerParams(
            dimension_semantics=("arbitrary", "arbitrary"),
            vmem_limit_bytes=V7X_VMEM_LIMIT_BYTES),
        name="dsa_attend",
    )(qh, iq3, cols, k, v, ik3)


def dsa_mixer_pallas(cols, b_, l_, rope, kv_gain, w_uk, w_uv, q_gain, k_gain, idxk_gain):
    rope_i = rope_tables(l_, IDX_DIM)
    k, v, ik3 = dsa_kv_prep(cols, b_, l_, rope_tables_kv_pair(rope), rope_tables_pad(rope_i, LANES),
                            kv_gain, w_uk, w_uv, k_gain, idxk_gain)
    qh, iq3 = dsa_q_prep(cols, b_, l_, jnp.tile(rope, (1, 1, GROUP_HEADS)),
                         jnp.tile(rope_i, (1, 1, IDX_HEADS)), q_gain)
    return dsa_attend(cols, b_, l_, qh, iq3, k, v, ik3)


_TN = (((0,), (0,)), ((), ()))


def _dot_tn(a, b):
    return lax.dot_general(a, b, _TN, preferred_element_type=F32)


def _split3(t):
    hi = t.astype(BF16)
    r = t - hi.astype(F32)
    mid = r.astype(BF16)
    lo = (r - mid.astype(F32)).astype(BF16)
    return hi, mid, lo


def _tri_cumsum(tri, t):
    hi, mid, lo = _split3(t)
    return _dot(tri, hi) + _dot(tri, mid) + _dot(tri, lo)


def _cumsum_tri_rows(t, tri_u):
    hi, mid, lo = _split3(t)
    return _dot(hi, tri_u) + _dot(mid, tri_u) + _dot(lo, tri_u)


def _head_rms(o, gain):
    return jnp.concatenate([_rms_rows(o[:, h * HEAD_DIM:(h + 1) * HEAD_DIM], gain)
                            for h in range(GROUP_HEADS)], axis=-1)


HG_SUB = 8
HG_BLOCK = 16
HG_CHUNKS_PER_STEP = 16
LOG2_E = 1.4426950408889634


def _hgrn2_kernel(q_ref, f_ref, i_ref, g_ref, lb_ref, gain_ref, tri_ref, ones_ref, bd_ref,
                  hm_ref, jm_ref, o_ref, st_ref, b_ref, kk_ref, v_ref):
    @pl.when(pl.program_id(1) == 0)
    def _():
        st_ref[...] = jnp.zeros_like(st_ref)

    for ci in range(HG_CHUNKS_PER_STEP):
        rows = pl.ds(ci * HG_CHUNK, HG_CHUNK)
        _hgrn2_chunk(q_ref.at[rows], f_ref.at[rows], i_ref.at[rows], g_ref.at[rows], lb_ref,
                     gain_ref, tri_ref, ones_ref, bd_ref, hm_ref, jm_ref, o_ref.at[rows], st_ref,
                     b_ref, kk_ref, v_ref)


def _hgrn2_chunk(q_ref, f_ref, i_ref, g_ref, lb_ref, gain_ref, tri_ref, ones_ref, bd_ref,
                 hm_ref, jm_ref, o_ref, st_ref, b_ref, kk_ref, v_ref):
    c = HG_CHUNK
    w = GROUP_WIDTH
    lb = lb_ref[...]
    q = q_ref[...]
    qs = q * jax.nn.sigmoid(q) * (HEAD_DIM ** -0.5)
    forget = lb + (1.0 - lb) * jax.nn.sigmoid(f_ref[...])
    kk = 1.0 - forget
    bcum = _tri_cumsum(tri_ref[...], jnp.log(forget))
    v = i_ref[...]
    b2 = bcum * LOG2_E
    b_ref[...] = b2
    kk_ref[...] = kk
    v_ref[...] = v

    out = _dot_nt((qs * jnp.exp(bcum)).astype(BF16), st_ref[...].astype(BF16))

    ones_bd = ones_ref[...]
    pieces = []
    for g in range(c // HG_SUB):
        r0 = g * HG_SUB
        nr = (r0 // HG_BLOCK + 1) * HG_BLOCK - r0
        qg = qs[r0:r0 + nr, :]
        bg = b2[r0:r0 + nr, :]
        trow = r0 + lax.broadcasted_iota(jnp.int32, (HG_SUB, w), 0)
        terms = []
        for j in range(HG_SUB):
            s = r0 + j
            d = qg * kk_ref[s:s + 1, :] * jnp.exp2(bg - b_ref[s:s + 1, :])
            if j > 0:
                head = jnp.where(trow >= s, d[:HG_SUB], 0.0)
                d = jnp.concatenate([head, d[HG_SUB:]], axis=0) if nr > HG_SUB else head
            terms.append(d.astype(BF16))
        red = _dot(jnp.concatenate(terms, axis=0), ones_bd)
        acc = red[0:nr] * v_ref[r0:r0 + 1, :]
        for j in range(1, HG_SUB):
            acc = acc + red[j * nr:(j + 1) * nr] * v_ref[r0 + j:r0 + j + 1, :]
        pieces.append((r0, acc))

    nb = c // HG_BLOCK
    hm = hm_ref[...]
    q_rows, k_rows, v_rows = [], [], []
    for j in range(nb - 1):
        blk = slice(j * HG_BLOCK, (j + 1) * HG_BLOCK)
        r_j = b_ref[(j + 1) * HG_BLOCK - 1:(j + 1) * HG_BLOCK, :]
        later = slice((j + 1) * HG_BLOCK, c)
        q_rows.append(qs[later] * jnp.exp2(b2[later] - r_j))
        k_blk = kk[blk] * jnp.exp2(r_j - b2[blk])
        k_rows.append(jnp.concatenate([k_blk] * GROUP_HEADS, axis=0) * hm)
        v_rows.append(jnp.concatenate([v[blk]] * GROUP_HEADS, axis=0) * hm)
    scores = _dot_nt(jnp.concatenate(q_rows, axis=0).astype(BF16),
                     jnp.concatenate(k_rows, axis=0).astype(BF16))
    scores = jnp.where(jm_ref[...] > 0.0, scores, 0.0)
    off = _dot(scores.astype(BF16), jnp.concatenate(v_rows, axis=0).astype(BF16))
    row0 = 0
    for j in range(nb - 1):
        n_later = c - (j + 1) * HG_BLOCK
        pieces.append(((j + 1) * HG_BLOCK, off[row0:row0 + n_later]))
        row0 += n_later

    for r0, piece in pieces:
        parts = [piece]
        if r0 > 0:
            parts.insert(0, jnp.zeros((r0, w), F32))
        if r0 + piece.shape[0] < c:
            parts.append(jnp.zeros((c - r0 - piece.shape[0], w), F32))
        out = out + (jnp.concatenate(parts, axis=0) if len(parts) > 1 else piece)

    b_last = bcum[c - 1:c, :]
    kt = (kk * jnp.exp(b_last - bcum)).astype(BF16)
    st_ref[...] = jnp.exp(b_last) * st_ref[...] + _dot_tn(v.astype(BF16), kt) * bd_ref[...]

    g_in = g_ref[...]
    o_ref[...] = _head_rms(out, gain_ref[...]) * (g_in * jax.nn.sigmoid(g_in))


def hgrn2_mixer_pallas(cols, b_, l_, lb, o_gain):
    c = HG_CHUNK
    rows = c * HG_CHUNKS_PER_STEP
    nt = l_ // rows
    w = GROUP_WIDTH
    head = np.arange(w) // HEAD_DIM
    same = (head[:, None] == head[None, :]).astype(np.float32)
    tri = np.tril(np.ones((c, c), np.float32))
    nb = c // HG_BLOCK
    row_head = np.arange(GROUP_HEADS * HG_BLOCK) // HG_BLOCK
    hm = (row_head[:, None] == head[None, :]).astype(np.float32)
    q_block = np.concatenate([np.full(c - (j + 1) * HG_BLOCK, j) for j in range(nb - 1)])
    k_block = np.arange((nb - 1) * GROUP_HEADS * HG_BLOCK) // (GROUP_HEADS * HG_BLOCK)
    jm = (q_block[:, None] == k_block[None, :]).astype(np.float32)
    col = lambda j: pl.BlockSpec((rows, w), lambda b, i: (b * nt + i, C_HG // w + j))
    const = lambda shape: pl.BlockSpec(shape, lambda b, i: (0,) * len(shape))
    return pl.pallas_call(
        _hgrn2_kernel,
        out_shape=jax.ShapeDtypeStruct((b_ * l_, w), F32),
        grid=(b_, nt),
        in_specs=[col(0), col(1), col(2), col(3), const((1, w)), const((1, HEAD_DIM)),
                  const((c, c)), const((w, w)), const((w, w)), const(hm.shape), const(jm.shape)],
        out_specs=pl.BlockSpec((rows, w), lambda b, i: (b * nt + i, 0)),
        scratch_shapes=[pltpu.VMEM((w, w), F32), pltpu.VMEM((c, w), F32),
                        pltpu.VMEM((c, w), F32), pltpu.VMEM((c, w), F32)],
        compiler_params=pltpu.CompilerParams(
            dimension_semantics=("arbitrary", "arbitrary"),
            vmem_limit_bytes=V7X_VMEM_LIMIT_BYTES),
        name="hgrn2",
    )(cols, cols, cols, cols, lb.reshape(1, w), o_gain.reshape(1, HEAD_DIM),
      jnp.asarray(tri, BF16), jnp.asarray(same, BF16), jnp.asarray(same, F32),
      jnp.asarray(hm), jnp.asarray(jm))


ML_TC = 256
ML_M_INIT = -1e30
ML_CHUNKS_PER_STEP = 4


def _mlstm_kernel(gate_ref, qk_ref, v_ref, og_ref, cw_ref, cb_ref, gb_ref, gain_ref, tril_ref,
                  triu_ref, o_ref, xprev_ref, cmat_ref, nvec_ref, m_ref):
    @pl.when(pl.program_id(1) == 0)
    def _():
        xprev_ref[...] = jnp.zeros_like(xprev_ref)
        cmat_ref[...] = jnp.zeros_like(cmat_ref)
        nvec_ref[...] = jnp.zeros_like(nvec_ref)
        m_ref[...] = jnp.full(m_ref.shape, ML_M_INIT, F32)

    for ci in range(ML_CHUNKS_PER_STEP):
        rows = pl.ds(ci * ML_TC, ML_TC)
        _mlstm_chunk(gate_ref.at[rows], qk_ref.at[rows], v_ref.at[rows], og_ref.at[rows], cw_ref,
                     cb_ref, gb_ref, gain_ref, tril_ref, triu_ref, o_ref.at[rows], xprev_ref,
                     cmat_ref, nvec_ref, m_ref)


def _mlstm_chunk(gate_ref, qk_ref, v_ref, og_ref, cw_ref, cb_ref, gb_ref, gain_ref, tril_ref,
                 triu_ref, o_ref, xprev_ref, cmat_ref, nvec_ref, m_ref):
    c = ML_TC
    nh = GROUP_HEADS
    w = GROUP_WIDTH

    x = qk_ref[...]
    prev = xprev_ref[...]
    row = lax.broadcasted_iota(jnp.int32, x.shape, 0)
    acc = x * cw_ref[CONV_WIDTH - 1:CONV_WIDTH, :] + cb_ref[...]
    for j in range(1, CONV_WIDTH):
        shifted = jnp.where(row < j, pltpu.roll(prev, j, 0), pltpu.roll(x, j, 0))
        acc = acc + shifted * cw_ref[CONV_WIDTH - 1 - j:CONV_WIDTH - j, :]
    xprev_ref[...] = x
    qk = acc * jax.nn.sigmoid(acc)
    q = qk[:, :w]
    k = qk[:, w:] * (HEAD_DIM ** -0.5)
    v = v_ref[...]

    pre = gate_ref[...] + gb_ref[...]
    lane = lax.broadcasted_iota(jnp.int32, pre.shape, 1)
    log_f = jnp.minimum(pre, 0.0) - jnp.log1p(jnp.exp(-jnp.abs(pre)))
    log_f = jnp.where((lane >= nh) & (lane < 2 * nh), log_f, 0.0)
    bcum_c = _tri_cumsum(tril_ref[...], log_f)
    bcum_r = _cumsum_tri_rows(log_f.T, triu_ref[...])
    pre_r = pre.T
    tri = (lax.broadcasted_iota(jnp.int32, (c, c), 0) >= lax.broadcasted_iota(jnp.int32, (c, c), 1))

    heads = [slice(h * HEAD_DIM, (h + 1) * HEAD_DIM) for h in range(nh)]
    qb = [q[:, sl].astype(BF16) for sl in heads]
    vb = [v[:, sl].astype(BF16) for sl in heads]
    qk_raw = [_dot_nt(qb[h], k[:, heads[h]].astype(BF16)) for h in range(nh)]
    q_state = [_dot(qb[h], cmat_ref[h].astype(BF16)) for h in range(nh)]

    s_all, m_all, w_all, upd = [], [], [], []
    for h in range(nh):
        bc = bcum_c[:, nh + h:nh + h + 1]
        li_c = pre[:, h:h + 1]
        a_r = pre_r[h:h + 1, :] - bcum_r[nh + h:nh + h + 1, :]
        m_prev = m_ref[h:h + 1, 0:1]
        log_d = jnp.where(tri, bc + a_r, -jnp.inf)
        inter = bc + m_prev
        m_t = jnp.maximum(inter, jnp.max(log_d, axis=-1, keepdims=True))
        s_all.append(qk_raw[h] * jnp.exp(log_d - m_t))
        m_all.append(m_t)
        w_all.append(jnp.exp(inter - m_t))
        b_last = bc[c - 1:c, :]
        log_w = b_last + (li_c - bc)
        m_new = jnp.maximum(b_last + m_prev, jnp.max(log_w, axis=0, keepdims=True))
        kw = k[:, heads[h]] * jnp.exp(log_w - m_new)
        upd.append((m_new, kw, jnp.exp(b_last + m_prev - m_new)))

    sv = [_dot(s_all[h].astype(BF16), vb[h]) for h in range(nh)]
    kv = [_dot_tn(upd[h][1].astype(BF16), vb[h]) for h in range(nh)]

    outs = []
    for h in range(nh):
        num = w_all[h] * q_state[h] + sv[h]
        den = (w_all[h] * jnp.sum(q[:, heads[h]] * nvec_ref[h], axis=-1, keepdims=True)
               + jnp.sum(s_all[h], axis=-1, keepdims=True))
        outs.append(num / jnp.maximum(jnp.abs(den), jnp.exp(-m_all[h])))
        m_new, kw, decay = upd[h]
        cmat_ref[h] = decay * cmat_ref[h] + kv[h]
        nvec_ref[h] = decay * nvec_ref[h] + jnp.sum(kw, axis=0, keepdims=True)
        m_ref[h:h + 1, :] = jnp.broadcast_to(m_new, (1, LANES))

    hh = _head_rms(jnp.concatenate(outs, axis=-1), gain_ref[...])
    o_ref[...] = hh * jax.nn.sigmoid(og_ref[...])


def mlstm_mixer_pallas(cols, b_, l_, conv_w, conv_b, i_bias, f_bias, o_gain):
    c = ML_TC
    rows = c * ML_CHUNKS_PER_STEP
    nt = l_ // rows
    w = GROUP_WIDTH
    gb = jnp.pad(jnp.concatenate([i_bias, f_bias]), (0, LANES - 2 * GROUP_HEADS)).reshape(1, LANES)
    tril = np.tril(np.ones((c, c), np.float32))
    const = lambda shape: pl.BlockSpec(shape, lambda b, i: (0,) * len(shape))
    blk = lambda width, off: pl.BlockSpec((rows, width), lambda b, i: (b * nt + i, off // width))
    return pl.pallas_call(
        _mlstm_kernel,
        out_shape=jax.ShapeDtypeStruct((b_ * l_, w), F32),
        grid=(b_, nt),
        in_specs=[blk(LANES, C_ML_GATE), blk(2 * w, C_ML_QK), blk(w, C_ML_V), blk(w, C_ML_OG),
                  const((CONV_WIDTH, 2 * w)), const((1, 2 * w)), const((1, LANES)),
                  const((1, HEAD_DIM)), const((c, c)), const((c, c))],
        out_specs=pl.BlockSpec((rows, w), lambda b, i: (b * nt + i, 0)),
        scratch_shapes=[pltpu.VMEM((c, 2 * w), F32),
                        pltpu.VMEM((GROUP_HEADS, HEAD_DIM, HEAD_DIM), F32),
                        pltpu.VMEM((GROUP_HEADS, 1, HEAD_DIM), F32),
                        pltpu.VMEM((8, LANES), F32)],
        compiler_params=pltpu.CompilerParams(
            dimension_semantics=("arbitrary", "arbitrary"),
            vmem_limit_bytes=V7X_VMEM_LIMIT_BYTES),
        name="mlstm",
    )(cols, cols, cols, cols, conv_w, conv_b.reshape(1, 2 * w), gb, o_gain.reshape(1, HEAD_DIM),
      jnp.asarray(tril, BF16), jnp.asarray(tril.T, BF16))


def kernel(x, mem, lb_param, norm_mix, w_in, w_out, hg_o_gain, dsa_kv_gain, dsa_w_uk, dsa_w_uv,
           dsa_q_gain, dsa_k_gain, dsa_idxk_gain, nsa_pos_k, nsa_pos_v, nsa_k_w1, nsa_k_w2,
           nsa_v_w1, nsa_v_w2, nsa_q_gain, nsa_k_gains, ml_conv_w, ml_conv_b, ml_i_bias,
           ml_f_bias, ml_o_gain, norm_xa, norm_mem, xa_wq, xa_wkv, xa_wo, xa_q_gain, xa_k_gain,
           norm_ffn, ffn_w13, ffn_w2):
    b_, l_, d = x.shape
    lb_all = jnp.cumsum(jax.nn.softmax(lb_param.astype(F32), axis=0), axis=0)
    lb_all = lb_all - lb_all[:1]
    x2d = x.reshape(b_ * l_, d)
    mem2d = mem.reshape(b_ * N_MEM, d)
    rope = rope_tables(l_)
    for l in range(DEPTH):
        cols = norm_matmul(x2d, norm_mix[l], pack_w_in_bf16(w_in[l]), tm=512)
        mixers = (
            hgrn2_mixer_pallas(cols, b_, l_, lb_all[l], hg_o_gain[l]),
            dsa_mixer_pallas(cols, b_, l_, rope, dsa_kv_gain[l], dsa_w_uk[l], dsa_w_uv[l],
                             dsa_q_gain[l], dsa_k_gain[l], dsa_idxk_gain[l]),
            nsa_mixer_pallas(cols, b_, l_, rope, nsa_pos_k[l], nsa_pos_v[l], nsa_k_w1[l],
                             nsa_k_w2[l], nsa_v_w1[l], nsa_v_w2[l], nsa_q_gain[l],
                             nsa_k_gains[l]),
            mlstm_mixer_pallas(cols, b_, l_, ml_conv_w[l], ml_conv_b[l], ml_i_bias[l],
                               ml_f_bias[l], ml_o_gain[l]),
        )
        k_mem, v_mem = mem_kv(mem2d, norm_mem[l], xa_wkv[l].astype(BF16), xa_k_gain[l])
        x2d = post_block(x2d, mixers, l_, w_out[l].astype(BF16), norm_xa[l],
                         xa_wq[l].astype(BF16), xa_q_gain[l], k_mem, v_mem,
                         xa_wo[l].astype(BF16), norm_ffn[l], ffn_w13[l].astype(BF16),
                         ffn_w2[l].astype(BF16), tm=512)
    return x2d.reshape(b_, l_, d)
```
